```python
import math
import jax, jax.numpy as jnp
from jax import lax
import numpy as np

D_MODEL = 1024
BATCH = 8
SEQ = 4096
DEPTH = 2

D_RNN = 1024
RG_BLOCKS = 4
RG_BW = D_RNN // RG_BLOCKS
CONV_W = 4
RG_C = 8.0
N_HEADS = 16
N_KV = 4
HEAD_DIM = 64
HPG = N_HEADS // N_KV
CMP_LEN = 32
CMP_STRIDE = 16
CMP_HID = 256
SEL_LEN = 64
SEL_TOPN = 16
WINDOW = 512
Q_BLK = 64
D_FF = -(-8 * D_MODEL // (3 * 256)) * 256
EPS = 1e-6
FORCE_SCORE = 1e6

Q_W = N_HEADS * HEAD_DIM
KV_W = N_KV * HEAD_DIM
IN_SPLITS = (D_RNN, D_RNN, Q_W, KV_W, KV_W, KV_W, KV_W, KV_W, KV_W, 3 * N_HEADS, 2 * D_MODEL)
D_IN = sum(IN_SPLITS)

kernel_name = "hawk_nsa_parallel_gated_hybrid"


def rmsnorm(x, g):
    xf = x.astype(jnp.float32)
    y = xf * lax.rsqrt(jnp.mean(xf * xf, axis=-1, keepdims=True) + EPS)
    return (y * g.astype(jnp.float32)).astype(x.dtype)


def masked_softmax(s, mask):
    s = jnp.where(mask, s.astype(jnp.float32), -1e30)
    m = jnp.max(s, axis=-1, keepdims=True)
    e = jnp.where(mask, jnp.exp(s - m), 0.0)
    return e / jnp.maximum(jnp.sum(e, axis=-1, keepdims=True), 1e-30)


def causal_conv(x, w, b):
    S = x.shape[1]
    xp = jnp.pad(x, ((0, 0), (CONV_W - 1, 0), (0, 0)))
    y = b
    for k in range(CONV_W):
        y = y + xp[:, k:k + S] * w[k]
    return y


def rg_lru(x, w_a, b_a, w_i, b_i, lam):
    B, S, _ = x.shape
    xb = x.reshape(B, S, RG_BLOCKS, RG_BW)
    r = jax.nn.sigmoid((jnp.einsum('bsnc,ncd->bsnd', xb, w_a).reshape(B, S, D_RNN) + b_a).astype(jnp.float32))
    i = jax.nn.sigmoid((jnp.einsum('bsnc,ncd->bsnd', xb, w_i).reshape(B, S, D_RNN) + b_i).astype(jnp.float32))
    log_a = -RG_C * r * jax.nn.softplus(-lam.astype(jnp.float32))
    a = jnp.exp(log_a)
    mult = jnp.sqrt(-jnp.expm1(2.0 * log_a))
    mult = jnp.where(jnp.arange(S)[None, :, None] == 0, 1.0, mult)
    u = mult * i * x.astype(jnp.float32)

    def comb(c1, c2):
        a1, b1 = c1
        a2, b2 = c2
        return a1 * a2, a2 * b1 + b2

    _, h = lax.associative_scan(comb, (a, u), axis=1)
    return h.astype(x.dtype)


def compress(kv, pos, w1, w2):
    B, S, G, dh = kv.shape
    n_c = (S - CMP_LEN) // CMP_STRIDE + 1
    idx = jnp.arange(n_c)[:, None] * CMP_STRIDE + jnp.arange(CMP_LEN)[None, :]
    blk = kv[:, idx] + pos[None, None, :, None, :]
    blk = blk.transpose(0, 1, 3, 2, 4).reshape(B, n_c, G, CMP_LEN * dh)
    return jax.nn.gelu(blk @ w1) @ w2


def nsa(q, kc, vc, k_slc, v_slc, k_win, v_win, gates):
    B, S = q.shape[0], q.shape[1]
    n_c = kc.shape[1]
    n_sel = S // SEL_LEN
    topn = min(SEL_TOPN, n_sel)
    scale = HEAD_DIM ** -0.5
    cmp_end = jnp.arange(n_c) * CMP_STRIDE + CMP_LEN - 1
    cs = jnp.arange(n_c) * CMP_STRIDE
    ss = jnp.arange(n_sel) * SEL_LEN
    overlap = ((cs[:, None] < ss[None, :] + SEL_LEN) & (cs[:, None] + CMP_LEN > ss[None, :])).astype(jnp.float32)
    ksb = k_slc.reshape(B, n_sel, SEL_LEN, N_KV, HEAD_DIM).transpose(0, 3, 1, 2, 4)
    vsb = v_slc.reshape(B, n_sel, SEL_LEN, N_KV, HEAD_DIM).transpose(0, 3, 1, 2, 4)
    kwp = jnp.pad(k_win, ((0, 0), (WINDOW, 0), (0, 0), (0, 0)))
    vwp = jnp.pad(v_win, ((0, 0), (WINDOW, 0), (0, 0), (0, 0)))
    bi = jnp.arange(B)[:, None, None]
    gi = jnp.arange(N_KV)[None, :, None]
    blk_ids = jnp.arange(n_sel)

    def q_block(qi):
        s0 = qi * Q_BLK
        tpos = s0 + jnp.arange(Q_BLK)
        qb = lax.dynamic_slice_in_dim(q, s0, Q_BLK, axis=1).reshape(B, Q_BLK, N_KV, HPG, HEAD_DIM)
        gb = lax.dynamic_slice_in_dim(gates, s0, Q_BLK, axis=1).reshape(B, Q_BLK, 3, N_KV, HPG)[..., None]
        s_c = jnp.einsum('bqgjd,bcgd->bqgjc', qb, kc) * scale
        mask_c = (cmp_end[None, :] <= tpos[:, None])[None, :, None, None, :]
        p_c = masked_softmax(s_c, mask_c)
        o_c = jnp.einsum('bqgjc,bcgd->bqgjd', p_c.astype(vc.dtype), vc)
        imp = jnp.einsum('bqgjc,cn->bqgn', p_c, overlap)
        cur = tpos // SEL_LEN
        valid = blk_ids[None, :] <= cur[:, None]
        forced = (blk_ids[None, :] == 0) | (blk_ids[None, :] == cur[:, None]) | (blk_ids[None, :] == cur[:, None] - 1)
        score = jnp.where(forced[None, :, None, :], FORCE_SCORE, imp)
        score = jnp.where(valid[None, :, None, :], score, -1.0)
        top_v, top_i = lax.top_k(score, topn)
        sel_ok = top_v >= 0.0
        idx_t = top_i.transpose(0, 2, 1, 3).reshape(B, N_KV, Q_BLK * topn)
        ksel = ksb[bi, gi, idx_t].reshape(B, N_KV, Q_BLK, topn * SEL_LEN, HEAD_DIM)
        vsel = vsb[bi, gi, idx_t].reshape(B, N_KV, Q_BLK, topn * SEL_LEN, HEAD_DIM)
        kpos = top_i[..., None] * SEL_LEN + jnp.arange(SEL_LEN)
        mask_s = ((kpos <= tpos[None, :, None, None, None]) & sel_ok[..., None]).reshape(B, Q_BLK, N_KV, topn * SEL_LEN)
        s_s = jnp.einsum('bqgjd,bgqkd->bqgjk', qb, ksel) * scale
        p_s = masked_softmax(s_s, mask_s[:, :, :, None, :])
        o_s = jnp.einsum('bqgjk,bgqkd->bqgjd', p_s.astype(vsel.dtype), vsel)
        kw = lax.dynamic_slice_in_dim(kwp, s0, WINDOW + Q_BLK, axis=1)
        vw = lax.dynamic_slice_in_dim(vwp, s0, WINDOW + Q_BLK, axis=1)
        kp = s0 - WINDOW + jnp.arange(WINDOW + Q_BLK)
        mask_w = (kp[None, :] <= tpos[:, None]) & (tpos[:, None] - kp[None, :] < WINDOW) & (kp[None, :] >= 0)
        s_w = jnp.einsum('bqgjd,bkgd->bqgjk', qb, kw) * scale
        p_w = masked_softmax(s_w, mask_w[None, :, None, None, :])
        o_w = jnp.einsum('bqgjk,bkgd->bqgjd', p_w.astype(vw.dtype), vw)
        o = gb[:, :, 0] * o_c + gb[:, :, 1] * o_s + gb[:, :, 2] * o_w
        return o.reshape(B, Q_BLK, Q_W)

    out = lax.map(q_block, jnp.arange(S // Q_BLK))
    return out.transpose(1, 0, 2, 3).reshape(B, S, Q_W)


def setup_inputs(seed: int = 0) -> dict:
    key = jax.random.key(seed)
    ks = jax.random.split(key, 32)
    f32 = jnp.float32
    L = DEPTH

    def nrm(k, shape, fan_in):
        return jax.random.normal(k, shape, f32) * (fan_in ** -0.5)

    a0 = jax.random.uniform(ks[8], (L, D_RNN), f32, 0.9, 0.999)
    return {
        "x": jax.random.normal(ks[0], (BATCH, SEQ, D_MODEL), f32),
        "norm1": 1.0 + 0.02 * jax.random.normal(ks[1], (L, D_MODEL), f32),
        "w_in": nrm(ks[2], (L, D_MODEL, D_IN), D_MODEL),
        "conv_w": nrm(ks[3], (L, CONV_W, D_RNN), CONV_W),
        "conv_b": 0.01 * jax.random.normal(ks[4], (L, D_RNN), f32),
        "rg_wa": nrm(ks[5], (L, RG_BLOCKS, RG_BW, RG_BW), RG_BW),
        "rg_ba": 0.01 * jax.random.normal(ks[6], (L, D_RNN), f32),
        "rg_wi": nrm(ks[7], (L, RG_BLOCKS, RG_BW, RG_BW), RG_BW),
        "rg_bi": 0.01 * jax.random.normal(ks[9], (L, D_RNN), f32),
        "rg_lambda": jnp.log(a0) - jnp.log1p(-a0),
        "q_norm": 1.0 + 0.02 * jax.random.normal(ks[10], (L, HEAD_DIM), f32),
        "k_norm": 1.0 + 0.02 * jax.random.normal(ks[11], (L, 3, HEAD_DIM), f32),
        "cmp_pos_k": 0.02 * jax.random.normal(ks[12], (L, CMP_LEN, HEAD_DIM), f32),
        "cmp_pos_v": 0.02 * jax.random.normal(ks[13], (L, CMP_LEN, HEAD_DIM), f32),
        "cmp_k_w1": nrm(ks[14], (L, CMP_LEN * HEAD_DIM, CMP_HID), CMP_LEN * HEAD_DIM),
        "cmp_k_w2": nrm(ks[15], (L, CMP_HID, HEAD_DIM), CMP_HID),
        "cmp_v_w1": nrm(ks[16], (L, CMP_LEN * HEAD_DIM, CMP_HID), CMP_LEN * HEAD_DIM),
        "cmp_v_w2": nrm(ks[17], (L, CMP_HID, HEAD_DIM), CMP_HID),
        "w_rg_out": nrm(ks[18], (L, D_RNN, D_MODEL), D_RNN),
        "w_nsa_out": nrm(ks[19], (L, Q_W, D_MODEL), Q_W),
        "w_o": nrm(ks[20], (L, D_MODEL, D_MODEL), D_MODEL),
        "norm2": 1.0 + 0.02 * jax.random.normal(ks[21], (L, D_MODEL), f32),
        "w_gate": nrm(ks[22], (L, D_MODEL, D_FF), D_MODEL),
        "w_up": nrm(ks[23], (L, D_MODEL, D_FF), D_MODEL),
        "w_down": nrm(ks[24], (L, D_FF, D_MODEL), D_FF),
    }


def reference(x, norm1, w_in, conv_w, conv_b, rg_wa, rg_ba, rg_wi, rg_bi, rg_lambda, q_norm, k_norm,
              cmp_pos_k, cmp_pos_v, cmp_k_w1, cmp_k_w2, cmp_v_w1, cmp_v_w2, w_rg_out, w_nsa_out, w_o,
              norm2, w_gate, w_up, w_down):
    B, S, _ = x.shape
    bounds = np.cumsum((0,) + IN_SPLITS)
    for l in range(DEPTH):
        h = rmsnorm(x, norm1[l])
        z = h @ w_in[l]
        (z_rx, z_ry, z_q, z_kc, z_vc, z_ks, z_vs, z_kw, z_vw, z_ng, z_mg) = [
            z[..., int(bounds[i]):int(bounds[i + 1])] for i in range(len(IN_SPLITS))]
        xr = causal_conv(z_rx, conv_w[l], conv_b[l])
        xr = rg_lru(xr, rg_wa[l], rg_ba[l], rg_wi[l], rg_bi[l], rg_lambda[l])
        y_a = (xr * jax.nn.gelu(z_ry)) @ w_rg_out[l]
        kvs = lambda t: t.reshape(B, S, N_KV, HEAD_DIM)
        q = rmsnorm(z_q.reshape(B, S, N_HEADS, HEAD_DIM), q_norm[l])
        kc = rmsnorm(compress(kvs(z_kc), cmp_pos_k[l], cmp_k_w1[l], cmp_k_w2[l]), k_norm[l, 0])
        vc = compress(kvs(z_vc), cmp_pos_v[l], cmp_v_w1[l], cmp_v_w2[l])
        k_s = rmsnorm(kvs(z_ks), k_norm[l, 1])
        k_w = rmsnorm(kvs(z_kw), k_norm[l, 2])
        ng = jax.nn.sigmoid(z_ng).reshape(B, S, 3, N_HEADS)
        y_b = nsa(q, kc, vc, k_s, kvs(z_vs), k_w, kvs(z_vw), ng) @ w_nsa_out[l]
        mg = jax.nn.sigmoid(z_mg).reshape(B, S, 2, D_MODEL)
        x = x + (mg[:, :, 0] * y_a + mg[:, :, 1] * y_b) @ w_o[l]
        h2 = rmsnorm(x, norm2[l])
        x = x + (jax.nn.silu(h2 @ w_gate[l]) * (h2 @ w_up[l])) @ w_down[l]
    return x
```

```python
import functools

import jax
import jax.numpy as jnp
from jax import lax
from jax.experimental import pallas as pl
from jax.experimental.pallas import tpu as pltpu

D_MODEL = 1024
D_RNN = 1024
RG_BLOCKS = 4
RG_BW = D_RNN // RG_BLOCKS
CONV_W = 4
RG_C = 8.0
N_HEADS = 16
N_KV = 4
HEAD_DIM = 64
HPG = N_HEADS // N_KV
CMP_LEN = 32
CMP_STRIDE = 16
CMP_HID = 256
SEL_LEN = 64
SEL_TOPN = 16
WINDOW = 512
Q_BLK = 64
D_FF = 2816
EPS = 1e-6
FORCE_SCORE = 1e6
NEG = -1e30
KCH = 256

F32 = jnp.float32
BF16 = jnp.bfloat16

VMEM_LIMIT = 48 * 1024 * 1024


def _params(sem):
    return pltpu.CompilerParams(dimension_semantics=sem, vmem_limit_bytes=VMEM_LIMIT)


def _dot(a, b):
    return jnp.dot(a, b, preferred_element_type=F32)


def _dot_nt(a, b):
    return lax.dot_general(a, b, (((1,), (1,)), ((), ())), preferred_element_type=F32)


def _norm_matmul_kernel(x_ref, g_ref, w_ref, o_ref, h_ref, *, act):
    @pl.when(pl.program_id(2) == 0)
    def _():
        x = x_ref[...]
        y = x * lax.rsqrt(jnp.mean(x * x, axis=-1, keepdims=True) + EPS) * g_ref[...]
        h_ref[...] = y.astype(BF16)

    acc = _dot(h_ref[...], w_ref[...])
    if act == "sigmoid":
        acc = jax.nn.sigmoid(acc)
    elif act == "gelu":
        acc = jax.nn.gelu(acc)
    o_ref[...] = acc.astype(o_ref.dtype)


def _norm_matmul(x, gain, w, *, act=None, time_major=False, tm=512, tn=512):
    B, S, D = x.shape
    N = w.shape[1]
    tn = min(tn, N)
    assert S % tm == 0 and N % tn == 0
    nj = N // tn
    ni = S // tm
    if time_major:
        out_shape = jax.ShapeDtypeStruct((S, B * N), F32)
        out_spec = pl.BlockSpec((tm, tn), lambda b, i, j: (i, b * nj + j))
    else:
        out_shape = jax.ShapeDtypeStruct((B * S, N), F32)
        out_spec = pl.BlockSpec((tm, tn), lambda b, i, j: (b * ni + i, j))
    return pl.pallas_call(
        functools.partial(_norm_matmul_kernel, act=act),
        grid=(B, ni, nj),
        in_specs=[
            pl.BlockSpec((None, tm, D), lambda b, i, j: (b, i, 0)),
            pl.BlockSpec((1, D), lambda b, i, j: (0, 0)),
            pl.BlockSpec((D, tn), lambda b, i, j: (0, j)),
        ],
        out_specs=out_spec,
        out_shape=out_shape,
        scratch_shapes=[pltpu.VMEM((tm, D), BF16)],
        compiler_params=_params(("parallel", "parallel", "arbitrary")),
    )(x, gain, w)


def _rglru_kernel(x_ref, gy_ref, cw_ref, cb_ref, wa_ref, ba_ref, wi_ref, bi_ref, lam_ref, wo_ref,
                  o_ref, xb_ref, a_ref, u_ref, hb_ref, hs_ref, *, nb, tt):
    R = nb * tt
    halo = (CONV_W - 1) * nb
    step = pl.program_id(0)

    @pl.when(step == 0)
    def _():
        xb_ref[0:halo, :] = jnp.zeros((halo, D_RNN), F32)
        hs_ref[...] = jnp.zeros((nb, D_RNN), F32)

    xb_ref[halo:halo + R, :] = x_ref[...]
    xr = cb_ref[...] + cw_ref[0:1, :] * xb_ref[0:R, :]
    for k in range(1, CONV_W):
        xr = xr + cw_ref[k:k + 1, :] * xb_ref[k * nb:k * nb + R, :]
    tail = xb_ref[R:R + halo, :]
    xb_ref[0:halo, :] = tail

    xrb = xr.astype(BF16)
    sp = jax.nn.softplus(-lam_ref[...])
    row = lax.broadcasted_iota(jnp.int32, (R, RG_BW), 0)
    first = (row < nb) & (step == 0)
    for n in range(RG_BLOCKS):
        sl = slice(n * RG_BW, (n + 1) * RG_BW)
        xn = xrb[:, sl]
        r = jax.nn.sigmoid(_dot(xn, wa_ref[n]) + ba_ref[:, sl])
        ig = jax.nn.sigmoid(_dot(xn, wi_ref[n]) + bi_ref[:, sl])
        log_a = (-RG_C) * r * sp[:, sl]
        a = jnp.exp(log_a)
        mult = jnp.sqrt(1.0 - jnp.exp(2.0 * log_a))
        mult = jnp.where(first, 1.0, mult)
        a_ref[:, sl] = a
        u_ref[:, sl] = mult * ig * xr[:, sl]

    def body(t, h):
        r0 = pl.multiple_of(t * nb, nb)
        h = a_ref[pl.ds(r0, nb), :] * h + u_ref[pl.ds(r0, nb), :]
        hb_ref[pl.ds(r0, nb), :] = h
        return h

    h = lax.fori_loop(0, tt, body, hs_ref[...], unroll=8)
    hs_ref[...] = h
    o_ref[...] = _dot((hb_ref[...] * gy_ref[...]).astype(BF16), wo_ref[...])


def _rglru(rx_tm, gy_tm, conv_w, conv_b, wa, ba, wi, bi, lam, wo, *, nb, tt=32):
    rows = rx_tm.shape[0]
    S = rows // nb
    assert S % tt == 0
    R = nb * tt
    halo = (CONV_W - 1) * nb
    full = lambda shape: pl.BlockSpec(shape, lambda s: (0,) * len(shape))
    return pl.pallas_call(
        functools.partial(_rglru_kernel, nb=nb, tt=tt),
        grid=(S // tt,),
        in_specs=[
            pl.BlockSpec((R, D_RNN), lambda s: (s, 0)),
            pl.BlockSpec((R, D_RNN), lambda s: (s, 0)),
            full((CONV_W, D_RNN)),
            full((1, D_RNN)),
            full((RG_BLOCKS, RG_BW, RG_BW)),
            full((1, D_RNN)),
            full((RG_BLOCKS, RG_BW, RG_BW)),
            full((1, D_RNN)),
            full((1, D_RNN)),
            full((D_RNN, D_MODEL)),
        ],
        out_specs=pl.BlockSpec((R, D_MODEL), lambda s: (s, 0)),
        out_shape=jax.ShapeDtypeStruct((rows, D_MODEL), F32),
        scratch_shapes=[
            pltpu.VMEM((R + halo, D_RNN), F32),
            pltpu.VMEM((R, D_RNN), F32),
            pltpu.VMEM((R, D_RNN), F32),
            pltpu.VMEM((R, D_RNN), F32),
            pltpu.VMEM((nb, D_RNN), F32),
        ],
        compiler_params=_params(("arbitrary",)),
    )(rx_tm, gy_tm, conv_w, conv_b, wa, ba, wi, bi, lam, wo)


def _headnorm_kernel(x_ref, g_ref, o_ref, *, scale):
    x = x_ref[...]
    y = x * lax.rsqrt(jnp.mean(x * x, axis=-1, keepdims=True) + EPS) * g_ref[...]
    o_ref[...] = (y * scale).astype(o_ref.dtype)


def _headnorm(x, gain, *, scale=1.0, tr=4096):
    rows = x.shape[0]
    assert rows % tr == 0
    return pl.pallas_call(
        functools.partial(_headnorm_kernel, scale=scale),
        grid=(rows // tr,),
        in_specs=[pl.BlockSpec((tr, HEAD_DIM), lambda i: (i, 0)),
                  pl.BlockSpec((1, HEAD_DIM), lambda i: (0, 0))],
        out_specs=pl.BlockSpec((tr, HEAD_DIM), lambda i: (i, 0)),
        out_shape=jax.ShapeDtypeStruct((rows, HEAD_DIM), BF16),
        compiler_params=_params(("parallel",)),
    )(x, gain)


def _compress_kernel(u_ref, pos_ref, w1_ref, w2_ref, g_ref, o_ref):
    kind = pl.program_id(0)
    nch = u_ref.shape[0]
    u = u_ref[...]
    a = _dot((u + pos_ref[0:1, :]).astype(BF16), w1_ref[0])
    b = _dot((u + pos_ref[1:2, :]).astype(BF16), w1_ref[1])
    hid = a + pltpu.roll(b, nch - 1, axis=0)
    hid = jax.nn.gelu(hid)
    out = _dot(hid.astype(BF16), w2_ref[...])
    normed = out * lax.rsqrt(jnp.mean(out * out, axis=-1, keepdims=True) + EPS) * g_ref[...]
    res = jnp.where(kind == 0, normed, out)
    row = lax.broadcasted_iota(jnp.int32, res.shape, 0)
    o_ref[...] = jnp.where(row < nch - 1, res, 0.0).astype(o_ref.dtype)


def _compress(u, pos, w1, w2, gain):
    _, BG, nch, W = u.shape
    return pl.pallas_call(
        _compress_kernel,
        grid=(2, BG),
        in_specs=[
            pl.BlockSpec((None, None, nch, W), lambda k, i: (k, i, 0, 0)),
            pl.BlockSpec((None, 2, W), lambda k, i: (k, 0, 0)),
            pl.BlockSpec((None, 2, W, CMP_HID), lambda k, i: (k, 0, 0, 0)),
            pl.BlockSpec((None, CMP_HID, HEAD_DIM), lambda k, i: (k, 0, 0)),
            pl.BlockSpec((1, HEAD_DIM), lambda k, i: (0, 0)),
        ],
        out_specs=pl.BlockSpec((None, None, nch, HEAD_DIM), lambda k, i: (k, i, 0, 0)),
        out_shape=jax.ShapeDtypeStruct((2, BG, nch, HEAD_DIM), BF16),
        compiler_params=_params(("parallel", "parallel")),
    )(u, pos, w1, w2, gain)


def _tile4(x):
    return jnp.concatenate([x, x, x, x], axis=0)


def _nsa_kernel(q_ref, kct_ref, vc_ref, kst_ref, vs_ref, kwt_ref, vw_ref, ng_ref, wo_ref, o_ref, *, nsel):
    qi = pl.program_id(1)
    s0 = qi * Q_BLK
    cur = qi
    ncmp = kct_ref.shape[2]
    RQ = HPG * Q_BLK

    tok = lax.broadcasted_iota(jnp.int32, (Q_BLK, KCH), 0) + s0
    key = lax.broadcasted_iota(jnp.int32, (Q_BLK, KCH), 1)
    blk_row = lax.broadcasted_iota(jnp.int32, (nsel, KCH), 0)
    blk_of_key = lax.broadcasted_iota(jnp.int32, (nsel, KCH), 1) // SEL_LEN
    ng = ng_ref[...]
    lane_ng = lax.broadcasted_iota(jnp.int32, ng.shape, 1)

    ov_n = lax.broadcasted_iota(jnp.int32, (nsel, ncmp), 0) * SEL_LEN
    ov_c = lax.broadcasted_iota(jnp.int32, (nsel, ncmp), 1) * CMP_STRIDE
    overlap = jnp.where((ov_c < ov_n + SEL_LEN) & (ov_c + CMP_LEN > ov_n), 1.0, 0.0).astype(BF16)
    eye = jnp.where(lax.broadcasted_iota(jnp.int32, (Q_BLK, Q_BLK), 0)
                    == lax.broadcasted_iota(jnp.int32, (Q_BLK, Q_BLK), 1), 1.0, 0.0).astype(BF16)
    n_io = lax.broadcasted_iota(jnp.int32, (nsel, Q_BLK), 0)

    def flash(q2, kt_ref, v_ref, g, lo, hi, mask_fn):
        def body(i, carry):
            m, l, acc = carry
            s = _dot(q2, kt_ref[g, i])
            maskf = _tile4(mask_fn(i))
            ok = maskf > 0.5
            sm = jnp.where(ok, s, NEG)
            mnew = jnp.maximum(m, jnp.max(sm, axis=-1, keepdims=True))
            alpha = jnp.exp(m - mnew)
            e = jnp.where(ok, jnp.exp(sm - mnew), 0.0)
            l = alpha * l + jnp.sum(e, axis=-1, keepdims=True)
            k0 = pl.multiple_of(i * KCH, KCH)
            pv = _dot(e.astype(BF16), v_ref[g, pl.ds(k0, KCH), :])
            return mnew, l, alpha * acc + pv

        init = (jnp.full((RQ, 1), NEG, F32), jnp.zeros((RQ, 1), F32), jnp.zeros((RQ, HEAD_DIM), F32))
        _, l, acc = lax.fori_loop(lo, hi, body, init)
        return acc / jnp.maximum(l, 1e-30)

    def group(g, carry):
        q2 = q_ref[pl.ds(g * HPG, HPG)].reshape(RQ, HEAD_DIM)

        sc = _dot(q2, kct_ref[g])
        cend = lax.broadcasted_iota(jnp.int32, (Q_BLK, ncmp), 1) * CMP_STRIDE + (CMP_LEN - 1)
        tq = lax.broadcasted_iota(jnp.int32, (Q_BLK, ncmp), 0) + s0
        okc = _tile4(jnp.where(cend <= tq, 1.0, 0.0)) > 0.5
        scm = jnp.where(okc, sc, NEG)
        mc = jnp.max(scm, axis=-1, keepdims=True)
        ec = jnp.where(okc, jnp.exp(scm - mc), 0.0)
        pc = ec / jnp.maximum(jnp.sum(ec, axis=-1, keepdims=True), 1e-30)
        o_c = _dot(pc.astype(BF16), vc_ref[g])

        psum = pc[0:Q_BLK]
        for j in range(1, HPG):
            psum = psum + pc[j * Q_BLK:(j + 1) * Q_BLK]
        p_hi = psum.astype(BF16)
        p_lo = (psum - p_hi.astype(F32)).astype(BF16)
        imp = _dot_nt(overlap, p_hi) + _dot_nt(overlap, p_lo)
        forced = (n_io == 0) | (n_io == cur) | (n_io == cur - 1)
        valid = n_io <= cur
        score = jnp.where(forced, FORCE_SCORE, imp)
        score = jnp.where(valid, score, -1.0)
        rank = jnp.zeros((nsel, Q_BLK), F32)
        for mrow in range(nsel):
            rowv = score[mrow:mrow + 1, :]
            ge = jnp.where(rowv >= score, 1.0, 0.0)
            gt = jnp.where(rowv > score, 1.0, 0.0)
            rank = rank + jnp.where(n_io > mrow, ge, gt)
        sel_t = jnp.where((rank < float(SEL_TOPN)) & valid, 1.0, 0.0).astype(BF16)
        sel = _dot_nt(eye, sel_t).astype(BF16)

        def mask_sel(i):
            expand = jnp.where(blk_row == i * (KCH // SEL_LEN) + blk_of_key, 1.0, 0.0).astype(BF16)
            picked = _dot(sel, expand)
            return jnp.where(key + i * KCH <= tok, picked, 0.0)

        o_s = flash(q2, kst_ref, vs_ref, g, 0, qi // (KCH // SEL_LEN) + 1, mask_sel)

        def mask_win(i):
            kp = key + i * KCH
            return jnp.where((kp <= tok) & (tok - kp < WINDOW), 1.0, 0.0)

        hi_w = qi // (KCH // SEL_LEN) + 1
        lo_w = jnp.maximum(hi_w - (WINDOW // KCH + 1), 0)
        o_w = flash(q2, kwt_ref, vw_ref, g, lo_w, hi_w, mask_win)

        for j in range(HPG):
            h = g * HPG + j
            rs = slice(j * Q_BLK, (j + 1) * Q_BLK)
            gates = [jnp.sum(jnp.where(lane_ng == c * N_HEADS + h, ng, 0.0), axis=1, keepdims=True)
                     for c in range(3)]
            o_h = gates[0] * o_c[rs] + gates[1] * o_s[rs] + gates[2] * o_w[rs]
            o_ref[...] += _dot(o_h.astype(BF16), wo_ref[h])
        return carry

    o_ref[...] = jnp.zeros(o_ref.shape, F32)
    lax.fori_loop(0, N_KV, group, 0)


def _nsa(q, kct, vc, kst, vs, kwt, vw, ng, wo):
    B, H, S, dh = q.shape
    nq = S // Q_BLK
    ncmp = kct.shape[3]
    nch = S // KCH
    per_b = lambda shape: pl.BlockSpec((None,) + shape, lambda b, i: (b,) + (0,) * len(shape))
    return pl.pallas_call(
        functools.partial(_nsa_kernel, nsel=S // SEL_LEN),
        grid=(B, nq),
        in_specs=[
            pl.BlockSpec((None, H, Q_BLK, dh), lambda b, i: (b, 0, i, 0)),
            per_b((N_KV, dh, ncmp)),
            per_b((N_KV, ncmp, dh)),
            per_b((N_KV, nch, dh, KCH)),
            per_b((N_KV, S, dh)),
            per_b((N_KV, nch, dh, KCH)),
            per_b((N_KV, S, dh)),
            pl.BlockSpec((Q_BLK, 128), lambda b, i: (b * nq + i, 0)),
            pl.BlockSpec((H, dh, D_MODEL), lambda b, i: (0, 0, 0)),
        ],
        out_specs=pl.BlockSpec((Q_BLK, D_MODEL), lambda b, i: (b * nq + i, 0)),
        out_shape=jax.ShapeDtypeStruct((B * S, D_MODEL), F32),
        compiler_params=_params(("parallel", "arbitrary")),
    )(q, kct, vc, kst, vs, kwt, vw, ng, wo)


def _merge_kernel(x_ref, ya_ref, yb_ref, mg_ref, wo_ref, o_ref):
    mixed = mg_ref[:, 0:D_MODEL] * ya_ref[...] + mg_ref[:, D_MODEL:2 * D_MODEL] * yb_ref[...]
    o_ref[...] = x_ref[...] + _dot(mixed.astype(BF16), wo_ref[...])


def _merge(x2d, ya_tm, yb, mg, wo, *, B, S, tm=512):
    ni = S // tm
    return pl.pallas_call(
        _merge_kernel,
        grid=(B, ni),
        in_specs=[
            pl.BlockSpec((tm, D_MODEL), lambda b, i: (b * ni + i, 0)),
            pl.BlockSpec((tm, D_MODEL), lambda b, i: (i, b)),
            pl.BlockSpec((tm, D_MODEL), lambda b, i: (b * ni + i, 0)),
            pl.BlockSpec((tm, 2 * D_MODEL), lambda b, i: (b * ni + i, 0)),
            pl.BlockSpec((D_MODEL, D_MODEL), lambda b, i: (0, 0)),
        ],
        out_specs=pl.BlockSpec((tm, D_MODEL), lambda b, i: (b * ni + i, 0)),
        out_shape=jax.ShapeDtypeStruct((B * S, D_MODEL), F32),
        compiler_params=_params(("parallel", "parallel")),
    )(x2d, ya_tm, yb, mg, wo)


def _ffn_kernel(x_ref, g_ref, wg_ref, wu_ref, wd_ref, o_ref, h_ref, acc_ref):
    j = pl.program_id(1)

    @pl.when(j == 0)
    def _():
        x = x_ref[...]
        y = x * lax.rsqrt(jnp.mean(x * x, axis=-1, keepdims=True) + EPS) * g_ref[...]
        h_ref[...] = y.astype(BF16)
        acc_ref[...] = jnp.zeros(acc_ref.shape, F32)

    h = h_ref[...]
    act = jax.nn.silu(_dot(h, wg_ref[...])) * _dot(h, wu_ref[...])
    acc_ref[...] += _dot(act.astype(BF16), wd_ref[...])

    @pl.when(j == pl.num_programs(1) - 1)
    def _():
        o_ref[...] = x_ref[...] + acc_ref[...]


def _ffn(x2d, gain, wg, wu, wd, *, tm=512, tf=256):
    T = x2d.shape[0]
    assert T % tm == 0 and D_FF % tf == 0
    return pl.pallas_call(
        _ffn_kernel,
        grid=(T // tm, D_FF // tf),
        in_specs=[
            pl.BlockSpec((tm, D_MODEL), lambda i, j: (i, 0)),
            pl.BlockSpec((1, D_MODEL), lambda i, j: (0, 0)),
            pl.BlockSpec((D_MODEL, tf), lambda i, j: (0, j)),
            pl.BlockSpec((D_MODEL, tf), lambda i, j: (0, j)),
            pl.BlockSpec((tf, D_MODEL), lambda i, j: (j, 0)),
        ],
        out_specs=pl.BlockSpec((tm, D_MODEL), lambda i, j: (i, 0)),
        out_shape=jax.ShapeDtypeStruct((T, D_MODEL), F32),
        scratch_shapes=[pltpu.VMEM((tm, D_MODEL), BF16), pltpu.VMEM((tm, D_MODEL), F32)],
        compiler_params=_params(("parallel", "arbitrary")),
    )(x2d, gain, wg, wu, wd)


def _layer(x, p):
    B, S, _ = x.shape
    T = B * S
    G, dh = N_KV, HEAD_DIM
    row = lambda v: v.reshape(1, -1)
    w_in = p["w_in"].astype(BF16)
    o_q = 2 * D_RNN
    o_kv = o_q + N_HEADS * dh
    o_ng = o_kv + 6 * G * dh
    o_mg = o_ng + 3 * N_HEADS
    g1 = row(p["norm1"])

    rx_tm = _norm_matmul(x, g1, w_in[:, 0:D_RNN], time_major=True)
    gy_tm = _norm_matmul(x, g1, w_in[:, D_RNN:o_q], act="gelu", time_major=True)
    zq = _norm_matmul(x, g1, w_in[:, o_q:o_kv])
    zkv = _norm_matmul(x, g1, w_in[:, o_kv:o_ng])
    w_ng = jnp.pad(w_in[:, o_ng:o_mg], ((0, 0), (0, 128 - 3 * N_HEADS)))
    ng = _norm_matmul(x, g1, w_ng, act="sigmoid")
    mg = _norm_matmul(x, g1, w_in[:, o_mg:], act="sigmoid")

    ya_tm = _rglru(rx_tm.reshape(S * B, D_RNN), gy_tm.reshape(S * B, D_RNN), p["conv_w"], row(p["conv_b"]),
                   p["rg_wa"].astype(BF16), row(p["rg_ba"]), p["rg_wi"].astype(BF16), row(p["rg_bi"]),
                   row(p["rg_lambda"]), p["w_rg_out"].astype(BF16), nb=B)
    ya_tm = ya_tm.reshape(S, B * D_MODEL)

    q = _headnorm(zq.reshape(T * N_HEADS, dh), row(p["q_norm"]), scale=dh ** -0.5)
    q = q.reshape(B, S, N_HEADS, dh).transpose(0, 2, 1, 3)
    kv = zkv.reshape(B, S, 6, G, dh).transpose(2, 0, 3, 1, 4)
    nch16 = S // CMP_STRIDE
    u = kv[0:2].reshape(2, B * G, nch16, CMP_STRIDE * dh)
    pos = jnp.stack([p["cmp_pos_k"], p["cmp_pos_v"]]).reshape(2, 2, CMP_STRIDE * dh)
    w1 = jnp.stack([p["cmp_k_w1"], p["cmp_v_w1"]]).astype(BF16).reshape(2, 2, CMP_STRIDE * dh, CMP_HID)
    w2 = jnp.stack([p["cmp_k_w2"], p["cmp_v_w2"]]).astype(BF16)
    cmp = _compress(u, pos, w1, w2, row(p["k_norm"][0]))
    kct = cmp[0].reshape(B, G, nch16, dh).transpose(0, 1, 3, 2)
    vc = cmp[1].reshape(B, G, nch16, dh)

    def keys_t(z, gain):
        k = _headnorm(z.reshape(B * G * S, dh), gain)
        return k.reshape(B, G, S // KCH, KCH, dh).transpose(0, 1, 2, 4, 3)

    kst = keys_t(kv[2], row(p["k_norm"][1]))
    kwt = keys_t(kv[4], row(p["k_norm"][2]))
    vs = kv[3].astype(BF16)
    vw = kv[5].astype(BF16)
    wno = p["w_nsa_out"].astype(BF16).reshape(N_HEADS, dh, D_MODEL)
    yb = _nsa(q, kct, vc, kst, vs, kwt, vw, ng, wno)

    x2d = x.reshape(T, D_MODEL)
    x1 = _merge(x2d, ya_tm, yb, mg, p["w_o"].astype(BF16), B=B, S=S)
    x2 = _ffn(x1, row(p["norm2"]), p["w_gate"].astype(BF16), p["w_up"].astype(BF16), p["w_down"].astype(BF16))
    return x2.reshape(B, S, D_MODEL)


def kernel(x, norm1, w_in, conv_w, conv_b, rg_wa, rg_ba, rg_wi, rg_bi, rg_lambda, q_norm, k_norm, cmp_pos_k,
           cmp_pos_v, cmp_k_w1, cmp_k_w2, cmp_v_w1, cmp_v_w2, w_rg_out, w_nsa_out, w_o, norm2, w_gate, w_up, w_down):
    params = dict(norm1=norm1, w_in=w_in, conv_w=conv_w, conv_b=conv_b, rg_wa=rg_wa, rg_ba=rg_ba, rg_wi=rg_wi,
                  rg_bi=rg_bi, rg_lambda=rg_lambda, q_norm=q_norm, k_norm=k_norm, cmp_pos_k=cmp_pos_k,
                  cmp_pos_v=cmp_pos_v, cmp_k_w1=cmp_k_w1, cmp_k_w2=cmp_k_w2, cmp_v_w1=cmp_v_w1, cmp_v_w2=cmp_v_w2,
                  w_rg_out=w_rg_out, w_nsa_out=w_nsa_out, w_o=w_o, norm2=norm2, w_gate=w_gate, w_up=w_up,
                  w_down=w_down)
    for l in range(norm1.shape[0]):
        x = _layer(x, {k: v[l] for k, v in params.items()})
    return x
```

```python
import functools

import jax
import jax.numpy as jnp
from jax import lax
from jax.experimental import pallas as pl
from jax.experimental.pallas import tpu as pltpu

D_MODEL = 1024
D_RNN = 1024
RG_BLOCKS = 4
RG_BW = D_RNN // RG_BLOCKS
CONV_W = 4
RG_C = 8.0
N_HEADS = 16
N_KV = 4
HEAD_DIM = 64
HPG = N_HEADS // N_KV
CMP_LEN = 32
CMP_STRIDE = 16
CMP_HID = 256
SEL_LEN = 64
SEL_TOPN = 16
WINDOW = 512
Q_BLK = 64
D_FF = 2816
EPS = 1e-6
FORCE_SCORE = 1e6
NEG = -1e30

KCH = 256
GW = HPG * HEAD_DIM
KVW = N_KV * HEAD_DIM
VROWS = 80
NG_PAD = 128

F32 = jnp.float32
BF16 = jnp.bfloat16

VMEM_LIMIT = 56 * 1024 * 1024


def _params(sem):
    return pltpu.CompilerParams(dimension_semantics=sem, vmem_limit_bytes=VMEM_LIMIT)


def _dot(a, b):
    return jnp.dot(a, b, preferred_element_type=F32)


def _dot_nt(a, b):
    return lax.dot_general(a, b, (((1,), (1,)), ((), ())), preferred_element_type=F32)


def _split(x):
    hi = x.astype(BF16)
    lo = (x - hi.astype(F32)).astype(BF16)
    return hi, lo


def _iota(shape, dim):
    return lax.broadcasted_iota(jnp.int32, shape, dim)


def _const_spec(shape):
    return pl.BlockSpec(shape, lambda *_: (0,) * len(shape))


_O_RY = D_RNN
_O_Q = 2 * D_RNN
_O_KV = _O_Q + N_HEADS * HEAD_DIM
_O_NG = _O_KV + 6 * KVW
_O_MG = _O_NG + NG_PAD
_W_COLS = _O_MG + 2 * D_MODEL


def _inproj_kernel(x_ref, g_ref, w_ref, rx_ref, gy_ref, zq_ref, zkv_ref, ng_ref, mg_ref, h_ref):
    x = x_ref[...]
    y = x * lax.rsqrt(jnp.mean(x * x, axis=-1, keepdims=True) + EPS) * g_ref[...]
    h_ref[...] = y.astype(BF16)

    def proj(c0, width):
        return _dot(h_ref[...], w_ref[:, c0:c0 + width])

    cw = 256
    for c in range(D_RNN // cw):
        rx_ref[:, c * cw:(c + 1) * cw] = proj(c * cw, cw)
        gy_ref[:, c * cw:(c + 1) * cw] = jax.nn.gelu(proj(_O_RY + c * cw, cw)).astype(gy_ref.dtype)
    for g in range(N_KV):
        zq_ref[g] = proj(_O_Q + g * GW, GW)
    for s in range(6):
        z = proj(_O_KV + s * KVW, KVW)
        for pr in range(KVW // 128):
            zkv_ref[s, pr] = z[:, pr * 128:(pr + 1) * 128]
    ng_ref[...] = jax.nn.sigmoid(proj(_O_NG, NG_PAD))
    for c in range(2 * D_MODEL // cw):
        mg_ref[:, c * cw:(c + 1) * cw] = jax.nn.sigmoid(proj(_O_MG + c * cw, cw)).astype(mg_ref.dtype)


def _inproj(x, gain, w, *, tm=256):
    B, S, D = x.shape
    T = B * S
    ni = S // tm
    tok = lambda b, i: (b * ni + i, 0)
    return pl.pallas_call(
        _inproj_kernel,
        grid=(B, ni),
        in_specs=[
            pl.BlockSpec((None, tm, D), lambda b, i: (b, i, 0)),
            _const_spec((1, D)),
            pl.BlockSpec((D, _W_COLS), lambda b, i: (0, 0), pipeline_mode=pl.Buffered(1)),
        ],
        out_specs=[
            pl.BlockSpec((tm, D_RNN), lambda b, i: (i, b)),
            pl.BlockSpec((tm, D_RNN), lambda b, i: (i, b)),
            pl.BlockSpec((N_KV, tm, GW), lambda b, i: (0, b * ni + i, 0)),
            pl.BlockSpec((6, KVW // 128, tm, 128), lambda b, i: (0, 0, b * ni + i, 0)),
            pl.BlockSpec((tm, NG_PAD), tok),
            pl.BlockSpec((tm, 2 * D_MODEL), tok),
        ],
        out_shape=[
            jax.ShapeDtypeStruct((S, B * D_RNN), F32),
            jax.ShapeDtypeStruct((S, B * D_RNN), BF16),
            jax.ShapeDtypeStruct((N_KV, T, GW), F32),
            jax.ShapeDtypeStruct((6, KVW // 128, T, 128), F32),
            jax.ShapeDtypeStruct((T, NG_PAD), F32),
            jax.ShapeDtypeStruct((T, 2 * D_MODEL), BF16),
        ],
        scratch_shapes=[pltpu.VMEM((tm, D), BF16)],
        compiler_params=_params(("parallel", "parallel")),
        name="inproj",
    )(x, gain, w)


def _rglru_kernel(x_ref, gy_ref, cw_ref, cb_ref, wa_ref, ba_ref, wi_ref, bi_ref, lam_ref, wo_ref,
                  o_ref, xb_ref, a_ref, u_ref, hb_ref, hs_ref, *, nb, tt):
    R = nb * tt
    halo = (CONV_W - 1) * nb
    step = pl.program_id(0)

    @pl.when(step == 0)
    def _():
        xb_ref[0:halo, :] = jnp.zeros((halo, D_RNN), F32)
        hs_ref[...] = jnp.zeros((nb, D_RNN), F32)

    xb_ref[halo:halo + R, :] = x_ref[...]
    xr = cb_ref[...] + cw_ref[0:1, :] * xb_ref[0:R, :]
    for k in range(1, CONV_W):
        xr = xr + cw_ref[k:k + 1, :] * xb_ref[k * nb:k * nb + R, :]
    tail = xb_ref[R:R + halo, :]
    xb_ref[0:halo, :] = tail

    xrb = xr.astype(BF16)
    sp = jax.nn.softplus(-lam_ref[...])
    row = _iota((R, RG_BW), 0)
    first = (row < nb) & (step == 0)
    for n in range(RG_BLOCKS):
        sl = slice(n * RG_BW, (n + 1) * RG_BW)
        xn = xrb[:, sl]
        r = jax.nn.sigmoid(_dot(xn, wa_ref[n]) + ba_ref[:, sl])
        ig = jax.nn.sigmoid(_dot(xn, wi_ref[n]) + bi_ref[:, sl])
        log_a = (-RG_C) * r * sp[:, sl]
        a = jnp.exp(log_a)
        mult = jnp.sqrt(1.0 - jnp.exp(2.0 * log_a))
        mult = jnp.where(first, 1.0, mult)
        a_ref[:, sl] = a
        u_ref[:, sl] = mult * ig * xr[:, sl]

    def body(t, h):
        r0 = pl.multiple_of(t * nb, nb)
        h = a_ref[pl.ds(r0, nb), :] * h + u_ref[pl.ds(r0, nb), :]
        hb_ref[pl.ds(r0, nb), :] = h
        return h

    h = lax.fori_loop(0, tt, body, hs_ref[...], unroll=8)
    hs_ref[...] = h
    o_ref[...] = _dot((hb_ref[...] * gy_ref[...].astype(F32)).astype(BF16), wo_ref[...])


def _rglru(rx_tm, gy_tm, conv_w, conv_b, wa, ba, wi, bi, lam, wo, *, nb, tt=32):
    rows = rx_tm.shape[0]
    S = rows // nb
    assert S % tt == 0
    R = nb * tt
    halo = (CONV_W - 1) * nb
    return pl.pallas_call(
        functools.partial(_rglru_kernel, nb=nb, tt=tt),
        grid=(S // tt,),
        in_specs=[
            pl.BlockSpec((R, D_RNN), lambda s: (s, 0)),
            pl.BlockSpec((R, D_RNN), lambda s: (s, 0)),
            _const_spec((CONV_W, D_RNN)),
            _const_spec((1, D_RNN)),
            _const_spec((RG_BLOCKS, RG_BW, RG_BW)),
            _const_spec((1, D_RNN)),
            _const_spec((RG_BLOCKS, RG_BW, RG_BW)),
            _const_spec((1, D_RNN)),
            _const_spec((1, D_RNN)),
            _const_spec((D_RNN, D_MODEL)),
        ],
        out_specs=pl.BlockSpec((R, D_MODEL), lambda s: (s, 0)),
        out_shape=jax.ShapeDtypeStruct((rows, D_MODEL), F32),
        scratch_shapes=[
            pltpu.VMEM((R + halo, D_RNN), F32),
            pltpu.VMEM((R, D_RNN), F32),
            pltpu.VMEM((R, D_RNN), F32),
            pltpu.VMEM((R, D_RNN), F32),
            pltpu.VMEM((nb, D_RNN), F32),
        ],
        compiler_params=_params(("arbitrary",)),
        name="rglru",
    )(rx_tm, gy_tm, conv_w, conv_b, wa, ba, wi, bi, lam, wo)


def _kvprep_kernel(zk_ref, zv_ref, g_ref, kn_ref, vt_ref):
    x = jnp.concatenate([zk_ref[0], zk_ref[1]], axis=1)
    v = jnp.concatenate([zv_ref[0], zv_ref[1]], axis=1)
    seg = jnp.where(_iota((KVW, KVW), 0) // HEAD_DIM == _iota((KVW, KVW), 1) // HEAD_DIM, 1.0, 0.0).astype(BF16)
    hi, lo = _split(x * x)
    ssq = _dot(hi, seg) + _dot(lo, seg)
    kn = (x * lax.rsqrt(ssq * (1.0 / HEAD_DIM) + EPS) * g_ref[...]).astype(BF16)
    for pr in range(KVW // 128):
        kn_ref[pr] = kn[:, pr * 128:(pr + 1) * 128]
    eye = jnp.where(_iota((KVW, KVW), 0) == _iota((KVW, KVW), 1), 1.0, 0.0).astype(BF16)
    vt = _dot_nt(eye, v.astype(BF16)).astype(BF16)
    pad = jnp.where(_iota((VROWS - HEAD_DIM, KCH), 0) == 0, 1.0, 0.0).astype(BF16)
    for g in range(N_KV):
        vt_ref[g, 0:HEAD_DIM, :] = vt[g * HEAD_DIM:(g + 1) * HEAD_DIM]
        vt_ref[g, HEAD_DIM:VROWS, :] = pad


def _kvprep(zkv, gains, *, B, S):
    nch = S // KCH
    return pl.pallas_call(
        _kvprep_kernel,
        grid=(2, B, nch),
        in_specs=[
            pl.BlockSpec((None, KVW // 128, KCH, 128), lambda w, b, i: (2 + 2 * w, 0, b * nch + i, 0)),
            pl.BlockSpec((None, KVW // 128, KCH, 128), lambda w, b, i: (3 + 2 * w, 0, b * nch + i, 0)),
            pl.BlockSpec((None, 1, KVW), lambda w, b, i: (w, 0, 0)),
        ],
        out_specs=[
            pl.BlockSpec((None, None, KVW // 128, KCH, 128), lambda w, b, i: (w, b, 0, i, 0)),
            pl.BlockSpec((None, None, None, N_KV, VROWS, KCH), lambda w, b, i: (w, b, i, 0, 0, 0)),
        ],
        out_shape=[
            jax.ShapeDtypeStruct((2, B, KVW // 128, S, 128), BF16),
            jax.ShapeDtypeStruct((2, B, nch, N_KV, VROWS, KCH), BF16),
        ],
        compiler_params=_params(("parallel", "parallel", "parallel")),
        name="kvprep",
    )(zkv, zkv, gains)


def _compress_kernel(z_ref, pos_ref, w1_ref, w2_ref, g_ref, o_ref):
    kind = pl.program_id(0)
    nch = z_ref.shape[0] // CMP_STRIDE
    lane = _iota((nch, 128), 1)
    npair = CMP_STRIDE // 2
    for ge in range(2):
        halves = []
        for half in range(2):
            acc = None
            for p in range(npair):
                a = z_ref[pl.ds(2 * p, nch, stride=CMP_STRIDE), :]
                b = z_ref[pl.ds(2 * p + 1, nch, stride=CMP_STRIDE), :]
                if ge == 0:
                    comb = jnp.where(lane < HEAD_DIM, a, pltpu.roll(b, HEAD_DIM, axis=1))
                else:
                    comb = jnp.where(lane < HEAD_DIM, pltpu.roll(a, HEAD_DIM, axis=1), b)
                pr = half * npair + p
                lhs = (comb + pos_ref[pr:pr + 1, :]).astype(BF16)
                d = _dot(lhs, w1_ref[pr * 128:(pr + 1) * 128, :])
                acc = d if acc is None else acc + d
            halves.append(acc)
        hid = halves[0] + pltpu.roll(halves[1], nch - 1, axis=0)
        hid = jax.nn.gelu(hid)
        out = _dot(hid.astype(BF16), w2_ref[...])
        normed = out * lax.rsqrt(jnp.mean(out * out, axis=-1, keepdims=True) + EPS) * g_ref[...]
        res = jnp.where(kind == 0, normed, out)
        rowi = _iota(res.shape, 0)
        o_ref[ge] = jnp.where(rowi < nch - 1, res, 0.0).astype(o_ref.dtype)


def _compress(zkv, pos, w1, w2, gain, *, B, S):
    nch = S // CMP_STRIDE
    return pl.pallas_call(
        _compress_kernel,
        grid=(2, B, KVW // 128),
        in_specs=[
            pl.BlockSpec((None, None, S, 128), lambda k, b, lp: (k, lp, b, 0)),
            pl.BlockSpec((None, CMP_STRIDE, 128), lambda k, b, lp: (k, 0, 0)),
            pl.BlockSpec((None, CMP_LEN * HEAD_DIM, CMP_HID), lambda k, b, lp: (k, 0, 0)),
            pl.BlockSpec((None, CMP_HID, HEAD_DIM), lambda k, b, lp: (k, 0, 0)),
            _const_spec((1, HEAD_DIM)),
        ],
        out_specs=pl.BlockSpec((None, None, 2, nch, HEAD_DIM), lambda k, b, lp: (k, b, lp, 0, 0)),
        out_shape=jax.ShapeDtypeStruct((2, B, N_KV, nch, HEAD_DIM), BF16),
        compiler_params=_params(("parallel", "parallel", "parallel")),
        name="compress",
    )(zkv, pos, w1, w2, gain)


def _tile4(x):
    return jnp.concatenate([x, x, x, x], axis=0)


def _spread_heads(x):
    y = x + pltpu.roll(x, 2 * Q_BLK, axis=1)
    return y + pltpu.roll(y, Q_BLK, axis=1)


def _nsa_kernel(zq_ref, qg_ref, kc_ref, vc_ref, ks_ref, kw_ref, vst_ref, vwt_ref, ng_ref, o_ref,
                qtp_ref, oc_ref, score_ref, bias_ref, ms_ref, accs_ref, mw_ref, accw_ref, sbuf_ref, wbuf_ref,
                gates_ref, *, nsel):
    qi = pl.program_id(1)
    s0 = qi * Q_BLK
    cur = qi
    ncmp = kc_ref.shape[1]
    per_chunk = KCH // SEL_LEN

    lane_t = s0 + (_iota((1, GW), 1) % Q_BLK)
    lane_grp = _iota((1, GW), 1) // Q_BLK
    unit = jnp.where(_iota((Q_BLK, GW), 1) % Q_BLK == _iota((Q_BLK, GW), 0), 1.0, 0.0).astype(BF16)
    same_blk = _iota((GW, GW), 0) // Q_BLK == _iota((GW, GW), 1) // Q_BLK
    eye = jnp.where(_iota((HEAD_DIM, HEAD_DIM), 0) == _iota((HEAD_DIM, HEAD_DIM), 1), 1.0, 0.0).astype(BF16)
    ov_n = _iota((nsel, ncmp), 0) * SEL_LEN
    ov_c = _iota((nsel, ncmp), 1) * CMP_STRIDE
    overlap = jnp.where((ov_c < ov_n + SEL_LEN) & (ov_c + CMP_LEN > ov_n), 1.0, 0.0).astype(BF16)


    ok_c = _iota((ncmp, GW), 0) * CMP_STRIDE + (CMP_LEN - 1) <= lane_t
    groups = range(N_KV)
    ng_rows = _iota((GW, NG_PAD), 0) // Q_BLK
    ng_cols = _iota((GW, NG_PAD), 1)
    branch_of_col = jnp.where(_iota((8, NG_PAD), 1) // N_HEADS == _iota((8, NG_PAD), 0), 1.0, 0.0).astype(BF16)
    ng4 = _tile4(ng_ref[...])
    for g in groups:
        pick = (ng_cols % HPG == ng_rows) & ((ng_cols // HPG) % N_KV == g)
        g_hi, g_lo = _split(jnp.where(pick, ng4, 0.0))
        gates_ref[g] = _dot_nt(branch_of_col, g_hi) + _dot_nt(branch_of_col, g_lo)

    xts, vcts = [], []
    for g in groups:
        blocks = jnp.where(same_blk, _tile4(zq_ref[g]), 0.0)
        hi, lo = _split(blocks)
        xts.append(_dot_nt(unit, hi) + _dot_nt(unit, lo))
        vcts.append(_dot_nt(eye, vc_ref[g]).astype(BF16))
    scs = []
    for g in groups:
        xt = xts[g]
        ssq = jnp.sum(xt * xt, axis=0, keepdims=True)
        qt = (xt * lax.rsqrt(ssq * (1.0 / HEAD_DIM) + EPS) * qg_ref[...]).astype(BF16)
        off = (g % 2) * HEAD_DIM
        qtp_ref[g, off:off + HEAD_DIM, :] = qt
        qtp_ref[g, HEAD_DIM - off:2 * HEAD_DIM - off, :] = jnp.zeros((HEAD_DIM, GW), BF16)
        scs.append(_dot(kc_ref[g], qt))
    imp_all = jnp.zeros((nsel, GW), F32)
    for g in groups:
        scm = jnp.where(ok_c, scs[g], NEG)
        mc = jnp.max(scm, axis=0, keepdims=True)
        ec = jnp.where(ok_c, jnp.exp(scm - mc), 0.0)
        pc = ec * (1.0 / jnp.maximum(jnp.sum(ec, axis=0, keepdims=True), 1e-30))
        p_hi, p_lo = _split(pc)
        both = _dot(jnp.concatenate([vcts[g], overlap], axis=0), p_hi)
        oc_ref[g] = both[0:HEAD_DIM]
        imp = _spread_heads(both[HEAD_DIM:HEAD_DIM + nsel] + _dot(overlap, p_lo))
        imp_all = jnp.where(lane_grp == g, imp, imp_all)

    n_io = _iota((nsel, GW), 0)
    forced = (n_io == 0) | (n_io == cur) | (n_io == cur - 1)
    valid = n_io <= cur
    score = jnp.where(forced, FORCE_SCORE, imp_all)
    score = jnp.where(valid, score, -1.0)
    score_ref[...] = score

    def rank_step(m, rank):
        rowv = score_ref[pl.ds(m, 1), :]
        ge = jnp.where(rowv >= score, 1.0, 0.0)
        gt = jnp.where(rowv > score, 1.0, 0.0)
        return rank + jnp.where(n_io > m, ge, gt)

    rank = lax.fori_loop(0, cur + 1, rank_step, jnp.zeros((nsel, GW), F32))
    sel_all = jnp.where((rank < float(SEL_TOPN)) & valid, 1.0, 0.0)

    key_row = _iota((KCH, GW), 0)
    for g in range(N_KV):
        mine = jnp.where(lane_grp == g, sel_all, 0.0)
        bias_ref[g] = (_spread_heads(mine) - 1.0) * (-NEG)
    ms_ref[...] = jnp.full(ms_ref.shape, NEG, F32)
    mw_ref[...] = jnp.full(mw_ref.shape, NEG, F32)
    accs_ref[...] = jnp.zeros(accs_ref.shape, F32)
    accw_ref[...] = jnp.zeros(accw_ref.shape, F32)

    def update(g, s, vt, m_ref, acc_ref):
        m = m_ref[g]
        mnew = jnp.maximum(m, jnp.max(s, axis=0, keepdims=True))
        e = jnp.exp(s - mnew).astype(BF16)
        acc_ref[g] = jnp.exp(m - mnew) * acc_ref[g] + _dot(vt, e)
        m_ref[g] = mnew

    def scores(k_ref, i):
        k0 = pl.multiple_of(i * KCH, KCH)
        return [_dot(k_ref[g // 2, pl.ds(k0, KCH), :], qtp_ref[g]) for g in range(N_KV)]

    def park(buf_ref, vals):
        for g in range(N_KV):
            buf_ref[g] = vals[g]

    def sel_softmax(i, causal):
        for g in range(N_KV):
            parts = [sbuf_ref[g, nb * SEL_LEN:(nb + 1) * SEL_LEN, :] + bias_ref[g, pl.ds(i * per_chunk + nb, 1), :]
                     for nb in range(per_chunk)]
            s = jnp.concatenate(parts, axis=0)
            if causal:
                s = jnp.where(key_row + i * KCH <= lane_t, s, NEG)
            update(g, s, vst_ref[i, g], ms_ref, accs_ref)

    def win_softmax(i):
        dist = lane_t - (key_row + i * KCH)
        for g in range(N_KV):
            s = jnp.where(dist >= 0, jnp.where(dist < WINDOW, wbuf_ref[g], NEG), NEG)
            update(g, s, vwt_ref[i, g], mw_ref, accw_ref)

    def sel_step(i, carry):
        nxt = scores(ks_ref, i + 1)
        sel_softmax(i, False)
        park(sbuf_ref, nxt)
        return carry

    def win_step(i, carry):
        nxt = scores(kw_ref, i + 1)
        win_softmax(i)
        park(wbuf_ref, nxt)
        return carry

    n_chunks = qi // per_chunk + 1
    lo_w = jnp.maximum(n_chunks - (WINDOW // KCH + 1), 0)
    park(sbuf_ref, scores(ks_ref, 0))
    park(wbuf_ref, scores(kw_ref, lo_w))
    lax.fori_loop(0, n_chunks - 1, sel_step, 0)
    sel_softmax(n_chunks - 1, True)
    lax.fori_loop(lo_w, n_chunks - 1, win_step, 0)
    win_softmax(n_chunks - 1)

    outs = []
    for g in groups:
        o_s = accs_ref[g, 0:HEAD_DIM, :] * (1.0 / jnp.maximum(accs_ref[g, HEAD_DIM:HEAD_DIM + 1, :], 1e-30))
        o_w = accw_ref[g, 0:HEAD_DIM, :] * (1.0 / jnp.maximum(accw_ref[g, HEAD_DIM:HEAD_DIM + 1, :], 1e-30))
        gates = gates_ref[g]
        o_t = gates[0:1] * oc_ref[g] + gates[1:2] * o_s + gates[2:3] * o_w
        blocks = jnp.where(same_blk, _tile4(o_t), 0.0).astype(BF16)
        outs.append(_dot_nt(unit, blocks))
    for g in groups:
        o_ref[g] = outs[g].astype(o_ref.dtype)


def _nsa(zq, qgain, cmp, kn, vt, ng, *, B, S):
    T = B * S
    nq = S // Q_BLK
    nch = S // KCH
    ncmp = S // CMP_STRIDE
    nsel = S // SEL_LEN
    tok = lambda b, i: (0, b * nq + i, 0)
    once = dict(pipeline_mode=pl.Buffered(1))
    return pl.pallas_call(
        functools.partial(_nsa_kernel, nsel=nsel),
        grid=(B, nq),
        in_specs=[
            pl.BlockSpec((N_KV, Q_BLK, GW), tok),
            _const_spec((HEAD_DIM, GW)),
            pl.BlockSpec((None, None, N_KV, ncmp, HEAD_DIM), lambda b, i: (0, b, 0, 0, 0)),
            pl.BlockSpec((None, None, N_KV, ncmp, HEAD_DIM), lambda b, i: (1, b, 0, 0, 0)),
            pl.BlockSpec((None, None, 2, S, 128), lambda b, i: (0, b, 0, 0, 0)),
            pl.BlockSpec((None, None, 2, S, 128), lambda b, i: (1, b, 0, 0, 0)),
            pl.BlockSpec((None, None, nch, N_KV, VROWS, KCH), lambda b, i: (0, b, 0, 0, 0, 0)),
            pl.BlockSpec((None, None, nch, N_KV, VROWS, KCH), lambda b, i: (1, b, 0, 0, 0, 0)),
            pl.BlockSpec((Q_BLK, NG_PAD), lambda b, i: (b * nq + i, 0)),
        ],
        out_specs=pl.BlockSpec((N_KV, Q_BLK, GW), tok),
        out_shape=jax.ShapeDtypeStruct((N_KV, T, GW), BF16),
        scratch_shapes=[
            pltpu.VMEM((N_KV, 2 * HEAD_DIM, GW), BF16),
            pltpu.VMEM((N_KV, HEAD_DIM, GW), F32),
            pltpu.VMEM((nsel, GW), F32),
            pltpu.VMEM((N_KV, nsel, GW), F32),
            pltpu.VMEM((N_KV, 1, GW), F32),
            pltpu.VMEM((N_KV, VROWS, GW), F32),
            pltpu.VMEM((N_KV, 1, GW), F32),
            pltpu.VMEM((N_KV, VROWS, GW), F32),
            pltpu.VMEM((N_KV, KCH, GW), F32),
            pltpu.VMEM((N_KV, KCH, GW), F32),
            pltpu.VMEM((N_KV, 8, GW), F32),
        ],
        compiler_params=_params(("parallel", "arbitrary")),
        name="nsa",
    )(zq, qgain, cmp, cmp, kn, kn, vt, vt, ng)


def _merge_kernel(x_ref, ya_ref, o_ref_in, mg_ref, wn_ref, wo_ref, out_ref):
    yb = _dot(o_ref_in[0], wn_ref[0:GW, :])
    for g in range(1, N_KV):
        yb = yb + _dot(o_ref_in[g], wn_ref[g * GW:(g + 1) * GW, :])
    mixed = (mg_ref[:, 0:D_MODEL].astype(F32) * ya_ref[...]
             + mg_ref[:, D_MODEL:2 * D_MODEL].astype(F32) * yb)
    out_ref[...] = x_ref[...] + _dot(mixed.astype(BF16), wo_ref[...])


def _merge(x2d, ya_tm, o, mg, wn, wo, *, B, S, tm=512):
    ni = S // tm
    tok = lambda b, i: (b * ni + i, 0)
    return pl.pallas_call(
        _merge_kernel,
        grid=(B, ni),
        in_specs=[
            pl.BlockSpec((tm, D_MODEL), tok),
            pl.BlockSpec((tm, D_MODEL), lambda b, i: (i, b)),
            pl.BlockSpec((N_KV, tm, GW), lambda b, i: (0, b * ni + i, 0)),
            pl.BlockSpec((tm, 2 * D_MODEL), tok),
            _const_spec((N_HEADS * HEAD_DIM, D_MODEL)),
            _const_spec((D_MODEL, D_MODEL)),
        ],
        out_specs=pl.BlockSpec((tm, D_MODEL), tok),
        out_shape=jax.ShapeDtypeStruct((B * S, D_MODEL), F32),
        compiler_params=_params(("parallel", "parallel")),
        name="merge",
    )(x2d, ya_tm, o, mg, wn, wo)


def _ffn_kernel(x_ref, g_ref, wg_ref, wu_ref, wd_ref, o_ref, h_ref, *, tf):
    x = x_ref[...]
    y = x * lax.rsqrt(jnp.mean(x * x, axis=-1, keepdims=True) + EPS) * g_ref[...]
    h_ref[...] = y.astype(BF16)
    o_ref[...] = x
    for j in range(D_FF // tf):
        h = h_ref[...]
        sl = slice(j * tf, (j + 1) * tf)
        act = jax.nn.silu(_dot(h, wg_ref[:, sl])) * _dot(h, wu_ref[:, sl])
        o_ref[...] += _dot(act.astype(BF16), wd_ref[sl, :])


def _ffn(x2d, gain, wg, wu, wd, *, tm=512, tf=256):
    T = x2d.shape[0]
    assert T % tm == 0 and D_FF % tf == 0
    once = dict(pipeline_mode=pl.Buffered(1))
    return pl.pallas_call(
        functools.partial(_ffn_kernel, tf=tf),
        grid=(T // tm,),
        in_specs=[
            pl.BlockSpec((tm, D_MODEL), lambda i: (i, 0)),
            _const_spec((1, D_MODEL)),
            pl.BlockSpec((D_MODEL, D_FF), lambda i: (0, 0), **once),
            pl.BlockSpec((D_MODEL, D_FF), lambda i: (0, 0), **once),
            pl.BlockSpec((D_FF, D_MODEL), lambda i: (0, 0), **once),
        ],
        out_specs=pl.BlockSpec((tm, D_MODEL), lambda i: (i, 0)),
        out_shape=jax.ShapeDtypeStruct((T, D_MODEL), F32),
        scratch_shapes=[pltpu.VMEM((tm, D_MODEL), BF16)],
        compiler_params=_params(("parallel",)),
        name="ffn",
    )(x2d, gain, wg, wu, wd)


def _layer(x, p):
    B, S, _ = x.shape
    T = B * S
    dh = HEAD_DIM
    row = lambda v: v.reshape(1, -1)
    w_in = p["w_in"]
    n_ng = 3 * N_HEADS
    o_ng = 2 * D_RNN + N_HEADS * dh + 6 * KVW
    w_packed = jnp.concatenate(
        [w_in[:, :o_ng], jnp.pad(w_in[:, o_ng:o_ng + n_ng], ((0, 0), (0, NG_PAD - n_ng))), w_in[:, o_ng + n_ng:]],
        axis=1).astype(BF16)
    rx_tm, gy_tm, zq, zkv, ng, mg = _inproj(x, row(p["norm1"]), w_packed)

    ya_tm = _rglru(rx_tm.reshape(S * B, D_RNN), gy_tm.reshape(S * B, D_RNN), p["conv_w"], row(p["conv_b"]),
                   p["rg_wa"].astype(BF16), row(p["rg_ba"]), p["rg_wi"].astype(BF16), row(p["rg_bi"]),
                   row(p["rg_lambda"]), p["w_rg_out"].astype(BF16), nb=B)
    ya_tm = ya_tm.reshape(S, B * D_MODEL)

    kgains = jnp.tile(p["k_norm"][1:3], (1, N_KV)).reshape(2, 1, KVW)
    kn, vt = _kvprep(zkv, kgains, B=B, S=S)
    pos = jnp.stack([p["cmp_pos_k"], p["cmp_pos_v"]]).reshape(2, CMP_STRIDE, 2 * dh)
    w1 = jnp.stack([p["cmp_k_w1"], p["cmp_v_w1"]]).astype(BF16)
    w2 = jnp.stack([p["cmp_k_w2"], p["cmp_v_w2"]]).astype(BF16)
    cmp = _compress(zkv, pos, w1, w2, row(p["k_norm"][0]), B=B, S=S)
    qgain = jnp.broadcast_to((p["q_norm"] * dh ** -0.5)[:, None], (dh, GW))
    o = _nsa(zq, qgain, cmp, kn, vt, ng, B=B, S=S)

    x1 = _merge(x.reshape(T, D_MODEL), ya_tm, o, mg, p["w_nsa_out"].astype(BF16), p["w_o"].astype(BF16), B=B, S=S)
    x2 = _ffn(x1, row(p["norm2"]), p["w_gate"].astype(BF16), p["w_up"].astype(BF16), p["w_down"].astype(BF16))
    return x2.reshape(B, S, D_MODEL)


def kernel(x, norm1, w_in, conv_w, conv_b, rg_wa, rg_ba, rg_wi, rg_bi, rg_lambda, q_norm, k_norm, cmp_pos_k,
           cmp_pos_v, cmp_k_w1, cmp_k_w2, cmp_v_w1, cmp_v_w2, w_rg_out, w_nsa_out, w_o, norm2, w_gate, w_up, w_down):
    params = dict(norm1=norm1, w_in=w_in, conv_w=conv_w, conv_b=conv_b, rg_wa=rg_wa, rg_ba=rg_ba, rg_wi=rg_wi,
                  rg_bi=rg_bi, rg_lambda=rg_lambda, q_norm=q_norm, k_norm=k_norm, cmp_pos_k=cmp_pos_k,
                  cmp_pos_v=cmp_pos_v, cmp_k_w1=cmp_k_w1, cmp_k_w2=cmp_k_w2, cmp_v_w1=cmp_v_w1, cmp_v_w2=cmp_v_w2,
                  w_rg_out=w_rg_out, w_nsa_out=w_nsa_out, w_o=w_o, norm2=norm2, w_gate=w_gate, w_up=w_up,
                  w_down=w_down)
    for l in range(norm1.shape[0]):
        x = _layer(x, {k: v[l] for k, v in params.items()})
    return x
```

```python
import functools

import jax
import jax.numpy as jnp
from jax import lax
from jax.experimental import pallas as pl
from jax.experimental.pallas import tpu as pltpu

D_MODEL = 1024
D_RNN = 1024
RG_BLOCKS = 4
RG_BW = D_RNN // RG_BLOCKS
CONV_W = 4
RG_C = 8.0
N_HEADS = 16
N_KV = 4
HEAD_DIM = 64
HPG = N_HEADS // N_KV
CMP_LEN = 32
CMP_STRIDE = 16
CMP_HID = 256
SEL_LEN = 64
SEL_TOPN = 16
WINDOW = 512
Q_BLK = 64
D_FF = 2816
EPS = 1e-6
FORCE_SCORE = 1e6
NEG = -1e30
LOG2E = 1.4426950408889634

KCH = 256
GW = HPG * HEAD_DIM
KVW = N_KV * HEAD_DIM
VROWS = 80
NG_PAD = 128

F32 = jnp.float32
BF16 = jnp.bfloat16

VMEM_LIMIT = 56 * 1024 * 1024


def _params(sem):
    return pltpu.CompilerParams(dimension_semantics=sem, vmem_limit_bytes=VMEM_LIMIT)


def _dot(a, b):
    return jnp.dot(a, b, preferred_element_type=F32)


def _dot_nt(a, b):
    return lax.dot_general(a, b, (((1,), (1,)), ((), ())), preferred_element_type=F32)


def _split(x):
    hi = x.astype(BF16)
    lo = (x - hi.astype(F32)).astype(BF16)
    return hi, lo


def _iota(shape, dim):
    return lax.broadcasted_iota(jnp.int32, shape, dim)


def _const_spec(shape):
    return pl.BlockSpec(shape, lambda *_: (0,) * len(shape))


_O_RY = D_RNN
_O_Q = 2 * D_RNN
_O_KV = _O_Q + N_HEADS * HEAD_DIM
_O_NG = _O_KV + 6 * KVW
_O_MG = _O_NG + NG_PAD
_W_COLS = _O_MG + 2 * D_MODEL


def _inproj_kernel(x_ref, g_ref, w_ref, rx_ref, gy_ref, zq_ref, zkv_ref, ng_ref, mg_ref, h_ref):
    x = x_ref[...]
    y = x * lax.rsqrt(jnp.mean(x * x, axis=-1, keepdims=True) + EPS) * g_ref[...]
    h_ref[...] = y.astype(BF16)

    def proj(c0, width):
        return _dot(h_ref[...], w_ref[:, c0:c0 + width])

    cw = 256
    for c in range(D_RNN // cw):
        rx_ref[:, c * cw:(c + 1) * cw] = proj(c * cw, cw)
        gy_ref[:, c * cw:(c + 1) * cw] = jax.nn.gelu(proj(_O_RY + c * cw, cw)).astype(gy_ref.dtype)
    for g in range(N_KV):
        zq_ref[g] = proj(_O_Q + g * GW, GW)
    for s in range(6):
        z = proj(_O_KV + s * KVW, KVW)
        for pr in range(KVW // 128):
            zkv_ref[s, pr] = z[:, pr * 128:(pr + 1) * 128]
    ng_ref[...] = jax.nn.sigmoid(proj(_O_NG, NG_PAD))
    for c in range(2 * D_MODEL // cw):
        mg_ref[:, c * cw:(c + 1) * cw] = jax.nn.sigmoid(proj(_O_MG + c * cw, cw)).astype(mg_ref.dtype)


def _inproj(x, gain, w, *, tm=256):
    B, S, D = x.shape
    T = B * S
    ni = S // tm
    tok = lambda b, i: (b * ni + i, 0)
    return pl.pallas_call(
        _inproj_kernel,
        grid=(B, ni),
        in_specs=[
            pl.BlockSpec((None, tm, D), lambda b, i: (b, i, 0)),
            _const_spec((1, D)),
            pl.BlockSpec((D, _W_COLS), lambda b, i: (0, 0), pipeline_mode=pl.Buffered(1)),
        ],
        out_specs=[
            pl.BlockSpec((tm, D_RNN), lambda b, i: (i, b)),
            pl.BlockSpec((tm, D_RNN), lambda b, i: (i, b)),
            pl.BlockSpec((N_KV, tm, GW), lambda b, i: (0, b * ni + i, 0)),
            pl.BlockSpec((6, KVW // 128, tm, 128), lambda b, i: (0, 0, b * ni + i, 0)),
            pl.BlockSpec((tm, NG_PAD), tok),
            pl.BlockSpec((tm, 2 * D_MODEL), tok),
        ],
        out_shape=[
            jax.ShapeDtypeStruct((S, B * D_RNN), F32),
            jax.ShapeDtypeStruct((S, B * D_RNN), BF16),
            jax.ShapeDtypeStruct((N_KV, T, GW), F32),
            jax.ShapeDtypeStruct((6, KVW // 128, T, 128), F32),
            jax.ShapeDtypeStruct((T, NG_PAD), F32),
            jax.ShapeDtypeStruct((T, 2 * D_MODEL), BF16),
        ],
        scratch_shapes=[pltpu.VMEM((tm, D), BF16)],
        compiler_params=_params(("parallel", "parallel")),
        name="inproj",
    )(x, gain, w)


def _rglru_kernel(x_ref, gy_ref, cw_ref, cb_ref, wa_ref, ba_ref, wi_ref, bi_ref, lam_ref, wo_ref,
                  o_ref, xb_ref, a_ref, u_ref, hb_ref, hs_ref, *, nb, tt):
    R = nb * tt
    halo = (CONV_W - 1) * nb
    step = pl.program_id(0)

    @pl.when(step == 0)
    def _():
        xb_ref[0:halo, :] = jnp.zeros((halo, D_RNN), F32)
        hs_ref[...] = jnp.zeros((nb, D_RNN), F32)

    xb_ref[halo:halo + R, :] = x_ref[...]
    xr = cb_ref[...] + cw_ref[0:1, :] * xb_ref[0:R, :]
    for k in range(1, CONV_W):
        xr = xr + cw_ref[k:k + 1, :] * xb_ref[k * nb:k * nb + R, :]
    tail = xb_ref[R:R + halo, :]
    xb_ref[0:halo, :] = tail

    xrb = xr.astype(BF16)
    sp = jax.nn.softplus(-lam_ref[...])
    row = _iota((R, RG_BW), 0)
    first = (row < nb) & (step == 0)
    for n in range(RG_BLOCKS):
        sl = slice(n * RG_BW, (n + 1) * RG_BW)
        xn = xrb[:, sl]
        r = jax.nn.sigmoid(_dot(xn, wa_ref[n]) + ba_ref[:, sl])
        ig = jax.nn.sigmoid(_dot(xn, wi_ref[n]) + bi_ref[:, sl])
        log_a = (-RG_C) * r * sp[:, sl]
        a = jnp.exp(log_a)
        mult = jnp.sqrt(1.0 - jnp.exp(2.0 * log_a))
        mult = jnp.where(first, 1.0, mult)
        a_ref[:, sl] = a
        u_ref[:, sl] = mult * ig * xr[:, sl]

    def body(t, h):
        r0 = pl.multiple_of(t * nb, nb)
        h = a_ref[pl.ds(r0, nb), :] * h + u_ref[pl.ds(r0, nb), :]
        hb_ref[pl.ds(r0, nb), :] = h
        return h

    h = lax.fori_loop(0, tt, body, hs_ref[...], unroll=8)
    hs_ref[...] = h
    o_ref[...] = _dot((hb_ref[...] * gy_ref[...].astype(F32)).astype(BF16), wo_ref[...])


def _rglru(rx_tm, gy_tm, conv_w, conv_b, wa, ba, wi, bi, lam, wo, *, nb, tt=32):
    rows = rx_tm.shape[0]
    S = rows // nb
    assert S % tt == 0
    R = nb * tt
    halo = (CONV_W - 1) * nb
    return pl.pallas_call(
        functools.partial(_rglru_kernel, nb=nb, tt=tt),
        grid=(S // tt,),
        in_specs=[
            pl.BlockSpec((R, D_RNN), lambda s: (s, 0)),
            pl.BlockSpec((R, D_RNN), lambda s: (s, 0)),
            _const_spec((CONV_W, D_RNN)),
            _const_spec((1, D_RNN)),
            _const_spec((RG_BLOCKS, RG_BW, RG_BW)),
            _const_spec((1, D_RNN)),
            _const_spec((RG_BLOCKS, RG_BW, RG_BW)),
            _const_spec((1, D_RNN)),
            _const_spec((1, D_RNN)),
            _const_spec((D_RNN, D_MODEL)),
        ],
        out_specs=pl.BlockSpec((R, D_MODEL), lambda s: (s, 0)),
        out_shape=jax.ShapeDtypeStruct((rows, D_MODEL), F32),
        scratch_shapes=[
            pltpu.VMEM((R + halo, D_RNN), F32),
            pltpu.VMEM((R, D_RNN), F32),
            pltpu.VMEM((R, D_RNN), F32),
            pltpu.VMEM((R, D_RNN), F32),
            pltpu.VMEM((nb, D_RNN), F32),
        ],
        compiler_params=_params(("arbitrary",)),
        name="rglru",
    )(rx_tm, gy_tm, conv_w, conv_b, wa, ba, wi, bi, lam, wo)


def _kvprep_kernel(zk_ref, zv_ref, g_ref, kn_ref, vt_ref):
    x = jnp.concatenate([zk_ref[0], zk_ref[1]], axis=1)
    v = jnp.concatenate([zv_ref[0], zv_ref[1]], axis=1)
    seg = jnp.where(_iota((KVW, KVW), 0) // HEAD_DIM == _iota((KVW, KVW), 1) // HEAD_DIM, 1.0, 0.0).astype(BF16)
    hi, lo = _split(x * x)
    ssq = _dot(hi, seg) + _dot(lo, seg)
    kn = (x * lax.rsqrt(ssq * (1.0 / HEAD_DIM) + EPS) * g_ref[...]).astype(BF16)
    for pr in range(KVW // 128):
        kn_ref[pr] = kn[:, pr * 128:(pr + 1) * 128]
    eye = jnp.where(_iota((KVW, KVW), 0) == _iota((KVW, KVW), 1), 1.0, 0.0).astype(BF16)
    vt = _dot_nt(eye, v.astype(BF16)).astype(BF16)
    pad = jnp.where(_iota((VROWS - HEAD_DIM, KCH), 0) == 0, 1.0, 0.0).astype(BF16)
    for g in range(N_KV):
        vt_ref[g, 0:HEAD_DIM, :] = vt[g * HEAD_DIM:(g + 1) * HEAD_DIM]
        vt_ref[g, HEAD_DIM:VROWS, :] = pad


def _kvprep(zkv, gains, *, B, S):
    nch = S // KCH
    return pl.pallas_call(
        _kvprep_kernel,
        grid=(2, B, nch),
        in_specs=[
            pl.BlockSpec((None, KVW // 128, KCH, 128), lambda w, b, i: (2 + 2 * w, 0, b * nch + i, 0)),
            pl.BlockSpec((None, KVW // 128, KCH, 128), lambda w, b, i: (3 + 2 * w, 0, b * nch + i, 0)),
            pl.BlockSpec((None, 1, KVW), lambda w, b, i: (w, 0, 0)),
        ],
        out_specs=[
            pl.BlockSpec((None, None, KVW // 128, KCH, 128), lambda w, b, i: (w, b, 0, i, 0)),
            pl.BlockSpec((None, None, None, N_KV, VROWS, KCH), lambda w, b, i: (w, b, i, 0, 0, 0)),
        ],
        out_shape=[
            jax.ShapeDtypeStruct((2, B, KVW // 128, S, 128), BF16),
            jax.ShapeDtypeStruct((2, B, nch, N_KV, VROWS, KCH), BF16),
        ],
        compiler_params=_params(("parallel", "parallel", "parallel")),
        name="kvprep",
    )(zkv, zkv, gains)


def _compress_kernel(z_ref, pos_ref, w1_ref, w2_ref, g_ref, o_ref):
    kind = pl.program_id(0)
    nch = z_ref.shape[0] // CMP_STRIDE
    lane = _iota((nch, 128), 1)
    npair = CMP_STRIDE // 2
    for ge in range(2):
        halves = []
        for half in range(2):
            acc = None
            for p in range(npair):
                a = z_ref[pl.ds(2 * p, nch, stride=CMP_STRIDE), :]
                b = z_ref[pl.ds(2 * p + 1, nch, stride=CMP_STRIDE), :]
                if ge == 0:
                    comb = jnp.where(lane < HEAD_DIM, a, pltpu.roll(b, HEAD_DIM, axis=1))
                else:
                    comb = jnp.where(lane < HEAD_DIM, pltpu.roll(a, HEAD_DIM, axis=1), b)
                pr = half * npair + p
                lhs = (comb + pos_ref[pr:pr + 1, :]).astype(BF16)
                d = _dot(lhs, w1_ref[pr * 128:(pr + 1) * 128, :])
                acc = d if acc is None else acc + d
            halves.append(acc)
        hid = halves[0] + pltpu.roll(halves[1], nch - 1, axis=0)
        hid = jax.nn.gelu(hid)
        out = _dot(hid.astype(BF16), w2_ref[...])
        normed = out * lax.rsqrt(jnp.mean(out * out, axis=-1, keepdims=True) + EPS) * g_ref[...]
        res = jnp.where(kind == 0, normed, out)
        rowi = _iota(res.shape, 0)
        o_ref[ge] = jnp.where(rowi < nch - 1, res, 0.0).astype(o_ref.dtype)


def _compress(zkv, pos, w1, w2, gain, *, B, S):
    nch = S // CMP_STRIDE
    return pl.pallas_call(
        _compress_kernel,
        grid=(2, B, KVW // 128),
        in_specs=[
            pl.BlockSpec((None, None, S, 128), lambda k, b, lp: (k, lp, b, 0)),
            pl.BlockSpec((None, CMP_STRIDE, 128), lambda k, b, lp: (k, 0, 0)),
            pl.BlockSpec((None, CMP_LEN * HEAD_DIM, CMP_HID), lambda k, b, lp: (k, 0, 0)),
            pl.BlockSpec((None, CMP_HID, HEAD_DIM), lambda k, b, lp: (k, 0, 0)),
            _const_spec((1, HEAD_DIM)),
        ],
        out_specs=pl.BlockSpec((None, None, 2, nch, HEAD_DIM), lambda k, b, lp: (k, b, lp, 0, 0)),
        out_shape=jax.ShapeDtypeStruct((2, B, N_KV, nch, HEAD_DIM), BF16),
        compiler_params=_params(("parallel", "parallel", "parallel")),
        name="compress",
    )(zkv, pos, w1, w2, gain)


def _tile4(x):
    return jnp.concatenate([x, x, x, x], axis=0)


def _spread_heads(x):
    y = x + pltpu.roll(x, 2 * Q_BLK, axis=1)
    return y + pltpu.roll(y, Q_BLK, axis=1)


def _nsa_kernel(zq_ref, qg_ref, kc_ref, vc_ref, ks_ref, kw_ref, vst_ref, vwt_ref, ng_ref, o_ref,
                qtp_ref, oc_ref, score_ref, bias_ref, ms_ref, accs_ref, mw_ref, accw_ref, sbuf_ref, wbuf_ref,
                gates_ref, *, nsel):
    qi = pl.program_id(1)
    s0 = qi * Q_BLK
    cur = qi
    ncmp = kc_ref.shape[1]
    per_chunk = KCH // SEL_LEN

    lane_t = s0 + (_iota((1, GW), 1) % Q_BLK)
    lane_grp = _iota((1, GW), 1) // Q_BLK
    unit = jnp.where(_iota((Q_BLK, GW), 1) % Q_BLK == _iota((Q_BLK, GW), 0), 1.0, 0.0).astype(BF16)
    same_blk = _iota((GW, GW), 0) // Q_BLK == _iota((GW, GW), 1) // Q_BLK
    eye = jnp.where(_iota((HEAD_DIM, HEAD_DIM), 0) == _iota((HEAD_DIM, HEAD_DIM), 1), 1.0, 0.0).astype(BF16)
    ov_n = _iota((nsel, ncmp), 0) * SEL_LEN
    ov_c = _iota((nsel, ncmp), 1) * CMP_STRIDE
    overlap = jnp.where((ov_c < ov_n + SEL_LEN) & (ov_c + CMP_LEN > ov_n), 1.0, 0.0).astype(BF16)


    ok_c = _iota((ncmp, GW), 0) * CMP_STRIDE + (CMP_LEN - 1) <= lane_t
    groups = range(N_KV)
    ng_rows = _iota((GW, NG_PAD), 0) // Q_BLK
    ng_cols = _iota((GW, NG_PAD), 1)
    branch_of_col = jnp.where(_iota((8, NG_PAD), 1) // N_HEADS == _iota((8, NG_PAD), 0), 1.0, 0.0).astype(BF16)
    ng4 = _tile4(ng_ref[...])
    for g in groups:
        pick = (ng_cols % HPG == ng_rows) & ((ng_cols // HPG) % N_KV == g)
        g_hi, g_lo = _split(jnp.where(pick, ng4, 0.0))
        gates_ref[g] = _dot_nt(branch_of_col, g_hi) + _dot_nt(branch_of_col, g_lo)

    seg = jnp.where(same_blk, 1.0, 0.0).astype(BF16)
    zq = [zq_ref[g] for g in groups]
    sq_hi, sq_lo = _split(jnp.concatenate([z * z for z in zq], axis=0))
    ssq = _dot(jnp.concatenate([sq_hi, sq_lo], axis=0), seg)
    half = N_KV * Q_BLK
    inv_rms = lax.rsqrt((ssq[0:half] + ssq[half:2 * half]) * (1.0 / HEAD_DIM) + EPS)
    qts, vcts = [], []
    for g in groups:
        qn = (zq[g] * inv_rms[g * Q_BLK:(g + 1) * Q_BLK] * qg_ref[...]).astype(BF16)
        blocks = jnp.where(same_blk, _tile4(qn), jnp.zeros((GW, GW), BF16))
        qts.append(_dot_nt(unit, blocks).astype(BF16))
        vcts.append(_dot_nt(eye, vc_ref[g]).astype(BF16))
    scs = []
    for g in groups:
        off = (g % 2) * HEAD_DIM
        qtp_ref[g, off:off + HEAD_DIM, :] = qts[g]
        qtp_ref[g, HEAD_DIM - off:2 * HEAD_DIM - off, :] = jnp.zeros((HEAD_DIM, GW), BF16)
        scs.append(_dot(kc_ref[g], qts[g]))
    imp_all = jnp.zeros((nsel, GW), F32)
    some_c = lane_t >= CMP_LEN - 1
    for g in groups:
        scm = jnp.where(ok_c, scs[g], NEG)
        ec = jnp.exp2(scm - jnp.max(scm, axis=0, keepdims=True))
        inv = jnp.where(some_c, 1.0 / jnp.maximum(jnp.sum(ec, axis=0, keepdims=True), 1e-30), 0.0)
        pc = ec * inv
        p_hi, p_lo = _split(pc)
        both = _dot(jnp.concatenate([vcts[g], overlap], axis=0), p_hi)
        oc_ref[g] = both[0:HEAD_DIM]
        imp = _spread_heads(both[HEAD_DIM:HEAD_DIM + nsel] + _dot(overlap, p_lo))
        imp_all = jnp.where(lane_grp == g, imp, imp_all)

    n_io = _iota((nsel, GW), 0)
    forced = (n_io == 0) | (n_io == cur) | (n_io == cur - 1)
    valid = n_io <= cur
    score = jnp.where(forced, FORCE_SCORE, imp_all)
    score = jnp.where(valid, score, -1.0)
    score_ref[...] = score

    def rank_step(m, rank):
        rowv = score_ref[pl.ds(m, 1), :]
        ge = jnp.where(rowv >= score, 1.0, 0.0)
        gt = jnp.where(rowv > score, 1.0, 0.0)
        return rank + jnp.where(n_io > m, ge, gt)

    rank = lax.fori_loop(0, cur + 1, rank_step, jnp.zeros((nsel, GW), F32))
    sel_all = jnp.where((rank < float(SEL_TOPN)) & valid, 1.0, 0.0)

    key_row = _iota((SEL_LEN, GW), 0)
    no_shift = jnp.zeros((1, GW), F32)
    for g in range(N_KV):
        mine = jnp.where(lane_grp == g, sel_all, 0.0)
        bias_ref[g] = (_spread_heads(mine) - 1.0) * (-NEG)
    ms_ref[...] = jnp.full(ms_ref.shape, NEG, F32)
    mw_ref[...] = jnp.full(mw_ref.shape, NEG, F32)
    accs_ref[...] = jnp.zeros(accs_ref.shape, F32)
    accw_ref[...] = jnp.zeros(accw_ref.shape, F32)

    def update(g, blocks, shifts, vt, m_ref, acc_ref):
        m = m_ref[g]
        mnew = m
        for blk, sh in zip(blocks, shifts):
            mnew = jnp.maximum(mnew, jnp.max(blk, axis=0, keepdims=True) + sh)
        e = jnp.concatenate([jnp.exp2(blk + (sh - mnew)) for blk, sh in zip(blocks, shifts)], axis=0).astype(BF16)
        acc_ref[g] = jnp.exp2(m - mnew) * acc_ref[g] + _dot(vt, e)
        m_ref[g] = mnew

    def scores(k_ref, i):
        k0 = pl.multiple_of(i * KCH, KCH)
        return [_dot(k_ref[g // 2, pl.ds(k0, KCH), :], qtp_ref[g]) for g in range(N_KV)]

    def park(buf_ref, vals):
        for g in range(N_KV):
            buf_ref[g] = vals[g]

    def slabs(buf_ref, g, keep):
        out = []
        for nb in range(per_chunk):
            blk = buf_ref[g, nb * SEL_LEN:(nb + 1) * SEL_LEN, :]
            out.append(blk if keep is None else jnp.where(keep(nb * SEL_LEN), blk, NEG))
        return out

    def sel_softmax(i, last):
        keep = (lambda k0: key_row + (i * KCH + k0) <= lane_t) if last else None
        for g in groups:
            shifts = [bias_ref[g, pl.ds(i * per_chunk + nb, 1), :] for nb in range(per_chunk)]
            update(g, slabs(sbuf_ref, g, keep), shifts, vst_ref[i, g], ms_ref, accs_ref)

    def win_softmax(i, last):
        if last:
            keep = lambda k0: key_row + (i * KCH + k0) <= lane_t
        else:
            keep = lambda k0: key_row + (i * KCH + k0) > lane_t - WINDOW
        for g in groups:
            update(g, slabs(wbuf_ref, g, keep), [no_shift] * per_chunk, vwt_ref[i, g], mw_ref, accw_ref)

    def sel_step(i, carry):
        nxt = scores(ks_ref, i + 1)
        sel_softmax(i, False)
        park(sbuf_ref, nxt)
        return carry

    def win_step(i, carry):
        nxt = scores(kw_ref, i + 1)
        win_softmax(i, False)
        park(wbuf_ref, nxt)
        return carry

    n_chunks = qi // per_chunk + 1
    lo_w = jnp.maximum(n_chunks - (WINDOW // KCH + 1), 0)
    park(sbuf_ref, scores(ks_ref, 0))
    park(wbuf_ref, scores(kw_ref, lo_w))
    lax.fori_loop(0, n_chunks - 1, sel_step, 0)
    sel_softmax(n_chunks - 1, True)
    lax.fori_loop(lo_w, n_chunks - 1, win_step, 0)
    win_softmax(n_chunks - 1, True)

    outs = []
    for g in groups:
        o_s = accs_ref[g, 0:HEAD_DIM, :] * (1.0 / jnp.maximum(accs_ref[g, HEAD_DIM:HEAD_DIM + 1, :], 1e-30))
        o_w = accw_ref[g, 0:HEAD_DIM, :] * (1.0 / jnp.maximum(accw_ref[g, HEAD_DIM:HEAD_DIM + 1, :], 1e-30))
        gates = gates_ref[g]
        o_t = gates[0:1] * oc_ref[g] + gates[1:2] * o_s + gates[2:3] * o_w
        blocks = jnp.where(same_blk, _tile4(o_t), 0.0).astype(BF16)
        outs.append(_dot_nt(unit, blocks))
    for g in groups:
        o_ref[g] = outs[g].astype(o_ref.dtype)


def _nsa(zq, qgain, cmp, kn, vt, ng, *, B, S):
    T = B * S
    nq = S // Q_BLK
    nch = S // KCH
    ncmp = S // CMP_STRIDE
    nsel = S // SEL_LEN
    tok = lambda b, i: (0, b * nq + i, 0)
    once = dict(pipeline_mode=pl.Buffered(1))
    return pl.pallas_call(
        functools.partial(_nsa_kernel, nsel=nsel),
        grid=(B, nq),
        in_specs=[
            pl.BlockSpec((N_KV, Q_BLK, GW), tok),
            _const_spec((1, GW)),
            pl.BlockSpec((None, None, N_KV, ncmp, HEAD_DIM), lambda b, i: (0, b, 0, 0, 0)),
            pl.BlockSpec((None, None, N_KV, ncmp, HEAD_DIM), lambda b, i: (1, b, 0, 0, 0)),
            pl.BlockSpec((None, None, 2, S, 128), lambda b, i: (0, b, 0, 0, 0)),
            pl.BlockSpec((None, None, 2, S, 128), lambda b, i: (1, b, 0, 0, 0)),
            pl.BlockSpec((None, None, nch, N_KV, VROWS, KCH), lambda b, i: (0, b, 0, 0, 0, 0)),
            pl.BlockSpec((None, None, nch, N_KV, VROWS, KCH), lambda b, i: (1, b, 0, 0, 0, 0)),
            pl.BlockSpec((Q_BLK, NG_PAD), lambda b, i: (b * nq + i, 0)),
        ],
        out_specs=pl.BlockSpec((N_KV, Q_BLK, GW), tok),
        out_shape=jax.ShapeDtypeStruct((N_KV, T, GW), BF16),
        scratch_shapes=[
            pltpu.VMEM((N_KV, 2 * HEAD_DIM, GW), BF16),
            pltpu.VMEM((N_KV, HEAD_DIM, GW), F32),
            pltpu.VMEM((nsel, GW), F32),
            pltpu.VMEM((N_KV, nsel, GW), F32),
            pltpu.VMEM((N_KV, 1, GW), F32),
            pltpu.VMEM((N_KV, VROWS, GW), F32),
            pltpu.VMEM((N_KV, 1, GW), F32),
            pltpu.VMEM((N_KV, VROWS, GW), F32),
            pltpu.VMEM((N_KV, KCH, GW), F32),
            pltpu.VMEM((N_KV, KCH, GW), F32),
            pltpu.VMEM((N_KV, 8, GW), F32),
        ],
        compiler_params=_params(("parallel", "arbitrary")),
        name="nsa",
    )(zq, qgain, cmp, cmp, kn, kn, vt, vt, ng)


def _merge_kernel(x_ref, ya_ref, o_ref_in, mg_ref, wn_ref, wo_ref, out_ref):
    yb = _dot(o_ref_in[0], wn_ref[0:GW, :])
    for g in range(1, N_KV):
        yb = yb + _dot(o_ref_in[g], wn_ref[g * GW:(g + 1) * GW, :])
    mixed = (mg_ref[:, 0:D_MODEL].astype(F32) * ya_ref[...]
             + mg_ref[:, D_MODEL:2 * D_MODEL].astype(F32) * yb)
    out_ref[...] = x_ref[...] + _dot(mixed.astype(BF16), wo_ref[...])


def _merge(x2d, ya_tm, o, mg, wn, wo, *, B, S, tm=512):
    ni = S // tm
    tok = lambda b, i: (b * ni + i, 0)
    return pl.pallas_call(
        _merge_kernel,
        grid=(B, ni),
        in_specs=[
            pl.BlockSpec((tm, D_MODEL), tok),
            pl.BlockSpec((tm, D_MODEL), lambda b, i: (i, b)),
            pl.BlockSpec((N_KV, tm, GW), lambda b, i: (0, b * ni + i, 0)),
            pl.BlockSpec((tm, 2 * D_MODEL), tok),
            _const_spec((N_HEADS * HEAD_DIM, D_MODEL)),
            _const_spec((D_MODEL, D_MODEL)),
        ],
        out_specs=pl.BlockSpec((tm, D_MODEL), tok),
        out_shape=jax.ShapeDtypeStruct((B * S, D_MODEL), F32),
        compiler_params=_params(("parallel", "parallel")),
        name="merge",
    )(x2d, ya_tm, o, mg, wn, wo)


def _ffn_kernel(x_ref, g_ref, wg_ref, wu_ref, wd_ref, o_ref, h_ref, *, tf):
    x = x_ref[...]
    y = x * lax.rsqrt(jnp.mean(x * x, axis=-1, keepdims=True) + EPS) * g_ref[...]
    h_ref[...] = y.astype(BF16)
    o_ref[...] = x
    for j in range(D_FF // tf):
        h = h_ref[...]
        sl = slice(j * tf, (j + 1) * tf)
        act = jax.nn.silu(_dot(h, wg_ref[:, sl])) * _dot(h, wu_ref[:, sl])
        o_ref[...] += _dot(act.astype(BF16), wd_ref[sl, :])


def _ffn(x2d, gain, wg, wu, wd, *, tm=512, tf=256):
    T = x2d.shape[0]
    assert T % tm == 0 and D_FF % tf == 0
    once = dict(pipeline_mode=pl.Buffered(1))
    return pl.pallas_call(
        functools.partial(_ffn_kernel, tf=tf),
        grid=(T // tm,),
        in_specs=[
            pl.BlockSpec((tm, D_MODEL), lambda i: (i, 0)),
            _const_spec((1, D_MODEL)),
            pl.BlockSpec((D_MODEL, D_FF), lambda i: (0, 0), **once),
            pl.BlockSpec((D_MODEL, D_FF), lambda i: (0, 0), **once),
            pl.BlockSpec((D_FF, D_MODEL), lambda i: (0, 0), **once),
        ],
        out_specs=pl.BlockSpec((tm, D_MODEL), lambda i: (i, 0)),
        out_shape=jax.ShapeDtypeStruct((T, D_MODEL), F32),
        scratch_shapes=[pltpu.VMEM((tm, D_MODEL), BF16)],
        compiler_params=_params(("parallel",)),
        name="ffn",
    )(x2d, gain, wg, wu, wd)


def _layer(x, p):
    B, S, _ = x.shape
    T = B * S
    dh = HEAD_DIM
    row = lambda v: v.reshape(1, -1)
    w_in = p["w_in"]
    n_ng = 3 * N_HEADS
    o_ng = 2 * D_RNN + N_HEADS * dh + 6 * KVW
    w_packed = jnp.concatenate(
        [w_in[:, :o_ng], jnp.pad(w_in[:, o_ng:o_ng + n_ng], ((0, 0), (0, NG_PAD - n_ng))), w_in[:, o_ng + n_ng:]],
        axis=1).astype(BF16)
    rx_tm, gy_tm, zq, zkv, ng, mg = _inproj(x, row(p["norm1"]), w_packed)

    ya_tm = _rglru(rx_tm.reshape(S * B, D_RNN), gy_tm.reshape(S * B, D_RNN), p["conv_w"], row(p["conv_b"]),
                   p["rg_wa"].astype(BF16), row(p["rg_ba"]), p["rg_wi"].astype(BF16), row(p["rg_bi"]),
                   row(p["rg_lambda"]), p["w_rg_out"].astype(BF16), nb=B)
    ya_tm = ya_tm.reshape(S, B * D_MODEL)

    kgains = jnp.tile(p["k_norm"][1:3], (1, N_KV)).reshape(2, 1, KVW)
    kn, vt = _kvprep(zkv, kgains, B=B, S=S)
    pos = jnp.stack([p["cmp_pos_k"], p["cmp_pos_v"]]).reshape(2, CMP_STRIDE, 2 * dh)
    w1 = jnp.stack([p["cmp_k_w1"], p["cmp_v_w1"]]).astype(BF16)
    w2 = jnp.stack([p["cmp_k_w2"], p["cmp_v_w2"]]).astype(BF16)
    cmp = _compress(zkv, pos, w1, w2, row(p["k_norm"][0]), B=B, S=S)
    qgain = jnp.tile(p["q_norm"] * (dh ** -0.5 * LOG2E), HPG).reshape(1, GW)
    o = _nsa(zq, qgain, cmp, kn, vt, ng, B=B, S=S)

    x1 = _merge(x.reshape(T, D_MODEL), ya_tm, o, mg, p["w_nsa_out"].astype(BF16), p["w_o"].astype(BF16), B=B, S=S)
    x2 = _ffn(x1, row(p["norm2"]), p["w_gate"].astype(BF16), p["w_up"].astype(BF16), p["w_down"].astype(BF16))
    return x2.reshape(B, S, D_MODEL)


def kernel(x, norm1, w_in, conv_w, conv_b, rg_wa, rg_ba, rg_wi, rg_bi, rg_lambda, q_norm, k_norm, cmp_pos_k,
           cmp_pos_v, cmp_k_w1, cmp_k_w2, cmp_v_w1, cmp_v_w2, w_rg_out, w_nsa_out, w_o, norm2, w_gate, w_up, w_down):
    params = dict(norm1=norm1, w_in=w_in, conv_w=conv_w, conv_b=conv_b, rg_wa=rg_wa, rg_ba=rg_ba, rg_wi=rg_wi,
                  rg_bi=rg_bi, rg_lambda=rg_lambda, q_norm=q_norm, k_norm=k_norm, cmp_pos_k=cmp_pos_k,
                  cmp_pos_v=cmp_pos_v, cmp_k_w1=cmp_k_w1, cmp_k_w2=cmp_k_w2, cmp_v_w1=cmp_v_w1, cmp_v_w2=cmp_v_w2,
                  w_rg_out=w_rg_out, w_nsa_out=w_nsa_out, w_o=w_o, norm2=norm2, w_gate=w_gate, w_up=w_up,
                  w_down=w_down)
    for l in range(norm1.shape[0]):
        x = _layer(x, {k: v[l] for k, v in params.items()})
    return x
```

```python
import functools

import jax
import jax.numpy as jnp
from jax import lax
from jax.experimental import pallas as pl
from jax.experimental.pallas import tpu as pltpu

D_MODEL = 1024
D_RNN = 1024
RG_BLOCKS = 4
RG_BW = D_RNN // RG_BLOCKS
CONV_W = 4
RG_C = 8.0
N_HEADS = 16
N_KV = 4
HEAD_DIM = 64
HPG = N_HEADS // N_KV
CMP_LEN = 32
CMP_STRIDE = 16
CMP_HID = 256
SEL_LEN = 64
SEL_TOPN = 16
WINDOW = 512
Q_BLK = 64
D_FF = 2816
EPS = 1e-6
FORCE_SCORE = 1e6
NEG = -1e30
LOG2E = 1.4426950408889634

KCH = 256
GW = HPG * HEAD_DIM
KVW = N_KV * HEAD_DIM
VROWS = 80
NG_PAD = 128

F32 = jnp.float32
BF16 = jnp.bfloat16

VMEM_LIMIT = 56 * 1024 * 1024


def _params(sem):
    return pltpu.CompilerParams(dimension_semantics=sem, vmem_limit_bytes=VMEM_LIMIT)


def _dot(a, b):
    return jnp.dot(a, b, preferred_element_type=F32)


def _dot_nt(a, b):
    return lax.dot_general(a, b, (((1,), (1,)), ((), ())), preferred_element_type=F32)


def _split(x):
    hi = x.astype(BF16)
    lo = (x - hi.astype(F32)).astype(BF16)
    return hi, lo


def _iota(shape, dim):
    return lax.broadcasted_iota(jnp.int32, shape, dim)


def _const_spec(shape):
    return pl.BlockSpec(shape, lambda *_: (0,) * len(shape))


_O_RY = D_RNN
_O_Q = 2 * D_RNN
_O_KV = _O_Q + N_HEADS * HEAD_DIM
_O_NG = _O_KV + 6 * KVW
_O_MG = _O_NG + NG_PAD
_W_COLS = _O_MG + 2 * D_MODEL


def _inproj_kernel(x_ref, g_ref, w_ref, rx_ref, gy_ref, zq_ref, zkv_ref, ng_ref, mg_ref, h_ref):
    x = x_ref[...]
    y = x * lax.rsqrt(jnp.mean(x * x, axis=-1, keepdims=True) + EPS) * g_ref[...]
    h_ref[...] = y.astype(BF16)

    def proj(c0, width):
        return _dot(h_ref[...], w_ref[:, c0:c0 + width])

    cw = 256
    for c in range(D_RNN // cw):
        rx_ref[:, c * cw:(c + 1) * cw] = proj(c * cw, cw)
        gy_ref[:, c * cw:(c + 1) * cw] = jax.nn.gelu(proj(_O_RY + c * cw, cw)).astype(gy_ref.dtype)
    for g in range(N_KV):
        zq_ref[g] = proj(_O_Q + g * GW, GW)
    for s in range(6):
        z = proj(_O_KV + s * KVW, KVW)
        for pr in range(KVW // 128):
            zkv_ref[s, pr] = z[:, pr * 128:(pr + 1) * 128]
    ng_ref[...] = jax.nn.sigmoid(proj(_O_NG, NG_PAD))
    for c in range(2 * D_MODEL // cw):
        mg_ref[:, c * cw:(c + 1) * cw] = jax.nn.sigmoid(proj(_O_MG + c * cw, cw)).astype(mg_ref.dtype)


def _inproj(x, gain, w, *, tm=256):
    B, S, D = x.shape
    T = B * S
    ni = S // tm
    tok = lambda b, i: (b * ni + i, 0)
    return pl.pallas_call(
        _inproj_kernel,
        grid=(B, ni),
        in_specs=[
            pl.BlockSpec((None, tm, D), lambda b, i: (b, i, 0)),
            _const_spec((1, D)),
            pl.BlockSpec((D, _W_COLS), lambda b, i: (0, 0), pipeline_mode=pl.Buffered(1)),
        ],
        out_specs=[
            pl.BlockSpec((tm, D_RNN), lambda b, i: (i, b)),
            pl.BlockSpec((tm, D_RNN), lambda b, i: (i, b)),
            pl.BlockSpec((N_KV, tm, GW), lambda b, i: (0, b * ni + i, 0)),
            pl.BlockSpec((6, KVW // 128, tm, 128), lambda b, i: (0, 0, b * ni + i, 0)),
            pl.BlockSpec((tm, NG_PAD), tok),
            pl.BlockSpec((tm, 2 * D_MODEL), tok),
        ],
        out_shape=[
            jax.ShapeDtypeStruct((S, B * D_RNN), F32),
            jax.ShapeDtypeStruct((S, B * D_RNN), BF16),
            jax.ShapeDtypeStruct((N_KV, T, GW), F32),
            jax.ShapeDtypeStruct((6, KVW // 128, T, 128), F32),
            jax.ShapeDtypeStruct((T, NG_PAD), F32),
            jax.ShapeDtypeStruct((T, 2 * D_MODEL), BF16),
        ],
        scratch_shapes=[pltpu.VMEM((tm, D), BF16)],
        compiler_params=_params(("parallel", "parallel")),
        name="inproj",
    )(x, gain, w)


def _rglru_kernel(x_ref, gy_ref, cw_ref, cb_ref, wa_ref, ba_ref, wi_ref, bi_ref, lam_ref, wo_ref,
                  o_ref, xb_ref, a_ref, u_ref, hb_ref, hs_ref, *, nb, tt):
    R = nb * tt
    halo = (CONV_W - 1) * nb
    step = pl.program_id(0)

    @pl.when(step == 0)
    def _():
        xb_ref[0:halo, :] = jnp.zeros((halo, D_RNN), F32)
        hs_ref[...] = jnp.zeros((nb, D_RNN), F32)

    xb_ref[halo:halo + R, :] = x_ref[...]
    xr = cb_ref[...] + cw_ref[0:1, :] * xb_ref[0:R, :]
    for k in range(1, CONV_W):
        xr = xr + cw_ref[k:k + 1, :] * xb_ref[k * nb:k * nb + R, :]
    tail = xb_ref[R:R + halo, :]
    xb_ref[0:halo, :] = tail

    xrb = xr.astype(BF16)
    sp = jax.nn.softplus(-lam_ref[...])
    row = _iota((R, RG_BW), 0)
    first = (row < nb) & (step == 0)
    for n in range(RG_BLOCKS):
        sl = slice(n * RG_BW, (n + 1) * RG_BW)
        xn = xrb[:, sl]
        r = jax.nn.sigmoid(_dot(xn, wa_ref[n]) + ba_ref[:, sl])
        ig = jax.nn.sigmoid(_dot(xn, wi_ref[n]) + bi_ref[:, sl])
        log_a = (-RG_C) * r * sp[:, sl]
        a = jnp.exp(log_a)
        mult = jnp.sqrt(1.0 - jnp.exp(2.0 * log_a))
        mult = jnp.where(first, 1.0, mult)
        a_ref[:, sl] = a
        u_ref[:, sl] = mult * ig * xr[:, sl]

    def body(t, h):
        r0 = pl.multiple_of(t * nb, nb)
        h = a_ref[pl.ds(r0, nb), :] * h + u_ref[pl.ds(r0, nb), :]
        hb_ref[pl.ds(r0, nb), :] = h
        return h

    h = lax.fori_loop(0, tt, body, hs_ref[...], unroll=8)
    hs_ref[...] = h
    o_ref[...] = _dot((hb_ref[...] * gy_ref[...].astype(F32)).astype(BF16), wo_ref[...])


def _rglru(rx_tm, gy_tm, conv_w, conv_b, wa, ba, wi, bi, lam, wo, *, nb, tt=32):
    rows = rx_tm.shape[0]
    S = rows // nb
    assert S % tt == 0
    R = nb * tt
    halo = (CONV_W - 1) * nb
    return pl.pallas_call(
        functools.partial(_rglru_kernel, nb=nb, tt=tt),
        grid=(S // tt,),
        in_specs=[
            pl.BlockSpec((R, D_RNN), lambda s: (s, 0)),
            pl.BlockSpec((R, D_RNN), lambda s: (s, 0)),
            _const_spec((CONV_W, D_RNN)),
            _const_spec((1, D_RNN)),
            _const_spec((RG_BLOCKS, RG_BW, RG_BW)),
            _const_spec((1, D_RNN)),
            _const_spec((RG_BLOCKS, RG_BW, RG_BW)),
            _const_spec((1, D_RNN)),
            _const_spec((1, D_RNN)),
            _const_spec((D_RNN, D_MODEL)),
        ],
        out_specs=pl.BlockSpec((R, D_MODEL), lambda s: (s, 0)),
        out_shape=jax.ShapeDtypeStruct((rows, D_MODEL), F32),
        scratch_shapes=[
            pltpu.VMEM((R + halo, D_RNN), F32),
            pltpu.VMEM((R, D_RNN), F32),
            pltpu.VMEM((R, D_RNN), F32),
            pltpu.VMEM((R, D_RNN), F32),
            pltpu.VMEM((nb, D_RNN), F32),
        ],
        compiler_params=_params(("arbitrary",)),
        name="rglru",
    )(rx_tm, gy_tm, conv_w, conv_b, wa, ba, wi, bi, lam, wo)


def _kvprep_kernel(zk_ref, zv_ref, g_ref, kn_ref, vt_ref):
    rows = zk_ref.shape[1]
    x = jnp.concatenate([zk_ref[0], zk_ref[1]], axis=1)
    seg = jnp.where(_iota((KVW, KVW), 0) // HEAD_DIM == _iota((KVW, KVW), 1) // HEAD_DIM, 1.0, 0.0).astype(BF16)
    hi, lo = _split(x * x)
    ssq = _dot(jnp.concatenate([hi, lo], axis=0), seg)
    kn = (x * lax.rsqrt((ssq[0:rows] + ssq[rows:2 * rows]) * (1.0 / HEAD_DIM) + EPS) * g_ref[...]).astype(BF16)
    for pr in range(KVW // 128):
        kn_ref[pr] = kn[:, pr * 128:(pr + 1) * 128]
    eye = jnp.where(_iota((KVW, KVW), 0) == _iota((KVW, KVW), 1), 1.0, 0.0).astype(BF16)
    pad = jnp.where(_iota((VROWS - HEAD_DIM, KCH), 0) == 0, 1.0, 0.0).astype(BF16)
    for c in range(rows // KCH):
        sl = slice(c * KCH, (c + 1) * KCH)
        v = jnp.concatenate([zv_ref[0, sl, :], zv_ref[1, sl, :]], axis=1).astype(BF16)
        vt = _dot_nt(eye, v).astype(BF16)
        for g in range(N_KV):
            vt_ref[c, g, 0:HEAD_DIM, :] = vt[g * HEAD_DIM:(g + 1) * HEAD_DIM]
            vt_ref[c, g, HEAD_DIM:VROWS, :] = pad


def _kvprep(zkv, gains, *, B, S, cpb=4):
    nch = S // KCH
    assert nch % cpb == 0
    nblk = nch // cpb
    rows = cpb * KCH
    return pl.pallas_call(
        _kvprep_kernel,
        grid=(2, B, nblk),
        in_specs=[
            pl.BlockSpec((None, KVW // 128, rows, 128), lambda w, b, i: (2 + 2 * w, 0, b * nblk + i, 0)),
            pl.BlockSpec((None, KVW // 128, rows, 128), lambda w, b, i: (3 + 2 * w, 0, b * nblk + i, 0)),
            pl.BlockSpec((None, 1, KVW), lambda w, b, i: (w, 0, 0)),
        ],
        out_specs=[
            pl.BlockSpec((None, None, KVW // 128, rows, 128), lambda w, b, i: (w, b, 0, i, 0)),
            pl.BlockSpec((None, None, cpb, N_KV, VROWS, KCH), lambda w, b, i: (w, b, i, 0, 0, 0)),
        ],
        out_shape=[
            jax.ShapeDtypeStruct((2, B, KVW // 128, S, 128), BF16),
            jax.ShapeDtypeStruct((2, B, nch, N_KV, VROWS, KCH), BF16),
        ],
        compiler_params=_params(("parallel", "parallel", "parallel")),
        name="kvprep",
    )(zkv, zkv, gains)


def _compress_kernel(zk_ref, zv_ref, pos_ref, w1_ref, w2_ref, g_ref, kc_ref, vct_ref):
    nch = zk_ref.shape[0] // CMP_STRIDE
    lane = _iota((nch, 128), 1)
    npair = CMP_STRIDE // 2
    eye = jnp.where(_iota((HEAD_DIM, HEAD_DIM), 0) == _iota((HEAD_DIM, HEAD_DIM), 1), 1.0, 0.0).astype(BF16)
    for kind, z_ref in enumerate((zk_ref, zv_ref)):
        for ge in range(2):
            halves = []
            for half in range(2):
                acc = None
                for p in range(npair):
                    a = z_ref[pl.ds(2 * p, nch, stride=CMP_STRIDE), :]
                    b = z_ref[pl.ds(2 * p + 1, nch, stride=CMP_STRIDE), :]
                    if ge == 0:
                        comb = jnp.where(lane < HEAD_DIM, a, pltpu.roll(b, HEAD_DIM, axis=1))
                    else:
                        comb = jnp.where(lane < HEAD_DIM, pltpu.roll(a, HEAD_DIM, axis=1), b)
                    pr = half * npair + p
                    lhs = (comb + pos_ref[kind, pr:pr + 1, :]).astype(BF16)
                    d = _dot(lhs, w1_ref[kind, pr * 128:(pr + 1) * 128, :])
                    acc = d if acc is None else acc + d
                halves.append(acc)
            hid = halves[0] + pltpu.roll(halves[1], nch - 1, axis=0)
            hid = jax.nn.gelu(hid)
            out = _dot(hid.astype(BF16), w2_ref[kind])
            if kind == 0:
                out = out * lax.rsqrt(jnp.mean(out * out, axis=-1, keepdims=True) + EPS) * g_ref[...]
            res = jnp.where(_iota(out.shape, 0) < nch - 1, out, 0.0).astype(BF16)
            if kind == 0:
                kc_ref[ge] = res
            else:
                vct_ref[ge] = _dot_nt(eye, res).astype(BF16)


def _compress(zkv, pos, w1, w2, gain, *, B, S):
    nch = S // CMP_STRIDE
    return pl.pallas_call(
        _compress_kernel,
        grid=(B, KVW // 128),
        in_specs=[
            pl.BlockSpec((None, None, S, 128), lambda b, lp: (0, lp, b, 0)),
            pl.BlockSpec((None, None, S, 128), lambda b, lp: (1, lp, b, 0)),
            _const_spec((2, CMP_STRIDE, 128)),
            _const_spec((2, CMP_LEN * HEAD_DIM, CMP_HID)),
            _const_spec((2, CMP_HID, HEAD_DIM)),
            _const_spec((1, HEAD_DIM)),
        ],
        out_specs=[
            pl.BlockSpec((None, 2, nch, HEAD_DIM), lambda b, lp: (b, lp, 0, 0)),
            pl.BlockSpec((None, 2, HEAD_DIM, nch), lambda b, lp: (b, lp, 0, 0)),
        ],
        out_shape=[
            jax.ShapeDtypeStruct((B, N_KV, nch, HEAD_DIM), BF16),
            jax.ShapeDtypeStruct((B, N_KV, HEAD_DIM, nch), BF16),
        ],
        compiler_params=_params(("parallel", "parallel")),
        name="compress",
    )(zkv, zkv, pos, w1, w2, gain)


def _tile4(x):
    return jnp.concatenate([x, x, x, x], axis=0)


def _spread_heads(x):
    y = x + pltpu.roll(x, 2 * Q_BLK, axis=1)
    return y + pltpu.roll(y, Q_BLK, axis=1)


def _nsa_kernel(zq_ref, qg_ref, kc_ref, vct_ref, ks_ref, kw_ref, vst_ref, vwt_ref, ng_ref, o_ref,
                qtp_ref, oc_ref, bias_ref, ms_ref, accs_ref, mw_ref, accw_ref, sbuf_ref, wbuf_ref,
                gates_ref, scs_ref, *, nsel):
    qi = pl.program_id(1)
    s0 = qi * Q_BLK
    cur = qi
    ncmp = kc_ref.shape[1]
    per_chunk = KCH // SEL_LEN

    lane_t = s0 + (_iota((1, GW), 1) % Q_BLK)
    lane_grp = _iota((1, GW), 1) // Q_BLK
    unit = jnp.where(_iota((Q_BLK, GW), 1) % Q_BLK == _iota((Q_BLK, GW), 0), 1.0, 0.0).astype(BF16)
    same_blk = _iota((GW, GW), 0) // Q_BLK == _iota((GW, GW), 1) // Q_BLK
    ov_n = _iota((nsel, ncmp), 0) * SEL_LEN
    ov_c = _iota((nsel, ncmp), 1) * CMP_STRIDE
    overlap = jnp.where((ov_c < ov_n + SEL_LEN) & (ov_c + CMP_LEN > ov_n), 1.0, 0.0).astype(BF16)


    ok_c = _iota((ncmp, GW), 0) * CMP_STRIDE + (CMP_LEN - 1) <= lane_t
    groups = range(N_KV)
    ng_rows = _iota((GW, NG_PAD), 0) // Q_BLK
    ng_cols = _iota((GW, NG_PAD), 1)
    branch_of_col = jnp.where(_iota((8, NG_PAD), 1) // N_HEADS == _iota((8, NG_PAD), 0), 1.0, 0.0).astype(BF16)
    ng4 = _tile4(ng_ref[...])
    for g in groups:
        pick = (ng_cols % HPG == ng_rows) & ((ng_cols // HPG) % N_KV == g)
        g_hi, g_lo = _split(jnp.where(pick, ng4, 0.0))
        gates_ref[g] = _dot_nt(branch_of_col, g_hi) + _dot_nt(branch_of_col, g_lo)

    seg = jnp.where(same_blk, 1.0, 0.0).astype(BF16)
    zq = [zq_ref[g] for g in groups]
    sq_hi, sq_lo = _split(jnp.concatenate([z * z for z in zq], axis=0))
    ssq = _dot(jnp.concatenate([sq_hi, sq_lo], axis=0), seg)
    half = N_KV * Q_BLK
    inv_rms = lax.rsqrt((ssq[0:half] + ssq[half:2 * half]) * (1.0 / HEAD_DIM) + EPS)
    qts = []
    for g in groups:
        qn = (zq[g] * inv_rms[g * Q_BLK:(g + 1) * Q_BLK] * qg_ref[...]).astype(BF16)
        blocks = jnp.where(same_blk, _tile4(qn), jnp.zeros((GW, GW), BF16))
        qts.append(_dot_nt(unit, blocks).astype(BF16))
    scs = []
    for g in groups:
        off = (g % 2) * HEAD_DIM
        qtp_ref[g, off:off + HEAD_DIM, :] = qts[g]
        qtp_ref[g, HEAD_DIM - off:2 * HEAD_DIM - off, :] = jnp.zeros((HEAD_DIM, GW), BF16)
        scs.append(_dot(kc_ref[g], qts[g]))

    def scores(k_ref, i):
        k0 = pl.multiple_of(i * KCH, KCH)
        return [_dot(k_ref[g // 2, pl.ds(k0, KCH), :], qtp_ref[g]) for g in groups]

    def park(buf_ref, vals):
        for g in groups:
            buf_ref[g] = vals[g]

    n_chunks = qi // per_chunk + 1
    lo_w = jnp.maximum(n_chunks - (WINDOW // KCH + 1), 0)
    park(scs_ref, scs)
    park(sbuf_ref, scores(ks_ref, 0))
    park(wbuf_ref, scores(kw_ref, lo_w))

    key_row = _iota((SEL_LEN, GW), 0)
    no_shift = jnp.zeros((1, GW), F32)
    ms_ref[...] = jnp.full(ms_ref.shape, NEG, F32)
    mw_ref[...] = jnp.full(mw_ref.shape, NEG, F32)
    accs_ref[...] = jnp.zeros(accs_ref.shape, F32)
    accw_ref[...] = jnp.zeros(accw_ref.shape, F32)

    def update(g, blocks, shifts, vt, m_ref, acc_ref):
        m = m_ref[g]
        mnew = m
        for blk, sh in zip(blocks, shifts):
            mnew = jnp.maximum(mnew, jnp.max(blk, axis=0, keepdims=True) + sh)
        e = jnp.concatenate([jnp.exp2(blk + (sh - mnew)) for blk, sh in zip(blocks, shifts)], axis=0).astype(BF16)
        acc_ref[g] = jnp.exp2(m - mnew) * acc_ref[g] + _dot(vt, e)
        m_ref[g] = mnew

    def slabs(buf_ref, g, keep):
        out = []
        for nb in range(per_chunk):
            blk = buf_ref[g, nb * SEL_LEN:(nb + 1) * SEL_LEN, :]
            out.append(blk if keep is None else jnp.where(keep(nb * SEL_LEN), blk, NEG))
        return out

    def sel_softmax(i, last):
        keep = (lambda k0: key_row + (i * KCH + k0) <= lane_t) if last else None
        for g in groups:
            shifts = [bias_ref[g, pl.ds(i * per_chunk + nb, 1), :] for nb in range(per_chunk)]
            update(g, slabs(sbuf_ref, g, keep), shifts, vst_ref[i, g], ms_ref, accs_ref)

    def win_softmax(i, last):
        if last:
            keep = lambda k0: key_row + (i * KCH + k0) <= lane_t
        else:
            keep = lambda k0: key_row + (i * KCH + k0) > lane_t - WINDOW
        for g in groups:
            update(g, slabs(wbuf_ref, g, keep), [no_shift] * per_chunk, vwt_ref[i, g], mw_ref, accw_ref)

    def sel_step(i, carry):
        nxt = scores(ks_ref, i + 1)
        sel_softmax(i, False)
        park(sbuf_ref, nxt)
        return carry

    def win_step(i, carry):
        nxt = scores(kw_ref, i + 1)
        win_softmax(i, False)
        park(wbuf_ref, nxt)
        return carry

    lax.fori_loop(lo_w, n_chunks - 1, win_step, 0)

    imp_all = jnp.zeros((nsel, GW), F32)
    some_c = lane_t >= CMP_LEN - 1
    for g in groups:
        scm = jnp.where(ok_c, scs_ref[g], NEG)
        ec = jnp.exp2(scm - jnp.max(scm, axis=0, keepdims=True))
        inv = jnp.where(some_c, 1.0 / jnp.maximum(jnp.sum(ec, axis=0, keepdims=True), 1e-30), 0.0)
        pc = ec * inv
        p_hi, p_lo = _split(pc)
        both = _dot(jnp.concatenate([vct_ref[g], overlap], axis=0), p_hi)
        oc_ref[g] = both[0:HEAD_DIM]
        imp = _spread_heads(both[HEAD_DIM:HEAD_DIM + nsel] + _dot(overlap, p_lo))
        imp_all = jnp.where(lane_grp == g, imp, imp_all)

    n_io = _iota((nsel, GW), 0)
    forced = (n_io == 0) | (n_io == cur) | (n_io == cur - 1)
    valid = n_io <= cur
    score = jnp.where(forced, FORCE_SCORE, imp_all)
    score = jnp.where(valid, score, -1.0)
    rows8 = [score[r:r + 8] for r in range(0, nsel, 8)]
    ranks = [jnp.zeros((8, GW), F32) for _ in rows8]
    sub = _iota((8, GW), 0)
    for m in range(nsel):
        rowv = score[m:m + 1, :]
        for k, blk in enumerate(rows8):
            ge = jnp.where(rowv >= blk, 1.0, 0.0)
            gt = jnp.where(rowv > blk, 1.0, 0.0)
            if 8 * k > m:
                inc = ge
            elif 8 * k + 7 < m:
                inc = gt
            else:
                inc = jnp.where(sub + 8 * k > m, ge, gt)
            ranks[k] = ranks[k] + inc
    rank = jnp.concatenate(ranks, axis=0)
    sel_all = jnp.where((rank < float(min(SEL_TOPN, nsel))) & valid, 1.0, 0.0)
    for g in groups:
        mine = jnp.where(lane_grp == g, sel_all, 0.0)
        bias_ref[g] = (_spread_heads(mine) - 1.0) * (-NEG)

    win_softmax(n_chunks - 1, True)

    lax.fori_loop(0, n_chunks - 1, sel_step, 0)
    sel_softmax(n_chunks - 1, True)

    outs = []
    for g in groups:
        o_s = accs_ref[g, 0:HEAD_DIM, :] * (1.0 / jnp.maximum(accs_ref[g, HEAD_DIM:HEAD_DIM + 1, :], 1e-30))
        o_w = accw_ref[g, 0:HEAD_DIM, :] * (1.0 / jnp.maximum(accw_ref[g, HEAD_DIM:HEAD_DIM + 1, :], 1e-30))
        gates = gates_ref[g]
        o_t = gates[0:1] * oc_ref[g] + gates[1:2] * o_s + gates[2:3] * o_w
        blocks = jnp.where(same_blk, _tile4(o_t), 0.0).astype(BF16)
        outs.append(_dot_nt(unit, blocks))
    for g in groups:
        o_ref[g] = outs[g].astype(o_ref.dtype)


def _nsa(zq, qgain, kc, vct, kn, vt, ng, *, B, S):
    T = B * S
    nq = S // Q_BLK
    nch = S // KCH
    ncmp = S // CMP_STRIDE
    nsel = S // SEL_LEN
    tok = lambda b, i: (0, b * nq + i, 0)
    return pl.pallas_call(
        functools.partial(_nsa_kernel, nsel=nsel),
        grid=(B, nq),
        in_specs=[
            pl.BlockSpec((N_KV, Q_BLK, GW), tok),
            _const_spec((1, GW)),
            pl.BlockSpec((None, N_KV, ncmp, HEAD_DIM), lambda b, i: (b, 0, 0, 0)),
            pl.BlockSpec((None, N_KV, HEAD_DIM, ncmp), lambda b, i: (b, 0, 0, 0)),
            pl.BlockSpec((None, None, 2, S, 128), lambda b, i: (0, b, 0, 0, 0)),
            pl.BlockSpec((None, None, 2, S, 128), lambda b, i: (1, b, 0, 0, 0)),
            pl.BlockSpec((None, None, nch, N_KV, VROWS, KCH), lambda b, i: (0, b, 0, 0, 0, 0)),
            pl.BlockSpec((None, None, nch, N_KV, VROWS, KCH), lambda b, i: (1, b, 0, 0, 0, 0)),
            pl.BlockSpec((Q_BLK, NG_PAD), lambda b, i: (b * nq + i, 0)),
        ],
        out_specs=pl.BlockSpec((N_KV, Q_BLK, GW), tok),
        out_shape=jax.ShapeDtypeStruct((N_KV, T, GW), BF16),
        scratch_shapes=[
            pltpu.VMEM((N_KV, 2 * HEAD_DIM, GW), BF16),
            pltpu.VMEM((N_KV, HEAD_DIM, GW), F32),
            pltpu.VMEM((N_KV, nsel, GW), F32),
            pltpu.VMEM((N_KV, 1, GW), F32),
            pltpu.VMEM((N_KV, VROWS, GW), F32),
            pltpu.VMEM((N_KV, 1, GW), F32),
            pltpu.VMEM((N_KV, VROWS, GW), F32),
            pltpu.VMEM((N_KV, KCH, GW), F32),
            pltpu.VMEM((N_KV, KCH, GW), F32),
            pltpu.VMEM((N_KV, 8, GW), F32),
            pltpu.VMEM((N_KV, ncmp, GW), F32),
        ],
        compiler_params=_params(("parallel", "arbitrary")),
        name="nsa",
    )(zq, qgain, kc, vct, kn, kn, vt, vt, ng)


def _merge_kernel(x_ref, ya_ref, o_ref_in, mg_ref, wn_ref, wo_ref, out_ref):
    yb = _dot(o_ref_in[0], wn_ref[0:GW, :])
    for g in range(1, N_KV):
        yb = yb + _dot(o_ref_in[g], wn_ref[g * GW:(g + 1) * GW, :])
    mixed = (mg_ref[:, 0:D_MODEL].astype(F32) * ya_ref[...]
             + mg_ref[:, D_MODEL:2 * D_MODEL].astype(F32) * yb)
    out_ref[...] = x_ref[...] + _dot(mixed.astype(BF16), wo_ref[...])


def _merge(x2d, ya_tm, o, mg, wn, wo, *, B, S, tm=512):
    ni = S // tm
    tok = lambda b, i: (b * ni + i, 0)
    return pl.pallas_call(
        _merge_kernel,
        grid=(B, ni),
        in_specs=[
            pl.BlockSpec((tm, D_MODEL), tok),
            pl.BlockSpec((tm, D_MODEL), lambda b, i: (i, b)),
            pl.BlockSpec((N_KV, tm, GW), lambda b, i: (0, b * ni + i, 0)),
            pl.BlockSpec((tm, 2 * D_MODEL), tok),
            _const_spec((N_HEADS * HEAD_DIM, D_MODEL)),
            _const_spec((D_MODEL, D_MODEL)),
        ],
        out_specs=pl.BlockSpec((tm, D_MODEL), tok),
        out_shape=jax.ShapeDtypeStruct((B * S, D_MODEL), F32),
        compiler_params=_params(("parallel", "parallel")),
        name="merge",
    )(x2d, ya_tm, o, mg, wn, wo)


def _ffn_kernel(x_ref, g_ref, wg_ref, wu_ref, wd_ref, o_ref, h_ref, *, tf):
    x = x_ref[...]
    y = x * lax.rsqrt(jnp.mean(x * x, axis=-1, keepdims=True) + EPS) * g_ref[...]
    h_ref[...] = y.astype(BF16)
    o_ref[...] = x
    for j in range(D_FF // tf):
        h = h_ref[...]
        sl = slice(j * tf, (j + 1) * tf)
        act = jax.nn.silu(_dot(h, wg_ref[:, sl])) * _dot(h, wu_ref[:, sl])
        o_ref[...] += _dot(act.astype(BF16), wd_ref[sl, :])


def _ffn(x2d, gain, wg, wu, wd, *, tm=512, tf=256):
    T = x2d.shape[0]
    assert T % tm == 0 and D_FF % tf == 0
    once = dict(pipeline_mode=pl.Buffered(1))
    return pl.pallas_call(
        functools.partial(_ffn_kernel, tf=tf),
        grid=(T // tm,),
        in_specs=[
            pl.BlockSpec((tm, D_MODEL), lambda i: (i, 0)),
            _const_spec((1, D_MODEL)),
            pl.BlockSpec((D_MODEL, D_FF), lambda i: (0, 0), **once),
            pl.BlockSpec((D_MODEL, D_FF), lambda i: (0, 0), **once),
            pl.BlockSpec((D_FF, D_MODEL), lambda i: (0, 0), **once),
        ],
        out_specs=pl.BlockSpec((tm, D_MODEL), lambda i: (i, 0)),
        out_shape=jax.ShapeDtypeStruct((T, D_MODEL), F32),
        scratch_shapes=[pltpu.VMEM((tm, D_MODEL), BF16)],
        compiler_params=_params(("parallel",)),
        name="ffn",
    )(x2d, gain, wg, wu, wd)


def _layer(x, p):
    B, S, _ = x.shape
    T = B * S
    dh = HEAD_DIM
    row = lambda v: v.reshape(1, -1)
    w_in = p["w_in"]
    n_ng = 3 * N_HEADS
    o_ng = 2 * D_RNN + N_HEADS * dh + 6 * KVW
    w_packed = jnp.concatenate(
        [w_in[:, :o_ng], jnp.pad(w_in[:, o_ng:o_ng + n_ng], ((0, 0), (0, NG_PAD - n_ng))), w_in[:, o_ng + n_ng:]],
        axis=1).astype(BF16)
    rx_tm, gy_tm, zq, zkv, ng, mg = _inproj(x, row(p["norm1"]), w_packed)

    ya_tm = _rglru(rx_tm.reshape(S * B, D_RNN), gy_tm.reshape(S * B, D_RNN), p["conv_w"], row(p["conv_b"]),
                   p["rg_wa"].astype(BF16), row(p["rg_ba"]), p["rg_wi"].astype(BF16), row(p["rg_bi"]),
                   row(p["rg_lambda"]), p["w_rg_out"].astype(BF16), nb=B)
    ya_tm = ya_tm.reshape(S, B * D_MODEL)

    kgains = jnp.tile(p["k_norm"][1:3], (1, N_KV)).reshape(2, 1, KVW)
    kn, vt = _kvprep(zkv, kgains, B=B, S=S)
    pos = jnp.stack([p["cmp_pos_k"], p["cmp_pos_v"]]).reshape(2, CMP_STRIDE, 2 * dh)
    w1 = jnp.stack([p["cmp_k_w1"], p["cmp_v_w1"]]).astype(BF16)
    w2 = jnp.stack([p["cmp_k_w2"], p["cmp_v_w2"]]).astype(BF16)
    kc, vct = _compress(zkv, pos, w1, w2, row(p["k_norm"][0]), B=B, S=S)
    qgain = jnp.tile(p["q_norm"] * (dh ** -0.5 * LOG2E), HPG).reshape(1, GW)
    o = _nsa(zq, qgain, kc, vct, kn, vt, ng, B=B, S=S)

    x1 = _merge(x.reshape(T, D_MODEL), ya_tm, o, mg, p["w_nsa_out"].astype(BF16), p["w_o"].astype(BF16), B=B, S=S)
    x2 = _ffn(x1, row(p["norm2"]), p["w_gate"].astype(BF16), p["w_up"].astype(BF16), p["w_down"].astype(BF16))
    return x2.reshape(B, S, D_MODEL)


def kernel(x, norm1, w_in, conv_w, conv_b, rg_wa, rg_ba, rg_wi, rg_bi, rg_lambda, q_norm, k_norm, cmp_pos_k,
           cmp_pos_v, cmp_k_w1, cmp_k_w2, cmp_v_w1, cmp_v_w2, w_rg_out, w_nsa_out, w_o, norm2, w_gate, w_up, w_down):
    params = dict(norm1=norm1, w_in=w_in, conv_w=conv_w, conv_b=conv_b, rg_wa=rg_wa, rg_ba=rg_ba, rg_wi=rg_wi,
                  rg_bi=rg_bi, rg_lambda=rg_lambda, q_norm=q_norm, k_norm=k_norm, cmp_pos_k=cmp_pos_k,
                  cmp_pos_v=cmp_pos_v, cmp_k_w1=cmp_k_w1, cmp_k_w2=cmp_k_w2, cmp_v_w1=cmp_v_w1, cmp_v_w2=cmp_v_w2,
                  w_rg_out=w_rg_out, w_nsa_out=w_nsa_out, w_o=w_o, norm2=norm2, w_gate=w_gate, w_up=w_up,
                  w_down=w_down)
    for l in range(norm1.shape[0]):
        x = _layer(x, {k: v[l] for k, v in params.items()})
    return x
```

```python
import functools

import jax
import jax.numpy as jnp
from jax import lax
from jax.experimental import pallas as pl
from jax.experimental.pallas import tpu as pltpu

D_MODEL = 1024
D_RNN = 1024
RG_BLOCKS = 4
RG_BW = D_RNN // RG_BLOCKS
CONV_W = 4
RG_C = 8.0
N_HEADS = 16
N_KV = 4
HEAD_DIM = 64
HPG = N_HEADS // N_KV
CMP_LEN = 32
CMP_STRIDE = 16
CMP_HID = 256
SEL_LEN = 64
SEL_TOPN = 16
WINDOW = 512
Q_BLK = 64
D_FF = 2816
EPS = 1e-6
FORCE_SCORE = 1e6
NEG = -1e30
LOG2E = 1.4426950408889634

KCH = 256
GW = HPG * HEAD_DIM
KVW = N_KV * HEAD_DIM
VROWS = 80
NG_PAD = 128

F32 = jnp.float32
BF16 = jnp.bfloat16

VMEM_LIMIT = 56 * 1024 * 1024


def _params(sem):
    return pltpu.CompilerParams(dimension_semantics=sem, vmem_limit_bytes=VMEM_LIMIT)


def _dot(a, b):
    return jnp.dot(a, b, preferred_element_type=F32)


def _dot_nt(a, b):
    return lax.dot_general(a, b, (((1,), (1,)), ((), ())), preferred_element_type=F32)


def _split(x):
    hi = x.astype(BF16)
    lo = (x - hi.astype(F32)).astype(BF16)
    return hi, lo


def _iota(shape, dim):
    return lax.broadcasted_iota(jnp.int32, shape, dim)


def _const_spec(shape):
    return pl.BlockSpec(shape, lambda *_: (0,) * len(shape))


_O_RY = D_RNN
_O_Q = 2 * D_RNN
_O_KV = _O_Q + N_HEADS * HEAD_DIM
_O_NG = _O_KV + 6 * KVW
_O_MG = _O_NG + NG_PAD
_W_COLS = _O_MG + 2 * D_MODEL


def _inproj_kernel(x_ref, g_ref, w_ref, rx_ref, gy_ref, zq_ref, zkv_ref, ng_ref, mg_ref, h_ref):
    x = x_ref[...]
    y = x * lax.rsqrt(jnp.mean(x * x, axis=-1, keepdims=True) + EPS) * g_ref[...]
    h_ref[...] = y.astype(BF16)

    def proj(c0, width):
        return _dot(h_ref[...], w_ref[:, c0:c0 + width])

    cw = 256
    tm = x_ref.shape[0]
    for c in range(D_RNN // cw):
        rx = proj(c * cw, cw)
        gy = jax.nn.gelu(proj(_O_RY + c * cw, cw))
        for hf in range(cw // 128):
            rx_ref[c * (cw // 128) + hf] = rx[:, hf * 128:(hf + 1) * 128].reshape(tm // 8, 8, 128)
            gy_ref[c * (cw // 128) + hf] = gy[:, hf * 128:(hf + 1) * 128].reshape(tm // 8, 8, 128)
    for g in range(N_KV):
        zq_ref[g] = proj(_O_Q + g * GW, GW)
    for s in range(6):
        z = proj(_O_KV + s * KVW, KVW)
        for pr in range(KVW // 128):
            zkv_ref[s, pr] = z[:, pr * 128:(pr + 1) * 128]
    ng_ref[...] = jax.nn.sigmoid(proj(_O_NG, NG_PAD))
    for c in range(2 * D_MODEL // cw):
        mg_ref[:, c * cw:(c + 1) * cw] = jax.nn.sigmoid(proj(_O_MG + c * cw, cw)).astype(mg_ref.dtype)


def _inproj(x, gain, w, *, tm=256):
    B, S, D = x.shape
    T = B * S
    ni = S // tm
    nck = D_RNN // 128
    tok = lambda b, i: (b * ni + i, 0)
    slab_spec = pl.BlockSpec((nck, tm // 8, None, 8, 128), lambda b, i: (0, i, b, 0, 0))
    slab_shape = jax.ShapeDtypeStruct((nck, S // 8, B, 8, 128), F32)
    return pl.pallas_call(
        _inproj_kernel,
        grid=(B, ni),
        in_specs=[
            pl.BlockSpec((None, tm, D), lambda b, i: (b, i, 0)),
            _const_spec((1, D)),
            pl.BlockSpec((D, _W_COLS), lambda b, i: (0, 0), pipeline_mode=pl.Buffered(1)),
        ],
        out_specs=[
            slab_spec,
            slab_spec,
            pl.BlockSpec((N_KV, tm, GW), lambda b, i: (0, b * ni + i, 0)),
            pl.BlockSpec((6, KVW // 128, tm, 128), lambda b, i: (0, 0, b * ni + i, 0)),
            pl.BlockSpec((tm, NG_PAD), tok),
            pl.BlockSpec((tm, 2 * D_MODEL), tok),
        ],
        out_shape=[
            slab_shape,
            slab_shape,
            jax.ShapeDtypeStruct((N_KV, T, GW), F32),
            jax.ShapeDtypeStruct((6, KVW // 128, T, 128), F32),
            jax.ShapeDtypeStruct((T, NG_PAD), F32),
            jax.ShapeDtypeStruct((T, 2 * D_MODEL), BF16),
        ],
        scratch_shapes=[pltpu.VMEM((tm, D), BF16)],
        compiler_params=_params(("parallel", "parallel")),
        name="inproj",
    )(x, gain, w)


def _rglru_kernel(x_ref, gy_ref, cw_ref, cb_ref, wa_ref, ba_ref, wi_ref, bi_ref, lam_ref, wo_ref,
                  o_ref, xb_ref, gyb_ref, a_ref, u_ref, hb_ref, hs_ref, yb_ref, *, nb, tt):
    R = nb * tt
    halo = (CONV_W - 1) * nb
    step = pl.program_id(0)
    nck = D_RNN // 128

    @pl.when(step == 0)
    def _():
        xb_ref[0:halo, :] = jnp.zeros((halo, D_RNN), F32)
        hs_ref[...] = jnp.zeros((nb, D_RNN), F32)

    def slab_rows(t8, t_lo):
        return pl.ds(t8 * (8 * nb) + t_lo, nb, stride=8)

    def stage_in(t8, carry):
        for t_lo in range(8):
            r0 = pl.multiple_of((t8 * 8 + t_lo) * nb, nb)
            for ck in range(nck):
                cols = slice(ck * 128, (ck + 1) * 128)
                xb_ref[pl.ds(halo + r0, nb), cols] = x_ref[ck, slab_rows(t8, t_lo), :]
                gyb_ref[pl.ds(r0, nb), cols] = gy_ref[ck, slab_rows(t8, t_lo), :]
        return carry

    lax.fori_loop(0, tt // 8, stage_in, 0)
    xr = cb_ref[...] + cw_ref[0:1, :] * xb_ref[0:R, :]
    for k in range(1, CONV_W):
        xr = xr + cw_ref[k:k + 1, :] * xb_ref[k * nb:k * nb + R, :]
    tail = xb_ref[R:R + halo, :]
    xb_ref[0:halo, :] = tail

    xrb = xr.astype(BF16)
    sp = jax.nn.softplus(-lam_ref[...])
    row = _iota((R, RG_BW), 0)
    first = (row < nb) & (step == 0)
    for n in range(RG_BLOCKS):
        sl = slice(n * RG_BW, (n + 1) * RG_BW)
        xn = xrb[:, sl]
        r = jax.nn.sigmoid(_dot(xn, wa_ref[n]) + ba_ref[:, sl])
        ig = jax.nn.sigmoid(_dot(xn, wi_ref[n]) + bi_ref[:, sl])
        log_a = (-RG_C) * r * sp[:, sl]
        a = jnp.exp(log_a)
        mult = jnp.sqrt(1.0 - jnp.exp(2.0 * log_a))
        mult = jnp.where(first, 1.0, mult)
        a_ref[:, sl] = a
        u_ref[:, sl] = mult * ig * xr[:, sl]

    def body(t, h):
        r0 = pl.multiple_of(t * nb, nb)
        h = a_ref[pl.ds(r0, nb), :] * h + u_ref[pl.ds(r0, nb), :]
        hb_ref[pl.ds(r0, nb), :] = h
        return h

    h = lax.fori_loop(0, tt, body, hs_ref[...], unroll=8)
    hs_ref[...] = h
    yb_ref[...] = _dot((hb_ref[...] * gyb_ref[...]).astype(BF16), wo_ref[...])

    def stage_out(t8, carry):
        for t_lo in range(8):
            r0 = pl.multiple_of((t8 * 8 + t_lo) * nb, nb)
            for ck in range(nck):
                o_ref[ck, slab_rows(t8, t_lo), :] = yb_ref[pl.ds(r0, nb), ck * 128:(ck + 1) * 128]
        return carry

    lax.fori_loop(0, tt // 8, stage_out, 0)


def _rglru(rx, gy, conv_w, conv_b, wa, ba, wi, bi, lam, wo, *, nb, tt=32):
    nck, rows, _ = rx.shape
    S = rows // nb
    assert S % tt == 0 and tt % 8 == 0
    R = nb * tt
    halo = (CONV_W - 1) * nb
    return pl.pallas_call(
        functools.partial(_rglru_kernel, nb=nb, tt=tt),
        grid=(S // tt,),
        in_specs=[
            pl.BlockSpec((nck, R, 128), lambda s: (0, s, 0)),
            pl.BlockSpec((nck, R, 128), lambda s: (0, s, 0)),
            _const_spec((CONV_W, D_RNN)),
            _const_spec((1, D_RNN)),
            _const_spec((RG_BLOCKS, RG_BW, RG_BW)),
            _const_spec((1, D_RNN)),
            _const_spec((RG_BLOCKS, RG_BW, RG_BW)),
            _const_spec((1, D_RNN)),
            _const_spec((1, D_RNN)),
            _const_spec((D_RNN, D_MODEL)),
        ],
        out_specs=pl.BlockSpec((D_MODEL // 128, R, 128), lambda s: (0, s, 0)),
        out_shape=jax.ShapeDtypeStruct((D_MODEL // 128, rows, 128), F32),
        scratch_shapes=[
            pltpu.VMEM((R + halo, D_RNN), F32),
            pltpu.VMEM((R, D_RNN), F32),
            pltpu.VMEM((R, D_RNN), F32),
            pltpu.VMEM((R, D_RNN), F32),
            pltpu.VMEM((R, D_RNN), F32),
            pltpu.VMEM((nb, D_RNN), F32),
            pltpu.VMEM((R, D_MODEL), F32),
        ],
        compiler_params=_params(("arbitrary",)),
        name="rglru",
    )(rx, gy, conv_w, conv_b, wa, ba, wi, bi, lam, wo)


def _kvprep_kernel(zk_ref, zv_ref, g_ref, kn_ref, vt_ref):
    rows = zk_ref.shape[1]
    x = jnp.concatenate([zk_ref[0], zk_ref[1]], axis=1)
    seg = jnp.where(_iota((KVW, KVW), 0) // HEAD_DIM == _iota((KVW, KVW), 1) // HEAD_DIM, 1.0, 0.0).astype(BF16)
    hi, lo = _split(x * x)
    ssq = _dot(jnp.concatenate([hi, lo], axis=0), seg)
    kn = (x * lax.rsqrt((ssq[0:rows] + ssq[rows:2 * rows]) * (1.0 / HEAD_DIM) + EPS) * g_ref[...]).astype(BF16)
    for pr in range(KVW // 128):
        kn_ref[pr] = kn[:, pr * 128:(pr + 1) * 128]
    eye = jnp.where(_iota((KVW, KVW), 0) == _iota((KVW, KVW), 1), 1.0, 0.0).astype(BF16)
    pad = jnp.where(_iota((VROWS - HEAD_DIM, KCH), 0) == 0, 1.0, 0.0).astype(BF16)
    for c in range(rows // KCH):
        sl = slice(c * KCH, (c + 1) * KCH)
        v = jnp.concatenate([zv_ref[0, sl, :], zv_ref[1, sl, :]], axis=1).astype(BF16)
        vt = _dot_nt(eye, v).astype(BF16)
        for g in range(N_KV):
            vt_ref[c, g, 0:HEAD_DIM, :] = vt[g * HEAD_DIM:(g + 1) * HEAD_DIM]
            vt_ref[c, g, HEAD_DIM:VROWS, :] = pad


def _kvprep(zkv, gains, *, B, S, cpb=4):
    nch = S // KCH
    assert nch % cpb == 0
    nblk = nch // cpb
    rows = cpb * KCH
    return pl.pallas_call(
        _kvprep_kernel,
        grid=(2, B, nblk),
        in_specs=[
            pl.BlockSpec((None, KVW // 128, rows, 128), lambda w, b, i: (2 + 2 * w, 0, b * nblk + i, 0)),
            pl.BlockSpec((None, KVW // 128, rows, 128), lambda w, b, i: (3 + 2 * w, 0, b * nblk + i, 0)),
            pl.BlockSpec((None, 1, KVW), lambda w, b, i: (w, 0, 0)),
        ],
        out_specs=[
            pl.BlockSpec((None, None, KVW // 128, rows, 128), lambda w, b, i: (w, b, 0, i, 0)),
            pl.BlockSpec((None, None, cpb, N_KV, VROWS, KCH), lambda w, b, i: (w, b, i, 0, 0, 0)),
        ],
        out_shape=[
            jax.ShapeDtypeStruct((2, B, KVW // 128, S, 128), BF16),
            jax.ShapeDtypeStruct((2, B, nch, N_KV, VROWS, KCH), BF16),
        ],
        compiler_params=_params(("parallel", "parallel", "parallel")),
        name="kvprep",
    )(zkv, zkv, gains)


def _compress_kernel(zk_ref, zv_ref, pos_ref, w1_ref, w2_ref, g_ref, kc_ref, vct_ref):
    nch = zk_ref.shape[0] // CMP_STRIDE
    lane = _iota((nch, 128), 1)
    npair = CMP_STRIDE // 2
    eye = jnp.where(_iota((HEAD_DIM, HEAD_DIM), 0) == _iota((HEAD_DIM, HEAD_DIM), 1), 1.0, 0.0).astype(BF16)
    for kind, z_ref in enumerate((zk_ref, zv_ref)):
        for ge in range(2):
            halves = []
            for half in range(2):
                acc = None
                for p in range(npair):
                    a = z_ref[pl.ds(2 * p, nch, stride=CMP_STRIDE), :]
                    b = z_ref[pl.ds(2 * p + 1, nch, stride=CMP_STRIDE), :]
                    if ge == 0:
                        comb = jnp.where(lane < HEAD_DIM, a, pltpu.roll(b, HEAD_DIM, axis=1))
                    else:
                        comb = jnp.where(lane < HEAD_DIM, pltpu.roll(a, HEAD_DIM, axis=1), b)
                    pr = half * npair + p
                    lhs = (comb + pos_ref[kind, pr:pr + 1, :]).astype(BF16)
                    d = _dot(lhs, w1_ref[kind, pr * 128:(pr + 1) * 128, :])
                    acc = d if acc is None else acc + d
                halves.append(acc)
            hid = halves[0] + pltpu.roll(halves[1], nch - 1, axis=0)
            hid = jax.nn.gelu(hid)
            out = _dot(hid.astype(BF16), w2_ref[kind])
            if kind == 0:
                out = out * lax.rsqrt(jnp.mean(out * out, axis=-1, keepdims=True) + EPS) * g_ref[...]
            res = jnp.where(_iota(out.shape, 0) < nch - 1, out, 0.0).astype(BF16)
            if kind == 0:
                kc_ref[ge] = res
            else:
                vct_ref[ge] = _dot_nt(eye, res).astype(BF16)


def _compress(zkv, pos, w1, w2, gain, *, B, S):
    nch = S // CMP_STRIDE
    return pl.pallas_call(
        _compress_kernel,
        grid=(B, KVW // 128),
        in_specs=[
            pl.BlockSpec((None, None, S, 128), lambda b, lp: (0, lp, b, 0)),
            pl.BlockSpec((None, None, S, 128), lambda b, lp: (1, lp, b, 0)),
            _const_spec((2, CMP_STRIDE, 128)),
            _const_spec((2, CMP_LEN * HEAD_DIM, CMP_HID)),
            _const_spec((2, CMP_HID, HEAD_DIM)),
            _const_spec((1, HEAD_DIM)),
        ],
        out_specs=[
            pl.BlockSpec((None, 2, nch, HEAD_DIM), lambda b, lp: (b, lp, 0, 0)),
            pl.BlockSpec((None, 2, HEAD_DIM, nch), lambda b, lp: (b, lp, 0, 0)),
        ],
        out_shape=[
            jax.ShapeDtypeStruct((B, N_KV, nch, HEAD_DIM), BF16),
            jax.ShapeDtypeStruct((B, N_KV, HEAD_DIM, nch), BF16),
        ],
        compiler_params=_params(("parallel", "parallel")),
        name="compress",
    )(zkv, zkv, pos, w1, w2, gain)


def _tile4(x):
    return jnp.concatenate([x, x, x, x], axis=0)


def _spread_heads(x):
    y = x + pltpu.roll(x, 2 * Q_BLK, axis=1)
    return y + pltpu.roll(y, Q_BLK, axis=1)


def _nsa_kernel(zq_ref, qg_ref, kc_ref, vct_ref, ks_ref, kw_ref, vst_ref, vwt_ref, ng_ref, o_ref,
                qtp_ref, oc_ref, bias_ref, ms_ref, accs_ref, mw_ref, accw_ref, sbuf_ref, wbuf_ref,
                gates_ref, scs_ref, *, nsel):
    qi = pl.program_id(1)
    s0 = qi * Q_BLK
    cur = qi
    ncmp = kc_ref.shape[1]
    per_chunk = KCH // SEL_LEN

    lane_t = s0 + (_iota((1, GW), 1) % Q_BLK)
    lane_grp = _iota((1, GW), 1) // Q_BLK
    unit = jnp.where(_iota((Q_BLK, GW), 1) % Q_BLK == _iota((Q_BLK, GW), 0), 1.0, 0.0).astype(BF16)
    same_blk = _iota((GW, GW), 0) // Q_BLK == _iota((GW, GW), 1) // Q_BLK
    ov_n = _iota((nsel, ncmp), 0) * SEL_LEN
    ov_c = _iota((nsel, ncmp), 1) * CMP_STRIDE
    overlap = jnp.where((ov_c < ov_n + SEL_LEN) & (ov_c + CMP_LEN > ov_n), 1.0, 0.0).astype(BF16)


    ok_c = _iota((ncmp, GW), 0) * CMP_STRIDE + (CMP_LEN - 1) <= lane_t
    groups = range(N_KV)
    ng_rows = _iota((GW, NG_PAD), 0) // Q_BLK
    ng_cols = _iota((GW, NG_PAD), 1)
    branch_of_col = jnp.where(_iota((8, NG_PAD), 1) // N_HEADS == _iota((8, NG_PAD), 0), 1.0, 0.0).astype(BF16)
    ng4 = _tile4(ng_ref[...])
    for g in groups:
        pick = (ng_cols % HPG == ng_rows) & ((ng_cols // HPG) % N_KV == g)
        g_hi, g_lo = _split(jnp.where(pick, ng4, 0.0))
        gates_ref[g] = _dot_nt(branch_of_col, g_hi) + _dot_nt(branch_of_col, g_lo)

    seg = jnp.where(same_blk, 1.0, 0.0).astype(BF16)
    zq = [zq_ref[g] for g in groups]
    sq_hi, sq_lo = _split(jnp.concatenate([z * z for z in zq], axis=0))
    ssq = _dot(jnp.concatenate([sq_hi, sq_lo], axis=0), seg)
    half = N_KV * Q_BLK
    inv_rms = lax.rsqrt((ssq[0:half] + ssq[half:2 * half]) * (1.0 / HEAD_DIM) + EPS)
    qts = []
    for g in groups:
        qn = (zq[g] * inv_rms[g * Q_BLK:(g + 1) * Q_BLK] * qg_ref[...]).astype(BF16)
        blocks = jnp.where(same_blk, _tile4(qn), jnp.zeros((GW, GW), BF16))
        qts.append(_dot_nt(unit, blocks).astype(BF16))
    scs = []
    for g in groups:
        off = (g % 2) * HEAD_DIM
        qtp_ref[g, off:off + HEAD_DIM, :] = qts[g]
        qtp_ref[g, HEAD_DIM - off:2 * HEAD_DIM - off, :] = jnp.zeros((HEAD_DIM, GW), BF16)
        scs.append(_dot(kc_ref[g], qts[g]))

    def scores(k_ref, i):
        k0 = pl.multiple_of(i * KCH, KCH)
        return [_dot(k_ref[g // 2, pl.ds(k0, KCH), :], qtp_ref[g]) for g in groups]

    def park(buf_ref, vals):
        for g in groups:
            buf_ref[g] = vals[g]

    n_chunks = qi // per_chunk + 1
    lo_w = jnp.maximum(n_chunks - (WINDOW // KCH + 1), 0)
    park(scs_ref, scs)
    park(sbuf_ref, scores(ks_ref, 0))
    park(wbuf_ref, scores(kw_ref, lo_w))

    key_row = _iota((SEL_LEN, GW), 0)
    no_shift = jnp.zeros((1, GW), F32)
    ms_ref[...] = jnp.full(ms_ref.shape, NEG, F32)
    mw_ref[...] = jnp.full(mw_ref.shape, NEG, F32)
    accs_ref[...] = jnp.zeros(accs_ref.shape, F32)
    accw_ref[...] = jnp.zeros(accw_ref.shape, F32)

    def update(g, blocks, shifts, vt, m_ref, acc_ref):
        m = m_ref[g]
        mnew = m
        for blk, sh in zip(blocks, shifts):
            mnew = jnp.maximum(mnew, jnp.max(blk, axis=0, keepdims=True) + sh)
        e = jnp.concatenate([jnp.exp2(blk + (sh - mnew)) for blk, sh in zip(blocks, shifts)], axis=0).astype(BF16)
        acc_ref[g] = jnp.exp2(m - mnew) * acc_ref[g] + _dot(vt, e)
        m_ref[g] = mnew

    def slabs(buf_ref, g, keep):
        out = []
        for nb in range(per_chunk):
            blk = buf_ref[g, nb * SEL_LEN:(nb + 1) * SEL_LEN, :]
            out.append(blk if keep is None else jnp.where(keep(nb * SEL_LEN), blk, NEG))
        return out

    def sel_softmax(i, last):
        keep = (lambda k0: key_row + (i * KCH + k0) <= lane_t) if last else None
        for g in groups:
            shifts = [bias_ref[g, pl.ds(i * per_chunk + nb, 1), :] for nb in range(per_chunk)]
            update(g, slabs(sbuf_ref, g, keep), shifts, vst_ref[i, g], ms_ref, accs_ref)

    def win_softmax(i, last):
        if last:
            keep = lambda k0: key_row + (i * KCH + k0) <= lane_t
        else:
            keep = lambda k0: key_row + (i * KCH + k0) > lane_t - WINDOW
        for g in groups:
            update(g, slabs(wbuf_ref, g, keep), [no_shift] * per_chunk, vwt_ref[i, g], mw_ref, accw_ref)

    def sel_step(i, carry):
        nxt = scores(ks_ref, i + 1)
        sel_softmax(i, False)
        park(sbuf_ref, nxt)
        return carry

    def win_step(i, carry):
        nxt = scores(kw_ref, i + 1)
        win_softmax(i, False)
        park(wbuf_ref, nxt)
        return carry

    lax.fori_loop(lo_w, n_chunks - 1, win_step, 0)

    imp_all = jnp.zeros((nsel, GW), F32)
    some_c = lane_t >= CMP_LEN - 1
    for g in groups:
        scm = jnp.where(ok_c, scs_ref[g], NEG)
        ec = jnp.exp2(scm - jnp.max(scm, axis=0, keepdims=True))
        inv = jnp.where(some_c, 1.0 / jnp.maximum(jnp.sum(ec, axis=0, keepdims=True), 1e-30), 0.0)
        pc = ec * inv
        p_hi, p_lo = _split(pc)
        both = _dot(jnp.concatenate([vct_ref[g], overlap], axis=0), p_hi)
        oc_ref[g] = both[0:HEAD_DIM]
        imp = _spread_heads(both[HEAD_DIM:HEAD_DIM + nsel] + _dot(overlap, p_lo))
        imp_all = jnp.where(lane_grp == g, imp, imp_all)

    n_io = _iota((nsel, GW), 0)
    forced = (n_io == 0) | (n_io == cur) | (n_io == cur - 1)
    valid = n_io <= cur
    score = jnp.where(forced, FORCE_SCORE, imp_all)
    score = jnp.where(valid, score, -1.0)
    rows8 = [score[r:r + 8] for r in range(0, nsel, 8)]
    ranks = [jnp.zeros((8, GW), F32) for _ in rows8]
    sub = _iota((8, GW), 0)
    for m in range(nsel):
        rowv = score[m:m + 1, :]
        for k, blk in enumerate(rows8):
            ge = jnp.where(rowv >= blk, 1.0, 0.0)
            gt = jnp.where(rowv > blk, 1.0, 0.0)
            if 8 * k > m:
                inc = ge
            elif 8 * k + 7 < m:
                inc = gt
            else:
                inc = jnp.where(sub + 8 * k > m, ge, gt)
            ranks[k] = ranks[k] + inc
    rank = jnp.concatenate(ranks, axis=0)
    sel_all = jnp.where((rank < float(min(SEL_TOPN, nsel))) & valid, 1.0, 0.0)
    for g in groups:
        mine = jnp.where(lane_grp == g, sel_all, 0.0)
        bias_ref[g] = (_spread_heads(mine) - 1.0) * (-NEG)

    win_softmax(n_chunks - 1, True)

    lax.fori_loop(0, n_chunks - 1, sel_step, 0)
    sel_softmax(n_chunks - 1, True)

    outs = []
    for g in groups:
        o_s = accs_ref[g, 0:HEAD_DIM, :] * (1.0 / jnp.maximum(accs_ref[g, HEAD_DIM:HEAD_DIM + 1, :], 1e-30))
        o_w = accw_ref[g, 0:HEAD_DIM, :] * (1.0 / jnp.maximum(accw_ref[g, HEAD_DIM:HEAD_DIM + 1, :], 1e-30))
        gates = gates_ref[g]
        o_t = gates[0:1] * oc_ref[g] + gates[1:2] * o_s + gates[2:3] * o_w
        blocks = jnp.where(same_blk, _tile4(o_t), 0.0).astype(BF16)
        outs.append(_dot_nt(unit, blocks))
    for g in groups:
        o_ref[g] = outs[g].astype(o_ref.dtype)


def _nsa(zq, qgain, kc, vct, kn, vt, ng, *, B, S):
    T = B * S
    nq = S // Q_BLK
    nch = S // KCH
    ncmp = S // CMP_STRIDE
    nsel = S // SEL_LEN
    tok = lambda b, i: (0, b * nq + i, 0)
    return pl.pallas_call(
        functools.partial(_nsa_kernel, nsel=nsel),
        grid=(B, nq),
        in_specs=[
            pl.BlockSpec((N_KV, Q_BLK, GW), tok),
            _const_spec((1, GW)),
            pl.BlockSpec((None, N_KV, ncmp, HEAD_DIM), lambda b, i: (b, 0, 0, 0)),
            pl.BlockSpec((None, N_KV, HEAD_DIM, ncmp), lambda b, i: (b, 0, 0, 0)),
            pl.BlockSpec((None, None, 2, S, 128), lambda b, i: (0, b, 0, 0, 0)),
            pl.BlockSpec((None, None, 2, S, 128), lambda b, i: (1, b, 0, 0, 0)),
            pl.BlockSpec((None, None, nch, N_KV, VROWS, KCH), lambda b, i: (0, b, 0, 0, 0, 0)),
            pl.BlockSpec((None, None, nch, N_KV, VROWS, KCH), lambda b, i: (1, b, 0, 0, 0, 0)),
            pl.BlockSpec((Q_BLK, NG_PAD), lambda b, i: (b * nq + i, 0)),
        ],
        out_specs=pl.BlockSpec((N_KV, Q_BLK, GW), tok),
        out_shape=jax.ShapeDtypeStruct((N_KV, T, GW), BF16),
        scratch_shapes=[
            pltpu.VMEM((N_KV, 2 * HEAD_DIM, GW), BF16),
            pltpu.VMEM((N_KV, HEAD_DIM, GW), F32),
            pltpu.VMEM((N_KV, nsel, GW), F32),
            pltpu.VMEM((N_KV, 1, GW), F32),
            pltpu.VMEM((N_KV, VROWS, GW), F32),
            pltpu.VMEM((N_KV, 1, GW), F32),
            pltpu.VMEM((N_KV, VROWS, GW), F32),
            pltpu.VMEM((N_KV, KCH, GW), F32),
            pltpu.VMEM((N_KV, KCH, GW), F32),
            pltpu.VMEM((N_KV, 8, GW), F32),
            pltpu.VMEM((N_KV, ncmp, GW), F32),
        ],
        compiler_params=_params(("parallel", "arbitrary")),
        name="nsa",
    )(zq, qgain, kc, vct, kn, kn, vt, vt, ng)


def _merge_kernel(x_ref, ya_ref, o_ref_in, mg_ref, wn_ref, wo_ref, out_ref):
    yb = _dot(o_ref_in[0], wn_ref[0:GW, :])
    for g in range(1, N_KV):
        yb = yb + _dot(o_ref_in[g], wn_ref[g * GW:(g + 1) * GW, :])
    tm = x_ref.shape[0]
    ya = jnp.concatenate([ya_ref[ck].reshape(tm, 128) for ck in range(D_MODEL // 128)], axis=1)
    mixed = (mg_ref[:, 0:D_MODEL].astype(F32) * ya
             + mg_ref[:, D_MODEL:2 * D_MODEL].astype(F32) * yb)
    out_ref[...] = x_ref[...] + _dot(mixed.astype(BF16), wo_ref[...])


def _merge(x2d, ya, o, mg, wn, wo, *, B, S, tm=512):
    ni = S // tm
    tok = lambda b, i: (b * ni + i, 0)
    return pl.pallas_call(
        _merge_kernel,
        grid=(B, ni),
        in_specs=[
            pl.BlockSpec((tm, D_MODEL), tok),
            pl.BlockSpec((D_MODEL // 128, tm // 8, None, 8, 128), lambda b, i: (0, i, b, 0, 0)),
            pl.BlockSpec((N_KV, tm, GW), lambda b, i: (0, b * ni + i, 0)),
            pl.BlockSpec((tm, 2 * D_MODEL), tok),
            _const_spec((N_HEADS * HEAD_DIM, D_MODEL)),
            _const_spec((D_MODEL, D_MODEL)),
        ],
        out_specs=pl.BlockSpec((tm, D_MODEL), tok),
        out_shape=jax.ShapeDtypeStruct((B * S, D_MODEL), F32),
        compiler_params=_params(("parallel", "parallel")),
        name="merge",
    )(x2d, ya, o, mg, wn, wo)


def _ffn_kernel(x_ref, g_ref, wg_ref, wu_ref, wd_ref, o_ref, h_ref, *, tf):
    x = x_ref[...]
    y = x * lax.rsqrt(jnp.mean(x * x, axis=-1, keepdims=True) + EPS) * g_ref[...]
    h_ref[...] = y.astype(BF16)
    o_ref[...] = x
    for j in range(D_FF // tf):
        h = h_ref[...]
        sl = slice(j * tf, (j + 1) * tf)
        act = jax.nn.silu(_dot(h, wg_ref[:, sl])) * _dot(h, wu_ref[:, sl])
        o_ref[...] += _dot(act.astype(BF16), wd_ref[sl, :])


def _ffn(x2d, gain, wg, wu, wd, *, tm=512, tf=256):
    T = x2d.shape[0]
    assert T % tm == 0 and D_FF % tf == 0
    once = dict(pipeline_mode=pl.Buffered(1))
    return pl.pallas_call(
        functools.partial(_ffn_kernel, tf=tf),
        grid=(T // tm,),
        in_specs=[
            pl.BlockSpec((tm, D_MODEL), lambda i: (i, 0)),
            _const_spec((1, D_MODEL)),
            pl.BlockSpec((D_MODEL, D_FF), lambda i: (0, 0), **once),
            pl.BlockSpec((D_MODEL, D_FF), lambda i: (0, 0), **once),
            pl.BlockSpec((D_FF, D_MODEL), lambda i: (0, 0), **once),
        ],
        out_specs=pl.BlockSpec((tm, D_MODEL), lambda i: (i, 0)),
        out_shape=jax.ShapeDtypeStruct((T, D_MODEL), F32),
        scratch_shapes=[pltpu.VMEM((tm, D_MODEL), BF16)],
        compiler_params=_params(("parallel",)),
        name="ffn",
    )(x2d, gain, wg, wu, wd)


def _layer(x, p):
    B, S, _ = x.shape
    T = B * S
    dh = HEAD_DIM
    row = lambda v: v.reshape(1, -1)
    w_in = p["w_in"]
    n_ng = 3 * N_HEADS
    o_ng = 2 * D_RNN + N_HEADS * dh + 6 * KVW
    w_packed = jnp.concatenate(
        [w_in[:, :o_ng], jnp.pad(w_in[:, o_ng:o_ng + n_ng], ((0, 0), (0, NG_PAD - n_ng))), w_in[:, o_ng + n_ng:]],
        axis=1).astype(BF16)
    rx, gy, zq, zkv, ng, mg = _inproj(x, row(p["norm1"]), w_packed)

    slabs = lambda a: a.reshape(a.shape[0], S * B, 128)
    ya = _rglru(slabs(rx), slabs(gy), p["conv_w"], row(p["conv_b"]),
                p["rg_wa"].astype(BF16), row(p["rg_ba"]), p["rg_wi"].astype(BF16), row(p["rg_bi"]),
                row(p["rg_lambda"]), p["w_rg_out"].astype(BF16), nb=B)
    ya = ya.reshape(D_MODEL // 128, S // 8, B, 8, 128)

    kgains = jnp.tile(p["k_norm"][1:3], (1, N_KV)).reshape(2, 1, KVW)
    kn, vt = _kvprep(zkv, kgains, B=B, S=S)
    pos = jnp.stack([p["cmp_pos_k"], p["cmp_pos_v"]]).reshape(2, CMP_STRIDE, 2 * dh)
    w1 = jnp.stack([p["cmp_k_w1"], p["cmp_v_w1"]]).astype(BF16)
    w2 = jnp.stack([p["cmp_k_w2"], p["cmp_v_w2"]]).astype(BF16)
    kc, vct = _compress(zkv, pos, w1, w2, row(p["k_norm"][0]), B=B, S=S)
    qgain = jnp.tile(p["q_norm"] * (dh ** -0.5 * LOG2E), HPG).reshape(1, GW)
    o = _nsa(zq, qgain, kc, vct, kn, vt, ng, B=B, S=S)

    x1 = _merge(x.reshape(T, D_MODEL), ya, o, mg, p["w_nsa_out"].astype(BF16), p["w_o"].astype(BF16), B=B, S=S)
    x2 = _ffn(x1, row(p["norm2"]), p["w_gate"].astype(BF16), p["w_up"].astype(BF16), p["w_down"].astype(BF16))
    return x2.reshape(B, S, D_MODEL)


def kernel(x, norm1, w_in, conv_w, conv_b, rg_wa, rg_ba, rg_wi, rg_bi, rg_lambda, q_norm, k_norm, cmp_pos_k,
           cmp_pos_v, cmp_k_w1, cmp_k_w2, cmp_v_w1, cmp_v_w2, w_rg_out, w_nsa_out, w_o, norm2, w_gate, w_up, w_down):
    params = dict(norm1=norm1, w_in=w_in, conv_w=conv_w, conv_b=conv_b, rg_wa=rg_wa, rg_ba=rg_ba, rg_wi=rg_wi,
                  rg_bi=rg_bi, rg_lambda=rg_lambda, q_norm=q_norm, k_norm=k_norm, cmp_pos_k=cmp_pos_k,
                  cmp_pos_v=cmp_pos_v, cmp_k_w1=cmp_k_w1, cmp_k_w2=cmp_k_w2, cmp_v_w1=cmp_v_w1, cmp_v_w2=cmp_v_w2,
                  w_rg_out=w_rg_out, w_nsa_out=w_nsa_out, w_o=w_o, norm2=norm2, w_gate=w_gate, w_up=w_up,
                  w_down=w_down)
    for l in range(norm1.shape[0]):
        x = _layer(x, {k: v[l] for k, v in params.items()})
    return x
```

```python
import functools

import jax
import jax.numpy as jnp
from jax import lax
from jax.experimental import pallas as pl
from jax.experimental.pallas import tpu as pltpu

D_MODEL = 1024
D_RNN = 1024
RG_BLOCKS = 4
RG_BW = D_RNN // RG_BLOCKS
CONV_W = 4
RG_C = 8.0
N_HEADS = 16
N_KV = 4
HEAD_DIM = 64
HPG = N_HEADS // N_KV
CMP_LEN = 32
CMP_STRIDE = 16
CMP_HID = 256
SEL_LEN = 64
SEL_TOPN = 16
WINDOW = 512
Q_BLK = 64
D_FF = 2816
EPS = 1e-6
FORCE_SCORE = 1e6
NEG = -1e30
LOG2E = 1.4426950408889634

KCH = 256
GW = HPG * HEAD_DIM
KVW = N_KV * HEAD_DIM
VROWS = 80
NG_PAD = 128

F32 = jnp.float32
BF16 = jnp.bfloat16

VMEM_LIMIT = 56 * 1024 * 1024


def _params(sem):
    return pltpu.CompilerParams(dimension_semantics=sem, vmem_limit_bytes=VMEM_LIMIT)


def _dot(a, b):
    return jnp.dot(a, b, preferred_element_type=F32)


def _dot_nt(a, b):
    return lax.dot_general(a, b, (((1,), (1,)), ((), ())), preferred_element_type=F32)


def _split(x):
    hi = x.astype(BF16)
    lo = (x - hi.astype(F32)).astype(BF16)
    return hi, lo


def _iota(shape, dim):
    return lax.broadcasted_iota(jnp.int32, shape, dim)


def _const_spec(shape):
    return pl.BlockSpec(shape, lambda *_: (0,) * len(shape))


_O_RY = D_RNN
_O_Q = 2 * D_RNN
_O_KV = _O_Q + N_HEADS * HEAD_DIM
_O_NG = _O_KV + 6 * KVW
_O_MG = _O_NG + NG_PAD
_W_COLS = _O_MG + 2 * D_MODEL


def _inproj_kernel(x_ref, g_ref, w_ref, rx_ref, gy_ref, zq_ref, zkv_ref, ng_ref, mg_ref, h_ref):
    x = x_ref[...]
    y = x * lax.rsqrt(jnp.mean(x * x, axis=-1, keepdims=True) + EPS) * g_ref[...]
    h_ref[...] = y.astype(BF16)

    def proj(c0, width):
        return _dot(h_ref[...], w_ref[:, c0:c0 + width])

    cw = 256
    tm = x_ref.shape[0]
    for c in range(D_RNN // cw):
        rx = proj(c * cw, cw)
        gy = jax.nn.gelu(proj(_O_RY + c * cw, cw))
        for hf in range(cw // 128):
            rx_ref[c * (cw // 128) + hf] = rx[:, hf * 128:(hf + 1) * 128].reshape(tm // 8, 8, 128)
            gy_ref[c * (cw // 128) + hf] = gy[:, hf * 128:(hf + 1) * 128].reshape(tm // 8, 8, 128)
    for g in range(N_KV):
        zq_ref[g] = proj(_O_Q + g * GW, GW)
    for s in range(6):
        z = proj(_O_KV + s * KVW, KVW)
        for pr in range(KVW // 128):
            zkv_ref[s, pr] = z[:, pr * 128:(pr + 1) * 128]
    ng_ref[...] = jax.nn.sigmoid(proj(_O_NG, NG_PAD))
    for c in range(2 * D_MODEL // cw):
        mg_ref[:, c * cw:(c + 1) * cw] = jax.nn.sigmoid(proj(_O_MG + c * cw, cw)).astype(mg_ref.dtype)


def _inproj(x, gain, w, *, tm=256):
    B, S, D = x.shape
    T = B * S
    ni = S // tm
    nck = D_RNN // 128
    tok = lambda b, i: (b * ni + i, 0)
    slab_spec = pl.BlockSpec((nck, tm // 8, None, 8, 128), lambda b, i: (0, i, b, 0, 0))
    slab_shape = jax.ShapeDtypeStruct((nck, S // 8, B, 8, 128), F32)
    return pl.pallas_call(
        _inproj_kernel,
        grid=(B, ni),
        in_specs=[
            pl.BlockSpec((None, tm, D), lambda b, i: (b, i, 0)),
            _const_spec((1, D)),
            pl.BlockSpec((D, _W_COLS), lambda b, i: (0, 0), pipeline_mode=pl.Buffered(1)),
        ],
        out_specs=[
            slab_spec,
            slab_spec,
            pl.BlockSpec((N_KV, tm, GW), lambda b, i: (0, b * ni + i, 0)),
            pl.BlockSpec((6, KVW // 128, tm, 128), lambda b, i: (0, 0, b * ni + i, 0)),
            pl.BlockSpec((tm, NG_PAD), tok),
            pl.BlockSpec((tm, 2 * D_MODEL), tok),
        ],
        out_shape=[
            slab_shape,
            slab_shape,
            jax.ShapeDtypeStruct((N_KV, T, GW), F32),
            jax.ShapeDtypeStruct((6, KVW // 128, T, 128), F32),
            jax.ShapeDtypeStruct((T, NG_PAD), F32),
            jax.ShapeDtypeStruct((T, 2 * D_MODEL), BF16),
        ],
        scratch_shapes=[pltpu.VMEM((tm, D), BF16)],
        compiler_params=_params(("parallel", "parallel")),
        name="inproj",
    )(x, gain, w)


def _rglru_kernel(x_ref, gy_ref, cw_ref, cb_ref, wa_ref, ba_ref, wi_ref, bi_ref, lam_ref, wo_ref,
                  o_ref, xb_ref, gyb_ref, a_ref, u_ref, hb_ref, hs_ref, yb_ref, *, nb, tt):
    R = nb * tt
    halo = (CONV_W - 1) * nb
    step = pl.program_id(0)
    nck = D_RNN // 128

    @pl.when(step == 0)
    def _():
        xb_ref[0:halo, :] = jnp.zeros((halo, D_RNN), F32)
        hs_ref[...] = jnp.zeros((nb, D_RNN), F32)

    def slab_rows(t8, t_lo):
        return pl.ds(t8 * (8 * nb) + t_lo, nb, stride=8)

    def stage_in(t8, carry):
        for t_lo in range(8):
            r0 = pl.multiple_of((t8 * 8 + t_lo) * nb, nb)
            for ck in range(nck):
                cols = slice(ck * 128, (ck + 1) * 128)
                xb_ref[pl.ds(halo + r0, nb), cols] = x_ref[ck, slab_rows(t8, t_lo), :]
                gyb_ref[pl.ds(r0, nb), cols] = gy_ref[ck, slab_rows(t8, t_lo), :]
        return carry

    lax.fori_loop(0, tt // 8, stage_in, 0)
    xr = cb_ref[...] + cw_ref[0:1, :] * xb_ref[0:R, :]
    for k in range(1, CONV_W):
        xr = xr + cw_ref[k:k + 1, :] * xb_ref[k * nb:k * nb + R, :]
    tail = xb_ref[R:R + halo, :]
    xb_ref[0:halo, :] = tail

    xrb = xr.astype(BF16)
    sp = jax.nn.softplus(-lam_ref[...])
    row = _iota((R, RG_BW), 0)
    first = (row < nb) & (step == 0)
    for n in range(RG_BLOCKS):
        sl = slice(n * RG_BW, (n + 1) * RG_BW)
        xn = xrb[:, sl]
        r = jax.nn.sigmoid(_dot(xn, wa_ref[n]) + ba_ref[:, sl])
        ig = jax.nn.sigmoid(_dot(xn, wi_ref[n]) + bi_ref[:, sl])
        log_a = (-RG_C) * r * sp[:, sl]
        a = jnp.exp(log_a)
        mult = jnp.sqrt(1.0 - jnp.exp(2.0 * log_a))
        mult = jnp.where(first, 1.0, mult)
        a_ref[:, sl] = a
        u_ref[:, sl] = mult * ig * xr[:, sl]

    def body(t, h):
        r0 = pl.multiple_of(t * nb, nb)
        h = a_ref[pl.ds(r0, nb), :] * h + u_ref[pl.ds(r0, nb), :]
        hb_ref[pl.ds(r0, nb), :] = h
        return h

    h = lax.fori_loop(0, tt, body, hs_ref[...], unroll=8)
    hs_ref[...] = h
    yb_ref[...] = _dot((hb_ref[...] * gyb_ref[...]).astype(BF16), wo_ref[...])

    def stage_out(t8, carry):
        for t_lo in range(8):
            r0 = pl.multiple_of((t8 * 8 + t_lo) * nb, nb)
            for ck in range(nck):
                o_ref[ck, slab_rows(t8, t_lo), :] = yb_ref[pl.ds(r0, nb), ck * 128:(ck + 1) * 128]
        return carry

    lax.fori_loop(0, tt // 8, stage_out, 0)


def _rglru(rx, gy, conv_w, conv_b, wa, ba, wi, bi, lam, wo, *, nb, tt=32):
    nck, rows, _ = rx.shape
    S = rows // nb
    assert S % tt == 0 and tt % 8 == 0
    R = nb * tt
    halo = (CONV_W - 1) * nb
    return pl.pallas_call(
        functools.partial(_rglru_kernel, nb=nb, tt=tt),
        grid=(S // tt,),
        in_specs=[
            pl.BlockSpec((nck, R, 128), lambda s: (0, s, 0)),
            pl.BlockSpec((nck, R, 128), lambda s: (0, s, 0)),
            _const_spec((CONV_W, D_RNN)),
            _const_spec((1, D_RNN)),
            _const_spec((RG_BLOCKS, RG_BW, RG_BW)),
            _const_spec((1, D_RNN)),
            _const_spec((RG_BLOCKS, RG_BW, RG_BW)),
            _const_spec((1, D_RNN)),
            _const_spec((1, D_RNN)),
            _const_spec((D_RNN, D_MODEL)),
        ],
        out_specs=pl.BlockSpec((D_MODEL // 128, R, 128), lambda s: (0, s, 0)),
        out_shape=jax.ShapeDtypeStruct((D_MODEL // 128, rows, 128), F32),
        scratch_shapes=[
            pltpu.VMEM((R + halo, D_RNN), F32),
            pltpu.VMEM((R, D_RNN), F32),
            pltpu.VMEM((R, D_RNN), F32),
            pltpu.VMEM((R, D_RNN), F32),
            pltpu.VMEM((R, D_RNN), F32),
            pltpu.VMEM((nb, D_RNN), F32),
            pltpu.VMEM((R, D_MODEL), F32),
        ],
        compiler_params=_params(("arbitrary",)),
        name="rglru",
    )(rx, gy, conv_w, conv_b, wa, ba, wi, bi, lam, wo)


def _kvprep_kernel(zk_ref, zv_ref, g_ref, kn_ref, vt_ref):
    rows = zk_ref.shape[1]
    x = jnp.concatenate([zk_ref[0], zk_ref[1]], axis=1)
    seg = jnp.where(_iota((KVW, KVW), 0) // HEAD_DIM == _iota((KVW, KVW), 1) // HEAD_DIM, 1.0, 0.0).astype(BF16)
    hi, lo = _split(x * x)
    ssq = _dot(jnp.concatenate([hi, lo], axis=0), seg)
    kn = (x * lax.rsqrt((ssq[0:rows] + ssq[rows:2 * rows]) * (1.0 / HEAD_DIM) + EPS) * g_ref[...]).astype(BF16)
    for pr in range(KVW // 128):
        kn_ref[pr] = kn[:, pr * 128:(pr + 1) * 128]
    eye = jnp.where(_iota((KVW, KVW), 0) == _iota((KVW, KVW), 1), 1.0, 0.0).astype(BF16)
    pad = jnp.where(_iota((VROWS - HEAD_DIM, KCH), 0) == 0, 1.0, 0.0).astype(BF16)
    for c in range(rows // KCH):
        sl = slice(c * KCH, (c + 1) * KCH)
        v = jnp.concatenate([zv_ref[0, sl, :], zv_ref[1, sl, :]], axis=1).astype(BF16)
        vt = _dot_nt(eye, v).astype(BF16)
        for g in range(N_KV):
            vt_ref[c, g, 0:HEAD_DIM, :] = vt[g * HEAD_DIM:(g + 1) * HEAD_DIM]
            vt_ref[c, g, HEAD_DIM:VROWS, :] = pad


def _kvprep(zkv, gains, *, B, S, cpb=4):
    nch = S // KCH
    assert nch % cpb == 0
    nblk = nch // cpb
    rows = cpb * KCH
    return pl.pallas_call(
        _kvprep_kernel,
        grid=(2, B, nblk),
        in_specs=[
            pl.BlockSpec((None, KVW // 128, rows, 128), lambda w, b, i: (2 + 2 * w, 0, b * nblk + i, 0)),
            pl.BlockSpec((None, KVW // 128, rows, 128), lambda w, b, i: (3 + 2 * w, 0, b * nblk + i, 0)),
            pl.BlockSpec((None, 1, KVW), lambda w, b, i: (w, 0, 0)),
        ],
        out_specs=[
            pl.BlockSpec((None, None, KVW // 128, rows, 128), lambda w, b, i: (w, b, 0, i, 0)),
            pl.BlockSpec((None, None, cpb, N_KV, VROWS, KCH), lambda w, b, i: (w, b, i, 0, 0, 0)),
        ],
        out_shape=[
            jax.ShapeDtypeStruct((2, B, KVW // 128, S, 128), BF16),
            jax.ShapeDtypeStruct((2, B, nch, N_KV, VROWS, KCH), BF16),
        ],
        compiler_params=_params(("parallel", "parallel", "parallel")),
        name="kvprep",
    )(zkv, zkv, gains)


def _compress_kernel(zk_ref, zv_ref, pos_ref, w1_ref, w2_ref, g_ref, kc_ref, vct_ref):
    nch = zk_ref.shape[0] // CMP_STRIDE
    lane = _iota((nch, 128), 1)
    npair = CMP_STRIDE // 2
    eye = jnp.where(_iota((HEAD_DIM, HEAD_DIM), 0) == _iota((HEAD_DIM, HEAD_DIM), 1), 1.0, 0.0).astype(BF16)
    for kind, z_ref in enumerate((zk_ref, zv_ref)):
        for ge in range(2):
            halves = []
            for half in range(2):
                acc = None
                for p in range(npair):
                    a = z_ref[pl.ds(2 * p, nch, stride=CMP_STRIDE), :]
                    b = z_ref[pl.ds(2 * p + 1, nch, stride=CMP_STRIDE), :]
                    if ge == 0:
                        comb = jnp.where(lane < HEAD_DIM, a, pltpu.roll(b, HEAD_DIM, axis=1))
                    else:
                        comb = jnp.where(lane < HEAD_DIM, pltpu.roll(a, HEAD_DIM, axis=1), b)
                    pr = half * npair + p
                    lhs = (comb + pos_ref[kind, pr:pr + 1, :]).astype(BF16)
                    d = _dot(lhs, w1_ref[kind, pr * 128:(pr + 1) * 128, :])
                    acc = d if acc is None else acc + d
                halves.append(acc)
            hid = halves[0] + pltpu.roll(halves[1], nch - 1, axis=0)
            hid = jax.nn.gelu(hid)
            out = _dot(hid.astype(BF16), w2_ref[kind])
            if kind == 0:
                out = out * lax.rsqrt(jnp.mean(out * out, axis=-1, keepdims=True) + EPS) * g_ref[...]
            res = jnp.where(_iota(out.shape, 0) < nch - 1, out, 0.0).astype(BF16)
            if kind == 0:
                kc_ref[ge] = res
            else:
                vct_ref[ge] = _dot_nt(eye, res).astype(BF16)


def _compress(zkv, pos, w1, w2, gain, *, B, S):
    nch = S // CMP_STRIDE
    return pl.pallas_call(
        _compress_kernel,
        grid=(B, KVW // 128),
        in_specs=[
            pl.BlockSpec((None, None, S, 128), lambda b, lp: (0, lp, b, 0)),
            pl.BlockSpec((None, None, S, 128), lambda b, lp: (1, lp, b, 0)),
            _const_spec((2, CMP_STRIDE, 128)),
            _const_spec((2, CMP_LEN * HEAD_DIM, CMP_HID)),
            _const_spec((2, CMP_HID, HEAD_DIM)),
            _const_spec((1, HEAD_DIM)),
        ],
        out_specs=[
            pl.BlockSpec((None, 2, nch, HEAD_DIM), lambda b, lp: (b, lp, 0, 0)),
            pl.BlockSpec((None, 2, HEAD_DIM, nch), lambda b, lp: (b, lp, 0, 0)),
        ],
        out_shape=[
            jax.ShapeDtypeStruct((B, N_KV, nch, HEAD_DIM), BF16),
            jax.ShapeDtypeStruct((B, N_KV, HEAD_DIM, nch), BF16),
        ],
        compiler_params=_params(("parallel", "parallel")),
        name="compress",
    )(zkv, zkv, pos, w1, w2, gain)


def _spread_heads(x):
    y = x + pltpu.roll(x, 2 * Q_BLK, axis=1)
    return y + pltpu.roll(y, Q_BLK, axis=1)


def _swap_heads(x):
    n = x.shape[0]
    lane = _iota((n, 128), 1)
    halves = []
    for pr in range(x.shape[1] // 128):
        slab = jnp.concatenate([x[:, pr * 128:(pr + 1) * 128], jnp.zeros((128 - n, 128), x.dtype)], axis=0)
        tr = slab.T
        halves.append(jnp.where(lane < n, tr[0:n], pltpu.roll(tr[n:2 * n], n, axis=1)))
    return jnp.concatenate(halves, axis=1)


def _nsa_kernel(zq_ref, qg_ref, kc_ref, vct_ref, ks_ref, kw_ref, vst_ref, vwt_ref, ng_ref, o_ref,
                qtp_ref, oc_ref, bias_ref, ms_ref, accs_ref, mw_ref, accw_ref, sbuf_ref, wbuf_ref,
                gates_ref, scs_ref, *, nsel):
    qi = pl.program_id(1)
    s0 = qi * Q_BLK
    cur = qi
    ncmp = kc_ref.shape[1]
    per_chunk = KCH // SEL_LEN

    lane_t = s0 + (_iota((1, GW), 1) % Q_BLK)
    lane_grp = _iota((1, GW), 1) // Q_BLK
    ov_n = _iota((nsel, ncmp), 0) * SEL_LEN
    ov_c = _iota((nsel, ncmp), 1) * CMP_STRIDE
    overlap = jnp.where((ov_c < ov_n + SEL_LEN) & (ov_c + CMP_LEN > ov_n), 1.0, 0.0).astype(BF16)


    ok_c = _iota((ncmp, GW), 0) * CMP_STRIDE + (CMP_LEN - 1) <= lane_t
    groups = range(N_KV)
    ngt = jnp.concatenate([ng_ref[...], jnp.zeros((128 - Q_BLK, NG_PAD), F32)], axis=0).T
    ngt_hi = pltpu.roll(ngt, Q_BLK, axis=1)
    low_half = _iota((1, 128), 1) < Q_BLK
    for g in groups:
        for c in range(3):
            r = c * N_HEADS + g * HPG
            pairs = [jnp.where(low_half, ngt[r + j:r + j + 1], ngt_hi[r + j + 1:r + j + 2]) for j in (0, 2)]
            gates_ref[g, c:c + 1, :] = jnp.concatenate(pairs, axis=1)

    qts = []
    for g in groups:
        xt = _swap_heads(zq_ref[g])
        ssq = jnp.sum(xt * xt, axis=0, keepdims=True)
        qts.append((xt * lax.rsqrt(ssq * (1.0 / HEAD_DIM) + EPS) * qg_ref[...]).astype(BF16))
    scs = []
    for g in groups:
        off = (g % 2) * HEAD_DIM
        qtp_ref[g, off:off + HEAD_DIM, :] = qts[g]
        qtp_ref[g, HEAD_DIM - off:2 * HEAD_DIM - off, :] = jnp.zeros((HEAD_DIM, GW), BF16)
        scs.append(_dot(kc_ref[g], qts[g]))

    def scores(k_ref, i):
        k0 = pl.multiple_of(i * KCH, KCH)
        return [_dot(k_ref[g // 2, pl.ds(k0, KCH), :], qtp_ref[g]) for g in groups]

    def park(buf_ref, vals):
        for g in groups:
            buf_ref[g] = vals[g]

    n_chunks = qi // per_chunk + 1
    lo_w = jnp.maximum(n_chunks - (WINDOW // KCH + 1), 0)
    park(scs_ref, scs)
    park(sbuf_ref, scores(ks_ref, 0))
    park(wbuf_ref, scores(kw_ref, lo_w))

    key_row = _iota((SEL_LEN, GW), 0)
    no_shift = jnp.zeros((1, GW), F32)
    ms_ref[...] = jnp.full(ms_ref.shape, NEG, F32)
    mw_ref[...] = jnp.full(mw_ref.shape, NEG, F32)
    accs_ref[...] = jnp.zeros(accs_ref.shape, F32)
    accw_ref[...] = jnp.zeros(accw_ref.shape, F32)

    def update(g, blocks, shifts, vt, m_ref, acc_ref):
        m = m_ref[g]
        mnew = m
        for blk, sh in zip(blocks, shifts):
            mnew = jnp.maximum(mnew, jnp.max(blk, axis=0, keepdims=True) + sh)
        e = jnp.concatenate([jnp.exp2(blk + (sh - mnew)) for blk, sh in zip(blocks, shifts)], axis=0).astype(BF16)
        acc_ref[g] = jnp.exp2(m - mnew) * acc_ref[g] + _dot(vt, e)
        m_ref[g] = mnew

    def slabs(buf_ref, g, keep):
        out = []
        for nb in range(per_chunk):
            blk = buf_ref[g, nb * SEL_LEN:(nb + 1) * SEL_LEN, :]
            out.append(blk if keep is None else jnp.where(keep(nb * SEL_LEN), blk, NEG))
        return out

    def sel_softmax(i, last):
        keep = (lambda k0: key_row + (i * KCH + k0) <= lane_t) if last else None
        for g in groups:
            shifts = [bias_ref[g, pl.ds(i * per_chunk + nb, 1), :] for nb in range(per_chunk)]
            update(g, slabs(sbuf_ref, g, keep), shifts, vst_ref[i, g], ms_ref, accs_ref)

    def win_softmax(i, last):
        if last:
            keep = lambda k0: key_row + (i * KCH + k0) <= lane_t
        else:
            keep = lambda k0: key_row + (i * KCH + k0) > lane_t - WINDOW
        for g in groups:
            update(g, slabs(wbuf_ref, g, keep), [no_shift] * per_chunk, vwt_ref[i, g], mw_ref, accw_ref)

    def sel_step(i, carry):
        nxt = scores(ks_ref, i + 1)
        sel_softmax(i, False)
        park(sbuf_ref, nxt)
        return carry

    def win_step(i, carry):
        nxt = scores(kw_ref, i + 1)
        win_softmax(i, False)
        park(wbuf_ref, nxt)
        return carry

    lax.fori_loop(lo_w, n_chunks - 1, win_step, 0)

    imp_all = jnp.zeros((nsel, GW), F32)
    some_c = lane_t >= CMP_LEN - 1
    for g in groups:
        scm = jnp.where(ok_c, scs_ref[g], NEG)
        ec = jnp.exp2(scm - jnp.max(scm, axis=0, keepdims=True))
        inv = jnp.where(some_c, 1.0 / jnp.maximum(jnp.sum(ec, axis=0, keepdims=True), 1e-30), 0.0)
        pc = ec * inv
        p_hi, p_lo = _split(pc)
        both = _dot(jnp.concatenate([vct_ref[g], overlap], axis=0), p_hi)
        oc_ref[g] = both[0:HEAD_DIM]
        imp = _spread_heads(both[HEAD_DIM:HEAD_DIM + nsel] + _dot(overlap, p_lo))
        imp_all = jnp.where(lane_grp == g, imp, imp_all)

    n_io = _iota((nsel, GW), 0)
    forced = (n_io == 0) | (n_io == cur) | (n_io == cur - 1)
    valid = n_io <= cur
    score = jnp.where(forced, FORCE_SCORE, imp_all)
    score = jnp.where(valid, score, -1.0)
    rows8 = [score[r:r + 8] for r in range(0, nsel, 8)]
    ranks = [jnp.zeros((8, GW), F32) for _ in rows8]
    sub = _iota((8, GW), 0)
    for m in range(nsel):
        rowv = score[m:m + 1, :]
        for k, blk in enumerate(rows8):
            ge = jnp.where(rowv >= blk, 1.0, 0.0)
            gt = jnp.where(rowv > blk, 1.0, 0.0)
            if 8 * k > m:
                inc = ge
            elif 8 * k + 7 < m:
                inc = gt
            else:
                inc = jnp.where(sub + 8 * k > m, ge, gt)
            ranks[k] = ranks[k] + inc
    rank = jnp.concatenate(ranks, axis=0)
    sel_all = jnp.where((rank < float(min(SEL_TOPN, nsel))) & valid, 1.0, 0.0)
    for g in groups:
        mine = jnp.where(lane_grp == g, sel_all, 0.0)
        bias_ref[g] = (_spread_heads(mine) - 1.0) * (-NEG)

    win_softmax(n_chunks - 1, True)

    lax.fori_loop(0, n_chunks - 1, sel_step, 0)
    sel_softmax(n_chunks - 1, True)

    outs = []
    for g in groups:
        o_s = accs_ref[g, 0:HEAD_DIM, :] * (1.0 / jnp.maximum(accs_ref[g, HEAD_DIM:HEAD_DIM + 1, :], 1e-30))
        o_w = accw_ref[g, 0:HEAD_DIM, :] * (1.0 / jnp.maximum(accw_ref[g, HEAD_DIM:HEAD_DIM + 1, :], 1e-30))
        gates = gates_ref[g]
        o_t = gates[0:1] * oc_ref[g] + gates[1:2] * o_s + gates[2:3] * o_w
        outs.append(_swap_heads(o_t))
    for g in groups:
        o_ref[g] = outs[g].astype(o_ref.dtype)


def _nsa(zq, qgain, kc, vct, kn, vt, ng, *, B, S):
    T = B * S
    nq = S // Q_BLK
    nch = S // KCH
    ncmp = S // CMP_STRIDE
    nsel = S // SEL_LEN
    tok = lambda b, i: (0, b * nq + i, 0)
    return pl.pallas_call(
        functools.partial(_nsa_kernel, nsel=nsel),
        grid=(B, nq),
        in_specs=[
            pl.BlockSpec((N_KV, Q_BLK, GW), tok),
            _const_spec((HEAD_DIM, GW)),
            pl.BlockSpec((None, N_KV, ncmp, HEAD_DIM), lambda b, i: (b, 0, 0, 0)),
            pl.BlockSpec((None, N_KV, HEAD_DIM, ncmp), lambda b, i: (b, 0, 0, 0)),
            pl.BlockSpec((None, None, 2, S, 128), lambda b, i: (0, b, 0, 0, 0)),
            pl.BlockSpec((None, None, 2, S, 128), lambda b, i: (1, b, 0, 0, 0)),
            pl.BlockSpec((None, None, nch, N_KV, VROWS, KCH), lambda b, i: (0, b, 0, 0, 0, 0)),
            pl.BlockSpec((None, None, nch, N_KV, VROWS, KCH), lambda b, i: (1, b, 0, 0, 0, 0)),
            pl.BlockSpec((Q_BLK, NG_PAD), lambda b, i: (b * nq + i, 0)),
        ],
        out_specs=pl.BlockSpec((N_KV, Q_BLK, GW), tok),
        out_shape=jax.ShapeDtypeStruct((N_KV, T, GW), BF16),
        scratch_shapes=[
            pltpu.VMEM((N_KV, 2 * HEAD_DIM, GW), BF16),
            pltpu.VMEM((N_KV, HEAD_DIM, GW), F32),
            pltpu.VMEM((N_KV, nsel, GW), F32),
            pltpu.VMEM((N_KV, 1, GW), F32),
            pltpu.VMEM((N_KV, VROWS, GW), F32),
            pltpu.VMEM((N_KV, 1, GW), F32),
            pltpu.VMEM((N_KV, VROWS, GW), F32),
            pltpu.VMEM((N_KV, KCH, GW), F32),
            pltpu.VMEM((N_KV, KCH, GW), F32),
            pltpu.VMEM((N_KV, 8, GW), F32),
            pltpu.VMEM((N_KV, ncmp, GW), F32),
        ],
        compiler_params=_params(("parallel", "arbitrary")),
        name="nsa",
    )(zq, qgain, kc, vct, kn, kn, vt, vt, ng)


def _merge_kernel(x_ref, ya_ref, o_ref_in, mg_ref, wn_ref, wo_ref, out_ref):
    yb = _dot(o_ref_in[0], wn_ref[0:GW, :])
    for g in range(1, N_KV):
        yb = yb + _dot(o_ref_in[g], wn_ref[g * GW:(g + 1) * GW, :])
    tm = x_ref.shape[0]
    ya = jnp.concatenate([ya_ref[ck].reshape(tm, 128) for ck in range(D_MODEL // 128)], axis=1)
    mixed = (mg_ref[:, 0:D_MODEL].astype(F32) * ya
             + mg_ref[:, D_MODEL:2 * D_MODEL].astype(F32) * yb)
    out_ref[...] = x_ref[...] + _dot(mixed.astype(BF16), wo_ref[...])


def _merge(x2d, ya, o, mg, wn, wo, *, B, S, tm=512):
    ni = S // tm
    tok = lambda b, i: (b * ni + i, 0)
    return pl.pallas_call(
        _merge_kernel,
        grid=(B, ni),
        in_specs=[
            pl.BlockSpec((tm, D_MODEL), tok),
            pl.BlockSpec((D_MODEL // 128, tm // 8, None, 8, 128), lambda b, i: (0, i, b, 0, 0)),
            pl.BlockSpec((N_KV, tm, GW), lambda b, i: (0, b * ni + i, 0)),
            pl.BlockSpec((tm, 2 * D_MODEL), tok),
            _const_spec((N_HEADS * HEAD_DIM, D_MODEL)),
            _const_spec((D_MODEL, D_MODEL)),
        ],
        out_specs=pl.BlockSpec((tm, D_MODEL), tok),
        out_shape=jax.ShapeDtypeStruct((B * S, D_MODEL), F32),
        compiler_params=_params(("parallel", "parallel")),
        name="merge",
    )(x2d, ya, o, mg, wn, wo)


def _ffn_kernel(x_ref, g_ref, wg_ref, wu_ref, wd_ref, o_ref, h_ref, *, tf):
    x = x_ref[...]
    y = x * lax.rsqrt(jnp.mean(x * x, axis=-1, keepdims=True) + EPS) * g_ref[...]
    h_ref[...] = y.astype(BF16)
    o_ref[...] = x
    for j in range(D_FF // tf):
        h = h_ref[...]
        sl = slice(j * tf, (j + 1) * tf)
        act = jax.nn.silu(_dot(h, wg_ref[:, sl])) * _dot(h, wu_ref[:, sl])
        o_ref[...] += _dot(act.astype(BF16), wd_ref[sl, :])


def _ffn(x2d, gain, wg, wu, wd, *, tm=512, tf=256):
    T = x2d.shape[0]
    assert T % tm == 0 and D_FF % tf == 0
    once = dict(pipeline_mode=pl.Buffered(1))
    return pl.pallas_call(
        functools.partial(_ffn_kernel, tf=tf),
        grid=(T // tm,),
        in_specs=[
            pl.BlockSpec((tm, D_MODEL), lambda i: (i, 0)),
            _const_spec((1, D_MODEL)),
            pl.BlockSpec((D_MODEL, D_FF), lambda i: (0, 0), **once),
            pl.BlockSpec((D_MODEL, D_FF), lambda i: (0, 0), **once),
            pl.BlockSpec((D_FF, D_MODEL), lambda i: (0, 0), **once),
        ],
        out_specs=pl.BlockSpec((tm, D_MODEL), lambda i: (i, 0)),
        out_shape=jax.ShapeDtypeStruct((T, D_MODEL), F32),
        scratch_shapes=[pltpu.VMEM((tm, D_MODEL), BF16)],
        compiler_params=_params(("parallel",)),
        name="ffn",
    )(x2d, gain, wg, wu, wd)


def _layer(x, p):
    B, S, _ = x.shape
    T = B * S
    dh = HEAD_DIM
    row = lambda v: v.reshape(1, -1)
    w_in = p["w_in"]
    n_ng = 3 * N_HEADS
    o_ng = 2 * D_RNN + N_HEADS * dh + 6 * KVW
    w_packed = jnp.concatenate(
        [w_in[:, :o_ng], jnp.pad(w_in[:, o_ng:o_ng + n_ng], ((0, 0), (0, NG_PAD - n_ng))), w_in[:, o_ng + n_ng:]],
        axis=1).astype(BF16)
    rx, gy, zq, zkv, ng, mg = _inproj(x, row(p["norm1"]), w_packed)

    slabs = lambda a: a.reshape(a.shape[0], S * B, 128)
    ya = _rglru(slabs(rx), slabs(gy), p["conv_w"], row(p["conv_b"]),
                p["rg_wa"].astype(BF16), row(p["rg_ba"]), p["rg_wi"].astype(BF16), row(p["rg_bi"]),
                row(p["rg_lambda"]), p["w_rg_out"].astype(BF16), nb=B)
    ya = ya.reshape(D_MODEL // 128, S // 8, B, 8, 128)

    kgains = jnp.tile(p["k_norm"][1:3], (1, N_KV)).reshape(2, 1, KVW)
    kn, vt = _kvprep(zkv, kgains, B=B, S=S)
    pos = jnp.stack([p["cmp_pos_k"], p["cmp_pos_v"]]).reshape(2, CMP_STRIDE, 2 * dh)
    w1 = jnp.stack([p["cmp_k_w1"], p["cmp_v_w1"]]).astype(BF16)
    w2 = jnp.stack([p["cmp_k_w2"], p["cmp_v_w2"]]).astype(BF16)
    kc, vct = _compress(zkv, pos, w1, w2, row(p["k_norm"][0]), B=B, S=S)
    qgain = jnp.broadcast_to((p["q_norm"] * (dh ** -0.5 * LOG2E))[:, None], (dh, GW))
    o = _nsa(zq, qgain, kc, vct, kn, vt, ng, B=B, S=S)

    x1 = _merge(x.reshape(T, D_MODEL), ya, o, mg, p["w_nsa_out"].astype(BF16), p["w_o"].astype(BF16), B=B, S=S)
    x2 = _ffn(x1, row(p["norm2"]), p["w_gate"].astype(BF16), p["w_up"].astype(BF16), p["w_down"].astype(BF16))
    return x2.reshape(B, S, D_MODEL)


def kernel(x, norm1, w_in, conv_w, conv_b, rg_wa, rg_ba, rg_wi, rg_bi, rg_lambda, q_norm, k_norm, cmp_pos_k,
           cmp_pos_v, cmp_k_w1, cmp_k_w2, cmp_v_w1, cmp_v_w2, w_rg_out, w_nsa_out, w_o, norm2, w_gate, w_up, w_down):
    params = dict(norm1=norm1, w_in=w_in, conv_w=conv_w, conv_b=conv_b, rg_wa=rg_wa, rg_ba=rg_ba, rg_wi=rg_wi,
                  rg_bi=rg_bi, rg_lambda=rg_lambda, q_norm=q_norm, k_norm=k_norm, cmp_pos_k=cmp_pos_k,
                  cmp_pos_v=cmp_pos_v, cmp_k_w1=cmp_k_w1, cmp_k_w2=cmp_k_w2, cmp_v_w1=cmp_v_w1, cmp_v_w2=cmp_v_w2,
                  w_rg_out=w_rg_out, w_nsa_out=w_nsa_out, w_o=w_o, norm2=norm2, w_gate=w_gate, w_up=w_up,
                  w_down=w_down)
    for l in range(norm1.shape[0]):
        x = _layer(x, {k: v[l] for k, v in params.items()})
    return x
```

```python
import functools

import jax
import jax.numpy as jnp
from jax import lax
from jax.experimental import pallas as pl
from jax.experimental.pallas import tpu as pltpu

D_MODEL = 1024
D_RNN = 1024
RG_BLOCKS = 4
RG_BW = D_RNN // RG_BLOCKS
CONV_W = 4
RG_C = 8.0
N_HEADS = 16
N_KV = 4
HEAD_DIM = 64
HPG = N_HEADS // N_KV
CMP_LEN = 32
CMP_STRIDE = 16
CMP_HID = 256
SEL_LEN = 64
SEL_TOPN = 16
WINDOW = 512
Q_BLK = 64
D_FF = 2816
EPS = 1e-6
FORCE_SCORE = 1e6
NEG = -1e30
LOG2E = 1.4426950408889634

KCH = 256
GW = HPG * HEAD_DIM
KVW = N_KV * HEAD_DIM
VROWS = 80
NG_PAD = 128

F32 = jnp.float32
BF16 = jnp.bfloat16

VMEM_LIMIT = 56 * 1024 * 1024


def _params(sem):
    return pltpu.CompilerParams(dimension_semantics=sem, vmem_limit_bytes=VMEM_LIMIT)


def _dot(a, b):
    return jnp.dot(a, b, preferred_element_type=F32)


def _dot_nt(a, b):
    return lax.dot_general(a, b, (((1,), (1,)), ((), ())), preferred_element_type=F32)


def _split(x):
    hi = x.astype(BF16)
    lo = (x - hi.astype(F32)).astype(BF16)
    return hi, lo


def _iota(shape, dim):
    return lax.broadcasted_iota(jnp.int32, shape, dim)


def _const_spec(shape):
    return pl.BlockSpec(shape, lambda *_: (0,) * len(shape))


_O_RY = D_RNN
_O_Q = 2 * D_RNN
_O_KV = _O_Q + N_HEADS * HEAD_DIM
_O_NG = _O_KV + 6 * KVW
_O_MG = _O_NG + NG_PAD
_W_COLS = _O_MG + 2 * D_MODEL


def _inproj_kernel(x_ref, g_ref, w_ref, rx_ref, gy_ref, zq_ref, zkv_ref, ng_ref, mg_ref, h_ref):
    x = x_ref[...]
    y = x * lax.rsqrt(jnp.mean(x * x, axis=-1, keepdims=True) + EPS) * g_ref[...]
    h_ref[...] = y.astype(BF16)

    def proj(c0, width):
        return _dot(h_ref[...], w_ref[:, c0:c0 + width])

    cw = 256
    tm = x_ref.shape[0]
    for c in range(D_RNN // cw):
        rx = proj(c * cw, cw)
        gy = jax.nn.gelu(proj(_O_RY + c * cw, cw))
        for hf in range(cw // 128):
            rx_ref[c * (cw // 128) + hf] = rx[:, hf * 128:(hf + 1) * 128].reshape(tm // 8, 8, 128)
            gy_ref[c * (cw // 128) + hf] = gy[:, hf * 128:(hf + 1) * 128].reshape(tm // 8, 8, 128)
    for g in range(N_KV):
        zq_ref[g] = proj(_O_Q + g * GW, GW)
    for s in range(6):
        z = proj(_O_KV + s * KVW, KVW)
        for pr in range(KVW // 128):
            zkv_ref[s, pr] = z[:, pr * 128:(pr + 1) * 128]
    ng_ref[...] = jax.nn.sigmoid(proj(_O_NG, NG_PAD))
    for c in range(2 * D_MODEL // cw):
        mg_ref[:, c * cw:(c + 1) * cw] = jax.nn.sigmoid(proj(_O_MG + c * cw, cw)).astype(mg_ref.dtype)


def _inproj(x, gain, w, *, tm=256):
    B, S, D = x.shape
    T = B * S
    ni = S // tm
    nck = D_RNN // 128
    tok = lambda b, i: (b * ni + i, 0)
    slab_spec = pl.BlockSpec((nck, tm // 8, None, 8, 128), lambda b, i: (0, i, b, 0, 0))
    slab_shape = jax.ShapeDtypeStruct((nck, S // 8, B, 8, 128), F32)
    return pl.pallas_call(
        _inproj_kernel,
        grid=(B, ni),
        in_specs=[
            pl.BlockSpec((None, tm, D), lambda b, i: (b, i, 0)),
            _const_spec((1, D)),
            pl.BlockSpec((D, _W_COLS), lambda b, i: (0, 0), pipeline_mode=pl.Buffered(1)),
        ],
        out_specs=[
            slab_spec,
            slab_spec,
            pl.BlockSpec((N_KV, tm, GW), lambda b, i: (0, b * ni + i, 0)),
            pl.BlockSpec((6, KVW // 128, tm, 128), lambda b, i: (0, 0, b * ni + i, 0)),
            pl.BlockSpec((tm, NG_PAD), tok),
            pl.BlockSpec((tm, 2 * D_MODEL), tok),
        ],
        out_shape=[
            slab_shape,
            slab_shape,
            jax.ShapeDtypeStruct((N_KV, T, GW), F32),
            jax.ShapeDtypeStruct((6, KVW // 128, T, 128), F32),
            jax.ShapeDtypeStruct((T, NG_PAD), F32),
            jax.ShapeDtypeStruct((T, 2 * D_MODEL), BF16),
        ],
        scratch_shapes=[pltpu.VMEM((tm, D), BF16)],
        compiler_params=_params(("parallel", "parallel")),
        name="inproj",
    )(x, gain, w)


def _rglru_kernel(x_ref, gy_ref, cw_ref, cb_ref, wa_ref, ba_ref, wi_ref, bi_ref, lam_ref, wo_ref,
                  o_ref, xb_ref, gyb_ref, a_ref, u_ref, hb_ref, hs_ref, yb_ref, *, nb, tt):
    R = nb * tt
    halo = (CONV_W - 1) * nb
    step = pl.program_id(0)
    nck = D_RNN // 128

    @pl.when(step == 0)
    def _():
        xb_ref[0:halo, :] = jnp.zeros((halo, D_RNN), F32)
        hs_ref[...] = jnp.zeros((nb, D_RNN), F32)

    def slab_rows(t8, t_lo):
        return pl.ds(t8 * (8 * nb) + t_lo, nb, stride=8)

    def stage_in(t8, carry):
        for t_lo in range(8):
            r0 = pl.multiple_of((t8 * 8 + t_lo) * nb, nb)
            for ck in range(nck):
                cols = slice(ck * 128, (ck + 1) * 128)
                xb_ref[pl.ds(halo + r0, nb), cols] = x_ref[ck, slab_rows(t8, t_lo), :]
                gyb_ref[pl.ds(r0, nb), cols] = gy_ref[ck, slab_rows(t8, t_lo), :]
        return carry

    lax.fori_loop(0, tt // 8, stage_in, 0)
    xr = cb_ref[...] + cw_ref[0:1, :] * xb_ref[0:R, :]
    for k in range(1, CONV_W):
        xr = xr + cw_ref[k:k + 1, :] * xb_ref[k * nb:k * nb + R, :]
    tail = xb_ref[R:R + halo, :]
    xb_ref[0:halo, :] = tail

    xrb = xr.astype(BF16)
    sp = jax.nn.softplus(-lam_ref[...])
    row = _iota((R, RG_BW), 0)
    first = (row < nb) & (step == 0)
    for n in range(RG_BLOCKS):
        sl = slice(n * RG_BW, (n + 1) * RG_BW)
        xn = xrb[:, sl]
        r = jax.nn.sigmoid(_dot(xn, wa_ref[n]) + ba_ref[:, sl])
        ig = jax.nn.sigmoid(_dot(xn, wi_ref[n]) + bi_ref[:, sl])
        log_a = (-RG_C) * r * sp[:, sl]
        a = jnp.exp(log_a)
        mult = jnp.sqrt(1.0 - a * a)
        mult = jnp.where(first, 1.0, mult)
        a_ref[:, sl] = a
        u_ref[:, sl] = mult * ig * xr[:, sl]

    def body(t, h):
        r0 = pl.multiple_of(t * nb, nb)
        h = a_ref[pl.ds(r0, nb), :] * h + u_ref[pl.ds(r0, nb), :]
        hb_ref[pl.ds(r0, nb), :] = h
        return h

    h = lax.fori_loop(0, tt, body, hs_ref[...], unroll=8)
    hs_ref[...] = h
    yb_ref[...] = _dot((hb_ref[...] * gyb_ref[...]).astype(BF16), wo_ref[...])

    def stage_out(t8, carry):
        for t_lo in range(8):
            r0 = pl.multiple_of((t8 * 8 + t_lo) * nb, nb)
            for ck in range(nck):
                o_ref[ck, slab_rows(t8, t_lo), :] = yb_ref[pl.ds(r0, nb), ck * 128:(ck + 1) * 128]
        return carry

    lax.fori_loop(0, tt // 8, stage_out, 0)


def _rglru(rx, gy, conv_w, conv_b, wa, ba, wi, bi, lam, wo, *, nb, tt=32):
    nck, rows, _ = rx.shape
    S = rows // nb
    assert S % tt == 0 and tt % 8 == 0
    R = nb * tt
    halo = (CONV_W - 1) * nb
    return pl.pallas_call(
        functools.partial(_rglru_kernel, nb=nb, tt=tt),
        grid=(S // tt,),
        in_specs=[
            pl.BlockSpec((nck, R, 128), lambda s: (0, s, 0)),
            pl.BlockSpec((nck, R, 128), lambda s: (0, s, 0)),
            _const_spec((CONV_W, D_RNN)),
            _const_spec((1, D_RNN)),
            _const_spec((RG_BLOCKS, RG_BW, RG_BW)),
            _const_spec((1, D_RNN)),
            _const_spec((RG_BLOCKS, RG_BW, RG_BW)),
            _const_spec((1, D_RNN)),
            _const_spec((1, D_RNN)),
            _const_spec((D_RNN, D_MODEL)),
        ],
        out_specs=pl.BlockSpec((D_MODEL // 128, R, 128), lambda s: (0, s, 0)),
        out_shape=jax.ShapeDtypeStruct((D_MODEL // 128, rows, 128), F32),
        scratch_shapes=[
            pltpu.VMEM((R + halo, D_RNN), F32),
            pltpu.VMEM((R, D_RNN), F32),
            pltpu.VMEM((R, D_RNN), F32),
            pltpu.VMEM((R, D_RNN), F32),
            pltpu.VMEM((R, D_RNN), F32),
            pltpu.VMEM((nb, D_RNN), F32),
            pltpu.VMEM((R, D_MODEL), F32),
        ],
        compiler_params=_params(("arbitrary",)),
        name="rglru",
    )(rx, gy, conv_w, conv_b, wa, ba, wi, bi, lam, wo)


def _kvprep_kernel(zk_ref, zv_ref, g_ref, kn_ref, vt_ref):
    rows = zk_ref.shape[1]
    x = jnp.concatenate([zk_ref[0], zk_ref[1]], axis=1)
    seg = jnp.where(_iota((KVW, KVW), 0) // HEAD_DIM == _iota((KVW, KVW), 1) // HEAD_DIM, 1.0, 0.0).astype(BF16)
    hi, lo = _split(x * x)
    ssq = _dot(jnp.concatenate([hi, lo], axis=0), seg)
    kn = (x * lax.rsqrt((ssq[0:rows] + ssq[rows:2 * rows]) * (1.0 / HEAD_DIM) + EPS) * g_ref[...]).astype(BF16)
    for pr in range(KVW // 128):
        kn_ref[pr] = kn[:, pr * 128:(pr + 1) * 128]
    eye = jnp.where(_iota((KVW, KVW), 0) == _iota((KVW, KVW), 1), 1.0, 0.0).astype(BF16)
    pad = jnp.where(_iota((VROWS - HEAD_DIM, KCH), 0) == 0, 1.0, 0.0).astype(BF16)
    for c in range(rows // KCH):
        sl = slice(c * KCH, (c + 1) * KCH)
        v = jnp.concatenate([zv_ref[0, sl, :], zv_ref[1, sl, :]], axis=1).astype(BF16)
        vt = _dot_nt(eye, v).astype(BF16)
        for g in range(N_KV):
            vt_ref[c, g, 0:HEAD_DIM, :] = vt[g * HEAD_DIM:(g + 1) * HEAD_DIM]
            vt_ref[c, g, HEAD_DIM:VROWS, :] = pad


def _kvprep(zkv, gains, *, B, S, cpb=4):
    nch = S // KCH
    assert nch % cpb == 0
    nblk = nch // cpb
    rows = cpb * KCH
    return pl.pallas_call(
        _kvprep_kernel,
        grid=(2, B, nblk),
        in_specs=[
            pl.BlockSpec((None, KVW // 128, rows, 128), lambda w, b, i: (2 + 2 * w, 0, b * nblk + i, 0)),
            pl.BlockSpec((None, KVW // 128, rows, 128), lambda w, b, i: (3 + 2 * w, 0, b * nblk + i, 0)),
            pl.BlockSpec((None, 1, KVW), lambda w, b, i: (w, 0, 0)),
        ],
        out_specs=[
            pl.BlockSpec((None, None, KVW // 128, rows, 128), lambda w, b, i: (w, b, 0, i, 0)),
            pl.BlockSpec((None, None, cpb, N_KV, VROWS, KCH), lambda w, b, i: (w, b, i, 0, 0, 0)),
        ],
        out_shape=[
            jax.ShapeDtypeStruct((2, B, KVW // 128, S, 128), BF16),
            jax.ShapeDtypeStruct((2, B, nch, N_KV, VROWS, KCH), BF16),
        ],
        compiler_params=_params(("parallel", "parallel", "parallel")),
        name="kvprep",
    )(zkv, zkv, gains)


def _compress_kernel(zk_ref, zv_ref, pos_ref, w1_ref, w2_ref, g_ref, kc_ref, vct_ref):
    nch = zk_ref.shape[0] // CMP_STRIDE
    lane = _iota((nch, 128), 1)
    npair = CMP_STRIDE // 2
    eye = jnp.where(_iota((HEAD_DIM, HEAD_DIM), 0) == _iota((HEAD_DIM, HEAD_DIM), 1), 1.0, 0.0).astype(BF16)
    for kind, z_ref in enumerate((zk_ref, zv_ref)):
        for ge in range(2):
            halves = []
            for half in range(2):
                acc = None
                for p in range(npair):
                    a = z_ref[pl.ds(2 * p, nch, stride=CMP_STRIDE), :]
                    b = z_ref[pl.ds(2 * p + 1, nch, stride=CMP_STRIDE), :]
                    if ge == 0:
                        comb = jnp.where(lane < HEAD_DIM, a, pltpu.roll(b, HEAD_DIM, axis=1))
                    else:
                        comb = jnp.where(lane < HEAD_DIM, pltpu.roll(a, HEAD_DIM, axis=1), b)
                    pr = half * npair + p
                    lhs = (comb + pos_ref[kind, pr:pr + 1, :]).astype(BF16)
                    d = _dot(lhs, w1_ref[kind, pr * 128:(pr + 1) * 128, :])
                    acc = d if acc is None else acc + d
                halves.append(acc)
            hid = halves[0] + pltpu.roll(halves[1], nch - 1, axis=0)
            hid = jax.nn.gelu(hid)
            out = _dot(hid.astype(BF16), w2_ref[kind])
            if kind == 0:
                out = out * lax.rsqrt(jnp.mean(out * out, axis=-1, keepdims=True) + EPS) * g_ref[...]
            res = jnp.where(_iota(out.shape, 0) < nch - 1, out, 0.0).astype(BF16)
            if kind == 0:
                kc_ref[ge] = res
            else:
                vct_ref[ge] = _dot_nt(eye, res).astype(BF16)


def _compress(zkv, pos, w1, w2, gain, *, B, S):
    nch = S // CMP_STRIDE
    return pl.pallas_call(
        _compress_kernel,
        grid=(B, KVW // 128),
        in_specs=[
            pl.BlockSpec((None, None, S, 128), lambda b, lp: (0, lp, b, 0)),
            pl.BlockSpec((None, None, S, 128), lambda b, lp: (1, lp, b, 0)),
            _const_spec((2, CMP_STRIDE, 128)),
            _const_spec((2, CMP_LEN * HEAD_DIM, CMP_HID)),
            _const_spec((2, CMP_HID, HEAD_DIM)),
            _const_spec((1, HEAD_DIM)),
        ],
        out_specs=[
            pl.BlockSpec((None, 2, nch, HEAD_DIM), lambda b, lp: (b, lp, 0, 0)),
            pl.BlockSpec((None, 2, HEAD_DIM, nch), lambda b, lp: (b, lp, 0, 0)),
        ],
        out_shape=[
            jax.ShapeDtypeStruct((B, N_KV, nch, HEAD_DIM), BF16),
            jax.ShapeDtypeStruct((B, N_KV, HEAD_DIM, nch), BF16),
        ],
        compiler_params=_params(("parallel", "parallel")),
        name="compress",
    )(zkv, zkv, pos, w1, w2, gain)


def _spread_heads(x):
    y = x + pltpu.roll(x, 2 * Q_BLK, axis=1)
    return y + pltpu.roll(y, Q_BLK, axis=1)


def _swap_heads(x):
    n = x.shape[0]
    lane = _iota((n, 128), 1)
    halves = []
    for pr in range(x.shape[1] // 128):
        slab = jnp.concatenate([x[:, pr * 128:(pr + 1) * 128], jnp.zeros((128 - n, 128), x.dtype)], axis=0)
        tr = slab.T
        halves.append(jnp.where(lane < n, tr[0:n], pltpu.roll(tr[n:2 * n], n, axis=1)))
    return jnp.concatenate(halves, axis=1)


def _nsa_kernel(zq_ref, qg_ref, kc_ref, vct_ref, ks_ref, kw_ref, vst_ref, vwt_ref, ng_ref, o_ref,
                qtp_ref, oc_ref, bias_ref, ms_ref, accs_ref, mw_ref, accw_ref, sbuf_ref, wbuf_ref,
                gates_ref, scs_ref, *, nsel):
    qi = pl.program_id(1)
    s0 = qi * Q_BLK
    cur = qi
    ncmp = kc_ref.shape[1]
    per_chunk = KCH // SEL_LEN

    lane_t = s0 + (_iota((1, GW), 1) % Q_BLK)
    lane_grp = _iota((1, GW), 1) // Q_BLK
    ov_n = _iota((nsel, ncmp), 0) * SEL_LEN
    ov_c = _iota((nsel, ncmp), 1) * CMP_STRIDE
    overlap = jnp.where((ov_c < ov_n + SEL_LEN) & (ov_c + CMP_LEN > ov_n), 1.0, 0.0).astype(BF16)


    ok_c = _iota((ncmp, GW), 0) * CMP_STRIDE + (CMP_LEN - 1) <= lane_t
    groups = range(N_KV)
    ngt = jnp.concatenate([ng_ref[...], jnp.zeros((128 - Q_BLK, NG_PAD), F32)], axis=0).T
    ngt_hi = pltpu.roll(ngt, Q_BLK, axis=1)
    low_half = _iota((1, 128), 1) < Q_BLK
    for g in groups:
        for c in range(3):
            r = c * N_HEADS + g * HPG
            pairs = [jnp.where(low_half, ngt[r + j:r + j + 1], ngt_hi[r + j + 1:r + j + 2]) for j in (0, 2)]
            gates_ref[g, c:c + 1, :] = jnp.concatenate(pairs, axis=1)

    qts = []
    for g in groups:
        xt = _swap_heads(zq_ref[g])
        ssq = jnp.sum(xt * xt, axis=0, keepdims=True)
        qts.append((xt * lax.rsqrt(ssq * (1.0 / HEAD_DIM) + EPS) * qg_ref[...]).astype(BF16))
    scs = []
    for g in groups:
        off = (g % 2) * HEAD_DIM
        qtp_ref[g, off:off + HEAD_DIM, :] = qts[g]
        qtp_ref[g, HEAD_DIM - off:2 * HEAD_DIM - off, :] = jnp.zeros((HEAD_DIM, GW), BF16)
        scs.append(_dot(kc_ref[g], qts[g]))

    def scores(k_ref, i):
        k0 = pl.multiple_of(i * KCH, KCH)
        return [_dot(k_ref[g // 2, pl.ds(k0, KCH), :], qtp_ref[g]) for g in groups]

    def park(buf_ref, vals):
        for g in groups:
            buf_ref[g] = vals[g]

    n_chunks = qi // per_chunk + 1
    lo_w = jnp.maximum(n_chunks - (WINDOW // KCH + 1), 0)
    park(scs_ref, scs)
    park(sbuf_ref, scores(ks_ref, 0))
    park(wbuf_ref, scores(kw_ref, lo_w))

    key_row = _iota((SEL_LEN, GW), 0)
    no_shift = jnp.zeros((1, GW), F32)
    ms_ref[...] = jnp.full(ms_ref.shape, NEG, F32)
    mw_ref[...] = jnp.full(mw_ref.shape, NEG, F32)
    accs_ref[...] = jnp.zeros(accs_ref.shape, F32)
    accw_ref[...] = jnp.zeros(accw_ref.shape, F32)

    def update(g, blocks, shifts, vt, m_ref, acc_ref):
        m = m_ref[g]
        mnew = m
        for blk, sh in zip(blocks, shifts):
            mnew = jnp.maximum(mnew, jnp.max(blk, axis=0, keepdims=True) + sh)
        e = jnp.concatenate([jnp.exp2(blk + (sh - mnew)) for blk, sh in zip(blocks, shifts)], axis=0).astype(BF16)
        acc_ref[g] = jnp.exp2(m - mnew) * acc_ref[g] + _dot(vt, e)
        m_ref[g] = mnew

    def slabs(buf_ref, g, keep):
        out = []
        for nb in range(per_chunk):
            blk = buf_ref[g, nb * SEL_LEN:(nb + 1) * SEL_LEN, :]
            out.append(blk if keep is None else jnp.where(keep(nb * SEL_LEN), blk, NEG))
        return out

    def sel_softmax(i, last):
        keep = (lambda k0: key_row + (i * KCH + k0) <= lane_t) if last else None
        for g in groups:
            shifts = [bias_ref[g, pl.ds(i * per_chunk + nb, 1), :] for nb in range(per_chunk)]
            update(g, slabs(sbuf_ref, g, keep), shifts, vst_ref[i, g], ms_ref, accs_ref)

    def win_softmax(i, last):
        if last:
            keep = lambda k0: key_row + (i * KCH + k0) <= lane_t
        else:
            keep = lambda k0: key_row + (i * KCH + k0) > lane_t - WINDOW
        for g in groups:
            update(g, slabs(wbuf_ref, g, keep), [no_shift] * per_chunk, vwt_ref[i, g], mw_ref, accw_ref)

    def sel_step(i, carry):
        nxt = scores(ks_ref, i + 1)
        sel_softmax(i, False)
        park(sbuf_ref, nxt)
        return carry

    def win_step(i, carry):
        nxt = scores(kw_ref, i + 1)
        win_softmax(i, False)
        park(wbuf_ref, nxt)
        return carry

    lax.fori_loop(lo_w, n_chunks - 1, win_step, 0)

    imp_all = jnp.zeros((nsel, GW), F32)
    some_c = lane_t >= CMP_LEN - 1
    for g in groups:
        scm = jnp.where(ok_c, scs_ref[g], NEG)
        ec = jnp.exp2(scm - jnp.max(scm, axis=0, keepdims=True))
        inv = jnp.where(some_c, 1.0 / jnp.maximum(jnp.sum(ec, axis=0, keepdims=True), 1e-30), 0.0)
        both = _dot(jnp.concatenate([vct_ref[g], overlap], axis=0), ec.astype(BF16)) * inv
        oc_ref[g] = both[0:HEAD_DIM]
        imp = _spread_heads(both[HEAD_DIM:HEAD_DIM + nsel])
        imp_all = jnp.where(lane_grp == g, imp, imp_all)

    n_io = _iota((nsel, GW), 0)
    forced = (n_io == 0) | (n_io == cur) | (n_io == cur - 1)
    valid = n_io <= cur
    score = jnp.where(forced, FORCE_SCORE, imp_all)
    score = jnp.where(valid, score, -1.0)
    rows8 = [score[r:r + 8] for r in range(0, nsel, 8)]
    ranks = [jnp.zeros((8, GW), F32) for _ in rows8]
    sub = _iota((8, GW), 0)
    for m in range(nsel):
        rowv = score[m:m + 1, :]
        for k, blk in enumerate(rows8):
            ge = jnp.where(rowv >= blk, 1.0, 0.0)
            gt = jnp.where(rowv > blk, 1.0, 0.0)
            if 8 * k > m:
                inc = ge
            elif 8 * k + 7 < m:
                inc = gt
            else:
                inc = jnp.where(sub + 8 * k > m, ge, gt)
            ranks[k] = ranks[k] + inc
    rank = jnp.concatenate(ranks, axis=0)
    sel_all = jnp.where((rank < float(min(SEL_TOPN, nsel))) & valid, 1.0, 0.0)
    for g in groups:
        mine = jnp.where(lane_grp == g, sel_all, 0.0)
        bias_ref[g] = (_spread_heads(mine) - 1.0) * (-NEG)

    win_softmax(n_chunks - 1, True)

    lax.fori_loop(0, n_chunks - 1, sel_step, 0)
    sel_softmax(n_chunks - 1, True)

    outs = []
    for g in groups:
        o_s = accs_ref[g, 0:HEAD_DIM, :] * (1.0 / jnp.maximum(accs_ref[g, HEAD_DIM:HEAD_DIM + 1, :], 1e-30))
        o_w = accw_ref[g, 0:HEAD_DIM, :] * (1.0 / jnp.maximum(accw_ref[g, HEAD_DIM:HEAD_DIM + 1, :], 1e-30))
        gates = gates_ref[g]
        o_t = gates[0:1] * oc_ref[g] + gates[1:2] * o_s + gates[2:3] * o_w
        outs.append(_swap_heads(o_t))
    for g in groups:
        o_ref[g] = outs[g].astype(o_ref.dtype)


def _nsa(zq, qgain, kc, vct, kn, vt, ng, *, B, S):
    T = B * S
    nq = S // Q_BLK
    nch = S // KCH
    ncmp = S // CMP_STRIDE
    nsel = S // SEL_LEN
    tok = lambda b, i: (0, b * nq + i, 0)
    return pl.pallas_call(
        functools.partial(_nsa_kernel, nsel=nsel),
        grid=(B, nq),
        in_specs=[
            pl.BlockSpec((N_KV, Q_BLK, GW), tok),
            _const_spec((HEAD_DIM, GW)),
            pl.BlockSpec((None, N_KV, ncmp, HEAD_DIM), lambda b, i: (b, 0, 0, 0)),
            pl.BlockSpec((None, N_KV, HEAD_DIM, ncmp), lambda b, i: (b, 0, 0, 0)),
            pl.BlockSpec((None, None, 2, S, 128), lambda b, i: (0, b, 0, 0, 0)),
            pl.BlockSpec((None, None, 2, S, 128), lambda b, i: (1, b, 0, 0, 0)),
            pl.BlockSpec((None, None, nch, N_KV, VROWS, KCH), lambda b, i: (0, b, 0, 0, 0, 0)),
            pl.BlockSpec((None, None, nch, N_KV, VROWS, KCH), lambda b, i: (1, b, 0, 0, 0, 0)),
            pl.BlockSpec((Q_BLK, NG_PAD), lambda b, i: (b * nq + i, 0)),
        ],
        out_specs=pl.BlockSpec((N_KV, Q_BLK, GW), tok),
        out_shape=jax.ShapeDtypeStruct((N_KV, T, GW), BF16),
        scratch_shapes=[
            pltpu.VMEM((N_KV, 2 * HEAD_DIM, GW), BF16),
            pltpu.VMEM((N_KV, HEAD_DIM, GW), F32),
            pltpu.VMEM((N_KV, nsel, GW), F32),
            pltpu.VMEM((N_KV, 1, GW), F32),
            pltpu.VMEM((N_KV, VROWS, GW), F32),
            pltpu.VMEM((N_KV, 1, GW), F32),
            pltpu.VMEM((N_KV, VROWS, GW), F32),
            pltpu.VMEM((N_KV, KCH, GW), F32),
            pltpu.VMEM((N_KV, KCH, GW), F32),
            pltpu.VMEM((N_KV, 8, GW), F32),
            pltpu.VMEM((N_KV, ncmp, GW), F32),
        ],
        compiler_params=_params(("parallel", "arbitrary")),
        name="nsa",
    )(zq, qgain, kc, vct, kn, kn, vt, vt, ng)


def _merge_kernel(x_ref, ya_ref, o_ref_in, mg_ref, wn_ref, wo_ref, out_ref):
    yb = _dot(o_ref_in[0], wn_ref[0:GW, :])
    for g in range(1, N_KV):
        yb = yb + _dot(o_ref_in[g], wn_ref[g * GW:(g + 1) * GW, :])
    tm = x_ref.shape[0]
    ya = jnp.concatenate([ya_ref[ck].reshape(tm, 128) for ck in range(D_MODEL // 128)], axis=1)
    mixed = (mg_ref[:, 0:D_MODEL].astype(F32) * ya
             + mg_ref[:, D_MODEL:2 * D_MODEL].astype(F32) * yb)
    out_ref[...] = x_ref[...] + _dot(mixed.astype(BF16), wo_ref[...])


def _merge(x2d, ya, o, mg, wn, wo, *, B, S, tm=512):
    ni = S // tm
    tok = lambda b, i: (b * ni + i, 0)
    return pl.pallas_call(
        _merge_kernel,
        grid=(B, ni),
        in_specs=[
            pl.BlockSpec((tm, D_MODEL), tok),
            pl.BlockSpec((D_MODEL // 128, tm // 8, None, 8, 128), lambda b, i: (0, i, b, 0, 0)),
            pl.BlockSpec((N_KV, tm, GW), lambda b, i: (0, b * ni + i, 0)),
            pl.BlockSpec((tm, 2 * D_MODEL), tok),
            _const_spec((N_HEADS * HEAD_DIM, D_MODEL)),
            _const_spec((D_MODEL, D_MODEL)),
        ],
        out_specs=pl.BlockSpec((tm, D_MODEL), tok),
        out_shape=jax.ShapeDtypeStruct((B * S, D_MODEL), F32),
        compiler_params=_params(("parallel", "parallel")),
        name="merge",
    )(x2d, ya, o, mg, wn, wo)


def _ffn_kernel(x_ref, g_ref, wg_ref, wu_ref, wd_ref, o_ref, h_ref, *, tf):
    x = x_ref[...]
    y = x * lax.rsqrt(jnp.mean(x * x, axis=-1, keepdims=True) + EPS) * g_ref[...]
    h_ref[...] = y.astype(BF16)
    o_ref[...] = x
    for j in range(D_FF // tf):
        h = h_ref[...]
        sl = slice(j * tf, (j + 1) * tf)
        act = jax.nn.silu(_dot(h, wg_ref[:, sl])) * _dot(h, wu_ref[:, sl])
        o_ref[...] += _dot(act.astype(BF16), wd_ref[sl, :])


def _ffn(x2d, gain, wg, wu, wd, *, tm=512, tf=256):
    T = x2d.shape[0]
    assert T % tm == 0 and D_FF % tf == 0
    once = dict(pipeline_mode=pl.Buffered(1))
    return pl.pallas_call(
        functools.partial(_ffn_kernel, tf=tf),
        grid=(T // tm,),
        in_specs=[
            pl.BlockSpec((tm, D_MODEL), lambda i: (i, 0)),
            _const_spec((1, D_MODEL)),
            pl.BlockSpec((D_MODEL, D_FF), lambda i: (0, 0), **once),
            pl.BlockSpec((D_MODEL, D_FF), lambda i: (0, 0), **once),
            pl.BlockSpec((D_FF, D_MODEL), lambda i: (0, 0), **once),
        ],
        out_specs=pl.BlockSpec((tm, D_MODEL), lambda i: (i, 0)),
        out_shape=jax.ShapeDtypeStruct((T, D_MODEL), F32),
        scratch_shapes=[pltpu.VMEM((tm, D_MODEL), BF16)],
        compiler_params=_params(("parallel",)),
        name="ffn",
    )(x2d, gain, wg, wu, wd)


def _layer(x, p):
    B, S, _ = x.shape
    T = B * S
    dh = HEAD_DIM
    row = lambda v: v.reshape(1, -1)
    w_in = p["w_in"]
    n_ng = 3 * N_HEADS
    o_ng = 2 * D_RNN + N_HEADS * dh + 6 * KVW
    w_packed = jnp.concatenate(
        [w_in[:, :o_ng], jnp.pad(w_in[:, o_ng:o_ng + n_ng], ((0, 0), (0, NG_PAD - n_ng))), w_in[:, o_ng + n_ng:]],
        axis=1).astype(BF16)
    rx, gy, zq, zkv, ng, mg = _inproj(x, row(p["norm1"]), w_packed)

    slabs = lambda a: a.reshape(a.shape[0], S * B, 128)
    ya = _rglru(slabs(rx), slabs(gy), p["conv_w"], row(p["conv_b"]),
                p["rg_wa"].astype(BF16), row(p["rg_ba"]), p["rg_wi"].astype(BF16), row(p["rg_bi"]),
                row(p["rg_lambda"]), p["w_rg_out"].astype(BF16), nb=B)
    ya = ya.reshape(D_MODEL // 128, S // 8, B, 8, 128)

    kgains = jnp.tile(p["k_norm"][1:3], (1, N_KV)).reshape(2, 1, KVW)
    kn, vt = _kvprep(zkv, kgains, B=B, S=S)
    pos = jnp.stack([p["cmp_pos_k"], p["cmp_pos_v"]]).reshape(2, CMP_STRIDE, 2 * dh)
    w1 = jnp.stack([p["cmp_k_w1"], p["cmp_v_w1"]]).astype(BF16)
    w2 = jnp.stack([p["cmp_k_w2"], p["cmp_v_w2"]]).astype(BF16)
    kc, vct = _compress(zkv, pos, w1, w2, row(p["k_norm"][0]), B=B, S=S)
    qgain = jnp.broadcast_to((p["q_norm"] * (dh ** -0.5 * LOG2E))[:, None], (dh, GW))
    o = _nsa(zq, qgain, kc, vct, kn, vt, ng, B=B, S=S)

    x1 = _merge(x.reshape(T, D_MODEL), ya, o, mg, p["w_nsa_out"].astype(BF16), p["w_o"].astype(BF16), B=B, S=S)
    x2 = _ffn(x1, row(p["norm2"]), p["w_gate"].astype(BF16), p["w_up"].astype(BF16), p["w_down"].astype(BF16))
    return x2.reshape(B, S, D_MODEL)


def kernel(x, norm1, w_in, conv_w, conv_b, rg_wa, rg_ba, rg_wi, rg_bi, rg_lambda, q_norm, k_norm, cmp_pos_k,
           cmp_pos_v, cmp_k_w1, cmp_k_w2, cmp_v_w1, cmp_v_w2, w_rg_out, w_nsa_out, w_o, norm2, w_gate, w_up, w_down):
    params = dict(norm1=norm1, w_in=w_in, conv_w=conv_w, conv_b=conv_b, rg_wa=rg_wa, rg_ba=rg_ba, rg_wi=rg_wi,
                  rg_bi=rg_bi, rg_lambda=rg_lambda, q_norm=q_norm, k_norm=k_norm, cmp_pos_k=cmp_pos_k,
                  cmp_pos_v=cmp_pos_v, cmp_k_w1=cmp_k_w1, cmp_k_w2=cmp_k_w2, cmp_v_w1=cmp_v_w1, cmp_v_w2=cmp_v_w2,
                  w_rg_out=w_rg_out, w_nsa_out=w_nsa_out, w_o=w_o, norm2=norm2, w_gate=w_gate, w_up=w_up,
                  w_down=w_down)
    for l in range(norm1.shape[0]):
        x = _layer(x, {k: v[l] for k, v in params.items()})
    return x
```

```python
import functools

import jax
import jax.numpy as jnp
from jax import lax
from jax.experimental import pallas as pl
from jax.experimental.pallas import tpu as pltpu

D_MODEL = 1024
D_RNN = 1024
RG_BLOCKS = 4
RG_BW = D_RNN // RG_BLOCKS
CONV_W = 4
RG_C = 8.0
N_HEADS = 16
N_KV = 4
HEAD_DIM = 64
HPG = N_HEADS // N_KV
CMP_LEN = 32
CMP_STRIDE = 16
CMP_HID = 256
SEL_LEN = 64
SEL_TOPN = 16
WINDOW = 512
Q_BLK = 64
D_FF = 2816
EPS = 1e-6
FORCE_SCORE = 1e6
NEG = -1e30
LOG2E = 1.4426950408889634

KCH = 256
GW = HPG * HEAD_DIM
KVW = N_KV * HEAD_DIM
VROWS = 80
NG_PAD = 128

F32 = jnp.float32
BF16 = jnp.bfloat16

VMEM_LIMIT = 56 * 1024 * 1024


def _params(sem):
    return pltpu.CompilerParams(dimension_semantics=sem, vmem_limit_bytes=VMEM_LIMIT)


def _dot(a, b):
    return jnp.dot(a, b, preferred_element_type=F32)


def _dot_nt(a, b):
    return lax.dot_general(a, b, (((1,), (1,)), ((), ())), preferred_element_type=F32)


def _split(x):
    hi = x.astype(BF16)
    lo = (x - hi.astype(F32)).astype(BF16)
    return hi, lo


def _iota(shape, dim):
    return lax.broadcasted_iota(jnp.int32, shape, dim)


def _const_spec(shape):
    return pl.BlockSpec(shape, lambda *_: (0,) * len(shape))


_O_RY = D_RNN
_O_Q = 2 * D_RNN
_O_KV = _O_Q + N_HEADS * HEAD_DIM
_O_NG = _O_KV + 6 * KVW
_O_MG = _O_NG + NG_PAD
_W_COLS = _O_MG + 2 * D_MODEL


def _inproj_kernel(x_ref, g_ref, w_ref, rx_ref, gy_ref, zq_ref, zkv_ref, ng_ref, mg_ref, h_ref):
    x = x_ref[...]
    y = x * lax.rsqrt(jnp.mean(x * x, axis=-1, keepdims=True) + EPS) * g_ref[...]
    h_ref[...] = y.astype(BF16)

    def proj(c0, width):
        return _dot(h_ref[...], w_ref[:, c0:c0 + width])

    cw = 256
    tm = x_ref.shape[0]
    for c in range(D_RNN // cw):
        rx = proj(c * cw, cw)
        gy = jax.nn.gelu(proj(_O_RY + c * cw, cw))
        for hf in range(cw // 128):
            rx_ref[c * (cw // 128) + hf] = rx[:, hf * 128:(hf + 1) * 128].reshape(tm // 8, 8, 128)
            gy_ref[c * (cw // 128) + hf] = gy[:, hf * 128:(hf + 1) * 128].reshape(tm // 8, 8, 128)
    for g in range(N_KV):
        zq_ref[g] = proj(_O_Q + g * GW, GW)
    for s in range(6):
        z = proj(_O_KV + s * KVW, KVW)
        for pr in range(KVW // 128):
            zkv_ref[s, pr] = z[:, pr * 128:(pr + 1) * 128]
    ng_ref[...] = jax.nn.sigmoid(proj(_O_NG, NG_PAD))
    for c in range(2 * D_MODEL // cw):
        mg_ref[:, c * cw:(c + 1) * cw] = jax.nn.sigmoid(proj(_O_MG + c * cw, cw)).astype(mg_ref.dtype)


def _inproj(x, gain, w, *, tm=256):
    B, S, D = x.shape
    T = B * S
    ni = S // tm
    nck = D_RNN // 128
    tok = lambda b, i: (b * ni + i, 0)
    slab_spec = pl.BlockSpec((nck, tm // 8, None, 8, 128), lambda b, i: (0, i, b, 0, 0))
    slab_shape = jax.ShapeDtypeStruct((nck, S // 8, B, 8, 128), F32)
    return pl.pallas_call(
        _inproj_kernel,
        grid=(B, ni),
        in_specs=[
            pl.BlockSpec((None, tm, D), lambda b, i: (b, i, 0)),
            _const_spec((1, D)),
            pl.BlockSpec((D, _W_COLS), lambda b, i: (0, 0), pipeline_mode=pl.Buffered(1)),
        ],
        out_specs=[
            slab_spec,
            slab_spec,
            pl.BlockSpec((N_KV, tm, GW), lambda b, i: (0, b * ni + i, 0)),
            pl.BlockSpec((6, KVW // 128, tm, 128), lambda b, i: (0, 0, b * ni + i, 0)),
            pl.BlockSpec((tm, NG_PAD), tok),
            pl.BlockSpec((tm, 2 * D_MODEL), tok),
        ],
        out_shape=[
            slab_shape,
            slab_shape,
            jax.ShapeDtypeStruct((N_KV, T, GW), F32),
            jax.ShapeDtypeStruct((6, KVW // 128, T, 128), F32),
            jax.ShapeDtypeStruct((T, NG_PAD), F32),
            jax.ShapeDtypeStruct((T, 2 * D_MODEL), BF16),
        ],
        scratch_shapes=[pltpu.VMEM((tm, D), BF16)],
        compiler_params=_params(("parallel", "parallel")),
        name="inproj",
    )(x, gain, w)


def _rglru_kernel(x_ref, gy_ref, cw_ref, cb_ref, wa_ref, ba_ref, wi_ref, bi_ref, lam_ref, wo_ref,
                  o_ref, xb_ref, gyb_ref, a_ref, u_ref, hb_ref, hs_ref, yb_ref, *, nb, tt):
    R = nb * tt
    halo = (CONV_W - 1) * nb
    step = pl.program_id(0)
    nck = D_RNN // 128

    @pl.when(step == 0)
    def _():
        xb_ref[0:halo, :] = jnp.zeros((halo, D_RNN), F32)
        hs_ref[...] = jnp.zeros((nb, D_RNN), F32)

    def slab_rows(t8, t_lo):
        return pl.ds(t8 * (8 * nb) + t_lo, nb, stride=8)

    def stage_in(t8, carry):
        for t_lo in range(8):
            r0 = pl.multiple_of((t8 * 8 + t_lo) * nb, nb)
            for ck in range(nck):
                cols = slice(ck * 128, (ck + 1) * 128)
                xb_ref[pl.ds(halo + r0, nb), cols] = x_ref[ck, slab_rows(t8, t_lo), :]
                gyb_ref[pl.ds(r0, nb), cols] = gy_ref[ck, slab_rows(t8, t_lo), :]
        return carry

    lax.fori_loop(0, tt // 8, stage_in, 0)
    xr = cb_ref[...] + cw_ref[0:1, :] * xb_ref[0:R, :]
    for k in range(1, CONV_W):
        xr = xr + cw_ref[k:k + 1, :] * xb_ref[k * nb:k * nb + R, :]
    tail = xb_ref[R:R + halo, :]
    xb_ref[0:halo, :] = tail

    xrb = xr.astype(BF16)
    sp = jax.nn.softplus(-lam_ref[...])
    row = _iota((R, RG_BW), 0)
    first = (row < nb) & (step == 0)
    for n in range(RG_BLOCKS):
        sl = slice(n * RG_BW, (n + 1) * RG_BW)
        xn = xrb[:, sl]
        r = jax.nn.sigmoid(_dot(xn, wa_ref[n]) + ba_ref[:, sl])
        ig = jax.nn.sigmoid(_dot(xn, wi_ref[n]) + bi_ref[:, sl])
        log_a = (-RG_C) * r * sp[:, sl]
        a = jnp.exp(log_a)
        mult = jnp.sqrt(1.0 - a * a)
        mult = jnp.where(first, 1.0, mult)
        a_ref[:, sl] = a
        u_ref[:, sl] = mult * ig * xr[:, sl]

    def body(t, h):
        r0 = pl.multiple_of(t * nb, nb)
        h = a_ref[pl.ds(r0, nb), :] * h + u_ref[pl.ds(r0, nb), :]
        hb_ref[pl.ds(r0, nb), :] = h
        return h

    h = lax.fori_loop(0, tt, body, hs_ref[...], unroll=8)
    hs_ref[...] = h
    yb_ref[...] = _dot((hb_ref[...] * gyb_ref[...]).astype(BF16), wo_ref[...])

    def stage_out(t8, carry):
        for t_lo in range(8):
            r0 = pl.multiple_of((t8 * 8 + t_lo) * nb, nb)
            for ck in range(nck):
                o_ref[ck, slab_rows(t8, t_lo), :] = yb_ref[pl.ds(r0, nb), ck * 128:(ck + 1) * 128]
        return carry

    lax.fori_loop(0, tt // 8, stage_out, 0)


def _rglru(rx, gy, conv_w, conv_b, wa, ba, wi, bi, lam, wo, *, nb, tt=32):
    nck, rows, _ = rx.shape
    S = rows // nb
    assert S % tt == 0 and tt % 8 == 0
    R = nb * tt
    halo = (CONV_W - 1) * nb
    return pl.pallas_call(
        functools.partial(_rglru_kernel, nb=nb, tt=tt),
        grid=(S // tt,),
        in_specs=[
            pl.BlockSpec((nck, R, 128), lambda s: (0, s, 0)),
            pl.BlockSpec((nck, R, 128), lambda s: (0, s, 0)),
            _const_spec((CONV_W, D_RNN)),
            _const_spec((1, D_RNN)),
            _const_spec((RG_BLOCKS, RG_BW, RG_BW)),
            _const_spec((1, D_RNN)),
            _const_spec((RG_BLOCKS, RG_BW, RG_BW)),
            _const_spec((1, D_RNN)),
            _const_spec((1, D_RNN)),
            _const_spec((D_RNN, D_MODEL)),
        ],
        out_specs=pl.BlockSpec((D_MODEL // 128, R, 128), lambda s: (0, s, 0)),
        out_shape=jax.ShapeDtypeStruct((D_MODEL // 128, rows, 128), F32),
        scratch_shapes=[
            pltpu.VMEM((R + halo, D_RNN), F32),
            pltpu.VMEM((R, D_RNN), F32),
            pltpu.VMEM((R, D_RNN), F32),
            pltpu.VMEM((R, D_RNN), F32),
            pltpu.VMEM((R, D_RNN), F32),
            pltpu.VMEM((nb, D_RNN), F32),
            pltpu.VMEM((R, D_MODEL), F32),
        ],
        compiler_params=_params(("arbitrary",)),
        name="rglru",
    )(rx, gy, conv_w, conv_b, wa, ba, wi, bi, lam, wo)


def _kvprep_kernel(zk_ref, zv_ref, g_ref, kn_ref, vt_ref):
    rows = zk_ref.shape[1]
    x = jnp.concatenate([zk_ref[0], zk_ref[1]], axis=1)
    seg = jnp.where(_iota((KVW, KVW), 0) // HEAD_DIM == _iota((KVW, KVW), 1) // HEAD_DIM, 1.0, 0.0).astype(BF16)
    hi, lo = _split(x * x)
    ssq = _dot(jnp.concatenate([hi, lo], axis=0), seg)
    kn = (x * lax.rsqrt((ssq[0:rows] + ssq[rows:2 * rows]) * (1.0 / HEAD_DIM) + EPS) * g_ref[...]).astype(BF16)
    for pr in range(KVW // 128):
        kn_ref[pr] = kn[:, pr * 128:(pr + 1) * 128]
    eye = jnp.where(_iota((KVW, KVW), 0) == _iota((KVW, KVW), 1), 1.0, 0.0).astype(BF16)
    pad = jnp.where(_iota((VROWS - HEAD_DIM, KCH), 0) == 0, 1.0, 0.0).astype(BF16)
    for c in range(rows // KCH):
        sl = slice(c * KCH, (c + 1) * KCH)
        v = jnp.concatenate([zv_ref[0, sl, :], zv_ref[1, sl, :]], axis=1).astype(BF16)
        vt = _dot_nt(eye, v).astype(BF16)
        for g in range(N_KV):
            vt_ref[c, g, 0:HEAD_DIM, :] = vt[g * HEAD_DIM:(g + 1) * HEAD_DIM]
            vt_ref[c, g, HEAD_DIM:VROWS, :] = pad


def _kvprep(zkv, gains, *, B, S, cpb=4):
    nch = S // KCH
    assert nch % cpb == 0
    nblk = nch // cpb
    rows = cpb * KCH
    return pl.pallas_call(
        _kvprep_kernel,
        grid=(2, B, nblk),
        in_specs=[
            pl.BlockSpec((None, KVW // 128, rows, 128), lambda w, b, i: (2 + 2 * w, 0, b * nblk + i, 0)),
            pl.BlockSpec((None, KVW // 128, rows, 128), lambda w, b, i: (3 + 2 * w, 0, b * nblk + i, 0)),
            pl.BlockSpec((None, 1, KVW), lambda w, b, i: (w, 0, 0)),
        ],
        out_specs=[
            pl.BlockSpec((None, None, KVW // 128, rows, 128), lambda w, b, i: (w, b, 0, i, 0)),
            pl.BlockSpec((None, None, cpb, N_KV, VROWS, KCH), lambda w, b, i: (w, b, i, 0, 0, 0)),
        ],
        out_shape=[
            jax.ShapeDtypeStruct((2, B, KVW // 128, S, 128), BF16),
            jax.ShapeDtypeStruct((2, B, nch, N_KV, VROWS, KCH), BF16),
        ],
        compiler_params=_params(("parallel", "parallel", "parallel")),
        name="kvprep",
    )(zkv, zkv, gains)


def _compress_kernel(zk_ref, zv_ref, pos_ref, w1_ref, w2_ref, g_ref, kc_ref, vct_ref):
    nch = zk_ref.shape[0] // CMP_STRIDE
    lane = _iota((nch, 128), 1)
    npair = CMP_STRIDE // 2
    eye = jnp.where(_iota((HEAD_DIM, HEAD_DIM), 0) == _iota((HEAD_DIM, HEAD_DIM), 1), 1.0, 0.0).astype(BF16)
    for kind, z_ref in enumerate((zk_ref, zv_ref)):
        for ge in range(2):
            halves = []
            for half in range(2):
                acc = None
                for p in range(npair):
                    a = z_ref[pl.ds(2 * p, nch, stride=CMP_STRIDE), :]
                    b = z_ref[pl.ds(2 * p + 1, nch, stride=CMP_STRIDE), :]
                    if ge == 0:
                        comb = jnp.where(lane < HEAD_DIM, a, pltpu.roll(b, HEAD_DIM, axis=1))
                    else:
                        comb = jnp.where(lane < HEAD_DIM, pltpu.roll(a, HEAD_DIM, axis=1), b)
                    pr = half * npair + p
                    lhs = (comb + pos_ref[kind, pr:pr + 1, :]).astype(BF16)
                    d = _dot(lhs, w1_ref[kind, pr * 128:(pr + 1) * 128, :])
                    acc = d if acc is None else acc + d
                halves.append(acc)
            hid = halves[0] + pltpu.roll(halves[1], nch - 1, axis=0)
            hid = jax.nn.gelu(hid)
            out = _dot(hid.astype(BF16), w2_ref[kind])
            if kind == 0:
                out = out * lax.rsqrt(jnp.mean(out * out, axis=-1, keepdims=True) + EPS) * g_ref[...]
            res = jnp.where(_iota(out.shape, 0) < nch - 1, out, 0.0).astype(BF16)
            if kind == 0:
                kc_ref[ge] = res
            else:
                vct_ref[ge] = _dot_nt(eye, res).astype(BF16)


def _compress(zkv, pos, w1, w2, gain, *, B, S):
    nch = S // CMP_STRIDE
    return pl.pallas_call(
        _compress_kernel,
        grid=(B, KVW // 128),
        in_specs=[
            pl.BlockSpec((None, None, S, 128), lambda b, lp: (0, lp, b, 0)),
            pl.BlockSpec((None, None, S, 128), lambda b, lp: (1, lp, b, 0)),
            _const_spec((2, CMP_STRIDE, 128)),
            _const_spec((2, CMP_LEN * HEAD_DIM, CMP_HID)),
            _const_spec((2, CMP_HID, HEAD_DIM)),
            _const_spec((1, HEAD_DIM)),
        ],
        out_specs=[
            pl.BlockSpec((None, 2, nch, HEAD_DIM), lambda b, lp: (b, lp, 0, 0)),
            pl.BlockSpec((None, 2, HEAD_DIM, nch), lambda b, lp: (b, lp, 0, 0)),
        ],
        out_shape=[
            jax.ShapeDtypeStruct((B, N_KV, nch, HEAD_DIM), BF16),
            jax.ShapeDtypeStruct((B, N_KV, HEAD_DIM, nch), BF16),
        ],
        compiler_params=_params(("parallel", "parallel")),
        name="compress",
    )(zkv, zkv, pos, w1, w2, gain)


def _spread_heads(x):
    y = x + pltpu.roll(x, 2 * Q_BLK, axis=1)
    return y + pltpu.roll(y, Q_BLK, axis=1)


def _swap_heads(x):
    n = x.shape[0]
    lane = _iota((n, 128), 1)
    halves = []
    for pr in range(x.shape[1] // 128):
        slab = jnp.concatenate([x[:, pr * 128:(pr + 1) * 128], jnp.zeros((128 - n, 128), x.dtype)], axis=0)
        tr = slab.T
        halves.append(jnp.where(lane < n, tr[0:n], pltpu.roll(tr[n:2 * n], n, axis=1)))
    return jnp.concatenate(halves, axis=1)


def _nsa_kernel(*refs, nsel, qpb):
    def two_blocks(p, carry):
        _nsa_block(2 * p, *refs, nsel=nsel, qpb=qpb)
        _nsa_block(2 * p + 1, *refs, nsel=nsel, qpb=qpb)
        return carry

    lax.fori_loop(0, qpb // 2, two_blocks, 0)


def _nsa_block(k, zq_ref, qg_ref, kc_ref, vct_ref, ks_ref, kw_ref, vst_ref, vwt_ref, ng_ref, o_ref,
               qtp_ref, oc_ref, bias_ref, ms_ref, accs_ref, mw_ref, accw_ref, sbuf_ref, wbuf_ref,
               gates_ref, scs_ref, *, nsel, qpb):
    qi = pl.program_id(1) * qpb + k
    q_rows = pl.ds(pl.multiple_of(k * Q_BLK, Q_BLK), Q_BLK)
    s0 = qi * Q_BLK
    cur = qi
    ncmp = kc_ref.shape[1]
    per_chunk = KCH // SEL_LEN

    lane_t = s0 + (_iota((1, GW), 1) % Q_BLK)
    lane_grp = _iota((1, GW), 1) // Q_BLK
    ov_n = _iota((nsel, ncmp), 0) * SEL_LEN
    ov_c = _iota((nsel, ncmp), 1) * CMP_STRIDE
    overlap = jnp.where((ov_c < ov_n + SEL_LEN) & (ov_c + CMP_LEN > ov_n), 1.0, 0.0).astype(BF16)


    ok_c = _iota((ncmp, GW), 0) * CMP_STRIDE + (CMP_LEN - 1) <= lane_t
    groups = range(N_KV)
    ngt = jnp.concatenate([ng_ref[q_rows, :], jnp.zeros((128 - Q_BLK, NG_PAD), F32)], axis=0).T
    ngt_hi = pltpu.roll(ngt, Q_BLK, axis=1)
    low_half = _iota((1, 128), 1) < Q_BLK
    for g in groups:
        for c in range(3):
            r = c * N_HEADS + g * HPG
            pairs = [jnp.where(low_half, ngt[r + j:r + j + 1], ngt_hi[r + j + 1:r + j + 2]) for j in (0, 2)]
            gates_ref[g, c:c + 1, :] = jnp.concatenate(pairs, axis=1)

    qts = []
    for g in groups:
        xt = _swap_heads(zq_ref[g, q_rows, :])
        ssq = jnp.sum(xt * xt, axis=0, keepdims=True)
        qts.append((xt * lax.rsqrt(ssq * (1.0 / HEAD_DIM) + EPS) * qg_ref[...]).astype(BF16))
    scs = []
    for g in groups:
        off = (g % 2) * HEAD_DIM
        qtp_ref[g, off:off + HEAD_DIM, :] = qts[g]
        qtp_ref[g, HEAD_DIM - off:2 * HEAD_DIM - off, :] = jnp.zeros((HEAD_DIM, GW), BF16)
        scs.append(_dot(kc_ref[g], qts[g]))

    def scores(k_ref, i):
        k0 = pl.multiple_of(i * KCH, KCH)
        return [_dot(k_ref[g // 2, pl.ds(k0, KCH), :], qtp_ref[g]) for g in groups]

    def park(buf_ref, vals):
        for g in groups:
            buf_ref[g] = vals[g]

    n_chunks = qi // per_chunk + 1
    lo_w = jnp.maximum(n_chunks - (WINDOW // KCH + 1), 0)
    park(scs_ref, scs)
    park(sbuf_ref, scores(ks_ref, 0))
    park(wbuf_ref, scores(kw_ref, lo_w))

    key_row = _iota((SEL_LEN, GW), 0)
    no_shift = jnp.zeros((1, GW), F32)
    ms_ref[...] = jnp.full(ms_ref.shape, NEG, F32)
    mw_ref[...] = jnp.full(mw_ref.shape, NEG, F32)
    accs_ref[...] = jnp.zeros(accs_ref.shape, F32)
    accw_ref[...] = jnp.zeros(accw_ref.shape, F32)

    def update(g, blocks, shifts, vt, m_ref, acc_ref):
        m = m_ref[g]
        mnew = m
        for blk, sh in zip(blocks, shifts):
            mnew = jnp.maximum(mnew, jnp.max(blk, axis=0, keepdims=True) + sh)
        e = jnp.concatenate([jnp.exp2(blk + (sh - mnew)) for blk, sh in zip(blocks, shifts)], axis=0).astype(BF16)
        acc_ref[g] = jnp.exp2(m - mnew) * acc_ref[g] + _dot(vt, e)
        m_ref[g] = mnew

    def slabs(buf_ref, g, keep):
        out = []
        for nb in range(per_chunk):
            blk = buf_ref[g, nb * SEL_LEN:(nb + 1) * SEL_LEN, :]
            out.append(blk if keep is None else jnp.where(keep(nb * SEL_LEN), blk, NEG))
        return out

    def sel_softmax(i, last):
        keep = (lambda k0: key_row + (i * KCH + k0) <= lane_t) if last else None
        for g in groups:
            shifts = [bias_ref[g, pl.ds(i * per_chunk + nb, 1), :] for nb in range(per_chunk)]
            update(g, slabs(sbuf_ref, g, keep), shifts, vst_ref[i, g], ms_ref, accs_ref)

    def win_softmax(i, last):
        if last:
            keep = lambda k0: key_row + (i * KCH + k0) <= lane_t
        else:
            keep = lambda k0: key_row + (i * KCH + k0) > lane_t - WINDOW
        for g in groups:
            update(g, slabs(wbuf_ref, g, keep), [no_shift] * per_chunk, vwt_ref[i, g], mw_ref, accw_ref)

    def sel_step(i, carry):
        nxt = scores(ks_ref, i + 1)
        sel_softmax(i, False)
        park(sbuf_ref, nxt)
        return carry

    def win_step(i, carry):
        nxt = scores(kw_ref, i + 1)
        win_softmax(i, False)
        park(wbuf_ref, nxt)
        return carry

    lax.fori_loop(lo_w, n_chunks - 1, win_step, 0)

    imp_all = jnp.zeros((nsel, GW), F32)
    some_c = lane_t >= CMP_LEN - 1
    for g in groups:
        scm = jnp.where(ok_c, scs_ref[g], NEG)
        ec = jnp.exp2(scm - jnp.max(scm, axis=0, keepdims=True))
        inv = jnp.where(some_c, 1.0 / jnp.maximum(jnp.sum(ec, axis=0, keepdims=True), 1e-30), 0.0)
        both = _dot(jnp.concatenate([vct_ref[g], overlap], axis=0), ec.astype(BF16)) * inv
        oc_ref[g] = both[0:HEAD_DIM]
        imp = _spread_heads(both[HEAD_DIM:HEAD_DIM + nsel])
        imp_all = jnp.where(lane_grp == g, imp, imp_all)

    n_io = _iota((nsel, GW), 0)
    forced = (n_io == 0) | (n_io == cur) | (n_io == cur - 1)
    valid = n_io <= cur
    score = jnp.where(forced, FORCE_SCORE, imp_all)
    score = jnp.where(valid, score, -1.0)
    rows8 = [score[r:r + 8] for r in range(0, nsel, 8)]
    ranks = [jnp.zeros((8, GW), F32) for _ in rows8]
    sub = _iota((8, GW), 0)
    for m in range(nsel):
        rowv = score[m:m + 1, :]
        for k, blk in enumerate(rows8):
            ge = jnp.where(rowv >= blk, 1.0, 0.0)
            gt = jnp.where(rowv > blk, 1.0, 0.0)
            if 8 * k > m:
                inc = ge
            elif 8 * k + 7 < m:
                inc = gt
            else:
                inc = jnp.where(sub + 8 * k > m, ge, gt)
            ranks[k] = ranks[k] + inc
    rank = jnp.concatenate(ranks, axis=0)
    sel_all = jnp.where((rank < float(min(SEL_TOPN, nsel))) & valid, 1.0, 0.0)
    for g in groups:
        mine = jnp.where(lane_grp == g, sel_all, 0.0)
        bias_ref[g] = (_spread_heads(mine) - 1.0) * (-NEG)

    win_softmax(n_chunks - 1, True)

    lax.fori_loop(0, n_chunks - 1, sel_step, 0)
    sel_softmax(n_chunks - 1, True)

    outs = []
    for g in groups:
        o_s = accs_ref[g, 0:HEAD_DIM, :] * (1.0 / jnp.maximum(accs_ref[g, HEAD_DIM:HEAD_DIM + 1, :], 1e-30))
        o_w = accw_ref[g, 0:HEAD_DIM, :] * (1.0 / jnp.maximum(accw_ref[g, HEAD_DIM:HEAD_DIM + 1, :], 1e-30))
        gates = gates_ref[g]
        o_t = gates[0:1] * oc_ref[g] + gates[1:2] * o_s + gates[2:3] * o_w
        outs.append(_swap_heads(o_t))
    for g in groups:
        o_ref[g, q_rows, :] = outs[g].astype(o_ref.dtype)


def _nsa(zq, qgain, kc, vct, kn, vt, ng, *, B, S, qpb=4):
    T = B * S
    assert S % (Q_BLK * qpb) == 0
    nq = S // (Q_BLK * qpb)
    nch = S // KCH
    ncmp = S // CMP_STRIDE
    nsel = S // SEL_LEN
    tok = lambda b, i: (0, b * nq + i, 0)
    return pl.pallas_call(
        functools.partial(_nsa_kernel, nsel=nsel, qpb=qpb),
        grid=(B, nq),
        in_specs=[
            pl.BlockSpec((N_KV, Q_BLK * qpb, GW), tok),
            _const_spec((HEAD_DIM, GW)),
            pl.BlockSpec((None, N_KV, ncmp, HEAD_DIM), lambda b, i: (b, 0, 0, 0)),
            pl.BlockSpec((None, N_KV, HEAD_DIM, ncmp), lambda b, i: (b, 0, 0, 0)),
            pl.BlockSpec((None, None, 2, S, 128), lambda b, i: (0, b, 0, 0, 0)),
            pl.BlockSpec((None, None, 2, S, 128), lambda b, i: (1, b, 0, 0, 0)),
            pl.BlockSpec((None, None, nch, N_KV, VROWS, KCH), lambda b, i: (0, b, 0, 0, 0, 0)),
            pl.BlockSpec((None, None, nch, N_KV, VROWS, KCH), lambda b, i: (1, b, 0, 0, 0, 0)),
            pl.BlockSpec((Q_BLK * qpb, NG_PAD), lambda b, i: (b * nq + i, 0)),
        ],
        out_specs=pl.BlockSpec((N_KV, Q_BLK * qpb, GW), tok),
        out_shape=jax.ShapeDtypeStruct((N_KV, T, GW), BF16),
        scratch_shapes=[
            pltpu.VMEM((N_KV, 2 * HEAD_DIM, GW), BF16),
            pltpu.VMEM((N_KV, HEAD_DIM, GW), F32),
            pltpu.VMEM((N_KV, nsel, GW), F32),
            pltpu.VMEM((N_KV, 1, GW), F32),
            pltpu.VMEM((N_KV, VROWS, GW), F32),
            pltpu.VMEM((N_KV, 1, GW), F32),
            pltpu.VMEM((N_KV, VROWS, GW), F32),
            pltpu.VMEM((N_KV, KCH, GW), F32),
            pltpu.VMEM((N_KV, KCH, GW), F32),
            pltpu.VMEM((N_KV, 8, GW), F32),
            pltpu.VMEM((N_KV, ncmp, GW), F32),
        ],
        compiler_params=_params(("parallel", "arbitrary")),
        name="nsa",
    )(zq, qgain, kc, vct, kn, kn, vt, vt, ng)


def _merge_kernel(x_ref, ya_ref, o_ref_in, mg_ref, wn_ref, wo_ref, out_ref):
    yb = _dot(o_ref_in[0], wn_ref[0:GW, :])
    for g in range(1, N_KV):
        yb = yb + _dot(o_ref_in[g], wn_ref[g * GW:(g + 1) * GW, :])
    tm = x_ref.shape[0]
    ya = jnp.concatenate([ya_ref[ck].reshape(tm, 128) for ck in range(D_MODEL // 128)], axis=1)
    mixed = (mg_ref[:, 0:D_MODEL].astype(F32) * ya
             + mg_ref[:, D_MODEL:2 * D_MODEL].astype(F32) * yb)
    out_ref[...] = x_ref[...] + _dot(mixed.astype(BF16), wo_ref[...])


def _merge(x2d, ya, o, mg, wn, wo, *, B, S, tm=512):
    ni = S // tm
    tok = lambda b, i: (b * ni + i, 0)
    return pl.pallas_call(
        _merge_kernel,
        grid=(B, ni),
        in_specs=[
            pl.BlockSpec((tm, D_MODEL), tok),
            pl.BlockSpec((D_MODEL // 128, tm // 8, None, 8, 128), lambda b, i: (0, i, b, 0, 0)),
            pl.BlockSpec((N_KV, tm, GW), lambda b, i: (0, b * ni + i, 0)),
            pl.BlockSpec((tm, 2 * D_MODEL), tok),
            _const_spec((N_HEADS * HEAD_DIM, D_MODEL)),
            _const_spec((D_MODEL, D_MODEL)),
        ],
        out_specs=pl.BlockSpec((tm, D_MODEL), tok),
        out_shape=jax.ShapeDtypeStruct((B * S, D_MODEL), F32),
        compiler_params=_params(("parallel", "parallel")),
        name="merge",
    )(x2d, ya, o, mg, wn, wo)


def _ffn_kernel(x_ref, g_ref, wg_ref, wu_ref, wd_ref, o_ref, h_ref, *, tf):
    x = x_ref[...]
    y = x * lax.rsqrt(jnp.mean(x * x, axis=-1, keepdims=True) + EPS) * g_ref[...]
    h_ref[...] = y.astype(BF16)
    o_ref[...] = x
    for j in range(D_FF // tf):
        h = h_ref[...]
        sl = slice(j * tf, (j + 1) * tf)
        act = jax.nn.silu(_dot(h, wg_ref[:, sl])) * _dot(h, wu_ref[:, sl])
        o_ref[...] += _dot(act.astype(BF16), wd_ref[sl, :])


def _ffn(x2d, gain, wg, wu, wd, *, tm=512, tf=256):
    T = x2d.shape[0]
    assert T % tm == 0 and D_FF % tf == 0
    once = dict(pipeline_mode=pl.Buffered(1))
    return pl.pallas_call(
        functools.partial(_ffn_kernel, tf=tf),
        grid=(T // tm,),
        in_specs=[
            pl.BlockSpec((tm, D_MODEL), lambda i: (i, 0)),
            _const_spec((1, D_MODEL)),
            pl.BlockSpec((D_MODEL, D_FF), lambda i: (0, 0), **once),
            pl.BlockSpec((D_MODEL, D_FF), lambda i: (0, 0), **once),
            pl.BlockSpec((D_FF, D_MODEL), lambda i: (0, 0), **once),
        ],
        out_specs=pl.BlockSpec((tm, D_MODEL), lambda i: (i, 0)),
        out_shape=jax.ShapeDtypeStruct((T, D_MODEL), F32),
        scratch_shapes=[pltpu.VMEM((tm, D_MODEL), BF16)],
        compiler_params=_params(("parallel",)),
        name="ffn",
    )(x2d, gain, wg, wu, wd)


def _layer(x, p):
    B, S, _ = x.shape
    T = B * S
    dh = HEAD_DIM
    row = lambda v: v.reshape(1, -1)
    w_in = p["w_in"]
    n_ng = 3 * N_HEADS
    o_ng = 2 * D_RNN + N_HEADS * dh + 6 * KVW
    w_packed = jnp.concatenate(
        [w_in[:, :o_ng], jnp.pad(w_in[:, o_ng:o_ng + n_ng], ((0, 0), (0, NG_PAD - n_ng))), w_in[:, o_ng + n_ng:]],
        axis=1).astype(BF16)
    rx, gy, zq, zkv, ng, mg = _inproj(x, row(p["norm1"]), w_packed)

    slabs = lambda a: a.reshape(a.shape[0], S * B, 128)
    ya = _rglru(slabs(rx), slabs(gy), p["conv_w"], row(p["conv_b"]),
                p["rg_wa"].astype(BF16), row(p["rg_ba"]), p["rg_wi"].astype(BF16), row(p["rg_bi"]),
                row(p["rg_lambda"]), p["w_rg_out"].astype(BF16), nb=B)
    ya = ya.reshape(D_MODEL // 128, S // 8, B, 8, 128)

    kgains = jnp.tile(p["k_norm"][1:3], (1, N_KV)).reshape(2, 1, KVW)
    kn, vt = _kvprep(zkv, kgains, B=B, S=S)
    pos = jnp.stack([p["cmp_pos_k"], p["cmp_pos_v"]]).reshape(2, CMP_STRIDE, 2 * dh)
    w1 = jnp.stack([p["cmp_k_w1"], p["cmp_v_w1"]]).astype(BF16)
    w2 = jnp.stack([p["cmp_k_w2"], p["cmp_v_w2"]]).astype(BF16)
    kc, vct = _compress(zkv, pos, w1, w2, row(p["k_norm"][0]), B=B, S=S)
    qgain = jnp.broadcast_to((p["q_norm"] * (dh ** -0.5 * LOG2E))[:, None], (dh, GW))
    o = _nsa(zq, qgain, kc, vct, kn, vt, ng, B=B, S=S)

    x1 = _merge(x.reshape(T, D_MODEL), ya, o, mg, p["w_nsa_out"].astype(BF16), p["w_o"].astype(BF16), B=B, S=S)
    x2 = _ffn(x1, row(p["norm2"]), p["w_gate"].astype(BF16), p["w_up"].astype(BF16), p["w_down"].astype(BF16))
    return x2.reshape(B, S, D_MODEL)


def kernel(x, norm1, w_in, conv_w, conv_b, rg_wa, rg_ba, rg_wi, rg_bi, rg_lambda, q_norm, k_norm, cmp_pos_k,
           cmp_pos_v, cmp_k_w1, cmp_k_w2, cmp_v_w1, cmp_v_w2, w_rg_out, w_nsa_out, w_o, norm2, w_gate, w_up, w_down):
    params = dict(norm1=norm1, w_in=w_in, conv_w=conv_w, conv_b=conv_b, rg_wa=rg_wa, rg_ba=rg_ba, rg_wi=rg_wi,
                  rg_bi=rg_bi, rg_lambda=rg_lambda, q_norm=q_norm, k_norm=k_norm, cmp_pos_k=cmp_pos_k,
                  cmp_pos_v=cmp_pos_v, cmp_k_w1=cmp_k_w1, cmp_k_w2=cmp_k_w2, cmp_v_w1=cmp_v_w1, cmp_v_w2=cmp_v_w2,
                  w_rg_out=w_rg_out, w_nsa_out=w_nsa_out, w_o=w_o, norm2=norm2, w_gate=w_gate, w_up=w_up,
                  w_down=w_down)
    for l in range(norm1.shape[0]):
        x = _layer(x, {k: v[l] for k, v in params.items()})
    return x
```

```python
import functools

import jax
import jax.numpy as jnp
from jax import lax
from jax.experimental import pallas as pl
from jax.experimental.pallas import tpu as pltpu

D_MODEL = 1024
D_RNN = 1024
RG_BLOCKS = 4
RG_BW = D_RNN // RG_BLOCKS
CONV_W = 4
RG_C = 8.0
N_HEADS = 16
N_KV = 4
HEAD_DIM = 64
HPG = N_HEADS // N_KV
CMP_LEN = 32
CMP_STRIDE = 16
CMP_HID = 256
SEL_LEN = 64
SEL_TOPN = 16
WINDOW = 512
Q_BLK = 64
D_FF = 2816
EPS = 1e-6
FORCE_SCORE = 1e6
NEG = -1e30
LOG2E = 1.4426950408889634

KCH = 256
GW = HPG * HEAD_DIM
KVW = N_KV * HEAD_DIM
VROWS = 80
NG_PAD = 128

F32 = jnp.float32
BF16 = jnp.bfloat16

VMEM_LIMIT = 56 * 1024 * 1024


def _params(sem):
    return pltpu.CompilerParams(dimension_semantics=sem, vmem_limit_bytes=VMEM_LIMIT)


def _dot(a, b):
    return jnp.dot(a, b, preferred_element_type=F32)


def _dot_nt(a, b):
    return lax.dot_general(a, b, (((1,), (1,)), ((), ())), preferred_element_type=F32)


def _split(x):
    hi = x.astype(BF16)
    lo = (x - hi.astype(F32)).astype(BF16)
    return hi, lo


def _iota(shape, dim):
    return lax.broadcasted_iota(jnp.int32, shape, dim)


def _const_spec(shape):
    return pl.BlockSpec(shape, lambda *_: (0,) * len(shape))


_O_RY = D_RNN
_O_Q = 2 * D_RNN
_O_KV = _O_Q + N_HEADS * HEAD_DIM
_O_NG = _O_KV + 6 * KVW
_O_MG = _O_NG + NG_PAD
_W_COLS = _O_MG + 2 * D_MODEL


def _inproj_kernel(x_ref, g_ref, w_ref, rx_ref, gy_ref, zq_ref, zkv_ref, ng_ref, mg_ref, h_ref):
    x = x_ref[...]
    y = x * lax.rsqrt(jnp.mean(x * x, axis=-1, keepdims=True) + EPS) * g_ref[...]
    h_ref[...] = y.astype(BF16)

    def proj(c0, width):
        return _dot(h_ref[...], w_ref[:, c0:c0 + width])

    cw = 256
    tm = x_ref.shape[0]
    for c in range(D_RNN // cw):
        rx = proj(c * cw, cw)
        gy = jax.nn.gelu(proj(_O_RY + c * cw, cw))
        for hf in range(cw // 128):
            rx_ref[c * (cw // 128) + hf] = rx[:, hf * 128:(hf + 1) * 128].reshape(tm // 8, 8, 128)
            gy_ref[c * (cw // 128) + hf] = gy[:, hf * 128:(hf + 1) * 128].reshape(tm // 8, 8, 128)
    for g in range(N_KV):
        zq_ref[g] = proj(_O_Q + g * GW, GW)
    for s in range(6):
        z = proj(_O_KV + s * KVW, KVW)
        for pr in range(KVW // 128):
            zkv_ref[s, pr] = z[:, pr * 128:(pr + 1) * 128]
    ng_ref[...] = jax.nn.sigmoid(proj(_O_NG, NG_PAD))
    for c in range(2 * D_MODEL // cw):
        mg_ref[:, c * cw:(c + 1) * cw] = jax.nn.sigmoid(proj(_O_MG + c * cw, cw)).astype(mg_ref.dtype)


def _inproj(x, gain, w, *, tm=256):
    B, S, D = x.shape
    T = B * S
    ni = S // tm
    nck = D_RNN // 128
    tok = lambda b, i: (b * ni + i, 0)
    slab_spec = pl.BlockSpec((nck, tm // 8, None, 8, 128), lambda b, i: (0, i, b, 0, 0))
    slab_shape = jax.ShapeDtypeStruct((nck, S // 8, B, 8, 128), F32)
    return pl.pallas_call(
        _inproj_kernel,
        grid=(B, ni),
        in_specs=[
            pl.BlockSpec((None, tm, D), lambda b, i: (b, i, 0)),
            _const_spec((1, D)),
            pl.BlockSpec((D, _W_COLS), lambda b, i: (0, 0), pipeline_mode=pl.Buffered(1)),
        ],
        out_specs=[
            slab_spec,
            slab_spec,
            pl.BlockSpec((N_KV, tm, GW), lambda b, i: (0, b * ni + i, 0)),
            pl.BlockSpec((6, KVW // 128, tm, 128), lambda b, i: (0, 0, b * ni + i, 0)),
            pl.BlockSpec((tm, NG_PAD), tok),
            pl.BlockSpec((tm, 2 * D_MODEL), tok),
        ],
        out_shape=[
            slab_shape,
            slab_shape,
            jax.ShapeDtypeStruct((N_KV, T, GW), F32),
            jax.ShapeDtypeStruct((6, KVW // 128, T, 128), F32),
            jax.ShapeDtypeStruct((T, NG_PAD), F32),
            jax.ShapeDtypeStruct((T, 2 * D_MODEL), BF16),
        ],
        scratch_shapes=[pltpu.VMEM((tm, D), BF16)],
        compiler_params=_params(("parallel", "parallel")),
        name="inproj",
    )(x, gain, w)


def _rglru_kernel(x_ref, gy_ref, cw_ref, cb_ref, wa_ref, ba_ref, wi_ref, bi_ref, lam_ref, wo_ref,
                  o_ref, xb_ref, gyb_ref, a_ref, u_ref, hb_ref, hs_ref, yb_ref, *, nb, tt):
    R = nb * tt
    halo = (CONV_W - 1) * nb
    step = pl.program_id(0)
    nck = D_RNN // 128

    @pl.when(step == 0)
    def _():
        xb_ref[0:halo, :] = jnp.zeros((halo, D_RNN), F32)
        hs_ref[...] = jnp.zeros((nb, D_RNN), F32)

    def slab_rows(t8, t_lo):
        return pl.ds(t8 * (8 * nb) + t_lo, nb, stride=8)

    def stage_in(t8, carry):
        for t_lo in range(8):
            r0 = pl.multiple_of((t8 * 8 + t_lo) * nb, nb)
            for ck in range(nck):
                cols = slice(ck * 128, (ck + 1) * 128)
                xb_ref[pl.ds(halo + r0, nb), cols] = x_ref[ck, slab_rows(t8, t_lo), :]
                gyb_ref[pl.ds(r0, nb), cols] = gy_ref[ck, slab_rows(t8, t_lo), :]
        return carry

    lax.fori_loop(0, tt // 8, stage_in, 0)
    xr = cb_ref[...] + cw_ref[0:1, :] * xb_ref[0:R, :]
    for k in range(1, CONV_W):
        xr = xr + cw_ref[k:k + 1, :] * xb_ref[k * nb:k * nb + R, :]
    tail = xb_ref[R:R + halo, :]
    xb_ref[0:halo, :] = tail

    xrb = xr.astype(BF16)
    sp = jax.nn.softplus(-lam_ref[...])
    row = _iota((R, RG_BW), 0)
    first = (row < nb) & (step == 0)
    for n in range(RG_BLOCKS):
        sl = slice(n * RG_BW, (n + 1) * RG_BW)
        xn = xrb[:, sl]
        r = jax.nn.sigmoid(_dot(xn, wa_ref[n]) + ba_ref[:, sl])
        ig = jax.nn.sigmoid(_dot(xn, wi_ref[n]) + bi_ref[:, sl])
        log_a = (-RG_C) * r * sp[:, sl]
        a = jnp.exp(log_a)
        mult = jnp.sqrt(1.0 - a * a)
        mult = jnp.where(first, 1.0, mult)
        a_ref[:, sl] = a
        u_ref[:, sl] = mult * ig * xr[:, sl]

    def body(t, h):
        r0 = pl.multiple_of(t * nb, nb)
        h = a_ref[pl.ds(r0, nb), :] * h + u_ref[pl.ds(r0, nb), :]
        hb_ref[pl.ds(r0, nb), :] = h
        return h

    h = lax.fori_loop(0, tt, body, hs_ref[...], unroll=8)
    hs_ref[...] = h
    yb_ref[...] = _dot((hb_ref[...] * gyb_ref[...]).astype(BF16), wo_ref[...])

    def stage_out(t8, carry):
        for t_lo in range(8):
            r0 = pl.multiple_of((t8 * 8 + t_lo) * nb, nb)
            for ck in range(nck):
                o_ref[ck, slab_rows(t8, t_lo), :] = yb_ref[pl.ds(r0, nb), ck * 128:(ck + 1) * 128]
        return carry

    lax.fori_loop(0, tt // 8, stage_out, 0)


def _rglru(rx, gy, conv_w, conv_b, wa, ba, wi, bi, lam, wo, *, nb, tt=32):
    nck, rows, _ = rx.shape
    S = rows // nb
    assert S % tt == 0 and tt % 8 == 0
    R = nb * tt
    halo = (CONV_W - 1) * nb
    return pl.pallas_call(
        functools.partial(_rglru_kernel, nb=nb, tt=tt),
        grid=(S // tt,),
        in_specs=[
            pl.BlockSpec((nck, R, 128), lambda s: (0, s, 0)),
            pl.BlockSpec((nck, R, 128), lambda s: (0, s, 0)),
            _const_spec((CONV_W, D_RNN)),
            _const_spec((1, D_RNN)),
            _const_spec((RG_BLOCKS, RG_BW, RG_BW)),
            _const_spec((1, D_RNN)),
            _const_spec((RG_BLOCKS, RG_BW, RG_BW)),
            _const_spec((1, D_RNN)),
            _const_spec((1, D_RNN)),
            _const_spec((D_RNN, D_MODEL)),
        ],
        out_specs=pl.BlockSpec((D_MODEL // 128, R, 128), lambda s: (0, s, 0)),
        out_shape=jax.ShapeDtypeStruct((D_MODEL // 128, rows, 128), F32),
        scratch_shapes=[
            pltpu.VMEM((R + halo, D_RNN), F32),
            pltpu.VMEM((R, D_RNN), F32),
            pltpu.VMEM((R, D_RNN), F32),
            pltpu.VMEM((R, D_RNN), F32),
            pltpu.VMEM((R, D_RNN), F32),
            pltpu.VMEM((nb, D_RNN), F32),
            pltpu.VMEM((R, D_MODEL), F32),
        ],
        compiler_params=_params(("arbitrary",)),
        name="rglru",
    )(rx, gy, conv_w, conv_b, wa, ba, wi, bi, lam, wo)


def _kvprep_kernel(zk_ref, zv_ref, g_ref, kn_ref, vt_ref):
    rows = zk_ref.shape[1]
    x = jnp.concatenate([zk_ref[0], zk_ref[1]], axis=1)
    seg = jnp.where(_iota((KVW, KVW), 0) // HEAD_DIM == _iota((KVW, KVW), 1) // HEAD_DIM, 1.0, 0.0).astype(BF16)
    hi, lo = _split(x * x)
    ssq = _dot(jnp.concatenate([hi, lo], axis=0), seg)
    kn = (x * lax.rsqrt((ssq[0:rows] + ssq[rows:2 * rows]) * (1.0 / HEAD_DIM) + EPS) * g_ref[...]).astype(BF16)
    for pr in range(KVW // 128):
        kn_ref[pr] = kn[:, pr * 128:(pr + 1) * 128]
    eye = jnp.where(_iota((KVW, KVW), 0) == _iota((KVW, KVW), 1), 1.0, 0.0).astype(BF16)
    pad = jnp.where(_iota((VROWS - HEAD_DIM, KCH), 0) == 0, 1.0, 0.0).astype(BF16)
    for c in range(rows // KCH):
        sl = slice(c * KCH, (c + 1) * KCH)
        v = jnp.concatenate([zv_ref[0, sl, :], zv_ref[1, sl, :]], axis=1).astype(BF16)
        vt = _dot_nt(eye, v).astype(BF16)
        for g in range(N_KV):
            vt_ref[c, g, 0:HEAD_DIM, :] = vt[g * HEAD_DIM:(g + 1) * HEAD_DIM]
            vt_ref[c, g, HEAD_DIM:VROWS, :] = pad


def _kvprep(zkv, gains, *, B, S, cpb=4):
    nch = S // KCH
    assert nch % cpb == 0
    nblk = nch // cpb
    rows = cpb * KCH
    return pl.pallas_call(
        _kvprep_kernel,
        grid=(2, B, nblk),
        in_specs=[
            pl.BlockSpec((None, KVW // 128, rows, 128), lambda w, b, i: (2 + 2 * w, 0, b * nblk + i, 0)),
            pl.BlockSpec((None, KVW // 128, rows, 128), lambda w, b, i: (3 + 2 * w, 0, b * nblk + i, 0)),
            pl.BlockSpec((None, 1, KVW), lambda w, b, i: (w, 0, 0)),
        ],
        out_specs=[
            pl.BlockSpec((None, None, KVW // 128, rows, 128), lambda w, b, i: (w, b, 0, i, 0)),
            pl.BlockSpec((None, None, cpb, N_KV, VROWS, KCH), lambda w, b, i: (w, b, i, 0, 0, 0)),
        ],
        out_shape=[
            jax.ShapeDtypeStruct((2, B, KVW // 128, S, 128), BF16),
            jax.ShapeDtypeStruct((2, B, nch, N_KV, VROWS, KCH), BF16),
        ],
        compiler_params=_params(("parallel", "parallel", "parallel")),
        name="kvprep",
    )(zkv, zkv, gains)


def _compress_kernel(zk_ref, zv_ref, pos_ref, w1_ref, w2_ref, g_ref, kc_ref, vct_ref):
    nch = zk_ref.shape[0] // CMP_STRIDE
    lane = _iota((nch, 128), 1)
    npair = CMP_STRIDE // 2
    eye = jnp.where(_iota((HEAD_DIM, HEAD_DIM), 0) == _iota((HEAD_DIM, HEAD_DIM), 1), 1.0, 0.0).astype(BF16)
    for kind, z_ref in enumerate((zk_ref, zv_ref)):
        for ge in range(2):
            halves = []
            for half in range(2):
                acc = None
                for p in range(npair):
                    a = z_ref[pl.ds(2 * p, nch, stride=CMP_STRIDE), :]
                    b = z_ref[pl.ds(2 * p + 1, nch, stride=CMP_STRIDE), :]
                    if ge == 0:
                        comb = jnp.where(lane < HEAD_DIM, a, pltpu.roll(b, HEAD_DIM, axis=1))
                    else:
                        comb = jnp.where(lane < HEAD_DIM, pltpu.roll(a, HEAD_DIM, axis=1), b)
                    pr = half * npair + p
                    lhs = (comb + pos_ref[kind, pr:pr + 1, :]).astype(BF16)
                    d = _dot(lhs, w1_ref[kind, pr * 128:(pr + 1) * 128, :])
                    acc = d if acc is None else acc + d
                halves.append(acc)
            hid = halves[0] + pltpu.roll(halves[1], nch - 1, axis=0)
            hid = jax.nn.gelu(hid)
            out = _dot(hid.astype(BF16), w2_ref[kind])
            if kind == 0:
                out = out * lax.rsqrt(jnp.mean(out * out, axis=-1, keepdims=True) + EPS) * g_ref[...]
            res = jnp.where(_iota(out.shape, 0) < nch - 1, out, 0.0).astype(BF16)
            if kind == 0:
                kc_ref[ge] = res
            else:
                vct_ref[ge] = _dot_nt(eye, res).astype(BF16)


def _compress(zkv, pos, w1, w2, gain, *, B, S):
    nch = S // CMP_STRIDE
    return pl.pallas_call(
        _compress_kernel,
        grid=(B, KVW // 128),
        in_specs=[
            pl.BlockSpec((None, None, S, 128), lambda b, lp: (0, lp, b, 0)),
            pl.BlockSpec((None, None, S, 128), lambda b, lp: (1, lp, b, 0)),
            _const_spec((2, CMP_STRIDE, 128)),
            _const_spec((2, CMP_LEN * HEAD_DIM, CMP_HID)),
            _const_spec((2, CMP_HID, HEAD_DIM)),
            _const_spec((1, HEAD_DIM)),
        ],
        out_specs=[
            pl.BlockSpec((None, 2, nch, HEAD_DIM), lambda b, lp: (b, lp, 0, 0)),
            pl.BlockSpec((None, 2, HEAD_DIM, nch), lambda b, lp: (b, lp, 0, 0)),
        ],
        out_shape=[
            jax.ShapeDtypeStruct((B, N_KV, nch, HEAD_DIM), BF16),
            jax.ShapeDtypeStruct((B, N_KV, HEAD_DIM, nch), BF16),
        ],
        compiler_params=_params(("parallel", "parallel")),
        name="compress",
    )(zkv, zkv, pos, w1, w2, gain)


def _spread_heads(x):
    y = x + pltpu.roll(x, 2 * Q_BLK, axis=1)
    return y + pltpu.roll(y, Q_BLK, axis=1)


def _swap_heads(x):
    n = x.shape[0]
    lane = _iota((n, 128), 1)
    halves = []
    for pr in range(x.shape[1] // 128):
        slab = jnp.concatenate([x[:, pr * 128:(pr + 1) * 128], jnp.zeros((128 - n, 128), x.dtype)], axis=0)
        tr = slab.T
        halves.append(jnp.where(lane < n, tr[0:n], pltpu.roll(tr[n:2 * n], n, axis=1)))
    return jnp.concatenate(halves, axis=1)


def _nsa_kernel(*refs, nsel, qpb):
    for k in range(qpb):
        _nsa_block(k, *refs, nsel=nsel, qpb=qpb)


def _nsa_block(k, zq_ref, qg_ref, kc_ref, vct_ref, ks_ref, kw_ref, vst_ref, vwt_ref, ng_ref, o_ref,
               qtp_ref, oc_ref, bias_ref, ms_ref, accs_ref, mw_ref, accw_ref, sbuf_ref, wbuf_ref,
               gates_ref, scs_ref, *, nsel, qpb):
    qi = pl.program_id(1) * qpb + k
    q_rows = pl.ds(k * Q_BLK, Q_BLK)
    s0 = qi * Q_BLK
    cur = qi
    ncmp = kc_ref.shape[1]
    per_chunk = KCH // SEL_LEN

    lane_t = s0 + (_iota((1, GW), 1) % Q_BLK)
    lane_grp = _iota((1, GW), 1) // Q_BLK
    ov_n = _iota((nsel, ncmp), 0) * SEL_LEN
    ov_c = _iota((nsel, ncmp), 1) * CMP_STRIDE
    overlap = jnp.where((ov_c < ov_n + SEL_LEN) & (ov_c + CMP_LEN > ov_n), 1.0, 0.0).astype(BF16)


    ok_c = _iota((ncmp, GW), 0) * CMP_STRIDE + (CMP_LEN - 1) <= lane_t
    groups = range(N_KV)
    ngt = jnp.concatenate([ng_ref[q_rows, :], jnp.zeros((128 - Q_BLK, NG_PAD), F32)], axis=0).T
    ngt_hi = pltpu.roll(ngt, Q_BLK, axis=1)
    low_half = _iota((1, 128), 1) < Q_BLK
    for g in groups:
        for c in range(3):
            r = c * N_HEADS + g * HPG
            pairs = [jnp.where(low_half, ngt[r + j:r + j + 1], ngt_hi[r + j + 1:r + j + 2]) for j in (0, 2)]
            gates_ref[g, c:c + 1, :] = jnp.concatenate(pairs, axis=1)

    qts = []
    for g in groups:
        xt = _swap_heads(zq_ref[g, q_rows, :])
        ssq = jnp.sum(xt * xt, axis=0, keepdims=True)
        qts.append((xt * lax.rsqrt(ssq * (1.0 / HEAD_DIM) + EPS) * qg_ref[...]).astype(BF16))
    scs = []
    for g in groups:
        off = (g % 2) * HEAD_DIM
        qtp_ref[g, off:off + HEAD_DIM, :] = qts[g]
        qtp_ref[g, HEAD_DIM - off:2 * HEAD_DIM - off, :] = jnp.zeros((HEAD_DIM, GW), BF16)
        scs.append(_dot(kc_ref[g], qts[g]))

    def scores(k_ref, i):
        k0 = pl.multiple_of(i * KCH, KCH)
        return [_dot(k_ref[g // 2, pl.ds(k0, KCH), :], qtp_ref[g]) for g in groups]

    def park(buf_ref, vals):
        for g in groups:
            buf_ref[g] = vals[g]

    n_chunks = qi // per_chunk + 1
    lo_w = jnp.maximum(n_chunks - (WINDOW // KCH + 1), 0)
    park(scs_ref, scs)
    park(sbuf_ref, scores(ks_ref, 0))
    park(wbuf_ref, scores(kw_ref, lo_w))

    key_row = _iota((SEL_LEN, GW), 0)
    no_shift = jnp.zeros((1, GW), F32)
    ms_ref[...] = jnp.full(ms_ref.shape, NEG, F32)
    mw_ref[...] = jnp.full(mw_ref.shape, NEG, F32)
    accs_ref[...] = jnp.zeros(accs_ref.shape, F32)
    accw_ref[...] = jnp.zeros(accw_ref.shape, F32)

    def update(g, blocks, shifts, vt, m_ref, acc_ref):
        m = m_ref[g]
        mnew = m
        for blk, sh in zip(blocks, shifts):
            mnew = jnp.maximum(mnew, jnp.max(blk, axis=0, keepdims=True) + sh)
        e = jnp.concatenate([jnp.exp2(blk + (sh - mnew)) for blk, sh in zip(blocks, shifts)], axis=0).astype(BF16)
        acc_ref[g] = jnp.exp2(m - mnew) * acc_ref[g] + _dot(vt, e)
        m_ref[g] = mnew

    def slabs(buf_ref, g, keep):
        out = []
        for nb in range(per_chunk):
            blk = buf_ref[g, nb * SEL_LEN:(nb + 1) * SEL_LEN, :]
            out.append(blk if keep is None else jnp.where(keep(nb * SEL_LEN), blk, NEG))
        return out

    def sel_softmax(i, last):
        keep = (lambda k0: key_row + (i * KCH + k0) <= lane_t) if last else None
        for g in groups:
            shifts = [bias_ref[g, pl.ds(i * per_chunk + nb, 1), :] for nb in range(per_chunk)]
            update(g, slabs(sbuf_ref, g, keep), shifts, vst_ref[i, g], ms_ref, accs_ref)

    def win_softmax(i, last):
        if last:
            keep = lambda k0: key_row + (i * KCH + k0) <= lane_t
        else:
            keep = lambda k0: key_row + (i * KCH + k0) > lane_t - WINDOW
        for g in groups:
            update(g, slabs(wbuf_ref, g, keep), [no_shift] * per_chunk, vwt_ref[i, g], mw_ref, accw_ref)

    def sel_step(i, carry):
        nxt = scores(ks_ref, i + 1)
        sel_softmax(i, False)
        park(sbuf_ref, nxt)
        return carry

    def win_step(i, carry):
        nxt = scores(kw_ref, i + 1)
        win_softmax(i, False)
        park(wbuf_ref, nxt)
        return carry

    imp_all = jnp.zeros((nsel, GW), F32)
    some_c = lane_t >= CMP_LEN - 1
    for g in groups:
        scm = jnp.where(ok_c, scs_ref[g], NEG)
        ec = jnp.exp2(scm - jnp.max(scm, axis=0, keepdims=True))
        inv = jnp.where(some_c, 1.0 / jnp.maximum(jnp.sum(ec, axis=0, keepdims=True), 1e-30), 0.0)
        both = _dot(jnp.concatenate([vct_ref[g], overlap], axis=0), ec.astype(BF16)) * inv
        oc_ref[g] = both[0:HEAD_DIM]
        imp = _spread_heads(both[HEAD_DIM:HEAD_DIM + nsel])
        imp_all = jnp.where(lane_grp == g, imp, imp_all)

    n_io = _iota((nsel, GW), 0)
    forced = (n_io == 0) | (n_io == cur) | (n_io == cur - 1)
    valid = n_io <= cur
    score = jnp.where(forced, FORCE_SCORE, imp_all)
    score = jnp.where(valid, score, -1.0)
    rows8 = [score[r:r + 8] for r in range(0, nsel, 8)]
    ranks = [jnp.zeros((8, GW), F32) for _ in rows8]
    sub = _iota((8, GW), 0)
    for m in range(nsel):
        rowv = score[m:m + 1, :]
        for k, blk in enumerate(rows8):
            ge = jnp.where(rowv >= blk, 1.0, 0.0)
            gt = jnp.where(rowv > blk, 1.0, 0.0)
            if 8 * k > m:
                inc = ge
            elif 8 * k + 7 < m:
                inc = gt
            else:
                inc = jnp.where(sub + 8 * k > m, ge, gt)
            ranks[k] = ranks[k] + inc
    rank = jnp.concatenate(ranks, axis=0)
    sel_all = jnp.where((rank < float(min(SEL_TOPN, nsel))) & valid, 1.0, 0.0)
    for g in groups:
        mine = jnp.where(lane_grp == g, sel_all, 0.0)
        bias_ref[g] = (_spread_heads(mine) - 1.0) * (-NEG)

    lax.fori_loop(lo_w, n_chunks - 1, win_step, 0)
    lax.fori_loop(0, n_chunks - 1, sel_step, 0)

    win_softmax(n_chunks - 1, True)
    sel_softmax(n_chunks - 1, True)
    outs = []
    for g in groups:
        o_s = accs_ref[g, 0:HEAD_DIM, :] * (1.0 / jnp.maximum(accs_ref[g, HEAD_DIM:HEAD_DIM + 1, :], 1e-30))
        o_w = accw_ref[g, 0:HEAD_DIM, :] * (1.0 / jnp.maximum(accw_ref[g, HEAD_DIM:HEAD_DIM + 1, :], 1e-30))
        gates = gates_ref[g]
        o_t = gates[0:1] * oc_ref[g] + gates[1:2] * o_s + gates[2:3] * o_w
        outs.append(_swap_heads(o_t))
    for g in groups:
        o_ref[g, q_rows, :] = outs[g].astype(o_ref.dtype)


def _nsa(zq, qgain, kc, vct, kn, vt, ng, *, B, S, qpb=4):
    T = B * S
    assert S % (Q_BLK * qpb) == 0
    nq = S // (Q_BLK * qpb)
    nch = S // KCH
    ncmp = S // CMP_STRIDE
    nsel = S // SEL_LEN
    tok = lambda b, i: (0, b * nq + i, 0)
    return pl.pallas_call(
        functools.partial(_nsa_kernel, nsel=nsel, qpb=qpb),
        grid=(B, nq),
        in_specs=[
            pl.BlockSpec((N_KV, Q_BLK * qpb, GW), tok),
            _const_spec((HEAD_DIM, GW)),
            pl.BlockSpec((None, N_KV, ncmp, HEAD_DIM), lambda b, i: (b, 0, 0, 0)),
            pl.BlockSpec((None, N_KV, HEAD_DIM, ncmp), lambda b, i: (b, 0, 0, 0)),
            pl.BlockSpec((None, None, 2, S, 128), lambda b, i: (0, b, 0, 0, 0)),
            pl.BlockSpec((None, None, 2, S, 128), lambda b, i: (1, b, 0, 0, 0)),
            pl.BlockSpec((None, None, nch, N_KV, VROWS, KCH), lambda b, i: (0, b, 0, 0, 0, 0)),
            pl.BlockSpec((None, None, nch, N_KV, VROWS, KCH), lambda b, i: (1, b, 0, 0, 0, 0)),
            pl.BlockSpec((Q_BLK * qpb, NG_PAD), lambda b, i: (b * nq + i, 0)),
        ],
        out_specs=pl.BlockSpec((N_KV, Q_BLK * qpb, GW), tok),
        out_shape=jax.ShapeDtypeStruct((N_KV, T, GW), BF16),
        scratch_shapes=[
            pltpu.VMEM((N_KV, 2 * HEAD_DIM, GW), BF16),
            pltpu.VMEM((N_KV, HEAD_DIM, GW), F32),
            pltpu.VMEM((N_KV, nsel, GW), F32),
            pltpu.VMEM((N_KV, 1, GW), F32),
            pltpu.VMEM((N_KV, VROWS, GW), F32),
            pltpu.VMEM((N_KV, 1, GW), F32),
            pltpu.VMEM((N_KV, VROWS, GW), F32),
            pltpu.VMEM((N_KV, KCH, GW), F32),
            pltpu.VMEM((N_KV, KCH, GW), F32),
            pltpu.VMEM((N_KV, 8, GW), F32),
            pltpu.VMEM((N_KV, ncmp, GW), F32),
        ],
        compiler_params=_params(("parallel", "arbitrary")),
        name="nsa",
    )(zq, qgain, kc, vct, kn, kn, vt, vt, ng)


def _merge_kernel(x_ref, ya_ref, o_ref_in, mg_ref, wn_ref, wo_ref, out_ref):
    yb = _dot(o_ref_in[0], wn_ref[0:GW, :])
    for g in range(1, N_KV):
        yb = yb + _dot(o_ref_in[g], wn_ref[g * GW:(g + 1) * GW, :])
    tm = x_ref.shape[0]
    ya = jnp.concatenate([ya_ref[ck].reshape(tm, 128) for ck in range(D_MODEL // 128)], axis=1)
    mixed = (mg_ref[:, 0:D_MODEL].astype(F32) * ya
             + mg_ref[:, D_MODEL:2 * D_MODEL].astype(F32) * yb)
    out_ref[...] = x_ref[...] + _dot(mixed.astype(BF16), wo_ref[...])


def _merge(x2d, ya, o, mg, wn, wo, *, B, S, tm=512):
    ni = S // tm
    tok = lambda b, i: (b * ni + i, 0)
    return pl.pallas_call(
        _merge_kernel,
        grid=(B, ni),
        in_specs=[
            pl.BlockSpec((tm, D_MODEL), tok),
            pl.BlockSpec((D_MODEL // 128, tm // 8, None, 8, 128), lambda b, i: (0, i, b, 0, 0)),
            pl.BlockSpec((N_KV, tm, GW), lambda b, i: (0, b * ni + i, 0)),
            pl.BlockSpec((tm, 2 * D_MODEL), tok),
            _const_spec((N_HEADS * HEAD_DIM, D_MODEL)),
            _const_spec((D_MODEL, D_MODEL)),
        ],
        out_specs=pl.BlockSpec((tm, D_MODEL), tok),
        out_shape=jax.ShapeDtypeStruct((B * S, D_MODEL), F32),
        compiler_params=_params(("parallel", "parallel")),
        name="merge",
    )(x2d, ya, o, mg, wn, wo)


def _ffn_kernel(x_ref, g_ref, wg_ref, wu_ref, wd_ref, o_ref, h_ref, *, tf):
    x = x_ref[...]
    y = x * lax.rsqrt(jnp.mean(x * x, axis=-1, keepdims=True) + EPS) * g_ref[...]
    h_ref[...] = y.astype(BF16)
    o_ref[...] = x
    for j in range(D_FF // tf):
        h = h_ref[...]
        sl = slice(j * tf, (j + 1) * tf)
        act = jax.nn.silu(_dot(h, wg_ref[:, sl])) * _dot(h, wu_ref[:, sl])
        o_ref[...] += _dot(act.astype(BF16), wd_ref[sl, :])


def _ffn(x2d, gain, wg, wu, wd, *, tm=512, tf=256):
    T = x2d.shape[0]
    assert T % tm == 0 and D_FF % tf == 0
    once = dict(pipeline_mode=pl.Buffered(1))
    return pl.pallas_call(
        functools.partial(_ffn_kernel, tf=tf),
        grid=(T // tm,),
        in_specs=[
            pl.BlockSpec((tm, D_MODEL), lambda i: (i, 0)),
            _const_spec((1, D_MODEL)),
            pl.BlockSpec((D_MODEL, D_FF), lambda i: (0, 0), **once),
            pl.BlockSpec((D_MODEL, D_FF), lambda i: (0, 0), **once),
            pl.BlockSpec((D_FF, D_MODEL), lambda i: (0, 0), **once),
        ],
        out_specs=pl.BlockSpec((tm, D_MODEL), lambda i: (i, 0)),
        out_shape=jax.ShapeDtypeStruct((T, D_MODEL), F32),
        scratch_shapes=[pltpu.VMEM((tm, D_MODEL), BF16)],
        compiler_params=_params(("parallel",)),
        name="ffn",
    )(x2d, gain, wg, wu, wd)


def _layer(x, p):
    B, S, _ = x.shape
    T = B * S
    dh = HEAD_DIM
    row = lambda v: v.reshape(1, -1)
    w_in = p["w_in"]
    n_ng = 3 * N_HEADS
    o_ng = 2 * D_RNN + N_HEADS * dh + 6 * KVW
    w_packed = jnp.concatenate(
        [w_in[:, :o_ng], jnp.pad(w_in[:, o_ng:o_ng + n_ng], ((0, 0), (0, NG_PAD - n_ng))), w_in[:, o_ng + n_ng:]],
        axis=1).astype(BF16)
    rx, gy, zq, zkv, ng, mg = _inproj(x, row(p["norm1"]), w_packed)

    slabs = lambda a: a.reshape(a.shape[0], S * B, 128)
    ya = _rglru(slabs(rx), slabs(gy), p["conv_w"], row(p["conv_b"]),
                p["rg_wa"].astype(BF16), row(p["rg_ba"]), p["rg_wi"].astype(BF16), row(p["rg_bi"]),
                row(p["rg_lambda"]), p["w_rg_out"].astype(BF16), nb=B)
    ya = ya.reshape(D_MODEL // 128, S // 8, B, 8, 128)

    kgains = jnp.tile(p["k_norm"][1:3], (1, N_KV)).reshape(2, 1, KVW)
    kn, vt = _kvprep(zkv, kgains, B=B, S=S)
    pos = jnp.stack([p["cmp_pos_k"], p["cmp_pos_v"]]).reshape(2, CMP_STRIDE, 2 * dh)
    w1 = jnp.stack([p["cmp_k_w1"], p["cmp_v_w1"]]).astype(BF16)
    w2 = jnp.stack([p["cmp_k_w2"], p["cmp_v_w2"]]).astype(BF16)
    kc, vct = _compress(zkv, pos, w1, w2, row(p["k_norm"][0]), B=B, S=S)
    qgain = jnp.broadcast_to((p["q_norm"] * (dh ** -0.5 * LOG2E))[:, None], (dh, GW))
    o = _nsa(zq, qgain, kc, vct, kn, vt, ng, B=B, S=S)

    x1 = _merge(x.reshape(T, D_MODEL), ya, o, mg, p["w_nsa_out"].astype(BF16), p["w_o"].astype(BF16), B=B, S=S)
    x2 = _ffn(x1, row(p["norm2"]), p["w_gate"].astype(BF16), p["w_up"].astype(BF16), p["w_down"].astype(BF16))
    return x2.reshape(B, S, D_MODEL)


def kernel(x, norm1, w_in, conv_w, conv_b, rg_wa, rg_ba, rg_wi, rg_bi, rg_lambda, q_norm, k_norm, cmp_pos_k,
           cmp_pos_v, cmp_k_w1, cmp_k_w2, cmp_v_w1, cmp_v_w2, w_rg_out, w_nsa_out, w_o, norm2, w_gate, w_up, w_down):
    params = dict(norm1=norm1, w_in=w_in, conv_w=conv_w, conv_b=conv_b, rg_wa=rg_wa, rg_ba=rg_ba, rg_wi=rg_wi,
                  rg_bi=rg_bi, rg_lambda=rg_lambda, q_norm=q_norm, k_norm=k_norm, cmp_pos_k=cmp_pos_k,
                  cmp_pos_v=cmp_pos_v, cmp_k_w1=cmp_k_w1, cmp_k_w2=cmp_k_w2, cmp_v_w1=cmp_v_w1, cmp_v_w2=cmp_v_w2,
                  w_rg_out=w_rg_out, w_nsa_out=w_nsa_out, w_o=w_o, norm2=norm2, w_gate=w_gate, w_up=w_up,
                  w_down=w_down)
    for l in range(norm1.shape[0]):
        x = _layer(x, {k: v[l] for k, v in params.items()})
    return x
```

```python
import functools

import jax
import jax.numpy as jnp
from jax import lax
from jax.experimental import pallas as pl
from jax.experimental.pallas import tpu as pltpu

D_MODEL = 1024
D_RNN = 1024
RG_BLOCKS = 4
RG_BW = D_RNN // RG_BLOCKS
CONV_W = 4
RG_C = 8.0
N_HEADS = 16
N_KV = 4
HEAD_DIM = 64
HPG = N_HEADS // N_KV
CMP_LEN = 32
CMP_STRIDE = 16
CMP_HID = 256
SEL_LEN = 64
SEL_TOPN = 16
WINDOW = 512
Q_BLK = 64
D_FF = 2816
EPS = 1e-6
FORCE_SCORE = 1e6
NEG = -1e30
LOG2E = 1.4426950408889634

KCH = 256
GW = HPG * HEAD_DIM
KVW = N_KV * HEAD_DIM
VROWS = 80
NG_PAD = 128

F32 = jnp.float32
BF16 = jnp.bfloat16

VMEM_LIMIT = 56 * 1024 * 1024


def _params(sem):
    return pltpu.CompilerParams(dimension_semantics=sem, vmem_limit_bytes=VMEM_LIMIT)


def _dot(a, b):
    return jnp.dot(a, b, preferred_element_type=F32)


def _dot_nt(a, b):
    return lax.dot_general(a, b, (((1,), (1,)), ((), ())), preferred_element_type=F32)


def _split(x):
    hi = x.astype(BF16)
    lo = (x - hi.astype(F32)).astype(BF16)
    return hi, lo


def _iota(shape, dim):
    return lax.broadcasted_iota(jnp.int32, shape, dim)


def _const_spec(shape):
    return pl.BlockSpec(shape, lambda *_: (0,) * len(shape))


_O_RY = D_RNN
_O_Q = 2 * D_RNN
_O_KV = _O_Q + N_HEADS * HEAD_DIM
_O_NG = _O_KV + 6 * KVW
_O_MG = _O_NG + NG_PAD
_W_COLS = _O_MG + 2 * D_MODEL


def _inproj_kernel(x_ref, g_ref, w_ref, rx_ref, gy_ref, zq_ref, zkv_ref, ng_ref, mg_ref, h_ref):
    x = x_ref[...]
    y = x * lax.rsqrt(jnp.mean(x * x, axis=-1, keepdims=True) + EPS) * g_ref[...]
    h_ref[...] = y.astype(BF16)

    def proj(c0, width):
        return _dot(h_ref[...], w_ref[:, c0:c0 + width])

    cw = 256
    tm = x_ref.shape[0]
    for c in range(D_RNN // cw):
        rx = proj(c * cw, cw)
        gy = jax.nn.gelu(proj(_O_RY + c * cw, cw))
        for hf in range(cw // 128):
            rx_ref[c * (cw // 128) + hf] = rx[:, hf * 128:(hf + 1) * 128].reshape(tm // 8, 8, 128)
            gy_ref[c * (cw // 128) + hf] = gy[:, hf * 128:(hf + 1) * 128].reshape(tm // 8, 8, 128)
    for g in range(N_KV):
        zq_ref[g] = proj(_O_Q + g * GW, GW)
    for s in range(6):
        z = proj(_O_KV + s * KVW, KVW)
        for pr in range(KVW // 128):
            zkv_ref[s, pr] = z[:, pr * 128:(pr + 1) * 128]
    ng_ref[...] = jax.nn.sigmoid(proj(_O_NG, NG_PAD))
    for c in range(2 * D_MODEL // cw):
        mg_ref[:, c * cw:(c + 1) * cw] = jax.nn.sigmoid(proj(_O_MG + c * cw, cw)).astype(mg_ref.dtype)


def _inproj(x, gain, w, *, tm=256):
    B, S, D = x.shape
    T = B * S
    ni = S // tm
    nck = D_RNN // 128
    tok = lambda b, i: (b * ni + i, 0)
    slab_spec = pl.BlockSpec((nck, tm // 8, None, 8, 128), lambda b, i: (0, i, b, 0, 0))
    slab_shape = jax.ShapeDtypeStruct((nck, S // 8, B, 8, 128), F32)
    return pl.pallas_call(
        _inproj_kernel,
        grid=(B, ni),
        in_specs=[
            pl.BlockSpec((None, tm, D), lambda b, i: (b, i, 0)),
            _const_spec((1, D)),
            pl.BlockSpec((D, _W_COLS), lambda b, i: (0, 0), pipeline_mode=pl.Buffered(1)),
        ],
        out_specs=[
            slab_spec,
            slab_spec,
            pl.BlockSpec((N_KV, tm, GW), lambda b, i: (0, b * ni + i, 0)),
            pl.BlockSpec((6, KVW // 128, tm, 128), lambda b, i: (0, 0, b * ni + i, 0)),
            pl.BlockSpec((tm, NG_PAD), tok),
            pl.BlockSpec((tm, 2 * D_MODEL), tok),
        ],
        out_shape=[
            slab_shape,
            slab_shape,
            jax.ShapeDtypeStruct((N_KV, T, GW), F32),
            jax.ShapeDtypeStruct((6, KVW // 128, T, 128), F32),
            jax.ShapeDtypeStruct((T, NG_PAD), F32),
            jax.ShapeDtypeStruct((T, 2 * D_MODEL), BF16),
        ],
        scratch_shapes=[pltpu.VMEM((tm, D), BF16)],
        compiler_params=_params(("parallel", "parallel")),
        name="inproj",
    )(x, gain, w)


def _rglru_kernel(x_ref, gy_ref, cw_ref, cb_ref, wa_ref, ba_ref, wi_ref, bi_ref, lam_ref, wo_ref,
                  o_ref, xb_ref, gyb_ref, a_ref, u_ref, hb_ref, hs_ref, yb_ref, *, nb, tt):
    R = nb * tt
    halo = (CONV_W - 1) * nb
    step = pl.program_id(0)
    nck = D_RNN // 128

    @pl.when(step == 0)
    def _():
        xb_ref[0:halo, :] = jnp.zeros((halo, D_RNN), F32)
        hs_ref[...] = jnp.zeros((nb, D_RNN), F32)

    def slab_rows(t8, t_lo):
        return pl.ds(t8 * (8 * nb) + t_lo, nb, stride=8)

    def stage_in(t8, carry):
        for t_lo in range(8):
            r0 = pl.multiple_of((t8 * 8 + t_lo) * nb, nb)
            for ck in range(nck):
                cols = slice(ck * 128, (ck + 1) * 128)
                xb_ref[pl.ds(halo + r0, nb), cols] = x_ref[ck, slab_rows(t8, t_lo), :]
                gyb_ref[pl.ds(r0, nb), cols] = gy_ref[ck, slab_rows(t8, t_lo), :]
        return carry

    lax.fori_loop(0, tt // 8, stage_in, 0)
    xr = cb_ref[...] + cw_ref[0:1, :] * xb_ref[0:R, :]
    for k in range(1, CONV_W):
        xr = xr + cw_ref[k:k + 1, :] * xb_ref[k * nb:k * nb + R, :]
    tail = xb_ref[R:R + halo, :]
    xb_ref[0:halo, :] = tail

    xrb = xr.astype(BF16)
    sp = jax.nn.softplus(-lam_ref[...])
    row = _iota((R, RG_BW), 0)
    first = (row < nb) & (step == 0)
    for n in range(RG_BLOCKS):
        sl = slice(n * RG_BW, (n + 1) * RG_BW)
        xn = xrb[:, sl]
        r = jax.nn.sigmoid(_dot(xn, wa_ref[n]) + ba_ref[:, sl])
        ig = jax.nn.sigmoid(_dot(xn, wi_ref[n]) + bi_ref[:, sl])
        log_a = (-RG_C) * r * sp[:, sl]
        a = jnp.exp(log_a)
        mult = jnp.sqrt(1.0 - a * a)
        mult = jnp.where(first, 1.0, mult)
        a_ref[:, sl] = a
        u_ref[:, sl] = mult * ig * xr[:, sl]

    def body(t, h):
        r0 = pl.multiple_of(t * nb, nb)
        h = a_ref[pl.ds(r0, nb), :] * h + u_ref[pl.ds(r0, nb), :]
        hb_ref[pl.ds(r0, nb), :] = h
        return h

    h = lax.fori_loop(0, tt, body, hs_ref[...], unroll=8)
    hs_ref[...] = h
    yb_ref[...] = _dot((hb_ref[...] * gyb_ref[...]).astype(BF16), wo_ref[...])

    def stage_out(t8, carry):
        for t_lo in range(8):
            r0 = pl.multiple_of((t8 * 8 + t_lo) * nb, nb)
            for ck in range(nck):
                o_ref[ck, slab_rows(t8, t_lo), :] = yb_ref[pl.ds(r0, nb), ck * 128:(ck + 1) * 128]
        return carry

    lax.fori_loop(0, tt // 8, stage_out, 0)


def _rglru(rx, gy, conv_w, conv_b, wa, ba, wi, bi, lam, wo, *, nb, tt=32):
    nck, rows, _ = rx.shape
    S = rows // nb
    assert S % tt == 0 and tt % 8 == 0
    R = nb * tt
    halo = (CONV_W - 1) * nb
    return pl.pallas_call(
        functools.partial(_rglru_kernel, nb=nb, tt=tt),
        grid=(S // tt,),
        in_specs=[
            pl.BlockSpec((nck, R, 128), lambda s: (0, s, 0)),
            pl.BlockSpec((nck, R, 128), lambda s: (0, s, 0)),
            _const_spec((CONV_W, D_RNN)),
            _const_spec((1, D_RNN)),
            _const_spec((RG_BLOCKS, RG_BW, RG_BW)),
            _const_spec((1, D_RNN)),
            _const_spec((RG_BLOCKS, RG_BW, RG_BW)),
            _const_spec((1, D_RNN)),
            _const_spec((1, D_RNN)),
            _const_spec((D_RNN, D_MODEL)),
        ],
        out_specs=pl.BlockSpec((D_MODEL // 128, R, 128), lambda s: (0, s, 0)),
        out_shape=jax.ShapeDtypeStruct((D_MODEL // 128, rows, 128), F32),
        scratch_shapes=[
            pltpu.VMEM((R + halo, D_RNN), F32),
            pltpu.VMEM((R, D_RNN), F32),
            pltpu.VMEM((R, D_RNN), F32),
            pltpu.VMEM((R, D_RNN), F32),
            pltpu.VMEM((R, D_RNN), F32),
            pltpu.VMEM((nb, D_RNN), F32),
            pltpu.VMEM((R, D_MODEL), F32),
        ],
        compiler_params=_params(("arbitrary",)),
        name="rglru",
    )(rx, gy, conv_w, conv_b, wa, ba, wi, bi, lam, wo)


def _kvprep_kernel(zk_ref, zv_ref, g_ref, kn_ref, vt_ref):
    rows = zk_ref.shape[1]
    x = jnp.concatenate([zk_ref[0], zk_ref[1]], axis=1)
    seg = jnp.where(_iota((KVW, KVW), 0) // HEAD_DIM == _iota((KVW, KVW), 1) // HEAD_DIM, 1.0, 0.0).astype(BF16)
    hi, lo = _split(x * x)
    ssq = _dot(jnp.concatenate([hi, lo], axis=0), seg)
    kn = (x * lax.rsqrt((ssq[0:rows] + ssq[rows:2 * rows]) * (1.0 / HEAD_DIM) + EPS) * g_ref[...]).astype(BF16)
    for pr in range(KVW // 128):
        kn_ref[pr] = kn[:, pr * 128:(pr + 1) * 128]
    eye = jnp.where(_iota((KVW, KVW), 0) == _iota((KVW, KVW), 1), 1.0, 0.0).astype(BF16)
    pad = jnp.where(_iota((VROWS - HEAD_DIM, KCH), 0) == 0, 1.0, 0.0).astype(BF16)
    for c in range(rows // KCH):
        sl = slice(c * KCH, (c + 1) * KCH)
        v = jnp.concatenate([zv_ref[0, sl, :], zv_ref[1, sl, :]], axis=1).astype(BF16)
        vt = _dot_nt(eye, v).astype(BF16)
        for g in range(N_KV):
            vt_ref[c, g, 0:HEAD_DIM, :] = vt[g * HEAD_DIM:(g + 1) * HEAD_DIM]
            vt_ref[c, g, HEAD_DIM:VROWS, :] = pad


def _kvprep(zkv, gains, *, B, S, cpb=4):
    nch = S // KCH
    assert nch % cpb == 0
    nblk = nch // cpb
    rows = cpb * KCH
    return pl.pallas_call(
        _kvprep_kernel,
        grid=(2, B, nblk),
        in_specs=[
            pl.BlockSpec((None, KVW // 128, rows, 128), lambda w, b, i: (2 + 2 * w, 0, b * nblk + i, 0)),
            pl.BlockSpec((None, KVW // 128, rows, 128), lambda w, b, i: (3 + 2 * w, 0, b * nblk + i, 0)),
            pl.BlockSpec((None, 1, KVW), lambda w, b, i: (w, 0, 0)),
        ],
        out_specs=[
            pl.BlockSpec((None, None, KVW // 128, rows, 128), lambda w, b, i: (w, b, 0, i, 0)),
            pl.BlockSpec((None, None, cpb, N_KV, VROWS, KCH), lambda w, b, i: (w, b, i, 0, 0, 0)),
        ],
        out_shape=[
            jax.ShapeDtypeStruct((2, B, KVW // 128, S, 128), BF16),
            jax.ShapeDtypeStruct((2, B, nch, N_KV, VROWS, KCH), BF16),
        ],
        compiler_params=_params(("parallel", "parallel", "parallel")),
        name="kvprep",
    )(zkv, zkv, gains)


def _compress_kernel(zk_ref, zv_ref, pos_ref, w1_ref, w2_ref, g_ref, kc_ref, vct_ref):
    nch = zk_ref.shape[0] // CMP_STRIDE
    lane = _iota((nch, 128), 1)
    npair = CMP_STRIDE // 2
    eye = jnp.where(_iota((HEAD_DIM, HEAD_DIM), 0) == _iota((HEAD_DIM, HEAD_DIM), 1), 1.0, 0.0).astype(BF16)
    for kind, z_ref in enumerate((zk_ref, zv_ref)):
        for ge in range(2):
            halves = []
            for half in range(2):
                acc = None
                for p in range(npair):
                    a = z_ref[pl.ds(2 * p, nch, stride=CMP_STRIDE), :]
                    b = z_ref[pl.ds(2 * p + 1, nch, stride=CMP_STRIDE), :]
                    if ge == 0:
                        comb = jnp.where(lane < HEAD_DIM, a, pltpu.roll(b, HEAD_DIM, axis=1))
                    else:
                        comb = jnp.where(lane < HEAD_DIM, pltpu.roll(a, HEAD_DIM, axis=1), b)
                    pr = half * npair + p
                    lhs = (comb + pos_ref[kind, pr:pr + 1, :]).astype(BF16)
                    d = _dot(lhs, w1_ref[kind, pr * 128:(pr + 1) * 128, :])
                    acc = d if acc is None else acc + d
                halves.append(acc)
            hid = halves[0] + pltpu.roll(halves[1], nch - 1, axis=0)
            hid = jax.nn.gelu(hid)
            out = _dot(hid.astype(BF16), w2_ref[kind])
            if kind == 0:
                out = out * lax.rsqrt(jnp.mean(out * out, axis=-1, keepdims=True) + EPS) * g_ref[...]
            res = jnp.where(_iota(out.shape, 0) < nch - 1, out, 0.0).astype(BF16)
            if kind == 0:
                kc_ref[ge] = res
            else:
                vct_ref[ge] = _dot_nt(eye, res).astype(BF16)


def _compress(zkv, pos, w1, w2, gain, *, B, S):
    nch = S // CMP_STRIDE
    return pl.pallas_call(
        _compress_kernel,
        grid=(B, KVW // 128),
        in_specs=[
            pl.BlockSpec((None, None, S, 128), lambda b, lp: (0, lp, b, 0)),
            pl.BlockSpec((None, None, S, 128), lambda b, lp: (1, lp, b, 0)),
            _const_spec((2, CMP_STRIDE, 128)),
            _const_spec((2, CMP_LEN * HEAD_DIM, CMP_HID)),
            _const_spec((2, CMP_HID, HEAD_DIM)),
            _const_spec((1, HEAD_DIM)),
        ],
        out_specs=[
            pl.BlockSpec((None, 2, nch, HEAD_DIM), lambda b, lp: (b, lp, 0, 0)),
            pl.BlockSpec((None, 2, HEAD_DIM, nch), lambda b, lp: (b, lp, 0, 0)),
        ],
        out_shape=[
            jax.ShapeDtypeStruct((B, N_KV, nch, HEAD_DIM), BF16),
            jax.ShapeDtypeStruct((B, N_KV, HEAD_DIM, nch), BF16),
        ],
        compiler_params=_params(("parallel", "parallel")),
        name="compress",
    )(zkv, zkv, pos, w1, w2, gain)


def _spread_heads(x):
    y = x + pltpu.roll(x, 2 * Q_BLK, axis=1)
    return y + pltpu.roll(y, Q_BLK, axis=1)


def _swap_heads(x):
    n = x.shape[0]
    lane = _iota((n, 128), 1)
    halves = []
    for pr in range(x.shape[1] // 128):
        slab = jnp.concatenate([x[:, pr * 128:(pr + 1) * 128], jnp.zeros((128 - n, 128), x.dtype)], axis=0)
        tr = slab.T
        halves.append(jnp.where(lane < n, tr[0:n], pltpu.roll(tr[n:2 * n], n, axis=1)))
    return jnp.concatenate(halves, axis=1)


def _nsa_kernel(*refs, nsel, qpb):
    for k in range(qpb):
        _nsa_block(k, *refs, nsel=nsel, qpb=qpb)


def _nsa_block(k, zq_ref, qg_ref, kc_ref, vct_ref, ks_ref, kw_ref, vst_ref, vwt_ref, ng_ref, o_ref,
               qtp_ref, oc_ref, bias_ref, ms_ref, accs_ref, mw_ref, accw_ref, sbuf_ref, wbuf_ref,
               gates_ref, scs_ref, wbias_ref, *, nsel, qpb):
    qi = pl.program_id(1) * qpb + k
    q_rows = pl.ds(k * Q_BLK, Q_BLK)
    s0 = qi * Q_BLK
    cur = qi
    ncmp = kc_ref.shape[1]
    per_chunk = KCH // SEL_LEN

    lane_t = s0 + (_iota((1, GW), 1) % Q_BLK)
    lane_grp = _iota((1, GW), 1) // Q_BLK
    ov_n = _iota((nsel, ncmp), 0) * SEL_LEN
    ov_c = _iota((nsel, ncmp), 1) * CMP_STRIDE
    overlap = jnp.where((ov_c < ov_n + SEL_LEN) & (ov_c + CMP_LEN > ov_n), 1.0, 0.0).astype(BF16)


    ok_c = _iota((ncmp, GW), 0) * CMP_STRIDE + (CMP_LEN - 1) <= lane_t
    groups = range(N_KV)
    ngt = jnp.concatenate([ng_ref[q_rows, :], jnp.zeros((128 - Q_BLK, NG_PAD), F32)], axis=0).T
    ngt_hi = pltpu.roll(ngt, Q_BLK, axis=1)
    low_half = _iota((1, 128), 1) < Q_BLK
    for g in groups:
        for c in range(3):
            r = c * N_HEADS + g * HPG
            pairs = [jnp.where(low_half, ngt[r + j:r + j + 1], ngt_hi[r + j + 1:r + j + 2]) for j in (0, 2)]
            gates_ref[g, c:c + 1, :] = jnp.concatenate(pairs, axis=1)

    qts = []
    for g in groups:
        xt = _swap_heads(zq_ref[g, q_rows, :])
        ssq = jnp.sum(xt * xt, axis=0, keepdims=True)
        qts.append((xt * lax.rsqrt(ssq * (1.0 / HEAD_DIM) + EPS) * qg_ref[...]).astype(BF16))
    scs = []
    for g in groups:
        off = (g % 2) * HEAD_DIM
        qtp_ref[g, off:off + HEAD_DIM, :] = qts[g]
        qtp_ref[g, HEAD_DIM - off:2 * HEAD_DIM - off, :] = jnp.zeros((HEAD_DIM, GW), BF16)
        scs.append(_dot(kc_ref[g], qts[g]))

    def scores(k_ref, i):
        k0 = pl.multiple_of(i * KCH, KCH)
        return [_dot(k_ref[g // 2, pl.ds(k0, KCH), :], qtp_ref[g]) for g in groups]

    def park(buf_ref, vals):
        for g in groups:
            buf_ref[g] = vals[g]

    n_chunks = qi // per_chunk + 1
    lo_w = jnp.maximum(n_chunks - (WINDOW // KCH + 1), 0)
    park(scs_ref, scs)
    park(sbuf_ref, scores(ks_ref, 0))
    park(wbuf_ref, scores(kw_ref, lo_w))

    key_row = _iota((SEL_LEN, GW), 0)
    in_blk_t = _iota((SEL_LEN, GW), 1) % Q_BLK
    causal_tri = jnp.where(key_row <= in_blk_t, 0.0, NEG)
    expired_tri = jnp.where(key_row > in_blk_t, 0.0, NEG)
    win_blocks = WINDOW // SEL_LEN
    cut_rows = pl.ds(pl.multiple_of((cur % per_chunk) * SEL_LEN, SEL_LEN), SEL_LEN)
    n_all = _iota((nsel, GW), 0)
    wbias_ref[...] = jnp.where((n_all >= cur - win_blocks) & (n_all <= cur), 0.0, NEG)

    def cut(buf_ref, tri):
        for g in groups:
            buf_ref[g, cut_rows, :] = buf_ref[g, cut_rows, :] + tri

    cut(wbuf_ref, jnp.where(cur >= win_blocks, expired_tri, 0.0))

    ms_ref[...] = jnp.full(ms_ref.shape, NEG, F32)
    mw_ref[...] = jnp.full(mw_ref.shape, NEG, F32)
    accs_ref[...] = jnp.zeros(accs_ref.shape, F32)
    accw_ref[...] = jnp.zeros(accw_ref.shape, F32)

    def update(g, blocks, shifts, vt, m_ref, acc_ref):
        m = m_ref[g]
        mnew = m
        for blk, sh in zip(blocks, shifts):
            mnew = jnp.maximum(mnew, jnp.max(blk, axis=0, keepdims=True) + sh)
        e = jnp.concatenate([jnp.exp2(blk + (sh - mnew)) for blk, sh in zip(blocks, shifts)], axis=0).astype(BF16)
        acc_ref[g] = jnp.exp2(m - mnew) * acc_ref[g] + _dot(vt, e)
        m_ref[g] = mnew

    def slabs(buf_ref, g):
        return [buf_ref[g, nb * SEL_LEN:(nb + 1) * SEL_LEN, :] for nb in range(per_chunk)]

    def sel_softmax(i):
        for g in groups:
            shifts = [bias_ref[g, pl.ds(i * per_chunk + nb, 1), :] for nb in range(per_chunk)]
            update(g, slabs(sbuf_ref, g), shifts, vst_ref[i, g], ms_ref, accs_ref)

    def win_softmax(i):
        shifts = [wbias_ref[pl.ds(i * per_chunk + nb, 1), :] for nb in range(per_chunk)]
        for g in groups:
            update(g, slabs(wbuf_ref, g), shifts, vwt_ref[i, g], mw_ref, accw_ref)

    def sel_step(i, carry):
        nxt = scores(ks_ref, i + 1)
        sel_softmax(i)
        park(sbuf_ref, nxt)
        return carry

    def win_step(i, carry):
        nxt = scores(kw_ref, i + 1)
        win_softmax(i)
        park(wbuf_ref, nxt)
        return carry

    imp_all = jnp.zeros((nsel, GW), F32)
    some_c = lane_t >= CMP_LEN - 1
    for g in groups:
        scm = jnp.where(ok_c, scs_ref[g], NEG)
        ec = jnp.exp2(scm - jnp.max(scm, axis=0, keepdims=True))
        inv = jnp.where(some_c, 1.0 / jnp.maximum(jnp.sum(ec, axis=0, keepdims=True), 1e-30), 0.0)
        both = _dot(jnp.concatenate([vct_ref[g], overlap], axis=0), ec.astype(BF16)) * inv
        oc_ref[g] = both[0:HEAD_DIM]
        imp = _spread_heads(both[HEAD_DIM:HEAD_DIM + nsel])
        imp_all = jnp.where(lane_grp == g, imp, imp_all)

    n_io = _iota((nsel, GW), 0)
    forced = (n_io == 0) | (n_io == cur) | (n_io == cur - 1)
    valid = n_io <= cur
    score = jnp.where(forced, FORCE_SCORE, imp_all)
    score = jnp.where(valid, score, -1.0)
    rows8 = [score[r:r + 8] for r in range(0, nsel, 8)]
    ranks = [jnp.zeros((8, GW), F32) for _ in rows8]
    sub = _iota((8, GW), 0)
    for m in range(nsel):
        rowv = score[m:m + 1, :]
        for k, blk in enumerate(rows8):
            ge = jnp.where(rowv >= blk, 1.0, 0.0)
            gt = jnp.where(rowv > blk, 1.0, 0.0)
            if 8 * k > m:
                inc = ge
            elif 8 * k + 7 < m:
                inc = gt
            else:
                inc = jnp.where(sub + 8 * k > m, ge, gt)
            ranks[k] = ranks[k] + inc
    rank = jnp.concatenate(ranks, axis=0)
    sel_all = jnp.where((rank < float(min(SEL_TOPN, nsel))) & valid, 1.0, 0.0)
    for g in groups:
        mine = jnp.where(lane_grp == g, sel_all, 0.0)
        bias_ref[g] = (_spread_heads(mine) - 1.0) * (-NEG)

    lax.fori_loop(lo_w, n_chunks - 1, win_step, 0)
    lax.fori_loop(0, n_chunks - 1, sel_step, 0)

    cut(wbuf_ref, causal_tri)
    cut(sbuf_ref, causal_tri)
    win_softmax(n_chunks - 1)
    sel_softmax(n_chunks - 1)
    outs = []
    for g in groups:
        o_s = accs_ref[g, 0:HEAD_DIM, :] * (1.0 / jnp.maximum(accs_ref[g, HEAD_DIM:HEAD_DIM + 1, :], 1e-30))
        o_w = accw_ref[g, 0:HEAD_DIM, :] * (1.0 / jnp.maximum(accw_ref[g, HEAD_DIM:HEAD_DIM + 1, :], 1e-30))
        gates = gates_ref[g]
        o_t = gates[0:1] * oc_ref[g] + gates[1:2] * o_s + gates[2:3] * o_w
        outs.append(_swap_heads(o_t))
    for g in groups:
        o_ref[g, q_rows, :] = outs[g].astype(o_ref.dtype)


def _nsa(zq, qgain, kc, vct, kn, vt, ng, *, B, S, qpb=4):
    T = B * S
    assert S % (Q_BLK * qpb) == 0
    nq = S // (Q_BLK * qpb)
    nch = S // KCH
    ncmp = S // CMP_STRIDE
    nsel = S // SEL_LEN
    tok = lambda b, i: (0, b * nq + i, 0)
    return pl.pallas_call(
        functools.partial(_nsa_kernel, nsel=nsel, qpb=qpb),
        grid=(B, nq),
        in_specs=[
            pl.BlockSpec((N_KV, Q_BLK * qpb, GW), tok),
            _const_spec((HEAD_DIM, GW)),
            pl.BlockSpec((None, N_KV, ncmp, HEAD_DIM), lambda b, i: (b, 0, 0, 0)),
            pl.BlockSpec((None, N_KV, HEAD_DIM, ncmp), lambda b, i: (b, 0, 0, 0)),
            pl.BlockSpec((None, None, 2, S, 128), lambda b, i: (0, b, 0, 0, 0)),
            pl.BlockSpec((None, None, 2, S, 128), lambda b, i: (1, b, 0, 0, 0)),
            pl.BlockSpec((None, None, nch, N_KV, VROWS, KCH), lambda b, i: (0, b, 0, 0, 0, 0)),
            pl.BlockSpec((None, None, nch, N_KV, VROWS, KCH), lambda b, i: (1, b, 0, 0, 0, 0)),
            pl.BlockSpec((Q_BLK * qpb, NG_PAD), lambda b, i: (b * nq + i, 0)),
        ],
        out_specs=pl.BlockSpec((N_KV, Q_BLK * qpb, GW), tok),
        out_shape=jax.ShapeDtypeStruct((N_KV, T, GW), BF16),
        scratch_shapes=[
            pltpu.VMEM((N_KV, 2 * HEAD_DIM, GW), BF16),
            pltpu.VMEM((N_KV, HEAD_DIM, GW), F32),
            pltpu.VMEM((N_KV, nsel, GW), F32),
            pltpu.VMEM((N_KV, 1, GW), F32),
            pltpu.VMEM((N_KV, VROWS, GW), F32),
            pltpu.VMEM((N_KV, 1, GW), F32),
            pltpu.VMEM((N_KV, VROWS, GW), F32),
            pltpu.VMEM((N_KV, KCH, GW), F32),
            pltpu.VMEM((N_KV, KCH, GW), F32),
            pltpu.VMEM((N_KV, 8, GW), F32),
            pltpu.VMEM((N_KV, ncmp, GW), F32),
            pltpu.VMEM((nsel, GW), F32),
        ],
        compiler_params=_params(("parallel", "arbitrary")),
        name="nsa",
    )(zq, qgain, kc, vct, kn, kn, vt, vt, ng)


def _merge_kernel(x_ref, ya_ref, o_ref_in, mg_ref, wn_ref, wo_ref, out_ref):
    yb = _dot(o_ref_in[0], wn_ref[0:GW, :])
    for g in range(1, N_KV):
        yb = yb + _dot(o_ref_in[g], wn_ref[g * GW:(g + 1) * GW, :])
    tm = x_ref.shape[0]
    ya = jnp.concatenate([ya_ref[ck].reshape(tm, 128) for ck in range(D_MODEL // 128)], axis=1)
    mixed = (mg_ref[:, 0:D_MODEL].astype(F32) * ya
             + mg_ref[:, D_MODEL:2 * D_MODEL].astype(F32) * yb)
    out_ref[...] = x_ref[...] + _dot(mixed.astype(BF16), wo_ref[...])


def _merge(x2d, ya, o, mg, wn, wo, *, B, S, tm=512):
    ni = S // tm
    tok = lambda b, i: (b * ni + i, 0)
    return pl.pallas_call(
        _merge_kernel,
        grid=(B, ni),
        in_specs=[
            pl.BlockSpec((tm, D_MODEL), tok),
            pl.BlockSpec((D_MODEL // 128, tm // 8, None, 8, 128), lambda b, i: (0, i, b, 0, 0)),
            pl.BlockSpec((N_KV, tm, GW), lambda b, i: (0, b * ni + i, 0)),
            pl.BlockSpec((tm, 2 * D_MODEL), tok),
            _const_spec((N_HEADS * HEAD_DIM, D_MODEL)),
            _const_spec((D_MODEL, D_MODEL)),
        ],
        out_specs=pl.BlockSpec((tm, D_MODEL), tok),
        out_shape=jax.ShapeDtypeStruct((B * S, D_MODEL), F32),
        compiler_params=_params(("parallel", "parallel")),
        name="merge",
    )(x2d, ya, o, mg, wn, wo)


def _ffn_kernel(x_ref, g_ref, wg_ref, wu_ref, wd_ref, o_ref, h_ref, *, tf):
    x = x_ref[...]
    y = x * lax.rsqrt(jnp.mean(x * x, axis=-1, keepdims=True) + EPS) * g_ref[...]
    h_ref[...] = y.astype(BF16)
    o_ref[...] = x
    for j in range(D_FF // tf):
        h = h_ref[...]
        sl = slice(j * tf, (j + 1) * tf)
        act = jax.nn.silu(_dot(h, wg_ref[:, sl])) * _dot(h, wu_ref[:, sl])
        o_ref[...] += _dot(act.astype(BF16), wd_ref[sl, :])


def _ffn(x2d, gain, wg, wu, wd, *, tm=512, tf=256):
    T = x2d.shape[0]
    assert T % tm == 0 and D_FF % tf == 0
    once = dict(pipeline_mode=pl.Buffered(1))
    return pl.pallas_call(
        functools.partial(_ffn_kernel, tf=tf),
        grid=(T // tm,),
        in_specs=[
            pl.BlockSpec((tm, D_MODEL), lambda i: (i, 0)),
            _const_spec((1, D_MODEL)),
            pl.BlockSpec((D_MODEL, D_FF), lambda i: (0, 0), **once),
            pl.BlockSpec((D_MODEL, D_FF), lambda i: (0, 0), **once),
            pl.BlockSpec((D_FF, D_MODEL), lambda i: (0, 0), **once),
        ],
        out_specs=pl.BlockSpec((tm, D_MODEL), lambda i: (i, 0)),
        out_shape=jax.ShapeDtypeStruct((T, D_MODEL), F32),
        scratch_shapes=[pltpu.VMEM((tm, D_MODEL), BF16)],
        compiler_params=_params(("parallel",)),
        name="ffn",
    )(x2d, gain, wg, wu, wd)


def _layer(x, p):
    B, S, _ = x.shape
    T = B * S
    dh = HEAD_DIM
    row = lambda v: v.reshape(1, -1)
    w_in = p["w_in"]
    n_ng = 3 * N_HEADS
    o_ng = 2 * D_RNN + N_HEADS * dh + 6 * KVW
    w_packed = jnp.concatenate(
        [w_in[:, :o_ng], jnp.pad(w_in[:, o_ng:o_ng + n_ng], ((0, 0), (0, NG_PAD - n_ng))), w_in[:, o_ng + n_ng:]],
        axis=1).astype(BF16)
    rx, gy, zq, zkv, ng, mg = _inproj(x, row(p["norm1"]), w_packed)

    slabs = lambda a: a.reshape(a.shape[0], S * B, 128)
    ya = _rglru(slabs(rx), slabs(gy), p["conv_w"], row(p["conv_b"]),
                p["rg_wa"].astype(BF16), row(p["rg_ba"]), p["rg_wi"].astype(BF16), row(p["rg_bi"]),
                row(p["rg_lambda"]), p["w_rg_out"].astype(BF16), nb=B)
    ya = ya.reshape(D_MODEL // 128, S // 8, B, 8, 128)

    kgains = jnp.tile(p["k_norm"][1:3], (1, N_KV)).reshape(2, 1, KVW)
    kn, vt = _kvprep(zkv, kgains, B=B, S=S)
    pos = jnp.stack([p["cmp_pos_k"], p["cmp_pos_v"]]).reshape(2, CMP_STRIDE, 2 * dh)
    w1 = jnp.stack([p["cmp_k_w1"], p["cmp_v_w1"]]).astype(BF16)
    w2 = jnp.stack([p["cmp_k_w2"], p["cmp_v_w2"]]).astype(BF16)
    kc, vct = _compress(zkv, pos, w1, w2, row(p["k_norm"][0]), B=B, S=S)
    qgain = jnp.broadcast_to((p["q_norm"] * (dh ** -0.5 * LOG2E))[:, None], (dh, GW))
    o = _nsa(zq, qgain, kc, vct, kn, vt, ng, B=B, S=S)

    x1 = _merge(x.reshape(T, D_MODEL), ya, o, mg, p["w_nsa_out"].astype(BF16), p["w_o"].astype(BF16), B=B, S=S)
    x2 = _ffn(x1, row(p["norm2"]), p["w_gate"].astype(BF16), p["w_up"].astype(BF16), p["w_down"].astype(BF16))
    return x2.reshape(B, S, D_MODEL)


def kernel(x, norm1, w_in, conv_w, conv_b, rg_wa, rg_ba, rg_wi, rg_bi, rg_lambda, q_norm, k_norm, cmp_pos_k,
           cmp_pos_v, cmp_k_w1, cmp_k_w2, cmp_v_w1, cmp_v_w2, w_rg_out, w_nsa_out, w_o, norm2, w_gate, w_up, w_down):
    params = dict(norm1=norm1, w_in=w_in, conv_w=conv_w, conv_b=conv_b, rg_wa=rg_wa, rg_ba=rg_ba, rg_wi=rg_wi,
                  rg_bi=rg_bi, rg_lambda=rg_lambda, q_norm=q_norm, k_norm=k_norm, cmp_pos_k=cmp_pos_k,
                  cmp_pos_v=cmp_pos_v, cmp_k_w1=cmp_k_w1, cmp_k_w2=cmp_k_w2, cmp_v_w1=cmp_v_w1, cmp_v_w2=cmp_v_w2,
                  w_rg_out=w_rg_out, w_nsa_out=w_nsa_out, w_o=w_o, norm2=norm2, w_gate=w_gate, w_up=w_up,
                  w_down=w_down)
    for l in range(norm1.shape[0]):
        x = _layer(x, {k: v[l] for k, v in params.items()})
    return x
```

```python
import functools

import jax
import jax.numpy as jnp
from jax import lax
from jax.experimental import pallas as pl
from jax.experimental.pallas import tpu as pltpu

D_MODEL = 1024
D_RNN = 1024
RG_BLOCKS = 4
RG_BW = D_RNN // RG_BLOCKS
CONV_W = 4
RG_C = 8.0
N_HEADS = 16
N_KV = 4
HEAD_DIM = 64
HPG = N_HEADS // N_KV
CMP_LEN = 32
CMP_STRIDE = 16
CMP_HID = 256
SEL_LEN = 64
SEL_TOPN = 16
WINDOW = 512
Q_BLK = 64
D_FF = 2816
EPS = 1e-6
FORCE_SCORE = 1e6
NEG = -1e30
LOG2E = 1.4426950408889634

KCH = 256
GW = HPG * HEAD_DIM
KVW = N_KV * HEAD_DIM
VROWS = 80
NG_PAD = 128

F32 = jnp.float32
BF16 = jnp.bfloat16

VMEM_LIMIT = 56 * 1024 * 1024


def _params(sem):
    return pltpu.CompilerParams(dimension_semantics=sem, vmem_limit_bytes=VMEM_LIMIT)


def _dot(a, b):
    return jnp.dot(a, b, preferred_element_type=F32)


def _dot_nt(a, b):
    return lax.dot_general(a, b, (((1,), (1,)), ((), ())), preferred_element_type=F32)


def _split(x):
    hi = x.astype(BF16)
    lo = (x - hi.astype(F32)).astype(BF16)
    return hi, lo


def _iota(shape, dim):
    return lax.broadcasted_iota(jnp.int32, shape, dim)


def _const_spec(shape):
    return pl.BlockSpec(shape, lambda *_: (0,) * len(shape))


_O_RY = D_RNN
_O_Q = 2 * D_RNN
_O_KV = _O_Q + N_HEADS * HEAD_DIM
_O_NG = _O_KV + 6 * KVW
_O_MG = _O_NG + NG_PAD
_W_COLS = _O_MG + 2 * D_MODEL


def _inproj_kernel(x_ref, g_ref, w_ref, rx_ref, gy_ref, zq_ref, zkv_ref, ng_ref, mg_ref, h_ref):
    x = x_ref[...]
    y = x * lax.rsqrt(jnp.mean(x * x, axis=-1, keepdims=True) + EPS) * g_ref[...]
    h_ref[...] = y.astype(BF16)

    def proj(c0, width):
        return _dot(h_ref[...], w_ref[:, c0:c0 + width])

    cw = 256
    tm = x_ref.shape[0]
    for c in range(D_RNN // cw):
        rx = proj(c * cw, cw)
        gy = jax.nn.gelu(proj(_O_RY + c * cw, cw))
        for hf in range(cw // 128):
            rx_ref[c * (cw // 128) + hf] = rx[:, hf * 128:(hf + 1) * 128].reshape(tm // 8, 8, 128)
            gy_ref[c * (cw // 128) + hf] = gy[:, hf * 128:(hf + 1) * 128].reshape(tm // 8, 8, 128)
    for g in range(N_KV):
        zq_ref[g] = proj(_O_Q + g * GW, GW)
    for s in range(6):
        z = proj(_O_KV + s * KVW, KVW)
        for pr in range(KVW // 128):
            zkv_ref[s, pr] = z[:, pr * 128:(pr + 1) * 128]
    ng_ref[...] = jax.nn.sigmoid(proj(_O_NG, NG_PAD))
    for c in range(2 * D_MODEL // cw):
        mg_ref[:, c * cw:(c + 1) * cw] = jax.nn.sigmoid(proj(_O_MG + c * cw, cw)).astype(mg_ref.dtype)


def _inproj(x, gain, w, *, tm=256):
    B, S, D = x.shape
    T = B * S
    ni = S // tm
    nck = D_RNN // 128
    tok = lambda b, i: (b * ni + i, 0)
    slab_spec = pl.BlockSpec((nck, tm // 8, None, 8, 128), lambda b, i: (0, i, b, 0, 0))
    slab_shape = jax.ShapeDtypeStruct((nck, S // 8, B, 8, 128), F32)
    return pl.pallas_call(
        _inproj_kernel,
        grid=(B, ni),
        in_specs=[
            pl.BlockSpec((None, tm, D), lambda b, i: (b, i, 0)),
            _const_spec((1, D)),
            pl.BlockSpec((D, _W_COLS), lambda b, i: (0, 0), pipeline_mode=pl.Buffered(1)),
        ],
        out_specs=[
            slab_spec,
            slab_spec,
            pl.BlockSpec((N_KV, tm, GW), lambda b, i: (0, b * ni + i, 0)),
            pl.BlockSpec((6, KVW // 128, tm, 128), lambda b, i: (0, 0, b * ni + i, 0)),
            pl.BlockSpec((tm, NG_PAD), tok),
            pl.BlockSpec((tm, 2 * D_MODEL), tok),
        ],
        out_shape=[
            slab_shape,
            slab_shape,
            jax.ShapeDtypeStruct((N_KV, T, GW), F32),
            jax.ShapeDtypeStruct((6, KVW // 128, T, 128), F32),
            jax.ShapeDtypeStruct((T, NG_PAD), F32),
            jax.ShapeDtypeStruct((T, 2 * D_MODEL), BF16),
        ],
        scratch_shapes=[pltpu.VMEM((tm, D), BF16)],
        compiler_params=_params(("parallel", "parallel")),
        name="inproj",
    )(x, gain, w)


def _rglru_kernel(x_ref, gy_ref, cw_ref, cb_ref, wa_ref, ba_ref, wi_ref, bi_ref, lam_ref, wo_ref,
                  o_ref, xb_ref, gyb_ref, a_ref, u_ref, hb_ref, hs_ref, yb_ref, *, nb, tt):
    R = nb * tt
    halo = (CONV_W - 1) * nb
    step = pl.program_id(0)
    nck = D_RNN // 128

    @pl.when(step == 0)
    def _():
        xb_ref[0:halo, :] = jnp.zeros((halo, D_RNN), F32)
        hs_ref[...] = jnp.zeros((nb, D_RNN), F32)

    def slab_rows(t8, t_lo):
        return pl.ds(t8 * (8 * nb) + t_lo, nb, stride=8)

    def stage_in(t8, carry):
        for t_lo in range(8):
            r0 = pl.multiple_of((t8 * 8 + t_lo) * nb, nb)
            for ck in range(nck):
                cols = slice(ck * 128, (ck + 1) * 128)
                xb_ref[pl.ds(halo + r0, nb), cols] = x_ref[ck, slab_rows(t8, t_lo), :]
                gyb_ref[pl.ds(r0, nb), cols] = gy_ref[ck, slab_rows(t8, t_lo), :]
        return carry

    lax.fori_loop(0, tt // 8, stage_in, 0)
    xr = cb_ref[...] + cw_ref[0:1, :] * xb_ref[0:R, :]
    for k in range(1, CONV_W):
        xr = xr + cw_ref[k:k + 1, :] * xb_ref[k * nb:k * nb + R, :]
    tail = xb_ref[R:R + halo, :]
    xb_ref[0:halo, :] = tail

    xrb = xr.astype(BF16)
    sp = jax.nn.softplus(-lam_ref[...])
    row = _iota((R, RG_BW), 0)
    first = (row < nb) & (step == 0)
    for n in range(RG_BLOCKS):
        sl = slice(n * RG_BW, (n + 1) * RG_BW)
        xn = xrb[:, sl]
        r = jax.nn.sigmoid(_dot(xn, wa_ref[n]) + ba_ref[:, sl])
        ig = jax.nn.sigmoid(_dot(xn, wi_ref[n]) + bi_ref[:, sl])
        log_a = (-RG_C) * r * sp[:, sl]
        a = jnp.exp(log_a)
        mult = jnp.sqrt(1.0 - a * a)
        mult = jnp.where(first, 1.0, mult)
        a_ref[:, sl] = a
        u_ref[:, sl] = mult * ig * xr[:, sl]

    def body(t, h):
        r0 = pl.multiple_of(t * nb, nb)
        h = a_ref[pl.ds(r0, nb), :] * h + u_ref[pl.ds(r0, nb), :]
        hb_ref[pl.ds(r0, nb), :] = h
        return h

    h = lax.fori_loop(0, tt, body, hs_ref[...], unroll=8)
    hs_ref[...] = h
    yb_ref[...] = _dot((hb_ref[...] * gyb_ref[...]).astype(BF16), wo_ref[...])

    def stage_out(t8, carry):
        for t_lo in range(8):
            r0 = pl.multiple_of((t8 * 8 + t_lo) * nb, nb)
            for ck in range(nck):
                o_ref[ck, slab_rows(t8, t_lo), :] = yb_ref[pl.ds(r0, nb), ck * 128:(ck + 1) * 128]
        return carry

    lax.fori_loop(0, tt // 8, stage_out, 0)


def _rglru(rx, gy, conv_w, conv_b, wa, ba, wi, bi, lam, wo, *, nb, tt=32):
    nck, rows, _ = rx.shape
    S = rows // nb
    assert S % tt == 0 and tt % 8 == 0
    R = nb * tt
    halo = (CONV_W - 1) * nb
    return pl.pallas_call(
        functools.partial(_rglru_kernel, nb=nb, tt=tt),
        grid=(S // tt,),
        in_specs=[
            pl.BlockSpec((nck, R, 128), lambda s: (0, s, 0)),
            pl.BlockSpec((nck, R, 128), lambda s: (0, s, 0)),
            _const_spec((CONV_W, D_RNN)),
            _const_spec((1, D_RNN)),
            _const_spec((RG_BLOCKS, RG_BW, RG_BW)),
            _const_spec((1, D_RNN)),
            _const_spec((RG_BLOCKS, RG_BW, RG_BW)),
            _const_spec((1, D_RNN)),
            _const_spec((1, D_RNN)),
            _const_spec((D_RNN, D_MODEL)),
        ],
        out_specs=pl.BlockSpec((D_MODEL // 128, R, 128), lambda s: (0, s, 0)),
        out_shape=jax.ShapeDtypeStruct((D_MODEL // 128, rows, 128), F32),
        scratch_shapes=[
            pltpu.VMEM((R + halo, D_RNN), F32),
            pltpu.VMEM((R, D_RNN), F32),
            pltpu.VMEM((R, D_RNN), F32),
            pltpu.VMEM((R, D_RNN), F32),
            pltpu.VMEM((R, D_RNN), F32),
            pltpu.VMEM((nb, D_RNN), F32),
            pltpu.VMEM((R, D_MODEL), F32),
        ],
        compiler_params=_params(("arbitrary",)),
        name="rglru",
    )(rx, gy, conv_w, conv_b, wa, ba, wi, bi, lam, wo)


def _kvprep_kernel(zk_ref, zv_ref, g_ref, kn_ref, vt_ref):
    rows = zk_ref.shape[1]
    x = jnp.concatenate([zk_ref[0], zk_ref[1]], axis=1)
    seg = jnp.where(_iota((KVW, KVW), 0) // HEAD_DIM == _iota((KVW, KVW), 1) // HEAD_DIM, 1.0, 0.0).astype(BF16)
    hi, lo = _split(x * x)
    ssq = _dot(jnp.concatenate([hi, lo], axis=0), seg)
    kn = (x * lax.rsqrt((ssq[0:rows] + ssq[rows:2 * rows]) * (1.0 / HEAD_DIM) + EPS) * g_ref[...]).astype(BF16)
    for pr in range(KVW // 128):
        kn_ref[pr] = kn[:, pr * 128:(pr + 1) * 128]
    eye = jnp.where(_iota((KVW, KVW), 0) == _iota((KVW, KVW), 1), 1.0, 0.0).astype(BF16)
    pad = jnp.where(_iota((VROWS - HEAD_DIM, KCH), 0) == 0, 1.0, 0.0).astype(BF16)
    for c in range(rows // KCH):
        sl = slice(c * KCH, (c + 1) * KCH)
        v = jnp.concatenate([zv_ref[0, sl, :], zv_ref[1, sl, :]], axis=1).astype(BF16)
        vt = _dot_nt(eye, v).astype(BF16)
        for g in range(N_KV):
            vt_ref[c, g, 0:HEAD_DIM, :] = vt[g * HEAD_DIM:(g + 1) * HEAD_DIM]
            vt_ref[c, g, HEAD_DIM:VROWS, :] = pad


def _kvprep(zkv, gains, *, B, S, cpb=4):
    nch = S // KCH
    assert nch % cpb == 0
    nblk = nch // cpb
    rows = cpb * KCH
    return pl.pallas_call(
        _kvprep_kernel,
        grid=(2, B, nblk),
        in_specs=[
            pl.BlockSpec((None, KVW // 128, rows, 128), lambda w, b, i: (2 + 2 * w, 0, b * nblk + i, 0)),
            pl.BlockSpec((None, KVW // 128, rows, 128), lambda w, b, i: (3 + 2 * w, 0, b * nblk + i, 0)),
            pl.BlockSpec((None, 1, KVW), lambda w, b, i: (w, 0, 0)),
        ],
        out_specs=[
            pl.BlockSpec((None, None, KVW // 128, rows, 128), lambda w, b, i: (w, b, 0, i, 0)),
            pl.BlockSpec((None, None, cpb, N_KV, VROWS, KCH), lambda w, b, i: (w, b, i, 0, 0, 0)),
        ],
        out_shape=[
            jax.ShapeDtypeStruct((2, B, KVW // 128, S, 128), BF16),
            jax.ShapeDtypeStruct((2, B, nch, N_KV, VROWS, KCH), BF16),
        ],
        compiler_params=_params(("parallel", "parallel", "parallel")),
        name="kvprep",
    )(zkv, zkv, gains)


def _compress_kernel(zk_ref, zv_ref, pos_ref, w1_ref, w2_ref, g_ref, kc_ref, vct_ref):
    nch = zk_ref.shape[0] // CMP_STRIDE
    lane = _iota((nch, 128), 1)
    npair = CMP_STRIDE // 2
    eye = jnp.where(_iota((HEAD_DIM, HEAD_DIM), 0) == _iota((HEAD_DIM, HEAD_DIM), 1), 1.0, 0.0).astype(BF16)
    for kind, z_ref in enumerate((zk_ref, zv_ref)):
        for ge in range(2):
            combs = []
            for p in range(npair):
                a = z_ref[pl.ds(2 * p, nch, stride=CMP_STRIDE), :]
                b = z_ref[pl.ds(2 * p + 1, nch, stride=CMP_STRIDE), :]
                if ge == 0:
                    combs.append(jnp.where(lane < HEAD_DIM, a, pltpu.roll(b, HEAD_DIM, axis=1)))
                else:
                    combs.append(jnp.where(lane < HEAD_DIM, pltpu.roll(a, HEAD_DIM, axis=1), b))
            halves = []
            half_w = npair * 128
            for half in range(2):
                lhs = jnp.concatenate(
                    [(combs[p] + pos_ref[kind, half * npair + p:half * npair + p + 1, :]).astype(BF16)
                     for p in range(npair)], axis=1)
                halves.append(_dot(lhs, w1_ref[kind, half * half_w:(half + 1) * half_w, :]))
            hid = halves[0] + pltpu.roll(halves[1], nch - 1, axis=0)
            hid = jax.nn.gelu(hid)
            out = _dot(hid.astype(BF16), w2_ref[kind])
            if kind == 0:
                out = out * lax.rsqrt(jnp.mean(out * out, axis=-1, keepdims=True) + EPS) * g_ref[...]
            res = jnp.where(_iota(out.shape, 0) < nch - 1, out, 0.0).astype(BF16)
            if kind == 0:
                kc_ref[ge] = res
            else:
                vct_ref[ge] = _dot_nt(eye, res).astype(BF16)


def _compress(zkv, pos, w1, w2, gain, *, B, S):
    nch = S // CMP_STRIDE
    return pl.pallas_call(
        _compress_kernel,
        grid=(B, KVW // 128),
        in_specs=[
            pl.BlockSpec((None, None, S, 128), lambda b, lp: (0, lp, b, 0)),
            pl.BlockSpec((None, None, S, 128), lambda b, lp: (1, lp, b, 0)),
            _const_spec((2, CMP_STRIDE, 128)),
            _const_spec((2, CMP_LEN * HEAD_DIM, CMP_HID)),
            _const_spec((2, CMP_HID, HEAD_DIM)),
            _const_spec((1, HEAD_DIM)),
        ],
        out_specs=[
            pl.BlockSpec((None, 2, nch, HEAD_DIM), lambda b, lp: (b, lp, 0, 0)),
            pl.BlockSpec((None, 2, HEAD_DIM, nch), lambda b, lp: (b, lp, 0, 0)),
        ],
        out_shape=[
            jax.ShapeDtypeStruct((B, N_KV, nch, HEAD_DIM), BF16),
            jax.ShapeDtypeStruct((B, N_KV, HEAD_DIM, nch), BF16),
        ],
        compiler_params=_params(("parallel", "parallel")),
        name="compress",
    )(zkv, zkv, pos, w1, w2, gain)


def _spread_heads(x):
    y = x + pltpu.roll(x, 2 * Q_BLK, axis=1)
    return y + pltpu.roll(y, Q_BLK, axis=1)


def _swap_heads(x):
    n = x.shape[0]
    lane = _iota((n, 128), 1)
    halves = []
    for pr in range(x.shape[1] // 128):
        slab = jnp.concatenate([x[:, pr * 128:(pr + 1) * 128], jnp.zeros((128 - n, 128), x.dtype)], axis=0)
        tr = slab.T
        halves.append(jnp.where(lane < n, tr[0:n], pltpu.roll(tr[n:2 * n], n, axis=1)))
    return jnp.concatenate(halves, axis=1)


def _nsa_kernel(*refs, nsel, qpb):
    for k in range(qpb):
        _nsa_block(k, *refs, nsel=nsel, qpb=qpb)


def _nsa_block(k, zq_ref, qg_ref, kc_ref, vct_ref, ks_ref, kw_ref, vst_ref, vwt_ref, ng_ref, o_ref,
               qtp_ref, oc_ref, bias_ref, ms_ref, accs_ref, mw_ref, accw_ref, sbuf_ref, wbuf_ref,
               gates_ref, scs_ref, wbias_ref, *, nsel, qpb):
    qi = pl.program_id(1) * qpb + k
    q_rows = pl.ds(k * Q_BLK, Q_BLK)
    s0 = qi * Q_BLK
    cur = qi
    ncmp = kc_ref.shape[1]
    per_chunk = KCH // SEL_LEN

    lane_t = s0 + (_iota((1, GW), 1) % Q_BLK)
    lane_grp = _iota((1, GW), 1) // Q_BLK
    ov_n = _iota((nsel, ncmp), 0) * SEL_LEN
    ov_c = _iota((nsel, ncmp), 1) * CMP_STRIDE
    overlap = jnp.where((ov_c < ov_n + SEL_LEN) & (ov_c + CMP_LEN > ov_n), 1.0, 0.0).astype(BF16)


    ok_c = _iota((ncmp, GW), 0) * CMP_STRIDE + (CMP_LEN - 1) <= lane_t
    groups = range(N_KV)
    ngt = jnp.concatenate([ng_ref[q_rows, :], jnp.zeros((128 - Q_BLK, NG_PAD), F32)], axis=0).T
    ngt_hi = pltpu.roll(ngt, Q_BLK, axis=1)
    low_half = _iota((1, 128), 1) < Q_BLK
    for g in groups:
        for c in range(3):
            r = c * N_HEADS + g * HPG
            pairs = [jnp.where(low_half, ngt[r + j:r + j + 1], ngt_hi[r + j + 1:r + j + 2]) for j in (0, 2)]
            gates_ref[g, c:c + 1, :] = jnp.concatenate(pairs, axis=1)

    qts = []
    for g in groups:
        xt = _swap_heads(zq_ref[g, q_rows, :])
        ssq = jnp.sum(xt * xt, axis=0, keepdims=True)
        qts.append((xt * lax.rsqrt(ssq * (1.0 / HEAD_DIM) + EPS) * qg_ref[...]).astype(BF16))
    scs = []
    for g in groups:
        off = (g % 2) * HEAD_DIM
        qtp_ref[g, off:off + HEAD_DIM, :] = qts[g]
        qtp_ref[g, HEAD_DIM - off:2 * HEAD_DIM - off, :] = jnp.zeros((HEAD_DIM, GW), BF16)
        scs.append(_dot(kc_ref[g], qts[g]))

    def scores(k_ref, i):
        k0 = pl.multiple_of(i * KCH, KCH)
        return [_dot(k_ref[g // 2, pl.ds(k0, KCH), :], qtp_ref[g]) for g in groups]

    def park(buf_ref, vals):
        for g in groups:
            buf_ref[g] = vals[g]

    n_chunks = qi // per_chunk + 1
    n_win = WINDOW // KCH + 1
    win_ids = [jnp.maximum(n_chunks - (n_win - j), 0) for j in range(n_win)]
    win_dead = [jnp.where(n_chunks >= n_win - j, 0.0, NEG) for j in range(n_win)]
    park(scs_ref, scs)
    park(sbuf_ref, scores(ks_ref, 0))
    park(wbuf_ref, scores(kw_ref, win_ids[0]))

    key_row = _iota((SEL_LEN, GW), 0)
    in_blk_t = _iota((SEL_LEN, GW), 1) % Q_BLK
    causal_tri = jnp.where(key_row <= in_blk_t, 0.0, NEG)
    expired_tri = jnp.where(key_row > in_blk_t, 0.0, NEG)
    win_blocks = WINDOW // SEL_LEN
    cur_slab = k % per_chunk
    cut_rows = slice(cur_slab * SEL_LEN, (cur_slab + 1) * SEL_LEN)
    last_slabs = per_chunk // 2 if cur_slab < per_chunk // 2 else per_chunk
    n_all = _iota((nsel, GW), 0)
    wbias_ref[...] = jnp.where((n_all >= cur - win_blocks) & (n_all <= cur), 0.0, NEG)

    def cut(buf_ref, tri):
        for g in groups:
            buf_ref[g, cut_rows, :] = buf_ref[g, cut_rows, :] + tri

    cut(wbuf_ref, jnp.where(cur >= win_blocks, expired_tri, 0.0))

    ms_ref[...] = jnp.full(ms_ref.shape, NEG, F32)
    mw_ref[...] = jnp.full(mw_ref.shape, NEG, F32)
    accs_ref[...] = jnp.zeros(accs_ref.shape, F32)
    accw_ref[...] = jnp.zeros(accw_ref.shape, F32)

    def update(g, blocks, shifts, vt, m_ref, acc_ref):
        m = m_ref[g]
        mnew = m
        for blk, sh in zip(blocks, shifts):
            mnew = jnp.maximum(mnew, jnp.max(blk, axis=0, keepdims=True) + sh)
        e = jnp.concatenate([jnp.exp2(blk + (sh - mnew)) for blk, sh in zip(blocks, shifts)], axis=0).astype(BF16)
        acc_ref[g] = jnp.exp2(m - mnew) * acc_ref[g] + _dot(vt, e)
        m_ref[g] = mnew

    def slabs(buf_ref, g, ns):
        return [buf_ref[g, nb * SEL_LEN:(nb + 1) * SEL_LEN, :] for nb in range(ns)]

    def sel_softmax(i, ns=per_chunk):
        for g in groups:
            shifts = [bias_ref[g, pl.ds(i * per_chunk + nb, 1), :] for nb in range(ns)]
            update(g, slabs(sbuf_ref, g, ns), shifts, vst_ref[i, g, :, 0:ns * SEL_LEN], ms_ref, accs_ref)

    def win_softmax(i, dead, ns=per_chunk):
        shifts = [wbias_ref[pl.ds(i * per_chunk + nb, 1), :] + dead for nb in range(ns)]
        for g in groups:
            update(g, slabs(wbuf_ref, g, ns), shifts, vwt_ref[i, g, :, 0:ns * SEL_LEN], mw_ref, accw_ref)

    def sel_step(i, carry):
        nxt = scores(ks_ref, i + 1)
        sel_softmax(i)
        park(sbuf_ref, nxt)
        return carry

    for j in range(n_win - 1):
        nxt = scores(kw_ref, win_ids[j + 1])
        win_softmax(win_ids[j], win_dead[j])
        park(wbuf_ref, nxt)

    imp_all = jnp.zeros((nsel, GW), F32)
    some_c = lane_t >= CMP_LEN - 1
    for g in groups:
        scm = jnp.where(ok_c, scs_ref[g], NEG)
        ec = jnp.exp2(scm - jnp.max(scm, axis=0, keepdims=True))
        inv = jnp.where(some_c, 1.0 / jnp.maximum(jnp.sum(ec, axis=0, keepdims=True), 1e-30), 0.0)
        both = _dot(jnp.concatenate([vct_ref[g], overlap], axis=0), ec.astype(BF16)) * inv
        oc_ref[g] = both[0:HEAD_DIM]
        imp = _spread_heads(both[HEAD_DIM:HEAD_DIM + nsel])
        imp_all = jnp.where(lane_grp == g, imp, imp_all)

    n_io = _iota((nsel, GW), 0)
    forced = (n_io == 0) | (n_io == cur) | (n_io == cur - 1)
    valid = n_io <= cur
    score = jnp.where(forced, FORCE_SCORE, imp_all)
    score = jnp.where(valid, score, -1.0)
    rows8 = [score[r:r + 8] for r in range(0, nsel, 8)]
    ranks = [jnp.zeros((8, GW), F32) for _ in rows8]
    sub = _iota((8, GW), 0)
    for m in range(nsel):
        rowv = score[m:m + 1, :]
        for k, blk in enumerate(rows8):
            ge = jnp.where(rowv >= blk, 1.0, 0.0)
            gt = jnp.where(rowv > blk, 1.0, 0.0)
            if 8 * k > m:
                inc = ge
            elif 8 * k + 7 < m:
                inc = gt
            else:
                inc = jnp.where(sub + 8 * k > m, ge, gt)
            ranks[k] = ranks[k] + inc
    rank = jnp.concatenate(ranks, axis=0)
    sel_all = jnp.where((rank < float(min(SEL_TOPN, nsel))) & valid, 1.0, 0.0)
    for g in groups:
        mine = jnp.where(lane_grp == g, sel_all, 0.0)
        bias_ref[g] = (_spread_heads(mine) - 1.0) * (-NEG)

    cut(wbuf_ref, causal_tri)
    win_softmax(win_ids[n_win - 1], win_dead[n_win - 1], last_slabs)

    lax.fori_loop(0, n_chunks - 1, sel_step, 0)

    cut(sbuf_ref, causal_tri)
    sel_softmax(n_chunks - 1, last_slabs)
    outs = []
    for g in groups:
        o_s = accs_ref[g, 0:HEAD_DIM, :] * (1.0 / jnp.maximum(accs_ref[g, HEAD_DIM:HEAD_DIM + 1, :], 1e-30))
        o_w = accw_ref[g, 0:HEAD_DIM, :] * (1.0 / jnp.maximum(accw_ref[g, HEAD_DIM:HEAD_DIM + 1, :], 1e-30))
        gates = gates_ref[g]
        o_t = gates[0:1] * oc_ref[g] + gates[1:2] * o_s + gates[2:3] * o_w
        outs.append(_swap_heads(o_t))
    for g in groups:
        o_ref[g, q_rows, :] = outs[g].astype(o_ref.dtype)


def _nsa(zq, qgain, kc, vct, kn, vt, ng, *, B, S, qpb=4):
    T = B * S
    assert S % (Q_BLK * qpb) == 0 and qpb % (KCH // SEL_LEN) == 0
    nq = S // (Q_BLK * qpb)
    nch = S // KCH
    ncmp = S // CMP_STRIDE
    nsel = S // SEL_LEN
    tok = lambda b, i: (0, b * nq + i, 0)
    return pl.pallas_call(
        functools.partial(_nsa_kernel, nsel=nsel, qpb=qpb),
        grid=(B, nq),
        in_specs=[
            pl.BlockSpec((N_KV, Q_BLK * qpb, GW), tok),
            _const_spec((HEAD_DIM, GW)),
            pl.BlockSpec((None, N_KV, ncmp, HEAD_DIM), lambda b, i: (b, 0, 0, 0)),
            pl.BlockSpec((None, N_KV, HEAD_DIM, ncmp), lambda b, i: (b, 0, 0, 0)),
            pl.BlockSpec((None, None, 2, S, 128), lambda b, i: (0, b, 0, 0, 0)),
            pl.BlockSpec((None, None, 2, S, 128), lambda b, i: (1, b, 0, 0, 0)),
            pl.BlockSpec((None, None, nch, N_KV, VROWS, KCH), lambda b, i: (0, b, 0, 0, 0, 0)),
            pl.BlockSpec((None, None, nch, N_KV, VROWS, KCH), lambda b, i: (1, b, 0, 0, 0, 0)),
            pl.BlockSpec((Q_BLK * qpb, NG_PAD), lambda b, i: (b * nq + i, 0)),
        ],
        out_specs=pl.BlockSpec((N_KV, Q_BLK * qpb, GW), tok),
        out_shape=jax.ShapeDtypeStruct((N_KV, T, GW), BF16),
        scratch_shapes=[
            pltpu.VMEM((N_KV, 2 * HEAD_DIM, GW), BF16),
            pltpu.VMEM((N_KV, HEAD_DIM, GW), F32),
            pltpu.VMEM((N_KV, nsel, GW), F32),
            pltpu.VMEM((N_KV, 1, GW), F32),
            pltpu.VMEM((N_KV, VROWS, GW), F32),
            pltpu.VMEM((N_KV, 1, GW), F32),
            pltpu.VMEM((N_KV, VROWS, GW), F32),
            pltpu.VMEM((N_KV, KCH, GW), F32),
            pltpu.VMEM((N_KV, KCH, GW), F32),
            pltpu.VMEM((N_KV, 8, GW), F32),
            pltpu.VMEM((N_KV, ncmp, GW), F32),
            pltpu.VMEM((nsel, GW), F32),
        ],
        compiler_params=_params(("parallel", "arbitrary")),
        name="nsa",
    )(zq, qgain, kc, vct, kn, kn, vt, vt, ng)


def _merge_kernel(x_ref, ya_ref, o_ref_in, mg_ref, wn_ref, wo_ref, out_ref):
    yb = _dot(o_ref_in[0], wn_ref[0:GW, :])
    for g in range(1, N_KV):
        yb = yb + _dot(o_ref_in[g], wn_ref[g * GW:(g + 1) * GW, :])
    tm = x_ref.shape[0]
    ya = jnp.concatenate([ya_ref[ck].reshape(tm, 128) for ck in range(D_MODEL // 128)], axis=1)
    mixed = (mg_ref[:, 0:D_MODEL].astype(F32) * ya
             + mg_ref[:, D_MODEL:2 * D_MODEL].astype(F32) * yb)
    out_ref[...] = x_ref[...] + _dot(mixed.astype(BF16), wo_ref[...])


def _merge(x2d, ya, o, mg, wn, wo, *, B, S, tm=512):
    ni = S // tm
    tok = lambda b, i: (b * ni + i, 0)
    return pl.pallas_call(
        _merge_kernel,
        grid=(B, ni),
        in_specs=[
            pl.BlockSpec((tm, D_MODEL), tok),
            pl.BlockSpec((D_MODEL // 128, tm // 8, None, 8, 128), lambda b, i: (0, i, b, 0, 0)),
            pl.BlockSpec((N_KV, tm, GW), lambda b, i: (0, b * ni + i, 0)),
            pl.BlockSpec((tm, 2 * D_MODEL), tok),
            _const_spec((N_HEADS * HEAD_DIM, D_MODEL)),
            _const_spec((D_MODEL, D_MODEL)),
        ],
        out_specs=pl.BlockSpec((tm, D_MODEL), tok),
        out_shape=jax.ShapeDtypeStruct((B * S, D_MODEL), F32),
        compiler_params=_params(("parallel", "parallel")),
        name="merge",
    )(x2d, ya, o, mg, wn, wo)


def _ffn_kernel(x_ref, g_ref, wg_ref, wu_ref, wd_ref, o_ref, h_ref, *, tf):
    x = x_ref[...]
    y = x * lax.rsqrt(jnp.mean(x * x, axis=-1, keepdims=True) + EPS) * g_ref[...]
    h_ref[...] = y.astype(BF16)
    o_ref[...] = x
    for j in range(D_FF // tf):
        h = h_ref[...]
        sl = slice(j * tf, (j + 1) * tf)
        act = jax.nn.silu(_dot(h, wg_ref[:, sl])) * _dot(h, wu_ref[:, sl])
        o_ref[...] += _dot(act.astype(BF16), wd_ref[sl, :])


def _ffn(x2d, gain, wg, wu, wd, *, tm=512, tf=256):
    T = x2d.shape[0]
    assert T % tm == 0 and D_FF % tf == 0
    once = dict(pipeline_mode=pl.Buffered(1))
    return pl.pallas_call(
        functools.partial(_ffn_kernel, tf=tf),
        grid=(T // tm,),
        in_specs=[
            pl.BlockSpec((tm, D_MODEL), lambda i: (i, 0)),
            _const_spec((1, D_MODEL)),
            pl.BlockSpec((D_MODEL, D_FF), lambda i: (0, 0), **once),
            pl.BlockSpec((D_MODEL, D_FF), lambda i: (0, 0), **once),
            pl.BlockSpec((D_FF, D_MODEL), lambda i: (0, 0), **once),
        ],
        out_specs=pl.BlockSpec((tm, D_MODEL), lambda i: (i, 0)),
        out_shape=jax.ShapeDtypeStruct((T, D_MODEL), F32),
        scratch_shapes=[pltpu.VMEM((tm, D_MODEL), BF16)],
        compiler_params=_params(("parallel",)),
        name="ffn",
    )(x2d, gain, wg, wu, wd)


def _layer(x, p):
    B, S, _ = x.shape
    T = B * S
    dh = HEAD_DIM
    row = lambda v: v.reshape(1, -1)
    w_in = p["w_in"]
    n_ng = 3 * N_HEADS
    o_ng = 2 * D_RNN + N_HEADS * dh + 6 * KVW
    w_packed = jnp.concatenate(
        [w_in[:, :o_ng], jnp.pad(w_in[:, o_ng:o_ng + n_ng], ((0, 0), (0, NG_PAD - n_ng))), w_in[:, o_ng + n_ng:]],
        axis=1).astype(BF16)
    rx, gy, zq, zkv, ng, mg = _inproj(x, row(p["norm1"]), w_packed)

    slabs = lambda a: a.reshape(a.shape[0], S * B, 128)
    ya = _rglru(slabs(rx), slabs(gy), p["conv_w"], row(p["conv_b"]),
                p["rg_wa"].astype(BF16), row(p["rg_ba"]), p["rg_wi"].astype(BF16), row(p["rg_bi"]),
                row(p["rg_lambda"]), p["w_rg_out"].astype(BF16), nb=B)
    ya = ya.reshape(D_MODEL // 128, S // 8, B, 8, 128)

    kgains = jnp.tile(p["k_norm"][1:3], (1, N_KV)).reshape(2, 1, KVW)
    kn, vt = _kvprep(zkv, kgains, B=B, S=S)
    pos = jnp.stack([p["cmp_pos_k"], p["cmp_pos_v"]]).reshape(2, CMP_STRIDE, 2 * dh)
    w1 = jnp.stack([p["cmp_k_w1"], p["cmp_v_w1"]]).astype(BF16)
    w2 = jnp.stack([p["cmp_k_w2"], p["cmp_v_w2"]]).astype(BF16)
    kc, vct = _compress(zkv, pos, w1, w2, row(p["k_norm"][0]), B=B, S=S)
    qgain = jnp.broadcast_to((p["q_norm"] * (dh ** -0.5 * LOG2E))[:, None], (dh, GW))
    o = _nsa(zq, qgain, kc, vct, kn, vt, ng, B=B, S=S)

    x1 = _merge(x.reshape(T, D_MODEL), ya, o, mg, p["w_nsa_out"].astype(BF16), p["w_o"].astype(BF16), B=B, S=S)
    x2 = _ffn(x1, row(p["norm2"]), p["w_gate"].astype(BF16), p["w_up"].astype(BF16), p["w_down"].astype(BF16))
    return x2.reshape(B, S, D_MODEL)


def kernel(x, norm1, w_in, conv_w, conv_b, rg_wa, rg_ba, rg_wi, rg_bi, rg_lambda, q_norm, k_norm, cmp_pos_k,
           cmp_pos_v, cmp_k_w1, cmp_k_w2, cmp_v_w1, cmp_v_w2, w_rg_out, w_nsa_out, w_o, norm2, w_gate, w_up, w_down):
    params = dict(norm1=norm1, w_in=w_in, conv_w=conv_w, conv_b=conv_b, rg_wa=rg_wa, rg_ba=rg_ba, rg_wi=rg_wi,
                  rg_bi=rg_bi, rg_lambda=rg_lambda, q_norm=q_norm, k_norm=k_norm, cmp_pos_k=cmp_pos_k,
                  cmp_pos_v=cmp_pos_v, cmp_k_w1=cmp_k_w1, cmp_k_w2=cmp_k_w2, cmp_v_w1=cmp_v_w1, cmp_v_w2=cmp_v_w2,
                  w_rg_out=w_rg_out, w_nsa_out=w_nsa_out, w_o=w_o, norm2=norm2, w_gate=w_gate, w_up=w_up,
                  w_down=w_down)
    for l in range(norm1.shape[0]):
        x = _layer(x, {k: v[l] for k, v in params.items()})
    return x
```

```python
import functools

import jax
import jax.numpy as jnp
from jax import lax
from jax.experimental import pallas as pl
from jax.experimental.pallas import tpu as pltpu

D_MODEL = 1024
D_RNN = 1024
RG_BLOCKS = 4
RG_BW = D_RNN // RG_BLOCKS
CONV_W = 4
RG_C = 8.0
N_HEADS = 16
N_KV = 4
HEAD_DIM = 64
HPG = N_HEADS // N_KV
CMP_LEN = 32
CMP_STRIDE = 16
CMP_HID = 256
SEL_LEN = 64
SEL_TOPN = 16
WINDOW = 512
Q_BLK = 64
D_FF = 2816
EPS = 1e-6
FORCE_SCORE = 1e6
NEG = -1e30
LOG2E = 1.4426950408889634

KCH = 256
GW = HPG * HEAD_DIM
KVW = N_KV * HEAD_DIM
VROWS = 80
NG_PAD = 128

F32 = jnp.float32
BF16 = jnp.bfloat16

VMEM_LIMIT = 56 * 1024 * 1024


def _params(sem):
    return pltpu.CompilerParams(dimension_semantics=sem, vmem_limit_bytes=VMEM_LIMIT)


def _dot(a, b):
    return jnp.dot(a, b, preferred_element_type=F32)


def _dot_nt(a, b):
    return lax.dot_general(a, b, (((1,), (1,)), ((), ())), preferred_element_type=F32)


def _split(x):
    hi = x.astype(BF16)
    lo = (x - hi.astype(F32)).astype(BF16)
    return hi, lo


def _iota(shape, dim):
    return lax.broadcasted_iota(jnp.int32, shape, dim)


def _const_spec(shape):
    return pl.BlockSpec(shape, lambda *_: (0,) * len(shape))


_O_RY = D_RNN
_O_Q = 2 * D_RNN
_O_KV = _O_Q + N_HEADS * HEAD_DIM
_O_NG = _O_KV + 6 * KVW
_O_MG = _O_NG + NG_PAD
_W_COLS = _O_MG + 2 * D_MODEL


def _inproj_kernel(x_ref, g_ref, w_ref, rx_ref, gy_ref, zq_ref, zkv_ref, ng_ref, mg_ref, h_ref):
    x = x_ref[...]
    y = x * lax.rsqrt(jnp.mean(x * x, axis=-1, keepdims=True) + EPS) * g_ref[...]
    h_ref[...] = y.astype(BF16)

    def proj(c0, width):
        return _dot(h_ref[...], w_ref[:, c0:c0 + width])

    cw = 256
    tm = x_ref.shape[0]
    for c in range(D_RNN // cw):
        rx = proj(c * cw, cw)
        gy = jax.nn.gelu(proj(_O_RY + c * cw, cw))
        for hf in range(cw // 128):
            rx_ref[c * (cw // 128) + hf] = rx[:, hf * 128:(hf + 1) * 128].reshape(tm // 8, 8, 128)
            gy_ref[c * (cw // 128) + hf] = gy[:, hf * 128:(hf + 1) * 128].reshape(tm // 8, 8, 128)
    for g in range(N_KV):
        zq_ref[g] = proj(_O_Q + g * GW, GW)
    for s in range(6):
        z = proj(_O_KV + s * KVW, KVW)
        for pr in range(KVW // 128):
            zkv_ref[s, pr] = z[:, pr * 128:(pr + 1) * 128]
    ng_ref[...] = jax.nn.sigmoid(proj(_O_NG, NG_PAD))
    for c in range(2 * D_MODEL // cw):
        mg_ref[:, c * cw:(c + 1) * cw] = jax.nn.sigmoid(proj(_O_MG + c * cw, cw)).astype(mg_ref.dtype)


def _inproj(x, gain, w, *, tm=256):
    B, S, D = x.shape
    T = B * S
    ni = S // tm
    nck = D_RNN // 128
    tok = lambda b, i: (b * ni + i, 0)
    slab_spec = pl.BlockSpec((nck, tm // 8, None, 8, 128), lambda b, i: (0, i, b, 0, 0))
    slab_shape = jax.ShapeDtypeStruct((nck, S // 8, B, 8, 128), F32)
    return pl.pallas_call(
        _inproj_kernel,
        grid=(B, ni),
        in_specs=[
            pl.BlockSpec((None, tm, D), lambda b, i: (b, i, 0)),
            _const_spec((1, D)),
            pl.BlockSpec((D, _W_COLS), lambda b, i: (0, 0), pipeline_mode=pl.Buffered(1)),
        ],
        out_specs=[
            slab_spec,
            slab_spec,
            pl.BlockSpec((N_KV, tm, GW), lambda b, i: (0, b * ni + i, 0)),
            pl.BlockSpec((6, KVW // 128, tm, 128), lambda b, i: (0, 0, b * ni + i, 0)),
            pl.BlockSpec((tm, NG_PAD), tok),
            pl.BlockSpec((tm, 2 * D_MODEL), tok),
        ],
        out_shape=[
            slab_shape,
            slab_shape,
            jax.ShapeDtypeStruct((N_KV, T, GW), F32),
            jax.ShapeDtypeStruct((6, KVW // 128, T, 128), F32),
            jax.ShapeDtypeStruct((T, NG_PAD), F32),
            jax.ShapeDtypeStruct((T, 2 * D_MODEL), BF16),
        ],
        scratch_shapes=[pltpu.VMEM((tm, D), BF16)],
        compiler_params=_params(("parallel", "parallel")),
        name="inproj",
    )(x, gain, w)


def _rglru_kernel(x_ref, gy_ref, cw_ref, cb_ref, wa_ref, ba_ref, wi_ref, bi_ref, lam_ref, wo_ref,
                  o_ref, xb_ref, gyb_ref, a_ref, u_ref, hb_ref, hs_ref, yb_ref, *, nb, tt):
    R = nb * tt
    halo = (CONV_W - 1) * nb
    step = pl.program_id(0)
    nck = D_RNN // 128

    @pl.when(step == 0)
    def _():
        xb_ref[0:halo, :] = jnp.zeros((halo, D_RNN), F32)
        hs_ref[...] = jnp.zeros((nb, D_RNN), F32)

    def slab_rows(t8, t_lo):
        return pl.ds(t8 * (8 * nb) + t_lo, nb, stride=8)

    def stage_in(t8, carry):
        for t_lo in range(8):
            r0 = pl.multiple_of((t8 * 8 + t_lo) * nb, nb)
            for ck in range(nck):
                cols = slice(ck * 128, (ck + 1) * 128)
                xb_ref[pl.ds(halo + r0, nb), cols] = x_ref[ck, slab_rows(t8, t_lo), :]
                gyb_ref[pl.ds(r0, nb), cols] = gy_ref[ck, slab_rows(t8, t_lo), :]
        return carry

    lax.fori_loop(0, tt // 8, stage_in, 0)
    xr = cb_ref[...] + cw_ref[0:1, :] * xb_ref[0:R, :]
    for k in range(1, CONV_W):
        xr = xr + cw_ref[k:k + 1, :] * xb_ref[k * nb:k * nb + R, :]
    tail = xb_ref[R:R + halo, :]
    xb_ref[0:halo, :] = tail

    xrb = xr.astype(BF16)
    sp = jax.nn.softplus(-lam_ref[...])
    row = _iota((R, RG_BW), 0)
    first = (row < nb) & (step == 0)
    for n in range(RG_BLOCKS):
        sl = slice(n * RG_BW, (n + 1) * RG_BW)
        xn = xrb[:, sl]
        r = jax.nn.sigmoid(_dot(xn, wa_ref[n]) + ba_ref[:, sl])
        ig = jax.nn.sigmoid(_dot(xn, wi_ref[n]) + bi_ref[:, sl])
        log_a = (-RG_C) * r * sp[:, sl]
        a = jnp.exp(log_a)
        mult = jnp.sqrt(1.0 - a * a)
        mult = jnp.where(first, 1.0, mult)
        a_ref[:, sl] = a
        u_ref[:, sl] = mult * ig * xr[:, sl]

    def body(t, h):
        r0 = pl.multiple_of(t * nb, nb)
        h = a_ref[pl.ds(r0, nb), :] * h + u_ref[pl.ds(r0, nb), :]
        hb_ref[pl.ds(r0, nb), :] = h
        return h

    h = lax.fori_loop(0, tt, body, hs_ref[...], unroll=8)
    hs_ref[...] = h
    yb_ref[...] = _dot((hb_ref[...] * gyb_ref[...]).astype(BF16), wo_ref[...])

    def stage_out(t8, carry):
        for t_lo in range(8):
            r0 = pl.multiple_of((t8 * 8 + t_lo) * nb, nb)
            for ck in range(nck):
                o_ref[ck, slab_rows(t8, t_lo), :] = yb_ref[pl.ds(r0, nb), ck * 128:(ck + 1) * 128]
        return carry

    lax.fori_loop(0, tt // 8, stage_out, 0)


def _rglru(rx, gy, conv_w, conv_b, wa, ba, wi, bi, lam, wo, *, nb, tt=32):
    nck, rows, _ = rx.shape
    S = rows // nb
    assert S % tt == 0 and tt % 8 == 0
    R = nb * tt
    halo = (CONV_W - 1) * nb
    return pl.pallas_call(
        functools.partial(_rglru_kernel, nb=nb, tt=tt),
        grid=(S // tt,),
        in_specs=[
            pl.BlockSpec((nck, R, 128), lambda s: (0, s, 0)),
            pl.BlockSpec((nck, R, 128), lambda s: (0, s, 0)),
            _const_spec((CONV_W, D_RNN)),
            _const_spec((1, D_RNN)),
            _const_spec((RG_BLOCKS, RG_BW, RG_BW)),
            _const_spec((1, D_RNN)),
            _const_spec((RG_BLOCKS, RG_BW, RG_BW)),
            _const_spec((1, D_RNN)),
            _const_spec((1, D_RNN)),
            _const_spec((D_RNN, D_MODEL)),
        ],
        out_specs=pl.BlockSpec((D_MODEL // 128, R, 128), lambda s: (0, s, 0)),
        out_shape=jax.ShapeDtypeStruct((D_MODEL // 128, rows, 128), F32),
        scratch_shapes=[
            pltpu.VMEM((R + halo, D_RNN), F32),
            pltpu.VMEM((R, D_RNN), F32),
            pltpu.VMEM((R, D_RNN), F32),
            pltpu.VMEM((R, D_RNN), F32),
            pltpu.VMEM((R, D_RNN), F32),
            pltpu.VMEM((nb, D_RNN), F32),
            pltpu.VMEM((R, D_MODEL), F32),
        ],
        compiler_params=_params(("arbitrary",)),
        name="rglru",
    )(rx, gy, conv_w, conv_b, wa, ba, wi, bi, lam, wo)


def _kvprep_kernel(zk_ref, zv_ref, g_ref, kn_ref, vt_ref):
    rows = zk_ref.shape[1]
    x = jnp.concatenate([zk_ref[0], zk_ref[1]], axis=1)
    seg = jnp.where(_iota((KVW, KVW), 0) // HEAD_DIM == _iota((KVW, KVW), 1) // HEAD_DIM, 1.0, 0.0).astype(BF16)
    hi, lo = _split(x * x)
    ssq = _dot(jnp.concatenate([hi, lo], axis=0), seg)
    kn = (x * lax.rsqrt((ssq[0:rows] + ssq[rows:2 * rows]) * (1.0 / HEAD_DIM) + EPS) * g_ref[...]).astype(BF16)
    for pr in range(KVW // 128):
        kn_ref[pr] = kn[:, pr * 128:(pr + 1) * 128]
    eye = jnp.where(_iota((KVW, KVW), 0) == _iota((KVW, KVW), 1), 1.0, 0.0).astype(BF16)
    pad = jnp.where(_iota((VROWS - HEAD_DIM, KCH), 0) == 0, 1.0, 0.0).astype(BF16)
    for c in range(rows // KCH):
        sl = slice(c * KCH, (c + 1) * KCH)
        v = jnp.concatenate([zv_ref[0, sl, :], zv_ref[1, sl, :]], axis=1).astype(BF16)
        vt = _dot_nt(eye, v).astype(BF16)
        for g in range(N_KV):
            vt_ref[c, g, 0:HEAD_DIM, :] = vt[g * HEAD_DIM:(g + 1) * HEAD_DIM]
            vt_ref[c, g, HEAD_DIM:VROWS, :] = pad


def _kvprep(zkv, gains, *, B, S, cpb=4):
    nch = S // KCH
    assert nch % cpb == 0
    nblk = nch // cpb
    rows = cpb * KCH
    return pl.pallas_call(
        _kvprep_kernel,
        grid=(2, B, nblk),
        in_specs=[
            pl.BlockSpec((None, KVW // 128, rows, 128), lambda w, b, i: (2 + 2 * w, 0, b * nblk + i, 0)),
            pl.BlockSpec((None, KVW // 128, rows, 128), lambda w, b, i: (3 + 2 * w, 0, b * nblk + i, 0)),
            pl.BlockSpec((None, 1, KVW), lambda w, b, i: (w, 0, 0)),
        ],
        out_specs=[
            pl.BlockSpec((None, None, KVW // 128, rows, 128), lambda w, b, i: (w, b, 0, i, 0)),
            pl.BlockSpec((None, None, cpb, N_KV, VROWS, KCH), lambda w, b, i: (w, b, i, 0, 0, 0)),
        ],
        out_shape=[
            jax.ShapeDtypeStruct((2, B, KVW // 128, S, 128), BF16),
            jax.ShapeDtypeStruct((2, B, nch, N_KV, VROWS, KCH), BF16),
        ],
        compiler_params=_params(("parallel", "parallel", "parallel")),
        name="kvprep",
    )(zkv, zkv, gains)


def _compress_kernel(zk_ref, zv_ref, pos_ref, w1_ref, w2_ref, g_ref, kc_ref, vct_ref):
    nch = zk_ref.shape[0] // CMP_STRIDE
    lane = _iota((nch, 128), 1)
    npair = CMP_STRIDE // 2
    eye = jnp.where(_iota((HEAD_DIM, HEAD_DIM), 0) == _iota((HEAD_DIM, HEAD_DIM), 1), 1.0, 0.0).astype(BF16)
    for kind, z_ref in enumerate((zk_ref, zv_ref)):
        for ge in range(2):
            combs = []
            for p in range(npair):
                a = z_ref[pl.ds(2 * p, nch, stride=CMP_STRIDE), :]
                b = z_ref[pl.ds(2 * p + 1, nch, stride=CMP_STRIDE), :]
                if ge == 0:
                    combs.append(jnp.where(lane < HEAD_DIM, a, pltpu.roll(b, HEAD_DIM, axis=1)))
                else:
                    combs.append(jnp.where(lane < HEAD_DIM, pltpu.roll(a, HEAD_DIM, axis=1), b))
            halves = []
            half_w = npair * 128
            for half in range(2):
                lhs = jnp.concatenate(
                    [(combs[p] + pos_ref[kind, half * npair + p:half * npair + p + 1, :]).astype(BF16)
                     for p in range(npair)], axis=1)
                halves.append(_dot(lhs, w1_ref[kind, half * half_w:(half + 1) * half_w, :]))
            hid = halves[0] + pltpu.roll(halves[1], nch - 1, axis=0)
            hid = jax.nn.gelu(hid)
            out = _dot(hid.astype(BF16), w2_ref[kind])
            if kind == 0:
                out = out * lax.rsqrt(jnp.mean(out * out, axis=-1, keepdims=True) + EPS) * g_ref[...]
            res = jnp.where(_iota(out.shape, 0) < nch - 1, out, 0.0).astype(BF16)
            if kind == 0:
                kc_ref[ge] = res
            else:
                vct_ref[ge] = _dot_nt(eye, res).astype(BF16)


def _compress(zkv, pos, w1, w2, gain, *, B, S):
    nch = S // CMP_STRIDE
    return pl.pallas_call(
        _compress_kernel,
        grid=(B, KVW // 128),
        in_specs=[
            pl.BlockSpec((None, None, S, 128), lambda b, lp: (0, lp, b, 0)),
            pl.BlockSpec((None, None, S, 128), lambda b, lp: (1, lp, b, 0)),
            _const_spec((2, CMP_STRIDE, 128)),
            _const_spec((2, CMP_LEN * HEAD_DIM, CMP_HID)),
            _const_spec((2, CMP_HID, HEAD_DIM)),
            _const_spec((1, HEAD_DIM)),
        ],
        out_specs=[
            pl.BlockSpec((None, 2, nch, HEAD_DIM), lambda b, lp: (b, lp, 0, 0)),
            pl.BlockSpec((None, 2, HEAD_DIM, nch), lambda b, lp: (b, lp, 0, 0)),
        ],
        out_shape=[
            jax.ShapeDtypeStruct((B, N_KV, nch, HEAD_DIM), BF16),
            jax.ShapeDtypeStruct((B, N_KV, HEAD_DIM, nch), BF16),
        ],
        compiler_params=_params(("parallel", "parallel")),
        name="compress",
    )(zkv, zkv, pos, w1, w2, gain)


def _spread_heads(x):
    y = x + pltpu.roll(x, 2 * Q_BLK, axis=1)
    return y + pltpu.roll(y, Q_BLK, axis=1)


def _swap_heads(x):
    n = x.shape[0]
    lane = _iota((n, 128), 1)
    halves = []
    for pr in range(x.shape[1] // 128):
        slab = jnp.concatenate([x[:, pr * 128:(pr + 1) * 128], jnp.zeros((128 - n, 128), x.dtype)], axis=0)
        tr = slab.T
        halves.append(jnp.where(lane < n, tr[0:n], pltpu.roll(tr[n:2 * n], n, axis=1)))
    return jnp.concatenate(halves, axis=1)


def _nsa_kernel(*refs, nsel, qpb):
    for k in range(qpb):
        _nsa_block(k, *refs, nsel=nsel, qpb=qpb)


def _nsa_block(k, zq_ref, qg_ref, kc_ref, vct_ref, ks_ref, kw_ref, vst_ref, vwt_ref, ng_ref, o_ref,
               qtp_ref, oc_ref, bias_ref, ms_ref, accs_ref, mw_ref, accw_ref, sbuf_ref, wbuf_ref,
               gates_ref, scs_ref, wbias_ref, *, nsel, qpb):
    qi = pl.program_id(1) * qpb + k
    q_rows = pl.ds(k * Q_BLK, Q_BLK)
    s0 = qi * Q_BLK
    cur = qi
    ncmp = kc_ref.shape[1]
    per_chunk = KCH // SEL_LEN

    lane_t = s0 + (_iota((1, GW), 1) % Q_BLK)
    lane_grp = _iota((1, GW), 1) // Q_BLK
    ov_n = _iota((nsel, ncmp), 0) * SEL_LEN
    ov_c = _iota((nsel, ncmp), 1) * CMP_STRIDE
    overlap = jnp.where((ov_c < ov_n + SEL_LEN) & (ov_c + CMP_LEN > ov_n), 1.0, 0.0).astype(BF16)


    ok_c = _iota((ncmp, GW), 0) * CMP_STRIDE + (CMP_LEN - 1) <= lane_t
    groups = range(N_KV)
    ngt = jnp.concatenate([ng_ref[q_rows, :], jnp.zeros((128 - Q_BLK, NG_PAD), F32)], axis=0).T
    ngt_hi = pltpu.roll(ngt, Q_BLK, axis=1)
    low_half = _iota((1, 128), 1) < Q_BLK
    for g in groups:
        for c in range(3):
            r = c * N_HEADS + g * HPG
            pairs = [jnp.where(low_half, ngt[r + j:r + j + 1], ngt_hi[r + j + 1:r + j + 2]) for j in (0, 2)]
            gates_ref[g, c:c + 1, :] = jnp.concatenate(pairs, axis=1)

    qts = []
    for g in groups:
        xt = _swap_heads(zq_ref[g, q_rows, :])
        ssq = jnp.sum(xt * xt, axis=0, keepdims=True)
        qts.append((xt * lax.rsqrt(ssq * (1.0 / HEAD_DIM) + EPS) * qg_ref[...]).astype(BF16))
    scs = []
    for g in groups:
        off = (g % 2) * HEAD_DIM
        qtp_ref[g, off:off + HEAD_DIM, :] = qts[g]
        qtp_ref[g, HEAD_DIM - off:2 * HEAD_DIM - off, :] = jnp.zeros((HEAD_DIM, GW), BF16)
        scs.append(_dot(kc_ref[g], qts[g]))

    def scores(k_ref, i):
        k0 = pl.multiple_of(i * KCH, KCH)
        return [_dot(k_ref[g // 2, pl.ds(k0, KCH), :], qtp_ref[g]) for g in groups]

    def park(buf_ref, vals):
        for g in groups:
            buf_ref[g] = vals[g]

    n_chunks = qi // per_chunk + 1
    n_win = WINDOW // KCH + 1
    win_ids = [jnp.maximum(n_chunks - (n_win - j), 0) for j in range(n_win)]
    win_dead = [jnp.where(n_chunks >= n_win - j, 0.0, NEG) for j in range(n_win)]
    park(scs_ref, scs)
    park(sbuf_ref, scores(ks_ref, 0))
    park(wbuf_ref, scores(kw_ref, win_ids[0]))

    key_row = _iota((SEL_LEN, GW), 0)
    in_blk_t = _iota((SEL_LEN, GW), 1) % Q_BLK
    causal_tri = jnp.where(key_row <= in_blk_t, 0.0, NEG)
    expired_tri = jnp.where(key_row > in_blk_t, 0.0, NEG)
    win_blocks = WINDOW // SEL_LEN
    cur_slab = k % per_chunk
    cut_rows = slice(cur_slab * SEL_LEN, (cur_slab + 1) * SEL_LEN)
    last_slabs = per_chunk // 2 if cur_slab < per_chunk // 2 else per_chunk
    n_all = _iota((nsel, GW), 0)
    wbias_ref[...] = jnp.where((n_all >= cur - win_blocks) & (n_all <= cur), 0.0, NEG)

    def cut(buf_ref, tri):
        for g in groups:
            buf_ref[g, cut_rows, :] = buf_ref[g, cut_rows, :] + tri

    cut(wbuf_ref, jnp.where(cur >= win_blocks, expired_tri, 0.0))

    ms_ref[...] = jnp.full(ms_ref.shape, NEG, F32)
    mw_ref[...] = jnp.full(mw_ref.shape, NEG, F32)
    accs_ref[...] = jnp.zeros(accs_ref.shape, F32)
    accw_ref[...] = jnp.zeros(accw_ref.shape, F32)

    def update(g, blocks, shifts, vt, m_ref, acc_ref):
        m = m_ref[g]
        mnew = m
        for blk, sh in zip(blocks, shifts):
            mnew = jnp.maximum(mnew, jnp.max(blk, axis=0, keepdims=True) + sh)
        e = jnp.concatenate([jnp.exp2(blk + (sh - mnew)) for blk, sh in zip(blocks, shifts)], axis=0).astype(BF16)
        acc_ref[g] = jnp.exp2(m - mnew) * acc_ref[g] + _dot(vt, e)
        m_ref[g] = mnew

    def slabs(buf_ref, g, ns):
        return [buf_ref[g, nb * SEL_LEN:(nb + 1) * SEL_LEN, :] for nb in range(ns)]

    def sel_softmax(i, ns=per_chunk):
        for g in groups:
            shifts = [bias_ref[g, pl.ds(i * per_chunk + nb, 1), :] for nb in range(ns)]
            update(g, slabs(sbuf_ref, g, ns), shifts, vst_ref[i, g, :, 0:ns * SEL_LEN], ms_ref, accs_ref)

    def win_softmax(i, dead, ns=per_chunk):
        shifts = [wbias_ref[pl.ds(i * per_chunk + nb, 1), :] + dead for nb in range(ns)]
        for g in groups:
            update(g, slabs(wbuf_ref, g, ns), shifts, vwt_ref[i, g, :, 0:ns * SEL_LEN], mw_ref, accw_ref)

    def sel_step(i, carry):
        nxt = scores(ks_ref, i + 1)
        sel_softmax(i)
        park(sbuf_ref, nxt)
        return carry

    for j in range(n_win - 1):
        nxt = scores(kw_ref, win_ids[j + 1])
        win_softmax(win_ids[j], win_dead[j])
        park(wbuf_ref, nxt)

    imp_all = jnp.zeros((nsel, GW), F32)
    some_c = lane_t >= CMP_LEN - 1
    for g in groups:
        scm = jnp.where(ok_c, scs_ref[g], NEG)
        ec = jnp.exp2(scm - jnp.max(scm, axis=0, keepdims=True))
        inv = jnp.where(some_c, 1.0 / jnp.maximum(jnp.sum(ec, axis=0, keepdims=True), 1e-30), 0.0)
        both = _dot(jnp.concatenate([vct_ref[g], overlap], axis=0), ec.astype(BF16)) * inv
        oc_ref[g] = both[0:HEAD_DIM]
        imp = _spread_heads(both[HEAD_DIM:HEAD_DIM + nsel])
        imp_all = jnp.where(lane_grp == g, imp, imp_all)

    n_io = _iota((nsel, GW), 0)
    forced = (n_io == 0) | (n_io == cur) | (n_io == cur - 1)
    valid = n_io <= cur
    score = jnp.where(forced, FORCE_SCORE, imp_all)
    score = jnp.where(valid, score, -1.0)
    rows8 = [score[r:r + 8] for r in range(0, nsel, 8)]
    ranks = [jnp.zeros((8, GW), F32) for _ in rows8]
    sub = _iota((8, GW), 0)
    for m in range(nsel):
        rowv = score[m:m + 1, :]
        for k, blk in enumerate(rows8):
            ge = jnp.where(rowv >= blk, 1.0, 0.0)
            gt = jnp.where(rowv > blk, 1.0, 0.0)
            if 8 * k > m:
                inc = ge
            elif 8 * k + 7 < m:
                inc = gt
            else:
                inc = jnp.where(sub + 8 * k > m, ge, gt)
            ranks[k] = ranks[k] + inc
    rank = jnp.concatenate(ranks, axis=0)
    sel_all = jnp.where((rank < float(min(SEL_TOPN, nsel))) & valid, 1.0, 0.0)
    for g in groups:
        mine = jnp.where(lane_grp == g, sel_all, 0.0)
        bias_ref[g] = (_spread_heads(mine) - 1.0) * (-NEG)

    cut(wbuf_ref, causal_tri)
    win_softmax(win_ids[n_win - 1], win_dead[n_win - 1], last_slabs)

    lax.fori_loop(0, n_chunks - 1, sel_step, 0)

    cut(sbuf_ref, causal_tri)
    sel_softmax(n_chunks - 1, last_slabs)
    outs = []
    for g in groups:
        o_s = accs_ref[g, 0:HEAD_DIM, :] * (1.0 / jnp.maximum(accs_ref[g, HEAD_DIM:HEAD_DIM + 1, :], 1e-30))
        o_w = accw_ref[g, 0:HEAD_DIM, :] * (1.0 / jnp.maximum(accw_ref[g, HEAD_DIM:HEAD_DIM + 1, :], 1e-30))
        gates = gates_ref[g]
        o_t = gates[0:1] * oc_ref[g] + gates[1:2] * o_s + gates[2:3] * o_w
        outs.append(_swap_heads(o_t))
    for g in groups:
        o_ref[g, q_rows, :] = outs[g].astype(o_ref.dtype)


def _nsa(zq, qgain, kc, vct, kn, vt, ng, *, B, S, qpb=8):
    T = B * S
    assert S % (Q_BLK * qpb) == 0 and qpb % (KCH // SEL_LEN) == 0
    nq = S // (Q_BLK * qpb)
    nch = S // KCH
    ncmp = S // CMP_STRIDE
    nsel = S // SEL_LEN
    tok = lambda b, i: (0, b * nq + i, 0)
    return pl.pallas_call(
        functools.partial(_nsa_kernel, nsel=nsel, qpb=qpb),
        grid=(B, nq),
        in_specs=[
            pl.BlockSpec((N_KV, Q_BLK * qpb, GW), tok),
            _const_spec((HEAD_DIM, GW)),
            pl.BlockSpec((None, N_KV, ncmp, HEAD_DIM), lambda b, i: (b, 0, 0, 0)),
            pl.BlockSpec((None, N_KV, HEAD_DIM, ncmp), lambda b, i: (b, 0, 0, 0)),
            pl.BlockSpec((None, None, 2, S, 128), lambda b, i: (0, b, 0, 0, 0)),
            pl.BlockSpec((None, None, 2, S, 128), lambda b, i: (1, b, 0, 0, 0)),
            pl.BlockSpec((None, None, nch, N_KV, VROWS, KCH), lambda b, i: (0, b, 0, 0, 0, 0)),
            pl.BlockSpec((None, None, nch, N_KV, VROWS, KCH), lambda b, i: (1, b, 0, 0, 0, 0)),
            pl.BlockSpec((Q_BLK * qpb, NG_PAD), lambda b, i: (b * nq + i, 0)),
        ],
        out_specs=pl.BlockSpec((N_KV, Q_BLK * qpb, GW), tok),
        out_shape=jax.ShapeDtypeStruct((N_KV, T, GW), BF16),
        scratch_shapes=[
            pltpu.VMEM((N_KV, 2 * HEAD_DIM, GW), BF16),
            pltpu.VMEM((N_KV, HEAD_DIM, GW), F32),
            pltpu.VMEM((N_KV, nsel, GW), F32),
            pltpu.VMEM((N_KV, 1, GW), F32),
            pltpu.VMEM((N_KV, VROWS, GW), F32),
            pltpu.VMEM((N_KV, 1, GW), F32),
            pltpu.VMEM((N_KV, VROWS, GW), F32),
            pltpu.VMEM((N_KV, KCH, GW), F32),
            pltpu.VMEM((N_KV, KCH, GW), F32),
            pltpu.VMEM((N_KV, 8, GW), F32),
            pltpu.VMEM((N_KV, ncmp, GW), F32),
            pltpu.VMEM((nsel, GW), F32),
        ],
        compiler_params=_params(("parallel", "arbitrary")),
        name="nsa",
    )(zq, qgain, kc, vct, kn, kn, vt, vt, ng)


def _merge_kernel(x_ref, ya_ref, o_ref_in, mg_ref, wn_ref, wo_ref, out_ref):
    yb = _dot(o_ref_in[0], wn_ref[0:GW, :])
    for g in range(1, N_KV):
        yb = yb + _dot(o_ref_in[g], wn_ref[g * GW:(g + 1) * GW, :])
    tm = x_ref.shape[0]
    ya = jnp.concatenate([ya_ref[ck].reshape(tm, 128) for ck in range(D_MODEL // 128)], axis=1)
    mixed = (mg_ref[:, 0:D_MODEL].astype(F32) * ya
             + mg_ref[:, D_MODEL:2 * D_MODEL].astype(F32) * yb)
    out_ref[...] = x_ref[...] + _dot(mixed.astype(BF16), wo_ref[...])


def _merge(x2d, ya, o, mg, wn, wo, *, B, S, tm=512):
    ni = S // tm
    tok = lambda b, i: (b * ni + i, 0)
    return pl.pallas_call(
        _merge_kernel,
        grid=(B, ni),
        in_specs=[
            pl.BlockSpec((tm, D_MODEL), tok),
            pl.BlockSpec((D_MODEL // 128, tm // 8, None, 8, 128), lambda b, i: (0, i, b, 0, 0)),
            pl.BlockSpec((N_KV, tm, GW), lambda b, i: (0, b * ni + i, 0)),
            pl.BlockSpec((tm, 2 * D_MODEL), tok),
            _const_spec((N_HEADS * HEAD_DIM, D_MODEL)),
            _const_spec((D_MODEL, D_MODEL)),
        ],
        out_specs=pl.BlockSpec((tm, D_MODEL), tok),
        out_shape=jax.ShapeDtypeStruct((B * S, D_MODEL), F32),
        compiler_params=_params(("parallel", "parallel")),
        name="merge",
    )(x2d, ya, o, mg, wn, wo)


def _ffn_kernel(x_ref, g_ref, wg_ref, wu_ref, wd_ref, o_ref, h_ref, *, tf):
    x = x_ref[...]
    y = x * lax.rsqrt(jnp.mean(x * x, axis=-1, keepdims=True) + EPS) * g_ref[...]
    h_ref[...] = y.astype(BF16)
    o_ref[...] = x
    for j in range(D_FF // tf):
        h = h_ref[...]
        sl = slice(j * tf, (j + 1) * tf)
        act = jax.nn.silu(_dot(h, wg_ref[:, sl])) * _dot(h, wu_ref[:, sl])
        o_ref[...] += _dot(act.astype(BF16), wd_ref[sl, :])


def _ffn(x2d, gain, wg, wu, wd, *, tm=512, tf=256):
    T = x2d.shape[0]
    assert T % tm == 0 and D_FF % tf == 0
    once = dict(pipeline_mode=pl.Buffered(1))
    return pl.pallas_call(
        functools.partial(_ffn_kernel, tf=tf),
        grid=(T // tm,),
        in_specs=[
            pl.BlockSpec((tm, D_MODEL), lambda i: (i, 0)),
            _const_spec((1, D_MODEL)),
            pl.BlockSpec((D_MODEL, D_FF), lambda i: (0, 0), **once),
            pl.BlockSpec((D_MODEL, D_FF), lambda i: (0, 0), **once),
            pl.BlockSpec((D_FF, D_MODEL), lambda i: (0, 0), **once),
        ],
        out_specs=pl.BlockSpec((tm, D_MODEL), lambda i: (i, 0)),
        out_shape=jax.ShapeDtypeStruct((T, D_MODEL), F32),
        scratch_shapes=[pltpu.VMEM((tm, D_MODEL), BF16)],
        compiler_params=_params(("parallel",)),
        name="ffn",
    )(x2d, gain, wg, wu, wd)


def _layer(x, p):
    B, S, _ = x.shape
    T = B * S
    dh = HEAD_DIM
    row = lambda v: v.reshape(1, -1)
    w_in = p["w_in"]
    n_ng = 3 * N_HEADS
    o_ng = 2 * D_RNN + N_HEADS * dh + 6 * KVW
    w_packed = jnp.concatenate(
        [w_in[:, :o_ng], jnp.pad(w_in[:, o_ng:o_ng + n_ng], ((0, 0), (0, NG_PAD - n_ng))), w_in[:, o_ng + n_ng:]],
        axis=1).astype(BF16)
    rx, gy, zq, zkv, ng, mg = _inproj(x, row(p["norm1"]), w_packed)

    slabs = lambda a: a.reshape(a.shape[0], S * B, 128)
    ya = _rglru(slabs(rx), slabs(gy), p["conv_w"], row(p["conv_b"]),
                p["rg_wa"].astype(BF16), row(p["rg_ba"]), p["rg_wi"].astype(BF16), row(p["rg_bi"]),
                row(p["rg_lambda"]), p["w_rg_out"].astype(BF16), nb=B)
    ya = ya.reshape(D_MODEL // 128, S // 8, B, 8, 128)

    kgains = jnp.tile(p["k_norm"][1:3], (1, N_KV)).reshape(2, 1, KVW)
    kn, vt = _kvprep(zkv, kgains, B=B, S=S)
    pos = jnp.stack([p["cmp_pos_k"], p["cmp_pos_v"]]).reshape(2, CMP_STRIDE, 2 * dh)
    w1 = jnp.stack([p["cmp_k_w1"], p["cmp_v_w1"]]).astype(BF16)
    w2 = jnp.stack([p["cmp_k_w2"], p["cmp_v_w2"]]).astype(BF16)
    kc, vct = _compress(zkv, pos, w1, w2, row(p["k_norm"][0]), B=B, S=S)
    qgain = jnp.broadcast_to((p["q_norm"] * (dh ** -0.5 * LOG2E))[:, None], (dh, GW))
    o = _nsa(zq, qgain, kc, vct, kn, vt, ng, B=B, S=S)

    x1 = _merge(x.reshape(T, D_MODEL), ya, o, mg, p["w_nsa_out"].astype(BF16), p["w_o"].astype(BF16), B=B, S=S)
    x2 = _ffn(x1, row(p["norm2"]), p["w_gate"].astype(BF16), p["w_up"].astype(BF16), p["w_down"].astype(BF16))
    return x2.reshape(B, S, D_MODEL)


def kernel(x, norm1, w_in, conv_w, conv_b, rg_wa, rg_ba, rg_wi, rg_bi, rg_lambda, q_norm, k_norm, cmp_pos_k,
           cmp_pos_v, cmp_k_w1, cmp_k_w2, cmp_v_w1, cmp_v_w2, w_rg_out, w_nsa_out, w_o, norm2, w_gate, w_up, w_down):
    params = dict(norm1=norm1, w_in=w_in, conv_w=conv_w, conv_b=conv_b, rg_wa=rg_wa, rg_ba=rg_ba, rg_wi=rg_wi,
                  rg_bi=rg_bi, rg_lambda=rg_lambda, q_norm=q_norm, k_norm=k_norm, cmp_pos_k=cmp_pos_k,
                  cmp_pos_v=cmp_pos_v, cmp_k_w1=cmp_k_w1, cmp_k_w2=cmp_k_w2, cmp_v_w1=cmp_v_w1, cmp_v_w2=cmp_v_w2,
                  w_rg_out=w_rg_out, w_nsa_out=w_nsa_out, w_o=w_o, norm2=norm2, w_gate=w_gate, w_up=w_up,
                  w_down=w_down)
    for l in range(norm1.shape[0]):
        x = _layer(x, {k: v[l] for k, v in params.items()})
    return x
```

```python
import functools

import jax
import jax.numpy as jnp
from jax import lax
from jax.experimental import pallas as pl
from jax.experimental.pallas import tpu as pltpu

D_MODEL = 1024
D_RNN = 1024
RG_BLOCKS = 4
RG_BW = D_RNN // RG_BLOCKS
CONV_W = 4
RG_C = 8.0
N_HEADS = 16
N_KV = 4
HEAD_DIM = 64
HPG = N_HEADS // N_KV
CMP_LEN = 32
CMP_STRIDE = 16
CMP_HID = 256
SEL_LEN = 64
SEL_TOPN = 16
WINDOW = 512
Q_BLK = 64
D_FF = 2816
EPS = 1e-6
FORCE_SCORE = 1e6
NEG = -1e30
LOG2E = 1.4426950408889634

KCH = 256
GW = HPG * HEAD_DIM
KVW = N_KV * HEAD_DIM
VROWS = 80
NG_PAD = 128

F32 = jnp.float32
BF16 = jnp.bfloat16

VMEM_LIMIT = 56 * 1024 * 1024


def _params(sem):
    return pltpu.CompilerParams(dimension_semantics=sem, vmem_limit_bytes=VMEM_LIMIT)


def _dot(a, b):
    return jnp.dot(a, b, preferred_element_type=F32)


def _dot_nt(a, b):
    return lax.dot_general(a, b, (((1,), (1,)), ((), ())), preferred_element_type=F32)


def _split(x):
    hi = x.astype(BF16)
    lo = (x - hi.astype(F32)).astype(BF16)
    return hi, lo


def _iota(shape, dim):
    return lax.broadcasted_iota(jnp.int32, shape, dim)


def _const_spec(shape):
    return pl.BlockSpec(shape, lambda *_: (0,) * len(shape))


_O_RY = D_RNN
_O_Q = 2 * D_RNN
_O_KV = _O_Q + N_HEADS * HEAD_DIM
_O_NG = _O_KV + 6 * KVW
_O_MG = _O_NG + NG_PAD
_W_COLS = _O_MG + 2 * D_MODEL


def _inproj_kernel(x_ref, g_ref, w_ref, rx_ref, gy_ref, zq_ref, zkv_ref, ng_ref, mg_ref, h_ref):
    x = x_ref[...]
    y = x * lax.rsqrt(jnp.mean(x * x, axis=-1, keepdims=True) + EPS) * g_ref[...]
    h_ref[...] = y.astype(BF16)

    def proj(c0, width):
        return _dot(h_ref[...], w_ref[:, c0:c0 + width])

    cw = 256
    tm = x_ref.shape[0]
    for c in range(D_RNN // cw):
        rx = proj(c * cw, cw)
        gy = jax.nn.gelu(proj(_O_RY + c * cw, cw))
        for hf in range(cw // 128):
            rx_ref[c * (cw // 128) + hf] = rx[:, hf * 128:(hf + 1) * 128].reshape(tm // 8, 8, 128)
            gy_ref[c * (cw // 128) + hf] = gy[:, hf * 128:(hf + 1) * 128].reshape(tm // 8, 8, 128)
    for g in range(N_KV):
        zq_ref[g] = proj(_O_Q + g * GW, GW)
    for s in range(6):
        z = proj(_O_KV + s * KVW, KVW)
        for pr in range(KVW // 128):
            zkv_ref[s, pr] = z[:, pr * 128:(pr + 1) * 128]
    ng_ref[...] = jax.nn.sigmoid(proj(_O_NG, NG_PAD))
    for c in range(2 * D_MODEL // cw):
        mg_ref[:, c * cw:(c + 1) * cw] = jax.nn.sigmoid(proj(_O_MG + c * cw, cw)).astype(mg_ref.dtype)


def _inproj(x, gain, w, *, tm=256):
    B, S, D = x.shape
    T = B * S
    ni = S // tm
    nck = D_RNN // 128
    tok = lambda b, i: (b * ni + i, 0)
    slab_spec = pl.BlockSpec((nck, tm // 8, None, 8, 128), lambda b, i: (0, i, b, 0, 0))
    slab_shape = jax.ShapeDtypeStruct((nck, S // 8, B, 8, 128), F32)
    return pl.pallas_call(
        _inproj_kernel,
        grid=(B, ni),
        in_specs=[
            pl.BlockSpec((None, tm, D), lambda b, i: (b, i, 0)),
            _const_spec((1, D)),
            pl.BlockSpec((D, _W_COLS), lambda b, i: (0, 0), pipeline_mode=pl.Buffered(1)),
        ],
        out_specs=[
            slab_spec,
            slab_spec,
            pl.BlockSpec((N_KV, tm, GW), lambda b, i: (0, b * ni + i, 0)),
            pl.BlockSpec((6, KVW // 128, tm, 128), lambda b, i: (0, 0, b * ni + i, 0)),
            pl.BlockSpec((tm, NG_PAD), tok),
            pl.BlockSpec((tm, 2 * D_MODEL), tok),
        ],
        out_shape=[
            slab_shape,
            slab_shape,
            jax.ShapeDtypeStruct((N_KV, T, GW), F32),
            jax.ShapeDtypeStruct((6, KVW // 128, T, 128), F32),
            jax.ShapeDtypeStruct((T, NG_PAD), F32),
            jax.ShapeDtypeStruct((T, 2 * D_MODEL), BF16),
        ],
        scratch_shapes=[pltpu.VMEM((tm, D), BF16)],
        compiler_params=_params(("parallel", "parallel")),
        name="inproj",
    )(x, gain, w)


def _rglru_kernel(x_ref, gy_ref, cw_ref, cb_ref, wa_ref, ba_ref, wi_ref, bi_ref, lam_ref, wo_ref,
                  o_ref, xb_ref, gyb_ref, a_ref, u_ref, hb_ref, hs_ref, yb_ref, *, nb, tt):
    R = nb * tt
    halo = (CONV_W - 1) * nb
    step = pl.program_id(0)
    nck = D_RNN // 128

    @pl.when(step == 0)
    def _():
        xb_ref[0:halo, :] = jnp.zeros((halo, D_RNN), F32)
        hs_ref[...] = jnp.zeros((nb, D_RNN), F32)

    def slab_rows(t8, t_lo):
        return pl.ds(t8 * (8 * nb) + t_lo, nb, stride=8)

    def stage_in(t8, carry):
        for t_lo in range(8):
            r0 = pl.multiple_of((t8 * 8 + t_lo) * nb, nb)
            for ck in range(nck):
                cols = slice(ck * 128, (ck + 1) * 128)
                xb_ref[pl.ds(halo + r0, nb), cols] = x_ref[ck, slab_rows(t8, t_lo), :]
                gyb_ref[pl.ds(r0, nb), cols] = gy_ref[ck, slab_rows(t8, t_lo), :]
        return carry

    lax.fori_loop(0, tt // 8, stage_in, 0)
    xr = cb_ref[...] + cw_ref[0:1, :] * xb_ref[0:R, :]
    for k in range(1, CONV_W):
        xr = xr + cw_ref[k:k + 1, :] * xb_ref[k * nb:k * nb + R, :]
    tail = xb_ref[R:R + halo, :]
    xb_ref[0:halo, :] = tail

    xrb = xr.astype(BF16)
    sp = jax.nn.softplus(-lam_ref[...])
    row = _iota((R, RG_BW), 0)
    first = (row < nb) & (step == 0)
    for n in range(RG_BLOCKS):
        sl = slice(n * RG_BW, (n + 1) * RG_BW)
        xn = xrb[:, sl]
        r = jax.nn.sigmoid(_dot(xn, wa_ref[n]) + ba_ref[:, sl])
        ig = jax.nn.sigmoid(_dot(xn, wi_ref[n]) + bi_ref[:, sl])
        log_a = (-RG_C) * r * sp[:, sl]
        a = jnp.exp(log_a)
        mult = jnp.sqrt(1.0 - a * a)
        mult = jnp.where(first, 1.0, mult)
        a_ref[:, sl] = a
        u_ref[:, sl] = mult * ig * xr[:, sl]

    def body(t, h):
        r0 = pl.multiple_of(t * nb, nb)
        h = a_ref[pl.ds(r0, nb), :] * h + u_ref[pl.ds(r0, nb), :]
        hb_ref[pl.ds(r0, nb), :] = h
        return h

    h = lax.fori_loop(0, tt, body, hs_ref[...], unroll=8)
    hs_ref[...] = h
    yb_ref[...] = _dot((hb_ref[...] * gyb_ref[...]).astype(BF16), wo_ref[...])

    def stage_out(t8, carry):
        for t_lo in range(8):
            r0 = pl.multiple_of((t8 * 8 + t_lo) * nb, nb)
            for ck in range(nck):
                o_ref[ck, slab_rows(t8, t_lo), :] = yb_ref[pl.ds(r0, nb), ck * 128:(ck + 1) * 128]
        return carry

    lax.fori_loop(0, tt // 8, stage_out, 0)


def _rglru(rx, gy, conv_w, conv_b, wa, ba, wi, bi, lam, wo, *, nb, tt=32):
    nck, rows, _ = rx.shape
    S = rows // nb
    assert S % tt == 0 and tt % 8 == 0
    R = nb * tt
    halo = (CONV_W - 1) * nb
    return pl.pallas_call(
        functools.partial(_rglru_kernel, nb=nb, tt=tt),
        grid=(S // tt,),
        in_specs=[
            pl.BlockSpec((nck, R, 128), lambda s: (0, s, 0)),
            pl.BlockSpec((nck, R, 128), lambda s: (0, s, 0)),
            _const_spec((CONV_W, D_RNN)),
            _const_spec((1, D_RNN)),
            _const_spec((RG_BLOCKS, RG_BW, RG_BW)),
            _const_spec((1, D_RNN)),
            _const_spec((RG_BLOCKS, RG_BW, RG_BW)),
            _const_spec((1, D_RNN)),
            _const_spec((1, D_RNN)),
            _const_spec((D_RNN, D_MODEL)),
        ],
        out_specs=pl.BlockSpec((D_MODEL // 128, R, 128), lambda s: (0, s, 0)),
        out_shape=jax.ShapeDtypeStruct((D_MODEL // 128, rows, 128), F32),
        scratch_shapes=[
            pltpu.VMEM((R + halo, D_RNN), F32),
            pltpu.VMEM((R, D_RNN), F32),
            pltpu.VMEM((R, D_RNN), F32),
            pltpu.VMEM((R, D_RNN), F32),
            pltpu.VMEM((R, D_RNN), F32),
            pltpu.VMEM((nb, D_RNN), F32),
            pltpu.VMEM((R, D_MODEL), F32),
        ],
        compiler_params=_params(("arbitrary",)),
        name="rglru",
    )(rx, gy, conv_w, conv_b, wa, ba, wi, bi, lam, wo)


def _kvprep_kernel(zk_ref, zv_ref, g_ref, kn_ref, vt_ref):
    rows = zk_ref.shape[1]
    x = jnp.concatenate([zk_ref[0], zk_ref[1]], axis=1)
    seg = jnp.where(_iota((KVW, KVW), 0) // HEAD_DIM == _iota((KVW, KVW), 1) // HEAD_DIM, 1.0, 0.0).astype(BF16)
    hi, lo = _split(x * x)
    ssq = _dot(jnp.concatenate([hi, lo], axis=0), seg)
    kn = (x * lax.rsqrt((ssq[0:rows] + ssq[rows:2 * rows]) * (1.0 / HEAD_DIM) + EPS) * g_ref[...]).astype(BF16)
    for pr in range(KVW // 128):
        kn_ref[pr] = kn[:, pr * 128:(pr + 1) * 128]
    eye = jnp.where(_iota((KVW, KVW), 0) == _iota((KVW, KVW), 1), 1.0, 0.0).astype(BF16)
    pad = jnp.where(_iota((VROWS - HEAD_DIM, KCH), 0) == 0, 1.0, 0.0).astype(BF16)
    for c in range(rows // KCH):
        sl = slice(c * KCH, (c + 1) * KCH)
        v = jnp.concatenate([zv_ref[0, sl, :], zv_ref[1, sl, :]], axis=1).astype(BF16)
        vt = _dot_nt(eye, v).astype(BF16)
        for g in range(N_KV):
            vt_ref[c, g, 0:HEAD_DIM, :] = vt[g * HEAD_DIM:(g + 1) * HEAD_DIM]
            vt_ref[c, g, HEAD_DIM:VROWS, :] = pad


def _kvprep(zkv, gains, *, B, S, cpb=4):
    nch = S // KCH
    assert nch % cpb == 0
    nblk = nch // cpb
    rows = cpb * KCH
    return pl.pallas_call(
        _kvprep_kernel,
        grid=(2, B, nblk),
        in_specs=[
            pl.BlockSpec((None, KVW // 128, rows, 128), lambda w, b, i: (2 + 2 * w, 0, b * nblk + i, 0)),
            pl.BlockSpec((None, KVW // 128, rows, 128), lambda w, b, i: (3 + 2 * w, 0, b * nblk + i, 0)),
            pl.BlockSpec((None, 1, KVW), lambda w, b, i: (w, 0, 0)),
        ],
        out_specs=[
            pl.BlockSpec((None, None, KVW // 128, rows, 128), lambda w, b, i: (w, b, 0, i, 0)),
            pl.BlockSpec((None, None, cpb, N_KV, VROWS, KCH), lambda w, b, i: (w, b, i, 0, 0, 0)),
        ],
        out_shape=[
            jax.ShapeDtypeStruct((2, B, KVW // 128, S, 128), BF16),
            jax.ShapeDtypeStruct((2, B, nch, N_KV, VROWS, KCH), BF16),
        ],
        compiler_params=_params(("parallel", "parallel", "parallel")),
        name="kvprep",
    )(zkv, zkv, gains)


def _compress_kernel(zk_ref, zv_ref, pos_ref, w1_ref, w2_ref, g_ref, kc_ref, vct_ref):
    nch = zk_ref.shape[0] // CMP_STRIDE
    lane = _iota((nch, 128), 1)
    npair = CMP_STRIDE // 2
    eye = jnp.where(_iota((HEAD_DIM, HEAD_DIM), 0) == _iota((HEAD_DIM, HEAD_DIM), 1), 1.0, 0.0).astype(BF16)
    for kind, z_ref in enumerate((zk_ref, zv_ref)):
        for ge in range(2):
            combs = []
            for p in range(npair):
                a = z_ref[pl.ds(2 * p, nch, stride=CMP_STRIDE), :]
                b = z_ref[pl.ds(2 * p + 1, nch, stride=CMP_STRIDE), :]
                if ge == 0:
                    combs.append(jnp.where(lane < HEAD_DIM, a, pltpu.roll(b, HEAD_DIM, axis=1)))
                else:
                    combs.append(jnp.where(lane < HEAD_DIM, pltpu.roll(a, HEAD_DIM, axis=1), b))
            halves = []
            half_w = npair * 128
            for half in range(2):
                lhs = jnp.concatenate(
                    [(combs[p] + pos_ref[kind, half * npair + p:half * npair + p + 1, :]).astype(BF16)
                     for p in range(npair)], axis=1)
                halves.append(_dot(lhs, w1_ref[kind, half * half_w:(half + 1) * half_w, :]))
            hid = halves[0] + pltpu.roll(halves[1], nch - 1, axis=0)
            hid = jax.nn.gelu(hid)
            out = _dot(hid.astype(BF16), w2_ref[kind])
            if kind == 0:
                out = out * lax.rsqrt(jnp.mean(out * out, axis=-1, keepdims=True) + EPS) * g_ref[...]
            res = jnp.where(_iota(out.shape, 0) < nch - 1, out, 0.0).astype(BF16)
            if kind == 0:
                kc_ref[ge] = res
            else:
                vct_ref[ge] = _dot_nt(eye, res).astype(BF16)


def _compress(zkv, pos, w1, w2, gain, *, B, S):
    nch = S // CMP_STRIDE
    return pl.pallas_call(
        _compress_kernel,
        grid=(B, KVW // 128),
        in_specs=[
            pl.BlockSpec((None, None, S, 128), lambda b, lp: (0, lp, b, 0)),
            pl.BlockSpec((None, None, S, 128), lambda b, lp: (1, lp, b, 0)),
            _const_spec((2, CMP_STRIDE, 128)),
            _const_spec((2, CMP_LEN * HEAD_DIM, CMP_HID)),
            _const_spec((2, CMP_HID, HEAD_DIM)),
            _const_spec((1, HEAD_DIM)),
        ],
        out_specs=[
            pl.BlockSpec((None, 2, nch, HEAD_DIM), lambda b, lp: (b, lp, 0, 0)),
            pl.BlockSpec((None, 2, HEAD_DIM, nch), lambda b, lp: (b, lp, 0, 0)),
        ],
        out_shape=[
            jax.ShapeDtypeStruct((B, N_KV, nch, HEAD_DIM), BF16),
            jax.ShapeDtypeStruct((B, N_KV, HEAD_DIM, nch), BF16),
        ],
        compiler_params=_params(("parallel", "parallel")),
        name="compress",
    )(zkv, zkv, pos, w1, w2, gain)


def _spread_heads(x):
    y = x + pltpu.roll(x, 2 * Q_BLK, axis=1)
    return y + pltpu.roll(y, Q_BLK, axis=1)


def _swap_heads(x):
    n = x.shape[0]
    lane = _iota((n, 128), 1)
    halves = []
    for pr in range(x.shape[1] // 128):
        slab = jnp.concatenate([x[:, pr * 128:(pr + 1) * 128], jnp.zeros((128 - n, 128), x.dtype)], axis=0)
        tr = slab.T
        halves.append(jnp.where(lane < n, tr[0:n], pltpu.roll(tr[n:2 * n], n, axis=1)))
    return jnp.concatenate(halves, axis=1)


def _nsa_kernel(*refs, nsel, qpb):
    for k in range(qpb):
        _nsa_block(k, "head", None, *refs, nsel=nsel, qpb=qpb)

    def step(i, carry):
        for k in range(qpb):
            _nsa_block(k, "step", i, *refs, nsel=nsel, qpb=qpb)
        return carry

    lax.fori_loop(0, pl.program_id(1), step, 0)
    for k in range(qpb):
        _nsa_block(k, "tail", None, *refs, nsel=nsel, qpb=qpb)


def _nsa_block(k, phase, step_i, zq_ref, qg_ref, kc_ref, vct_ref, ks_ref, kw_ref, vst_ref, vwt_ref, ng_ref, o_ref,
               qtp_ref, oc_ref, ow_ref, bias_ref, ms_ref, accs_ref, mw_ref, accw_ref, sbuf_ref, wbuf_ref,
               gates_ref, scs_ref, wbias_ref, *, nsel, qpb):
    qtp_ref, oc_ref, ow_ref, bias_ref, ms_ref, accs_ref, sbuf_ref, gates_ref = (
        r.at[k] for r in (qtp_ref, oc_ref, ow_ref, bias_ref, ms_ref, accs_ref, sbuf_ref, gates_ref))
    qi = pl.program_id(1) * qpb + k
    q_rows = pl.ds(k * Q_BLK, Q_BLK)
    s0 = qi * Q_BLK
    cur = qi
    ncmp = kc_ref.shape[1]
    per_chunk = KCH // SEL_LEN
    groups = range(N_KV)
    n_chunks = qi // per_chunk + 1
    cur_slab = k % per_chunk
    cut_rows = slice(cur_slab * SEL_LEN, (cur_slab + 1) * SEL_LEN)
    last_slabs = per_chunk // 2 if cur_slab < per_chunk // 2 else per_chunk

    def scores(k_ref, i):
        k0 = pl.multiple_of(i * KCH, KCH)
        return [_dot(k_ref[g // 2, pl.ds(k0, KCH), :], qtp_ref[g]) for g in groups]

    def park(buf_ref, vals):
        for g in groups:
            buf_ref[g] = vals[g]

    def cut(buf_ref, tri):
        for g in groups:
            buf_ref[g, cut_rows, :] = buf_ref[g, cut_rows, :] + tri

    def update(g, blocks, shifts, vt, m_ref, acc_ref):
        m = m_ref[g]
        mnew = m
        for blk, sh in zip(blocks, shifts):
            mnew = jnp.maximum(mnew, jnp.max(blk, axis=0, keepdims=True) + sh)
        e = jnp.concatenate([jnp.exp2(blk + (sh - mnew)) for blk, sh in zip(blocks, shifts)], axis=0).astype(BF16)
        acc_ref[g] = jnp.exp2(m - mnew) * acc_ref[g] + _dot(vt, e)
        m_ref[g] = mnew

    def slabs(buf_ref, g, ns):
        return [buf_ref[g, nb * SEL_LEN:(nb + 1) * SEL_LEN, :] for nb in range(ns)]

    def sel_softmax(i, ns=per_chunk):
        for g in groups:
            shifts = [bias_ref[g, pl.ds(i * per_chunk + nb, 1), :] for nb in range(ns)]
            update(g, slabs(sbuf_ref, g, ns), shifts, vst_ref[i, g, :, 0:ns * SEL_LEN], ms_ref, accs_ref)

    key_row = _iota((SEL_LEN, GW), 0)
    in_blk_t = _iota((SEL_LEN, GW), 1) % Q_BLK
    causal_tri = jnp.where(key_row <= in_blk_t, 0.0, NEG)

    if phase == "step":
        nxt = scores(ks_ref, step_i + 1)
        sel_softmax(step_i)
        park(sbuf_ref, nxt)
        return

    if phase == "tail":
        cut(sbuf_ref, causal_tri)
        sel_softmax(n_chunks - 1, last_slabs)
        outs = []
        for g in groups:
            o_s = accs_ref[g, 0:HEAD_DIM, :] * (1.0 / jnp.maximum(accs_ref[g, HEAD_DIM:HEAD_DIM + 1, :], 1e-30))
            gates = gates_ref[g]
            o_t = gates[0:1] * oc_ref[g] + gates[1:2] * o_s + gates[2:3] * ow_ref[g]
            outs.append(_swap_heads(o_t))
        for g in groups:
            o_ref[g, q_rows, :] = outs[g].astype(o_ref.dtype)
        return

    lane_t = s0 + (_iota((1, GW), 1) % Q_BLK)
    lane_grp = _iota((1, GW), 1) // Q_BLK
    ov_n = _iota((nsel, ncmp), 0) * SEL_LEN
    ov_c = _iota((nsel, ncmp), 1) * CMP_STRIDE
    overlap = jnp.where((ov_c < ov_n + SEL_LEN) & (ov_c + CMP_LEN > ov_n), 1.0, 0.0).astype(BF16)


    ok_c = _iota((ncmp, GW), 0) * CMP_STRIDE + (CMP_LEN - 1) <= lane_t
    ngt = jnp.concatenate([ng_ref[q_rows, :], jnp.zeros((128 - Q_BLK, NG_PAD), F32)], axis=0).T
    ngt_hi = pltpu.roll(ngt, Q_BLK, axis=1)
    low_half = _iota((1, 128), 1) < Q_BLK
    for g in groups:
        for c in range(3):
            r = c * N_HEADS + g * HPG
            pairs = [jnp.where(low_half, ngt[r + j:r + j + 1], ngt_hi[r + j + 1:r + j + 2]) for j in (0, 2)]
            gates_ref[g, c:c + 1, :] = jnp.concatenate(pairs, axis=1)

    qts = []
    for g in groups:
        xt = _swap_heads(zq_ref[g, q_rows, :])
        ssq = jnp.sum(xt * xt, axis=0, keepdims=True)
        qts.append((xt * lax.rsqrt(ssq * (1.0 / HEAD_DIM) + EPS) * qg_ref[...]).astype(BF16))
    scs = []
    for g in groups:
        off = (g % 2) * HEAD_DIM
        qtp_ref[g, off:off + HEAD_DIM, :] = qts[g]
        qtp_ref[g, HEAD_DIM - off:2 * HEAD_DIM - off, :] = jnp.zeros((HEAD_DIM, GW), BF16)
        scs.append(_dot(kc_ref[g], qts[g]))

    n_win = WINDOW // KCH + 1
    win_ids = [jnp.maximum(n_chunks - (n_win - j), 0) for j in range(n_win)]
    win_dead = [jnp.where(n_chunks >= n_win - j, 0.0, NEG) for j in range(n_win)]
    park(scs_ref, scs)
    park(sbuf_ref, scores(ks_ref, 0))
    park(wbuf_ref, scores(kw_ref, win_ids[0]))

    expired_tri = jnp.where(key_row > in_blk_t, 0.0, NEG)
    win_blocks = WINDOW // SEL_LEN
    n_all = _iota((nsel, GW), 0)
    wbias_ref[...] = jnp.where((n_all >= cur - win_blocks) & (n_all <= cur), 0.0, NEG)
    cut(wbuf_ref, jnp.where(cur >= win_blocks, expired_tri, 0.0))

    ms_ref[...] = jnp.full(ms_ref.shape, NEG, F32)
    mw_ref[...] = jnp.full(mw_ref.shape, NEG, F32)
    accs_ref[...] = jnp.zeros(accs_ref.shape, F32)
    accw_ref[...] = jnp.zeros(accw_ref.shape, F32)

    def win_softmax(i, dead, ns=per_chunk):
        shifts = [wbias_ref[pl.ds(i * per_chunk + nb, 1), :] + dead for nb in range(ns)]
        for g in groups:
            update(g, slabs(wbuf_ref, g, ns), shifts, vwt_ref[i, g, :, 0:ns * SEL_LEN], mw_ref, accw_ref)

    for j in range(n_win - 1):
        nxt = scores(kw_ref, win_ids[j + 1])
        win_softmax(win_ids[j], win_dead[j])
        park(wbuf_ref, nxt)

    imp_all = jnp.zeros((nsel, GW), F32)
    some_c = lane_t >= CMP_LEN - 1
    for g in groups:
        scm = jnp.where(ok_c, scs_ref[g], NEG)
        ec = jnp.exp2(scm - jnp.max(scm, axis=0, keepdims=True))
        inv = jnp.where(some_c, 1.0 / jnp.maximum(jnp.sum(ec, axis=0, keepdims=True), 1e-30), 0.0)
        both = _dot(jnp.concatenate([vct_ref[g], overlap], axis=0), ec.astype(BF16)) * inv
        oc_ref[g] = both[0:HEAD_DIM]
        imp = _spread_heads(both[HEAD_DIM:HEAD_DIM + nsel])
        imp_all = jnp.where(lane_grp == g, imp, imp_all)

    n_io = _iota((nsel, GW), 0)
    forced = (n_io == 0) | (n_io == cur) | (n_io == cur - 1)
    valid = n_io <= cur
    score = jnp.where(forced, FORCE_SCORE, imp_all)
    score = jnp.where(valid, score, -1.0)
    rows8 = [score[r:r + 8] for r in range(0, nsel, 8)]
    ranks = [jnp.zeros((8, GW), F32) for _ in rows8]
    sub = _iota((8, GW), 0)
    for m in range(nsel):
        rowv = score[m:m + 1, :]
        for k, blk in enumerate(rows8):
            ge = jnp.where(rowv >= blk, 1.0, 0.0)
            gt = jnp.where(rowv > blk, 1.0, 0.0)
            if 8 * k > m:
                inc = ge
            elif 8 * k + 7 < m:
                inc = gt
            else:
                inc = jnp.where(sub + 8 * k > m, ge, gt)
            ranks[k] = ranks[k] + inc
    rank = jnp.concatenate(ranks, axis=0)
    sel_all = jnp.where((rank < float(min(SEL_TOPN, nsel))) & valid, 1.0, 0.0)
    for g in groups:
        mine = jnp.where(lane_grp == g, sel_all, 0.0)
        bias_ref[g] = (_spread_heads(mine) - 1.0) * (-NEG)

    cut(wbuf_ref, causal_tri)
    win_softmax(win_ids[n_win - 1], win_dead[n_win - 1], last_slabs)
    for g in groups:
        ow_ref[g] = accw_ref[g, 0:HEAD_DIM, :] * (1.0 / jnp.maximum(accw_ref[g, HEAD_DIM:HEAD_DIM + 1, :], 1e-30))


def _nsa(zq, qgain, kc, vct, kn, vt, ng, *, B, S):
    T = B * S
    qpb = KCH // SEL_LEN
    assert S % KCH == 0
    nq = S // (Q_BLK * qpb)
    nch = S // KCH
    ncmp = S // CMP_STRIDE
    nsel = S // SEL_LEN
    tok = lambda b, i: (0, b * nq + i, 0)
    return pl.pallas_call(
        functools.partial(_nsa_kernel, nsel=nsel, qpb=qpb),
        grid=(B, nq),
        in_specs=[
            pl.BlockSpec((N_KV, Q_BLK * qpb, GW), tok),
            _const_spec((HEAD_DIM, GW)),
            pl.BlockSpec((None, N_KV, ncmp, HEAD_DIM), lambda b, i: (b, 0, 0, 0)),
            pl.BlockSpec((None, N_KV, HEAD_DIM, ncmp), lambda b, i: (b, 0, 0, 0)),
            pl.BlockSpec((None, None, 2, S, 128), lambda b, i: (0, b, 0, 0, 0)),
            pl.BlockSpec((None, None, 2, S, 128), lambda b, i: (1, b, 0, 0, 0)),
            pl.BlockSpec((None, None, nch, N_KV, VROWS, KCH), lambda b, i: (0, b, 0, 0, 0, 0)),
            pl.BlockSpec((None, None, nch, N_KV, VROWS, KCH), lambda b, i: (1, b, 0, 0, 0, 0)),
            pl.BlockSpec((Q_BLK * qpb, NG_PAD), lambda b, i: (b * nq + i, 0)),
        ],
        out_specs=pl.BlockSpec((N_KV, Q_BLK * qpb, GW), tok),
        out_shape=jax.ShapeDtypeStruct((N_KV, T, GW), BF16),
        scratch_shapes=[
            pltpu.VMEM((qpb, N_KV, 2 * HEAD_DIM, GW), BF16),
            pltpu.VMEM((qpb, N_KV, HEAD_DIM, GW), F32),
            pltpu.VMEM((qpb, N_KV, HEAD_DIM, GW), F32),
            pltpu.VMEM((qpb, N_KV, nsel, GW), F32),
            pltpu.VMEM((qpb, N_KV, 1, GW), F32),
            pltpu.VMEM((qpb, N_KV, VROWS, GW), F32),
            pltpu.VMEM((N_KV, 1, GW), F32),
            pltpu.VMEM((N_KV, VROWS, GW), F32),
            pltpu.VMEM((qpb, N_KV, KCH, GW), F32),
            pltpu.VMEM((N_KV, KCH, GW), F32),
            pltpu.VMEM((qpb, N_KV, 8, GW), F32),
            pltpu.VMEM((N_KV, ncmp, GW), F32),
            pltpu.VMEM((nsel, GW), F32),
        ],
        compiler_params=_params(("parallel", "arbitrary")),
        name="nsa",
    )(zq, qgain, kc, vct, kn, kn, vt, vt, ng)


def _merge_kernel(x_ref, ya_ref, o_ref_in, mg_ref, wn_ref, wo_ref, out_ref):
    yb = _dot(o_ref_in[0], wn_ref[0:GW, :])
    for g in range(1, N_KV):
        yb = yb + _dot(o_ref_in[g], wn_ref[g * GW:(g + 1) * GW, :])
    tm = x_ref.shape[0]
    ya = jnp.concatenate([ya_ref[ck].reshape(tm, 128) for ck in range(D_MODEL // 128)], axis=1)
    mixed = (mg_ref[:, 0:D_MODEL].astype(F32) * ya
             + mg_ref[:, D_MODEL:2 * D_MODEL].astype(F32) * yb)
    out_ref[...] = x_ref[...] + _dot(mixed.astype(BF16), wo_ref[...])


def _merge(x2d, ya, o, mg, wn, wo, *, B, S, tm=512):
    ni = S // tm
    tok = lambda b, i: (b * ni + i, 0)
    return pl.pallas_call(
        _merge_kernel,
        grid=(B, ni),
        in_specs=[
            pl.BlockSpec((tm, D_MODEL), tok),
            pl.BlockSpec((D_MODEL // 128, tm // 8, None, 8, 128), lambda b, i: (0, i, b, 0, 0)),
            pl.BlockSpec((N_KV, tm, GW), lambda b, i: (0, b * ni + i, 0)),
            pl.BlockSpec((tm, 2 * D_MODEL), tok),
            _const_spec((N_HEADS * HEAD_DIM, D_MODEL)),
            _const_spec((D_MODEL, D_MODEL)),
        ],
        out_specs=pl.BlockSpec((tm, D_MODEL), tok),
        out_shape=jax.ShapeDtypeStruct((B * S, D_MODEL), F32),
        compiler_params=_params(("parallel", "parallel")),
        name="merge",
    )(x2d, ya, o, mg, wn, wo)


def _ffn_kernel(x_ref, g_ref, wg_ref, wu_ref, wd_ref, o_ref, h_ref, *, tf):
    x = x_ref[...]
    y = x * lax.rsqrt(jnp.mean(x * x, axis=-1, keepdims=True) + EPS) * g_ref[...]
    h_ref[...] = y.astype(BF16)
    o_ref[...] = x
    for j in range(D_FF // tf):
        h = h_ref[...]
        sl = slice(j * tf, (j + 1) * tf)
        act = jax.nn.silu(_dot(h, wg_ref[:, sl])) * _dot(h, wu_ref[:, sl])
        o_ref[...] += _dot(act.astype(BF16), wd_ref[sl, :])


def _ffn(x2d, gain, wg, wu, wd, *, tm=512, tf=256):
    T = x2d.shape[0]
    assert T % tm == 0 and D_FF % tf == 0
    once = dict(pipeline_mode=pl.Buffered(1))
    return pl.pallas_call(
        functools.partial(_ffn_kernel, tf=tf),
        grid=(T // tm,),
        in_specs=[
            pl.BlockSpec((tm, D_MODEL), lambda i: (i, 0)),
            _const_spec((1, D_MODEL)),
            pl.BlockSpec((D_MODEL, D_FF), lambda i: (0, 0), **once),
            pl.BlockSpec((D_MODEL, D_FF), lambda i: (0, 0), **once),
            pl.BlockSpec((D_FF, D_MODEL), lambda i: (0, 0), **once),
        ],
        out_specs=pl.BlockSpec((tm, D_MODEL), lambda i: (i, 0)),
        out_shape=jax.ShapeDtypeStruct((T, D_MODEL), F32),
        scratch_shapes=[pltpu.VMEM((tm, D_MODEL), BF16)],
        compiler_params=_params(("parallel",)),
        name="ffn",
    )(x2d, gain, wg, wu, wd)


def _layer(x, p):
    B, S, _ = x.shape
    T = B * S
    dh = HEAD_DIM
    row = lambda v: v.reshape(1, -1)
    w_in = p["w_in"]
    n_ng = 3 * N_HEADS
    o_ng = 2 * D_RNN + N_HEADS * dh + 6 * KVW
    w_packed = jnp.concatenate(
        [w_in[:, :o_ng], jnp.pad(w_in[:, o_ng:o_ng + n_ng], ((0, 0), (0, NG_PAD - n_ng))), w_in[:, o_ng + n_ng:]],
        axis=1).astype(BF16)
    rx, gy, zq, zkv, ng, mg = _inproj(x, row(p["norm1"]), w_packed)

    slabs = lambda a: a.reshape(a.shape[0], S * B, 128)
    ya = _rglru(slabs(rx), slabs(gy), p["conv_w"], row(p["conv_b"]),
                p["rg_wa"].astype(BF16), row(p["rg_ba"]), p["rg_wi"].astype(BF16), row(p["rg_bi"]),
                row(p["rg_lambda"]), p["w_rg_out"].astype(BF16), nb=B)
    ya = ya.reshape(D_MODEL // 128, S // 8, B, 8, 128)

    kgains = jnp.tile(p["k_norm"][1:3], (1, N_KV)).reshape(2, 1, KVW)
    kn, vt = _kvprep(zkv, kgains, B=B, S=S)
    pos = jnp.stack([p["cmp_pos_k"], p["cmp_pos_v"]]).reshape(2, CMP_STRIDE, 2 * dh)
    w1 = jnp.stack([p["cmp_k_w1"], p["cmp_v_w1"]]).astype(BF16)
    w2 = jnp.stack([p["cmp_k_w2"], p["cmp_v_w2"]]).astype(BF16)
    kc, vct = _compress(zkv, pos, w1, w2, row(p["k_norm"][0]), B=B, S=S)
    qgain = jnp.broadcast_to((p["q_norm"] * (dh ** -0.5 * LOG2E))[:, None], (dh, GW))
    o = _nsa(zq, qgain, kc, vct, kn, vt, ng, B=B, S=S)

    x1 = _merge(x.reshape(T, D_MODEL), ya, o, mg, p["w_nsa_out"].astype(BF16), p["w_o"].astype(BF16), B=B, S=S)
    x2 = _ffn(x1, row(p["norm2"]), p["w_gate"].astype(BF16), p["w_up"].astype(BF16), p["w_down"].astype(BF16))
    return x2.reshape(B, S, D_MODEL)


def kernel(x, norm1, w_in, conv_w, conv_b, rg_wa, rg_ba, rg_wi, rg_bi, rg_lambda, q_norm, k_norm, cmp_pos_k,
           cmp_pos_v, cmp_k_w1, cmp_k_w2, cmp_v_w1, cmp_v_w2, w_rg_out, w_nsa_out, w_o, norm2, w_gate, w_up, w_down):
    params = dict(norm1=norm1, w_in=w_in, conv_w=conv_w, conv_b=conv_b, rg_wa=rg_wa, rg_ba=rg_ba, rg_wi=rg_wi,
                  rg_bi=rg_bi, rg_lambda=rg_lambda, q_norm=q_norm, k_norm=k_norm, cmp_pos_k=cmp_pos_k,
                  cmp_pos_v=cmp_pos_v, cmp_k_w1=cmp_k_w1, cmp_k_w2=cmp_k_w2, cmp_v_w1=cmp_v_w1, cmp_v_w2=cmp_v_w2,
                  w_rg_out=w_rg_out, w_nsa_out=w_nsa_out, w_o=w_o, norm2=norm2, w_gate=w_gate, w_up=w_up,
                  w_down=w_down)
    for l in range(norm1.shape[0]):
        x = _layer(x, {k: v[l] for k, v in params.items()})
    return x
```

```python
import functools

import jax
import jax.numpy as jnp
from jax import lax
from jax.experimental import pallas as pl
from jax.experimental.pallas import tpu as pltpu

D_MODEL = 1024
D_RNN = 1024
RG_BLOCKS = 4
RG_BW = D_RNN // RG_BLOCKS
CONV_W = 4
RG_C = 8.0
N_HEADS = 16
N_KV = 4
HEAD_DIM = 64
HPG = N_HEADS // N_KV
CMP_LEN = 32
CMP_STRIDE = 16
CMP_HID = 256
SEL_LEN = 64
SEL_TOPN = 16
WINDOW = 512
Q_BLK = 64
D_FF = 2816
EPS = 1e-6
FORCE_SCORE = 1e6
NEG = -1e30
LOG2E = 1.4426950408889634

KCH = 256
GW = HPG * HEAD_DIM
KVW = N_KV * HEAD_DIM
VROWS = 80
NG_PAD = 128

F32 = jnp.float32
BF16 = jnp.bfloat16

VMEM_LIMIT = 56 * 1024 * 1024


def _params(sem):
    return pltpu.CompilerParams(dimension_semantics=sem, vmem_limit_bytes=VMEM_LIMIT)


def _dot(a, b):
    return jnp.dot(a, b, preferred_element_type=F32)


def _dot_nt(a, b):
    return lax.dot_general(a, b, (((1,), (1,)), ((), ())), preferred_element_type=F32)


def _split(x):
    hi = x.astype(BF16)
    lo = (x - hi.astype(F32)).astype(BF16)
    return hi, lo


def _iota(shape, dim):
    return lax.broadcasted_iota(jnp.int32, shape, dim)


def _const_spec(shape):
    return pl.BlockSpec(shape, lambda *_: (0,) * len(shape))


_O_RY = D_RNN
_O_Q = 2 * D_RNN
_O_KV = _O_Q + N_HEADS * HEAD_DIM
_O_NG = _O_KV + 6 * KVW
_O_MG = _O_NG + NG_PAD
_W_COLS = _O_MG + 2 * D_MODEL


def _inproj_kernel(x_ref, g_ref, w_ref, rx_ref, gy_ref, zq_ref, zkv_ref, ng_ref, mg_ref, h_ref):
    x = x_ref[...]
    y = x * lax.rsqrt(jnp.mean(x * x, axis=-1, keepdims=True) + EPS) * g_ref[...]
    h_ref[...] = y.astype(BF16)

    def proj(c0, width):
        return _dot(h_ref[...], w_ref[:, c0:c0 + width])

    cw = 256
    tm = x_ref.shape[0]
    for c in range(D_RNN // cw):
        rx = proj(c * cw, cw)
        gy = jax.nn.gelu(proj(_O_RY + c * cw, cw))
        for hf in range(cw // 128):
            rx_ref[c * (cw // 128) + hf] = rx[:, hf * 128:(hf + 1) * 128].reshape(tm // 8, 8, 128)
            gy_ref[c * (cw // 128) + hf] = gy[:, hf * 128:(hf + 1) * 128].reshape(tm // 8, 8, 128)
    for g in range(N_KV):
        zq_ref[g] = proj(_O_Q + g * GW, GW)
    for s in range(6):
        z = proj(_O_KV + s * KVW, KVW)
        for pr in range(KVW // 128):
            zkv_ref[s, pr] = z[:, pr * 128:(pr + 1) * 128]
    ng_ref[...] = jax.nn.sigmoid(proj(_O_NG, NG_PAD))
    for c in range(2 * D_MODEL // cw):
        mg_ref[:, c * cw:(c + 1) * cw] = jax.nn.sigmoid(proj(_O_MG + c * cw, cw)).astype(mg_ref.dtype)


def _inproj(x, gain, w, *, tm=256):
    B, S, D = x.shape
    T = B * S
    ni = S // tm
    nck = D_RNN // 128
    tok = lambda b, i: (b * ni + i, 0)
    slab_spec = pl.BlockSpec((nck, tm // 8, None, 8, 128), lambda b, i: (0, i, b, 0, 0))
    slab_shape = jax.ShapeDtypeStruct((nck, S // 8, B, 8, 128), F32)
    return pl.pallas_call(
        _inproj_kernel,
        grid=(B, ni),
        in_specs=[
            pl.BlockSpec((None, tm, D), lambda b, i: (b, i, 0)),
            _const_spec((1, D)),
            pl.BlockSpec((D, _W_COLS), lambda b, i: (0, 0), pipeline_mode=pl.Buffered(1)),
        ],
        out_specs=[
            slab_spec,
            slab_spec,
            pl.BlockSpec((N_KV, tm, GW), lambda b, i: (0, b * ni + i, 0)),
            pl.BlockSpec((6, KVW // 128, tm, 128), lambda b, i: (0, 0, b * ni + i, 0)),
            pl.BlockSpec((tm, NG_PAD), tok),
            pl.BlockSpec((tm, 2 * D_MODEL), tok),
        ],
        out_shape=[
            slab_shape,
            slab_shape,
            jax.ShapeDtypeStruct((N_KV, T, GW), F32),
            jax.ShapeDtypeStruct((6, KVW // 128, T, 128), F32),
            jax.ShapeDtypeStruct((T, NG_PAD), F32),
            jax.ShapeDtypeStruct((T, 2 * D_MODEL), BF16),
        ],
        scratch_shapes=[pltpu.VMEM((tm, D), BF16)],
        compiler_params=_params(("parallel", "parallel")),
        name="inproj",
    )(x, gain, w)


def _rglru_kernel(x_ref, gy_ref, cw_ref, cb_ref, wa_ref, ba_ref, wi_ref, bi_ref, lam_ref, wo_ref,
                  o_ref, xb_ref, gyb_ref, a_ref, u_ref, hb_ref, hs_ref, yb_ref, *, nb, tt):
    R = nb * tt
    halo = (CONV_W - 1) * nb
    step = pl.program_id(0)
    nck = D_RNN // 128

    @pl.when(step == 0)
    def _():
        xb_ref[0:halo, :] = jnp.zeros((halo, D_RNN), F32)
        hs_ref[...] = jnp.zeros((nb, D_RNN), F32)

    def slab_rows(t8, t_lo):
        return pl.ds(t8 * (8 * nb) + t_lo, nb, stride=8)

    def stage_in(t8, carry):
        for t_lo in range(8):
            r0 = pl.multiple_of((t8 * 8 + t_lo) * nb, nb)
            for ck in range(nck):
                cols = slice(ck * 128, (ck + 1) * 128)
                xb_ref[pl.ds(halo + r0, nb), cols] = x_ref[ck, slab_rows(t8, t_lo), :]
                gyb_ref[pl.ds(r0, nb), cols] = gy_ref[ck, slab_rows(t8, t_lo), :]
        return carry

    lax.fori_loop(0, tt // 8, stage_in, 0)
    xr = cb_ref[...] + cw_ref[0:1, :] * xb_ref[0:R, :]
    for k in range(1, CONV_W):
        xr = xr + cw_ref[k:k + 1, :] * xb_ref[k * nb:k * nb + R, :]
    tail = xb_ref[R:R + halo, :]
    xb_ref[0:halo, :] = tail

    xrb = xr.astype(BF16)
    sp = jax.nn.softplus(-lam_ref[...])
    row = _iota((R, RG_BW), 0)
    first = (row < nb) & (step == 0)
    for n in range(RG_BLOCKS):
        sl = slice(n * RG_BW, (n + 1) * RG_BW)
        xn = xrb[:, sl]
        r = jax.nn.sigmoid(_dot(xn, wa_ref[n]) + ba_ref[:, sl])
        ig = jax.nn.sigmoid(_dot(xn, wi_ref[n]) + bi_ref[:, sl])
        log_a = (-RG_C) * r * sp[:, sl]
        a = jnp.exp(log_a)
        mult = jnp.sqrt(1.0 - a * a)
        mult = jnp.where(first, 1.0, mult)
        a_ref[:, sl] = a
        u_ref[:, sl] = mult * ig * xr[:, sl]

    def body(t, h):
        r0 = pl.multiple_of(t * nb, nb)
        h = a_ref[pl.ds(r0, nb), :] * h + u_ref[pl.ds(r0, nb), :]
        hb_ref[pl.ds(r0, nb), :] = h
        return h

    h = lax.fori_loop(0, tt, body, hs_ref[...], unroll=8)
    hs_ref[...] = h
    yb_ref[...] = _dot((hb_ref[...] * gyb_ref[...]).astype(BF16), wo_ref[...])

    def stage_out(t8, carry):
        for t_lo in range(8):
            r0 = pl.multiple_of((t8 * 8 + t_lo) * nb, nb)
            for ck in range(nck):
                o_ref[ck, slab_rows(t8, t_lo), :] = yb_ref[pl.ds(r0, nb), ck * 128:(ck + 1) * 128]
        return carry

    lax.fori_loop(0, tt // 8, stage_out, 0)


def _rglru(rx, gy, conv_w, conv_b, wa, ba, wi, bi, lam, wo, *, nb, tt=32):
    nck, rows, _ = rx.shape
    S = rows // nb
    assert S % tt == 0 and tt % 8 == 0
    R = nb * tt
    halo = (CONV_W - 1) * nb
    return pl.pallas_call(
        functools.partial(_rglru_kernel, nb=nb, tt=tt),
        grid=(S // tt,),
        in_specs=[
            pl.BlockSpec((nck, R, 128), lambda s: (0, s, 0)),
            pl.BlockSpec((nck, R, 128), lambda s: (0, s, 0)),
            _const_spec((CONV_W, D_RNN)),
            _const_spec((1, D_RNN)),
            _const_spec((RG_BLOCKS, RG_BW, RG_BW)),
            _const_spec((1, D_RNN)),
            _const_spec((RG_BLOCKS, RG_BW, RG_BW)),
            _const_spec((1, D_RNN)),
            _const_spec((1, D_RNN)),
            _const_spec((D_RNN, D_MODEL)),
        ],
        out_specs=pl.BlockSpec((D_MODEL // 128, R, 128), lambda s: (0, s, 0)),
        out_shape=jax.ShapeDtypeStruct((D_MODEL // 128, rows, 128), F32),
        scratch_shapes=[
            pltpu.VMEM((R + halo, D_RNN), F32),
            pltpu.VMEM((R, D_RNN), F32),
            pltpu.VMEM((R, D_RNN), F32),
            pltpu.VMEM((R, D_RNN), F32),
            pltpu.VMEM((R, D_RNN), F32),
            pltpu.VMEM((nb, D_RNN), F32),
            pltpu.VMEM((R, D_MODEL), F32),
        ],
        compiler_params=_params(("arbitrary",)),
        name="rglru",
    )(rx, gy, conv_w, conv_b, wa, ba, wi, bi, lam, wo)


def _kvprep_kernel(zk_ref, zv_ref, g_ref, kn_ref, vt_ref):
    rows = zk_ref.shape[1]
    x = jnp.concatenate([zk_ref[0], zk_ref[1]], axis=1)
    seg = jnp.where(_iota((KVW, KVW), 0) // HEAD_DIM == _iota((KVW, KVW), 1) // HEAD_DIM, 1.0, 0.0).astype(BF16)
    hi, lo = _split(x * x)
    ssq = _dot(jnp.concatenate([hi, lo], axis=0), seg)
    kn = (x * lax.rsqrt((ssq[0:rows] + ssq[rows:2 * rows]) * (1.0 / HEAD_DIM) + EPS) * g_ref[...]).astype(BF16)
    for pr in range(KVW // 128):
        kn_ref[pr] = kn[:, pr * 128:(pr + 1) * 128]
    eye = jnp.where(_iota((KVW, KVW), 0) == _iota((KVW, KVW), 1), 1.0, 0.0).astype(BF16)
    pad = jnp.where(_iota((VROWS - HEAD_DIM, KCH), 0) == 0, 1.0, 0.0).astype(BF16)
    for c in range(rows // KCH):
        sl = slice(c * KCH, (c + 1) * KCH)
        v = jnp.concatenate([zv_ref[0, sl, :], zv_ref[1, sl, :]], axis=1).astype(BF16)
        vt = _dot_nt(eye, v).astype(BF16)
        for g in range(N_KV):
            vt_ref[c, g, 0:HEAD_DIM, :] = vt[g * HEAD_DIM:(g + 1) * HEAD_DIM]
            vt_ref[c, g, HEAD_DIM:VROWS, :] = pad


def _kvprep(zkv, gains, *, B, S, cpb=4):
    nch = S // KCH
    assert nch % cpb == 0
    nblk = nch // cpb
    rows = cpb * KCH
    return pl.pallas_call(
        _kvprep_kernel,
        grid=(2, B, nblk),
        in_specs=[
            pl.BlockSpec((None, KVW // 128, rows, 128), lambda w, b, i: (2 + 2 * w, 0, b * nblk + i, 0)),
            pl.BlockSpec((None, KVW // 128, rows, 128), lambda w, b, i: (3 + 2 * w, 0, b * nblk + i, 0)),
            pl.BlockSpec((None, 1, KVW), lambda w, b, i: (w, 0, 0)),
        ],
        out_specs=[
            pl.BlockSpec((None, None, KVW // 128, rows, 128), lambda w, b, i: (w, b, 0, i, 0)),
            pl.BlockSpec((None, None, cpb, N_KV, VROWS, KCH), lambda w, b, i: (w, b, i, 0, 0, 0)),
        ],
        out_shape=[
            jax.ShapeDtypeStruct((2, B, KVW // 128, S, 128), BF16),
            jax.ShapeDtypeStruct((2, B, nch, N_KV, VROWS, KCH), BF16),
        ],
        compiler_params=_params(("parallel", "parallel", "parallel")),
        name="kvprep",
    )(zkv, zkv, gains)


def _compress_kernel(zk_ref, zv_ref, pos_ref, w1_ref, w2_ref, g_ref, kc_ref, vct_ref):
    nch = zk_ref.shape[0] // CMP_STRIDE
    lane = _iota((nch, 128), 1)
    npair = CMP_STRIDE // 2
    eye = jnp.where(_iota((HEAD_DIM, HEAD_DIM), 0) == _iota((HEAD_DIM, HEAD_DIM), 1), 1.0, 0.0).astype(BF16)
    for kind, z_ref in enumerate((zk_ref, zv_ref)):
        for ge in range(2):
            combs = []
            for p in range(npair):
                a = z_ref[pl.ds(2 * p, nch, stride=CMP_STRIDE), :]
                b = z_ref[pl.ds(2 * p + 1, nch, stride=CMP_STRIDE), :]
                if ge == 0:
                    combs.append(jnp.where(lane < HEAD_DIM, a, pltpu.roll(b, HEAD_DIM, axis=1)))
                else:
                    combs.append(jnp.where(lane < HEAD_DIM, pltpu.roll(a, HEAD_DIM, axis=1), b))
            halves = []
            half_w = npair * 128
            for half in range(2):
                lhs = jnp.concatenate(
                    [(combs[p] + pos_ref[kind, half * npair + p:half * npair + p + 1, :]).astype(BF16)
                     for p in range(npair)], axis=1)
                halves.append(_dot(lhs, w1_ref[kind, half * half_w:(half + 1) * half_w, :]))
            hid = halves[0] + pltpu.roll(halves[1], nch - 1, axis=0)
            hid = jax.nn.gelu(hid)
            out = _dot(hid.astype(BF16), w2_ref[kind])
            if kind == 0:
                out = out * lax.rsqrt(jnp.mean(out * out, axis=-1, keepdims=True) + EPS) * g_ref[...]
            res = jnp.where(_iota(out.shape, 0) < nch - 1, out, 0.0).astype(BF16)
            if kind == 0:
                kc_ref[ge] = res
            else:
                vct_ref[ge] = _dot_nt(eye, res).astype(BF16)


def _compress(zkv, pos, w1, w2, gain, *, B, S):
    nch = S // CMP_STRIDE
    return pl.pallas_call(
        _compress_kernel,
        grid=(B, KVW // 128),
        in_specs=[
            pl.BlockSpec((None, None, S, 128), lambda b, lp: (0, lp, b, 0)),
            pl.BlockSpec((None, None, S, 128), lambda b, lp: (1, lp, b, 0)),
            _const_spec((2, CMP_STRIDE, 128)),
            _const_spec((2, CMP_LEN * HEAD_DIM, CMP_HID)),
            _const_spec((2, CMP_HID, HEAD_DIM)),
            _const_spec((1, HEAD_DIM)),
        ],
        out_specs=[
            pl.BlockSpec((None, 2, nch, HEAD_DIM), lambda b, lp: (b, lp, 0, 0)),
            pl.BlockSpec((None, 2, HEAD_DIM, nch), lambda b, lp: (b, lp, 0, 0)),
        ],
        out_shape=[
            jax.ShapeDtypeStruct((B, N_KV, nch, HEAD_DIM), BF16),
            jax.ShapeDtypeStruct((B, N_KV, HEAD_DIM, nch), BF16),
        ],
        compiler_params=_params(("parallel", "parallel")),
        name="compress",
    )(zkv, zkv, pos, w1, w2, gain)


def _spread_heads(x):
    y = x + pltpu.roll(x, 2 * Q_BLK, axis=1)
    return y + pltpu.roll(y, Q_BLK, axis=1)


def _swap_heads(x):
    n = x.shape[0]
    lane = _iota((n, 128), 1)
    halves = []
    for pr in range(x.shape[1] // 128):
        slab = jnp.concatenate([x[:, pr * 128:(pr + 1) * 128], jnp.zeros((128 - n, 128), x.dtype)], axis=0)
        tr = slab.T
        halves.append(jnp.where(lane < n, tr[0:n], pltpu.roll(tr[n:2 * n], n, axis=1)))
    return jnp.concatenate(halves, axis=1)


def _nsa_kernel(*refs, nsel, qpb):
    per_chunk = KCH // SEL_LEN
    for k in range(qpb):
        _nsa_block(k, "head", None, *refs, nsel=nsel, qpb=qpb)

    def step(i, carry):
        for k in range(qpb):
            _nsa_block(k, "step", i, *refs, nsel=nsel, qpb=qpb)
        return carry

    common = pl.program_id(1) * (qpb // per_chunk)
    lax.fori_loop(0, common, step, 0)
    for k in range(qpb):
        for extra in range(k // per_chunk):
            _nsa_block(k, "step", common + extra, *refs, nsel=nsel, qpb=qpb)
    for k in range(qpb):
        _nsa_block(k, "tail", None, *refs, nsel=nsel, qpb=qpb)


def _nsa_block(k, phase, step_i, zq_ref, qg_ref, kc_ref, vct_ref, ks_ref, kw_ref, vst_ref, vwt_ref, ng_ref, o_ref,
               qtp_ref, oc_ref, ow_ref, bias_ref, ms_ref, accs_ref, mw_ref, accw_ref, sbuf_ref, wbuf_ref,
               gates_ref, scs_ref, wbias_ref, *, nsel, qpb):
    qtp_ref, oc_ref, ow_ref, bias_ref, ms_ref, accs_ref, sbuf_ref, gates_ref = (
        r.at[k] for r in (qtp_ref, oc_ref, ow_ref, bias_ref, ms_ref, accs_ref, sbuf_ref, gates_ref))
    qi = pl.program_id(1) * qpb + k
    q_rows = pl.ds(k * Q_BLK, Q_BLK)
    s0 = qi * Q_BLK
    cur = qi
    ncmp = kc_ref.shape[1]
    per_chunk = KCH // SEL_LEN
    groups = range(N_KV)
    n_chunks = qi // per_chunk + 1
    cur_slab = k % per_chunk
    cut_rows = slice(cur_slab * SEL_LEN, (cur_slab + 1) * SEL_LEN)
    last_slabs = per_chunk // 2 if cur_slab < per_chunk // 2 else per_chunk

    def scores(k_ref, i):
        k0 = pl.multiple_of(i * KCH, KCH)
        return [_dot(k_ref[g // 2, pl.ds(k0, KCH), :], qtp_ref[g]) for g in groups]

    def park(buf_ref, vals):
        for g in groups:
            buf_ref[g] = vals[g]

    def cut(buf_ref, tri):
        for g in groups:
            buf_ref[g, cut_rows, :] = buf_ref[g, cut_rows, :] + tri

    def update(g, blocks, shifts, vt, m_ref, acc_ref):
        m = m_ref[g]
        mnew = m
        for blk, sh in zip(blocks, shifts):
            mnew = jnp.maximum(mnew, jnp.max(blk, axis=0, keepdims=True) + sh)
        e = jnp.concatenate([jnp.exp2(blk + (sh - mnew)) for blk, sh in zip(blocks, shifts)], axis=0).astype(BF16)
        acc_ref[g] = jnp.exp2(m - mnew) * acc_ref[g] + _dot(vt, e)
        m_ref[g] = mnew

    def slabs(buf_ref, g, ns):
        return [buf_ref[g, nb * SEL_LEN:(nb + 1) * SEL_LEN, :] for nb in range(ns)]

    def sel_softmax(i, ns=per_chunk):
        for g in groups:
            shifts = [bias_ref[g, pl.ds(i * per_chunk + nb, 1), :] for nb in range(ns)]
            update(g, slabs(sbuf_ref, g, ns), shifts, vst_ref[i, g, :, 0:ns * SEL_LEN], ms_ref, accs_ref)

    key_row = _iota((SEL_LEN, GW), 0)
    in_blk_t = _iota((SEL_LEN, GW), 1) % Q_BLK
    causal_tri = jnp.where(key_row <= in_blk_t, 0.0, NEG)

    if phase == "step":
        nxt = scores(ks_ref, step_i + 1)
        sel_softmax(step_i)
        park(sbuf_ref, nxt)
        return

    if phase == "tail":
        cut(sbuf_ref, causal_tri)
        sel_softmax(n_chunks - 1, last_slabs)
        outs = []
        for g in groups:
            o_s = accs_ref[g, 0:HEAD_DIM, :] * (1.0 / jnp.maximum(accs_ref[g, HEAD_DIM:HEAD_DIM + 1, :], 1e-30))
            gates = gates_ref[g]
            o_t = gates[0:1] * oc_ref[g] + gates[1:2] * o_s + gates[2:3] * ow_ref[g]
            outs.append(_swap_heads(o_t))
        for g in groups:
            o_ref[g, q_rows, :] = outs[g].astype(o_ref.dtype)
        return

    lane_t = s0 + (_iota((1, GW), 1) % Q_BLK)
    lane_grp = _iota((1, GW), 1) // Q_BLK
    ov_n = _iota((nsel, ncmp), 0) * SEL_LEN
    ov_c = _iota((nsel, ncmp), 1) * CMP_STRIDE
    overlap = jnp.where((ov_c < ov_n + SEL_LEN) & (ov_c + CMP_LEN > ov_n), 1.0, 0.0).astype(BF16)


    ok_c = _iota((ncmp, GW), 0) * CMP_STRIDE + (CMP_LEN - 1) <= lane_t
    ngt = jnp.concatenate([ng_ref[q_rows, :], jnp.zeros((128 - Q_BLK, NG_PAD), F32)], axis=0).T
    ngt_hi = pltpu.roll(ngt, Q_BLK, axis=1)
    low_half = _iota((1, 128), 1) < Q_BLK
    for g in groups:
        for c in range(3):
            r = c * N_HEADS + g * HPG
            pairs = [jnp.where(low_half, ngt[r + j:r + j + 1], ngt_hi[r + j + 1:r + j + 2]) for j in (0, 2)]
            gates_ref[g, c:c + 1, :] = jnp.concatenate(pairs, axis=1)

    qts = []
    for g in groups:
        xt = _swap_heads(zq_ref[g, q_rows, :])
        ssq = jnp.sum(xt * xt, axis=0, keepdims=True)
        qts.append((xt * lax.rsqrt(ssq * (1.0 / HEAD_DIM) + EPS) * qg_ref[...]).astype(BF16))
    scs = []
    for g in groups:
        off = (g % 2) * HEAD_DIM
        qtp_ref[g, off:off + HEAD_DIM, :] = qts[g]
        qtp_ref[g, HEAD_DIM - off:2 * HEAD_DIM - off, :] = jnp.zeros((HEAD_DIM, GW), BF16)
        scs.append(_dot(kc_ref[g], qts[g]))

    n_win = WINDOW // KCH + 1
    win_ids = [jnp.maximum(n_chunks - (n_win - j), 0) for j in range(n_win)]
    win_dead = [jnp.where(n_chunks >= n_win - j, 0.0, NEG) for j in range(n_win)]
    park(scs_ref, scs)
    park(sbuf_ref, scores(ks_ref, 0))
    park(wbuf_ref, scores(kw_ref, win_ids[0]))

    expired_tri = jnp.where(key_row > in_blk_t, 0.0, NEG)
    win_blocks = WINDOW // SEL_LEN
    n_all = _iota((nsel, GW), 0)
    wbias_ref[...] = jnp.where((n_all >= cur - win_blocks) & (n_all <= cur), 0.0, NEG)
    cut(wbuf_ref, jnp.where(cur >= win_blocks, expired_tri, 0.0))

    ms_ref[...] = jnp.full(ms_ref.shape, NEG, F32)
    mw_ref[...] = jnp.full(mw_ref.shape, NEG, F32)
    accs_ref[...] = jnp.zeros(accs_ref.shape, F32)
    accw_ref[...] = jnp.zeros(accw_ref.shape, F32)

    def win_softmax(i, dead, ns=per_chunk):
        shifts = [wbias_ref[pl.ds(i * per_chunk + nb, 1), :] + dead for nb in range(ns)]
        for g in groups:
            update(g, slabs(wbuf_ref, g, ns), shifts, vwt_ref[i, g, :, 0:ns * SEL_LEN], mw_ref, accw_ref)

    for j in range(n_win - 1):
        nxt = scores(kw_ref, win_ids[j + 1])
        win_softmax(win_ids[j], win_dead[j])
        park(wbuf_ref, nxt)

    imp_all = jnp.zeros((nsel, GW), F32)
    some_c = lane_t >= CMP_LEN - 1
    for g in groups:
        scm = jnp.where(ok_c, scs_ref[g], NEG)
        ec = jnp.exp2(scm - jnp.max(scm, axis=0, keepdims=True))
        inv = jnp.where(some_c, 1.0 / jnp.maximum(jnp.sum(ec, axis=0, keepdims=True), 1e-30), 0.0)
        both = _dot(jnp.concatenate([vct_ref[g], overlap], axis=0), ec.astype(BF16)) * inv
        oc_ref[g] = both[0:HEAD_DIM]
        imp = _spread_heads(both[HEAD_DIM:HEAD_DIM + nsel])
        imp_all = jnp.where(lane_grp == g, imp, imp_all)

    n_io = _iota((nsel, GW), 0)
    forced = (n_io == 0) | (n_io == cur) | (n_io == cur - 1)
    valid = n_io <= cur
    score = jnp.where(forced, FORCE_SCORE, imp_all)
    score = jnp.where(valid, score, -1.0)
    rows8 = [score[r:r + 8] for r in range(0, nsel, 8)]
    ranks = [jnp.zeros((8, GW), F32) for _ in rows8]
    sub = _iota((8, GW), 0)
    for m in range(nsel):
        rowv = score[m:m + 1, :]
        for k, blk in enumerate(rows8):
            ge = jnp.where(rowv >= blk, 1.0, 0.0)
            gt = jnp.where(rowv > blk, 1.0, 0.0)
            if 8 * k > m:
                inc = ge
            elif 8 * k + 7 < m:
                inc = gt
            else:
                inc = jnp.where(sub + 8 * k > m, ge, gt)
            ranks[k] = ranks[k] + inc
    rank = jnp.concatenate(ranks, axis=0)
    sel_all = jnp.where((rank < float(min(SEL_TOPN, nsel))) & valid, 1.0, 0.0)
    for g in groups:
        mine = jnp.where(lane_grp == g, sel_all, 0.0)
        bias_ref[g] = (_spread_heads(mine) - 1.0) * (-NEG)

    cut(wbuf_ref, causal_tri)
    win_softmax(win_ids[n_win - 1], win_dead[n_win - 1], last_slabs)
    for g in groups:
        ow_ref[g] = accw_ref[g, 0:HEAD_DIM, :] * (1.0 / jnp.maximum(accw_ref[g, HEAD_DIM:HEAD_DIM + 1, :], 1e-30))


def _nsa(zq, qgain, kc, vct, kn, vt, ng, *, B, S):
    T = B * S
    qpb = 2 * (KCH // SEL_LEN)
    assert S % (Q_BLK * qpb) == 0
    nq = S // (Q_BLK * qpb)
    nch = S // KCH
    ncmp = S // CMP_STRIDE
    nsel = S // SEL_LEN
    tok = lambda b, i: (0, b * nq + i, 0)
    return pl.pallas_call(
        functools.partial(_nsa_kernel, nsel=nsel, qpb=qpb),
        grid=(B, nq),
        in_specs=[
            pl.BlockSpec((N_KV, Q_BLK * qpb, GW), tok),
            _const_spec((HEAD_DIM, GW)),
            pl.BlockSpec((None, N_KV, ncmp, HEAD_DIM), lambda b, i: (b, 0, 0, 0)),
            pl.BlockSpec((None, N_KV, HEAD_DIM, ncmp), lambda b, i: (b, 0, 0, 0)),
            pl.BlockSpec((None, None, 2, S, 128), lambda b, i: (0, b, 0, 0, 0)),
            pl.BlockSpec((None, None, 2, S, 128), lambda b, i: (1, b, 0, 0, 0)),
            pl.BlockSpec((None, None, nch, N_KV, VROWS, KCH), lambda b, i: (0, b, 0, 0, 0, 0)),
            pl.BlockSpec((None, None, nch, N_KV, VROWS, KCH), lambda b, i: (1, b, 0, 0, 0, 0)),
            pl.BlockSpec((Q_BLK * qpb, NG_PAD), lambda b, i: (b * nq + i, 0)),
        ],
        out_specs=pl.BlockSpec((N_KV, Q_BLK * qpb, GW), tok),
        out_shape=jax.ShapeDtypeStruct((N_KV, T, GW), BF16),
        scratch_shapes=[
            pltpu.VMEM((qpb, N_KV, 2 * HEAD_DIM, GW), BF16),
            pltpu.VMEM((qpb, N_KV, HEAD_DIM, GW), F32),
            pltpu.VMEM((qpb, N_KV, HEAD_DIM, GW), F32),
            pltpu.VMEM((qpb, N_KV, nsel, GW), F32),
            pltpu.VMEM((qpb, N_KV, 1, GW), F32),
            pltpu.VMEM((qpb, N_KV, VROWS, GW), F32),
            pltpu.VMEM((N_KV, 1, GW), F32),
            pltpu.VMEM((N_KV, VROWS, GW), F32),
            pltpu.VMEM((qpb, N_KV, KCH, GW), F32),
            pltpu.VMEM((N_KV, KCH, GW), F32),
            pltpu.VMEM((qpb, N_KV, 8, GW), F32),
            pltpu.VMEM((N_KV, ncmp, GW), F32),
            pltpu.VMEM((nsel, GW), F32),
        ],
        compiler_params=_params(("parallel", "arbitrary")),
        name="nsa",
    )(zq, qgain, kc, vct, kn, kn, vt, vt, ng)


def _merge_kernel(x_ref, ya_ref, o_ref_in, mg_ref, wn_ref, wo_ref, out_ref):
    yb = _dot(o_ref_in[0], wn_ref[0:GW, :])
    for g in range(1, N_KV):
        yb = yb + _dot(o_ref_in[g], wn_ref[g * GW:(g + 1) * GW, :])
    tm = x_ref.shape[0]
    ya = jnp.concatenate([ya_ref[ck].reshape(tm, 128) for ck in range(D_MODEL // 128)], axis=1)
    mixed = (mg_ref[:, 0:D_MODEL].astype(F32) * ya
             + mg_ref[:, D_MODEL:2 * D_MODEL].astype(F32) * yb)
    out_ref[...] = x_ref[...] + _dot(mixed.astype(BF16), wo_ref[...])


def _merge(x2d, ya, o, mg, wn, wo, *, B, S, tm=512):
    ni = S // tm
    tok = lambda b, i: (b * ni + i, 0)
    return pl.pallas_call(
        _merge_kernel,
        grid=(B, ni),
        in_specs=[
            pl.BlockSpec((tm, D_MODEL), tok),
            pl.BlockSpec((D_MODEL // 128, tm // 8, None, 8, 128), lambda b, i: (0, i, b, 0, 0)),
            pl.BlockSpec((N_KV, tm, GW), lambda b, i: (0, b * ni + i, 0)),
            pl.BlockSpec((tm, 2 * D_MODEL), tok),
            _const_spec((N_HEADS * HEAD_DIM, D_MODEL)),
            _const_spec((D_MODEL, D_MODEL)),
        ],
        out_specs=pl.BlockSpec((tm, D_MODEL), tok),
        out_shape=jax.ShapeDtypeStruct((B * S, D_MODEL), F32),
        compiler_params=_params(("parallel", "parallel")),
        name="merge",
    )(x2d, ya, o, mg, wn, wo)


def _ffn_kernel(x_ref, g_ref, wg_ref, wu_ref, wd_ref, o_ref, h_ref, *, tf):
    x = x_ref[...]
    y = x * lax.rsqrt(jnp.mean(x * x, axis=-1, keepdims=True) + EPS) * g_ref[...]
    h_ref[...] = y.astype(BF16)
    o_ref[...] = x
    for j in range(D_FF // tf):
        h = h_ref[...]
        sl = slice(j * tf, (j + 1) * tf)
        act = jax.nn.silu(_dot(h, wg_ref[:, sl])) * _dot(h, wu_ref[:, sl])
        o_ref[...] += _dot(act.astype(BF16), wd_ref[sl, :])


def _ffn(x2d, gain, wg, wu, wd, *, tm=512, tf=256):
    T = x2d.shape[0]
    assert T % tm == 0 and D_FF % tf == 0
    once = dict(pipeline_mode=pl.Buffered(1))
    return pl.pallas_call(
        functools.partial(_ffn_kernel, tf=tf),
        grid=(T // tm,),
        in_specs=[
            pl.BlockSpec((tm, D_MODEL), lambda i: (i, 0)),
            _const_spec((1, D_MODEL)),
            pl.BlockSpec((D_MODEL, D_FF), lambda i: (0, 0), **once),
            pl.BlockSpec((D_MODEL, D_FF), lambda i: (0, 0), **once),
            pl.BlockSpec((D_FF, D_MODEL), lambda i: (0, 0), **once),
        ],
        out_specs=pl.BlockSpec((tm, D_MODEL), lambda i: (i, 0)),
        out_shape=jax.ShapeDtypeStruct((T, D_MODEL), F32),
        scratch_shapes=[pltpu.VMEM((tm, D_MODEL), BF16)],
        compiler_params=_params(("parallel",)),
        name="ffn",
    )(x2d, gain, wg, wu, wd)


def _layer(x, p):
    B, S, _ = x.shape
    T = B * S
    dh = HEAD_DIM
    row = lambda v: v.reshape(1, -1)
    w_in = p["w_in"]
    n_ng = 3 * N_HEADS
    o_ng = 2 * D_RNN + N_HEADS * dh + 6 * KVW
    w_packed = jnp.concatenate(
        [w_in[:, :o_ng], jnp.pad(w_in[:, o_ng:o_ng + n_ng], ((0, 0), (0, NG_PAD - n_ng))), w_in[:, o_ng + n_ng:]],
        axis=1).astype(BF16)
    rx, gy, zq, zkv, ng, mg = _inproj(x, row(p["norm1"]), w_packed)

    slabs = lambda a: a.reshape(a.shape[0], S * B, 128)
    ya = _rglru(slabs(rx), slabs(gy), p["conv_w"], row(p["conv_b"]),
                p["rg_wa"].astype(BF16), row(p["rg_ba"]), p["rg_wi"].astype(BF16), row(p["rg_bi"]),
                row(p["rg_lambda"]), p["w_rg_out"].astype(BF16), nb=B)
    ya = ya.reshape(D_MODEL // 128, S // 8, B, 8, 128)

    kgains = jnp.tile(p["k_norm"][1:3], (1, N_KV)).reshape(2, 1, KVW)
    kn, vt = _kvprep(zkv, kgains, B=B, S=S)
    pos = jnp.stack([p["cmp_pos_k"], p["cmp_pos_v"]]).reshape(2, CMP_STRIDE, 2 * dh)
    w1 = jnp.stack([p["cmp_k_w1"], p["cmp_v_w1"]]).astype(BF16)
    w2 = jnp.stack([p["cmp_k_w2"], p["cmp_v_w2"]]).astype(BF16)
    kc, vct = _compress(zkv, pos, w1, w2, row(p["k_norm"][0]), B=B, S=S)
    qgain = jnp.broadcast_to((p["q_norm"] * (dh ** -0.5 * LOG2E))[:, None], (dh, GW))
    o = _nsa(zq, qgain, kc, vct, kn, vt, ng, B=B, S=S)

    x1 = _merge(x.reshape(T, D_MODEL), ya, o, mg, p["w_nsa_out"].astype(BF16), p["w_o"].astype(BF16), B=B, S=S)
    x2 = _ffn(x1, row(p["norm2"]), p["w_gate"].astype(BF16), p["w_up"].astype(BF16), p["w_down"].astype(BF16))
    return x2.reshape(B, S, D_MODEL)


def kernel(x, norm1, w_in, conv_w, conv_b, rg_wa, rg_ba, rg_wi, rg_bi, rg_lambda, q_norm, k_norm, cmp_pos_k,
           cmp_pos_v, cmp_k_w1, cmp_k_w2, cmp_v_w1, cmp_v_w2, w_rg_out, w_nsa_out, w_o, norm2, w_gate, w_up, w_down):
    params = dict(norm1=norm1, w_in=w_in, conv_w=conv_w, conv_b=conv_b, rg_wa=rg_wa, rg_ba=rg_ba, rg_wi=rg_wi,
                  rg_bi=rg_bi, rg_lambda=rg_lambda, q_norm=q_norm, k_norm=k_norm, cmp_pos_k=cmp_pos_k,
                  cmp_pos_v=cmp_pos_v, cmp_k_w1=cmp_k_w1, cmp_k_w2=cmp_k_w2, cmp_v_w1=cmp_v_w1, cmp_v_w2=cmp_v_w2,
                  w_rg_out=w_rg_out, w_nsa_out=w_nsa_out, w_o=w_o, norm2=norm2, w_gate=w_gate, w_up=w_up,
                  w_down=w_down)
    for l in range(norm1.shape[0]):
        x = _layer(x, {k: v[l] for k, v in params.items()})
    return x
```

```python
import functools

import jax
import jax.numpy as jnp
from jax import lax
from jax.experimental import pallas as pl
from jax.experimental.pallas import tpu as pltpu

D_MODEL = 1024
D_RNN = 1024
RG_BLOCKS = 4
RG_BW = D_RNN // RG_BLOCKS
CONV_W = 4
RG_C = 8.0
N_HEADS = 16
N_KV = 4
HEAD_DIM = 64
HPG = N_HEADS // N_KV
CMP_LEN = 32
CMP_STRIDE = 16
CMP_HID = 256
SEL_LEN = 64
SEL_TOPN = 16
WINDOW = 512
Q_BLK = 64
D_FF = 2816
EPS = 1e-6
FORCE_SCORE = 1e6
NEG = -1e30
LOG2E = 1.4426950408889634

KCH = 256
GW = HPG * HEAD_DIM
KVW = N_KV * HEAD_DIM
VROWS = 80
NG_PAD = 128

F32 = jnp.float32
BF16 = jnp.bfloat16

VMEM_LIMIT = 56 * 1024 * 1024


def _params(sem):
    return pltpu.CompilerParams(dimension_semantics=sem, vmem_limit_bytes=VMEM_LIMIT)


def _dot(a, b):
    return jnp.dot(a, b, preferred_element_type=F32)


def _dot_nt(a, b):
    return lax.dot_general(a, b, (((1,), (1,)), ((), ())), preferred_element_type=F32)


def _split(x):
    hi = x.astype(BF16)
    lo = (x - hi.astype(F32)).astype(BF16)
    return hi, lo


def _iota(shape, dim):
    return lax.broadcasted_iota(jnp.int32, shape, dim)


def _const_spec(shape):
    return pl.BlockSpec(shape, lambda *_: (0,) * len(shape))


_O_RY = D_RNN
_O_Q = 2 * D_RNN
_O_KV = _O_Q + N_HEADS * HEAD_DIM
_O_NG = _O_KV + 6 * KVW
_O_MG = _O_NG + NG_PAD
_W_COLS = _O_MG + 2 * D_MODEL


def _inproj_kernel(x_ref, g_ref, w_ref, rx_ref, gy_ref, zq_ref, zkv_ref, ng_ref, mg_ref, h_ref):
    x = x_ref[...]
    y = x * lax.rsqrt(jnp.mean(x * x, axis=-1, keepdims=True) + EPS) * g_ref[...]
    h_ref[...] = y.astype(BF16)

    def proj(c0, width):
        return _dot(h_ref[...], w_ref[:, c0:c0 + width])

    cw = 256
    tm = x_ref.shape[0]
    for c in range(D_RNN // cw):
        rx = proj(c * cw, cw)
        gy = jax.nn.gelu(proj(_O_RY + c * cw, cw))
        for hf in range(cw // 128):
            rx_ref[c * (cw // 128) + hf] = rx[:, hf * 128:(hf + 1) * 128].reshape(tm // 8, 8, 128)
            gy_ref[c * (cw // 128) + hf] = gy[:, hf * 128:(hf + 1) * 128].reshape(tm // 8, 8, 128)
    for g in range(N_KV):
        zq_ref[g] = proj(_O_Q + g * GW, GW)
    for s in range(6):
        z = proj(_O_KV + s * KVW, KVW)
        for pr in range(KVW // 128):
            zkv_ref[s, pr] = z[:, pr * 128:(pr + 1) * 128]
    ng_ref[...] = jax.nn.sigmoid(proj(_O_NG, NG_PAD))
    for c in range(2 * D_MODEL // cw):
        mg_ref[:, c * cw:(c + 1) * cw] = jax.nn.sigmoid(proj(_O_MG + c * cw, cw)).astype(mg_ref.dtype)


def _inproj(x, gain, w, *, tm=256):
    B, S, D = x.shape
    T = B * S
    ni = S // tm
    nck = D_RNN // 128
    tok = lambda b, i: (b * ni + i, 0)
    slab_spec = pl.BlockSpec((nck, tm // 8, None, 8, 128), lambda b, i: (0, i, b, 0, 0))
    slab_shape = jax.ShapeDtypeStruct((nck, S // 8, B, 8, 128), F32)
    return pl.pallas_call(
        _inproj_kernel,
        grid=(B, ni),
        in_specs=[
            pl.BlockSpec((None, tm, D), lambda b, i: (b, i, 0)),
            _const_spec((1, D)),
            pl.BlockSpec((D, _W_COLS), lambda b, i: (0, 0), pipeline_mode=pl.Buffered(1)),
        ],
        out_specs=[
            slab_spec,
            slab_spec,
            pl.BlockSpec((N_KV, tm, GW), lambda b, i: (0, b * ni + i, 0)),
            pl.BlockSpec((6, KVW // 128, tm, 128), lambda b, i: (0, 0, b * ni + i, 0)),
            pl.BlockSpec((tm, NG_PAD), tok),
            pl.BlockSpec((tm, 2 * D_MODEL), tok),
        ],
        out_shape=[
            slab_shape,
            slab_shape,
            jax.ShapeDtypeStruct((N_KV, T, GW), F32),
            jax.ShapeDtypeStruct((6, KVW // 128, T, 128), F32),
            jax.ShapeDtypeStruct((T, NG_PAD), F32),
            jax.ShapeDtypeStruct((T, 2 * D_MODEL), BF16),
        ],
        scratch_shapes=[pltpu.VMEM((tm, D), BF16)],
        compiler_params=_params(("parallel", "parallel")),
        name="inproj",
    )(x, gain, w)


def _rglru_kernel(x_ref, gy_ref, cw_ref, cb_ref, wa_ref, ba_ref, wi_ref, bi_ref, lam_ref, wo_ref,
                  o_ref, xb_ref, gyb_ref, a_ref, u_ref, hb_ref, hs_ref, yb_ref, *, nb, tt):
    R = nb * tt
    halo = (CONV_W - 1) * nb
    step = pl.program_id(0)
    nck = D_RNN // 128

    @pl.when(step == 0)
    def _():
        xb_ref[0:halo, :] = jnp.zeros((halo, D_RNN), F32)
        hs_ref[...] = jnp.zeros((nb, D_RNN), F32)

    def slab_rows(t8, t_lo):
        return pl.ds(t8 * (8 * nb) + t_lo, nb, stride=8)

    def stage_in(t8, carry):
        for t_lo in range(8):
            r0 = pl.multiple_of((t8 * 8 + t_lo) * nb, nb)
            for ck in range(nck):
                cols = slice(ck * 128, (ck + 1) * 128)
                xb_ref[pl.ds(halo + r0, nb), cols] = x_ref[ck, slab_rows(t8, t_lo), :]
                gyb_ref[pl.ds(r0, nb), cols] = gy_ref[ck, slab_rows(t8, t_lo), :]
        return carry

    lax.fori_loop(0, tt // 8, stage_in, 0)
    xr = cb_ref[...] + cw_ref[0:1, :] * xb_ref[0:R, :]
    for k in range(1, CONV_W):
        xr = xr + cw_ref[k:k + 1, :] * xb_ref[k * nb:k * nb + R, :]
    tail = xb_ref[R:R + halo, :]
    xb_ref[0:halo, :] = tail

    xrb = xr.astype(BF16)
    sp = jax.nn.softplus(-lam_ref[...])
    row = _iota((R, RG_BW), 0)
    first = (row < nb) & (step == 0)
    for n in range(RG_BLOCKS):
        sl = slice(n * RG_BW, (n + 1) * RG_BW)
        xn = xrb[:, sl]
        r = jax.nn.sigmoid(_dot(xn, wa_ref[n]) + ba_ref[:, sl])
        ig = jax.nn.sigmoid(_dot(xn, wi_ref[n]) + bi_ref[:, sl])
        log_a = (-RG_C) * r * sp[:, sl]
        a = jnp.exp(log_a)
        mult = jnp.sqrt(1.0 - a * a)
        mult = jnp.where(first, 1.0, mult)
        a_ref[:, sl] = a
        u_ref[:, sl] = mult * ig * xr[:, sl]

    def body(t, h):
        r0 = pl.multiple_of(t * nb, nb)
        h = a_ref[pl.ds(r0, nb), :] * h + u_ref[pl.ds(r0, nb), :]
        hb_ref[pl.ds(r0, nb), :] = h
        return h

    h = lax.fori_loop(0, tt, body, hs_ref[...], unroll=8)
    hs_ref[...] = h
    yb_ref[...] = _dot((hb_ref[...] * gyb_ref[...]).astype(BF16), wo_ref[...])

    def stage_out(t8, carry):
        for t_lo in range(8):
            r0 = pl.multiple_of((t8 * 8 + t_lo) * nb, nb)
            for ck in range(nck):
                o_ref[ck, slab_rows(t8, t_lo), :] = yb_ref[pl.ds(r0, nb), ck * 128:(ck + 1) * 128]
        return carry

    lax.fori_loop(0, tt // 8, stage_out, 0)


def _rglru(rx, gy, conv_w, conv_b, wa, ba, wi, bi, lam, wo, *, nb, tt=32):
    nck, rows, _ = rx.shape
    S = rows // nb
    assert S % tt == 0 and tt % 8 == 0
    R = nb * tt
    halo = (CONV_W - 1) * nb
    return pl.pallas_call(
        functools.partial(_rglru_kernel, nb=nb, tt=tt),
        grid=(S // tt,),
        in_specs=[
            pl.BlockSpec((nck, R, 128), lambda s: (0, s, 0)),
            pl.BlockSpec((nck, R, 128), lambda s: (0, s, 0)),
            _const_spec((CONV_W, D_RNN)),
            _const_spec((1, D_RNN)),
            _const_spec((RG_BLOCKS, RG_BW, RG_BW)),
            _const_spec((1, D_RNN)),
            _const_spec((RG_BLOCKS, RG_BW, RG_BW)),
            _const_spec((1, D_RNN)),
            _const_spec((1, D_RNN)),
            _const_spec((D_RNN, D_MODEL)),
        ],
        out_specs=pl.BlockSpec((D_MODEL // 128, R, 128), lambda s: (0, s, 0)),
        out_shape=jax.ShapeDtypeStruct((D_MODEL // 128, rows, 128), F32),
        scratch_shapes=[
            pltpu.VMEM((R + halo, D_RNN), F32),
            pltpu.VMEM((R, D_RNN), F32),
            pltpu.VMEM((R, D_RNN), F32),
            pltpu.VMEM((R, D_RNN), F32),
            pltpu.VMEM((R, D_RNN), F32),
            pltpu.VMEM((nb, D_RNN), F32),
            pltpu.VMEM((R, D_MODEL), F32),
        ],
        compiler_params=_params(("arbitrary",)),
        name="rglru",
    )(rx, gy, conv_w, conv_b, wa, ba, wi, bi, lam, wo)


def _kvprep_kernel(zk_ref, zv_ref, g_ref, kn_ref, vt_ref):
    rows = zk_ref.shape[1]
    x = jnp.concatenate([zk_ref[0], zk_ref[1]], axis=1)
    seg = jnp.where(_iota((KVW, KVW), 0) // HEAD_DIM == _iota((KVW, KVW), 1) // HEAD_DIM, 1.0, 0.0).astype(BF16)
    hi, lo = _split(x * x)
    ssq = _dot(jnp.concatenate([hi, lo], axis=0), seg)
    kn = (x * lax.rsqrt((ssq[0:rows] + ssq[rows:2 * rows]) * (1.0 / HEAD_DIM) + EPS) * g_ref[...]).astype(BF16)
    for pr in range(KVW // 128):
        kn_ref[pr] = kn[:, pr * 128:(pr + 1) * 128]
    eye = jnp.where(_iota((KVW, KVW), 0) == _iota((KVW, KVW), 1), 1.0, 0.0).astype(BF16)
    pad = jnp.where(_iota((VROWS - HEAD_DIM, KCH), 0) == 0, 1.0, 0.0).astype(BF16)
    for c in range(rows // KCH):
        sl = slice(c * KCH, (c + 1) * KCH)
        v = jnp.concatenate([zv_ref[0, sl, :], zv_ref[1, sl, :]], axis=1).astype(BF16)
        vt = _dot_nt(eye, v).astype(BF16)
        for g in range(N_KV):
            vt_ref[c, g, 0:HEAD_DIM, :] = vt[g * HEAD_DIM:(g + 1) * HEAD_DIM]
            vt_ref[c, g, HEAD_DIM:VROWS, :] = pad


def _kvprep(zkv, gains, *, B, S, cpb=4):
    nch = S // KCH
    assert nch % cpb == 0
    nblk = nch // cpb
    rows = cpb * KCH
    return pl.pallas_call(
        _kvprep_kernel,
        grid=(2, B, nblk),
        in_specs=[
            pl.BlockSpec((None, KVW // 128, rows, 128), lambda w, b, i: (2 + 2 * w, 0, b * nblk + i, 0)),
            pl.BlockSpec((None, KVW // 128, rows, 128), lambda w, b, i: (3 + 2 * w, 0, b * nblk + i, 0)),
            pl.BlockSpec((None, 1, KVW), lambda w, b, i: (w, 0, 0)),
        ],
        out_specs=[
            pl.BlockSpec((None, None, KVW // 128, rows, 128), lambda w, b, i: (w, b, 0, i, 0)),
            pl.BlockSpec((None, None, cpb, N_KV, VROWS, KCH), lambda w, b, i: (w, b, i, 0, 0, 0)),
        ],
        out_shape=[
            jax.ShapeDtypeStruct((2, B, KVW // 128, S, 128), BF16),
            jax.ShapeDtypeStruct((2, B, nch, N_KV, VROWS, KCH), BF16),
        ],
        compiler_params=_params(("parallel", "parallel", "parallel")),
        name="kvprep",
    )(zkv, zkv, gains)


def _compress_kernel(zk_ref, zv_ref, pos_ref, w1_ref, w2_ref, g_ref, kc_ref, vct_ref):
    nch = zk_ref.shape[0] // CMP_STRIDE
    lane = _iota((nch, 128), 1)
    npair = CMP_STRIDE // 2
    eye = jnp.where(_iota((HEAD_DIM, HEAD_DIM), 0) == _iota((HEAD_DIM, HEAD_DIM), 1), 1.0, 0.0).astype(BF16)
    for kind, z_ref in enumerate((zk_ref, zv_ref)):
        for ge in range(2):
            combs = []
            for p in range(npair):
                a = z_ref[pl.ds(2 * p, nch, stride=CMP_STRIDE), :]
                b = z_ref[pl.ds(2 * p + 1, nch, stride=CMP_STRIDE), :]
                if ge == 0:
                    combs.append(jnp.where(lane < HEAD_DIM, a, pltpu.roll(b, HEAD_DIM, axis=1)))
                else:
                    combs.append(jnp.where(lane < HEAD_DIM, pltpu.roll(a, HEAD_DIM, axis=1), b))
            halves = []
            half_w = npair * 128
            for half in range(2):
                lhs = jnp.concatenate(
                    [(combs[p] + pos_ref[kind, half * npair + p:half * npair + p + 1, :]).astype(BF16)
                     for p in range(npair)], axis=1)
                halves.append(_dot(lhs, w1_ref[kind, half * half_w:(half + 1) * half_w, :]))
            hid = halves[0] + pltpu.roll(halves[1], nch - 1, axis=0)
            hid = jax.nn.gelu(hid)
            out = _dot(hid.astype(BF16), w2_ref[kind])
            if kind == 0:
                out = out * lax.rsqrt(jnp.mean(out * out, axis=-1, keepdims=True) + EPS) * g_ref[...]
            res = jnp.where(_iota(out.shape, 0) < nch - 1, out, 0.0).astype(BF16)
            if kind == 0:
                kc_ref[ge] = res
            else:
                vct_ref[ge] = _dot_nt(eye, res).astype(BF16)


def _compress(zkv, pos, w1, w2, gain, *, B, S):
    nch = S // CMP_STRIDE
    return pl.pallas_call(
        _compress_kernel,
        grid=(B, KVW // 128),
        in_specs=[
            pl.BlockSpec((None, None, S, 128), lambda b, lp: (0, lp, b, 0)),
            pl.BlockSpec((None, None, S, 128), lambda b, lp: (1, lp, b, 0)),
            _const_spec((2, CMP_STRIDE, 128)),
            _const_spec((2, CMP_LEN * HEAD_DIM, CMP_HID)),
            _const_spec((2, CMP_HID, HEAD_DIM)),
            _const_spec((1, HEAD_DIM)),
        ],
        out_specs=[
            pl.BlockSpec((None, 2, nch, HEAD_DIM), lambda b, lp: (b, lp, 0, 0)),
            pl.BlockSpec((None, 2, HEAD_DIM, nch), lambda b, lp: (b, lp, 0, 0)),
        ],
        out_shape=[
            jax.ShapeDtypeStruct((B, N_KV, nch, HEAD_DIM), BF16),
            jax.ShapeDtypeStruct((B, N_KV, HEAD_DIM, nch), BF16),
        ],
        compiler_params=_params(("parallel", "parallel")),
        name="compress",
    )(zkv, zkv, pos, w1, w2, gain)


def _spread_heads(x):
    y = x + pltpu.roll(x, 2 * Q_BLK, axis=1)
    return y + pltpu.roll(y, Q_BLK, axis=1)


def _swap_heads(x):
    n = x.shape[0]
    lane = _iota((n, 128), 1)
    halves = []
    for pr in range(x.shape[1] // 128):
        slab = jnp.concatenate([x[:, pr * 128:(pr + 1) * 128], jnp.zeros((128 - n, 128), x.dtype)], axis=0)
        tr = slab.T
        halves.append(jnp.where(lane < n, tr[0:n], pltpu.roll(tr[n:2 * n], n, axis=1)))
    return jnp.concatenate(halves, axis=1)


def _nsa_kernel(*refs, nsel, qpb):
    per_chunk = KCH // SEL_LEN
    for k in range(qpb):
        _nsa_block(k, "head", None, *refs, nsel=nsel, qpb=qpb)

    def step(i, carry):
        for k in range(qpb):
            _nsa_block(k, "step", i, *refs, nsel=nsel, qpb=qpb)
        return carry

    common = pl.program_id(1) * (qpb // per_chunk)
    lax.fori_loop(0, common, step, 0)
    for k in range(qpb):
        for extra in range(k // per_chunk):
            _nsa_block(k, "step", common + extra, *refs, nsel=nsel, qpb=qpb)
    for k in range(qpb):
        _nsa_block(k, "tail", None, *refs, nsel=nsel, qpb=qpb)


def _nsa_block(k, phase, step_i, zq_ref, qg_ref, kc_ref, vct_ref, ks_ref, kw_ref, vst_ref, vwt_ref, ng_ref, o_ref,
               qtp_ref, oc_ref, ow_ref, bias_ref, ms_ref, accs_ref, mw_ref, accw_ref, sbuf_ref, wbuf_ref,
               gates_ref, scs_ref, wbias_ref, *, nsel, qpb):
    qtp_ref, oc_ref, ow_ref, bias_ref, ms_ref, accs_ref, sbuf_ref, gates_ref = (
        r.at[k] for r in (qtp_ref, oc_ref, ow_ref, bias_ref, ms_ref, accs_ref, sbuf_ref, gates_ref))
    qi = pl.program_id(1) * qpb + k
    q_rows = pl.ds(k * Q_BLK, Q_BLK)
    s0 = qi * Q_BLK
    cur = qi
    ncmp = kc_ref.shape[1]
    per_chunk = KCH // SEL_LEN
    groups = range(N_KV)
    n_chunks = qi // per_chunk + 1
    cur_slab = k % per_chunk
    cut_rows = slice(cur_slab * SEL_LEN, (cur_slab + 1) * SEL_LEN)
    last_slabs = per_chunk // 2 if cur_slab < per_chunk // 2 else per_chunk

    def scores(k_ref, i):
        k0 = pl.multiple_of(i * KCH, KCH)
        return [_dot(k_ref[g // 2, pl.ds(k0, KCH), :], qtp_ref[g]) for g in groups]

    def park(buf_ref, vals):
        for g in groups:
            buf_ref[g] = vals[g]

    def cut(buf_ref, tri):
        for g in groups:
            buf_ref[g, cut_rows, :] = buf_ref[g, cut_rows, :] + tri

    def update(g, blocks, shifts, vt, m_ref, acc_ref):
        m = m_ref[g]
        part = None
        for blk, sh in zip(blocks, shifts):
            top8 = jnp.max(blk.reshape(SEL_LEN // 8, 8, GW), axis=0) + sh
            part = top8 if part is None else jnp.maximum(part, top8)
        mnew = jnp.maximum(m, jnp.max(part, axis=0, keepdims=True))
        e = jnp.concatenate([jnp.exp2(blk + (sh - mnew)) for blk, sh in zip(blocks, shifts)], axis=0).astype(BF16)
        acc_ref[g] = jnp.exp2(m - mnew) * acc_ref[g] + _dot(vt, e)
        m_ref[g] = mnew

    def slabs(buf_ref, g, ns):
        return [buf_ref[g, nb * SEL_LEN:(nb + 1) * SEL_LEN, :] for nb in range(ns)]

    def sel_softmax(i, ns=per_chunk):
        for g in groups:
            shifts = [bias_ref[g, pl.ds(i * per_chunk + nb, 1), :] for nb in range(ns)]
            update(g, slabs(sbuf_ref, g, ns), shifts, vst_ref[i, g, :, 0:ns * SEL_LEN], ms_ref, accs_ref)

    key_row = _iota((SEL_LEN, GW), 0)
    in_blk_t = _iota((SEL_LEN, GW), 1) % Q_BLK
    causal_tri = jnp.where(key_row <= in_blk_t, 0.0, NEG)

    if phase == "step":
        nxt = scores(ks_ref, step_i + 1)
        sel_softmax(step_i)
        park(sbuf_ref, nxt)
        return

    if phase == "tail":
        cut(sbuf_ref, causal_tri)
        sel_softmax(n_chunks - 1, last_slabs)
        outs = []
        for g in groups:
            o_s = accs_ref[g, 0:HEAD_DIM, :] * (1.0 / jnp.maximum(accs_ref[g, HEAD_DIM:HEAD_DIM + 1, :], 1e-30))
            gates = gates_ref[g]
            o_t = gates[0:1] * oc_ref[g] + gates[1:2] * o_s + gates[2:3] * ow_ref[g]
            outs.append(_swap_heads(o_t))
        for g in groups:
            o_ref[g, q_rows, :] = outs[g].astype(o_ref.dtype)
        return

    lane_t = s0 + (_iota((1, GW), 1) % Q_BLK)
    lane_grp = _iota((1, GW), 1) // Q_BLK
    ov_n = _iota((nsel, ncmp), 0) * SEL_LEN
    ov_c = _iota((nsel, ncmp), 1) * CMP_STRIDE
    overlap = jnp.where((ov_c < ov_n + SEL_LEN) & (ov_c + CMP_LEN > ov_n), 1.0, 0.0).astype(BF16)


    ok_c = _iota((ncmp, GW), 0) * CMP_STRIDE + (CMP_LEN - 1) <= lane_t
    ngt = jnp.concatenate([ng_ref[q_rows, :], jnp.zeros((128 - Q_BLK, NG_PAD), F32)], axis=0).T
    ngt_hi = pltpu.roll(ngt, Q_BLK, axis=1)
    low_half = _iota((1, 128), 1) < Q_BLK
    for g in groups:
        for c in range(3):
            r = c * N_HEADS + g * HPG
            pairs = [jnp.where(low_half, ngt[r + j:r + j + 1], ngt_hi[r + j + 1:r + j + 2]) for j in (0, 2)]
            gates_ref[g, c:c + 1, :] = jnp.concatenate(pairs, axis=1)

    qts = []
    for g in groups:
        xt = _swap_heads(zq_ref[g, q_rows, :])
        ssq = jnp.sum(xt * xt, axis=0, keepdims=True)
        qts.append((xt * lax.rsqrt(ssq * (1.0 / HEAD_DIM) + EPS) * qg_ref[...]).astype(BF16))
    scs = []
    for g in groups:
        off = (g % 2) * HEAD_DIM
        qtp_ref[g, off:off + HEAD_DIM, :] = qts[g]
        qtp_ref[g, HEAD_DIM - off:2 * HEAD_DIM - off, :] = jnp.zeros((HEAD_DIM, GW), BF16)
        scs.append(_dot(kc_ref[g], qts[g]))

    n_win = WINDOW // KCH + 1
    win_ids = [jnp.maximum(n_chunks - (n_win - j), 0) for j in range(n_win)]
    win_dead = [jnp.where(n_chunks >= n_win - j, 0.0, NEG) for j in range(n_win)]
    park(scs_ref, scs)
    park(sbuf_ref, scores(ks_ref, 0))
    park(wbuf_ref, scores(kw_ref, win_ids[0]))

    expired_tri = jnp.where(key_row > in_blk_t, 0.0, NEG)
    win_blocks = WINDOW // SEL_LEN
    n_all = _iota((nsel, GW), 0)
    wbias_ref[...] = jnp.where((n_all >= cur - win_blocks) & (n_all <= cur), 0.0, NEG)
    cut(wbuf_ref, jnp.where(cur >= win_blocks, expired_tri, 0.0))

    ms_ref[...] = jnp.full(ms_ref.shape, NEG, F32)
    mw_ref[...] = jnp.full(mw_ref.shape, NEG, F32)
    accs_ref[...] = jnp.zeros(accs_ref.shape, F32)
    accw_ref[...] = jnp.zeros(accw_ref.shape, F32)

    def win_softmax(i, dead, ns=per_chunk):
        shifts = [wbias_ref[pl.ds(i * per_chunk + nb, 1), :] + dead for nb in range(ns)]
        for g in groups:
            update(g, slabs(wbuf_ref, g, ns), shifts, vwt_ref[i, g, :, 0:ns * SEL_LEN], mw_ref, accw_ref)

    for j in range(n_win - 1):
        nxt = scores(kw_ref, win_ids[j + 1])
        win_softmax(win_ids[j], win_dead[j])
        park(wbuf_ref, nxt)

    imp_all = jnp.zeros((nsel, GW), F32)
    some_c = lane_t >= CMP_LEN - 1
    for g in groups:
        scm = jnp.where(ok_c, scs_ref[g], NEG)
        ec = jnp.exp2(scm - jnp.max(scm, axis=0, keepdims=True))
        inv = jnp.where(some_c, 1.0 / jnp.maximum(jnp.sum(ec, axis=0, keepdims=True), 1e-30), 0.0)
        both = _dot(jnp.concatenate([vct_ref[g], overlap], axis=0), ec.astype(BF16)) * inv
        oc_ref[g] = both[0:HEAD_DIM]
        imp = _spread_heads(both[HEAD_DIM:HEAD_DIM + nsel])
        imp_all = jnp.where(lane_grp == g, imp, imp_all)

    n_io = _iota((nsel, GW), 0)
    forced = (n_io == 0) | (n_io == cur) | (n_io == cur - 1)
    valid = n_io <= cur
    score = jnp.where(forced, FORCE_SCORE, imp_all)
    score = jnp.where(valid, score, -1.0)
    rows8 = [score[r:r + 8] for r in range(0, nsel, 8)]
    ranks = [jnp.zeros((8, GW), F32) for _ in rows8]
    sub = _iota((8, GW), 0)
    for m in range(nsel):
        rowv = score[m:m + 1, :]
        for k, blk in enumerate(rows8):
            ge = jnp.where(rowv >= blk, 1.0, 0.0)
            gt = jnp.where(rowv > blk, 1.0, 0.0)
            if 8 * k > m:
                inc = ge
            elif 8 * k + 7 < m:
                inc = gt
            else:
                inc = jnp.where(sub + 8 * k > m, ge, gt)
            ranks[k] = ranks[k] + inc
    rank = jnp.concatenate(ranks, axis=0)
    sel_all = jnp.where((rank < float(min(SEL_TOPN, nsel))) & valid, 1.0, 0.0)
    for g in groups:
        mine = jnp.where(lane_grp == g, sel_all, 0.0)
        bias_ref[g] = (_spread_heads(mine) - 1.0) * (-NEG)

    cut(wbuf_ref, causal_tri)
    win_softmax(win_ids[n_win - 1], win_dead[n_win - 1], last_slabs)
    for g in groups:
        ow_ref[g] = accw_ref[g, 0:HEAD_DIM, :] * (1.0 / jnp.maximum(accw_ref[g, HEAD_DIM:HEAD_DIM + 1, :], 1e-30))


def _nsa(zq, qgain, kc, vct, kn, vt, ng, *, B, S):
    T = B * S
    qpb = 2 * (KCH // SEL_LEN)
    assert S % (Q_BLK * qpb) == 0
    nq = S // (Q_BLK * qpb)
    nch = S // KCH
    ncmp = S // CMP_STRIDE
    nsel = S // SEL_LEN
    tok = lambda b, i: (0, b * nq + i, 0)
    return pl.pallas_call(
        functools.partial(_nsa_kernel, nsel=nsel, qpb=qpb),
        grid=(B, nq),
        in_specs=[
            pl.BlockSpec((N_KV, Q_BLK * qpb, GW), tok),
            _const_spec((HEAD_DIM, GW)),
            pl.BlockSpec((None, N_KV, ncmp, HEAD_DIM), lambda b, i: (b, 0, 0, 0)),
            pl.BlockSpec((None, N_KV, HEAD_DIM, ncmp), lambda b, i: (b, 0, 0, 0)),
            pl.BlockSpec((None, None, 2, S, 128), lambda b, i: (0, b, 0, 0, 0)),
            pl.BlockSpec((None, None, 2, S, 128), lambda b, i: (1, b, 0, 0, 0)),
            pl.BlockSpec((None, None, nch, N_KV, VROWS, KCH), lambda b, i: (0, b, 0, 0, 0, 0)),
            pl.BlockSpec((None, None, nch, N_KV, VROWS, KCH), lambda b, i: (1, b, 0, 0, 0, 0)),
            pl.BlockSpec((Q_BLK * qpb, NG_PAD), lambda b, i: (b * nq + i, 0)),
        ],
        out_specs=pl.BlockSpec((N_KV, Q_BLK * qpb, GW), tok),
        out_shape=jax.ShapeDtypeStruct((N_KV, T, GW), BF16),
        scratch_shapes=[
            pltpu.VMEM((qpb, N_KV, 2 * HEAD_DIM, GW), BF16),
            pltpu.VMEM((qpb, N_KV, HEAD_DIM, GW), F32),
            pltpu.VMEM((qpb, N_KV, HEAD_DIM, GW), F32),
            pltpu.VMEM((qpb, N_KV, nsel, GW), F32),
            pltpu.VMEM((qpb, N_KV, 1, GW), F32),
            pltpu.VMEM((qpb, N_KV, VROWS, GW), F32),
            pltpu.VMEM((N_KV, 1, GW), F32),
            pltpu.VMEM((N_KV, VROWS, GW), F32),
            pltpu.VMEM((qpb, N_KV, KCH, GW), F32),
            pltpu.VMEM((N_KV, KCH, GW), F32),
            pltpu.VMEM((qpb, N_KV, 8, GW), F32),
            pltpu.VMEM((N_KV, ncmp, GW), F32),
            pltpu.VMEM((nsel, GW), F32),
        ],
        compiler_params=_params(("parallel", "arbitrary")),
        name="nsa",
    )(zq, qgain, kc, vct, kn, kn, vt, vt, ng)


def _merge_kernel(x_ref, ya_ref, o_ref_in, mg_ref, wn_ref, wo_ref, out_ref):
    yb = _dot(o_ref_in[0], wn_ref[0:GW, :])
    for g in range(1, N_KV):
        yb = yb + _dot(o_ref_in[g], wn_ref[g * GW:(g + 1) * GW, :])
    tm = x_ref.shape[0]
    ya = jnp.concatenate([ya_ref[ck].reshape(tm, 128) for ck in range(D_MODEL // 128)], axis=1)
    mixed = (mg_ref[:, 0:D_MODEL].astype(F32) * ya
             + mg_ref[:, D_MODEL:2 * D_MODEL].astype(F32) * yb)
    out_ref[...] = x_ref[...] + _dot(mixed.astype(BF16), wo_ref[...])


def _merge(x2d, ya, o, mg, wn, wo, *, B, S, tm=512):
    ni = S // tm
    tok = lambda b, i: (b * ni + i, 0)
    return pl.pallas_call(
        _merge_kernel,
        grid=(B, ni),
        in_specs=[
            pl.BlockSpec((tm, D_MODEL), tok),
            pl.BlockSpec((D_MODEL // 128, tm // 8, None, 8, 128), lambda b, i: (0, i, b, 0, 0)),
            pl.BlockSpec((N_KV, tm, GW), lambda b, i: (0, b * ni + i, 0)),
            pl.BlockSpec((tm, 2 * D_MODEL), tok),
            _const_spec((N_HEADS * HEAD_DIM, D_MODEL)),
            _const_spec((D_MODEL, D_MODEL)),
        ],
        out_specs=pl.BlockSpec((tm, D_MODEL), tok),
        out_shape=jax.ShapeDtypeStruct((B * S, D_MODEL), F32),
        compiler_params=_params(("parallel", "parallel")),
        name="merge",
    )(x2d, ya, o, mg, wn, wo)


def _ffn_kernel(x_ref, g_ref, wg_ref, wu_ref, wd_ref, o_ref, h_ref, *, tf):
    x = x_ref[...]
    y = x * lax.rsqrt(jnp.mean(x * x, axis=-1, keepdims=True) + EPS) * g_ref[...]
    h_ref[...] = y.astype(BF16)
    o_ref[...] = x
    for j in range(D_FF // tf):
        h = h_ref[...]
        sl = slice(j * tf, (j + 1) * tf)
        act = jax.nn.silu(_dot(h, wg_ref[:, sl])) * _dot(h, wu_ref[:, sl])
        o_ref[...] += _dot(act.astype(BF16), wd_ref[sl, :])


def _ffn(x2d, gain, wg, wu, wd, *, tm=512, tf=256):
    T = x2d.shape[0]
    assert T % tm == 0 and D_FF % tf == 0
    once = dict(pipeline_mode=pl.Buffered(1))
    return pl.pallas_call(
        functools.partial(_ffn_kernel, tf=tf),
        grid=(T // tm,),
        in_specs=[
            pl.BlockSpec((tm, D_MODEL), lambda i: (i, 0)),
            _const_spec((1, D_MODEL)),
            pl.BlockSpec((D_MODEL, D_FF), lambda i: (0, 0), **once),
            pl.BlockSpec((D_MODEL, D_FF), lambda i: (0, 0), **once),
            pl.BlockSpec((D_FF, D_MODEL), lambda i: (0, 0), **once),
        ],
        out_specs=pl.BlockSpec((tm, D_MODEL), lambda i: (i, 0)),
        out_shape=jax.ShapeDtypeStruct((T, D_MODEL), F32),
        scratch_shapes=[pltpu.VMEM((tm, D_MODEL), BF16)],
        compiler_params=_params(("parallel",)),
        name="ffn",
    )(x2d, gain, wg, wu, wd)


def _layer(x, p):
    B, S, _ = x.shape
    T = B * S
    dh = HEAD_DIM
    row = lambda v: v.reshape(1, -1)
    w_in = p["w_in"]
    n_ng = 3 * N_HEADS
    o_ng = 2 * D_RNN + N_HEADS * dh + 6 * KVW
    w_packed = jnp.concatenate(
        [w_in[:, :o_ng], jnp.pad(w_in[:, o_ng:o_ng + n_ng], ((0, 0), (0, NG_PAD - n_ng))), w_in[:, o_ng + n_ng:]],
        axis=1).astype(BF16)
    rx, gy, zq, zkv, ng, mg = _inproj(x, row(p["norm1"]), w_packed)

    slabs = lambda a: a.reshape(a.shape[0], S * B, 128)
    ya = _rglru(slabs(rx), slabs(gy), p["conv_w"], row(p["conv_b"]),
                p["rg_wa"].astype(BF16), row(p["rg_ba"]), p["rg_wi"].astype(BF16), row(p["rg_bi"]),
                row(p["rg_lambda"]), p["w_rg_out"].astype(BF16), nb=B)
    ya = ya.reshape(D_MODEL // 128, S // 8, B, 8, 128)

    kgains = jnp.tile(p["k_norm"][1:3], (1, N_KV)).reshape(2, 1, KVW)
    kn, vt = _kvprep(zkv, kgains, B=B, S=S)
    pos = jnp.stack([p["cmp_pos_k"], p["cmp_pos_v"]]).reshape(2, CMP_STRIDE, 2 * dh)
    w1 = jnp.stack([p["cmp_k_w1"], p["cmp_v_w1"]]).astype(BF16)
    w2 = jnp.stack([p["cmp_k_w2"], p["cmp_v_w2"]]).astype(BF16)
    kc, vct = _compress(zkv, pos, w1, w2, row(p["k_norm"][0]), B=B, S=S)
    qgain = jnp.broadcast_to((p["q_norm"] * (dh ** -0.5 * LOG2E))[:, None], (dh, GW))
    o = _nsa(zq, qgain, kc, vct, kn, vt, ng, B=B, S=S)

    x1 = _merge(x.reshape(T, D_MODEL), ya, o, mg, p["w_nsa_out"].astype(BF16), p["w_o"].astype(BF16), B=B, S=S)
    x2 = _ffn(x1, row(p["norm2"]), p["w_gate"].astype(BF16), p["w_up"].astype(BF16), p["w_down"].astype(BF16))
    return x2.reshape(B, S, D_MODEL)


def kernel(x, norm1, w_in, conv_w, conv_b, rg_wa, rg_ba, rg_wi, rg_bi, rg_lambda, q_norm, k_norm, cmp_pos_k,
           cmp_pos_v, cmp_k_w1, cmp_k_w2, cmp_v_w1, cmp_v_w2, w_rg_out, w_nsa_out, w_o, norm2, w_gate, w_up, w_down):
    params = dict(norm1=norm1, w_in=w_in, conv_w=conv_w, conv_b=conv_b, rg_wa=rg_wa, rg_ba=rg_ba, rg_wi=rg_wi,
                  rg_bi=rg_bi, rg_lambda=rg_lambda, q_norm=q_norm, k_norm=k_norm, cmp_pos_k=cmp_pos_k,
                  cmp_pos_v=cmp_pos_v, cmp_k_w1=cmp_k_w1, cmp_k_w2=cmp_k_w2, cmp_v_w1=cmp_v_w1, cmp_v_w2=cmp_v_w2,
                  w_rg_out=w_rg_out, w_nsa_out=w_nsa_out, w_o=w_o, norm2=norm2, w_gate=w_gate, w_up=w_up,
                  w_down=w_down)
    for l in range(norm1.shape[0]):
        x = _layer(x, {k: v[l] for k, v in params.items()})
    return x
```

```python
import functools

import jax
import jax.numpy as jnp
from jax import lax
from jax.experimental import pallas as pl
from jax.experimental.pallas import tpu as pltpu

D_MODEL = 1024
D_RNN = 1024
RG_BLOCKS = 4
RG_BW = D_RNN // RG_BLOCKS
CONV_W = 4
RG_C = 8.0
N_HEADS = 16
N_KV = 4
HEAD_DIM = 64
HPG = N_HEADS // N_KV
CMP_LEN = 32
CMP_STRIDE = 16
CMP_HID = 256
SEL_LEN = 64
SEL_TOPN = 16
WINDOW = 512
Q_BLK = 64
D_FF = 2816
EPS = 1e-6
FORCE_SCORE = 1e6
NEG = -1e30
LOG2E = 1.4426950408889634

KCH = 256
GW = HPG * HEAD_DIM
KVW = N_KV * HEAD_DIM
VROWS = 80
NG_PAD = 128

F32 = jnp.float32
BF16 = jnp.bfloat16

VMEM_LIMIT = 56 * 1024 * 1024


def _params(sem):
    return pltpu.CompilerParams(dimension_semantics=sem, vmem_limit_bytes=VMEM_LIMIT)


def _dot(a, b):
    return jnp.dot(a, b, preferred_element_type=F32)


def _dot_nt(a, b):
    return lax.dot_general(a, b, (((1,), (1,)), ((), ())), preferred_element_type=F32)


def _split(x):
    hi = x.astype(BF16)
    lo = (x - hi.astype(F32)).astype(BF16)
    return hi, lo


def _iota(shape, dim):
    return lax.broadcasted_iota(jnp.int32, shape, dim)


def _const_spec(shape):
    return pl.BlockSpec(shape, lambda *_: (0,) * len(shape))


_O_RY = D_RNN
_O_Q = 2 * D_RNN
_O_KV = _O_Q + N_HEADS * HEAD_DIM
_O_NG = _O_KV + 6 * KVW
_O_MG = _O_NG + NG_PAD
_W_COLS = _O_MG + 2 * D_MODEL


def _inproj_kernel(x_ref, g_ref, w_ref, rx_ref, gy_ref, zq_ref, zkv_ref, ng_ref, mg_ref, h_ref):
    x = x_ref[...]
    y = x * lax.rsqrt(jnp.mean(x * x, axis=-1, keepdims=True) + EPS) * g_ref[...]
    h_ref[...] = y.astype(BF16)

    def proj(c0, width):
        return _dot(h_ref[...], w_ref[:, c0:c0 + width])

    cw = 256
    tm = x_ref.shape[0]
    for c in range(D_RNN // cw):
        rx = proj(c * cw, cw)
        gy = jax.nn.gelu(proj(_O_RY + c * cw, cw))
        for hf in range(cw // 128):
            rx_ref[c * (cw // 128) + hf] = rx[:, hf * 128:(hf + 1) * 128].reshape(tm // 8, 8, 128)
            gy_ref[c * (cw // 128) + hf] = gy[:, hf * 128:(hf + 1) * 128].reshape(tm // 8, 8, 128)
    for g in range(N_KV):
        zq_ref[g] = proj(_O_Q + g * GW, GW)
    for s in range(6):
        z = proj(_O_KV + s * KVW, KVW)
        for pr in range(KVW // 128):
            zkv_ref[s, pr] = z[:, pr * 128:(pr + 1) * 128]
    ng_ref[...] = jax.nn.sigmoid(proj(_O_NG, NG_PAD))
    for c in range(2 * D_MODEL // cw):
        mg_ref[:, c * cw:(c + 1) * cw] = jax.nn.sigmoid(proj(_O_MG + c * cw, cw)).astype(mg_ref.dtype)


def _inproj(x, gain, w, *, tm=256):
    B, S, D = x.shape
    T = B * S
    ni = S // tm
    nck = D_RNN // 128
    tok = lambda b, i: (b * ni + i, 0)
    slab_spec = pl.BlockSpec((nck, tm // 8, None, 8, 128), lambda b, i: (0, i, b, 0, 0))
    slab_shape = jax.ShapeDtypeStruct((nck, S // 8, B, 8, 128), F32)
    return pl.pallas_call(
        _inproj_kernel,
        grid=(B, ni),
        in_specs=[
            pl.BlockSpec((None, tm, D), lambda b, i: (b, i, 0)),
            _const_spec((1, D)),
            pl.BlockSpec((D, _W_COLS), lambda b, i: (0, 0), pipeline_mode=pl.Buffered(1)),
        ],
        out_specs=[
            slab_spec,
            slab_spec,
            pl.BlockSpec((N_KV, tm, GW), lambda b, i: (0, b * ni + i, 0)),
            pl.BlockSpec((6, KVW // 128, tm, 128), lambda b, i: (0, 0, b * ni + i, 0)),
            pl.BlockSpec((tm, NG_PAD), tok),
            pl.BlockSpec((tm, 2 * D_MODEL), tok),
        ],
        out_shape=[
            slab_shape,
            slab_shape,
            jax.ShapeDtypeStruct((N_KV, T, GW), F32),
            jax.ShapeDtypeStruct((6, KVW // 128, T, 128), F32),
            jax.ShapeDtypeStruct((T, NG_PAD), F32),
            jax.ShapeDtypeStruct((T, 2 * D_MODEL), BF16),
        ],
        scratch_shapes=[pltpu.VMEM((tm, D), BF16)],
        compiler_params=_params(("parallel", "parallel")),
        name="inproj",
    )(x, gain, w)


def _rglru_kernel(x_ref, gy_ref, cw_ref, cb_ref, wa_ref, ba_ref, wi_ref, bi_ref, lam_ref, wo_ref,
                  o_ref, xb_ref, gyb_ref, a_ref, u_ref, hb_ref, hs_ref, yb_ref, *, nb, tt):
    R = nb * tt
    halo = (CONV_W - 1) * nb
    step = pl.program_id(0)
    nck = D_RNN // 128

    @pl.when(step == 0)
    def _():
        xb_ref[0:halo, :] = jnp.zeros((halo, D_RNN), F32)
        hs_ref[...] = jnp.zeros((nb, D_RNN), F32)

    def slab_rows(t8, t_lo):
        return pl.ds(t8 * (8 * nb) + t_lo, nb, stride=8)

    def stage_in(t8, carry):
        for t_lo in range(8):
            r0 = pl.multiple_of((t8 * 8 + t_lo) * nb, nb)
            for ck in range(nck):
                cols = slice(ck * 128, (ck + 1) * 128)
                xb_ref[pl.ds(halo + r0, nb), cols] = x_ref[ck, slab_rows(t8, t_lo), :]
                gyb_ref[pl.ds(r0, nb), cols] = gy_ref[ck, slab_rows(t8, t_lo), :]
        return carry

    lax.fori_loop(0, tt // 8, stage_in, 0)
    xr = cb_ref[...] + cw_ref[0:1, :] * xb_ref[0:R, :]
    for k in range(1, CONV_W):
        xr = xr + cw_ref[k:k + 1, :] * xb_ref[k * nb:k * nb + R, :]
    tail = xb_ref[R:R + halo, :]
    xb_ref[0:halo, :] = tail

    xrb = xr.astype(BF16)
    sp = jax.nn.softplus(-lam_ref[...])
    row = _iota((R, RG_BW), 0)
    first = (row < nb) & (step == 0)
    for n in range(RG_BLOCKS):
        sl = slice(n * RG_BW, (n + 1) * RG_BW)
        xn = xrb[:, sl]
        r = jax.nn.sigmoid(_dot(xn, wa_ref[n]) + ba_ref[:, sl])
        ig = jax.nn.sigmoid(_dot(xn, wi_ref[n]) + bi_ref[:, sl])
        log_a = (-RG_C) * r * sp[:, sl]
        a = jnp.exp(log_a)
        mult = jnp.sqrt(1.0 - a * a)
        mult = jnp.where(first, 1.0, mult)
        a_ref[:, sl] = a
        u_ref[:, sl] = mult * ig * xr[:, sl]

    def body(t, h):
        r0 = pl.multiple_of(t * nb, nb)
        h = a_ref[pl.ds(r0, nb), :] * h + u_ref[pl.ds(r0, nb), :]
        hb_ref[pl.ds(r0, nb), :] = h
        return h

    h = lax.fori_loop(0, tt, body, hs_ref[...], unroll=8)
    hs_ref[...] = h
    yb_ref[...] = _dot((hb_ref[...] * gyb_ref[...]).astype(BF16), wo_ref[...])

    def stage_out(t8, carry):
        for t_lo in range(8):
            r0 = pl.multiple_of((t8 * 8 + t_lo) * nb, nb)
            for ck in range(nck):
                o_ref[ck, slab_rows(t8, t_lo), :] = yb_ref[pl.ds(r0, nb), ck * 128:(ck + 1) * 128]
        return carry

    lax.fori_loop(0, tt // 8, stage_out, 0)


def _rglru(rx, gy, conv_w, conv_b, wa, ba, wi, bi, lam, wo, *, nb, tt=32):
    nck, rows, _ = rx.shape
    S = rows // nb
    assert S % tt == 0 and tt % 8 == 0
    R = nb * tt
    halo = (CONV_W - 1) * nb
    return pl.pallas_call(
        functools.partial(_rglru_kernel, nb=nb, tt=tt),
        grid=(S // tt,),
        in_specs=[
            pl.BlockSpec((nck, R, 128), lambda s: (0, s, 0)),
            pl.BlockSpec((nck, R, 128), lambda s: (0, s, 0)),
            _const_spec((CONV_W, D_RNN)),
            _const_spec((1, D_RNN)),
            _const_spec((RG_BLOCKS, RG_BW, RG_BW)),
            _const_spec((1, D_RNN)),
            _const_spec((RG_BLOCKS, RG_BW, RG_BW)),
            _const_spec((1, D_RNN)),
            _const_spec((1, D_RNN)),
            _const_spec((D_RNN, D_MODEL)),
        ],
        out_specs=pl.BlockSpec((D_MODEL // 128, R, 128), lambda s: (0, s, 0)),
        out_shape=jax.ShapeDtypeStruct((D_MODEL // 128, rows, 128), F32),
        scratch_shapes=[
            pltpu.VMEM((R + halo, D_RNN), F32),
            pltpu.VMEM((R, D_RNN), F32),
            pltpu.VMEM((R, D_RNN), F32),
            pltpu.VMEM((R, D_RNN), F32),
            pltpu.VMEM((R, D_RNN), F32),
            pltpu.VMEM((nb, D_RNN), F32),
            pltpu.VMEM((R, D_MODEL), F32),
        ],
        compiler_params=_params(("arbitrary",)),
        name="rglru",
    )(rx, gy, conv_w, conv_b, wa, ba, wi, bi, lam, wo)


def _kvprep_kernel(zk_ref, zv_ref, g_ref, kn_ref, vt_ref):
    rows = zk_ref.shape[1]
    x = jnp.concatenate([zk_ref[0], zk_ref[1]], axis=1)
    seg = jnp.where(_iota((KVW, KVW), 0) // HEAD_DIM == _iota((KVW, KVW), 1) // HEAD_DIM, 1.0, 0.0).astype(BF16)
    hi, lo = _split(x * x)
    ssq = _dot(jnp.concatenate([hi, lo], axis=0), seg)
    kn = (x * lax.rsqrt((ssq[0:rows] + ssq[rows:2 * rows]) * (1.0 / HEAD_DIM) + EPS) * g_ref[...]).astype(BF16)
    for pr in range(KVW // 128):
        kn_ref[pr] = kn[:, pr * 128:(pr + 1) * 128]
    eye = jnp.where(_iota((KVW, KVW), 0) == _iota((KVW, KVW), 1), 1.0, 0.0).astype(BF16)
    pad = jnp.where(_iota((VROWS - HEAD_DIM, KCH), 0) == 0, 1.0, 0.0).astype(BF16)
    for c in range(rows // KCH):
        sl = slice(c * KCH, (c + 1) * KCH)
        v = jnp.concatenate([zv_ref[0, sl, :], zv_ref[1, sl, :]], axis=1).astype(BF16)
        vt = _dot_nt(eye, v).astype(BF16)
        for g in range(N_KV):
            vt_ref[c, g, 0:HEAD_DIM, :] = vt[g * HEAD_DIM:(g + 1) * HEAD_DIM]
            vt_ref[c, g, HEAD_DIM:VROWS, :] = pad


def _kvprep(zkv, gains, *, B, S, cpb=4):
    nch = S // KCH
    assert nch % cpb == 0
    nblk = nch // cpb
    rows = cpb * KCH
    return pl.pallas_call(
        _kvprep_kernel,
        grid=(2, B, nblk),
        in_specs=[
            pl.BlockSpec((None, KVW // 128, rows, 128), lambda w, b, i: (2 + 2 * w, 0, b * nblk + i, 0)),
            pl.BlockSpec((None, KVW // 128, rows, 128), lambda w, b, i: (3 + 2 * w, 0, b * nblk + i, 0)),
            pl.BlockSpec((None, 1, KVW), lambda w, b, i: (w, 0, 0)),
        ],
        out_specs=[
            pl.BlockSpec((None, None, KVW // 128, rows, 128), lambda w, b, i: (w, b, 0, i, 0)),
            pl.BlockSpec((None, None, cpb, N_KV, VROWS, KCH), lambda w, b, i: (w, b, i, 0, 0, 0)),
        ],
        out_shape=[
            jax.ShapeDtypeStruct((2, B, KVW // 128, S, 128), BF16),
            jax.ShapeDtypeStruct((2, B, nch, N_KV, VROWS, KCH), BF16),
        ],
        compiler_params=_params(("parallel", "parallel", "parallel")),
        name="kvprep",
    )(zkv, zkv, gains)


def _compress_kernel(zk_ref, zv_ref, pos_ref, w1_ref, w2_ref, g_ref, kc_ref, vct_ref):
    nch = zk_ref.shape[0] // CMP_STRIDE
    lane = _iota((nch, 128), 1)
    npair = CMP_STRIDE // 2
    eye = jnp.where(_iota((HEAD_DIM, HEAD_DIM), 0) == _iota((HEAD_DIM, HEAD_DIM), 1), 1.0, 0.0).astype(BF16)
    for kind, z_ref in enumerate((zk_ref, zv_ref)):
        for ge in range(2):
            combs = []
            for p in range(npair):
                a = z_ref[pl.ds(2 * p, nch, stride=CMP_STRIDE), :]
                b = z_ref[pl.ds(2 * p + 1, nch, stride=CMP_STRIDE), :]
                if ge == 0:
                    combs.append(jnp.where(lane < HEAD_DIM, a, pltpu.roll(b, HEAD_DIM, axis=1)))
                else:
                    combs.append(jnp.where(lane < HEAD_DIM, pltpu.roll(a, HEAD_DIM, axis=1), b))
            halves = []
            half_w = npair * 128
            for half in range(2):
                lhs = jnp.concatenate(
                    [(combs[p] + pos_ref[kind, half * npair + p:half * npair + p + 1, :]).astype(BF16)
                     for p in range(npair)], axis=1)
                halves.append(_dot(lhs, w1_ref[kind, half * half_w:(half + 1) * half_w, :]))
            hid = halves[0] + pltpu.roll(halves[1], nch - 1, axis=0)
            hid = jax.nn.gelu(hid)
            out = _dot(hid.astype(BF16), w2_ref[kind])
            if kind == 0:
                out = out * lax.rsqrt(jnp.mean(out * out, axis=-1, keepdims=True) + EPS) * g_ref[...]
            res = jnp.where(_iota(out.shape, 0) < nch - 1, out, 0.0).astype(BF16)
            if kind == 0:
                kc_ref[ge] = res
            else:
                vct_ref[ge] = _dot_nt(eye, res).astype(BF16)


def _compress(zkv, pos, w1, w2, gain, *, B, S):
    nch = S // CMP_STRIDE
    return pl.pallas_call(
        _compress_kernel,
        grid=(B, KVW // 128),
        in_specs=[
            pl.BlockSpec((None, None, S, 128), lambda b, lp: (0, lp, b, 0)),
            pl.BlockSpec((None, None, S, 128), lambda b, lp: (1, lp, b, 0)),
            _const_spec((2, CMP_STRIDE, 128)),
            _const_spec((2, CMP_LEN * HEAD_DIM, CMP_HID)),
            _const_spec((2, CMP_HID, HEAD_DIM)),
            _const_spec((1, HEAD_DIM)),
        ],
        out_specs=[
            pl.BlockSpec((None, 2, nch, HEAD_DIM), lambda b, lp: (b, lp, 0, 0)),
            pl.BlockSpec((None, 2, HEAD_DIM, nch), lambda b, lp: (b, lp, 0, 0)),
        ],
        out_shape=[
            jax.ShapeDtypeStruct((B, N_KV, nch, HEAD_DIM), BF16),
            jax.ShapeDtypeStruct((B, N_KV, HEAD_DIM, nch), BF16),
        ],
        compiler_params=_params(("parallel", "parallel")),
        name="compress",
    )(zkv, zkv, pos, w1, w2, gain)


def _spread_heads(x):
    y = x + pltpu.roll(x, 2 * Q_BLK, axis=1)
    return y + pltpu.roll(y, Q_BLK, axis=1)


def _swap_heads(x):
    n = x.shape[0]
    lane = _iota((n, 128), 1)
    halves = []
    for pr in range(x.shape[1] // 128):
        slab = jnp.concatenate([x[:, pr * 128:(pr + 1) * 128], jnp.zeros((128 - n, 128), x.dtype)], axis=0)
        tr = slab.T
        halves.append(jnp.where(lane < n, tr[0:n], pltpu.roll(tr[n:2 * n], n, axis=1)))
    return jnp.concatenate(halves, axis=1)


def _nsa_kernel(*refs, nsel, qpb):
    per_chunk = KCH // SEL_LEN
    for k in range(qpb):
        _nsa_block(k, "head", None, *refs, nsel=nsel, qpb=qpb)

    def step(i, carry):
        for k in range(qpb):
            _nsa_block(k, "step", i, *refs, nsel=nsel, qpb=qpb)
        return carry

    common = pl.program_id(1) * (qpb // per_chunk)
    lax.fori_loop(0, common, step, 0)
    for k in range(qpb):
        for extra in range(k // per_chunk):
            _nsa_block(k, "step", common + extra, *refs, nsel=nsel, qpb=qpb)
    for k in range(qpb):
        _nsa_block(k, "tail", None, *refs, nsel=nsel, qpb=qpb)


def _nsa_block(k, phase, step_i, zq_ref, qg_ref, kc_ref, vct_ref, ks_ref, kw_ref, vst_ref, vwt_ref, ng_ref, o_ref,
               qtp_ref, oc_ref, ow_ref, bias_ref, ms_ref, accs_ref, mw_ref, accw_ref, sbuf_ref, wbuf_ref,
               gates_ref, scs_ref, wbias_ref, *, nsel, qpb):
    qtp_ref, oc_ref, ow_ref, bias_ref, ms_ref, accs_ref, sbuf_ref, gates_ref = (
        r.at[k] for r in (qtp_ref, oc_ref, ow_ref, bias_ref, ms_ref, accs_ref, sbuf_ref, gates_ref))
    qi = pl.program_id(1) * qpb + k
    q_rows = pl.ds(k * Q_BLK, Q_BLK)
    s0 = qi * Q_BLK
    cur = qi
    ncmp = kc_ref.shape[1]
    per_chunk = KCH // SEL_LEN
    groups = range(N_KV)
    n_chunks = qi // per_chunk + 1
    cur_slab = k % per_chunk
    cut_rows = slice(cur_slab * SEL_LEN, (cur_slab + 1) * SEL_LEN)
    last_slabs = per_chunk // 2 if cur_slab < per_chunk // 2 else per_chunk

    def scores(k_ref, i):
        k0 = pl.multiple_of(i * KCH, KCH)
        return [_dot(k_ref[g // 2, pl.ds(k0, KCH), :], qtp_ref[g]) for g in groups]

    def park(buf_ref, vals):
        for g in groups:
            buf_ref[g] = vals[g]

    def cut(buf_ref, tri):
        for g in groups:
            buf_ref[g, cut_rows, :] = buf_ref[g, cut_rows, :] + tri

    def update(g, blocks, shifts, vt, m_ref, acc_ref):
        m = m_ref[g]
        part = None
        for blk, sh in zip(blocks, shifts):
            top8 = jnp.max(blk.reshape(SEL_LEN // 8, 8, GW), axis=0) + sh
            part = top8 if part is None else jnp.maximum(part, top8)
        mnew = jnp.maximum(m, jnp.max(part, axis=0, keepdims=True))
        e = jnp.concatenate([jnp.exp2(blk + (sh - mnew)) for blk, sh in zip(blocks, shifts)], axis=0).astype(BF16)
        acc_ref[g] = jnp.exp2(m - mnew) * acc_ref[g] + _dot(vt, e)
        m_ref[g] = mnew

    def slabs(buf_ref, g, ns):
        return [buf_ref[g, nb * SEL_LEN:(nb + 1) * SEL_LEN, :] for nb in range(ns)]

    def sel_softmax(i, ns=per_chunk):
        for g in groups:
            shifts = [bias_ref[g, pl.ds(i * per_chunk + nb, 1), :] for nb in range(ns)]
            update(g, slabs(sbuf_ref, g, ns), shifts, vst_ref[i, g, :, 0:ns * SEL_LEN], ms_ref, accs_ref)

    key_row = _iota((SEL_LEN, GW), 0)
    in_blk_t = _iota((SEL_LEN, GW), 1) % Q_BLK
    causal_tri = jnp.where(key_row <= in_blk_t, 0.0, NEG)

    if phase == "step":
        nxt = scores(ks_ref, step_i + 1)
        sel_softmax(step_i)
        park(sbuf_ref, nxt)
        return

    if phase == "tail":
        cut(sbuf_ref, causal_tri)
        sel_softmax(n_chunks - 1, last_slabs)
        outs = []
        for g in groups:
            o_s = accs_ref[g, 0:HEAD_DIM, :] * (1.0 / jnp.maximum(accs_ref[g, HEAD_DIM:HEAD_DIM + 1, :], 1e-30))
            gates = gates_ref[g]
            o_t = gates[0:1] * oc_ref[g] + gates[1:2] * o_s + gates[2:3] * ow_ref[g]
            outs.append(_swap_heads(o_t))
        for g in groups:
            o_ref[g, q_rows, :] = outs[g].astype(o_ref.dtype)
        return

    lane_t = s0 + (_iota((1, GW), 1) % Q_BLK)
    lane_grp = _iota((1, GW), 1) // Q_BLK
    ov_n = _iota((nsel, ncmp), 0) * SEL_LEN
    ov_c = _iota((nsel, ncmp), 1) * CMP_STRIDE
    overlap = jnp.where((ov_c < ov_n + SEL_LEN) & (ov_c + CMP_LEN > ov_n), 1.0, 0.0).astype(BF16)


    ok_c = _iota((ncmp, GW), 0) * CMP_STRIDE + (CMP_LEN - 1) <= lane_t
    ngt = jnp.concatenate([ng_ref[q_rows, :], jnp.zeros((128 - Q_BLK, NG_PAD), F32)], axis=0).T
    ngt_hi = pltpu.roll(ngt, Q_BLK, axis=1)
    low_half = _iota((1, 128), 1) < Q_BLK
    for g in groups:
        for c in range(3):
            r = c * N_HEADS + g * HPG
            pairs = [jnp.where(low_half, ngt[r + j:r + j + 1], ngt_hi[r + j + 1:r + j + 2]) for j in (0, 2)]
            gates_ref[g, c:c + 1, :] = jnp.concatenate(pairs, axis=1)

    qts = []
    for g in groups:
        xt = _swap_heads(zq_ref[g, q_rows, :])
        ssq = jnp.sum(xt * xt, axis=0, keepdims=True)
        qts.append((xt * lax.rsqrt(ssq * (1.0 / HEAD_DIM) + EPS) * qg_ref[...]).astype(BF16))
    scs = []
    for g in groups:
        off = (g % 2) * HEAD_DIM
        qtp_ref[g, off:off + HEAD_DIM, :] = qts[g]
        qtp_ref[g, HEAD_DIM - off:2 * HEAD_DIM - off, :] = jnp.zeros((HEAD_DIM, GW), BF16)
        scs.append(_dot(kc_ref[g], qts[g]))

    n_win = WINDOW // KCH + 1
    win_ids = [jnp.maximum(n_chunks - (n_win - j), 0) for j in range(n_win)]
    win_dead = [jnp.where(n_chunks >= n_win - j, 0.0, NEG) for j in range(n_win)]
    park(scs_ref, scs)
    park(sbuf_ref, scores(ks_ref, 0))
    park(wbuf_ref, scores(kw_ref, win_ids[0]))

    expired_tri = jnp.where(key_row > in_blk_t, 0.0, NEG)
    win_blocks = WINDOW // SEL_LEN
    n_all = _iota((nsel, GW), 0)
    wbias_ref[...] = jnp.where((n_all >= cur - win_blocks) & (n_all <= cur), 0.0, NEG)
    cut(wbuf_ref, jnp.where(cur >= win_blocks, expired_tri, 0.0))

    ms_ref[...] = jnp.full(ms_ref.shape, NEG, F32)
    mw_ref[...] = jnp.full(mw_ref.shape, NEG, F32)
    accs_ref[...] = jnp.zeros(accs_ref.shape, F32)
    accw_ref[...] = jnp.zeros(accw_ref.shape, F32)

    def win_softmax(i, dead, lo=0, hi=per_chunk):
        shifts = [wbias_ref[pl.ds(i * per_chunk + nb, 1), :] + dead for nb in range(lo, hi)]
        for g in groups:
            blocks = [wbuf_ref[g, nb * SEL_LEN:(nb + 1) * SEL_LEN, :] for nb in range(lo, hi)]
            update(g, blocks, shifts, vwt_ref[i, g, :, lo * SEL_LEN:hi * SEL_LEN], mw_ref, accw_ref)

    first_lo = per_chunk // 2 if cur_slab >= per_chunk // 2 else 0
    for j in range(n_win - 1):
        nxt = scores(kw_ref, win_ids[j + 1])
        win_softmax(win_ids[j], win_dead[j], lo=first_lo if j == 0 else 0)
        park(wbuf_ref, nxt)

    imp_all = jnp.zeros((nsel, GW), F32)
    some_c = lane_t >= CMP_LEN - 1
    ones_rows = jnp.ones((16, ncmp), BF16)
    for g in groups:
        scm = jnp.where(ok_c, scs_ref[g], NEG)
        ec = jnp.exp2(scm - jnp.max(scm, axis=0, keepdims=True))
        both = _dot(jnp.concatenate([vct_ref[g], overlap, ones_rows], axis=0), ec.astype(BF16))
        denom = both[HEAD_DIM + nsel:HEAD_DIM + nsel + 1]
        both = both[0:HEAD_DIM + nsel] * jnp.where(some_c, 1.0 / jnp.maximum(denom, 1e-30), 0.0)
        oc_ref[g] = both[0:HEAD_DIM]
        imp = _spread_heads(both[HEAD_DIM:HEAD_DIM + nsel])
        imp_all = jnp.where(lane_grp == g, imp, imp_all)

    n_io = _iota((nsel, GW), 0)
    forced = (n_io == 0) | (n_io == cur) | (n_io == cur - 1)
    valid = n_io <= cur
    score = jnp.where(forced, FORCE_SCORE, imp_all)
    score = jnp.where(valid, score, -1.0)
    rows8 = [score[r:r + 8] for r in range(0, nsel, 8)]
    ranks = [jnp.zeros((8, GW), F32) for _ in rows8]
    sub = _iota((8, GW), 0)
    for m in range(nsel):
        rowv = score[m:m + 1, :]
        for k, blk in enumerate(rows8):
            ge = jnp.where(rowv >= blk, 1.0, 0.0)
            gt = jnp.where(rowv > blk, 1.0, 0.0)
            if 8 * k > m:
                inc = ge
            elif 8 * k + 7 < m:
                inc = gt
            else:
                inc = jnp.where(sub + 8 * k > m, ge, gt)
            ranks[k] = ranks[k] + inc
    rank = jnp.concatenate(ranks, axis=0)
    sel_all = jnp.where((rank < float(min(SEL_TOPN, nsel))) & valid, 1.0, 0.0)
    for g in groups:
        mine = jnp.where(lane_grp == g, sel_all, 0.0)
        bias_ref[g] = (_spread_heads(mine) - 1.0) * (-NEG)

    cut(wbuf_ref, causal_tri)
    win_softmax(win_ids[n_win - 1], win_dead[n_win - 1], hi=last_slabs)
    for g in groups:
        ow_ref[g] = accw_ref[g, 0:HEAD_DIM, :] * (1.0 / jnp.maximum(accw_ref[g, HEAD_DIM:HEAD_DIM + 1, :], 1e-30))


def _nsa(zq, qgain, kc, vct, kn, vt, ng, *, B, S):
    T = B * S
    qpb = 2 * (KCH // SEL_LEN)
    assert S % (Q_BLK * qpb) == 0
    nq = S // (Q_BLK * qpb)
    nch = S // KCH
    ncmp = S // CMP_STRIDE
    nsel = S // SEL_LEN
    tok = lambda b, i: (0, b * nq + i, 0)
    return pl.pallas_call(
        functools.partial(_nsa_kernel, nsel=nsel, qpb=qpb),
        grid=(B, nq),
        in_specs=[
            pl.BlockSpec((N_KV, Q_BLK * qpb, GW), tok),
            _const_spec((HEAD_DIM, GW)),
            pl.BlockSpec((None, N_KV, ncmp, HEAD_DIM), lambda b, i: (b, 0, 0, 0)),
            pl.BlockSpec((None, N_KV, HEAD_DIM, ncmp), lambda b, i: (b, 0, 0, 0)),
            pl.BlockSpec((None, None, 2, S, 128), lambda b, i: (0, b, 0, 0, 0)),
            pl.BlockSpec((None, None, 2, S, 128), lambda b, i: (1, b, 0, 0, 0)),
            pl.BlockSpec((None, None, nch, N_KV, VROWS, KCH), lambda b, i: (0, b, 0, 0, 0, 0)),
            pl.BlockSpec((None, None, nch, N_KV, VROWS, KCH), lambda b, i: (1, b, 0, 0, 0, 0)),
            pl.BlockSpec((Q_BLK * qpb, NG_PAD), lambda b, i: (b * nq + i, 0)),
        ],
        out_specs=pl.BlockSpec((N_KV, Q_BLK * qpb, GW), tok),
        out_shape=jax.ShapeDtypeStruct((N_KV, T, GW), BF16),
        scratch_shapes=[
            pltpu.VMEM((qpb, N_KV, 2 * HEAD_DIM, GW), BF16),
            pltpu.VMEM((qpb, N_KV, HEAD_DIM, GW), F32),
            pltpu.VMEM((qpb, N_KV, HEAD_DIM, GW), F32),
            pltpu.VMEM((qpb, N_KV, nsel, GW), F32),
            pltpu.VMEM((qpb, N_KV, 1, GW), F32),
            pltpu.VMEM((qpb, N_KV, VROWS, GW), F32),
            pltpu.VMEM((N_KV, 1, GW), F32),
            pltpu.VMEM((N_KV, VROWS, GW), F32),
            pltpu.VMEM((qpb, N_KV, KCH, GW), F32),
            pltpu.VMEM((N_KV, KCH, GW), F32),
            pltpu.VMEM((qpb, N_KV, 8, GW), F32),
            pltpu.VMEM((N_KV, ncmp, GW), F32),
            pltpu.VMEM((nsel, GW), F32),
        ],
        compiler_params=_params(("parallel", "arbitrary")),
        name="nsa",
    )(zq, qgain, kc, vct, kn, kn, vt, vt, ng)


def _merge_kernel(x_ref, ya_ref, o_ref_in, mg_ref, wn_ref, wo_ref, out_ref):
    yb = _dot(o_ref_in[0], wn_ref[0:GW, :])
    for g in range(1, N_KV):
        yb = yb + _dot(o_ref_in[g], wn_ref[g * GW:(g + 1) * GW, :])
    tm = x_ref.shape[0]
    ya = jnp.concatenate([ya_ref[ck].reshape(tm, 128) for ck in range(D_MODEL // 128)], axis=1)
    mixed = (mg_ref[:, 0:D_MODEL].astype(F32) * ya
             + mg_ref[:, D_MODEL:2 * D_MODEL].astype(F32) * yb)
    out_ref[...] = x_ref[...] + _dot(mixed.astype(BF16), wo_ref[...])


def _merge(x2d, ya, o, mg, wn, wo, *, B, S, tm=512):
    ni = S // tm
    tok = lambda b, i: (b * ni + i, 0)
    return pl.pallas_call(
        _merge_kernel,
        grid=(B, ni),
        in_specs=[
            pl.BlockSpec((tm, D_MODEL), tok),
            pl.BlockSpec((D_MODEL // 128, tm // 8, None, 8, 128), lambda b, i: (0, i, b, 0, 0)),
            pl.BlockSpec((N_KV, tm, GW), lambda b, i: (0, b * ni + i, 0)),
            pl.BlockSpec((tm, 2 * D_MODEL), tok),
            _const_spec((N_HEADS * HEAD_DIM, D_MODEL)),
            _const_spec((D_MODEL, D_MODEL)),
        ],
        out_specs=pl.BlockSpec((tm, D_MODEL), tok),
        out_shape=jax.ShapeDtypeStruct((B * S, D_MODEL), F32),
        compiler_params=_params(("parallel", "parallel")),
        name="merge",
    )(x2d, ya, o, mg, wn, wo)


def _ffn_kernel(x_ref, g_ref, wg_ref, wu_ref, wd_ref, o_ref, h_ref, *, tf):
    x = x_ref[...]
    y = x * lax.rsqrt(jnp.mean(x * x, axis=-1, keepdims=True) + EPS) * g_ref[...]
    h_ref[...] = y.astype(BF16)
    o_ref[...] = x
    for j in range(D_FF // tf):
        h = h_ref[...]
        sl = slice(j * tf, (j + 1) * tf)
        act = jax.nn.silu(_dot(h, wg_ref[:, sl])) * _dot(h, wu_ref[:, sl])
        o_ref[...] += _dot(act.astype(BF16), wd_ref[sl, :])


def _ffn(x2d, gain, wg, wu, wd, *, tm=512, tf=256):
    T = x2d.shape[0]
    assert T % tm == 0 and D_FF % tf == 0
    once = dict(pipeline_mode=pl.Buffered(1))
    return pl.pallas_call(
        functools.partial(_ffn_kernel, tf=tf),
        grid=(T // tm,),
        in_specs=[
            pl.BlockSpec((tm, D_MODEL), lambda i: (i, 0)),
            _const_spec((1, D_MODEL)),
            pl.BlockSpec((D_MODEL, D_FF), lambda i: (0, 0), **once),
            pl.BlockSpec((D_MODEL, D_FF), lambda i: (0, 0), **once),
            pl.BlockSpec((D_FF, D_MODEL), lambda i: (0, 0), **once),
        ],
        out_specs=pl.BlockSpec((tm, D_MODEL), lambda i: (i, 0)),
        out_shape=jax.ShapeDtypeStruct((T, D_MODEL), F32),
        scratch_shapes=[pltpu.VMEM((tm, D_MODEL), BF16)],
        compiler_params=_params(("parallel",)),
        name="ffn",
    )(x2d, gain, wg, wu, wd)


def _layer(x, p):
    B, S, _ = x.shape
    T = B * S
    dh = HEAD_DIM
    row = lambda v: v.reshape(1, -1)
    w_in = p["w_in"]
    n_ng = 3 * N_HEADS
    o_ng = 2 * D_RNN + N_HEADS * dh + 6 * KVW
    w_packed = jnp.concatenate(
        [w_in[:, :o_ng], jnp.pad(w_in[:, o_ng:o_ng + n_ng], ((0, 0), (0, NG_PAD - n_ng))), w_in[:, o_ng + n_ng:]],
        axis=1).astype(BF16)
    rx, gy, zq, zkv, ng, mg = _inproj(x, row(p["norm1"]), w_packed)

    slabs = lambda a: a.reshape(a.shape[0], S * B, 128)
    ya = _rglru(slabs(rx), slabs(gy), p["conv_w"], row(p["conv_b"]),
                p["rg_wa"].astype(BF16), row(p["rg_ba"]), p["rg_wi"].astype(BF16), row(p["rg_bi"]),
                row(p["rg_lambda"]), p["w_rg_out"].astype(BF16), nb=B)
    ya = ya.reshape(D_MODEL // 128, S // 8, B, 8, 128)

    kgains = jnp.tile(p["k_norm"][1:3], (1, N_KV)).reshape(2, 1, KVW)
    kn, vt = _kvprep(zkv, kgains, B=B, S=S)
    pos = jnp.stack([p["cmp_pos_k"], p["cmp_pos_v"]]).reshape(2, CMP_STRIDE, 2 * dh)
    w1 = jnp.stack([p["cmp_k_w1"], p["cmp_v_w1"]]).astype(BF16)
    w2 = jnp.stack([p["cmp_k_w2"], p["cmp_v_w2"]]).astype(BF16)
    kc, vct = _compress(zkv, pos, w1, w2, row(p["k_norm"][0]), B=B, S=S)
    qgain = jnp.broadcast_to((p["q_norm"] * (dh ** -0.5 * LOG2E))[:, None], (dh, GW))
    o = _nsa(zq, qgain, kc, vct, kn, vt, ng, B=B, S=S)

    x1 = _merge(x.reshape(T, D_MODEL), ya, o, mg, p["w_nsa_out"].astype(BF16), p["w_o"].astype(BF16), B=B, S=S)
    x2 = _ffn(x1, row(p["norm2"]), p["w_gate"].astype(BF16), p["w_up"].astype(BF16), p["w_down"].astype(BF16))
    return x2.reshape(B, S, D_MODEL)


def kernel(x, norm1, w_in, conv_w, conv_b, rg_wa, rg_ba, rg_wi, rg_bi, rg_lambda, q_norm, k_norm, cmp_pos_k,
           cmp_pos_v, cmp_k_w1, cmp_k_w2, cmp_v_w1, cmp_v_w2, w_rg_out, w_nsa_out, w_o, norm2, w_gate, w_up, w_down):
    params = dict(norm1=norm1, w_in=w_in, conv_w=conv_w, conv_b=conv_b, rg_wa=rg_wa, rg_ba=rg_ba, rg_wi=rg_wi,
                  rg_bi=rg_bi, rg_lambda=rg_lambda, q_norm=q_norm, k_norm=k_norm, cmp_pos_k=cmp_pos_k,
                  cmp_pos_v=cmp_pos_v, cmp_k_w1=cmp_k_w1, cmp_k_w2=cmp_k_w2, cmp_v_w1=cmp_v_w1, cmp_v_w2=cmp_v_w2,
                  w_rg_out=w_rg_out, w_nsa_out=w_nsa_out, w_o=w_o, norm2=norm2, w_gate=w_gate, w_up=w_up,
                  w_down=w_down)
    for l in range(norm1.shape[0]):
        x = _layer(x, {k: v[l] for k, v in params.items()})
    return x
```

```python
import functools

import jax
import jax.numpy as jnp
from jax import lax
from jax.experimental import pallas as pl
from jax.experimental.pallas import tpu as pltpu

D_MODEL = 1024
D_RNN = 1024
RG_BLOCKS = 4
RG_BW = D_RNN // RG_BLOCKS
CONV_W = 4
RG_C = 8.0
N_HEADS = 16
N_KV = 4
HEAD_DIM = 64
HPG = N_HEADS // N_KV
CMP_LEN = 32
CMP_STRIDE = 16
CMP_HID = 256
SEL_LEN = 64
SEL_TOPN = 16
WINDOW = 512
Q_BLK = 64
D_FF = 2816
EPS = 1e-6
FORCE_SCORE = 1e6
NEG = -1e30
LOG2E = 1.4426950408889634

KCH = 256
GW = HPG * HEAD_DIM
KVW = N_KV * HEAD_DIM
VROWS = 80
NG_PAD = 128

F32 = jnp.float32
BF16 = jnp.bfloat16

VMEM_LIMIT = 56 * 1024 * 1024


def _params(sem):
    return pltpu.CompilerParams(dimension_semantics=sem, vmem_limit_bytes=VMEM_LIMIT)


def _dot(a, b):
    return jnp.dot(a, b, preferred_element_type=F32)


def _dot_nt(a, b):
    return lax.dot_general(a, b, (((1,), (1,)), ((), ())), preferred_element_type=F32)


def _split(x):
    hi = x.astype(BF16)
    lo = (x - hi.astype(F32)).astype(BF16)
    return hi, lo


def _iota(shape, dim):
    return lax.broadcasted_iota(jnp.int32, shape, dim)


def _const_spec(shape):
    return pl.BlockSpec(shape, lambda *_: (0,) * len(shape))


_O_RY = D_RNN
_O_Q = 2 * D_RNN
_O_KV = _O_Q + N_HEADS * HEAD_DIM
_O_NG = _O_KV + 6 * KVW
_O_MG = _O_NG + NG_PAD
_W_COLS = _O_MG + 2 * D_MODEL


def _inproj_kernel(x_ref, g_ref, w_ref, rx_ref, gy_ref, zq_ref, zkv_ref, ng_ref, mg_ref, h_ref):
    x = x_ref[...]
    y = x * lax.rsqrt(jnp.mean(x * x, axis=-1, keepdims=True) + EPS) * g_ref[...]
    h_ref[...] = y.astype(BF16)

    def proj(c0, width):
        return _dot(h_ref[...], w_ref[:, c0:c0 + width])

    cw = 256
    tm = x_ref.shape[0]
    for c in range(D_RNN // cw):
        rx = proj(c * cw, cw)
        gy = jax.nn.gelu(proj(_O_RY + c * cw, cw))
        for hf in range(cw // 128):
            rx_ref[c * (cw // 128) + hf] = rx[:, hf * 128:(hf + 1) * 128].reshape(tm // 8, 8, 128)
            gy_ref[c * (cw // 128) + hf] = gy[:, hf * 128:(hf + 1) * 128].reshape(tm // 8, 8, 128)
    for g in range(N_KV):
        zq_ref[g] = proj(_O_Q + g * GW, GW)
    for s in range(6):
        z = proj(_O_KV + s * KVW, KVW)
        for pr in range(KVW // 128):
            zkv_ref[s, pr] = z[:, pr * 128:(pr + 1) * 128]
    ng_ref[...] = jax.nn.sigmoid(proj(_O_NG, NG_PAD))
    for c in range(2 * D_MODEL // cw):
        mg_ref[:, c * cw:(c + 1) * cw] = jax.nn.sigmoid(proj(_O_MG + c * cw, cw)).astype(mg_ref.dtype)


def _inproj(x, gain, w, *, tm=256):
    B, S, D = x.shape
    T = B * S
    ni = S // tm
    nck = D_RNN // 128
    tok = lambda b, i: (b * ni + i, 0)
    slab_spec = pl.BlockSpec((nck, tm // 8, None, 8, 128), lambda b, i: (0, i, b, 0, 0))
    slab_shape = jax.ShapeDtypeStruct((nck, S // 8, B, 8, 128), F32)
    return pl.pallas_call(
        _inproj_kernel,
        grid=(B, ni),
        in_specs=[
            pl.BlockSpec((None, tm, D), lambda b, i: (b, i, 0)),
            _const_spec((1, D)),
            pl.BlockSpec((D, _W_COLS), lambda b, i: (0, 0), pipeline_mode=pl.Buffered(1)),
        ],
        out_specs=[
            slab_spec,
            slab_spec,
            pl.BlockSpec((N_KV, tm, GW), lambda b, i: (0, b * ni + i, 0)),
            pl.BlockSpec((6, KVW // 128, tm, 128), lambda b, i: (0, 0, b * ni + i, 0)),
            pl.BlockSpec((tm, NG_PAD), tok),
            pl.BlockSpec((tm, 2 * D_MODEL), tok),
        ],
        out_shape=[
            slab_shape,
            slab_shape,
            jax.ShapeDtypeStruct((N_KV, T, GW), F32),
            jax.ShapeDtypeStruct((6, KVW // 128, T, 128), F32),
            jax.ShapeDtypeStruct((T, NG_PAD), F32),
            jax.ShapeDtypeStruct((T, 2 * D_MODEL), BF16),
        ],
        scratch_shapes=[pltpu.VMEM((tm, D), BF16)],
        compiler_params=_params(("parallel", "parallel")),
        name="inproj",
    )(x, gain, w)


def _rglru_kernel(x_ref, gy_ref, cw_ref, cb_ref, wa_ref, ba_ref, wi_ref, bi_ref, lam_ref, wo_ref,
                  o_ref, xb_ref, gyb_ref, a_ref, u_ref, hb_ref, hs_ref, yb_ref, *, nb, tt):
    R = nb * tt
    halo = (CONV_W - 1) * nb
    step = pl.program_id(0)
    nck = D_RNN // 128

    @pl.when(step == 0)
    def _():
        xb_ref[0:halo, :] = jnp.zeros((halo, D_RNN), F32)
        hs_ref[...] = jnp.zeros((nb, D_RNN), F32)

    def slab_rows(t8, t_lo):
        return pl.ds(t8 * (8 * nb) + t_lo, nb, stride=8)

    def stage_in(t8, carry):
        for t_lo in range(8):
            r0 = pl.multiple_of((t8 * 8 + t_lo) * nb, nb)
            for ck in range(nck):
                cols = slice(ck * 128, (ck + 1) * 128)
                xb_ref[pl.ds(halo + r0, nb), cols] = x_ref[ck, slab_rows(t8, t_lo), :]
                gyb_ref[pl.ds(r0, nb), cols] = gy_ref[ck, slab_rows(t8, t_lo), :]
        return carry

    lax.fori_loop(0, tt // 8, stage_in, 0)
    xr = cb_ref[...] + cw_ref[0:1, :] * xb_ref[0:R, :]
    for k in range(1, CONV_W):
        xr = xr + cw_ref[k:k + 1, :] * xb_ref[k * nb:k * nb + R, :]
    tail = xb_ref[R:R + halo, :]
    xb_ref[0:halo, :] = tail

    xrb = xr.astype(BF16)
    sp = jax.nn.softplus(-lam_ref[...])
    row = _iota((R, RG_BW), 0)
    first = (row < nb) & (step == 0)
    for n in range(RG_BLOCKS):
        sl = slice(n * RG_BW, (n + 1) * RG_BW)
        xn = xrb[:, sl]
        r = jax.nn.sigmoid(_dot(xn, wa_ref[n]) + ba_ref[:, sl])
        ig = jax.nn.sigmoid(_dot(xn, wi_ref[n]) + bi_ref[:, sl])
        log_a = (-RG_C) * r * sp[:, sl]
        a = jnp.exp(log_a)
        mult = jnp.sqrt(1.0 - a * a)
        mult = jnp.where(first, 1.0, mult)
        a_ref[:, sl] = a
        u_ref[:, sl] = mult * ig * xr[:, sl]

    def body(t, h):
        r0 = pl.multiple_of(t * nb, nb)
        h = a_ref[pl.ds(r0, nb), :] * h + u_ref[pl.ds(r0, nb), :]
        hb_ref[pl.ds(r0, nb), :] = h
        return h

    h = lax.fori_loop(0, tt, body, hs_ref[...], unroll=8)
    hs_ref[...] = h
    yb_ref[...] = _dot((hb_ref[...] * gyb_ref[...]).astype(BF16), wo_ref[...])

    def stage_out(t8, carry):
        for t_lo in range(8):
            r0 = pl.multiple_of((t8 * 8 + t_lo) * nb, nb)
            for ck in range(nck):
                o_ref[ck, slab_rows(t8, t_lo), :] = yb_ref[pl.ds(r0, nb), ck * 128:(ck + 1) * 128]
        return carry

    lax.fori_loop(0, tt // 8, stage_out, 0)


def _rglru(rx, gy, conv_w, conv_b, wa, ba, wi, bi, lam, wo, *, nb, tt=32):
    nck, rows, _ = rx.shape
    S = rows // nb
    assert S % tt == 0 and tt % 8 == 0
    R = nb * tt
    halo = (CONV_W - 1) * nb
    return pl.pallas_call(
        functools.partial(_rglru_kernel, nb=nb, tt=tt),
        grid=(S // tt,),
        in_specs=[
            pl.BlockSpec((nck, R, 128), lambda s: (0, s, 0)),
            pl.BlockSpec((nck, R, 128), lambda s: (0, s, 0)),
            _const_spec((CONV_W, D_RNN)),
            _const_spec((1, D_RNN)),
            _const_spec((RG_BLOCKS, RG_BW, RG_BW)),
            _const_spec((1, D_RNN)),
            _const_spec((RG_BLOCKS, RG_BW, RG_BW)),
            _const_spec((1, D_RNN)),
            _const_spec((1, D_RNN)),
            _const_spec((D_RNN, D_MODEL)),
        ],
        out_specs=pl.BlockSpec((D_MODEL // 128, R, 128), lambda s: (0, s, 0)),
        out_shape=jax.ShapeDtypeStruct((D_MODEL // 128, rows, 128), F32),
        scratch_shapes=[
            pltpu.VMEM((R + halo, D_RNN), F32),
            pltpu.VMEM((R, D_RNN), F32),
            pltpu.VMEM((R, D_RNN), F32),
            pltpu.VMEM((R, D_RNN), F32),
            pltpu.VMEM((R, D_RNN), F32),
            pltpu.VMEM((nb, D_RNN), F32),
            pltpu.VMEM((R, D_MODEL), F32),
        ],
        compiler_params=_params(("arbitrary",)),
        name="rglru",
    )(rx, gy, conv_w, conv_b, wa, ba, wi, bi, lam, wo)


def _kvprep_kernel(zk_ref, zv_ref, g_ref, kn_ref, vt_ref):
    rows = zk_ref.shape[1]
    x = jnp.concatenate([zk_ref[0], zk_ref[1]], axis=1)
    seg = jnp.where(_iota((KVW, KVW), 0) // HEAD_DIM == _iota((KVW, KVW), 1) // HEAD_DIM, 1.0, 0.0).astype(BF16)
    hi, lo = _split(x * x)
    ssq = _dot(jnp.concatenate([hi, lo], axis=0), seg)
    kn = (x * lax.rsqrt((ssq[0:rows] + ssq[rows:2 * rows]) * (1.0 / HEAD_DIM) + EPS) * g_ref[...]).astype(BF16)
    for pr in range(KVW // 128):
        kn_ref[pr] = kn[:, pr * 128:(pr + 1) * 128]
    eye = jnp.where(_iota((KVW, KVW), 0) == _iota((KVW, KVW), 1), 1.0, 0.0).astype(BF16)
    pad = jnp.where(_iota((VROWS - HEAD_DIM, KCH), 0) == 0, 1.0, 0.0).astype(BF16)
    for c in range(rows // KCH):
        sl = slice(c * KCH, (c + 1) * KCH)
        v = jnp.concatenate([zv_ref[0, sl, :], zv_ref[1, sl, :]], axis=1).astype(BF16)
        vt = _dot_nt(eye, v).astype(BF16)
        for g in range(N_KV):
            vt_ref[c, g, 0:HEAD_DIM, :] = vt[g * HEAD_DIM:(g + 1) * HEAD_DIM]
            vt_ref[c, g, HEAD_DIM:VROWS, :] = pad


def _kvprep(zkv, gains, *, B, S, cpb=4):
    nch = S // KCH
    assert nch % cpb == 0
    nblk = nch // cpb
    rows = cpb * KCH
    return pl.pallas_call(
        _kvprep_kernel,
        grid=(2, B, nblk),
        in_specs=[
            pl.BlockSpec((None, KVW // 128, rows, 128), lambda w, b, i: (2 + 2 * w, 0, b * nblk + i, 0)),
            pl.BlockSpec((None, KVW // 128, rows, 128), lambda w, b, i: (3 + 2 * w, 0, b * nblk + i, 0)),
            pl.BlockSpec((None, 1, KVW), lambda w, b, i: (w, 0, 0)),
        ],
        out_specs=[
            pl.BlockSpec((None, None, KVW // 128, rows, 128), lambda w, b, i: (w, b, 0, i, 0)),
            pl.BlockSpec((None, None, cpb, N_KV, VROWS, KCH), lambda w, b, i: (w, b, i, 0, 0, 0)),
        ],
        out_shape=[
            jax.ShapeDtypeStruct((2, B, KVW // 128, S, 128), BF16),
            jax.ShapeDtypeStruct((2, B, nch, N_KV, VROWS, KCH), BF16),
        ],
        compiler_params=_params(("parallel", "parallel", "parallel")),
        name="kvprep",
    )(zkv, zkv, gains)


def _compress_kernel(zk_ref, zv_ref, pos_ref, w1_ref, w2_ref, g_ref, kc_ref, vct_ref):
    nch = zk_ref.shape[0] // CMP_STRIDE
    lane = _iota((nch, 128), 1)
    npair = CMP_STRIDE // 2
    eye = jnp.where(_iota((HEAD_DIM, HEAD_DIM), 0) == _iota((HEAD_DIM, HEAD_DIM), 1), 1.0, 0.0).astype(BF16)
    for kind, z_ref in enumerate((zk_ref, zv_ref)):
        for ge in range(2):
            combs = []
            for p in range(npair):
                a = z_ref[pl.ds(2 * p, nch, stride=CMP_STRIDE), :]
                b = z_ref[pl.ds(2 * p + 1, nch, stride=CMP_STRIDE), :]
                if ge == 0:
                    combs.append(jnp.where(lane < HEAD_DIM, a, pltpu.roll(b, HEAD_DIM, axis=1)))
                else:
                    combs.append(jnp.where(lane < HEAD_DIM, pltpu.roll(a, HEAD_DIM, axis=1), b))
            halves = []
            half_w = npair * 128
            for half in range(2):
                lhs = jnp.concatenate(
                    [(combs[p] + pos_ref[kind, half * npair + p:half * npair + p + 1, :]).astype(BF16)
                     for p in range(npair)], axis=1)
                halves.append(_dot(lhs, w1_ref[kind, half * half_w:(half + 1) * half_w, :]))
            hid = halves[0] + pltpu.roll(halves[1], nch - 1, axis=0)
            hid = jax.nn.gelu(hid)
            out = _dot(hid.astype(BF16), w2_ref[kind])
            if kind == 0:
                out = out * lax.rsqrt(jnp.mean(out * out, axis=-1, keepdims=True) + EPS) * g_ref[...]
            res = jnp.where(_iota(out.shape, 0) < nch - 1, out, 0.0).astype(BF16)
            if kind == 0:
                kc_ref[ge] = res
            else:
                vct_ref[ge] = _dot_nt(eye, res).astype(BF16)


def _compress(zkv, pos, w1, w2, gain, *, B, S):
    nch = S // CMP_STRIDE
    return pl.pallas_call(
        _compress_kernel,
        grid=(B, KVW // 128),
        in_specs=[
            pl.BlockSpec((None, None, S, 128), lambda b, lp: (0, lp, b, 0)),
            pl.BlockSpec((None, None, S, 128), lambda b, lp: (1, lp, b, 0)),
            _const_spec((2, CMP_STRIDE, 128)),
            _const_spec((2, CMP_LEN * HEAD_DIM, CMP_HID)),
            _const_spec((2, CMP_HID, HEAD_DIM)),
            _const_spec((1, HEAD_DIM)),
        ],
        out_specs=[
            pl.BlockSpec((None, 2, nch, HEAD_DIM), lambda b, lp: (b, lp, 0, 0)),
            pl.BlockSpec((None, 2, HEAD_DIM, nch), lambda b, lp: (b, lp, 0, 0)),
        ],
        out_shape=[
            jax.ShapeDtypeStruct((B, N_KV, nch, HEAD_DIM), BF16),
            jax.ShapeDtypeStruct((B, N_KV, HEAD_DIM, nch), BF16),
        ],
        compiler_params=_params(("parallel", "parallel")),
        name="compress",
    )(zkv, zkv, pos, w1, w2, gain)


def _spread_heads(x):
    y = x + pltpu.roll(x, 2 * Q_BLK, axis=1)
    return y + pltpu.roll(y, Q_BLK, axis=1)


def _swap_heads(x):
    n = x.shape[0]
    lane = _iota((n, 128), 1)
    halves = []
    for pr in range(x.shape[1] // 128):
        slab = jnp.concatenate([x[:, pr * 128:(pr + 1) * 128], jnp.zeros((128 - n, 128), x.dtype)], axis=0)
        tr = slab.T
        halves.append(jnp.where(lane < n, tr[0:n], pltpu.roll(tr[n:2 * n], n, axis=1)))
    return jnp.concatenate(halves, axis=1)


def _nsa_kernel(*refs, nsel, qpb):
    per_chunk = KCH // SEL_LEN
    for k in range(qpb):
        _nsa_block(k, "head", None, *refs, nsel=nsel, qpb=qpb)

    def step(i, carry):
        for k in range(qpb):
            _nsa_block(k, "step", i, *refs, nsel=nsel, qpb=qpb)
        return carry

    common = pl.program_id(1) * (qpb // per_chunk)
    lax.fori_loop(0, common, step, 0)
    for k in range(qpb):
        for extra in range(k // per_chunk):
            _nsa_block(k, "step", common + extra, *refs, nsel=nsel, qpb=qpb)
    for k in range(qpb):
        _nsa_block(k, "tail", None, *refs, nsel=nsel, qpb=qpb)


def _nsa_block(k, phase, step_i, zq_ref, qg_ref, kc_ref, vct_ref, ks_ref, kw_ref, vst_ref, vwt_ref, ng_ref, o_ref,
               qtp_ref, oc_ref, ow_ref, bias_ref, ms_ref, accs_ref, mw_ref, accw_ref, sbuf_ref, wbuf_ref,
               gates_ref, scs_ref, wbias_ref, *, nsel, qpb):
    qtp_ref, oc_ref, ow_ref, bias_ref, ms_ref, accs_ref, sbuf_ref, gates_ref = (
        r.at[k] for r in (qtp_ref, oc_ref, ow_ref, bias_ref, ms_ref, accs_ref, sbuf_ref, gates_ref))
    qi = pl.program_id(1) * qpb + k
    q_rows = pl.ds(k * Q_BLK, Q_BLK)
    s0 = qi * Q_BLK
    cur = qi
    ncmp = kc_ref.shape[1]
    per_chunk = KCH // SEL_LEN
    groups = range(N_KV)
    n_chunks = qi // per_chunk + 1
    cur_slab = k % per_chunk
    cut_rows = slice(cur_slab * SEL_LEN, (cur_slab + 1) * SEL_LEN)
    last_slabs = per_chunk // 2 if cur_slab < per_chunk // 2 else per_chunk

    def scores(k_ref, i):
        k0 = pl.multiple_of(i * KCH, KCH)
        return [_dot(k_ref[g // 2, pl.ds(k0, KCH), :], qtp_ref[g]) for g in groups]

    def park(buf_ref, vals):
        for g in groups:
            buf_ref[g] = vals[g]

    def cut(buf_ref, tri):
        for g in groups:
            buf_ref[g, cut_rows, :] = buf_ref[g, cut_rows, :] + tri

    def update(g, blocks, shifts, vt, m_ref, acc_ref):
        m = m_ref[g]
        part = None
        for blk, sh in zip(blocks, shifts):
            top8 = jnp.max(blk.reshape(SEL_LEN // 8, 8, GW), axis=0) + sh
            part = top8 if part is None else jnp.maximum(part, top8)
        mnew = jnp.maximum(m, jnp.max(part, axis=0, keepdims=True))
        e = jnp.concatenate([jnp.exp2(blk + (sh - mnew)) for blk, sh in zip(blocks, shifts)], axis=0).astype(BF16)
        acc_ref[g] = jnp.exp2(m - mnew) * acc_ref[g] + _dot(vt, e)
        m_ref[g] = mnew

    def slabs(buf_ref, g, ns):
        return [buf_ref[g, nb * SEL_LEN:(nb + 1) * SEL_LEN, :] for nb in range(ns)]

    def sel_softmax(i, ns=per_chunk):
        for g in groups:
            shifts = [bias_ref[g, pl.ds(i * per_chunk + nb, 1), :] for nb in range(ns)]
            update(g, slabs(sbuf_ref, g, ns), shifts, vst_ref[i, g, :, 0:ns * SEL_LEN], ms_ref, accs_ref)

    key_row = _iota((SEL_LEN, GW), 0)
    in_blk_t = _iota((SEL_LEN, GW), 1) % Q_BLK
    causal_tri = jnp.where(key_row <= in_blk_t, 0.0, NEG)

    if phase == "step":
        nxt = scores(ks_ref, step_i + 1)
        sel_softmax(step_i)
        park(sbuf_ref, nxt)
        return

    if phase == "tail":
        cut(sbuf_ref, causal_tri)
        sel_softmax(n_chunks - 1, last_slabs)
        outs = []
        for g in groups:
            o_s = accs_ref[g, 0:HEAD_DIM, :] * (1.0 / jnp.maximum(accs_ref[g, HEAD_DIM:HEAD_DIM + 1, :], 1e-30))
            gate = [gates_ref[g, c:c + 1, :] for c in range(3)]
            o_t = gate[0] * oc_ref[g] + gate[1] * o_s + gate[2] * ow_ref[g]
            outs.append(_swap_heads(o_t))
        for g in groups:
            o_ref[g, q_rows, :] = outs[g].astype(o_ref.dtype)
        return

    lane_t = s0 + (_iota((1, GW), 1) % Q_BLK)
    lane_grp = _iota((1, GW), 1) // Q_BLK
    ov_n = _iota((nsel, ncmp), 0) * SEL_LEN
    ov_c = _iota((nsel, ncmp), 1) * CMP_STRIDE
    overlap = jnp.where((ov_c < ov_n + SEL_LEN) & (ov_c + CMP_LEN > ov_n), 1.0, 0.0).astype(BF16)


    ok_c = _iota((ncmp, GW), 0) * CMP_STRIDE + (CMP_LEN - 1) <= lane_t
    ngt = jnp.concatenate([ng_ref[q_rows, :], jnp.zeros((128 - Q_BLK, NG_PAD), F32)], axis=0).T
    ngt_hi = pltpu.roll(ngt, Q_BLK, axis=1)
    low_half = _iota((1, 128), 1) < Q_BLK
    for g in groups:
        for c in range(3):
            r = c * N_HEADS + g * HPG
            pairs = [jnp.where(low_half, ngt[r + j:r + j + 1], ngt_hi[r + j + 1:r + j + 2]) for j in (0, 2)]
            gates_ref[g, c:c + 1, :] = jnp.concatenate(pairs, axis=1)

    qts = []
    for g in groups:
        xt = _swap_heads(zq_ref[g, q_rows, :])
        ssq = jnp.sum(xt * xt, axis=0, keepdims=True)
        qts.append((xt * lax.rsqrt(ssq * (1.0 / HEAD_DIM) + EPS) * qg_ref[...]).astype(BF16))
    scs = []
    for g in groups:
        off = (g % 2) * HEAD_DIM
        qtp_ref[g, off:off + HEAD_DIM, :] = qts[g]
        qtp_ref[g, HEAD_DIM - off:2 * HEAD_DIM - off, :] = jnp.zeros((HEAD_DIM, GW), BF16)
        scs.append(_dot(kc_ref[g], qts[g]))

    n_win = WINDOW // KCH + 1
    win_ids = [jnp.maximum(n_chunks - (n_win - j), 0) for j in range(n_win)]
    win_dead = [jnp.where(n_chunks >= n_win - j, 0.0, NEG) for j in range(n_win)]
    park(scs_ref, scs)
    park(sbuf_ref, scores(ks_ref, 0))
    park(wbuf_ref, scores(kw_ref, win_ids[0]))

    expired_tri = jnp.where(key_row > in_blk_t, 0.0, NEG)
    win_blocks = WINDOW // SEL_LEN
    n_all = _iota((nsel, GW), 0)
    wbias_ref[...] = jnp.where((n_all >= cur - win_blocks) & (n_all <= cur), 0.0, NEG)
    cut(wbuf_ref, jnp.where(cur >= win_blocks, expired_tri, 0.0))

    ms_ref[...] = jnp.full(ms_ref.shape, 0.5 * NEG, F32)
    mw_ref[...] = jnp.full(mw_ref.shape, 0.5 * NEG, F32)
    accs_ref[...] = jnp.zeros(accs_ref.shape, F32)
    accw_ref[...] = jnp.zeros(accw_ref.shape, F32)

    def win_softmax(i, dead, lo=0, hi=per_chunk):
        shifts = [wbias_ref[pl.ds(i * per_chunk + nb, 1), :] + dead for nb in range(lo, hi)]
        for g in groups:
            blocks = [wbuf_ref[g, nb * SEL_LEN:(nb + 1) * SEL_LEN, :] for nb in range(lo, hi)]
            update(g, blocks, shifts, vwt_ref[i, g, :, lo * SEL_LEN:hi * SEL_LEN], mw_ref, accw_ref)

    first_lo = per_chunk // 2 if cur_slab >= per_chunk // 2 else 0
    for j in range(n_win - 1):
        nxt = scores(kw_ref, win_ids[j + 1])
        win_softmax(win_ids[j], win_dead[j], lo=first_lo if j == 0 else 0)
        park(wbuf_ref, nxt)

    imp_all = jnp.zeros((nsel, GW), F32)
    some_c = lane_t >= CMP_LEN - 1
    ones_rows = jnp.ones((16, ncmp), BF16)
    for g in groups:
        scm = jnp.where(ok_c, scs_ref[g], NEG)
        ec = jnp.exp2(scm - jnp.max(scm, axis=0, keepdims=True))
        both = _dot(jnp.concatenate([vct_ref[g], overlap, ones_rows], axis=0), ec.astype(BF16))
        denom = both[HEAD_DIM + nsel:HEAD_DIM + nsel + 1]
        both = both[0:HEAD_DIM + nsel] * jnp.where(some_c, 1.0 / jnp.maximum(denom, 1e-30), 0.0)
        oc_ref[g] = both[0:HEAD_DIM]
        imp = _spread_heads(both[HEAD_DIM:HEAD_DIM + nsel])
        imp_all = jnp.where(lane_grp == g, imp, imp_all)

    n_io = _iota((nsel, GW), 0)
    forced = (n_io == 0) | (n_io == cur) | (n_io == cur - 1)
    valid = n_io <= cur
    score = jnp.where(forced, FORCE_SCORE, imp_all)
    score = jnp.where(valid, score, -1.0)
    rows8 = [score[r:r + 8] for r in range(0, nsel, 8)]
    ranks = [jnp.zeros((8, GW), F32) for _ in rows8]
    sub = _iota((8, GW), 0)
    for m in range(nsel):
        rowv = score[m:m + 1, :]
        for k, blk in enumerate(rows8):
            ge = jnp.where(rowv >= blk, 1.0, 0.0)
            gt = jnp.where(rowv > blk, 1.0, 0.0)
            if 8 * k > m:
                inc = ge
            elif 8 * k + 7 < m:
                inc = gt
            else:
                inc = jnp.where(sub + 8 * k > m, ge, gt)
            ranks[k] = ranks[k] + inc
    rank = jnp.concatenate(ranks, axis=0)
    sel_all = jnp.where((rank < float(min(SEL_TOPN, nsel))) & valid, 1.0, 0.0)
    for g in groups:
        mine = jnp.where(lane_grp == g, sel_all, 0.0)
        bias_ref[g] = (_spread_heads(mine) - 1.0) * (-NEG)

    cut(wbuf_ref, causal_tri)
    win_softmax(win_ids[n_win - 1], win_dead[n_win - 1], hi=last_slabs)
    for g in groups:
        ow_ref[g] = accw_ref[g, 0:HEAD_DIM, :] * (1.0 / jnp.maximum(accw_ref[g, HEAD_DIM:HEAD_DIM + 1, :], 1e-30))


def _nsa(zq, qgain, kc, vct, kn, vt, ng, *, B, S):
    T = B * S
    qpb = 2 * (KCH // SEL_LEN)
    assert S % (Q_BLK * qpb) == 0
    nq = S // (Q_BLK * qpb)
    nch = S // KCH
    ncmp = S // CMP_STRIDE
    nsel = S // SEL_LEN
    tok = lambda b, i: (0, b * nq + i, 0)
    return pl.pallas_call(
        functools.partial(_nsa_kernel, nsel=nsel, qpb=qpb),
        grid=(B, nq),
        in_specs=[
            pl.BlockSpec((N_KV, Q_BLK * qpb, GW), tok),
            _const_spec((HEAD_DIM, GW)),
            pl.BlockSpec((None, N_KV, ncmp, HEAD_DIM), lambda b, i: (b, 0, 0, 0)),
            pl.BlockSpec((None, N_KV, HEAD_DIM, ncmp), lambda b, i: (b, 0, 0, 0)),
            pl.BlockSpec((None, None, 2, S, 128), lambda b, i: (0, b, 0, 0, 0)),
            pl.BlockSpec((None, None, 2, S, 128), lambda b, i: (1, b, 0, 0, 0)),
            pl.BlockSpec((None, None, nch, N_KV, VROWS, KCH), lambda b, i: (0, b, 0, 0, 0, 0)),
            pl.BlockSpec((None, None, nch, N_KV, VROWS, KCH), lambda b, i: (1, b, 0, 0, 0, 0)),
            pl.BlockSpec((Q_BLK * qpb, NG_PAD), lambda b, i: (b * nq + i, 0)),
        ],
        out_specs=pl.BlockSpec((N_KV, Q_BLK * qpb, GW), tok),
        out_shape=jax.ShapeDtypeStruct((N_KV, T, GW), BF16),
        scratch_shapes=[
            pltpu.VMEM((qpb, N_KV, 2 * HEAD_DIM, GW), BF16),
            pltpu.VMEM((qpb, N_KV, HEAD_DIM, GW), F32),
            pltpu.VMEM((qpb, N_KV, HEAD_DIM, GW), F32),
            pltpu.VMEM((qpb, N_KV, nsel, GW), F32),
            pltpu.VMEM((qpb, N_KV, 1, GW), F32),
            pltpu.VMEM((qpb, N_KV, VROWS, GW), F32),
            pltpu.VMEM((N_KV, 1, GW), F32),
            pltpu.VMEM((N_KV, VROWS, GW), F32),
            pltpu.VMEM((qpb, N_KV, KCH, GW), F32),
            pltpu.VMEM((N_KV, KCH, GW), F32),
            pltpu.VMEM((qpb, N_KV, 8, GW), F32),
            pltpu.VMEM((N_KV, ncmp, GW), F32),
            pltpu.VMEM((nsel, GW), F32),
        ],
        compiler_params=_params(("parallel", "arbitrary")),
        name="nsa",
    )(zq, qgain, kc, vct, kn, kn, vt, vt, ng)


def _merge_kernel(x_ref, ya_ref, o_ref_in, mg_ref, wn_ref, wo_ref, out_ref):
    yb = _dot(o_ref_in[0], wn_ref[0:GW, :])
    for g in range(1, N_KV):
        yb = yb + _dot(o_ref_in[g], wn_ref[g * GW:(g + 1) * GW, :])
    tm = x_ref.shape[0]
    ya = jnp.concatenate([ya_ref[ck].reshape(tm, 128) for ck in range(D_MODEL // 128)], axis=1)
    mixed = (mg_ref[:, 0:D_MODEL].astype(F32) * ya
             + mg_ref[:, D_MODEL:2 * D_MODEL].astype(F32) * yb)
    out_ref[...] = x_ref[...] + _dot(mixed.astype(BF16), wo_ref[...])


def _merge(x2d, ya, o, mg, wn, wo, *, B, S, tm=512):
    ni = S // tm
    tok = lambda b, i: (b * ni + i, 0)
    return pl.pallas_call(
        _merge_kernel,
        grid=(B, ni),
        in_specs=[
            pl.BlockSpec((tm, D_MODEL), tok),
            pl.BlockSpec((D_MODEL // 128, tm // 8, None, 8, 128), lambda b, i: (0, i, b, 0, 0)),
            pl.BlockSpec((N_KV, tm, GW), lambda b, i: (0, b * ni + i, 0)),
            pl.BlockSpec((tm, 2 * D_MODEL), tok),
            _const_spec((N_HEADS * HEAD_DIM, D_MODEL)),
            _const_spec((D_MODEL, D_MODEL)),
        ],
        out_specs=pl.BlockSpec((tm, D_MODEL), tok),
        out_shape=jax.ShapeDtypeStruct((B * S, D_MODEL), F32),
        compiler_params=_params(("parallel", "parallel")),
        name="merge",
    )(x2d, ya, o, mg, wn, wo)


def _ffn_kernel(x_ref, g_ref, wg_ref, wu_ref, wd_ref, o_ref, h_ref, *, tf):
    x = x_ref[...]
    y = x * lax.rsqrt(jnp.mean(x * x, axis=-1, keepdims=True) + EPS) * g_ref[...]
    h_ref[...] = y.astype(BF16)
    o_ref[...] = x
    for j in range(D_FF // tf):
        h = h_ref[...]
        sl = slice(j * tf, (j + 1) * tf)
        act = jax.nn.silu(_dot(h, wg_ref[:, sl])) * _dot(h, wu_ref[:, sl])
        o_ref[...] += _dot(act.astype(BF16), wd_ref[sl, :])


def _ffn(x2d, gain, wg, wu, wd, *, tm=512, tf=256):
    T = x2d.shape[0]
    assert T % tm == 0 and D_FF % tf == 0
    once = dict(pipeline_mode=pl.Buffered(1))
    return pl.pallas_call(
        functools.partial(_ffn_kernel, tf=tf),
        grid=(T // tm,),
        in_specs=[
            pl.BlockSpec((tm, D_MODEL), lambda i: (i, 0)),
            _const_spec((1, D_MODEL)),
            pl.BlockSpec((D_MODEL, D_FF), lambda i: (0, 0), **once),
            pl.BlockSpec((D_MODEL, D_FF), lambda i: (0, 0), **once),
            pl.BlockSpec((D_FF, D_MODEL), lambda i: (0, 0), **once),
        ],
        out_specs=pl.BlockSpec((tm, D_MODEL), lambda i: (i, 0)),
        out_shape=jax.ShapeDtypeStruct((T, D_MODEL), F32),
        scratch_shapes=[pltpu.VMEM((tm, D_MODEL), BF16)],
        compiler_params=_params(("parallel",)),
        name="ffn",
    )(x2d, gain, wg, wu, wd)


def _layer(x, p):
    B, S, _ = x.shape
    T = B * S
    dh = HEAD_DIM
    row = lambda v: v.reshape(1, -1)
    w_in = p["w_in"]
    n_ng = 3 * N_HEADS
    o_ng = 2 * D_RNN + N_HEADS * dh + 6 * KVW
    w_packed = jnp.concatenate(
        [w_in[:, :o_ng], jnp.pad(w_in[:, o_ng:o_ng + n_ng], ((0, 0), (0, NG_PAD - n_ng))), w_in[:, o_ng + n_ng:]],
        axis=1).astype(BF16)
    rx, gy, zq, zkv, ng, mg = _inproj(x, row(p["norm1"]), w_packed)

    slabs = lambda a: a.reshape(a.shape[0], S * B, 128)
    ya = _rglru(slabs(rx), slabs(gy), p["conv_w"], row(p["conv_b"]),
                p["rg_wa"].astype(BF16), row(p["rg_ba"]), p["rg_wi"].astype(BF16), row(p["rg_bi"]),
                row(p["rg_lambda"]), p["w_rg_out"].astype(BF16), nb=B)
    ya = ya.reshape(D_MODEL // 128, S // 8, B, 8, 128)

    kgains = jnp.tile(p["k_norm"][1:3], (1, N_KV)).reshape(2, 1, KVW)
    kn, vt = _kvprep(zkv, kgains, B=B, S=S)
    pos = jnp.stack([p["cmp_pos_k"], p["cmp_pos_v"]]).reshape(2, CMP_STRIDE, 2 * dh)
    w1 = jnp.stack([p["cmp_k_w1"], p["cmp_v_w1"]]).astype(BF16)
    w2 = jnp.stack([p["cmp_k_w2"], p["cmp_v_w2"]]).astype(BF16)
    kc, vct = _compress(zkv, pos, w1, w2, row(p["k_norm"][0]), B=B, S=S)
    qgain = jnp.broadcast_to((p["q_norm"] * (dh ** -0.5 * LOG2E))[:, None], (dh, GW))
    o = _nsa(zq, qgain, kc, vct, kn, vt, ng, B=B, S=S)

    x1 = _merge(x.reshape(T, D_MODEL), ya, o, mg, p["w_nsa_out"].astype(BF16), p["w_o"].astype(BF16), B=B, S=S)
    x2 = _ffn(x1, row(p["norm2"]), p["w_gate"].astype(BF16), p["w_up"].astype(BF16), p["w_down"].astype(BF16))
    return x2.reshape(B, S, D_MODEL)


def kernel(x, norm1, w_in, conv_w, conv_b, rg_wa, rg_ba, rg_wi, rg_bi, rg_lambda, q_norm, k_norm, cmp_pos_k,
           cmp_pos_v, cmp_k_w1, cmp_k_w2, cmp_v_w1, cmp_v_w2, w_rg_out, w_nsa_out, w_o, norm2, w_gate, w_up, w_down):
    params = dict(norm1=norm1, w_in=w_in, conv_w=conv_w, conv_b=conv_b, rg_wa=rg_wa, rg_ba=rg_ba, rg_wi=rg_wi,
                  rg_bi=rg_bi, rg_lambda=rg_lambda, q_norm=q_norm, k_norm=k_norm, cmp_pos_k=cmp_pos_k,
                  cmp_pos_v=cmp_pos_v, cmp_k_w1=cmp_k_w1, cmp_k_w2=cmp_k_w2, cmp_v_w1=cmp_v_w1, cmp_v_w2=cmp_v_w2,
                  w_rg_out=w_rg_out, w_nsa_out=w_nsa_out, w_o=w_o, norm2=norm2, w_gate=w_gate, w_up=w_up,
                  w_down=w_down)
    for l in range(norm1.shape[0]):
        x = _layer(x, {k: v[l] for k, v in params.items()})
    return x
```

```python
import functools

import jax
import jax.numpy as jnp
from jax import lax
from jax.experimental import pallas as pl
from jax.experimental.pallas import tpu as pltpu

D_MODEL = 1024
D_RNN = 1024
RG_BLOCKS = 4
RG_BW = D_RNN // RG_BLOCKS
CONV_W = 4
RG_C = 8.0
N_HEADS = 16
N_KV = 4
HEAD_DIM = 64
HPG = N_HEADS // N_KV
CMP_LEN = 32
CMP_STRIDE = 16
CMP_HID = 256
SEL_LEN = 64
SEL_TOPN = 16
WINDOW = 512
Q_BLK = 64
D_FF = 2816
EPS = 1e-6
FORCE_SCORE = 1e6
NEG = -1e30
LOG2E = 1.4426950408889634

KCH = 256
GW = HPG * HEAD_DIM
KVW = N_KV * HEAD_DIM
VROWS = 80
NG_PAD = 128

F32 = jnp.float32
BF16 = jnp.bfloat16

VMEM_LIMIT = 56 * 1024 * 1024


def _params(sem):
    return pltpu.CompilerParams(dimension_semantics=sem, vmem_limit_bytes=VMEM_LIMIT)


def _dot(a, b):
    return jnp.dot(a, b, preferred_element_type=F32)


def _dot_nt(a, b):
    return lax.dot_general(a, b, (((1,), (1,)), ((), ())), preferred_element_type=F32)


def _split(x):
    hi = x.astype(BF16)
    lo = (x - hi.astype(F32)).astype(BF16)
    return hi, lo


def _iota(shape, dim):
    return lax.broadcasted_iota(jnp.int32, shape, dim)


def _const_spec(shape):
    return pl.BlockSpec(shape, lambda *_: (0,) * len(shape))


_O_RY = D_RNN
_O_Q = 2 * D_RNN
_O_KV = _O_Q + N_HEADS * HEAD_DIM
_O_NG = _O_KV + 6 * KVW
_O_MG = _O_NG + NG_PAD
_W_COLS = _O_MG + 2 * D_MODEL


def _inproj_kernel(x_ref, g_ref, w_ref, rx_ref, gy_ref, zq_ref, zkv_ref, ng_ref, mg_ref, h_ref):
    x = x_ref[...]
    y = x * lax.rsqrt(jnp.mean(x * x, axis=-1, keepdims=True) + EPS) * g_ref[...]
    h_ref[...] = y.astype(BF16)

    def proj(c0, width):
        return _dot(h_ref[...], w_ref[:, c0:c0 + width])

    cw = 256
    tm = x_ref.shape[0]
    for c in range(D_RNN // cw):
        rx = proj(c * cw, cw)
        gy = jax.nn.gelu(proj(_O_RY + c * cw, cw))
        for hf in range(cw // 128):
            rx_ref[c * (cw // 128) + hf] = rx[:, hf * 128:(hf + 1) * 128].reshape(tm // 8, 8, 128)
            gy_ref[c * (cw // 128) + hf] = gy[:, hf * 128:(hf + 1) * 128].reshape(tm // 8, 8, 128)
    for g in range(N_KV):
        zq_ref[g] = proj(_O_Q + g * GW, GW)
    for s in range(6):
        z = proj(_O_KV + s * KVW, KVW)
        for pr in range(KVW // 128):
            zkv_ref[s, pr] = z[:, pr * 128:(pr + 1) * 128]
    ng_ref[...] = jax.nn.sigmoid(proj(_O_NG, NG_PAD))
    for c in range(2 * D_MODEL // cw):
        mg_ref[:, c * cw:(c + 1) * cw] = jax.nn.sigmoid(proj(_O_MG + c * cw, cw)).astype(mg_ref.dtype)


def _inproj(x, gain, w, *, tm=256):
    B, S, D = x.shape
    T = B * S
    ni = S // tm
    nck = D_RNN // 128
    tok = lambda b, i: (b * ni + i, 0)
    slab_spec = pl.BlockSpec((nck, tm // 8, None, 8, 128), lambda b, i: (0, i, b, 0, 0))
    slab_shape = jax.ShapeDtypeStruct((nck, S // 8, B, 8, 128), F32)
    return pl.pallas_call(
        _inproj_kernel,
        grid=(B, ni),
        in_specs=[
            pl.BlockSpec((None, tm, D), lambda b, i: (b, i, 0)),
            _const_spec((1, D)),
            pl.BlockSpec((D, _W_COLS), lambda b, i: (0, 0), pipeline_mode=pl.Buffered(1)),
        ],
        out_specs=[
            slab_spec,
            slab_spec,
            pl.BlockSpec((N_KV, tm, GW), lambda b, i: (0, b * ni + i, 0)),
            pl.BlockSpec((6, KVW // 128, tm, 128), lambda b, i: (0, 0, b * ni + i, 0)),
            pl.BlockSpec((tm, NG_PAD), tok),
            pl.BlockSpec((tm, 2 * D_MODEL), tok),
        ],
        out_shape=[
            slab_shape,
            slab_shape,
            jax.ShapeDtypeStruct((N_KV, T, GW), F32),
            jax.ShapeDtypeStruct((6, KVW // 128, T, 128), F32),
            jax.ShapeDtypeStruct((T, NG_PAD), F32),
            jax.ShapeDtypeStruct((T, 2 * D_MODEL), BF16),
        ],
        scratch_shapes=[pltpu.VMEM((tm, D), BF16)],
        compiler_params=_params(("parallel", "parallel")),
        name="inproj",
    )(x, gain, w)


def _rglru_kernel(x_ref, gy_ref, cw_ref, cb_ref, wa_ref, ba_ref, wi_ref, bi_ref, lam_ref, wo_ref,
                  o_ref, xb_ref, gyb_ref, a_ref, u_ref, hb_ref, hs_ref, yb_ref, *, nb, tt):
    R = nb * tt
    halo = (CONV_W - 1) * nb
    step = pl.program_id(0)
    nck = D_RNN // 128

    @pl.when(step == 0)
    def _():
        xb_ref[0:halo, :] = jnp.zeros((halo, D_RNN), F32)
        hs_ref[...] = jnp.zeros((nb, D_RNN), F32)

    def slab_rows(t8, t_lo):
        return pl.ds(t8 * (8 * nb) + t_lo, nb, stride=8)

    def stage_in(t8, carry):
        for t_lo in range(8):
            r0 = pl.multiple_of((t8 * 8 + t_lo) * nb, nb)
            for ck in range(nck):
                cols = slice(ck * 128, (ck + 1) * 128)
                xb_ref[pl.ds(halo + r0, nb), cols] = x_ref[ck, slab_rows(t8, t_lo), :]
                gyb_ref[pl.ds(r0, nb), cols] = gy_ref[ck, slab_rows(t8, t_lo), :]
        return carry

    lax.fori_loop(0, tt // 8, stage_in, 0)
    xr = cb_ref[...] + cw_ref[0:1, :] * xb_ref[0:R, :]
    for k in range(1, CONV_W):
        xr = xr + cw_ref[k:k + 1, :] * xb_ref[k * nb:k * nb + R, :]
    tail = xb_ref[R:R + halo, :]
    xb_ref[0:halo, :] = tail

    xrb = xr.astype(BF16)
    sp = jax.nn.softplus(-lam_ref[...])
    row = _iota((R, RG_BW), 0)
    first = (row < nb) & (step == 0)
    for n in range(RG_BLOCKS):
        sl = slice(n * RG_BW, (n + 1) * RG_BW)
        xn = xrb[:, sl]
        r = jax.nn.sigmoid(_dot(xn, wa_ref[n]) + ba_ref[:, sl])
        ig = jax.nn.sigmoid(_dot(xn, wi_ref[n]) + bi_ref[:, sl])
        log_a = (-RG_C) * r * sp[:, sl]
        a = jnp.exp(log_a)
        mult = jnp.sqrt(1.0 - a * a)
        mult = jnp.where(first, 1.0, mult)
        a_ref[:, sl] = a
        u_ref[:, sl] = mult * ig * xr[:, sl]

    def body(t, h):
        r0 = pl.multiple_of(t * nb, nb)
        h = a_ref[pl.ds(r0, nb), :] * h + u_ref[pl.ds(r0, nb), :]
        hb_ref[pl.ds(r0, nb), :] = h
        return h

    h = lax.fori_loop(0, tt, body, hs_ref[...], unroll=8)
    hs_ref[...] = h
    yb_ref[...] = _dot((hb_ref[...] * gyb_ref[...]).astype(BF16), wo_ref[...])

    def stage_out(t8, carry):
        for t_lo in range(8):
            r0 = pl.multiple_of((t8 * 8 + t_lo) * nb, nb)
            for ck in range(nck):
                o_ref[ck, slab_rows(t8, t_lo), :] = yb_ref[pl.ds(r0, nb), ck * 128:(ck + 1) * 128]
        return carry

    lax.fori_loop(0, tt // 8, stage_out, 0)


def _rglru(rx, gy, conv_w, conv_b, wa, ba, wi, bi, lam, wo, *, nb, tt=32):
    nck, rows, _ = rx.shape
    S = rows // nb
    assert S % tt == 0 and tt % 8 == 0
    R = nb * tt
    halo = (CONV_W - 1) * nb
    return pl.pallas_call(
        functools.partial(_rglru_kernel, nb=nb, tt=tt),
        grid=(S // tt,),
        in_specs=[
            pl.BlockSpec((nck, R, 128), lambda s: (0, s, 0)),
            pl.BlockSpec((nck, R, 128), lambda s: (0, s, 0)),
            _const_spec((CONV_W, D_RNN)),
            _const_spec((1, D_RNN)),
            _const_spec((RG_BLOCKS, RG_BW, RG_BW)),
            _const_spec((1, D_RNN)),
            _const_spec((RG_BLOCKS, RG_BW, RG_BW)),
            _const_spec((1, D_RNN)),
            _const_spec((1, D_RNN)),
            _const_spec((D_RNN, D_MODEL)),
        ],
        out_specs=pl.BlockSpec((D_MODEL // 128, R, 128), lambda s: (0, s, 0)),
        out_shape=jax.ShapeDtypeStruct((D_MODEL // 128, rows, 128), F32),
        scratch_shapes=[
            pltpu.VMEM((R + halo, D_RNN), F32),
            pltpu.VMEM((R, D_RNN), F32),
            pltpu.VMEM((R, D_RNN), F32),
            pltpu.VMEM((R, D_RNN), F32),
            pltpu.VMEM((R, D_RNN), F32),
            pltpu.VMEM((nb, D_RNN), F32),
            pltpu.VMEM((R, D_MODEL), F32),
        ],
        compiler_params=_params(("arbitrary",)),
        name="rglru",
    )(rx, gy, conv_w, conv_b, wa, ba, wi, bi, lam, wo)


def _kvprep_kernel(zk_ref, zv_ref, g_ref, kn_ref, vt_ref):
    rows = zk_ref.shape[1]
    x = jnp.concatenate([zk_ref[0], zk_ref[1]], axis=1)
    seg = jnp.where(_iota((KVW, KVW), 0) // HEAD_DIM == _iota((KVW, KVW), 1) // HEAD_DIM, 1.0, 0.0).astype(BF16)
    hi, lo = _split(x * x)
    ssq = _dot(jnp.concatenate([hi, lo], axis=0), seg)
    kn = (x * lax.rsqrt((ssq[0:rows] + ssq[rows:2 * rows]) * (1.0 / HEAD_DIM) + EPS) * g_ref[...]).astype(BF16)
    for pr in range(KVW // 128):
        kn_ref[pr] = kn[:, pr * 128:(pr + 1) * 128]
    eye = jnp.where(_iota((KVW, KVW), 0) == _iota((KVW, KVW), 1), 1.0, 0.0).astype(BF16)
    pad = jnp.where(_iota((VROWS - HEAD_DIM, KCH), 0) == 0, 1.0, 0.0).astype(BF16)
    for c in range(rows // KCH):
        sl = slice(c * KCH, (c + 1) * KCH)
        v = jnp.concatenate([zv_ref[0, sl, :], zv_ref[1, sl, :]], axis=1).astype(BF16)
        vt = _dot_nt(eye, v).astype(BF16)
        for g in range(N_KV):
            vt_ref[c, g, 0:HEAD_DIM, :] = vt[g * HEAD_DIM:(g + 1) * HEAD_DIM]
            vt_ref[c, g, HEAD_DIM:VROWS, :] = pad


def _kvprep(zkv, gains, *, B, S, cpb=4):
    nch = S // KCH
    assert nch % cpb == 0
    nblk = nch // cpb
    rows = cpb * KCH
    return pl.pallas_call(
        _kvprep_kernel,
        grid=(2, B, nblk),
        in_specs=[
            pl.BlockSpec((None, KVW // 128, rows, 128), lambda w, b, i: (2 + 2 * w, 0, b * nblk + i, 0)),
            pl.BlockSpec((None, KVW // 128, rows, 128), lambda w, b, i: (3 + 2 * w, 0, b * nblk + i, 0)),
            pl.BlockSpec((None, 1, KVW), lambda w, b, i: (w, 0, 0)),
        ],
        out_specs=[
            pl.BlockSpec((None, None, KVW // 128, rows, 128), lambda w, b, i: (w, b, 0, i, 0)),
            pl.BlockSpec((None, None, cpb, N_KV, VROWS, KCH), lambda w, b, i: (w, b, i, 0, 0, 0)),
        ],
        out_shape=[
            jax.ShapeDtypeStruct((2, B, KVW // 128, S, 128), BF16),
            jax.ShapeDtypeStruct((2, B, nch, N_KV, VROWS, KCH), BF16),
        ],
        compiler_params=_params(("parallel", "parallel", "parallel")),
        name="kvprep",
    )(zkv, zkv, gains)


def _compress_kernel(zk_ref, zv_ref, pos_ref, w1_ref, w2_ref, g_ref, kc_ref, vct_ref):
    nch = zk_ref.shape[0] // CMP_STRIDE
    lane = _iota((nch, 128), 1)
    npair = CMP_STRIDE // 2
    eye = jnp.where(_iota((HEAD_DIM, HEAD_DIM), 0) == _iota((HEAD_DIM, HEAD_DIM), 1), 1.0, 0.0).astype(BF16)
    for kind, z_ref in enumerate((zk_ref, zv_ref)):
        for ge in range(2):
            combs = []
            for p in range(npair):
                a = z_ref[pl.ds(2 * p, nch, stride=CMP_STRIDE), :]
                b = z_ref[pl.ds(2 * p + 1, nch, stride=CMP_STRIDE), :]
                if ge == 0:
                    combs.append(jnp.where(lane < HEAD_DIM, a, pltpu.roll(b, HEAD_DIM, axis=1)))
                else:
                    combs.append(jnp.where(lane < HEAD_DIM, pltpu.roll(a, HEAD_DIM, axis=1), b))
            halves = []
            half_w = npair * 128
            for half in range(2):
                lhs = jnp.concatenate(
                    [(combs[p] + pos_ref[kind, half * npair + p:half * npair + p + 1, :]).astype(BF16)
                     for p in range(npair)], axis=1)
                halves.append(_dot(lhs, w1_ref[kind, half * half_w:(half + 1) * half_w, :]))
            hid = halves[0] + pltpu.roll(halves[1], nch - 1, axis=0)
            hid = jax.nn.gelu(hid)
            out = _dot(hid.astype(BF16), w2_ref[kind])
            if kind == 0:
                out = out * lax.rsqrt(jnp.mean(out * out, axis=-1, keepdims=True) + EPS) * g_ref[...]
            res = jnp.where(_iota(out.shape, 0) < nch - 1, out, 0.0).astype(BF16)
            if kind == 0:
                kc_ref[ge] = res
            else:
                vct_ref[ge] = _dot_nt(eye, res).astype(BF16)


def _compress(zkv, pos, w1, w2, gain, *, B, S):
    nch = S // CMP_STRIDE
    return pl.pallas_call(
        _compress_kernel,
        grid=(B, KVW // 128),
        in_specs=[
            pl.BlockSpec((None, None, S, 128), lambda b, lp: (0, lp, b, 0)),
            pl.BlockSpec((None, None, S, 128), lambda b, lp: (1, lp, b, 0)),
            _const_spec((2, CMP_STRIDE, 128)),
            _const_spec((2, CMP_LEN * HEAD_DIM, CMP_HID)),
            _const_spec((2, CMP_HID, HEAD_DIM)),
            _const_spec((1, HEAD_DIM)),
        ],
        out_specs=[
            pl.BlockSpec((None, 2, nch, HEAD_DIM), lambda b, lp: (b, lp, 0, 0)),
            pl.BlockSpec((None, 2, HEAD_DIM, nch), lambda b, lp: (b, lp, 0, 0)),
        ],
        out_shape=[
            jax.ShapeDtypeStruct((B, N_KV, nch, HEAD_DIM), BF16),
            jax.ShapeDtypeStruct((B, N_KV, HEAD_DIM, nch), BF16),
        ],
        compiler_params=_params(("parallel", "parallel")),
        name="compress",
    )(zkv, zkv, pos, w1, w2, gain)


def _spread_heads(x):
    y = x + pltpu.roll(x, 2 * Q_BLK, axis=1)
    return y + pltpu.roll(y, Q_BLK, axis=1)


def _swap_heads(x):
    n = x.shape[0]
    lane = _iota((n, 128), 1)
    halves = []
    for pr in range(x.shape[1] // 128):
        slab = jnp.concatenate([x[:, pr * 128:(pr + 1) * 128], jnp.zeros((128 - n, 128), x.dtype)], axis=0)
        tr = slab.T
        halves.append(jnp.where(lane < n, tr[0:n], pltpu.roll(tr[n:2 * n], n, axis=1)))
    return jnp.concatenate(halves, axis=1)


def _nsa_kernel(*refs, nsel, qpb):
    per_chunk = KCH // SEL_LEN
    for k in range(qpb):
        _nsa_block(k, "head", None, *refs, nsel=nsel, qpb=qpb)

    def step(i, carry):
        for k in range(qpb):
            _nsa_block(k, "step", i, *refs, nsel=nsel, qpb=qpb)
        return carry

    common = pl.program_id(1) * (qpb // per_chunk)
    lax.fori_loop(0, common, step, 0)
    for k in range(qpb):
        for extra in range(k // per_chunk):
            _nsa_block(k, "step", common + extra, *refs, nsel=nsel, qpb=qpb)
    for k in range(qpb):
        _nsa_block(k, "tail", None, *refs, nsel=nsel, qpb=qpb)


def _nsa_block(k, phase, step_i, zq_ref, qg_ref, kc_ref, vct_ref, ks_ref, kw_ref, vst_ref, vwt_ref, ng_ref, o_ref,
               qtp_ref, oc_ref, ow_ref, bias_ref, ms_ref, accs_ref, mw_ref, accw_ref, sbuf_ref, wbuf_ref,
               gates_ref, scs_ref, wbias_ref, *, nsel, qpb):
    qtp_ref, oc_ref, ow_ref, bias_ref, ms_ref, accs_ref, sbuf_ref, gates_ref = (
        r.at[k] for r in (qtp_ref, oc_ref, ow_ref, bias_ref, ms_ref, accs_ref, sbuf_ref, gates_ref))
    qi = pl.program_id(1) * qpb + k
    q_rows = pl.ds(k * Q_BLK, Q_BLK)
    s0 = qi * Q_BLK
    cur = qi
    ncmp = kc_ref.shape[1]
    per_chunk = KCH // SEL_LEN
    groups = range(N_KV)
    n_chunks = qi // per_chunk + 1
    cur_slab = k % per_chunk
    cut_rows = slice(cur_slab * SEL_LEN, (cur_slab + 1) * SEL_LEN)
    last_slabs = per_chunk // 2 if cur_slab < per_chunk // 2 else per_chunk

    def scores(k_ref, i):
        k0 = pl.multiple_of(i * KCH, KCH)
        return [_dot(k_ref[g // 2, pl.ds(k0, KCH), :], qtp_ref[g]) for g in groups]

    def park(buf_ref, vals):
        for g in groups:
            buf_ref[g] = vals[g]

    def cut(buf_ref, tri):
        for g in groups:
            buf_ref[g, cut_rows, :] = buf_ref[g, cut_rows, :] + tri

    def update(g, blocks, shifts, vt, m_ref, acc_ref):
        m = m_ref[g]
        part = None
        for blk, sh in zip(blocks, shifts):
            top8 = jnp.max(blk.reshape(SEL_LEN // 8, 8, GW), axis=0) + sh
            part = top8 if part is None else jnp.maximum(part, top8)
        mnew = jnp.maximum(m, jnp.max(part, axis=0, keepdims=True))
        e = jnp.concatenate([jnp.exp2(blk + (sh - mnew)) for blk, sh in zip(blocks, shifts)], axis=0).astype(BF16)
        acc_ref[g] = jnp.exp2(m - mnew) * acc_ref[g] + _dot(vt, e)
        m_ref[g] = mnew

    def slabs(buf_ref, g, ns):
        return [buf_ref[g, nb * SEL_LEN:(nb + 1) * SEL_LEN, :] for nb in range(ns)]

    def sel_softmax(i, ns=per_chunk):
        for g in groups:
            shifts = [bias_ref[g, pl.ds(i * per_chunk + nb, 1), :] for nb in range(ns)]
            update(g, slabs(sbuf_ref, g, ns), shifts, vst_ref[i, g, :, 0:ns * SEL_LEN], ms_ref, accs_ref)

    key_row = _iota((SEL_LEN, GW), 0)
    in_blk_t = _iota((SEL_LEN, GW), 1) % Q_BLK
    causal_tri = jnp.where(key_row <= in_blk_t, 0.0, NEG)

    if phase == "step":
        nxt = scores(ks_ref, step_i + 1)
        sel_softmax(step_i)
        park(sbuf_ref, nxt)
        return

    if phase == "tail":
        cut(sbuf_ref, causal_tri)
        sel_softmax(n_chunks - 1, last_slabs)
        outs = []
        for g in groups:
            o_s = accs_ref[g, 0:HEAD_DIM, :] * (1.0 / jnp.maximum(accs_ref[g, HEAD_DIM:HEAD_DIM + 1, :], 1e-30))
            gate = [gates_ref[g, c:c + 1, :] for c in range(3)]
            o_t = gate[0] * oc_ref[g] + gate[1] * o_s + gate[2] * ow_ref[g]
            outs.append(_swap_heads(o_t))
        for g in groups:
            o_ref[g, q_rows, :] = outs[g].astype(o_ref.dtype)
        return

    lane_t = s0 + (_iota((1, GW), 1) % Q_BLK)
    lane_grp = _iota((1, GW), 1) // Q_BLK
    ov_n = _iota((nsel, ncmp), 0) * SEL_LEN
    ov_c = _iota((nsel, ncmp), 1) * CMP_STRIDE
    overlap = jnp.where((ov_c < ov_n + SEL_LEN) & (ov_c + CMP_LEN > ov_n), 1.0, 0.0).astype(BF16)


    ok_c = _iota((ncmp, GW), 0) * CMP_STRIDE + (CMP_LEN - 1) <= lane_t
    ngt = jnp.concatenate([ng_ref[q_rows, :], jnp.zeros((128 - Q_BLK, NG_PAD), F32)], axis=0).T
    ngt_hi = pltpu.roll(ngt, Q_BLK, axis=1)
    low_half = _iota((1, 128), 1) < Q_BLK
    for g in groups:
        for c in range(3):
            r = c * N_HEADS + g * HPG
            pairs = [jnp.where(low_half, ngt[r + j:r + j + 1], ngt_hi[r + j + 1:r + j + 2]) for j in (0, 2)]
            gates_ref[g, c:c + 1, :] = jnp.concatenate(pairs, axis=1)

    qts = []
    for g in groups:
        xt = _swap_heads(zq_ref[g, q_rows, :])
        ssq = jnp.sum(xt * xt, axis=0, keepdims=True)
        qts.append((xt * lax.rsqrt(ssq * (1.0 / HEAD_DIM) + EPS) * qg_ref[...]).astype(BF16))
    scs = []
    for g in groups:
        off = (g % 2) * HEAD_DIM
        qtp_ref[g, off:off + HEAD_DIM, :] = qts[g]
        qtp_ref[g, HEAD_DIM - off:2 * HEAD_DIM - off, :] = jnp.zeros((HEAD_DIM, GW), BF16)
        scs.append(_dot(kc_ref[g], qts[g]))

    n_win = WINDOW // KCH + 1
    win_ids = [jnp.maximum(n_chunks - (n_win - j), 0) for j in range(n_win)]
    win_dead = [jnp.where(n_chunks >= n_win - j, 0.0, NEG) for j in range(n_win)]
    park(scs_ref, scs)
    park(sbuf_ref, scores(ks_ref, 0))
    park(wbuf_ref, scores(kw_ref, win_ids[0]))

    expired_tri = jnp.where(key_row > in_blk_t, 0.0, NEG)
    win_blocks = WINDOW // SEL_LEN
    n_all = _iota((nsel, GW), 0)
    wbias_ref[...] = jnp.where((n_all >= cur - win_blocks) & (n_all <= cur), 0.0, NEG)
    cut(wbuf_ref, jnp.where(cur >= win_blocks, expired_tri, 0.0))

    ms_ref[...] = jnp.full(ms_ref.shape, 0.5 * NEG, F32)
    mw_ref[...] = jnp.full(mw_ref.shape, 0.5 * NEG, F32)
    accs_ref[...] = jnp.zeros(accs_ref.shape, F32)
    accw_ref[...] = jnp.zeros(accw_ref.shape, F32)

    def win_softmax(i, dead, lo=0, hi=per_chunk):
        shifts = [wbias_ref[pl.ds(i * per_chunk + nb, 1), :] + dead for nb in range(lo, hi)]
        for g in groups:
            blocks = [wbuf_ref[g, nb * SEL_LEN:(nb + 1) * SEL_LEN, :] for nb in range(lo, hi)]
            update(g, blocks, shifts, vwt_ref[i, g, :, lo * SEL_LEN:hi * SEL_LEN], mw_ref, accw_ref)

    first_lo = per_chunk // 2 if cur_slab >= per_chunk // 2 else 0
    for j in range(n_win - 1):
        nxt = scores(kw_ref, win_ids[j + 1])
        win_softmax(win_ids[j], win_dead[j], lo=first_lo if j == 0 else 0)
        park(wbuf_ref, nxt)

    imp_all = jnp.zeros((nsel, GW), F32)
    some_c = lane_t >= CMP_LEN - 1
    ones_rows = jnp.ones((16, ncmp), BF16)
    for g in groups:
        scm = jnp.where(ok_c, scs_ref[g], NEG)
        ec = jnp.exp2(scm - jnp.max(scm, axis=0, keepdims=True))
        both = _dot(jnp.concatenate([vct_ref[g], overlap, ones_rows], axis=0), ec.astype(BF16))
        denom = both[HEAD_DIM + nsel:HEAD_DIM + nsel + 1]
        both = both[0:HEAD_DIM + nsel] * jnp.where(some_c, 1.0 / jnp.maximum(denom, 1e-30), 0.0)
        oc_ref[g] = both[0:HEAD_DIM]
        imp = _spread_heads(both[HEAD_DIM:HEAD_DIM + nsel])
        imp_all = jnp.where(lane_grp == g, imp, imp_all)

    n_io = _iota((nsel, GW), 0)
    forced = (n_io == 0) | (n_io == cur) | (n_io == cur - 1)
    valid = n_io <= cur
    score = jnp.where(forced, FORCE_SCORE, imp_all)
    score = jnp.where(valid, score, -1.0)
    rows8 = [score[r:r + 8] for r in range(0, nsel, 8)]
    ranks = [jnp.zeros((8, GW), F32) for _ in rows8]
    sub = _iota((8, GW), 0)
    for m in range(nsel):
        rowv = score[m:m + 1, :]
        for k, blk in enumerate(rows8):
            ge = jnp.where(rowv >= blk, 1.0, 0.0)
            gt = jnp.where(rowv > blk, 1.0, 0.0)
            if 8 * k > m:
                inc = ge
            elif 8 * k + 7 < m:
                inc = gt
            else:
                inc = jnp.where(sub + 8 * k > m, ge, gt)
            ranks[k] = ranks[k] + inc
    rank = jnp.concatenate(ranks, axis=0)
    sel_all = jnp.where((rank < float(min(SEL_TOPN, nsel))) & valid, 1.0, 0.0)
    for g in groups:
        mine = jnp.where(lane_grp == g, sel_all, 0.0)
        bias_ref[g] = (_spread_heads(mine) - 1.0) * (-NEG)

    cut(wbuf_ref, causal_tri)
    win_softmax(win_ids[n_win - 1], win_dead[n_win - 1], hi=last_slabs)
    for g in groups:
        ow_ref[g] = accw_ref[g, 0:HEAD_DIM, :] * (1.0 / jnp.maximum(accw_ref[g, HEAD_DIM:HEAD_DIM + 1, :], 1e-30))


def _nsa(zq, qgain, kc, vct, kn, vt, ng, *, B, S):
    T = B * S
    qpb = 2 * (KCH // SEL_LEN)
    assert S % (Q_BLK * qpb) == 0
    nq = S // (Q_BLK * qpb)
    nch = S // KCH
    ncmp = S // CMP_STRIDE
    nsel = S // SEL_LEN
    tok = lambda b, i: (0, b * nq + i, 0)
    return pl.pallas_call(
        functools.partial(_nsa_kernel, nsel=nsel, qpb=qpb),
        grid=(B, nq),
        in_specs=[
            pl.BlockSpec((N_KV, Q_BLK * qpb, GW), tok),
            _const_spec((HEAD_DIM, GW)),
            pl.BlockSpec((None, N_KV, ncmp, HEAD_DIM), lambda b, i: (b, 0, 0, 0)),
            pl.BlockSpec((None, N_KV, HEAD_DIM, ncmp), lambda b, i: (b, 0, 0, 0)),
            pl.BlockSpec((None, None, 2, S, 128), lambda b, i: (0, b, 0, 0, 0)),
            pl.BlockSpec((None, None, 2, S, 128), lambda b, i: (1, b, 0, 0, 0)),
            pl.BlockSpec((None, None, nch, N_KV, VROWS, KCH), lambda b, i: (0, b, 0, 0, 0, 0)),
            pl.BlockSpec((None, None, nch, N_KV, VROWS, KCH), lambda b, i: (1, b, 0, 0, 0, 0)),
            pl.BlockSpec((Q_BLK * qpb, NG_PAD), lambda b, i: (b * nq + i, 0)),
        ],
        out_specs=pl.BlockSpec((N_KV, Q_BLK * qpb, GW), tok),
        out_shape=jax.ShapeDtypeStruct((N_KV, T, GW), BF16),
        scratch_shapes=[
            pltpu.VMEM((qpb, N_KV, 2 * HEAD_DIM, GW), BF16),
            pltpu.VMEM((qpb, N_KV, HEAD_DIM, GW), F32),
            pltpu.VMEM((qpb, N_KV, HEAD_DIM, GW), F32),
            pltpu.VMEM((qpb, N_KV, nsel, GW), F32),
            pltpu.VMEM((qpb, N_KV, 1, GW), F32),
            pltpu.VMEM((qpb, N_KV, VROWS, GW), F32),
            pltpu.VMEM((N_KV, 1, GW), F32),
            pltpu.VMEM((N_KV, VROWS, GW), F32),
            pltpu.VMEM((qpb, N_KV, KCH, GW), F32),
            pltpu.VMEM((N_KV, KCH, GW), F32),
            pltpu.VMEM((qpb, N_KV, 8, GW), F32),
            pltpu.VMEM((N_KV, ncmp, GW), F32),
            pltpu.VMEM((nsel, GW), F32),
        ],
        compiler_params=_params(("parallel", "arbitrary")),
        name="nsa",
    )(zq, qgain, kc, vct, kn, kn, vt, vt, ng)


def _merge_ffn_kernel(x_ref, ya_ref, o_ref_in, mg_ref, wn_ref, wo_ref, g2_ref, wg_ref, wu_ref, wd_ref,
                      out_ref, h_ref, *, tf):
    yb = _dot(o_ref_in[0], wn_ref[0:GW, :])
    for g in range(1, N_KV):
        yb = yb + _dot(o_ref_in[g], wn_ref[g * GW:(g + 1) * GW, :])
    tm = x_ref.shape[0]
    ya = jnp.concatenate([ya_ref[ck].reshape(tm, 128) for ck in range(D_MODEL // 128)], axis=1)
    mixed = (mg_ref[:, 0:D_MODEL].astype(F32) * ya
             + mg_ref[:, D_MODEL:2 * D_MODEL].astype(F32) * yb)
    x1 = x_ref[...] + _dot(mixed.astype(BF16), wo_ref[...])
    y = x1 * lax.rsqrt(jnp.mean(x1 * x1, axis=-1, keepdims=True) + EPS) * g2_ref[...]
    h_ref[...] = y.astype(BF16)
    out_ref[...] = x1
    for j in range(D_FF // tf):
        h = h_ref[...]
        sl = slice(j * tf, (j + 1) * tf)
        act = jax.nn.silu(_dot(h, wg_ref[:, sl])) * _dot(h, wu_ref[:, sl])
        out_ref[...] += _dot(act.astype(BF16), wd_ref[sl, :])


def _merge_ffn(x2d, ya, o, mg, wn, wo, gain2, wg, wu, wd, *, B, S, tm=512, tf=256):
    assert S % tm == 0 and D_FF % tf == 0
    ni = S // tm
    tok = lambda b, i: (b * ni + i, 0)
    once = lambda shape: pl.BlockSpec(shape, lambda b, i: (0,) * len(shape), pipeline_mode=pl.Buffered(1))
    return pl.pallas_call(
        functools.partial(_merge_ffn_kernel, tf=tf),
        grid=(B, ni),
        in_specs=[
            pl.BlockSpec((tm, D_MODEL), tok),
            pl.BlockSpec((D_MODEL // 128, tm // 8, None, 8, 128), lambda b, i: (0, i, b, 0, 0)),
            pl.BlockSpec((N_KV, tm, GW), lambda b, i: (0, b * ni + i, 0)),
            pl.BlockSpec((tm, 2 * D_MODEL), tok),
            once((N_HEADS * HEAD_DIM, D_MODEL)),
            once((D_MODEL, D_MODEL)),
            _const_spec((1, D_MODEL)),
            once((D_MODEL, D_FF)),
            once((D_MODEL, D_FF)),
            once((D_FF, D_MODEL)),
        ],
        out_specs=pl.BlockSpec((tm, D_MODEL), tok),
        out_shape=jax.ShapeDtypeStruct((B * S, D_MODEL), F32),
        scratch_shapes=[pltpu.VMEM((tm, D_MODEL), BF16)],
        compiler_params=_params(("parallel", "parallel")),
        name="merge_ffn",
    )(x2d, ya, o, mg, wn, wo, gain2, wg, wu, wd)


def _layer(x, p):
    B, S, _ = x.shape
    T = B * S
    dh = HEAD_DIM
    row = lambda v: v.reshape(1, -1)
    w_in = p["w_in"]
    n_ng = 3 * N_HEADS
    o_ng = 2 * D_RNN + N_HEADS * dh + 6 * KVW
    w_packed = jnp.concatenate(
        [w_in[:, :o_ng], jnp.pad(w_in[:, o_ng:o_ng + n_ng], ((0, 0), (0, NG_PAD - n_ng))), w_in[:, o_ng + n_ng:]],
        axis=1).astype(BF16)
    rx, gy, zq, zkv, ng, mg = _inproj(x, row(p["norm1"]), w_packed)

    slabs = lambda a: a.reshape(a.shape[0], S * B, 128)
    ya = _rglru(slabs(rx), slabs(gy), p["conv_w"], row(p["conv_b"]),
                p["rg_wa"].astype(BF16), row(p["rg_ba"]), p["rg_wi"].astype(BF16), row(p["rg_bi"]),
                row(p["rg_lambda"]), p["w_rg_out"].astype(BF16), nb=B)
    ya = ya.reshape(D_MODEL // 128, S // 8, B, 8, 128)

    kgains = jnp.tile(p["k_norm"][1:3], (1, N_KV)).reshape(2, 1, KVW)
    kn, vt = _kvprep(zkv, kgains, B=B, S=S)
    pos = jnp.stack([p["cmp_pos_k"], p["cmp_pos_v"]]).reshape(2, CMP_STRIDE, 2 * dh)
    w1 = jnp.stack([p["cmp_k_w1"], p["cmp_v_w1"]]).astype(BF16)
    w2 = jnp.stack([p["cmp_k_w2"], p["cmp_v_w2"]]).astype(BF16)
    kc, vct = _compress(zkv, pos, w1, w2, row(p["k_norm"][0]), B=B, S=S)
    qgain = jnp.broadcast_to((p["q_norm"] * (dh ** -0.5 * LOG2E))[:, None], (dh, GW))
    o = _nsa(zq, qgain, kc, vct, kn, vt, ng, B=B, S=S)

    x2 = _merge_ffn(x.reshape(T, D_MODEL), ya, o, mg, p["w_nsa_out"].astype(BF16), p["w_o"].astype(BF16),
                    row(p["norm2"]), p["w_gate"].astype(BF16), p["w_up"].astype(BF16), p["w_down"].astype(BF16),
                    B=B, S=S)
    return x2.reshape(B, S, D_MODEL)


def kernel(x, norm1, w_in, conv_w, conv_b, rg_wa, rg_ba, rg_wi, rg_bi, rg_lambda, q_norm, k_norm, cmp_pos_k,
           cmp_pos_v, cmp_k_w1, cmp_k_w2, cmp_v_w1, cmp_v_w2, w_rg_out, w_nsa_out, w_o, norm2, w_gate, w_up, w_down):
    params = dict(norm1=norm1, w_in=w_in, conv_w=conv_w, conv_b=conv_b, rg_wa=rg_wa, rg_ba=rg_ba, rg_wi=rg_wi,
                  rg_bi=rg_bi, rg_lambda=rg_lambda, q_norm=q_norm, k_norm=k_norm, cmp_pos_k=cmp_pos_k,
                  cmp_pos_v=cmp_pos_v, cmp_k_w1=cmp_k_w1, cmp_k_w2=cmp_k_w2, cmp_v_w1=cmp_v_w1, cmp_v_w2=cmp_v_w2,
                  w_rg_out=w_rg_out, w_nsa_out=w_nsa_out, w_o=w_o, norm2=norm2, w_gate=w_gate, w_up=w_up,
                  w_down=w_down)
    for l in range(norm1.shape[0]):
        x = _layer(x, {k: v[l] for k, v in params.items()})
    return x
```

```python
import functools

import jax
import jax.numpy as jnp
from jax import lax
from jax.experimental import pallas as pl
from jax.experimental.pallas import tpu as pltpu

D_MODEL = 1024
D_RNN = 1024
RG_BLOCKS = 4
RG_BW = D_RNN // RG_BLOCKS
CONV_W = 4
RG_C = 8.0
N_HEADS = 16
N_KV = 4
HEAD_DIM = 64
HPG = N_HEADS // N_KV
CMP_LEN = 32
CMP_STRIDE = 16
CMP_HID = 256
SEL_LEN = 64
SEL_TOPN = 16
WINDOW = 512
Q_BLK = 64
D_FF = 2816
EPS = 1e-6
FORCE_SCORE = 1e6
NEG = -1e30
LOG2E = 1.4426950408889634

KCH = 256
GW = HPG * HEAD_DIM
KVW = N_KV * HEAD_DIM
VROWS = 80
NG_PAD = 128

F32 = jnp.float32
BF16 = jnp.bfloat16

VMEM_LIMIT = 56 * 1024 * 1024


def _params(sem):
    return pltpu.CompilerParams(dimension_semantics=sem, vmem_limit_bytes=VMEM_LIMIT)


def _dot(a, b):
    return jnp.dot(a, b, preferred_element_type=F32)


def _dot_nt(a, b):
    return lax.dot_general(a, b, (((1,), (1,)), ((), ())), preferred_element_type=F32)


def _split(x):
    hi = x.astype(BF16)
    lo = (x - hi.astype(F32)).astype(BF16)
    return hi, lo


def _iota(shape, dim):
    return lax.broadcasted_iota(jnp.int32, shape, dim)


def _const_spec(shape):
    return pl.BlockSpec(shape, lambda *_: (0,) * len(shape))


_O_RY = D_RNN
_O_Q = 2 * D_RNN
_O_KV = _O_Q + N_HEADS * HEAD_DIM
_O_NG = _O_KV + 6 * KVW
_O_MG = _O_NG + NG_PAD
_W_COLS = _O_MG + 2 * D_MODEL


def _inproj_kernel(x_ref, g_ref, w_ref, rx_ref, gy_ref, zq_ref, zkv_ref, ng_ref, mg_ref, h_ref):
    x = x_ref[...]
    y = x * lax.rsqrt(jnp.mean(x * x, axis=-1, keepdims=True) + EPS) * g_ref[...]
    h_ref[...] = y.astype(BF16)

    def proj(c0, width):
        return _dot(h_ref[...], w_ref[:, c0:c0 + width])

    cw = 256
    tm = x_ref.shape[0]
    for c in range(D_RNN // cw):
        rx = proj(c * cw, cw)
        gy = jax.nn.gelu(proj(_O_RY + c * cw, cw))
        for hf in range(cw // 128):
            rx_ref[c * (cw // 128) + hf] = rx[:, hf * 128:(hf + 1) * 128].reshape(tm // 8, 8, 128)
            gy_ref[c * (cw // 128) + hf] = gy[:, hf * 128:(hf + 1) * 128].reshape(tm // 8, 8, 128)
    for g in range(N_KV):
        zq_ref[g] = proj(_O_Q + g * GW, GW)
    for s in range(6):
        z = proj(_O_KV + s * KVW, KVW)
        for pr in range(KVW // 128):
            zkv_ref[s, pr] = z[:, pr * 128:(pr + 1) * 128]
    ng_ref[...] = jax.nn.sigmoid(proj(_O_NG, NG_PAD))
    for c in range(2 * D_MODEL // cw):
        mg_ref[:, c * cw:(c + 1) * cw] = jax.nn.sigmoid(proj(_O_MG + c * cw, cw)).astype(mg_ref.dtype)


def _inproj(x, gain, w, *, tm=256):
    B, S, D = x.shape
    T = B * S
    ni = S // tm
    nck = D_RNN // 128
    tok = lambda b, i: (b * ni + i, 0)
    slab_spec = pl.BlockSpec((nck, tm // 8, None, 8, 128), lambda b, i: (0, i, b, 0, 0))
    slab_shape = jax.ShapeDtypeStruct((nck, S // 8, B, 8, 128), F32)
    return pl.pallas_call(
        _inproj_kernel,
        grid=(B, ni),
        in_specs=[
            pl.BlockSpec((None, tm, D), lambda b, i: (b, i, 0)),
            _const_spec((1, D)),
            pl.BlockSpec((D, _W_COLS), lambda b, i: (0, 0), pipeline_mode=pl.Buffered(1)),
        ],
        out_specs=[
            slab_spec,
            slab_spec,
            pl.BlockSpec((N_KV, tm, GW), lambda b, i: (0, b * ni + i, 0)),
            pl.BlockSpec((6, KVW // 128, tm, 128), lambda b, i: (0, 0, b * ni + i, 0)),
            pl.BlockSpec((tm, NG_PAD), tok),
            pl.BlockSpec((tm, 2 * D_MODEL), tok),
        ],
        out_shape=[
            slab_shape,
            slab_shape,
            jax.ShapeDtypeStruct((N_KV, T, GW), F32),
            jax.ShapeDtypeStruct((6, KVW // 128, T, 128), F32),
            jax.ShapeDtypeStruct((T, NG_PAD), F32),
            jax.ShapeDtypeStruct((T, 2 * D_MODEL), BF16),
        ],
        scratch_shapes=[pltpu.VMEM((tm, D), BF16)],
        compiler_params=_params(("parallel", "parallel")),
        name="inproj",
    )(x, gain, w)


def _rglru_kernel(x_ref, gy_ref, cw_ref, cb_ref, wa_ref, ba_ref, wi_ref, bi_ref, lam_ref, wo_ref,
                  o_ref, xb_ref, gyb_ref, a_ref, u_ref, hb_ref, hs_ref, yb_ref, *, nb, tt):
    R = nb * tt
    halo = (CONV_W - 1) * nb
    step = pl.program_id(0)
    nck = D_RNN // 128

    @pl.when(step == 0)
    def _():
        xb_ref[0:halo, :] = jnp.zeros((halo, D_RNN), F32)
        hs_ref[...] = jnp.zeros((nb, D_RNN), F32)

    def slab_rows(t8, t_lo):
        return pl.ds(t8 * (8 * nb) + t_lo, nb, stride=8)

    def stage_in(t8, carry):
        for t_lo in range(8):
            r0 = pl.multiple_of((t8 * 8 + t_lo) * nb, nb)
            for ck in range(nck):
                cols = slice(ck * 128, (ck + 1) * 128)
                xb_ref[pl.ds(halo + r0, nb), cols] = x_ref[ck, slab_rows(t8, t_lo), :]
                gyb_ref[pl.ds(r0, nb), cols] = gy_ref[ck, slab_rows(t8, t_lo), :]
        return carry

    lax.fori_loop(0, tt // 8, stage_in, 0)
    xr = cb_ref[...] + cw_ref[0:1, :] * xb_ref[0:R, :]
    for k in range(1, CONV_W):
        xr = xr + cw_ref[k:k + 1, :] * xb_ref[k * nb:k * nb + R, :]
    tail = xb_ref[R:R + halo, :]
    xb_ref[0:halo, :] = tail

    xrb = xr.astype(BF16)
    decay_log2 = (-RG_C * LOG2E) * jax.nn.softplus(-lam_ref[...])
    row = _iota((R, RG_BW), 0)
    first = (row < nb) & (step == 0)
    for n in range(RG_BLOCKS):
        sl = slice(n * RG_BW, (n + 1) * RG_BW)
        xn = xrb[:, sl]
        r = jax.nn.sigmoid(_dot(xn, wa_ref[n]) + ba_ref[:, sl])
        ig = jax.nn.sigmoid(_dot(xn, wi_ref[n]) + bi_ref[:, sl])
        a = jnp.exp2(r * decay_log2[:, sl])
        mult = jnp.sqrt(1.0 - a * a)
        mult = jnp.where(first, 1.0, mult)
        a_ref[:, sl] = a
        u_ref[:, sl] = mult * ig * xr[:, sl]

    def body(t, h):
        r0 = pl.multiple_of(t * nb, nb)
        h = a_ref[pl.ds(r0, nb), :] * h + u_ref[pl.ds(r0, nb), :]
        hb_ref[pl.ds(r0, nb), :] = h
        return h

    h = lax.fori_loop(0, tt, body, hs_ref[...], unroll=8)
    hs_ref[...] = h
    yb_ref[...] = _dot((hb_ref[...] * gyb_ref[...]).astype(BF16), wo_ref[...])

    def stage_out(t8, carry):
        for t_lo in range(8):
            r0 = pl.multiple_of((t8 * 8 + t_lo) * nb, nb)
            for ck in range(nck):
                o_ref[ck, slab_rows(t8, t_lo), :] = yb_ref[pl.ds(r0, nb), ck * 128:(ck + 1) * 128]
        return carry

    lax.fori_loop(0, tt // 8, stage_out, 0)


def _rglru(rx, gy, conv_w, conv_b, wa, ba, wi, bi, lam, wo, *, nb, tt=64):
    nck, rows, _ = rx.shape
    S = rows // nb
    assert S % tt == 0 and tt % 8 == 0
    R = nb * tt
    halo = (CONV_W - 1) * nb
    return pl.pallas_call(
        functools.partial(_rglru_kernel, nb=nb, tt=tt),
        grid=(S // tt,),
        in_specs=[
            pl.BlockSpec((nck, R, 128), lambda s: (0, s, 0)),
            pl.BlockSpec((nck, R, 128), lambda s: (0, s, 0)),
            _const_spec((CONV_W, D_RNN)),
            _const_spec((1, D_RNN)),
            _const_spec((RG_BLOCKS, RG_BW, RG_BW)),
            _const_spec((1, D_RNN)),
            _const_spec((RG_BLOCKS, RG_BW, RG_BW)),
            _const_spec((1, D_RNN)),
            _const_spec((1, D_RNN)),
            _const_spec((D_RNN, D_MODEL)),
        ],
        out_specs=pl.BlockSpec((D_MODEL // 128, R, 128), lambda s: (0, s, 0)),
        out_shape=jax.ShapeDtypeStruct((D_MODEL // 128, rows, 128), F32),
        scratch_shapes=[
            pltpu.VMEM((R + halo, D_RNN), F32),
            pltpu.VMEM((R, D_RNN), F32),
            pltpu.VMEM((R, D_RNN), F32),
            pltpu.VMEM((R, D_RNN), F32),
            pltpu.VMEM((R, D_RNN), F32),
            pltpu.VMEM((nb, D_RNN), F32),
            pltpu.VMEM((R, D_MODEL), F32),
        ],
        compiler_params=_params(("arbitrary",)),
        name="rglru",
    )(rx, gy, conv_w, conv_b, wa, ba, wi, bi, lam, wo)


def _kvprep_kernel(zk_ref, zv_ref, g_ref, kn_ref, vt_ref):
    rows = zk_ref.shape[1]
    x = jnp.concatenate([zk_ref[0], zk_ref[1]], axis=1)
    seg = jnp.where(_iota((KVW, KVW), 0) // HEAD_DIM == _iota((KVW, KVW), 1) // HEAD_DIM, 1.0, 0.0).astype(BF16)
    hi, lo = _split(x * x)
    ssq = _dot(jnp.concatenate([hi, lo], axis=0), seg)
    kn = (x * lax.rsqrt((ssq[0:rows] + ssq[rows:2 * rows]) * (1.0 / HEAD_DIM) + EPS) * g_ref[...]).astype(BF16)
    for pr in range(KVW // 128):
        kn_ref[pr] = kn[:, pr * 128:(pr + 1) * 128]
    eye = jnp.where(_iota((KVW, KVW), 0) == _iota((KVW, KVW), 1), 1.0, 0.0).astype(BF16)
    pad = jnp.where(_iota((VROWS - HEAD_DIM, KCH), 0) == 0, 1.0, 0.0).astype(BF16)
    for c in range(rows // KCH):
        sl = slice(c * KCH, (c + 1) * KCH)
        v = jnp.concatenate([zv_ref[0, sl, :], zv_ref[1, sl, :]], axis=1).astype(BF16)
        vt = _dot_nt(eye, v).astype(BF16)
        for g in range(N_KV):
            vt_ref[c, g, 0:HEAD_DIM, :] = vt[g * HEAD_DIM:(g + 1) * HEAD_DIM]
            vt_ref[c, g, HEAD_DIM:VROWS, :] = pad


def _kvprep(zkv, gains, *, B, S, cpb=4):
    nch = S // KCH
    assert nch % cpb == 0
    nblk = nch // cpb
    rows = cpb * KCH
    return pl.pallas_call(
        _kvprep_kernel,
        grid=(2, B, nblk),
        in_specs=[
            pl.BlockSpec((None, KVW // 128, rows, 128), lambda w, b, i: (2 + 2 * w, 0, b * nblk + i, 0)),
            pl.BlockSpec((None, KVW // 128, rows, 128), lambda w, b, i: (3 + 2 * w, 0, b * nblk + i, 0)),
            pl.BlockSpec((None, 1, KVW), lambda w, b, i: (w, 0, 0)),
        ],
        out_specs=[
            pl.BlockSpec((None, None, KVW // 128, rows, 128), lambda w, b, i: (w, b, 0, i, 0)),
            pl.BlockSpec((None, None, cpb, N_KV, VROWS, KCH), lambda w, b, i: (w, b, i, 0, 0, 0)),
        ],
        out_shape=[
            jax.ShapeDtypeStruct((2, B, KVW // 128, S, 128), BF16),
            jax.ShapeDtypeStruct((2, B, nch, N_KV, VROWS, KCH), BF16),
        ],
        compiler_params=_params(("parallel", "parallel", "parallel")),
        name="kvprep",
    )(zkv, zkv, gains)


def _compress_kernel(zk_ref, zv_ref, pos_ref, w1_ref, w2_ref, g_ref, kc_ref, vct_ref):
    nch = zk_ref.shape[0] // CMP_STRIDE
    lane = _iota((nch, 128), 1)
    npair = CMP_STRIDE // 2
    eye = jnp.where(_iota((HEAD_DIM, HEAD_DIM), 0) == _iota((HEAD_DIM, HEAD_DIM), 1), 1.0, 0.0).astype(BF16)
    for kind, z_ref in enumerate((zk_ref, zv_ref)):
        for ge in range(2):
            combs = []
            for p in range(npair):
                a = z_ref[pl.ds(2 * p, nch, stride=CMP_STRIDE), :]
                b = z_ref[pl.ds(2 * p + 1, nch, stride=CMP_STRIDE), :]
                if ge == 0:
                    combs.append(jnp.where(lane < HEAD_DIM, a, pltpu.roll(b, HEAD_DIM, axis=1)))
                else:
                    combs.append(jnp.where(lane < HEAD_DIM, pltpu.roll(a, HEAD_DIM, axis=1), b))
            halves = []
            half_w = npair * 128
            for half in range(2):
                lhs = jnp.concatenate(
                    [(combs[p] + pos_ref[kind, half * npair + p:half * npair + p + 1, :]).astype(BF16)
                     for p in range(npair)], axis=1)
                halves.append(_dot(lhs, w1_ref[kind, half * half_w:(half + 1) * half_w, :]))
            hid = halves[0] + pltpu.roll(halves[1], nch - 1, axis=0)
            hid = jax.nn.gelu(hid)
            out = _dot(hid.astype(BF16), w2_ref[kind])
            if kind == 0:
                out = out * lax.rsqrt(jnp.mean(out * out, axis=-1, keepdims=True) + EPS) * g_ref[...]
            res = jnp.where(_iota(out.shape, 0) < nch - 1, out, 0.0).astype(BF16)
            if kind == 0:
                kc_ref[ge] = res
            else:
                vct_ref[ge] = _dot_nt(eye, res).astype(BF16)


def _compress(zkv, pos, w1, w2, gain, *, B, S):
    nch = S // CMP_STRIDE
    return pl.pallas_call(
        _compress_kernel,
        grid=(B, KVW // 128),
        in_specs=[
            pl.BlockSpec((None, None, S, 128), lambda b, lp: (0, lp, b, 0)),
            pl.BlockSpec((None, None, S, 128), lambda b, lp: (1, lp, b, 0)),
            _const_spec((2, CMP_STRIDE, 128)),
            _const_spec((2, CMP_LEN * HEAD_DIM, CMP_HID)),
            _const_spec((2, CMP_HID, HEAD_DIM)),
            _const_spec((1, HEAD_DIM)),
        ],
        out_specs=[
            pl.BlockSpec((None, 2, nch, HEAD_DIM), lambda b, lp: (b, lp, 0, 0)),
            pl.BlockSpec((None, 2, HEAD_DIM, nch), lambda b, lp: (b, lp, 0, 0)),
        ],
        out_shape=[
            jax.ShapeDtypeStruct((B, N_KV, nch, HEAD_DIM), BF16),
            jax.ShapeDtypeStruct((B, N_KV, HEAD_DIM, nch), BF16),
        ],
        compiler_params=_params(("parallel", "parallel")),
        name="compress",
    )(zkv, zkv, pos, w1, w2, gain)


def _spread_heads(x):
    y = x + pltpu.roll(x, 2 * Q_BLK, axis=1)
    return y + pltpu.roll(y, Q_BLK, axis=1)


def _swap_heads(x):
    n = x.shape[0]
    lane = _iota((n, 128), 1)
    halves = []
    for pr in range(x.shape[1] // 128):
        slab = jnp.concatenate([x[:, pr * 128:(pr + 1) * 128], jnp.zeros((128 - n, 128), x.dtype)], axis=0)
        tr = slab.T
        halves.append(jnp.where(lane < n, tr[0:n], pltpu.roll(tr[n:2 * n], n, axis=1)))
    return jnp.concatenate(halves, axis=1)


def _nsa_kernel(*refs, nsel, qpb):
    per_chunk = KCH // SEL_LEN
    for k in range(qpb):
        _nsa_block(k, "head", None, *refs, nsel=nsel, qpb=qpb)

    def step(i, carry):
        for k in range(qpb):
            _nsa_block(k, "step", i, *refs, nsel=nsel, qpb=qpb)
        return carry

    common = pl.program_id(1) * (qpb // per_chunk)
    lax.fori_loop(0, common, step, 0)
    for k in range(qpb):
        for extra in range(k // per_chunk):
            _nsa_block(k, "step", common + extra, *refs, nsel=nsel, qpb=qpb)
    for k in range(qpb):
        _nsa_block(k, "tail", None, *refs, nsel=nsel, qpb=qpb)


def _nsa_block(k, phase, step_i, zq_ref, qg_ref, kc_ref, vct_ref, ks_ref, kw_ref, vst_ref, vwt_ref, ng_ref, o_ref,
               qtp_ref, oc_ref, ow_ref, bias_ref, ms_ref, accs_ref, mw_ref, accw_ref, sbuf_ref, wbuf_ref,
               gates_ref, scs_ref, wbias_ref, *, nsel, qpb):
    qtp_ref, oc_ref, ow_ref, bias_ref, ms_ref, accs_ref, sbuf_ref, gates_ref = (
        r.at[k] for r in (qtp_ref, oc_ref, ow_ref, bias_ref, ms_ref, accs_ref, sbuf_ref, gates_ref))
    qi = pl.program_id(1) * qpb + k
    q_rows = pl.ds(k * Q_BLK, Q_BLK)
    s0 = qi * Q_BLK
    cur = qi
    ncmp = kc_ref.shape[1]
    per_chunk = KCH // SEL_LEN
    groups = range(N_KV)
    n_chunks = qi // per_chunk + 1
    cur_slab = k % per_chunk
    cut_rows = slice(cur_slab * SEL_LEN, (cur_slab + 1) * SEL_LEN)
    last_slabs = per_chunk // 2 if cur_slab < per_chunk // 2 else per_chunk

    def scores(k_ref, i):
        k0 = pl.multiple_of(i * KCH, KCH)
        return [_dot(k_ref[g // 2, pl.ds(k0, KCH), :], qtp_ref[g]) for g in groups]

    def park(buf_ref, vals):
        for g in groups:
            buf_ref[g] = vals[g]

    def cut(buf_ref, tri):
        for g in groups:
            buf_ref[g, cut_rows, :] = buf_ref[g, cut_rows, :] + tri

    def update(g, blocks, shifts, vt, m_ref, acc_ref):
        m = m_ref[g]
        part = None
        for blk, sh in zip(blocks, shifts):
            top8 = jnp.max(blk.reshape(SEL_LEN // 8, 8, GW), axis=0) + sh
            part = top8 if part is None else jnp.maximum(part, top8)
        mnew = jnp.maximum(m, jnp.max(part, axis=0, keepdims=True))
        e = jnp.concatenate([jnp.exp2(blk + (sh - mnew)) for blk, sh in zip(blocks, shifts)], axis=0).astype(BF16)
        acc_ref[g] = jnp.exp2(m - mnew) * acc_ref[g] + _dot(vt, e)
        m_ref[g] = mnew

    def slabs(buf_ref, g, ns):
        return [buf_ref[g, nb * SEL_LEN:(nb + 1) * SEL_LEN, :] for nb in range(ns)]

    def sel_softmax(i, ns=per_chunk):
        for g in groups:
            shifts = [bias_ref[g, pl.ds(i * per_chunk + nb, 1), :] for nb in range(ns)]
            update(g, slabs(sbuf_ref, g, ns), shifts, vst_ref[i, g, :, 0:ns * SEL_LEN], ms_ref, accs_ref)

    key_row = _iota((SEL_LEN, GW), 0)
    in_blk_t = _iota((SEL_LEN, GW), 1) % Q_BLK
    causal_tri = jnp.where(key_row <= in_blk_t, 0.0, NEG)

    if phase == "step":
        nxt = scores(ks_ref, step_i + 1)
        sel_softmax(step_i)
        park(sbuf_ref, nxt)
        return

    if phase == "tail":
        cut(sbuf_ref, causal_tri)
        sel_softmax(n_chunks - 1, last_slabs)
        outs = []
        for g in groups:
            o_s = accs_ref[g, 0:HEAD_DIM, :] * (1.0 / jnp.maximum(accs_ref[g, HEAD_DIM:HEAD_DIM + 1, :], 1e-30))
            gate = [gates_ref[g, c:c + 1, :] for c in range(3)]
            o_t = gate[0] * oc_ref[g] + gate[1] * o_s + gate[2] * ow_ref[g]
            outs.append(_swap_heads(o_t))
        for g in groups:
            o_ref[g, q_rows, :] = outs[g].astype(o_ref.dtype)
        return

    lane_t = s0 + (_iota((1, GW), 1) % Q_BLK)
    lane_grp = _iota((1, GW), 1) // Q_BLK
    ov_n = _iota((nsel, ncmp), 0) * SEL_LEN
    ov_c = _iota((nsel, ncmp), 1) * CMP_STRIDE
    overlap = jnp.where((ov_c < ov_n + SEL_LEN) & (ov_c + CMP_LEN > ov_n), 1.0, 0.0).astype(BF16)


    ok_c = _iota((ncmp, GW), 0) * CMP_STRIDE + (CMP_LEN - 1) <= lane_t
    ngt = jnp.concatenate([ng_ref[q_rows, :], jnp.zeros((128 - Q_BLK, NG_PAD), F32)], axis=0).T
    ngt_hi = pltpu.roll(ngt, Q_BLK, axis=1)
    low_half = _iota((1, 128), 1) < Q_BLK
    for g in groups:
        for c in range(3):
            r = c * N_HEADS + g * HPG
            pairs = [jnp.where(low_half, ngt[r + j:r + j + 1], ngt_hi[r + j + 1:r + j + 2]) for j in (0, 2)]
            gates_ref[g, c:c + 1, :] = jnp.concatenate(pairs, axis=1)

    qts = []
    for g in groups:
        xt = _swap_heads(zq_ref[g, q_rows, :])
        ssq = jnp.sum(xt * xt, axis=0, keepdims=True)
        qts.append((xt * lax.rsqrt(ssq * (1.0 / HEAD_DIM) + EPS) * qg_ref[...]).astype(BF16))
    scs = []
    for g in groups:
        off = (g % 2) * HEAD_DIM
        qtp_ref[g, off:off + HEAD_DIM, :] = qts[g]
        qtp_ref[g, HEAD_DIM - off:2 * HEAD_DIM - off, :] = jnp.zeros((HEAD_DIM, GW), BF16)
        scs.append(_dot(kc_ref[g], qts[g]))

    n_win = WINDOW // KCH + 1
    win_ids = [jnp.maximum(n_chunks - (n_win - j), 0) for j in range(n_win)]
    win_dead = [jnp.where(n_chunks >= n_win - j, 0.0, NEG) for j in range(n_win)]
    park(scs_ref, scs)
    park(sbuf_ref, scores(ks_ref, 0))
    park(wbuf_ref, scores(kw_ref, win_ids[0]))

    expired_tri = jnp.where(key_row > in_blk_t, 0.0, NEG)
    win_blocks = WINDOW // SEL_LEN
    n_all = _iota((nsel, GW), 0)
    wbias_ref[...] = jnp.where((n_all >= cur - win_blocks) & (n_all <= cur), 0.0, NEG)
    cut(wbuf_ref, jnp.where(cur >= win_blocks, expired_tri, 0.0))

    ms_ref[...] = jnp.full(ms_ref.shape, 0.5 * NEG, F32)
    mw_ref[...] = jnp.full(mw_ref.shape, 0.5 * NEG, F32)
    accs_ref[...] = jnp.zeros(accs_ref.shape, F32)
    accw_ref[...] = jnp.zeros(accw_ref.shape, F32)

    def win_softmax(i, dead, lo=0, hi=per_chunk):
        shifts = [wbias_ref[pl.ds(i * per_chunk + nb, 1), :] + dead for nb in range(lo, hi)]
        for g in groups:
            blocks = [wbuf_ref[g, nb * SEL_LEN:(nb + 1) * SEL_LEN, :] for nb in range(lo, hi)]
            update(g, blocks, shifts, vwt_ref[i, g, :, lo * SEL_LEN:hi * SEL_LEN], mw_ref, accw_ref)

    first_lo = per_chunk // 2 if cur_slab >= per_chunk // 2 else 0
    for j in range(n_win - 1):
        nxt = scores(kw_ref, win_ids[j + 1])
        win_softmax(win_ids[j], win_dead[j], lo=first_lo if j == 0 else 0)
        park(wbuf_ref, nxt)

    imp_all = jnp.zeros((nsel, GW), F32)
    some_c = lane_t >= CMP_LEN - 1
    ones_rows = jnp.ones((16, ncmp), BF16)
    for g in groups:
        scm = jnp.where(ok_c, scs_ref[g], NEG)
        ec = jnp.exp2(scm - jnp.max(scm, axis=0, keepdims=True))
        both = _dot(jnp.concatenate([vct_ref[g], overlap, ones_rows], axis=0), ec.astype(BF16))
        denom = both[HEAD_DIM + nsel:HEAD_DIM + nsel + 1]
        both = both[0:HEAD_DIM + nsel] * jnp.where(some_c, 1.0 / jnp.maximum(denom, 1e-30), 0.0)
        oc_ref[g] = both[0:HEAD_DIM]
        imp = _spread_heads(both[HEAD_DIM:HEAD_DIM + nsel])
        imp_all = jnp.where(lane_grp == g, imp, imp_all)

    n_io = _iota((nsel, GW), 0)
    forced = (n_io == 0) | (n_io == cur) | (n_io == cur - 1)
    valid = n_io <= cur
    score = jnp.where(forced, FORCE_SCORE, imp_all)
    score = jnp.where(valid, score, -1.0)
    rows8 = [score[r:r + 8] for r in range(0, nsel, 8)]
    ranks = [jnp.zeros((8, GW), F32) for _ in rows8]
    sub = _iota((8, GW), 0)
    for m in range(nsel):
        rowv = score[m:m + 1, :]
        for k, blk in enumerate(rows8):
            ge = jnp.where(rowv >= blk, 1.0, 0.0)
            gt = jnp.where(rowv > blk, 1.0, 0.0)
            if 8 * k > m:
                inc = ge
            elif 8 * k + 7 < m:
                inc = gt
            else:
                inc = jnp.where(sub + 8 * k > m, ge, gt)
            ranks[k] = ranks[k] + inc
    rank = jnp.concatenate(ranks, axis=0)
    sel_all = jnp.where((rank < float(min(SEL_TOPN, nsel))) & valid, 1.0, 0.0)
    for g in groups:
        mine = jnp.where(lane_grp == g, sel_all, 0.0)
        bias_ref[g] = (_spread_heads(mine) - 1.0) * (-NEG)

    cut(wbuf_ref, causal_tri)
    win_softmax(win_ids[n_win - 1], win_dead[n_win - 1], hi=last_slabs)
    for g in groups:
        ow_ref[g] = accw_ref[g, 0:HEAD_DIM, :] * (1.0 / jnp.maximum(accw_ref[g, HEAD_DIM:HEAD_DIM + 1, :], 1e-30))


def _nsa(zq, qgain, kc, vct, kn, vt, ng, *, B, S):
    T = B * S
    qpb = 2 * (KCH // SEL_LEN)
    assert S % (Q_BLK * qpb) == 0
    nq = S // (Q_BLK * qpb)
    nch = S // KCH
    ncmp = S // CMP_STRIDE
    nsel = S // SEL_LEN
    tok = lambda b, i: (0, b * nq + i, 0)
    return pl.pallas_call(
        functools.partial(_nsa_kernel, nsel=nsel, qpb=qpb),
        grid=(B, nq),
        in_specs=[
            pl.BlockSpec((N_KV, Q_BLK * qpb, GW), tok),
            _const_spec((HEAD_DIM, GW)),
            pl.BlockSpec((None, N_KV, ncmp, HEAD_DIM), lambda b, i: (b, 0, 0, 0)),
            pl.BlockSpec((None, N_KV, HEAD_DIM, ncmp), lambda b, i: (b, 0, 0, 0)),
            pl.BlockSpec((None, None, 2, S, 128), lambda b, i: (0, b, 0, 0, 0)),
            pl.BlockSpec((None, None, 2, S, 128), lambda b, i: (1, b, 0, 0, 0)),
            pl.BlockSpec((None, None, nch, N_KV, VROWS, KCH), lambda b, i: (0, b, 0, 0, 0, 0)),
            pl.BlockSpec((None, None, nch, N_KV, VROWS, KCH), lambda b, i: (1, b, 0, 0, 0, 0)),
            pl.BlockSpec((Q_BLK * qpb, NG_PAD), lambda b, i: (b * nq + i, 0)),
        ],
        out_specs=pl.BlockSpec((N_KV, Q_BLK * qpb, GW), tok),
        out_shape=jax.ShapeDtypeStruct((N_KV, T, GW), BF16),
        scratch_shapes=[
            pltpu.VMEM((qpb, N_KV, 2 * HEAD_DIM, GW), BF16),
            pltpu.VMEM((qpb, N_KV, HEAD_DIM, GW), F32),
            pltpu.VMEM((qpb, N_KV, HEAD_DIM, GW), F32),
            pltpu.VMEM((qpb, N_KV, nsel, GW), F32),
            pltpu.VMEM((qpb, N_KV, 1, GW), F32),
            pltpu.VMEM((qpb, N_KV, VROWS, GW), F32),
            pltpu.VMEM((N_KV, 1, GW), F32),
            pltpu.VMEM((N_KV, VROWS, GW), F32),
            pltpu.VMEM((qpb, N_KV, KCH, GW), F32),
            pltpu.VMEM((N_KV, KCH, GW), F32),
            pltpu.VMEM((qpb, N_KV, 8, GW), F32),
            pltpu.VMEM((N_KV, ncmp, GW), F32),
            pltpu.VMEM((nsel, GW), F32),
        ],
        compiler_params=_params(("parallel", "arbitrary")),
        name="nsa",
    )(zq, qgain, kc, vct, kn, kn, vt, vt, ng)


def _merge_ffn_kernel(x_ref, ya_ref, o_ref_in, mg_ref, wn_ref, wo_ref, g2_ref, wg_ref, wu_ref, wd_ref,
                      out_ref, h_ref, *, tf):
    yb = _dot(o_ref_in[0], wn_ref[0:GW, :])
    for g in range(1, N_KV):
        yb = yb + _dot(o_ref_in[g], wn_ref[g * GW:(g + 1) * GW, :])
    tm = x_ref.shape[0]
    ya = jnp.concatenate([ya_ref[ck].reshape(tm, 128) for ck in range(D_MODEL // 128)], axis=1)
    mixed = (mg_ref[:, 0:D_MODEL].astype(F32) * ya
             + mg_ref[:, D_MODEL:2 * D_MODEL].astype(F32) * yb)
    x1 = x_ref[...] + _dot(mixed.astype(BF16), wo_ref[...])
    y = x1 * lax.rsqrt(jnp.mean(x1 * x1, axis=-1, keepdims=True) + EPS) * g2_ref[...]
    h_ref[...] = y.astype(BF16)
    out_ref[...] = x1
    for j in range(D_FF // tf):
        h = h_ref[...]
        sl = slice(j * tf, (j + 1) * tf)
        act = jax.nn.silu(_dot(h, wg_ref[:, sl])) * _dot(h, wu_ref[:, sl])
        out_ref[...] += _dot(act.astype(BF16), wd_ref[sl, :])


def _merge_ffn(x2d, ya, o, mg, wn, wo, gain2, wg, wu, wd, *, B, S, tm=512, tf=256):
    assert S % tm == 0 and D_FF % tf == 0
    ni = S // tm
    tok = lambda b, i: (b * ni + i, 0)
    once = lambda shape: pl.BlockSpec(shape, lambda b, i: (0,) * len(shape), pipeline_mode=pl.Buffered(1))
    return pl.pallas_call(
        functools.partial(_merge_ffn_kernel, tf=tf),
        grid=(B, ni),
        in_specs=[
            pl.BlockSpec((tm, D_MODEL), tok),
            pl.BlockSpec((D_MODEL // 128, tm // 8, None, 8, 128), lambda b, i: (0, i, b, 0, 0)),
            pl.BlockSpec((N_KV, tm, GW), lambda b, i: (0, b * ni + i, 0)),
            pl.BlockSpec((tm, 2 * D_MODEL), tok),
            once((N_HEADS * HEAD_DIM, D_MODEL)),
            once((D_MODEL, D_MODEL)),
            _const_spec((1, D_MODEL)),
            once((D_MODEL, D_FF)),
            once((D_MODEL, D_FF)),
            once((D_FF, D_MODEL)),
        ],
        out_specs=pl.BlockSpec((tm, D_MODEL), tok),
        out_shape=jax.ShapeDtypeStruct((B * S, D_MODEL), F32),
        scratch_shapes=[pltpu.VMEM((tm, D_MODEL), BF16)],
        compiler_params=_params(("parallel", "parallel")),
        name="merge_ffn",
    )(x2d, ya, o, mg, wn, wo, gain2, wg, wu, wd)


def _layer(x, p):
    B, S, _ = x.shape
    T = B * S
    dh = HEAD_DIM
    row = lambda v: v.reshape(1, -1)
    w_in = p["w_in"]
    n_ng = 3 * N_HEADS
    o_ng = 2 * D_RNN + N_HEADS * dh + 6 * KVW
    w_packed = jnp.concatenate(
        [w_in[:, :o_ng], jnp.pad(w_in[:, o_ng:o_ng + n_ng], ((0, 0), (0, NG_PAD - n_ng))), w_in[:, o_ng + n_ng:]],
        axis=1).astype(BF16)
    rx, gy, zq, zkv, ng, mg = _inproj(x, row(p["norm1"]), w_packed)

    slabs = lambda a: a.reshape(a.shape[0], S * B, 128)
    ya = _rglru(slabs(rx), slabs(gy), p["conv_w"], row(p["conv_b"]),
                p["rg_wa"].astype(BF16), row(p["rg_ba"]), p["rg_wi"].astype(BF16), row(p["rg_bi"]),
                row(p["rg_lambda"]), p["w_rg_out"].astype(BF16), nb=B)
    ya = ya.reshape(D_MODEL // 128, S // 8, B, 8, 128)

    kgains = jnp.tile(p["k_norm"][1:3], (1, N_KV)).reshape(2, 1, KVW)
    kn, vt = _kvprep(zkv, kgains, B=B, S=S)
    pos = jnp.stack([p["cmp_pos_k"], p["cmp_pos_v"]]).reshape(2, CMP_STRIDE, 2 * dh)
    w1 = jnp.stack([p["cmp_k_w1"], p["cmp_v_w1"]]).astype(BF16)
    w2 = jnp.stack([p["cmp_k_w2"], p["cmp_v_w2"]]).astype(BF16)
    kc, vct = _compress(zkv, pos, w1, w2, row(p["k_norm"][0]), B=B, S=S)
    qgain = jnp.broadcast_to((p["q_norm"] * (dh ** -0.5 * LOG2E))[:, None], (dh, GW))
    o = _nsa(zq, qgain, kc, vct, kn, vt, ng, B=B, S=S)

    x2 = _merge_ffn(x.reshape(T, D_MODEL), ya, o, mg, p["w_nsa_out"].astype(BF16), p["w_o"].astype(BF16),
                    row(p["norm2"]), p["w_gate"].astype(BF16), p["w_up"].astype(BF16), p["w_down"].astype(BF16),
                    B=B, S=S)
    return x2.reshape(B, S, D_MODEL)


def kernel(x, norm1, w_in, conv_w, conv_b, rg_wa, rg_ba, rg_wi, rg_bi, rg_lambda, q_norm, k_norm, cmp_pos_k,
           cmp_pos_v, cmp_k_w1, cmp_k_w2, cmp_v_w1, cmp_v_w2, w_rg_out, w_nsa_out, w_o, norm2, w_gate, w_up, w_down):
    params = dict(norm1=norm1, w_in=w_in, conv_w=conv_w, conv_b=conv_b, rg_wa=rg_wa, rg_ba=rg_ba, rg_wi=rg_wi,
                  rg_bi=rg_bi, rg_lambda=rg_lambda, q_norm=q_norm, k_norm=k_norm, cmp_pos_k=cmp_pos_k,
                  cmp_pos_v=cmp_pos_v, cmp_k_w1=cmp_k_w1, cmp_k_w2=cmp_k_w2, cmp_v_w1=cmp_v_w1, cmp_v_w2=cmp_v_w2,
                  w_rg_out=w_rg_out, w_nsa_out=w_nsa_out, w_o=w_o, norm2=norm2, w_gate=w_gate, w_up=w_up,
                  w_down=w_down)
    for l in range(norm1.shape[0]):
        x = _layer(x, {k: v[l] for k, v in params.items()})
    return x
```

```python
import functools

import jax
import jax.numpy as jnp
from jax import lax
from jax.experimental import pallas as pl
from jax.experimental.pallas import tpu as pltpu

D_MODEL = 1024
D_RNN = 1024
RG_BLOCKS = 4
RG_BW = D_RNN // RG_BLOCKS
CONV_W = 4
RG_C = 8.0
N_HEADS = 16
N_KV = 4
HEAD_DIM = 64
HPG = N_HEADS // N_KV
CMP_LEN = 32
CMP_STRIDE = 16
CMP_HID = 256
SEL_LEN = 64
SEL_TOPN = 16
WINDOW = 512
Q_BLK = 64
D_FF = 2816
EPS = 1e-6
FORCE_SCORE = 1e6
NEG = -1e30
LOG2E = 1.4426950408889634

KCH = 256
GW = HPG * HEAD_DIM
KVW = N_KV * HEAD_DIM
VROWS = 80
NG_PAD = 128

F32 = jnp.float32
BF16 = jnp.bfloat16

VMEM_LIMIT = 56 * 1024 * 1024


def _params(sem):
    return pltpu.CompilerParams(dimension_semantics=sem, vmem_limit_bytes=VMEM_LIMIT)


def _dot(a, b):
    return jnp.dot(a, b, preferred_element_type=F32)


def _dot_nt(a, b):
    return lax.dot_general(a, b, (((1,), (1,)), ((), ())), preferred_element_type=F32)


def _split(x):
    hi = x.astype(BF16)
    lo = (x - hi.astype(F32)).astype(BF16)
    return hi, lo


def _iota(shape, dim):
    return lax.broadcasted_iota(jnp.int32, shape, dim)


def _const_spec(shape):
    return pl.BlockSpec(shape, lambda *_: (0,) * len(shape))


_O_RY = D_RNN
_O_Q = 2 * D_RNN
_O_KV = _O_Q + N_HEADS * HEAD_DIM
_O_NG = _O_KV + 6 * KVW
_O_MG = _O_NG + NG_PAD
_W_COLS = _O_MG + 2 * D_MODEL


def _inproj_kernel(x_ref, g_ref, w_ref, kg_ref, rx_ref, gy_ref, zq_ref, zkv_ref, ng_ref, mg_ref, kn_ref, vt_ref,
                   h_ref):
    x = x_ref[...]
    y = x * lax.rsqrt(jnp.mean(x * x, axis=-1, keepdims=True) + EPS) * g_ref[...]
    h_ref[...] = y.astype(BF16)

    def proj(c0, width):
        return _dot(h_ref[...], w_ref[:, c0:c0 + width])

    cw = 256
    tm = x_ref.shape[0]
    for c in range(D_RNN // cw):
        rx = proj(c * cw, cw)
        gy = jax.nn.gelu(proj(_O_RY + c * cw, cw))
        for hf in range(cw // 128):
            rx_ref[c * (cw // 128) + hf] = rx[:, hf * 128:(hf + 1) * 128].reshape(tm // 8, 8, 128)
            gy_ref[c * (cw // 128) + hf] = gy[:, hf * 128:(hf + 1) * 128].reshape(tm // 8, 8, 128)
    for g in range(N_KV):
        zq_ref[g] = proj(_O_Q + g * GW, GW)
    for s in range(2):
        z = proj(_O_KV + s * KVW, KVW)
        for pr in range(KVW // 128):
            zkv_ref[s, pr] = z[:, pr * 128:(pr + 1) * 128]
    seg = jnp.where(_iota((KVW, KVW), 0) // HEAD_DIM == _iota((KVW, KVW), 1) // HEAD_DIM, 1.0, 0.0).astype(BF16)
    eye = jnp.where(_iota((KVW, KVW), 0) == _iota((KVW, KVW), 1), 1.0, 0.0).astype(BF16)
    pad = jnp.where(_iota((VROWS - HEAD_DIM, tm), 0) == 0, 1.0, 0.0).astype(BF16)
    for w in range(2):
        zk = proj(_O_KV + (2 + 2 * w) * KVW, KVW)
        hi, lo = _split(zk * zk)
        ssq = _dot(hi, seg) + _dot(lo, seg)
        kn = (zk * lax.rsqrt(ssq * (1.0 / HEAD_DIM) + EPS) * kg_ref[w]).astype(BF16)
        for pr in range(KVW // 128):
            kn_ref[w, pr] = kn[:, pr * 128:(pr + 1) * 128]
        zv = proj(_O_KV + (3 + 2 * w) * KVW, KVW).astype(BF16)
        vt = _dot_nt(eye, zv).astype(BF16)
        for g in range(N_KV):
            vt_ref[w, g, 0:HEAD_DIM, :] = vt[g * HEAD_DIM:(g + 1) * HEAD_DIM]
            vt_ref[w, g, HEAD_DIM:VROWS, :] = pad
    ng_ref[...] = jax.nn.sigmoid(proj(_O_NG, NG_PAD))
    for c in range(2 * D_MODEL // cw):
        mg_ref[:, c * cw:(c + 1) * cw] = jax.nn.sigmoid(proj(_O_MG + c * cw, cw)).astype(mg_ref.dtype)


def _inproj(x, gain, w, kgains, *, tm=KCH):
    B, S, D = x.shape
    T = B * S
    assert tm == KCH
    ni = S // tm
    nck = D_RNN // 128
    tok = lambda b, i: (b * ni + i, 0)
    slab_spec = pl.BlockSpec((nck, tm // 8, None, 8, 128), lambda b, i: (0, i, b, 0, 0))
    slab_shape = jax.ShapeDtypeStruct((nck, S // 8, B, 8, 128), F32)
    return pl.pallas_call(
        _inproj_kernel,
        grid=(B, ni),
        in_specs=[
            pl.BlockSpec((None, tm, D), lambda b, i: (b, i, 0)),
            _const_spec((1, D)),
            pl.BlockSpec((D, _W_COLS), lambda b, i: (0, 0), pipeline_mode=pl.Buffered(1)),
            _const_spec((2, 1, KVW)),
        ],
        out_specs=[
            slab_spec,
            slab_spec,
            pl.BlockSpec((N_KV, tm, GW), lambda b, i: (0, b * ni + i, 0)),
            pl.BlockSpec((2, KVW // 128, tm, 128), lambda b, i: (0, 0, b * ni + i, 0)),
            pl.BlockSpec((tm, NG_PAD), tok),
            pl.BlockSpec((tm, 2 * D_MODEL), tok),
            pl.BlockSpec((2, None, KVW // 128, tm, 128), lambda b, i: (0, b, 0, i, 0)),
            pl.BlockSpec((2, None, None, N_KV, VROWS, KCH), lambda b, i: (0, b, i, 0, 0, 0)),
        ],
        out_shape=[
            slab_shape,
            slab_shape,
            jax.ShapeDtypeStruct((N_KV, T, GW), F32),
            jax.ShapeDtypeStruct((2, KVW // 128, T, 128), F32),
            jax.ShapeDtypeStruct((T, NG_PAD), F32),
            jax.ShapeDtypeStruct((T, 2 * D_MODEL), BF16),
            jax.ShapeDtypeStruct((2, B, KVW // 128, S, 128), BF16),
            jax.ShapeDtypeStruct((2, B, ni, N_KV, VROWS, KCH), BF16),
        ],
        scratch_shapes=[pltpu.VMEM((tm, D), BF16)],
        compiler_params=_params(("parallel", "parallel")),
        name="inproj",
    )(x, gain, w, kgains)


def _rglru_kernel(x_ref, gy_ref, cw_ref, cb_ref, wa_ref, ba_ref, wi_ref, bi_ref, lam_ref, wo_ref,
                  o_ref, xb_ref, gyb_ref, a_ref, u_ref, hb_ref, hs_ref, yb_ref, *, nb, tt):
    R = nb * tt
    halo = (CONV_W - 1) * nb
    step = pl.program_id(0)
    nck = D_RNN // 128

    @pl.when(step == 0)
    def _():
        xb_ref[0:halo, :] = jnp.zeros((halo, D_RNN), F32)
        hs_ref[...] = jnp.zeros((nb, D_RNN), F32)

    def slab_rows(t8, t_lo):
        return pl.ds(t8 * (8 * nb) + t_lo, nb, stride=8)

    def stage_in(t8, carry):
        for t_lo in range(8):
            r0 = pl.multiple_of((t8 * 8 + t_lo) * nb, nb)
            for ck in range(nck):
                cols = slice(ck * 128, (ck + 1) * 128)
                xb_ref[pl.ds(halo + r0, nb), cols] = x_ref[ck, slab_rows(t8, t_lo), :]
                gyb_ref[pl.ds(r0, nb), cols] = gy_ref[ck, slab_rows(t8, t_lo), :]
        return carry

    lax.fori_loop(0, tt // 8, stage_in, 0)
    xr = cb_ref[...] + cw_ref[0:1, :] * xb_ref[0:R, :]
    for k in range(1, CONV_W):
        xr = xr + cw_ref[k:k + 1, :] * xb_ref[k * nb:k * nb + R, :]
    tail = xb_ref[R:R + halo, :]
    xb_ref[0:halo, :] = tail

    xrb = xr.astype(BF16)
    decay_log2 = (-RG_C * LOG2E) * jax.nn.softplus(-lam_ref[...])
    row = _iota((R, RG_BW), 0)
    first = (row < nb) & (step == 0)
    for n in range(RG_BLOCKS):
        sl = slice(n * RG_BW, (n + 1) * RG_BW)
        xn = xrb[:, sl]
        r = jax.nn.sigmoid(_dot(xn, wa_ref[n]) + ba_ref[:, sl])
        ig = jax.nn.sigmoid(_dot(xn, wi_ref[n]) + bi_ref[:, sl])
        a = jnp.exp2(r * decay_log2[:, sl])
        mult = jnp.sqrt(1.0 - a * a)
        mult = jnp.where(first, 1.0, mult)
        a_ref[:, sl] = a
        u_ref[:, sl] = mult * ig * xr[:, sl]

    def body(t, h):
        r0 = pl.multiple_of(t * nb, nb)
        h = a_ref[pl.ds(r0, nb), :] * h + u_ref[pl.ds(r0, nb), :]
        hb_ref[pl.ds(r0, nb), :] = h
        return h

    h = lax.fori_loop(0, tt, body, hs_ref[...], unroll=8)
    hs_ref[...] = h
    yb_ref[...] = _dot((hb_ref[...] * gyb_ref[...]).astype(BF16), wo_ref[...])

    def stage_out(t8, carry):
        for t_lo in range(8):
            r0 = pl.multiple_of((t8 * 8 + t_lo) * nb, nb)
            for ck in range(nck):
                o_ref[ck, slab_rows(t8, t_lo), :] = yb_ref[pl.ds(r0, nb), ck * 128:(ck + 1) * 128]
        return carry

    lax.fori_loop(0, tt // 8, stage_out, 0)


def _rglru(rx, gy, conv_w, conv_b, wa, ba, wi, bi, lam, wo, *, nb, tt=64):
    nck, rows, _ = rx.shape
    S = rows // nb
    assert S % tt == 0 and tt % 8 == 0
    R = nb * tt
    halo = (CONV_W - 1) * nb
    return pl.pallas_call(
        functools.partial(_rglru_kernel, nb=nb, tt=tt),
        grid=(S // tt,),
        in_specs=[
            pl.BlockSpec((nck, R, 128), lambda s: (0, s, 0)),
            pl.BlockSpec((nck, R, 128), lambda s: (0, s, 0)),
            _const_spec((CONV_W, D_RNN)),
            _const_spec((1, D_RNN)),
            _const_spec((RG_BLOCKS, RG_BW, RG_BW)),
            _const_spec((1, D_RNN)),
            _const_spec((RG_BLOCKS, RG_BW, RG_BW)),
            _const_spec((1, D_RNN)),
            _const_spec((1, D_RNN)),
            _const_spec((D_RNN, D_MODEL)),
        ],
        out_specs=pl.BlockSpec((D_MODEL // 128, R, 128), lambda s: (0, s, 0)),
        out_shape=jax.ShapeDtypeStruct((D_MODEL // 128, rows, 128), F32),
        scratch_shapes=[
            pltpu.VMEM((R + halo, D_RNN), F32),
            pltpu.VMEM((R, D_RNN), F32),
            pltpu.VMEM((R, D_RNN), F32),
            pltpu.VMEM((R, D_RNN), F32),
            pltpu.VMEM((R, D_RNN), F32),
            pltpu.VMEM((nb, D_RNN), F32),
            pltpu.VMEM((R, D_MODEL), F32),
        ],
        compiler_params=_params(("arbitrary",)),
        name="rglru",
    )(rx, gy, conv_w, conv_b, wa, ba, wi, bi, lam, wo)


def _compress_kernel(zk_ref, zv_ref, pos_ref, w1_ref, w2_ref, g_ref, kc_ref, vct_ref):
    nch = zk_ref.shape[0] // CMP_STRIDE
    lane = _iota((nch, 128), 1)
    npair = CMP_STRIDE // 2
    eye = jnp.where(_iota((HEAD_DIM, HEAD_DIM), 0) == _iota((HEAD_DIM, HEAD_DIM), 1), 1.0, 0.0).astype(BF16)
    for kind, z_ref in enumerate((zk_ref, zv_ref)):
        for ge in range(2):
            combs = []
            for p in range(npair):
                a = z_ref[pl.ds(2 * p, nch, stride=CMP_STRIDE), :]
                b = z_ref[pl.ds(2 * p + 1, nch, stride=CMP_STRIDE), :]
                if ge == 0:
                    combs.append(jnp.where(lane < HEAD_DIM, a, pltpu.roll(b, HEAD_DIM, axis=1)))
                else:
                    combs.append(jnp.where(lane < HEAD_DIM, pltpu.roll(a, HEAD_DIM, axis=1), b))
            halves = []
            half_w = npair * 128
            for half in range(2):
                lhs = jnp.concatenate(
                    [(combs[p] + pos_ref[kind, half * npair + p:half * npair + p + 1, :]).astype(BF16)
                     for p in range(npair)], axis=1)
                halves.append(_dot(lhs, w1_ref[kind, half * half_w:(half + 1) * half_w, :]))
            hid = halves[0] + pltpu.roll(halves[1], nch - 1, axis=0)
            hid = jax.nn.gelu(hid)
            out = _dot(hid.astype(BF16), w2_ref[kind])
            if kind == 0:
                out = out * lax.rsqrt(jnp.mean(out * out, axis=-1, keepdims=True) + EPS) * g_ref[...]
            res = jnp.where(_iota(out.shape, 0) < nch - 1, out, 0.0).astype(BF16)
            if kind == 0:
                kc_ref[ge] = res
            else:
                vct_ref[ge] = _dot_nt(eye, res).astype(BF16)


def _compress(zkv, pos, w1, w2, gain, *, B, S):
    nch = S // CMP_STRIDE
    return pl.pallas_call(
        _compress_kernel,
        grid=(B, KVW // 128),
        in_specs=[
            pl.BlockSpec((None, None, S, 128), lambda b, lp: (0, lp, b, 0)),
            pl.BlockSpec((None, None, S, 128), lambda b, lp: (1, lp, b, 0)),
            _const_spec((2, CMP_STRIDE, 128)),
            _const_spec((2, CMP_LEN * HEAD_DIM, CMP_HID)),
            _const_spec((2, CMP_HID, HEAD_DIM)),
            _const_spec((1, HEAD_DIM)),
        ],
        out_specs=[
            pl.BlockSpec((None, 2, nch, HEAD_DIM), lambda b, lp: (b, lp, 0, 0)),
            pl.BlockSpec((None, 2, HEAD_DIM, nch), lambda b, lp: (b, lp, 0, 0)),
        ],
        out_shape=[
            jax.ShapeDtypeStruct((B, N_KV, nch, HEAD_DIM), BF16),
            jax.ShapeDtypeStruct((B, N_KV, HEAD_DIM, nch), BF16),
        ],
        compiler_params=_params(("parallel", "parallel")),
        name="compress",
    )(zkv, zkv, pos, w1, w2, gain)


def _spread_heads(x):
    y = x + pltpu.roll(x, 2 * Q_BLK, axis=1)
    return y + pltpu.roll(y, Q_BLK, axis=1)


def _swap_heads(x):
    n = x.shape[0]
    lane = _iota((n, 128), 1)
    halves = []
    for pr in range(x.shape[1] // 128):
        slab = jnp.concatenate([x[:, pr * 128:(pr + 1) * 128], jnp.zeros((128 - n, 128), x.dtype)], axis=0)
        tr = slab.T
        halves.append(jnp.where(lane < n, tr[0:n], pltpu.roll(tr[n:2 * n], n, axis=1)))
    return jnp.concatenate(halves, axis=1)


def _nsa_kernel(*refs, nsel, qpb):
    per_chunk = KCH // SEL_LEN
    for k in range(qpb):
        _nsa_block(k, "head", None, *refs, nsel=nsel, qpb=qpb)

    def step(i, carry):
        for k in range(qpb):
            _nsa_block(k, "step", i, *refs, nsel=nsel, qpb=qpb)
        return carry

    common = pl.program_id(1) * (qpb // per_chunk)
    lax.fori_loop(0, common, step, 0)
    for k in range(qpb):
        for extra in range(k // per_chunk):
            _nsa_block(k, "step", common + extra, *refs, nsel=nsel, qpb=qpb)
    for k in range(qpb):
        _nsa_block(k, "tail", None, *refs, nsel=nsel, qpb=qpb)


def _nsa_block(k, phase, step_i, zq_ref, qg_ref, kc_ref, vct_ref, ks_ref, kw_ref, vst_ref, vwt_ref, ng_ref, o_ref,
               qtp_ref, oc_ref, ow_ref, bias_ref, ms_ref, accs_ref, mw_ref, accw_ref, sbuf_ref, wbuf_ref,
               gates_ref, scs_ref, wbias_ref, *, nsel, qpb):
    qtp_ref, oc_ref, ow_ref, bias_ref, ms_ref, accs_ref, sbuf_ref, gates_ref = (
        r.at[k] for r in (qtp_ref, oc_ref, ow_ref, bias_ref, ms_ref, accs_ref, sbuf_ref, gates_ref))
    qi = pl.program_id(1) * qpb + k
    q_rows = pl.ds(k * Q_BLK, Q_BLK)
    s0 = qi * Q_BLK
    cur = qi
    ncmp = kc_ref.shape[1]
    per_chunk = KCH // SEL_LEN
    groups = range(N_KV)
    n_chunks = qi // per_chunk + 1
    cur_slab = k % per_chunk
    cut_rows = slice(cur_slab * SEL_LEN, (cur_slab + 1) * SEL_LEN)
    last_slabs = per_chunk // 2 if cur_slab < per_chunk // 2 else per_chunk

    def scores(k_ref, i):
        k0 = pl.multiple_of(i * KCH, KCH)
        return [_dot(k_ref[g // 2, pl.ds(k0, KCH), :], qtp_ref[g]) for g in groups]

    def park(buf_ref, vals):
        for g in groups:
            buf_ref[g] = vals[g]

    def cut(buf_ref, tri):
        for g in groups:
            buf_ref[g, cut_rows, :] = buf_ref[g, cut_rows, :] + tri

    def update(g, blocks, shifts, vt, m_ref, acc_ref):
        m = m_ref[g]
        part = None
        for blk, sh in zip(blocks, shifts):
            top8 = jnp.max(blk.reshape(SEL_LEN // 8, 8, GW), axis=0) + sh
            part = top8 if part is None else jnp.maximum(part, top8)
        mnew = jnp.maximum(m, jnp.max(part, axis=0, keepdims=True))
        e = jnp.concatenate([jnp.exp2(blk + (sh - mnew)) for blk, sh in zip(blocks, shifts)], axis=0).astype(BF16)
        acc_ref[g] = jnp.exp2(m - mnew) * acc_ref[g] + _dot(vt, e)
        m_ref[g] = mnew

    def slabs(buf_ref, g, ns):
        return [buf_ref[g, nb * SEL_LEN:(nb + 1) * SEL_LEN, :] for nb in range(ns)]

    def sel_softmax(i, ns=per_chunk):
        for g in groups:
            shifts = [bias_ref[g, pl.ds(i * per_chunk + nb, 1), :] for nb in range(ns)]
            update(g, slabs(sbuf_ref, g, ns), shifts, vst_ref[i, g, :, 0:ns * SEL_LEN], ms_ref, accs_ref)

    key_row = _iota((SEL_LEN, GW), 0)
    in_blk_t = _iota((SEL_LEN, GW), 1) % Q_BLK
    causal_tri = jnp.where(key_row <= in_blk_t, 0.0, NEG)

    if phase == "step":
        nxt = scores(ks_ref, step_i + 1)
        sel_softmax(step_i)
        park(sbuf_ref, nxt)
        return

    if phase == "tail":
        cut(sbuf_ref, causal_tri)
        sel_softmax(n_chunks - 1, last_slabs)
        outs = []
        for g in groups:
            o_s = accs_ref[g, 0:HEAD_DIM, :] * (1.0 / jnp.maximum(accs_ref[g, HEAD_DIM:HEAD_DIM + 1, :], 1e-30))
            gate = [gates_ref[g, c:c + 1, :] for c in range(3)]
            o_t = gate[0] * oc_ref[g] + gate[1] * o_s + gate[2] * ow_ref[g]
            outs.append(_swap_heads(o_t))
        for g in groups:
            o_ref[g, q_rows, :] = outs[g].astype(o_ref.dtype)
        return

    lane_t = s0 + (_iota((1, GW), 1) % Q_BLK)
    lane_grp = _iota((1, GW), 1) // Q_BLK
    ov_n = _iota((nsel, ncmp), 0) * SEL_LEN
    ov_c = _iota((nsel, ncmp), 1) * CMP_STRIDE
    overlap = jnp.where((ov_c < ov_n + SEL_LEN) & (ov_c + CMP_LEN > ov_n), 1.0, 0.0).astype(BF16)


    ok_c = _iota((ncmp, GW), 0) * CMP_STRIDE + (CMP_LEN - 1) <= lane_t
    ngt = jnp.concatenate([ng_ref[q_rows, :], jnp.zeros((128 - Q_BLK, NG_PAD), F32)], axis=0).T
    ngt_hi = pltpu.roll(ngt, Q_BLK, axis=1)
    low_half = _iota((1, 128), 1) < Q_BLK
    for g in groups:
        for c in range(3):
            r = c * N_HEADS + g * HPG
            pairs = [jnp.where(low_half, ngt[r + j:r + j + 1], ngt_hi[r + j + 1:r + j + 2]) for j in (0, 2)]
            gates_ref[g, c:c + 1, :] = jnp.concatenate(pairs, axis=1)

    qts = []
    for g in groups:
        xt = _swap_heads(zq_ref[g, q_rows, :])
        ssq = jnp.sum(xt * xt, axis=0, keepdims=True)
        qts.append((xt * lax.rsqrt(ssq * (1.0 / HEAD_DIM) + EPS) * qg_ref[...]).astype(BF16))
    scs = []
    for g in groups:
        off = (g % 2) * HEAD_DIM
        qtp_ref[g, off:off + HEAD_DIM, :] = qts[g]
        qtp_ref[g, HEAD_DIM - off:2 * HEAD_DIM - off, :] = jnp.zeros((HEAD_DIM, GW), BF16)
        scs.append(_dot(kc_ref[g], qts[g]))

    n_win = WINDOW // KCH + 1
    win_ids = [jnp.maximum(n_chunks - (n_win - j), 0) for j in range(n_win)]
    win_dead = [jnp.where(n_chunks >= n_win - j, 0.0, NEG) for j in range(n_win)]
    park(scs_ref, scs)
    park(sbuf_ref, scores(ks_ref, 0))
    park(wbuf_ref, scores(kw_ref, win_ids[0]))

    expired_tri = jnp.where(key_row > in_blk_t, 0.0, NEG)
    win_blocks = WINDOW // SEL_LEN
    n_all = _iota((nsel, GW), 0)
    wbias_ref[...] = jnp.where((n_all >= cur - win_blocks) & (n_all <= cur), 0.0, NEG)
    cut(wbuf_ref, jnp.where(cur >= win_blocks, expired_tri, 0.0))

    ms_ref[...] = jnp.full(ms_ref.shape, 0.5 * NEG, F32)
    mw_ref[...] = jnp.full(mw_ref.shape, 0.5 * NEG, F32)
    accs_ref[...] = jnp.zeros(accs_ref.shape, F32)
    accw_ref[...] = jnp.zeros(accw_ref.shape, F32)

    def win_softmax(i, dead, lo=0, hi=per_chunk):
        shifts = [wbias_ref[pl.ds(i * per_chunk + nb, 1), :] + dead for nb in range(lo, hi)]
        for g in groups:
            blocks = [wbuf_ref[g, nb * SEL_LEN:(nb + 1) * SEL_LEN, :] for nb in range(lo, hi)]
            update(g, blocks, shifts, vwt_ref[i, g, :, lo * SEL_LEN:hi * SEL_LEN], mw_ref, accw_ref)

    first_lo = per_chunk // 2 if cur_slab >= per_chunk // 2 else 0
    for j in range(n_win - 1):
        nxt = scores(kw_ref, win_ids[j + 1])
        win_softmax(win_ids[j], win_dead[j], lo=first_lo if j == 0 else 0)
        park(wbuf_ref, nxt)

    imp_all = jnp.zeros((nsel, GW), F32)
    some_c = lane_t >= CMP_LEN - 1
    ones_rows = jnp.ones((16, ncmp), BF16)
    for g in groups:
        scm = jnp.where(ok_c, scs_ref[g], NEG)
        ec = jnp.exp2(scm - jnp.max(scm, axis=0, keepdims=True))
        both = _dot(jnp.concatenate([vct_ref[g], overlap, ones_rows], axis=0), ec.astype(BF16))
        denom = both[HEAD_DIM + nsel:HEAD_DIM + nsel + 1]
        both = both[0:HEAD_DIM + nsel] * jnp.where(some_c, 1.0 / jnp.maximum(denom, 1e-30), 0.0)
        oc_ref[g] = both[0:HEAD_DIM]
        imp = _spread_heads(both[HEAD_DIM:HEAD_DIM + nsel])
        imp_all = jnp.where(lane_grp == g, imp, imp_all)

    n_io = _iota((nsel, GW), 0)
    forced = (n_io == 0) | (n_io == cur) | (n_io == cur - 1)
    valid = n_io <= cur
    score = jnp.where(forced, FORCE_SCORE, imp_all)
    score = jnp.where(valid, score, -1.0)
    rows8 = [score[r:r + 8] for r in range(0, nsel, 8)]
    ranks = [jnp.zeros((8, GW), F32) for _ in rows8]
    sub = _iota((8, GW), 0)
    for m in range(nsel):
        rowv = score[m:m + 1, :]
        for k, blk in enumerate(rows8):
            ge = jnp.where(rowv >= blk, 1.0, 0.0)
            gt = jnp.where(rowv > blk, 1.0, 0.0)
            if 8 * k > m:
                inc = ge
            elif 8 * k + 7 < m:
                inc = gt
            else:
                inc = jnp.where(sub + 8 * k > m, ge, gt)
            ranks[k] = ranks[k] + inc
    rank = jnp.concatenate(ranks, axis=0)
    sel_all = jnp.where((rank < float(min(SEL_TOPN, nsel))) & valid, 1.0, 0.0)
    for g in groups:
        mine = jnp.where(lane_grp == g, sel_all, 0.0)
        bias_ref[g] = (_spread_heads(mine) - 1.0) * (-NEG)

    cut(wbuf_ref, causal_tri)
    win_softmax(win_ids[n_win - 1], win_dead[n_win - 1], hi=last_slabs)
    for g in groups:
        ow_ref[g] = accw_ref[g, 0:HEAD_DIM, :] * (1.0 / jnp.maximum(accw_ref[g, HEAD_DIM:HEAD_DIM + 1, :], 1e-30))


def _nsa(zq, qgain, kc, vct, kn, vt, ng, *, B, S):
    T = B * S
    qpb = 2 * (KCH // SEL_LEN)
    assert S % (Q_BLK * qpb) == 0
    nq = S // (Q_BLK * qpb)
    nch = S // KCH
    ncmp = S // CMP_STRIDE
    nsel = S // SEL_LEN
    tok = lambda b, i: (0, b * nq + i, 0)
    return pl.pallas_call(
        functools.partial(_nsa_kernel, nsel=nsel, qpb=qpb),
        grid=(B, nq),
        in_specs=[
            pl.BlockSpec((N_KV, Q_BLK * qpb, GW), tok),
            _const_spec((HEAD_DIM, GW)),
            pl.BlockSpec((None, N_KV, ncmp, HEAD_DIM), lambda b, i: (b, 0, 0, 0)),
            pl.BlockSpec((None, N_KV, HEAD_DIM, ncmp), lambda b, i: (b, 0, 0, 0)),
            pl.BlockSpec((None, None, 2, S, 128), lambda b, i: (0, b, 0, 0, 0)),
            pl.BlockSpec((None, None, 2, S, 128), lambda b, i: (1, b, 0, 0, 0)),
            pl.BlockSpec((None, None, nch, N_KV, VROWS, KCH), lambda b, i: (0, b, 0, 0, 0, 0)),
            pl.BlockSpec((None, None, nch, N_KV, VROWS, KCH), lambda b, i: (1, b, 0, 0, 0, 0)),
            pl.BlockSpec((Q_BLK * qpb, NG_PAD), lambda b, i: (b * nq + i, 0)),
        ],
        out_specs=pl.BlockSpec((N_KV, Q_BLK * qpb, GW), tok),
        out_shape=jax.ShapeDtypeStruct((N_KV, T, GW), BF16),
        scratch_shapes=[
            pltpu.VMEM((qpb, N_KV, 2 * HEAD_DIM, GW), BF16),
            pltpu.VMEM((qpb, N_KV, HEAD_DIM, GW), F32),
            pltpu.VMEM((qpb, N_KV, HEAD_DIM, GW), F32),
            pltpu.VMEM((qpb, N_KV, nsel, GW), F32),
            pltpu.VMEM((qpb, N_KV, 1, GW), F32),
            pltpu.VMEM((qpb, N_KV, VROWS, GW), F32),
            pltpu.VMEM((N_KV, 1, GW), F32),
            pltpu.VMEM((N_KV, VROWS, GW), F32),
            pltpu.VMEM((qpb, N_KV, KCH, GW), F32),
            pltpu.VMEM((N_KV, KCH, GW), F32),
            pltpu.VMEM((qpb, N_KV, 8, GW), F32),
            pltpu.VMEM((N_KV, ncmp, GW), F32),
            pltpu.VMEM((nsel, GW), F32),
        ],
        compiler_params=_params(("parallel", "arbitrary")),
        name="nsa",
    )(zq, qgain, kc, vct, kn, kn, vt, vt, ng)


def _merge_ffn_kernel(x_ref, ya_ref, o_ref_in, mg_ref, wn_ref, wo_ref, g2_ref, wg_ref, wu_ref, wd_ref,
                      out_ref, h_ref, *, tf):
    yb = _dot(o_ref_in[0], wn_ref[0:GW, :])
    for g in range(1, N_KV):
        yb = yb + _dot(o_ref_in[g], wn_ref[g * GW:(g + 1) * GW, :])
    tm = x_ref.shape[0]
    ya = jnp.concatenate([ya_ref[ck].reshape(tm, 128) for ck in range(D_MODEL // 128)], axis=1)
    mixed = (mg_ref[:, 0:D_MODEL].astype(F32) * ya
             + mg_ref[:, D_MODEL:2 * D_MODEL].astype(F32) * yb)
    x1 = x_ref[...] + _dot(mixed.astype(BF16), wo_ref[...])
    y = x1 * lax.rsqrt(jnp.mean(x1 * x1, axis=-1, keepdims=True) + EPS) * g2_ref[...]
    h_ref[...] = y.astype(BF16)
    out_ref[...] = x1
    for j in range(D_FF // tf):
        h = h_ref[...]
        sl = slice(j * tf, (j + 1) * tf)
        act = jax.nn.silu(_dot(h, wg_ref[:, sl])) * _dot(h, wu_ref[:, sl])
        out_ref[...] += _dot(act.astype(BF16), wd_ref[sl, :])


def _merge_ffn(x2d, ya, o, mg, wn, wo, gain2, wg, wu, wd, *, B, S, tm=512, tf=256):
    assert S % tm == 0 and D_FF % tf == 0
    ni = S // tm
    tok = lambda b, i: (b * ni + i, 0)
    once = lambda shape: pl.BlockSpec(shape, lambda b, i: (0,) * len(shape), pipeline_mode=pl.Buffered(1))
    return pl.pallas_call(
        functools.partial(_merge_ffn_kernel, tf=tf),
        grid=(B, ni),
        in_specs=[
            pl.BlockSpec((tm, D_MODEL), tok),
            pl.BlockSpec((D_MODEL // 128, tm // 8, None, 8, 128), lambda b, i: (0, i, b, 0, 0)),
            pl.BlockSpec((N_KV, tm, GW), lambda b, i: (0, b * ni + i, 0)),
            pl.BlockSpec((tm, 2 * D_MODEL), tok),
            once((N_HEADS * HEAD_DIM, D_MODEL)),
            once((D_MODEL, D_MODEL)),
            _const_spec((1, D_MODEL)),
            once((D_MODEL, D_FF)),
            once((D_MODEL, D_FF)),
            once((D_FF, D_MODEL)),
        ],
        out_specs=pl.BlockSpec((tm, D_MODEL), tok),
        out_shape=jax.ShapeDtypeStruct((B * S, D_MODEL), F32),
        scratch_shapes=[pltpu.VMEM((tm, D_MODEL), BF16)],
        compiler_params=_params(("parallel", "parallel")),
        name="merge_ffn",
    )(x2d, ya, o, mg, wn, wo, gain2, wg, wu, wd)


def _layer(x, p):
    B, S, _ = x.shape
    T = B * S
    dh = HEAD_DIM
    row = lambda v: v.reshape(1, -1)
    w_in = p["w_in"]
    n_ng = 3 * N_HEADS
    o_ng = 2 * D_RNN + N_HEADS * dh + 6 * KVW
    w_packed = jnp.concatenate(
        [w_in[:, :o_ng], jnp.pad(w_in[:, o_ng:o_ng + n_ng], ((0, 0), (0, NG_PAD - n_ng))), w_in[:, o_ng + n_ng:]],
        axis=1).astype(BF16)
    kgains = jnp.tile(p["k_norm"][1:3], (1, N_KV)).reshape(2, 1, KVW)
    rx, gy, zq, zkv, ng, mg, kn, vt = _inproj(x, row(p["norm1"]), w_packed, kgains)

    slabs = lambda a: a.reshape(a.shape[0], S * B, 128)
    ya = _rglru(slabs(rx), slabs(gy), p["conv_w"], row(p["conv_b"]),
                p["rg_wa"].astype(BF16), row(p["rg_ba"]), p["rg_wi"].astype(BF16), row(p["rg_bi"]),
                row(p["rg_lambda"]), p["w_rg_out"].astype(BF16), nb=B)
    ya = ya.reshape(D_MODEL // 128, S // 8, B, 8, 128)

    pos = jnp.stack([p["cmp_pos_k"], p["cmp_pos_v"]]).reshape(2, CMP_STRIDE, 2 * dh)
    w1 = jnp.stack([p["cmp_k_w1"], p["cmp_v_w1"]]).astype(BF16)
    w2 = jnp.stack([p["cmp_k_w2"], p["cmp_v_w2"]]).astype(BF16)
    kc, vct = _compress(zkv, pos, w1, w2, row(p["k_norm"][0]), B=B, S=S)
    qgain = jnp.broadcast_to((p["q_norm"] * (dh ** -0.5 * LOG2E))[:, None], (dh, GW))
    o = _nsa(zq, qgain, kc, vct, kn, vt, ng, B=B, S=S)

    x2 = _merge_ffn(x.reshape(T, D_MODEL), ya, o, mg, p["w_nsa_out"].astype(BF16), p["w_o"].astype(BF16),
                    row(p["norm2"]), p["w_gate"].astype(BF16), p["w_up"].astype(BF16), p["w_down"].astype(BF16),
                    B=B, S=S)
    return x2.reshape(B, S, D_MODEL)


def kernel(x, norm1, w_in, conv_w, conv_b, rg_wa, rg_ba, rg_wi, rg_bi, rg_lambda, q_norm, k_norm, cmp_pos_k,
           cmp_pos_v, cmp_k_w1, cmp_k_w2, cmp_v_w1, cmp_v_w2, w_rg_out, w_nsa_out, w_o, norm2, w_gate, w_up, w_down):
    params = dict(norm1=norm1, w_in=w_in, conv_w=conv_w, conv_b=conv_b, rg_wa=rg_wa, rg_ba=rg_ba, rg_wi=rg_wi,
                  rg_bi=rg_bi, rg_lambda=rg_lambda, q_norm=q_norm, k_norm=k_norm, cmp_pos_k=cmp_pos_k,
                  cmp_pos_v=cmp_pos_v, cmp_k_w1=cmp_k_w1, cmp_k_w2=cmp_k_w2, cmp_v_w1=cmp_v_w1, cmp_v_w2=cmp_v_w2,
                  w_rg_out=w_rg_out, w_nsa_out=w_nsa_out, w_o=w_o, norm2=norm2, w_gate=w_gate, w_up=w_up,
                  w_down=w_down)
    for l in range(norm1.shape[0]):
        x = _layer(x, {k: v[l] for k, v in params.items()})
    return x
```

```python
import functools

import jax
import jax.numpy as jnp
from jax import lax
from jax.experimental import pallas as pl
from jax.experimental.pallas import tpu as pltpu

D_MODEL = 1024
D_RNN = 1024
RG_BLOCKS = 4
RG_BW = D_RNN // RG_BLOCKS
CONV_W = 4
RG_C = 8.0
N_HEADS = 16
N_KV = 4
HEAD_DIM = 64
HPG = N_HEADS // N_KV
CMP_LEN = 32
CMP_STRIDE = 16
CMP_HID = 256
SEL_LEN = 64
SEL_TOPN = 16
WINDOW = 512
Q_BLK = 64
D_FF = 2816
EPS = 1e-6
FORCE_SCORE = 1e6
NEG = -1e30
LOG2E = 1.4426950408889634

KCH = 256
GW = HPG * HEAD_DIM
KVW = N_KV * HEAD_DIM
VROWS = 80
NG_PAD = 128

F32 = jnp.float32
BF16 = jnp.bfloat16

VMEM_LIMIT = 56 * 1024 * 1024


def _params(sem):
    return pltpu.CompilerParams(dimension_semantics=sem, vmem_limit_bytes=VMEM_LIMIT)


def _dot(a, b):
    return jnp.dot(a, b, preferred_element_type=F32)


def _dot_nt(a, b):
    return lax.dot_general(a, b, (((1,), (1,)), ((), ())), preferred_element_type=F32)


def _split(x):
    hi = x.astype(BF16)
    lo = (x - hi.astype(F32)).astype(BF16)
    return hi, lo


def _iota(shape, dim):
    return lax.broadcasted_iota(jnp.int32, shape, dim)


def _const_spec(shape):
    return pl.BlockSpec(shape, lambda *_: (0,) * len(shape))


_O_RY = D_RNN
_O_Q = 2 * D_RNN
_O_KV = _O_Q + N_HEADS * HEAD_DIM
_O_NG = _O_KV + 6 * KVW
_O_MG = _O_NG + NG_PAD
_W_COLS = _O_MG + 2 * D_MODEL


def _inproj_kernel(x_ref, g_ref, w_ref, rx_ref, gy_ref, zq_ref, zkv_ref, ng_ref, mg_ref, h_ref):
    x = x_ref[...]
    y = x * lax.rsqrt(jnp.mean(x * x, axis=-1, keepdims=True) + EPS) * g_ref[...]
    h_ref[...] = y.astype(BF16)

    def proj(c0, width):
        return _dot(h_ref[...], w_ref[:, c0:c0 + width])

    cw = 256
    tm = x_ref.shape[0]
    for c in range(D_RNN // cw):
        rx = proj(c * cw, cw)
        gy = jax.nn.gelu(proj(_O_RY + c * cw, cw))
        for hf in range(cw // 128):
            rx_ref[c * (cw // 128) + hf] = rx[:, hf * 128:(hf + 1) * 128].reshape(tm // 8, 8, 128)
            gy_ref[c * (cw // 128) + hf] = gy[:, hf * 128:(hf + 1) * 128].reshape(tm // 8, 8, 128)
    for g in range(N_KV):
        zq_ref[g] = proj(_O_Q + g * GW, GW)
    for s in range(6):
        z = proj(_O_KV + s * KVW, KVW)
        for pr in range(KVW // 128):
            zkv_ref[s, pr] = z[:, pr * 128:(pr + 1) * 128]
    ng_ref[...] = jax.nn.sigmoid(proj(_O_NG, NG_PAD))
    for c in range(2 * D_MODEL // cw):
        mg_ref[:, c * cw:(c + 1) * cw] = jax.nn.sigmoid(proj(_O_MG + c * cw, cw)).astype(mg_ref.dtype)


def _inproj(x, gain, w, *, tm=512):
    B, S, D = x.shape
    T = B * S
    ni = S // tm
    nck = D_RNN // 128
    tok = lambda b, i: (b * ni + i, 0)
    slab_spec = pl.BlockSpec((nck, tm // 8, None, 8, 128), lambda b, i: (0, i, b, 0, 0))
    slab_shape = jax.ShapeDtypeStruct((nck, S // 8, B, 8, 128), F32)
    return pl.pallas_call(
        _inproj_kernel,
        grid=(B, ni),
        in_specs=[
            pl.BlockSpec((None, tm, D), lambda b, i: (b, i, 0)),
            _const_spec((1, D)),
            pl.BlockSpec((D, _W_COLS), lambda b, i: (0, 0), pipeline_mode=pl.Buffered(1)),
        ],
        out_specs=[
            slab_spec,
            slab_spec,
            pl.BlockSpec((N_KV, tm, GW), lambda b, i: (0, b * ni + i, 0)),
            pl.BlockSpec((6, KVW // 128, tm, 128), lambda b, i: (0, 0, b * ni + i, 0)),
            pl.BlockSpec((tm, NG_PAD), tok),
            pl.BlockSpec((tm, 2 * D_MODEL), tok),
        ],
        out_shape=[
            slab_shape,
            slab_shape,
            jax.ShapeDtypeStruct((N_KV, T, GW), F32),
            jax.ShapeDtypeStruct((6, KVW // 128, T, 128), F32),
            jax.ShapeDtypeStruct((T, NG_PAD), F32),
            jax.ShapeDtypeStruct((T, 2 * D_MODEL), BF16),
        ],
        scratch_shapes=[pltpu.VMEM((tm, D), BF16)],
        compiler_params=_params(("parallel", "parallel")),
        name="inproj",
    )(x, gain, w)


def _rglru_kernel(x_ref, gy_ref, cw_ref, cb_ref, wa_ref, ba_ref, wi_ref, bi_ref, lam_ref, wo_ref,
                  o_ref, xb_ref, gyb_ref, a_ref, u_ref, hb_ref, hs_ref, yb_ref, *, nb, tt):
    R = nb * tt
    halo = (CONV_W - 1) * nb
    step = pl.program_id(0)
    nck = D_RNN // 128

    @pl.when(step == 0)
    def _():
        xb_ref[0:halo, :] = jnp.zeros((halo, D_RNN), F32)
        hs_ref[...] = jnp.zeros((nb, D_RNN), F32)

    def slab_rows(t8, t_lo):
        return pl.ds(t8 * (8 * nb) + t_lo, nb, stride=8)

    def stage_in(t8, carry):
        for t_lo in range(8):
            r0 = pl.multiple_of((t8 * 8 + t_lo) * nb, nb)
            for ck in range(nck):
                cols = slice(ck * 128, (ck + 1) * 128)
                xb_ref[pl.ds(halo + r0, nb), cols] = x_ref[ck, slab_rows(t8, t_lo), :]
                gyb_ref[pl.ds(r0, nb), cols] = gy_ref[ck, slab_rows(t8, t_lo), :]
        return carry

    lax.fori_loop(0, tt // 8, stage_in, 0)
    xr = cb_ref[...] + cw_ref[0:1, :] * xb_ref[0:R, :]
    for k in range(1, CONV_W):
        xr = xr + cw_ref[k:k + 1, :] * xb_ref[k * nb:k * nb + R, :]
    tail = xb_ref[R:R + halo, :]
    xb_ref[0:halo, :] = tail

    xrb = xr.astype(BF16)
    decay_log2 = (-RG_C * LOG2E) * jax.nn.softplus(-lam_ref[...])
    row = _iota((R, RG_BW), 0)
    first = (row < nb) & (step == 0)
    for n in range(RG_BLOCKS):
        sl = slice(n * RG_BW, (n + 1) * RG_BW)
        xn = xrb[:, sl]
        r = jax.nn.sigmoid(_dot(xn, wa_ref[n]) + ba_ref[:, sl])
        ig = jax.nn.sigmoid(_dot(xn, wi_ref[n]) + bi_ref[:, sl])
        a = jnp.exp2(r * decay_log2[:, sl])
        mult = jnp.sqrt(1.0 - a * a)
        mult = jnp.where(first, 1.0, mult)
        a_ref[:, sl] = a
        u_ref[:, sl] = mult * ig * xr[:, sl]

    def body(t, h):
        r0 = pl.multiple_of(t * nb, nb)
        h = a_ref[pl.ds(r0, nb), :] * h + u_ref[pl.ds(r0, nb), :]
        hb_ref[pl.ds(r0, nb), :] = h
        return h

    h = lax.fori_loop(0, tt, body, hs_ref[...], unroll=8)
    hs_ref[...] = h
    yb_ref[...] = _dot((hb_ref[...] * gyb_ref[...]).astype(BF16), wo_ref[...])

    def stage_out(t8, carry):
        for t_lo in range(8):
            r0 = pl.multiple_of((t8 * 8 + t_lo) * nb, nb)
            for ck in range(nck):
                o_ref[ck, slab_rows(t8, t_lo), :] = yb_ref[pl.ds(r0, nb), ck * 128:(ck + 1) * 128]
        return carry

    lax.fori_loop(0, tt // 8, stage_out, 0)


def _rglru(rx, gy, conv_w, conv_b, wa, ba, wi, bi, lam, wo, *, nb, tt=64):
    nck, rows, _ = rx.shape
    S = rows // nb
    assert S % tt == 0 and tt % 8 == 0
    R = nb * tt
    halo = (CONV_W - 1) * nb
    return pl.pallas_call(
        functools.partial(_rglru_kernel, nb=nb, tt=tt),
        grid=(S // tt,),
        in_specs=[
            pl.BlockSpec((nck, R, 128), lambda s: (0, s, 0)),
            pl.BlockSpec((nck, R, 128), lambda s: (0, s, 0)),
            _const_spec((CONV_W, D_RNN)),
            _const_spec((1, D_RNN)),
            _const_spec((RG_BLOCKS, RG_BW, RG_BW)),
            _const_spec((1, D_RNN)),
            _const_spec((RG_BLOCKS, RG_BW, RG_BW)),
            _const_spec((1, D_RNN)),
            _const_spec((1, D_RNN)),
            _const_spec((D_RNN, D_MODEL)),
        ],
        out_specs=pl.BlockSpec((D_MODEL // 128, R, 128), lambda s: (0, s, 0)),
        out_shape=jax.ShapeDtypeStruct((D_MODEL // 128, rows, 128), F32),
        scratch_shapes=[
            pltpu.VMEM((R + halo, D_RNN), F32),
            pltpu.VMEM((R, D_RNN), F32),
            pltpu.VMEM((R, D_RNN), F32),
            pltpu.VMEM((R, D_RNN), F32),
            pltpu.VMEM((R, D_RNN), F32),
            pltpu.VMEM((nb, D_RNN), F32),
            pltpu.VMEM((R, D_MODEL), F32),
        ],
        compiler_params=_params(("arbitrary",)),
        name="rglru",
    )(rx, gy, conv_w, conv_b, wa, ba, wi, bi, lam, wo)


def _kvprep_kernel(zk_ref, zv_ref, g_ref, kn_ref, vt_ref):
    rows = zk_ref.shape[1]
    x = jnp.concatenate([zk_ref[0], zk_ref[1]], axis=1)
    seg = jnp.where(_iota((KVW, KVW), 0) // HEAD_DIM == _iota((KVW, KVW), 1) // HEAD_DIM, 1.0, 0.0).astype(BF16)
    hi, lo = _split(x * x)
    ssq = _dot(jnp.concatenate([hi, lo], axis=0), seg)
    kn = (x * lax.rsqrt((ssq[0:rows] + ssq[rows:2 * rows]) * (1.0 / HEAD_DIM) + EPS) * g_ref[...]).astype(BF16)
    for pr in range(KVW // 128):
        kn_ref[pr] = kn[:, pr * 128:(pr + 1) * 128]
    eye = jnp.where(_iota((KVW, KVW), 0) == _iota((KVW, KVW), 1), 1.0, 0.0).astype(BF16)
    pad = jnp.where(_iota((VROWS - HEAD_DIM, KCH), 0) == 0, 1.0, 0.0).astype(BF16)
    for c in range(rows // KCH):
        sl = slice(c * KCH, (c + 1) * KCH)
        v = jnp.concatenate([zv_ref[0, sl, :], zv_ref[1, sl, :]], axis=1).astype(BF16)
        vt = _dot_nt(eye, v).astype(BF16)
        for g in range(N_KV):
            vt_ref[c, g, 0:HEAD_DIM, :] = vt[g * HEAD_DIM:(g + 1) * HEAD_DIM]
            vt_ref[c, g, HEAD_DIM:VROWS, :] = pad


def _kvprep(zkv, gains, *, B, S, cpb=4):
    nch = S // KCH
    assert nch % cpb == 0
    nblk = nch // cpb
    rows = cpb * KCH
    return pl.pallas_call(
        _kvprep_kernel,
        grid=(2, B, nblk),
        in_specs=[
            pl.BlockSpec((None, KVW // 128, rows, 128), lambda w, b, i: (2 + 2 * w, 0, b * nblk + i, 0)),
            pl.BlockSpec((None, KVW // 128, rows, 128), lambda w, b, i: (3 + 2 * w, 0, b * nblk + i, 0)),
            pl.BlockSpec((None, 1, KVW), lambda w, b, i: (w, 0, 0)),
        ],
        out_specs=[
            pl.BlockSpec((None, None, KVW // 128, rows, 128), lambda w, b, i: (w, b, 0, i, 0)),
            pl.BlockSpec((None, None, cpb, N_KV, VROWS, KCH), lambda w, b, i: (w, b, i, 0, 0, 0)),
        ],
        out_shape=[
            jax.ShapeDtypeStruct((2, B, KVW // 128, S, 128), BF16),
            jax.ShapeDtypeStruct((2, B, nch, N_KV, VROWS, KCH), BF16),
        ],
        compiler_params=_params(("parallel", "parallel", "parallel")),
        name="kvprep",
    )(zkv, zkv, gains)


def _compress_kernel(zk_ref, zv_ref, pos_ref, w1_ref, w2_ref, g_ref, kc_ref, vct_ref):
    nch = zk_ref.shape[0] // CMP_STRIDE
    lane = _iota((nch, 128), 1)
    npair = CMP_STRIDE // 2
    eye = jnp.where(_iota((HEAD_DIM, HEAD_DIM), 0) == _iota((HEAD_DIM, HEAD_DIM), 1), 1.0, 0.0).astype(BF16)
    for kind, z_ref in enumerate((zk_ref, zv_ref)):
        for ge in range(2):
            combs = []
            for p in range(npair):
                a = z_ref[pl.ds(2 * p, nch, stride=CMP_STRIDE), :]
                b = z_ref[pl.ds(2 * p + 1, nch, stride=CMP_STRIDE), :]
                if ge == 0:
                    combs.append(jnp.where(lane < HEAD_DIM, a, pltpu.roll(b, HEAD_DIM, axis=1)))
                else:
                    combs.append(jnp.where(lane < HEAD_DIM, pltpu.roll(a, HEAD_DIM, axis=1), b))
            halves = []
            half_w = npair * 128
            for half in range(2):
                lhs = jnp.concatenate(
                    [(combs[p] + pos_ref[kind, half * npair + p:half * npair + p + 1, :]).astype(BF16)
                     for p in range(npair)], axis=1)
                halves.append(_dot(lhs, w1_ref[kind, half * half_w:(half + 1) * half_w, :]))
            hid = halves[0] + pltpu.roll(halves[1], nch - 1, axis=0)
            hid = jax.nn.gelu(hid)
            out = _dot(hid.astype(BF16), w2_ref[kind])
            if kind == 0:
                out = out * lax.rsqrt(jnp.mean(out * out, axis=-1, keepdims=True) + EPS) * g_ref[...]
            res = jnp.where(_iota(out.shape, 0) < nch - 1, out, 0.0).astype(BF16)
            if kind == 0:
                kc_ref[ge] = res
            else:
                vct_ref[ge] = _dot_nt(eye, res).astype(BF16)


def _compress(zkv, pos, w1, w2, gain, *, B, S):
    nch = S // CMP_STRIDE
    return pl.pallas_call(
        _compress_kernel,
        grid=(B, KVW // 128),
        in_specs=[
            pl.BlockSpec((None, None, S, 128), lambda b, lp: (0, lp, b, 0)),
            pl.BlockSpec((None, None, S, 128), lambda b, lp: (1, lp, b, 0)),
            _const_spec((2, CMP_STRIDE, 128)),
            _const_spec((2, CMP_LEN * HEAD_DIM, CMP_HID)),
            _const_spec((2, CMP_HID, HEAD_DIM)),
            _const_spec((1, HEAD_DIM)),
        ],
        out_specs=[
            pl.BlockSpec((None, 2, nch, HEAD_DIM), lambda b, lp: (b, lp, 0, 0)),
            pl.BlockSpec((None, 2, HEAD_DIM, nch), lambda b, lp: (b, lp, 0, 0)),
        ],
        out_shape=[
            jax.ShapeDtypeStruct((B, N_KV, nch, HEAD_DIM), BF16),
            jax.ShapeDtypeStruct((B, N_KV, HEAD_DIM, nch), BF16),
        ],
        compiler_params=_params(("parallel", "parallel")),
        name="compress",
    )(zkv, zkv, pos, w1, w2, gain)


def _spread_heads(x):
    y = x + pltpu.roll(x, 2 * Q_BLK, axis=1)
    return y + pltpu.roll(y, Q_BLK, axis=1)


def _swap_heads(x):
    n = x.shape[0]
    lane = _iota((n, 128), 1)
    halves = []
    for pr in range(x.shape[1] // 128):
        slab = jnp.concatenate([x[:, pr * 128:(pr + 1) * 128], jnp.zeros((128 - n, 128), x.dtype)], axis=0)
        tr = slab.T
        halves.append(jnp.where(lane < n, tr[0:n], pltpu.roll(tr[n:2 * n], n, axis=1)))
    return jnp.concatenate(halves, axis=1)


def _nsa_kernel(*refs, nsel, qpb):
    per_chunk = KCH // SEL_LEN
    for k in range(qpb):
        _nsa_block(k, "head", None, *refs, nsel=nsel, qpb=qpb)

    def step(i, carry):
        for k in range(qpb):
            _nsa_block(k, "step", i, *refs, nsel=nsel, qpb=qpb)
        return carry

    common = pl.program_id(1) * (qpb // per_chunk)
    lax.fori_loop(0, common, step, 0)
    for k in range(qpb):
        for extra in range(k // per_chunk):
            _nsa_block(k, "step", common + extra, *refs, nsel=nsel, qpb=qpb)
    for k in range(qpb):
        _nsa_block(k, "tail", None, *refs, nsel=nsel, qpb=qpb)


def _nsa_block(k, phase, step_i, zq_ref, qg_ref, kc_ref, vct_ref, ks_ref, kw_ref, vst_ref, vwt_ref, ng_ref, o_ref,
               qtp_ref, oc_ref, ow_ref, bias_ref, ms_ref, accs_ref, mw_ref, accw_ref, sbuf_ref, wbuf_ref,
               gates_ref, scs_ref, wbias_ref, *, nsel, qpb):
    qtp_ref, oc_ref, ow_ref, bias_ref, ms_ref, accs_ref, sbuf_ref, gates_ref = (
        r.at[k] for r in (qtp_ref, oc_ref, ow_ref, bias_ref, ms_ref, accs_ref, sbuf_ref, gates_ref))
    qi = pl.program_id(1) * qpb + k
    q_rows = pl.ds(k * Q_BLK, Q_BLK)
    s0 = qi * Q_BLK
    cur = qi
    ncmp = kc_ref.shape[1]
    per_chunk = KCH // SEL_LEN
    groups = range(N_KV)
    n_chunks = qi // per_chunk + 1
    cur_slab = k % per_chunk
    cut_rows = slice(cur_slab * SEL_LEN, (cur_slab + 1) * SEL_LEN)
    last_slabs = per_chunk // 2 if cur_slab < per_chunk // 2 else per_chunk

    def scores(k_ref, i):
        k0 = pl.multiple_of(i * KCH, KCH)
        return [_dot(k_ref[g // 2, pl.ds(k0, KCH), :], qtp_ref[g]) for g in groups]

    def park(buf_ref, vals):
        for g in groups:
            buf_ref[g] = vals[g]

    def cut(buf_ref, tri):
        for g in groups:
            buf_ref[g, cut_rows, :] = buf_ref[g, cut_rows, :] + tri

    def update(g, blocks, shifts, vt, m_ref, acc_ref):
        m = m_ref[g]
        part = None
        for blk, sh in zip(blocks, shifts):
            top8 = jnp.max(blk.reshape(SEL_LEN // 8, 8, GW), axis=0) + sh
            part = top8 if part is None else jnp.maximum(part, top8)
        mnew = jnp.maximum(m, jnp.max(part, axis=0, keepdims=True))
        e = jnp.concatenate([jnp.exp2(blk + (sh - mnew)) for blk, sh in zip(blocks, shifts)], axis=0).astype(BF16)
        acc_ref[g] = jnp.exp2(m - mnew) * acc_ref[g] + _dot(vt, e)
        m_ref[g] = mnew

    def slabs(buf_ref, g, ns):
        return [buf_ref[g, nb * SEL_LEN:(nb + 1) * SEL_LEN, :] for nb in range(ns)]

    def sel_softmax(i, ns=per_chunk):
        for g in groups:
            shifts = [bias_ref[g, pl.ds(i * per_chunk + nb, 1), :] for nb in range(ns)]
            update(g, slabs(sbuf_ref, g, ns), shifts, vst_ref[i, g, :, 0:ns * SEL_LEN], ms_ref, accs_ref)

    key_row = _iota((SEL_LEN, GW), 0)
    in_blk_t = _iota((SEL_LEN, GW), 1) % Q_BLK
    causal_tri = jnp.where(key_row <= in_blk_t, 0.0, NEG)

    if phase == "step":
        nxt = scores(ks_ref, step_i + 1)
        sel_softmax(step_i)
        park(sbuf_ref, nxt)
        return

    if phase == "tail":
        cut(sbuf_ref, causal_tri)
        sel_softmax(n_chunks - 1, last_slabs)
        outs = []
        for g in groups:
            o_s = accs_ref[g, 0:HEAD_DIM, :] * (1.0 / jnp.maximum(accs_ref[g, HEAD_DIM:HEAD_DIM + 1, :], 1e-30))
            gate = [gates_ref[g, c:c + 1, :] for c in range(3)]
            o_t = gate[0] * oc_ref[g] + gate[1] * o_s + gate[2] * ow_ref[g]
            outs.append(_swap_heads(o_t))
        for g in groups:
            o_ref[g, q_rows, :] = outs[g].astype(o_ref.dtype)
        return

    lane_t = s0 + (_iota((1, GW), 1) % Q_BLK)
    lane_grp = _iota((1, GW), 1) // Q_BLK
    ov_n = _iota((nsel, ncmp), 0) * SEL_LEN
    ov_c = _iota((nsel, ncmp), 1) * CMP_STRIDE
    overlap = jnp.where((ov_c < ov_n + SEL_LEN) & (ov_c + CMP_LEN > ov_n), 1.0, 0.0).astype(BF16)


    ok_c = _iota((ncmp, GW), 0) * CMP_STRIDE + (CMP_LEN - 1) <= lane_t
    ngt = jnp.concatenate([ng_ref[q_rows, :], jnp.zeros((128 - Q_BLK, NG_PAD), F32)], axis=0).T
    ngt_hi = pltpu.roll(ngt, Q_BLK, axis=1)
    low_half = _iota((1, 128), 1) < Q_BLK
    for g in groups:
        for c in range(3):
            r = c * N_HEADS + g * HPG
            pairs = [jnp.where(low_half, ngt[r + j:r + j + 1], ngt_hi[r + j + 1:r + j + 2]) for j in (0, 2)]
            gates_ref[g, c:c + 1, :] = jnp.concatenate(pairs, axis=1)

    qts = []
    for g in groups:
        xt = _swap_heads(zq_ref[g, q_rows, :])
        ssq = jnp.sum(xt * xt, axis=0, keepdims=True)
        qts.append((xt * lax.rsqrt(ssq * (1.0 / HEAD_DIM) + EPS) * qg_ref[...]).astype(BF16))
    scs = []
    for g in groups:
        off = (g % 2) * HEAD_DIM
        qtp_ref[g, off:off + HEAD_DIM, :] = qts[g]
        qtp_ref[g, HEAD_DIM - off:2 * HEAD_DIM - off, :] = jnp.zeros((HEAD_DIM, GW), BF16)
        scs.append(_dot(kc_ref[g], qts[g]))

    n_win = WINDOW // KCH + 1
    win_ids = [jnp.maximum(n_chunks - (n_win - j), 0) for j in range(n_win)]
    win_dead = [jnp.where(n_chunks >= n_win - j, 0.0, NEG) for j in range(n_win)]
    park(scs_ref, scs)
    park(sbuf_ref, scores(ks_ref, 0))
    park(wbuf_ref, scores(kw_ref, win_ids[0]))

    expired_tri = jnp.where(key_row > in_blk_t, 0.0, NEG)
    win_blocks = WINDOW // SEL_LEN
    n_all = _iota((nsel, GW), 0)
    wbias_ref[...] = jnp.where((n_all >= cur - win_blocks) & (n_all <= cur), 0.0, NEG)
    cut(wbuf_ref, jnp.where(cur >= win_blocks, expired_tri, 0.0))

    ms_ref[...] = jnp.full(ms_ref.shape, 0.5 * NEG, F32)
    mw_ref[...] = jnp.full(mw_ref.shape, 0.5 * NEG, F32)
    accs_ref[...] = jnp.zeros(accs_ref.shape, F32)
    accw_ref[...] = jnp.zeros(accw_ref.shape, F32)

    def win_softmax(i, dead, lo=0, hi=per_chunk):
        shifts = [wbias_ref[pl.ds(i * per_chunk + nb, 1), :] + dead for nb in range(lo, hi)]
        for g in groups:
            blocks = [wbuf_ref[g, nb * SEL_LEN:(nb + 1) * SEL_LEN, :] for nb in range(lo, hi)]
            update(g, blocks, shifts, vwt_ref[i, g, :, lo * SEL_LEN:hi * SEL_LEN], mw_ref, accw_ref)

    first_lo = per_chunk // 2 if cur_slab >= per_chunk // 2 else 0
    for j in range(n_win - 1):
        nxt = scores(kw_ref, win_ids[j + 1])
        win_softmax(win_ids[j], win_dead[j], lo=first_lo if j == 0 else 0)
        park(wbuf_ref, nxt)

    imp_all = jnp.zeros((nsel, GW), F32)
    some_c = lane_t >= CMP_LEN - 1
    ones_rows = jnp.ones((16, ncmp), BF16)
    for g in groups:
        scm = jnp.where(ok_c, scs_ref[g], NEG)
        ec = jnp.exp2(scm - jnp.max(scm, axis=0, keepdims=True))
        both = _dot(jnp.concatenate([vct_ref[g], overlap, ones_rows], axis=0), ec.astype(BF16))
        denom = both[HEAD_DIM + nsel:HEAD_DIM + nsel + 1]
        both = both[0:HEAD_DIM + nsel] * jnp.where(some_c, 1.0 / jnp.maximum(denom, 1e-30), 0.0)
        oc_ref[g] = both[0:HEAD_DIM]
        imp = _spread_heads(both[HEAD_DIM:HEAD_DIM + nsel])
        imp_all = jnp.where(lane_grp == g, imp, imp_all)

    n_io = _iota((nsel, GW), 0)
    forced = (n_io == 0) | (n_io == cur) | (n_io == cur - 1)
    valid = n_io <= cur
    score = jnp.where(forced, FORCE_SCORE, imp_all)
    score = jnp.where(valid, score, -1.0)
    rows8 = [score[r:r + 8] for r in range(0, nsel, 8)]
    ranks = [jnp.zeros((8, GW), F32) for _ in rows8]
    sub = _iota((8, GW), 0)
    for m in range(nsel):
        rowv = score[m:m + 1, :]
        for k, blk in enumerate(rows8):
            ge = jnp.where(rowv >= blk, 1.0, 0.0)
            gt = jnp.where(rowv > blk, 1.0, 0.0)
            if 8 * k > m:
                inc = ge
            elif 8 * k + 7 < m:
                inc = gt
            else:
                inc = jnp.where(sub + 8 * k > m, ge, gt)
            ranks[k] = ranks[k] + inc
    rank = jnp.concatenate(ranks, axis=0)
    sel_all = jnp.where((rank < float(min(SEL_TOPN, nsel))) & valid, 1.0, 0.0)
    for g in groups:
        mine = jnp.where(lane_grp == g, sel_all, 0.0)
        bias_ref[g] = (_spread_heads(mine) - 1.0) * (-NEG)

    cut(wbuf_ref, causal_tri)
    win_softmax(win_ids[n_win - 1], win_dead[n_win - 1], hi=last_slabs)
    for g in groups:
        ow_ref[g] = accw_ref[g, 0:HEAD_DIM, :] * (1.0 / jnp.maximum(accw_ref[g, HEAD_DIM:HEAD_DIM + 1, :], 1e-30))


def _nsa(zq, qgain, kc, vct, kn, vt, ng, *, B, S):
    T = B * S
    qpb = 2 * (KCH // SEL_LEN)
    assert S % (Q_BLK * qpb) == 0
    nq = S // (Q_BLK * qpb)
    nch = S // KCH
    ncmp = S // CMP_STRIDE
    nsel = S // SEL_LEN
    tok = lambda b, i: (0, b * nq + i, 0)
    return pl.pallas_call(
        functools.partial(_nsa_kernel, nsel=nsel, qpb=qpb),
        grid=(B, nq),
        in_specs=[
            pl.BlockSpec((N_KV, Q_BLK * qpb, GW), tok),
            _const_spec((HEAD_DIM, GW)),
            pl.BlockSpec((None, N_KV, ncmp, HEAD_DIM), lambda b, i: (b, 0, 0, 0)),
            pl.BlockSpec((None, N_KV, HEAD_DIM, ncmp), lambda b, i: (b, 0, 0, 0)),
            pl.BlockSpec((None, None, 2, S, 128), lambda b, i: (0, b, 0, 0, 0)),
            pl.BlockSpec((None, None, 2, S, 128), lambda b, i: (1, b, 0, 0, 0)),
            pl.BlockSpec((None, None, nch, N_KV, VROWS, KCH), lambda b, i: (0, b, 0, 0, 0, 0)),
            pl.BlockSpec((None, None, nch, N_KV, VROWS, KCH), lambda b, i: (1, b, 0, 0, 0, 0)),
            pl.BlockSpec((Q_BLK * qpb, NG_PAD), lambda b, i: (b * nq + i, 0)),
        ],
        out_specs=pl.BlockSpec((N_KV, Q_BLK * qpb, GW), tok),
        out_shape=jax.ShapeDtypeStruct((N_KV, T, GW), BF16),
        scratch_shapes=[
            pltpu.VMEM((qpb, N_KV, 2 * HEAD_DIM, GW), BF16),
            pltpu.VMEM((qpb, N_KV, HEAD_DIM, GW), F32),
            pltpu.VMEM((qpb, N_KV, HEAD_DIM, GW), F32),
            pltpu.VMEM((qpb, N_KV, nsel, GW), F32),
            pltpu.VMEM((qpb, N_KV, 1, GW), F32),
            pltpu.VMEM((qpb, N_KV, VROWS, GW), F32),
            pltpu.VMEM((N_KV, 1, GW), F32),
            pltpu.VMEM((N_KV, VROWS, GW), F32),
            pltpu.VMEM((qpb, N_KV, KCH, GW), F32),
            pltpu.VMEM((N_KV, KCH, GW), F32),
            pltpu.VMEM((qpb, N_KV, 8, GW), F32),
            pltpu.VMEM((N_KV, ncmp, GW), F32),
            pltpu.VMEM((nsel, GW), F32),
        ],
        compiler_params=_params(("parallel", "arbitrary")),
        name="nsa",
    )(zq, qgain, kc, vct, kn, kn, vt, vt, ng)


def _merge_ffn_kernel(x_ref, ya_ref, o_ref_in, mg_ref, wn_ref, wo_ref, g2_ref, wg_ref, wu_ref, wd_ref,
                      out_ref, h_ref, *, tf):
    yb = _dot(o_ref_in[0], wn_ref[0:GW, :])
    for g in range(1, N_KV):
        yb = yb + _dot(o_ref_in[g], wn_ref[g * GW:(g + 1) * GW, :])
    tm = x_ref.shape[0]
    ya = jnp.concatenate([ya_ref[ck].reshape(tm, 128) for ck in range(D_MODEL // 128)], axis=1)
    mixed = (mg_ref[:, 0:D_MODEL].astype(F32) * ya
             + mg_ref[:, D_MODEL:2 * D_MODEL].astype(F32) * yb)
    x1 = x_ref[...] + _dot(mixed.astype(BF16), wo_ref[...])
    y = x1 * lax.rsqrt(jnp.mean(x1 * x1, axis=-1, keepdims=True) + EPS) * g2_ref[...]
    h_ref[...] = y.astype(BF16)
    out_ref[...] = x1
    for j in range(D_FF // tf):
        h = h_ref[...]
        sl = slice(j * tf, (j + 1) * tf)
        act = jax.nn.silu(_dot(h, wg_ref[:, sl])) * _dot(h, wu_ref[:, sl])
        out_ref[...] += _dot(act.astype(BF16), wd_ref[sl, :])


def _merge_ffn(x2d, ya, o, mg, wn, wo, gain2, wg, wu, wd, *, B, S, tm=512, tf=256):
    assert S % tm == 0 and D_FF % tf == 0
    ni = S // tm
    tok = lambda b, i: (b * ni + i, 0)
    once = lambda shape: pl.BlockSpec(shape, lambda b, i: (0,) * len(shape), pipeline_mode=pl.Buffered(1))
    return pl.pallas_call(
        functools.partial(_merge_ffn_kernel, tf=tf),
        grid=(B, ni),
        in_specs=[
            pl.BlockSpec((tm, D_MODEL), tok),
            pl.BlockSpec((D_MODEL // 128, tm // 8, None, 8, 128), lambda b, i: (0, i, b, 0, 0)),
            pl.BlockSpec((N_KV, tm, GW), lambda b, i: (0, b * ni + i, 0)),
            pl.BlockSpec((tm, 2 * D_MODEL), tok),
            once((N_HEADS * HEAD_DIM, D_MODEL)),
            once((D_MODEL, D_MODEL)),
            _const_spec((1, D_MODEL)),
            once((D_MODEL, D_FF)),
            once((D_MODEL, D_FF)),
            once((D_FF, D_MODEL)),
        ],
        out_specs=pl.BlockSpec((tm, D_MODEL), tok),
        out_shape=jax.ShapeDtypeStruct((B * S, D_MODEL), F32),
        scratch_shapes=[pltpu.VMEM((tm, D_MODEL), BF16)],
        compiler_params=_params(("parallel", "parallel")),
        name="merge_ffn",
    )(x2d, ya, o, mg, wn, wo, gain2, wg, wu, wd)


def _layer(x, p):
    B, S, _ = x.shape
    T = B * S
    dh = HEAD_DIM
    row = lambda v: v.reshape(1, -1)
    w_in = p["w_in"]
    n_ng = 3 * N_HEADS
    o_ng = 2 * D_RNN + N_HEADS * dh + 6 * KVW
    w_packed = jnp.concatenate(
        [w_in[:, :o_ng], jnp.pad(w_in[:, o_ng:o_ng + n_ng], ((0, 0), (0, NG_PAD - n_ng))), w_in[:, o_ng + n_ng:]],
        axis=1).astype(BF16)
    rx, gy, zq, zkv, ng, mg = _inproj(x, row(p["norm1"]), w_packed)

    slabs = lambda a: a.reshape(a.shape[0], S * B, 128)
    ya = _rglru(slabs(rx), slabs(gy), p["conv_w"], row(p["conv_b"]),
                p["rg_wa"].astype(BF16), row(p["rg_ba"]), p["rg_wi"].astype(BF16), row(p["rg_bi"]),
                row(p["rg_lambda"]), p["w_rg_out"].astype(BF16), nb=B)
    ya = ya.reshape(D_MODEL // 128, S // 8, B, 8, 128)

    kgains = jnp.tile(p["k_norm"][1:3], (1, N_KV)).reshape(2, 1, KVW)
    kn, vt = _kvprep(zkv, kgains, B=B, S=S)
    pos = jnp.stack([p["cmp_pos_k"], p["cmp_pos_v"]]).reshape(2, CMP_STRIDE, 2 * dh)
    w1 = jnp.stack([p["cmp_k_w1"], p["cmp_v_w1"]]).astype(BF16)
    w2 = jnp.stack([p["cmp_k_w2"], p["cmp_v_w2"]]).astype(BF16)
    kc, vct = _compress(zkv, pos, w1, w2, row(p["k_norm"][0]), B=B, S=S)
    qgain = jnp.broadcast_to((p["q_norm"] * (dh ** -0.5 * LOG2E))[:, None], (dh, GW))
    o = _nsa(zq, qgain, kc, vct, kn, vt, ng, B=B, S=S)

    x2 = _merge_ffn(x.reshape(T, D_MODEL), ya, o, mg, p["w_nsa_out"].astype(BF16), p["w_o"].astype(BF16),
                    row(p["norm2"]), p["w_gate"].astype(BF16), p["w_up"].astype(BF16), p["w_down"].astype(BF16),
                    B=B, S=S)
    return x2.reshape(B, S, D_MODEL)


def kernel(x, norm1, w_in, conv_w, conv_b, rg_wa, rg_ba, rg_wi, rg_bi, rg_lambda, q_norm, k_norm, cmp_pos_k,
           cmp_pos_v, cmp_k_w1, cmp_k_w2, cmp_v_w1, cmp_v_w2, w_rg_out, w_nsa_out, w_o, norm2, w_gate, w_up, w_down):
    params = dict(norm1=norm1, w_in=w_in, conv_w=conv_w, conv_b=conv_b, rg_wa=rg_wa, rg_ba=rg_ba, rg_wi=rg_wi,
                  rg_bi=rg_bi, rg_lambda=rg_lambda, q_norm=q_norm, k_norm=k_norm, cmp_pos_k=cmp_pos_k,
                  cmp_pos_v=cmp_pos_v, cmp_k_w1=cmp_k_w1, cmp_k_w2=cmp_k_w2, cmp_v_w1=cmp_v_w1, cmp_v_w2=cmp_v_w2,
                  w_rg_out=w_rg_out, w_nsa_out=w_nsa_out, w_o=w_o, norm2=norm2, w_gate=w_gate, w_up=w_up,
                  w_down=w_down)
    for l in range(norm1.shape[0]):
        x = _layer(x, {k: v[l] for k, v in params.items()})
    return x
```

```python
import functools

import jax
import jax.numpy as jnp
from jax import lax
from jax.experimental import pallas as pl
from jax.experimental.pallas import tpu as pltpu

D_MODEL = 1024
D_RNN = 1024
RG_BLOCKS = 4
RG_BW = D_RNN // RG_BLOCKS
CONV_W = 4
RG_C = 8.0
N_HEADS = 16
N_KV = 4
HEAD_DIM = 64
HPG = N_HEADS // N_KV
CMP_LEN = 32
CMP_STRIDE = 16
CMP_HID = 256
SEL_LEN = 64
SEL_TOPN = 16
WINDOW = 512
Q_BLK = 64
D_FF = 2816
EPS = 1e-6
FORCE_SCORE = 1e6
NEG = -1e30
LOG2E = 1.4426950408889634

KCH = 256
GW = HPG * HEAD_DIM
KVW = N_KV * HEAD_DIM
VROWS = 80
NG_PAD = 128

F32 = jnp.float32
BF16 = jnp.bfloat16

VMEM_LIMIT = 56 * 1024 * 1024


def _params(sem):
    return pltpu.CompilerParams(dimension_semantics=sem, vmem_limit_bytes=VMEM_LIMIT)


def _dot(a, b):
    return jnp.dot(a, b, preferred_element_type=F32)


def _dot_nt(a, b):
    return lax.dot_general(a, b, (((1,), (1,)), ((), ())), preferred_element_type=F32)


def _split(x):
    hi = x.astype(BF16)
    lo = (x - hi.astype(F32)).astype(BF16)
    return hi, lo


def _iota(shape, dim):
    return lax.broadcasted_iota(jnp.int32, shape, dim)


def _const_spec(shape):
    return pl.BlockSpec(shape, lambda *_: (0,) * len(shape))


_O_RY = D_RNN
_O_Q = 2 * D_RNN
_O_KV = _O_Q + N_HEADS * HEAD_DIM
_O_NG = _O_KV + 6 * KVW
_O_MG = _O_NG + NG_PAD
_W_COLS = _O_MG + 2 * D_MODEL


def _inproj_kernel(x_ref, g_ref, w_ref, rx_ref, gy_ref, zq_ref, zkv_ref, ng_ref, mg_ref, h_ref):
    x = x_ref[...]
    y = x * lax.rsqrt(jnp.mean(x * x, axis=-1, keepdims=True) + EPS) * g_ref[...]
    h_ref[...] = y.astype(BF16)

    def proj(c0, width):
        return _dot(h_ref[...], w_ref[:, c0:c0 + width])

    cw = 256
    tm = x_ref.shape[0]
    for c in range(D_RNN // cw):
        rx = proj(c * cw, cw)
        gy = jax.nn.gelu(proj(_O_RY + c * cw, cw))
        for hf in range(cw // 128):
            rx_ref[c * (cw // 128) + hf] = rx[:, hf * 128:(hf + 1) * 128].reshape(tm // 8, 8, 128)
            gy_ref[c * (cw // 128) + hf] = gy[:, hf * 128:(hf + 1) * 128].reshape(tm // 8, 8, 128)
    for g in range(N_KV):
        zq_ref[g] = proj(_O_Q + g * GW, GW)
    for s in range(6):
        z = proj(_O_KV + s * KVW, KVW)
        for pr in range(KVW // 128):
            zkv_ref[s, pr] = z[:, pr * 128:(pr + 1) * 128]
    ng_ref[...] = jax.nn.sigmoid(proj(_O_NG, NG_PAD))
    for c in range(2 * D_MODEL // cw):
        mg_ref[:, c * cw:(c + 1) * cw] = jax.nn.sigmoid(proj(_O_MG + c * cw, cw)).astype(mg_ref.dtype)


def _inproj(x, gain, w, *, tm=256):
    B, S, D = x.shape
    T = B * S
    ni = S // tm
    nck = D_RNN // 128
    tok = lambda b, i: (b * ni + i, 0)
    slab_spec = pl.BlockSpec((nck, tm // 8, None, 8, 128), lambda b, i: (0, i, b, 0, 0))
    slab_shape = jax.ShapeDtypeStruct((nck, S // 8, B, 8, 128), F32)
    return pl.pallas_call(
        _inproj_kernel,
        grid=(B, ni),
        in_specs=[
            pl.BlockSpec((None, tm, D), lambda b, i: (b, i, 0)),
            _const_spec((1, D)),
            pl.BlockSpec((D, _W_COLS), lambda b, i: (0, 0), pipeline_mode=pl.Buffered(1)),
        ],
        out_specs=[
            slab_spec,
            slab_spec,
            pl.BlockSpec((N_KV, tm, GW), lambda b, i: (0, b * ni + i, 0)),
            pl.BlockSpec((6, KVW // 128, tm, 128), lambda b, i: (0, 0, b * ni + i, 0)),
            pl.BlockSpec((tm, NG_PAD), tok),
            pl.BlockSpec((tm, 2 * D_MODEL), tok),
        ],
        out_shape=[
            slab_shape,
            slab_shape,
            jax.ShapeDtypeStruct((N_KV, T, GW), F32),
            jax.ShapeDtypeStruct((6, KVW // 128, T, 128), F32),
            jax.ShapeDtypeStruct((T, NG_PAD), F32),
            jax.ShapeDtypeStruct((T, 2 * D_MODEL), BF16),
        ],
        scratch_shapes=[pltpu.VMEM((tm, D), BF16)],
        compiler_params=_params(("parallel", "parallel")),
        name="inproj",
    )(x, gain, w)


def _rglru_kernel(x_ref, gy_ref, cw_ref, cb_ref, wa_ref, ba_ref, wi_ref, bi_ref, lam_ref, wo_ref,
                  o_ref, xb_ref, gyb_ref, a_ref, u_ref, hb_ref, hs_ref, yb_ref, *, nb, tt):
    R = nb * tt
    halo = (CONV_W - 1) * nb
    step = pl.program_id(0)
    nck = D_RNN // 128

    @pl.when(step == 0)
    def _():
        xb_ref[0:halo, :] = jnp.zeros((halo, D_RNN), F32)
        hs_ref[...] = jnp.zeros((nb, D_RNN), F32)

    def slab_rows(t8, t_lo):
        return pl.ds(t8 * (8 * nb) + t_lo, nb, stride=8)

    def stage_in(t8, carry):
        for t_lo in range(8):
            r0 = pl.multiple_of((t8 * 8 + t_lo) * nb, nb)
            for ck in range(nck):
                cols = slice(ck * 128, (ck + 1) * 128)
                xb_ref[pl.ds(halo + r0, nb), cols] = x_ref[ck, slab_rows(t8, t_lo), :]
                gyb_ref[pl.ds(r0, nb), cols] = gy_ref[ck, slab_rows(t8, t_lo), :]
        return carry

    lax.fori_loop(0, tt // 8, stage_in, 0)
    xr = cb_ref[...] + cw_ref[0:1, :] * xb_ref[0:R, :]
    for k in range(1, CONV_W):
        xr = xr + cw_ref[k:k + 1, :] * xb_ref[k * nb:k * nb + R, :]
    tail = xb_ref[R:R + halo, :]
    xb_ref[0:halo, :] = tail

    xrb = xr.astype(BF16)
    decay_log2 = (-RG_C * LOG2E) * jax.nn.softplus(-lam_ref[...])
    row = _iota((R, RG_BW), 0)
    first = (row < nb) & (step == 0)
    for n in range(RG_BLOCKS):
        sl = slice(n * RG_BW, (n + 1) * RG_BW)
        xn = xrb[:, sl]
        r = jax.nn.sigmoid(_dot(xn, wa_ref[n]) + ba_ref[:, sl])
        ig = jax.nn.sigmoid(_dot(xn, wi_ref[n]) + bi_ref[:, sl])
        a = jnp.exp2(r * decay_log2[:, sl])
        mult = jnp.sqrt(1.0 - a * a)
        mult = jnp.where(first, 1.0, mult)
        a_ref[:, sl] = a
        u_ref[:, sl] = mult * ig * xr[:, sl]

    def body(t, h):
        r0 = pl.multiple_of(t * nb, nb)
        h = a_ref[pl.ds(r0, nb), :] * h + u_ref[pl.ds(r0, nb), :]
        hb_ref[pl.ds(r0, nb), :] = h
        return h

    h = lax.fori_loop(0, tt, body, hs_ref[...], unroll=8)
    hs_ref[...] = h
    yb_ref[...] = _dot((hb_ref[...] * gyb_ref[...]).astype(BF16), wo_ref[...])

    def stage_out(t8, carry):
        for t_lo in range(8):
            r0 = pl.multiple_of((t8 * 8 + t_lo) * nb, nb)
            for ck in range(nck):
                o_ref[ck, slab_rows(t8, t_lo), :] = yb_ref[pl.ds(r0, nb), ck * 128:(ck + 1) * 128]
        return carry

    lax.fori_loop(0, tt // 8, stage_out, 0)


def _rglru(rx, gy, conv_w, conv_b, wa, ba, wi, bi, lam, wo, *, nb, tt=64):
    nck, rows, _ = rx.shape
    S = rows // nb
    assert S % tt == 0 and tt % 8 == 0
    R = nb * tt
    halo = (CONV_W - 1) * nb
    return pl.pallas_call(
        functools.partial(_rglru_kernel, nb=nb, tt=tt),
        grid=(S // tt,),
        in_specs=[
            pl.BlockSpec((nck, R, 128), lambda s: (0, s, 0)),
            pl.BlockSpec((nck, R, 128), lambda s: (0, s, 0)),
            _const_spec((CONV_W, D_RNN)),
            _const_spec((1, D_RNN)),
            _const_spec((RG_BLOCKS, RG_BW, RG_BW)),
            _const_spec((1, D_RNN)),
            _const_spec((RG_BLOCKS, RG_BW, RG_BW)),
            _const_spec((1, D_RNN)),
            _const_spec((1, D_RNN)),
            _const_spec((D_RNN, D_MODEL)),
        ],
        out_specs=pl.BlockSpec((D_MODEL // 128, R, 128), lambda s: (0, s, 0)),
        out_shape=jax.ShapeDtypeStruct((D_MODEL // 128, rows, 128), F32),
        scratch_shapes=[
            pltpu.VMEM((R + halo, D_RNN), F32),
            pltpu.VMEM((R, D_RNN), F32),
            pltpu.VMEM((R, D_RNN), F32),
            pltpu.VMEM((R, D_RNN), F32),
            pltpu.VMEM((R, D_RNN), F32),
            pltpu.VMEM((nb, D_RNN), F32),
            pltpu.VMEM((R, D_MODEL), F32),
        ],
        compiler_params=_params(("arbitrary",)),
        name="rglru",
    )(rx, gy, conv_w, conv_b, wa, ba, wi, bi, lam, wo)


def _kvprep_kernel(zk_ref, zv_ref, g_ref, kn_ref, vt_ref):
    rows = zk_ref.shape[1]
    x = jnp.concatenate([zk_ref[0], zk_ref[1]], axis=1)
    seg = jnp.where(_iota((KVW, KVW), 0) // HEAD_DIM == _iota((KVW, KVW), 1) // HEAD_DIM, 1.0, 0.0).astype(BF16)
    hi, lo = _split(x * x)
    ssq = _dot(jnp.concatenate([hi, lo], axis=0), seg)
    kn = (x * lax.rsqrt((ssq[0:rows] + ssq[rows:2 * rows]) * (1.0 / HEAD_DIM) + EPS) * g_ref[...]).astype(BF16)
    for g in range(N_KV):
        kn_ref[g] = kn[:, g * HEAD_DIM:(g + 1) * HEAD_DIM]
    eye = jnp.where(_iota((KVW, KVW), 0) == _iota((KVW, KVW), 1), 1.0, 0.0).astype(BF16)
    pad = jnp.where(_iota((VROWS - HEAD_DIM, KCH), 0) == 0, 1.0, 0.0).astype(BF16)
    for c in range(rows // KCH):
        sl = slice(c * KCH, (c + 1) * KCH)
        v = jnp.concatenate([zv_ref[0, sl, :], zv_ref[1, sl, :]], axis=1).astype(BF16)
        vt = _dot_nt(eye, v).astype(BF16)
        for g in range(N_KV):
            vt_ref[c, g, 0:HEAD_DIM, :] = vt[g * HEAD_DIM:(g + 1) * HEAD_DIM]
            vt_ref[c, g, HEAD_DIM:VROWS, :] = pad


def _kvprep(zkv, gains, *, B, S, cpb=4):
    nch = S // KCH
    assert nch % cpb == 0
    nblk = nch // cpb
    rows = cpb * KCH
    return pl.pallas_call(
        _kvprep_kernel,
        grid=(2, B, nblk),
        in_specs=[
            pl.BlockSpec((None, KVW // 128, rows, 128), lambda w, b, i: (2 + 2 * w, 0, b * nblk + i, 0)),
            pl.BlockSpec((None, KVW // 128, rows, 128), lambda w, b, i: (3 + 2 * w, 0, b * nblk + i, 0)),
            pl.BlockSpec((None, 1, KVW), lambda w, b, i: (w, 0, 0)),
        ],
        out_specs=[
            pl.BlockSpec((None, None, N_KV, rows, HEAD_DIM), lambda w, b, i: (w, b, 0, i, 0)),
            pl.BlockSpec((None, None, cpb, N_KV, VROWS, KCH), lambda w, b, i: (w, b, i, 0, 0, 0)),
        ],
        out_shape=[
            jax.ShapeDtypeStruct((2, B, N_KV, S, HEAD_DIM), BF16),
            jax.ShapeDtypeStruct((2, B, nch, N_KV, VROWS, KCH), BF16),
        ],
        compiler_params=_params(("parallel", "parallel", "parallel")),
        name="kvprep",
    )(zkv, zkv, gains)


def _compress_kernel(zk_ref, zv_ref, pos_ref, w1_ref, w2_ref, g_ref, kc_ref, vct_ref):
    nch = zk_ref.shape[0] // CMP_STRIDE
    lane = _iota((nch, 128), 1)
    npair = CMP_STRIDE // 2
    eye = jnp.where(_iota((HEAD_DIM, HEAD_DIM), 0) == _iota((HEAD_DIM, HEAD_DIM), 1), 1.0, 0.0).astype(BF16)
    for kind, z_ref in enumerate((zk_ref, zv_ref)):
        for ge in range(2):
            combs = []
            for p in range(npair):
                a = z_ref[pl.ds(2 * p, nch, stride=CMP_STRIDE), :]
                b = z_ref[pl.ds(2 * p + 1, nch, stride=CMP_STRIDE), :]
                if ge == 0:
                    combs.append(jnp.where(lane < HEAD_DIM, a, pltpu.roll(b, HEAD_DIM, axis=1)))
                else:
                    combs.append(jnp.where(lane < HEAD_DIM, pltpu.roll(a, HEAD_DIM, axis=1), b))
            halves = []
            half_w = npair * 128
            for half in range(2):
                lhs = jnp.concatenate(
                    [(combs[p] + pos_ref[kind, half * npair + p:half * npair + p + 1, :]).astype(BF16)
                     for p in range(npair)], axis=1)
                halves.append(_dot(lhs, w1_ref[kind, half * half_w:(half + 1) * half_w, :]))
            hid = halves[0] + pltpu.roll(halves[1], nch - 1, axis=0)
            hid = jax.nn.gelu(hid)
            out = _dot(hid.astype(BF16), w2_ref[kind])
            if kind == 0:
                out = out * lax.rsqrt(jnp.mean(out * out, axis=-1, keepdims=True) + EPS) * g_ref[...]
            res = jnp.where(_iota(out.shape, 0) < nch - 1, out, 0.0).astype(BF16)
            if kind == 0:
                kc_ref[ge] = res
            else:
                vct_ref[ge] = _dot_nt(eye, res).astype(BF16)


def _compress(zkv, pos, w1, w2, gain, *, B, S):
    nch = S // CMP_STRIDE
    return pl.pallas_call(
        _compress_kernel,
        grid=(B, KVW // 128),
        in_specs=[
            pl.BlockSpec((None, None, S, 128), lambda b, lp: (0, lp, b, 0)),
            pl.BlockSpec((None, None, S, 128), lambda b, lp: (1, lp, b, 0)),
            _const_spec((2, CMP_STRIDE, 128)),
            _const_spec((2, CMP_LEN * HEAD_DIM, CMP_HID)),
            _const_spec((2, CMP_HID, HEAD_DIM)),
            _const_spec((1, HEAD_DIM)),
        ],
        out_specs=[
            pl.BlockSpec((None, 2, nch, HEAD_DIM), lambda b, lp: (b, lp, 0, 0)),
            pl.BlockSpec((None, 2, HEAD_DIM, nch), lambda b, lp: (b, lp, 0, 0)),
        ],
        out_shape=[
            jax.ShapeDtypeStruct((B, N_KV, nch, HEAD_DIM), BF16),
            jax.ShapeDtypeStruct((B, N_KV, HEAD_DIM, nch), BF16),
        ],
        compiler_params=_params(("parallel", "parallel")),
        name="compress",
    )(zkv, zkv, pos, w1, w2, gain)


def _spread_heads(x):
    y = x + pltpu.roll(x, 2 * Q_BLK, axis=1)
    return y + pltpu.roll(y, Q_BLK, axis=1)


def _swap_heads(x):
    n = x.shape[0]
    lane = _iota((n, 128), 1)
    halves = []
    for pr in range(x.shape[1] // 128):
        slab = jnp.concatenate([x[:, pr * 128:(pr + 1) * 128], jnp.zeros((128 - n, 128), x.dtype)], axis=0)
        tr = slab.T
        halves.append(jnp.where(lane < n, tr[0:n], pltpu.roll(tr[n:2 * n], n, axis=1)))
    return jnp.concatenate(halves, axis=1)


def _nsa_kernel(*refs, nsel, qpb):
    per_chunk = KCH // SEL_LEN
    for k in range(qpb):
        _nsa_block(k, "head", None, *refs, nsel=nsel, qpb=qpb)

    def step(i, carry):
        for k in range(qpb):
            _nsa_block(k, "step", i, *refs, nsel=nsel, qpb=qpb)
        return carry

    common = pl.program_id(1) * (qpb // per_chunk)
    lax.fori_loop(0, common, step, 0)
    for k in range(qpb):
        for extra in range(k // per_chunk):
            _nsa_block(k, "step", common + extra, *refs, nsel=nsel, qpb=qpb)
    for k in range(qpb):
        _nsa_block(k, "tail", None, *refs, nsel=nsel, qpb=qpb)


def _nsa_block(k, phase, step_i, zq_ref, qg_ref, kc_ref, vct_ref, ks_ref, kw_ref, vst_ref, vwt_ref, ng_ref, o_ref,
               qtp_ref, oc_ref, ow_ref, bias_ref, ms_ref, accs_ref, mw_ref, accw_ref, sbuf_ref, wbuf_ref,
               gates_ref, scs_ref, wbias_ref, *, nsel, qpb):
    qtp_ref, oc_ref, ow_ref, bias_ref, ms_ref, accs_ref, sbuf_ref, gates_ref = (
        r.at[k] for r in (qtp_ref, oc_ref, ow_ref, bias_ref, ms_ref, accs_ref, sbuf_ref, gates_ref))
    qi = pl.program_id(1) * qpb + k
    q_rows = pl.ds(k * Q_BLK, Q_BLK)
    s0 = qi * Q_BLK
    cur = qi
    ncmp = kc_ref.shape[1]
    per_chunk = KCH // SEL_LEN
    groups = range(N_KV)
    n_chunks = qi // per_chunk + 1
    cur_slab = k % per_chunk
    cut_rows = slice(cur_slab * SEL_LEN, (cur_slab + 1) * SEL_LEN)
    last_slabs = per_chunk // 2 if cur_slab < per_chunk // 2 else per_chunk

    def scores(k_ref, i):
        k0 = pl.multiple_of(i * KCH, KCH)
        return [_dot(k_ref[g, pl.ds(k0, KCH), :], qtp_ref[g]) for g in groups]

    def park(buf_ref, vals):
        for g in groups:
            buf_ref[g] = vals[g]

    def cut(buf_ref, tri):
        for g in groups:
            buf_ref[g, cut_rows, :] = buf_ref[g, cut_rows, :] + tri

    def update(g, blocks, shifts, vt, m_ref, acc_ref):
        m = m_ref[g]
        part = None
        for blk, sh in zip(blocks, shifts):
            top8 = jnp.max(blk.reshape(SEL_LEN // 8, 8, GW), axis=0) + sh
            part = top8 if part is None else jnp.maximum(part, top8)
        mnew = jnp.maximum(m, jnp.max(part, axis=0, keepdims=True))
        e = jnp.concatenate([jnp.exp2(blk + (sh - mnew)) for blk, sh in zip(blocks, shifts)], axis=0).astype(BF16)
        acc_ref[g] = jnp.exp2(m - mnew) * acc_ref[g] + _dot(vt, e)
        m_ref[g] = mnew

    def slabs(buf_ref, g, ns):
        return [buf_ref[g, nb * SEL_LEN:(nb + 1) * SEL_LEN, :] for nb in range(ns)]

    def sel_softmax(i, ns=per_chunk):
        for g in groups:
            shifts = [bias_ref[g, pl.ds(i * per_chunk + nb, 1), :] for nb in range(ns)]
            update(g, slabs(sbuf_ref, g, ns), shifts, vst_ref[i, g, :, 0:ns * SEL_LEN], ms_ref, accs_ref)

    key_row = _iota((SEL_LEN, GW), 0)
    in_blk_t = _iota((SEL_LEN, GW), 1) % Q_BLK
    causal_tri = jnp.where(key_row <= in_blk_t, 0.0, NEG)

    if phase == "step":
        nxt = scores(ks_ref, step_i + 1)
        sel_softmax(step_i)
        park(sbuf_ref, nxt)
        return

    if phase == "tail":
        cut(sbuf_ref, causal_tri)
        sel_softmax(n_chunks - 1, last_slabs)
        outs = []
        for g in groups:
            o_s = accs_ref[g, 0:HEAD_DIM, :] * (1.0 / jnp.maximum(accs_ref[g, HEAD_DIM:HEAD_DIM + 1, :], 1e-30))
            gate = [gates_ref[g, c:c + 1, :] for c in range(3)]
            o_t = gate[0] * oc_ref[g] + gate[1] * o_s + gate[2] * ow_ref[g]
            outs.append(_swap_heads(o_t))
        for g in groups:
            o_ref[g, q_rows, :] = outs[g].astype(o_ref.dtype)
        return

    lane_t = s0 + (_iota((1, GW), 1) % Q_BLK)
    lane_grp = _iota((1, GW), 1) // Q_BLK
    ov_n = _iota((nsel, ncmp), 0) * SEL_LEN
    ov_c = _iota((nsel, ncmp), 1) * CMP_STRIDE
    overlap = jnp.where((ov_c < ov_n + SEL_LEN) & (ov_c + CMP_LEN > ov_n), 1.0, 0.0).astype(BF16)


    ok_c = _iota((ncmp, GW), 0) * CMP_STRIDE + (CMP_LEN - 1) <= lane_t
    ngt = jnp.concatenate([ng_ref[q_rows, :], jnp.zeros((128 - Q_BLK, NG_PAD), F32)], axis=0).T
    ngt_hi = pltpu.roll(ngt, Q_BLK, axis=1)
    low_half = _iota((1, 128), 1) < Q_BLK
    for g in groups:
        for c in range(3):
            r = c * N_HEADS + g * HPG
            pairs = [jnp.where(low_half, ngt[r + j:r + j + 1], ngt_hi[r + j + 1:r + j + 2]) for j in (0, 2)]
            gates_ref[g, c:c + 1, :] = jnp.concatenate(pairs, axis=1)

    qts = []
    for g in groups:
        xt = _swap_heads(zq_ref[g, q_rows, :])
        ssq = jnp.sum(xt * xt, axis=0, keepdims=True)
        qts.append((xt * lax.rsqrt(ssq * (1.0 / HEAD_DIM) + EPS) * qg_ref[...]).astype(BF16))
    scs = []
    for g in groups:
        qtp_ref[g] = qts[g]
        scs.append(_dot(kc_ref[g], qts[g]))

    n_win = WINDOW // KCH + 1
    win_ids = [jnp.maximum(n_chunks - (n_win - j), 0) for j in range(n_win)]
    win_dead = [jnp.where(n_chunks >= n_win - j, 0.0, NEG) for j in range(n_win)]
    park(scs_ref, scs)
    park(sbuf_ref, scores(ks_ref, 0))
    park(wbuf_ref, scores(kw_ref, win_ids[0]))

    expired_tri = jnp.where(key_row > in_blk_t, 0.0, NEG)
    win_blocks = WINDOW // SEL_LEN
    n_all = _iota((nsel, GW), 0)
    wbias_ref[...] = jnp.where((n_all >= cur - win_blocks) & (n_all <= cur), 0.0, NEG)
    cut(wbuf_ref, jnp.where(cur >= win_blocks, expired_tri, 0.0))

    ms_ref[...] = jnp.full(ms_ref.shape, 0.5 * NEG, F32)
    mw_ref[...] = jnp.full(mw_ref.shape, 0.5 * NEG, F32)
    accs_ref[...] = jnp.zeros(accs_ref.shape, F32)
    accw_ref[...] = jnp.zeros(accw_ref.shape, F32)

    def win_softmax(i, dead, lo=0, hi=per_chunk):
        shifts = [wbias_ref[pl.ds(i * per_chunk + nb, 1), :] + dead for nb in range(lo, hi)]
        for g in groups:
            blocks = [wbuf_ref[g, nb * SEL_LEN:(nb + 1) * SEL_LEN, :] for nb in range(lo, hi)]
            update(g, blocks, shifts, vwt_ref[i, g, :, lo * SEL_LEN:hi * SEL_LEN], mw_ref, accw_ref)

    first_lo = per_chunk // 2 if cur_slab >= per_chunk // 2 else 0
    for j in range(n_win - 1):
        nxt = scores(kw_ref, win_ids[j + 1])
        win_softmax(win_ids[j], win_dead[j], lo=first_lo if j == 0 else 0)
        park(wbuf_ref, nxt)

    imp_all = jnp.zeros((nsel, GW), F32)
    some_c = lane_t >= CMP_LEN - 1
    ones_rows = jnp.ones((16, ncmp), BF16)
    for g in groups:
        scm = jnp.where(ok_c, scs_ref[g], NEG)
        ec = jnp.exp2(scm - jnp.max(scm, axis=0, keepdims=True))
        both = _dot(jnp.concatenate([vct_ref[g], overlap, ones_rows], axis=0), ec.astype(BF16))
        denom = both[HEAD_DIM + nsel:HEAD_DIM + nsel + 1]
        both = both[0:HEAD_DIM + nsel] * jnp.where(some_c, 1.0 / jnp.maximum(denom, 1e-30), 0.0)
        oc_ref[g] = both[0:HEAD_DIM]
        imp = _spread_heads(both[HEAD_DIM:HEAD_DIM + nsel])
        imp_all = jnp.where(lane_grp == g, imp, imp_all)

    n_io = _iota((nsel, GW), 0)
    forced = (n_io == 0) | (n_io == cur) | (n_io == cur - 1)
    valid = n_io <= cur
    score = jnp.where(forced, FORCE_SCORE, imp_all)
    score = jnp.where(valid, score, -1.0)
    rows8 = [score[r:r + 8] for r in range(0, nsel, 8)]
    ranks = [jnp.zeros((8, GW), F32) for _ in rows8]
    sub = _iota((8, GW), 0)
    for m in range(nsel):
        rowv = score[m:m + 1, :]
        for k, blk in enumerate(rows8):
            ge = jnp.where(rowv >= blk, 1.0, 0.0)
            gt = jnp.where(rowv > blk, 1.0, 0.0)
            if 8 * k > m:
                inc = ge
            elif 8 * k + 7 < m:
                inc = gt
            else:
                inc = jnp.where(sub + 8 * k > m, ge, gt)
            ranks[k] = ranks[k] + inc
    rank = jnp.concatenate(ranks, axis=0)
    sel_all = jnp.where((rank < float(min(SEL_TOPN, nsel))) & valid, 1.0, 0.0)
    for g in groups:
        mine = jnp.where(lane_grp == g, sel_all, 0.0)
        bias_ref[g] = (_spread_heads(mine) - 1.0) * (-NEG)

    cut(wbuf_ref, causal_tri)
    win_softmax(win_ids[n_win - 1], win_dead[n_win - 1], hi=last_slabs)
    for g in groups:
        ow_ref[g] = accw_ref[g, 0:HEAD_DIM, :] * (1.0 / jnp.maximum(accw_ref[g, HEAD_DIM:HEAD_DIM + 1, :], 1e-30))


def _nsa(zq, qgain, kc, vct, kn, vt, ng, *, B, S):
    T = B * S
    qpb = 2 * (KCH // SEL_LEN)
    assert S % (Q_BLK * qpb) == 0
    nq = S // (Q_BLK * qpb)
    nch = S // KCH
    ncmp = S // CMP_STRIDE
    nsel = S // SEL_LEN
    tok = lambda b, i: (0, b * nq + i, 0)
    once = dict(pipeline_mode=pl.Buffered(1))
    return pl.pallas_call(
        functools.partial(_nsa_kernel, nsel=nsel, qpb=qpb),
        grid=(B, nq),
        in_specs=[
            pl.BlockSpec((N_KV, Q_BLK * qpb, GW), tok),
            _const_spec((HEAD_DIM, GW)),
            pl.BlockSpec((None, N_KV, ncmp, HEAD_DIM), lambda b, i: (b, 0, 0, 0)),
            pl.BlockSpec((None, N_KV, HEAD_DIM, ncmp), lambda b, i: (b, 0, 0, 0)),
            pl.BlockSpec((None, None, N_KV, S, HEAD_DIM), lambda b, i: (0, b, 0, 0, 0), **once),
            pl.BlockSpec((None, None, N_KV, S, HEAD_DIM), lambda b, i: (1, b, 0, 0, 0), **once),
            pl.BlockSpec((None, None, nch, N_KV, VROWS, KCH), lambda b, i: (0, b, 0, 0, 0, 0), **once),
            pl.BlockSpec((None, None, nch, N_KV, VROWS, KCH), lambda b, i: (1, b, 0, 0, 0, 0), **once),
            pl.BlockSpec((Q_BLK * qpb, NG_PAD), lambda b, i: (b * nq + i, 0)),
        ],
        out_specs=pl.BlockSpec((N_KV, Q_BLK * qpb, GW), tok),
        out_shape=jax.ShapeDtypeStruct((N_KV, T, GW), BF16),
        scratch_shapes=[
            pltpu.VMEM((qpb, N_KV, HEAD_DIM, GW), BF16),
            pltpu.VMEM((qpb, N_KV, HEAD_DIM, GW), F32),
            pltpu.VMEM((qpb, N_KV, HEAD_DIM, GW), F32),
            pltpu.VMEM((qpb, N_KV, nsel, GW), F32),
            pltpu.VMEM((qpb, N_KV, 1, GW), F32),
            pltpu.VMEM((qpb, N_KV, VROWS, GW), F32),
            pltpu.VMEM((N_KV, 1, GW), F32),
            pltpu.VMEM((N_KV, VROWS, GW), F32),
            pltpu.VMEM((qpb, N_KV, KCH, GW), F32),
            pltpu.VMEM((N_KV, KCH, GW), F32),
            pltpu.VMEM((qpb, N_KV, 8, GW), F32),
            pltpu.VMEM((N_KV, ncmp, GW), F32),
            pltpu.VMEM((nsel, GW), F32),
        ],
        compiler_params=_params(("parallel", "arbitrary")),
        name="nsa",
    )(zq, qgain, kc, vct, kn, kn, vt, vt, ng)


def _merge_ffn_kernel(x_ref, ya_ref, o_ref_in, mg_ref, wn_ref, wo_ref, g2_ref, wg_ref, wu_ref, wd_ref,
                      out_ref, h_ref, *, tf):
    yb = _dot(o_ref_in[0], wn_ref[0:GW, :])
    for g in range(1, N_KV):
        yb = yb + _dot(o_ref_in[g], wn_ref[g * GW:(g + 1) * GW, :])
    tm = x_ref.shape[0]
    ya = jnp.concatenate([ya_ref[ck].reshape(tm, 128) for ck in range(D_MODEL // 128)], axis=1)
    mixed = (mg_ref[:, 0:D_MODEL].astype(F32) * ya
             + mg_ref[:, D_MODEL:2 * D_MODEL].astype(F32) * yb)
    x1 = x_ref[...] + _dot(mixed.astype(BF16), wo_ref[...])
    y = x1 * lax.rsqrt(jnp.mean(x1 * x1, axis=-1, keepdims=True) + EPS) * g2_ref[...]
    h_ref[...] = y.astype(BF16)
    out_ref[...] = x1
    for j in range(D_FF // tf):
        h = h_ref[...]
        sl = slice(j * tf, (j + 1) * tf)
        act = jax.nn.silu(_dot(h, wg_ref[:, sl])) * _dot(h, wu_ref[:, sl])
        out_ref[...] += _dot(act.astype(BF16), wd_ref[sl, :])


def _merge_ffn(x2d, ya, o, mg, wn, wo, gain2, wg, wu, wd, *, B, S, tm=512, tf=256):
    assert S % tm == 0 and D_FF % tf == 0
    ni = S // tm
    tok = lambda b, i: (b * ni + i, 0)
    once = lambda shape: pl.BlockSpec(shape, lambda b, i: (0,) * len(shape), pipeline_mode=pl.Buffered(1))
    return pl.pallas_call(
        functools.partial(_merge_ffn_kernel, tf=tf),
        grid=(B, ni),
        in_specs=[
            pl.BlockSpec((tm, D_MODEL), tok),
            pl.BlockSpec((D_MODEL // 128, tm // 8, None, 8, 128), lambda b, i: (0, i, b, 0, 0)),
            pl.BlockSpec((N_KV, tm, GW), lambda b, i: (0, b * ni + i, 0)),
            pl.BlockSpec((tm, 2 * D_MODEL), tok),
            once((N_HEADS * HEAD_DIM, D_MODEL)),
            once((D_MODEL, D_MODEL)),
            _const_spec((1, D_MODEL)),
            once((D_MODEL, D_FF)),
            once((D_MODEL, D_FF)),
            once((D_FF, D_MODEL)),
        ],
        out_specs=pl.BlockSpec((tm, D_MODEL), tok),
        out_shape=jax.ShapeDtypeStruct((B * S, D_MODEL), F32),
        scratch_shapes=[pltpu.VMEM((tm, D_MODEL), BF16)],
        compiler_params=_params(("parallel", "parallel")),
        name="merge_ffn",
    )(x2d, ya, o, mg, wn, wo, gain2, wg, wu, wd)


def _layer(x, p):
    B, S, _ = x.shape
    T = B * S
    dh = HEAD_DIM
    row = lambda v: v.reshape(1, -1)
    w_in = p["w_in"]
    n_ng = 3 * N_HEADS
    o_ng = 2 * D_RNN + N_HEADS * dh + 6 * KVW
    w_packed = jnp.concatenate(
        [w_in[:, :o_ng], jnp.pad(w_in[:, o_ng:o_ng + n_ng], ((0, 0), (0, NG_PAD - n_ng))), w_in[:, o_ng + n_ng:]],
        axis=1).astype(BF16)
    rx, gy, zq, zkv, ng, mg = _inproj(x, row(p["norm1"]), w_packed)

    slabs = lambda a: a.reshape(a.shape[0], S * B, 128)
    ya = _rglru(slabs(rx), slabs(gy), p["conv_w"], row(p["conv_b"]),
                p["rg_wa"].astype(BF16), row(p["rg_ba"]), p["rg_wi"].astype(BF16), row(p["rg_bi"]),
                row(p["rg_lambda"]), p["w_rg_out"].astype(BF16), nb=B)
    ya = ya.reshape(D_MODEL // 128, S // 8, B, 8, 128)

    kgains = jnp.tile(p["k_norm"][1:3], (1, N_KV)).reshape(2, 1, KVW)
    kn, vt = _kvprep(zkv, kgains, B=B, S=S)
    pos = jnp.stack([p["cmp_pos_k"], p["cmp_pos_v"]]).reshape(2, CMP_STRIDE, 2 * dh)
    w1 = jnp.stack([p["cmp_k_w1"], p["cmp_v_w1"]]).astype(BF16)
    w2 = jnp.stack([p["cmp_k_w2"], p["cmp_v_w2"]]).astype(BF16)
    kc, vct = _compress(zkv, pos, w1, w2, row(p["k_norm"][0]), B=B, S=S)
    qgain = jnp.broadcast_to((p["q_norm"] * (dh ** -0.5 * LOG2E))[:, None], (dh, GW))
    o = _nsa(zq, qgain, kc, vct, kn, vt, ng, B=B, S=S)

    x2 = _merge_ffn(x.reshape(T, D_MODEL), ya, o, mg, p["w_nsa_out"].astype(BF16), p["w_o"].astype(BF16),
                    row(p["norm2"]), p["w_gate"].astype(BF16), p["w_up"].astype(BF16), p["w_down"].astype(BF16),
                    B=B, S=S)
    return x2.reshape(B, S, D_MODEL)


def kernel(x, norm1, w_in, conv_w, conv_b, rg_wa, rg_ba, rg_wi, rg_bi, rg_lambda, q_norm, k_norm, cmp_pos_k,
           cmp_pos_v, cmp_k_w1, cmp_k_w2, cmp_v_w1, cmp_v_w2, w_rg_out, w_nsa_out, w_o, norm2, w_gate, w_up, w_down):
    params = dict(norm1=norm1, w_in=w_in, conv_w=conv_w, conv_b=conv_b, rg_wa=rg_wa, rg_ba=rg_ba, rg_wi=rg_wi,
                  rg_bi=rg_bi, rg_lambda=rg_lambda, q_norm=q_norm, k_norm=k_norm, cmp_pos_k=cmp_pos_k,
                  cmp_pos_v=cmp_pos_v, cmp_k_w1=cmp_k_w1, cmp_k_w2=cmp_k_w2, cmp_v_w1=cmp_v_w1, cmp_v_w2=cmp_v_w2,
                  w_rg_out=w_rg_out, w_nsa_out=w_nsa_out, w_o=w_o, norm2=norm2, w_gate=w_gate, w_up=w_up,
                  w_down=w_down)
    for l in range(norm1.shape[0]):
        x = _layer(x, {k: v[l] for k, v in params.items()})
    return x
```

```python
import functools

import jax
import jax.numpy as jnp
from jax import lax
from jax.experimental import pallas as pl
from jax.experimental.pallas import tpu as pltpu

D_MODEL = 1024
D_RNN = 1024
RG_BLOCKS = 4
RG_BW = D_RNN // RG_BLOCKS
CONV_W = 4
RG_C = 8.0
N_HEADS = 16
N_KV = 4
HEAD_DIM = 64
HPG = N_HEADS // N_KV
CMP_LEN = 32
CMP_STRIDE = 16
CMP_HID = 256
SEL_LEN = 64
SEL_TOPN = 16
WINDOW = 512
Q_BLK = 64
D_FF = 2816
EPS = 1e-6
FORCE_SCORE = 1e6
NEG = -1e30
LOG2E = 1.4426950408889634

LANES = 128
KCH = 256
GW = HPG * HEAD_DIM
KVW = N_KV * HEAD_DIM
VROWS = 80
NG_PAD = LANES

F32 = jnp.float32
BF16 = jnp.bfloat16

VMEM_LIMIT = 56 * 1024 * 1024


def _params(sem):
    return pltpu.CompilerParams(dimension_semantics=sem, vmem_limit_bytes=VMEM_LIMIT)


def _dot(a, b):
    return jnp.dot(a, b, preferred_element_type=F32)


def _dot_nt(a, b):
    return lax.dot_general(a, b, (((1,), (1,)), ((), ())), preferred_element_type=F32)


def _split(x):
    hi = x.astype(BF16)
    lo = (x - hi.astype(F32)).astype(BF16)
    return hi, lo


def _iota(shape, dim):
    return lax.broadcasted_iota(jnp.int32, shape, dim)


def _const_spec(shape):
    return pl.BlockSpec(shape, lambda *_: (0,) * len(shape))


_O_RY = D_RNN
_O_Q = 2 * D_RNN
_O_KV = _O_Q + N_HEADS * HEAD_DIM
_O_NG = _O_KV + 6 * KVW
_O_MG = _O_NG + NG_PAD
_W_COLS = _O_MG + 2 * D_MODEL


def _inproj_kernel(x_ref, g_ref, w_ref, rx_ref, gy_ref, zq_ref, zkv_ref, ng_ref, mg_ref, h_ref):
    x = x_ref[...]
    y = x * lax.rsqrt(jnp.mean(x * x, axis=-1, keepdims=True) + EPS) * g_ref[...]
    h_ref[...] = y.astype(BF16)

    def proj(c0, width):
        return _dot(h_ref[...], w_ref[:, c0:c0 + width])

    cw = 256
    tm = x_ref.shape[0]
    for c in range(D_RNN // cw):
        rx = proj(c * cw, cw)
        gy = jax.nn.gelu(proj(_O_RY + c * cw, cw))
        for hf in range(cw // LANES):
            rx_ref[c * (cw // LANES) + hf] = rx[:, hf * LANES:(hf + 1) * LANES].reshape(tm // 8, 8, LANES)
            gy_ref[c * (cw // LANES) + hf] = gy[:, hf * LANES:(hf + 1) * LANES].reshape(tm // 8, 8, LANES)
    for g in range(N_KV):
        zq_ref[g] = proj(_O_Q + g * GW, GW)
    for s in range(6):
        z = proj(_O_KV + s * KVW, KVW)
        for pr in range(KVW // LANES):
            zkv_ref[s, pr] = z[:, pr * LANES:(pr + 1) * LANES]
    ng_ref[...] = jax.nn.sigmoid(proj(_O_NG, NG_PAD))
    for c in range(2 * D_MODEL // cw):
        mg_ref[:, c * cw:(c + 1) * cw] = jax.nn.sigmoid(proj(_O_MG + c * cw, cw)).astype(mg_ref.dtype)


def _inproj(x, gain, w, *, tm=256):
    B, S, D = x.shape
    T = B * S
    ni = S // tm
    nck = D_RNN // LANES
    tok = lambda b, i: (b * ni + i, 0)
    slab_spec = pl.BlockSpec((nck, tm // 8, None, 8, LANES), lambda b, i: (0, i, b, 0, 0))
    slab_shape = jax.ShapeDtypeStruct((nck, S // 8, B, 8, LANES), F32)
    return pl.pallas_call(
        _inproj_kernel,
        grid=(B, ni),
        in_specs=[
            pl.BlockSpec((None, tm, D), lambda b, i: (b, i, 0)),
            _const_spec((1, D)),
            pl.BlockSpec((D, _W_COLS), lambda b, i: (0, 0), pipeline_mode=pl.Buffered(1)),
        ],
        out_specs=[
            slab_spec,
            slab_spec,
            pl.BlockSpec((N_KV, tm, GW), lambda b, i: (0, b * ni + i, 0)),
            pl.BlockSpec((6, KVW // LANES, tm, LANES), lambda b, i: (0, 0, b * ni + i, 0)),
            pl.BlockSpec((tm, NG_PAD), tok),
            pl.BlockSpec((tm, 2 * D_MODEL), tok),
        ],
        out_shape=[
            slab_shape,
            slab_shape,
            jax.ShapeDtypeStruct((N_KV, T, GW), F32),
            jax.ShapeDtypeStruct((6, KVW // LANES, T, LANES), F32),
            jax.ShapeDtypeStruct((T, NG_PAD), F32),
            jax.ShapeDtypeStruct((T, 2 * D_MODEL), BF16),
        ],
        scratch_shapes=[pltpu.VMEM((tm, D), BF16)],
        compiler_params=_params(("parallel", "parallel")),
        name="inproj",
    )(x, gain, w)


def _rglru_kernel(x_ref, gy_ref, cw_ref, cb_ref, wa_ref, ba_ref, wi_ref, bi_ref, lam_ref, wo_ref,
                  o_ref, xb_ref, gyb_ref, a_ref, u_ref, hb_ref, hs_ref, yb_ref, *, nb, tt):
    R = nb * tt
    halo = (CONV_W - 1) * nb
    step = pl.program_id(0)
    nck = D_RNN // LANES

    @pl.when(step == 0)
    def _():
        xb_ref[0:halo, :] = jnp.zeros((halo, D_RNN), F32)
        hs_ref[...] = jnp.zeros((nb, D_RNN), F32)

    def slab_rows(t8, t_lo):
        return pl.ds(t8 * (8 * nb) + t_lo, nb, stride=8)

    def stage_in(t8, carry):
        for t_lo in range(8):
            r0 = pl.multiple_of((t8 * 8 + t_lo) * nb, nb)
            for ck in range(nck):
                cols = slice(ck * LANES, (ck + 1) * LANES)
                xb_ref[pl.ds(halo + r0, nb), cols] = x_ref[ck, slab_rows(t8, t_lo), :]
                gyb_ref[pl.ds(r0, nb), cols] = gy_ref[ck, slab_rows(t8, t_lo), :]
        return carry

    lax.fori_loop(0, tt // 8, stage_in, 0)
    xr = cb_ref[...] + cw_ref[0:1, :] * xb_ref[0:R, :]
    for k in range(1, CONV_W):
        xr = xr + cw_ref[k:k + 1, :] * xb_ref[k * nb:k * nb + R, :]
    tail = xb_ref[R:R + halo, :]
    xb_ref[0:halo, :] = tail

    xrb = xr.astype(BF16)
    decay_log2 = (-RG_C * LOG2E) * jax.nn.softplus(-lam_ref[...])
    row = _iota((R, RG_BW), 0)
    first = (row < nb) & (step == 0)
    for n in range(RG_BLOCKS):
        sl = slice(n * RG_BW, (n + 1) * RG_BW)
        xn = xrb[:, sl]
        r = jax.nn.sigmoid(_dot(xn, wa_ref[n]) + ba_ref[:, sl])
        ig = jax.nn.sigmoid(_dot(xn, wi_ref[n]) + bi_ref[:, sl])
        a = jnp.exp2(r * decay_log2[:, sl])
        mult = jnp.sqrt(1.0 - a * a)
        mult = jnp.where(first, 1.0, mult)
        a_ref[:, sl] = a
        u_ref[:, sl] = mult * ig * xr[:, sl]

    def body(t, h):
        r0 = pl.multiple_of(t * nb, nb)
        h = a_ref[pl.ds(r0, nb), :] * h + u_ref[pl.ds(r0, nb), :]
        hb_ref[pl.ds(r0, nb), :] = h
        return h

    h = lax.fori_loop(0, tt, body, hs_ref[...], unroll=8)
    hs_ref[...] = h
    yb_ref[...] = _dot((hb_ref[...] * gyb_ref[...]).astype(BF16), wo_ref[...])

    def stage_out(t8, carry):
        for t_lo in range(8):
            r0 = pl.multiple_of((t8 * 8 + t_lo) * nb, nb)
            for ck in range(nck):
                o_ref[ck, slab_rows(t8, t_lo), :] = yb_ref[pl.ds(r0, nb), ck * LANES:(ck + 1) * LANES]
        return carry

    lax.fori_loop(0, tt // 8, stage_out, 0)


def _rglru(rx, gy, conv_w, conv_b, wa, ba, wi, bi, lam, wo, *, nb, tt=64):
    nck, rows, _ = rx.shape
    S = rows // nb
    assert S % tt == 0 and tt % 8 == 0
    R = nb * tt
    halo = (CONV_W - 1) * nb
    return pl.pallas_call(
        functools.partial(_rglru_kernel, nb=nb, tt=tt),
        grid=(S // tt,),
        in_specs=[
            pl.BlockSpec((nck, R, LANES), lambda s: (0, s, 0)),
            pl.BlockSpec((nck, R, LANES), lambda s: (0, s, 0)),
            _const_spec((CONV_W, D_RNN)),
            _const_spec((1, D_RNN)),
            _const_spec((RG_BLOCKS, RG_BW, RG_BW)),
            _const_spec((1, D_RNN)),
            _const_spec((RG_BLOCKS, RG_BW, RG_BW)),
            _const_spec((1, D_RNN)),
            _const_spec((1, D_RNN)),
            _const_spec((D_RNN, D_MODEL)),
        ],
        out_specs=pl.BlockSpec((D_MODEL // LANES, R, LANES), lambda s: (0, s, 0)),
        out_shape=jax.ShapeDtypeStruct((D_MODEL // LANES, rows, LANES), F32),
        scratch_shapes=[
            pltpu.VMEM((R + halo, D_RNN), F32),
            pltpu.VMEM((R, D_RNN), F32),
            pltpu.VMEM((R, D_RNN), F32),
            pltpu.VMEM((R, D_RNN), F32),
            pltpu.VMEM((R, D_RNN), F32),
            pltpu.VMEM((nb, D_RNN), F32),
            pltpu.VMEM((R, D_MODEL), F32),
        ],
        compiler_params=_params(("arbitrary",)),
        name="rglru",
    )(rx, gy, conv_w, conv_b, wa, ba, wi, bi, lam, wo)


def _kvprep_kernel(zk_ref, zv_ref, g_ref, kn_ref, vt_ref):
    rows = zk_ref.shape[1]
    x = jnp.concatenate([zk_ref[0], zk_ref[1]], axis=1)
    seg = jnp.where(_iota((KVW, KVW), 0) // HEAD_DIM == _iota((KVW, KVW), 1) // HEAD_DIM, 1.0, 0.0).astype(BF16)
    hi, lo = _split(x * x)
    ssq = _dot(jnp.concatenate([hi, lo], axis=0), seg)
    kn = (x * lax.rsqrt((ssq[0:rows] + ssq[rows:2 * rows]) * (1.0 / HEAD_DIM) + EPS) * g_ref[...]).astype(BF16)
    for pr in range(KVW // LANES):
        kn_ref[pr] = kn[:, pr * LANES:(pr + 1) * LANES]
    eye = jnp.where(_iota((KVW, KVW), 0) == _iota((KVW, KVW), 1), 1.0, 0.0).astype(BF16)
    pad = jnp.where(_iota((VROWS - HEAD_DIM, KCH), 0) == 0, 1.0, 0.0).astype(BF16)
    for c in range(rows // KCH):
        sl = slice(c * KCH, (c + 1) * KCH)
        v = jnp.concatenate([zv_ref[0, sl, :], zv_ref[1, sl, :]], axis=1).astype(BF16)
        vt = _dot_nt(eye, v).astype(BF16)
        for g in range(N_KV):
            vt_ref[c, g, 0:HEAD_DIM, :] = vt[g * HEAD_DIM:(g + 1) * HEAD_DIM]
            vt_ref[c, g, HEAD_DIM:VROWS, :] = pad


def _kvprep(zkv, gains, *, B, S, cpb=4):
    nch = S // KCH
    assert nch % cpb == 0
    nblk = nch // cpb
    rows = cpb * KCH
    return pl.pallas_call(
        _kvprep_kernel,
        grid=(2, B, nblk),
        in_specs=[
            pl.BlockSpec((None, KVW // LANES, rows, LANES), lambda w, b, i: (2 + 2 * w, 0, b * nblk + i, 0)),
            pl.BlockSpec((None, KVW // LANES, rows, LANES), lambda w, b, i: (3 + 2 * w, 0, b * nblk + i, 0)),
            pl.BlockSpec((None, 1, KVW), lambda w, b, i: (w, 0, 0)),
        ],
        out_specs=[
            pl.BlockSpec((None, None, KVW // LANES, rows, LANES), lambda w, b, i: (w, b, 0, i, 0)),
            pl.BlockSpec((None, None, cpb, N_KV, VROWS, KCH), lambda w, b, i: (w, b, i, 0, 0, 0)),
        ],
        out_shape=[
            jax.ShapeDtypeStruct((2, B, KVW // LANES, S, LANES), BF16),
            jax.ShapeDtypeStruct((2, B, nch, N_KV, VROWS, KCH), BF16),
        ],
        compiler_params=_params(("parallel", "parallel", "parallel")),
        name="kvprep",
    )(zkv, zkv, gains)


def _compress_kernel(zk_ref, zv_ref, pos_ref, w1_ref, w2_ref, g_ref, kc_ref, vct_ref):
    nch = zk_ref.shape[0] // CMP_STRIDE
    lane = _iota((nch, LANES), 1)
    npair = CMP_STRIDE // 2
    eye = jnp.where(_iota((HEAD_DIM, HEAD_DIM), 0) == _iota((HEAD_DIM, HEAD_DIM), 1), 1.0, 0.0).astype(BF16)
    for kind, z_ref in enumerate((zk_ref, zv_ref)):
        for ge in range(2):
            combs = []
            for p in range(npair):
                a = z_ref[pl.ds(2 * p, nch, stride=CMP_STRIDE), :]
                b = z_ref[pl.ds(2 * p + 1, nch, stride=CMP_STRIDE), :]
                if ge == 0:
                    combs.append(jnp.where(lane < HEAD_DIM, a, pltpu.roll(b, HEAD_DIM, axis=1)))
                else:
                    combs.append(jnp.where(lane < HEAD_DIM, pltpu.roll(a, HEAD_DIM, axis=1), b))
            halves = []
            half_w = npair * LANES
            for half in range(2):
                lhs = jnp.concatenate(
                    [(combs[p] + pos_ref[kind, half * npair + p:half * npair + p + 1, :]).astype(BF16)
                     for p in range(npair)], axis=1)
                halves.append(_dot(lhs, w1_ref[kind, half * half_w:(half + 1) * half_w, :]))
            hid = halves[0] + pltpu.roll(halves[1], nch - 1, axis=0)
            hid = jax.nn.gelu(hid)
            out = _dot(hid.astype(BF16), w2_ref[kind])
            if kind == 0:
                out = out * lax.rsqrt(jnp.mean(out * out, axis=-1, keepdims=True) + EPS) * g_ref[...]
            res = jnp.where(_iota(out.shape, 0) < nch - 1, out, 0.0).astype(BF16)
            if kind == 0:
                kc_ref[ge] = res
            else:
                vct_ref[ge] = _dot_nt(eye, res).astype(BF16)


def _compress(zkv, pos, w1, w2, gain, *, B, S):
    nch = S // CMP_STRIDE
    return pl.pallas_call(
        _compress_kernel,
        grid=(B, KVW // LANES),
        in_specs=[
            pl.BlockSpec((None, None, S, LANES), lambda b, lp: (0, lp, b, 0)),
            pl.BlockSpec((None, None, S, LANES), lambda b, lp: (1, lp, b, 0)),
            _const_spec((2, CMP_STRIDE, LANES)),
            _const_spec((2, CMP_LEN * HEAD_DIM, CMP_HID)),
            _const_spec((2, CMP_HID, HEAD_DIM)),
            _const_spec((1, HEAD_DIM)),
        ],
        out_specs=[
            pl.BlockSpec((None, 2, nch, HEAD_DIM), lambda b, lp: (b, lp, 0, 0)),
            pl.BlockSpec((None, 2, HEAD_DIM, nch), lambda b, lp: (b, lp, 0, 0)),
        ],
        out_shape=[
            jax.ShapeDtypeStruct((B, N_KV, nch, HEAD_DIM), BF16),
            jax.ShapeDtypeStruct((B, N_KV, HEAD_DIM, nch), BF16),
        ],
        compiler_params=_params(("parallel", "parallel")),
        name="compress",
    )(zkv, zkv, pos, w1, w2, gain)


def _spread_heads(x):
    y = x + pltpu.roll(x, 2 * Q_BLK, axis=1)
    return y + pltpu.roll(y, Q_BLK, axis=1)


def _swap_heads(x):
    n = x.shape[0]
    lane = _iota((n, LANES), 1)
    halves = []
    for pr in range(x.shape[1] // LANES):
        slab = jnp.concatenate([x[:, pr * LANES:(pr + 1) * LANES], jnp.zeros((LANES -n, LANES), x.dtype)], axis=0)
        tr = slab.T
        halves.append(jnp.where(lane < n, tr[0:n], pltpu.roll(tr[n:2 * n], n, axis=1)))
    return jnp.concatenate(halves, axis=1)


def _nsa_kernel(*refs, nsel, qpb):
    per_chunk = KCH // SEL_LEN
    for k in range(qpb):
        _nsa_block(k, "head", None, *refs, nsel=nsel, qpb=qpb)

    def step(i, carry):
        for k in range(qpb):
            _nsa_block(k, "step", i, *refs, nsel=nsel, qpb=qpb)
        return carry

    common = pl.program_id(1) * (qpb // per_chunk)
    lax.fori_loop(0, common, step, 0)
    for k in range(qpb):
        for extra in range(k // per_chunk):
            _nsa_block(k, "step", common + extra, *refs, nsel=nsel, qpb=qpb)
    for k in range(qpb):
        _nsa_block(k, "tail", None, *refs, nsel=nsel, qpb=qpb)


def _nsa_block(k, phase, step_i, zq_ref, qg_ref, kc_ref, vct_ref, ks_ref, kw_ref, vst_ref, vwt_ref, ng_ref, o_ref,
               qtp_ref, oc_ref, ow_ref, bias_ref, ms_ref, accs_ref, mw_ref, accw_ref, sbuf_ref, wbuf_ref,
               gates_ref, scs_ref, wbias_ref, *, nsel, qpb):
    qtp_ref, oc_ref, ow_ref, bias_ref, ms_ref, accs_ref, sbuf_ref, gates_ref = (
        r.at[k] for r in (qtp_ref, oc_ref, ow_ref, bias_ref, ms_ref, accs_ref, sbuf_ref, gates_ref))
    qi = pl.program_id(1) * qpb + k
    q_rows = pl.ds(k * Q_BLK, Q_BLK)
    s0 = qi * Q_BLK
    cur = qi
    ncmp = kc_ref.shape[1]
    per_chunk = KCH // SEL_LEN
    groups = range(N_KV)
    n_chunks = qi // per_chunk + 1
    cur_slab = k % per_chunk
    cut_rows = slice(cur_slab * SEL_LEN, (cur_slab + 1) * SEL_LEN)
    last_slabs = per_chunk // 2 if cur_slab < per_chunk // 2 else per_chunk

    def scores(k_ref, i):
        k0 = pl.multiple_of(i * KCH, KCH)
        return [_dot(k_ref[g // 2, pl.ds(k0, KCH), :], qtp_ref[g]) for g in groups]

    def park(buf_ref, vals):
        for g in groups:
            buf_ref[g] = vals[g]

    def cut(buf_ref, tri):
        for g in groups:
            buf_ref[g, cut_rows, :] = buf_ref[g, cut_rows, :] + tri

    def update(g, blocks, shifts, vt, m_ref, acc_ref):
        m = m_ref[g]
        part = None
        for blk, sh in zip(blocks, shifts):
            top8 = jnp.max(blk.reshape(SEL_LEN // 8, 8, GW), axis=0) + sh
            part = top8 if part is None else jnp.maximum(part, top8)
        mnew = jnp.maximum(m, jnp.max(part, axis=0, keepdims=True))
        e = jnp.concatenate([jnp.exp2(blk + (sh - mnew)) for blk, sh in zip(blocks, shifts)], axis=0).astype(BF16)
        acc_ref[g] = jnp.exp2(m - mnew) * acc_ref[g] + _dot(vt, e)
        m_ref[g] = mnew

    def slabs(buf_ref, g, ns):
        return [buf_ref[g, nb * SEL_LEN:(nb + 1) * SEL_LEN, :] for nb in range(ns)]

    def sel_softmax(i, ns=per_chunk):
        for g in groups:
            shifts = [bias_ref[g, pl.ds(i * per_chunk + nb, 1), :] for nb in range(ns)]
            update(g, slabs(sbuf_ref, g, ns), shifts, vst_ref[i, g, :, 0:ns * SEL_LEN], ms_ref, accs_ref)

    key_row = _iota((SEL_LEN, GW), 0)
    in_blk_t = _iota((SEL_LEN, GW), 1) % Q_BLK
    causal_tri = jnp.where(key_row <= in_blk_t, 0.0, NEG)

    if phase == "step":
        nxt = scores(ks_ref, step_i + 1)
        sel_softmax(step_i)
        park(sbuf_ref, nxt)
        return

    if phase == "tail":
        cut(sbuf_ref, causal_tri)
        sel_softmax(n_chunks - 1, last_slabs)
        outs = []
        for g in groups:
            o_s = accs_ref[g, 0:HEAD_DIM, :] * (1.0 / jnp.maximum(accs_ref[g, HEAD_DIM:HEAD_DIM + 1, :], 1e-30))
            gate = [gates_ref[g, c:c + 1, :] for c in range(3)]
            o_t = gate[0] * oc_ref[g] + gate[1] * o_s + gate[2] * ow_ref[g]
            outs.append(_swap_heads(o_t))
        for g in groups:
            o_ref[g, q_rows, :] = outs[g].astype(o_ref.dtype)
        return

    lane_t = s0 + (_iota((1, GW), 1) % Q_BLK)
    lane_grp = _iota((1, GW), 1) // Q_BLK
    ov_n = _iota((nsel, ncmp), 0) * SEL_LEN
    ov_c = _iota((nsel, ncmp), 1) * CMP_STRIDE
    overlap = jnp.where((ov_c < ov_n + SEL_LEN) & (ov_c + CMP_LEN > ov_n), 1.0, 0.0).astype(BF16)


    ok_c = _iota((ncmp, GW), 0) * CMP_STRIDE + (CMP_LEN - 1) <= lane_t
    ngt = jnp.concatenate([ng_ref[q_rows, :], jnp.zeros((LANES -Q_BLK, NG_PAD), F32)], axis=0).T
    ngt_hi = pltpu.roll(ngt, Q_BLK, axis=1)
    low_half = _iota((1, LANES), 1) < Q_BLK
    for g in groups:
        for c in range(3):
            r = c * N_HEADS + g * HPG
            pairs = [jnp.where(low_half, ngt[r + j:r + j + 1], ngt_hi[r + j + 1:r + j + 2]) for j in (0, 2)]
            gates_ref[g, c:c + 1, :] = jnp.concatenate(pairs, axis=1)

    qts = []
    for g in groups:
        xt = _swap_heads(zq_ref[g, q_rows, :])
        ssq = jnp.sum(xt * xt, axis=0, keepdims=True)
        qts.append((xt * lax.rsqrt(ssq * (1.0 / HEAD_DIM) + EPS) * qg_ref[...]).astype(BF16))
    scs = []
    for g in groups:
        off = (g % 2) * HEAD_DIM
        qtp_ref[g, off:off + HEAD_DIM, :] = qts[g]
        qtp_ref[g, HEAD_DIM - off:2 * HEAD_DIM - off, :] = jnp.zeros((HEAD_DIM, GW), BF16)
        scs.append(_dot(kc_ref[g], qts[g]))

    n_win = WINDOW // KCH + 1
    win_ids = [jnp.maximum(n_chunks - (n_win - j), 0) for j in range(n_win)]
    win_dead = [jnp.where(n_chunks >= n_win - j, 0.0, NEG) for j in range(n_win)]
    park(scs_ref, scs)
    park(sbuf_ref, scores(ks_ref, 0))
    park(wbuf_ref, scores(kw_ref, win_ids[0]))

    expired_tri = jnp.where(key_row > in_blk_t, 0.0, NEG)
    win_blocks = WINDOW // SEL_LEN
    n_all = _iota((nsel, GW), 0)
    wbias_ref[...] = jnp.where((n_all >= cur - win_blocks) & (n_all <= cur), 0.0, NEG)
    cut(wbuf_ref, jnp.where(cur >= win_blocks, expired_tri, 0.0))

    ms_ref[...] = jnp.full(ms_ref.shape, 0.5 * NEG, F32)
    mw_ref[...] = jnp.full(mw_ref.shape, 0.5 * NEG, F32)
    accs_ref[...] = jnp.zeros(accs_ref.shape, F32)
    accw_ref[...] = jnp.zeros(accw_ref.shape, F32)

    def win_softmax(i, dead, lo=0, hi=per_chunk):
        shifts = [wbias_ref[pl.ds(i * per_chunk + nb, 1), :] + dead for nb in range(lo, hi)]
        for g in groups:
            blocks = [wbuf_ref[g, nb * SEL_LEN:(nb + 1) * SEL_LEN, :] for nb in range(lo, hi)]
            update(g, blocks, shifts, vwt_ref[i, g, :, lo * SEL_LEN:hi * SEL_LEN], mw_ref, accw_ref)

    first_lo = per_chunk // 2 if cur_slab >= per_chunk // 2 else 0
    for j in range(n_win - 1):
        nxt = scores(kw_ref, win_ids[j + 1])
        win_softmax(win_ids[j], win_dead[j], lo=first_lo if j == 0 else 0)
        park(wbuf_ref, nxt)

    imp_all = jnp.zeros((nsel, GW), F32)
    some_c = lane_t >= CMP_LEN - 1
    ones_rows = jnp.ones((16, ncmp), BF16)
    for g in groups:
        scm = jnp.where(ok_c, scs_ref[g], NEG)
        ec = jnp.exp2(scm - jnp.max(scm, axis=0, keepdims=True))
        both = _dot(jnp.concatenate([vct_ref[g], overlap, ones_rows], axis=0), ec.astype(BF16))
        denom = both[HEAD_DIM + nsel:HEAD_DIM + nsel + 1]
        both = both[0:HEAD_DIM + nsel] * jnp.where(some_c, 1.0 / jnp.maximum(denom, 1e-30), 0.0)
        oc_ref[g] = both[0:HEAD_DIM]
        imp = _spread_heads(both[HEAD_DIM:HEAD_DIM + nsel])
        imp_all = jnp.where(lane_grp == g, imp, imp_all)

    n_io = _iota((nsel, GW), 0)
    forced = (n_io == 0) | (n_io == cur) | (n_io == cur - 1)
    valid = n_io <= cur
    score = jnp.where(forced, FORCE_SCORE, imp_all)
    score = jnp.where(valid, score, -1.0)
    rows8 = [score[r:r + 8] for r in range(0, nsel, 8)]
    ranks = [jnp.zeros((8, GW), F32) for _ in rows8]
    sub = _iota((8, GW), 0)
    for m in range(nsel):
        rowv = score[m:m + 1, :]
        for k, blk in enumerate(rows8):
            ge = jnp.where(rowv >= blk, 1.0, 0.0)
            gt = jnp.where(rowv > blk, 1.0, 0.0)
            if 8 * k > m:
                inc = ge
            elif 8 * k + 7 < m:
                inc = gt
            else:
                inc = jnp.where(sub + 8 * k > m, ge, gt)
            ranks[k] = ranks[k] + inc
    rank = jnp.concatenate(ranks, axis=0)
    sel_all = jnp.where((rank < float(min(SEL_TOPN, nsel))) & valid, 1.0, 0.0)
    for g in groups:
        mine = jnp.where(lane_grp == g, sel_all, 0.0)
        bias_ref[g] = (_spread_heads(mine) - 1.0) * (-NEG)

    cut(wbuf_ref, causal_tri)
    win_softmax(win_ids[n_win - 1], win_dead[n_win - 1], hi=last_slabs)
    for g in groups:
        ow_ref[g] = accw_ref[g, 0:HEAD_DIM, :] * (1.0 / jnp.maximum(accw_ref[g, HEAD_DIM:HEAD_DIM + 1, :], 1e-30))


def _nsa(zq, qgain, kc, vct, kn, vt, ng, *, B, S):
    T = B * S
    qpb = 2 * (KCH // SEL_LEN)
    assert S % (Q_BLK * qpb) == 0
    nq = S // (Q_BLK * qpb)
    nch = S // KCH
    ncmp = S // CMP_STRIDE
    nsel = S // SEL_LEN
    tok = lambda b, i: (0, b * nq + i, 0)
    return pl.pallas_call(
        functools.partial(_nsa_kernel, nsel=nsel, qpb=qpb),
        grid=(B, nq),
        in_specs=[
            pl.BlockSpec((N_KV, Q_BLK * qpb, GW), tok),
            _const_spec((HEAD_DIM, GW)),
            pl.BlockSpec((None, N_KV, ncmp, HEAD_DIM), lambda b, i: (b, 0, 0, 0)),
            pl.BlockSpec((None, N_KV, HEAD_DIM, ncmp), lambda b, i: (b, 0, 0, 0)),
            pl.BlockSpec((None, None, 2, S, LANES), lambda b, i: (0, b, 0, 0, 0)),
            pl.BlockSpec((None, None, 2, S, LANES), lambda b, i: (1, b, 0, 0, 0)),
            pl.BlockSpec((None, None, nch, N_KV, VROWS, KCH), lambda b, i: (0, b, 0, 0, 0, 0)),
            pl.BlockSpec((None, None, nch, N_KV, VROWS, KCH), lambda b, i: (1, b, 0, 0, 0, 0)),
            pl.BlockSpec((Q_BLK * qpb, NG_PAD), lambda b, i: (b * nq + i, 0)),
        ],
        out_specs=pl.BlockSpec((N_KV, Q_BLK * qpb, GW), tok),
        out_shape=jax.ShapeDtypeStruct((N_KV, T, GW), BF16),
        scratch_shapes=[
            pltpu.VMEM((qpb, N_KV, 2 * HEAD_DIM, GW), BF16),
            pltpu.VMEM((qpb, N_KV, HEAD_DIM, GW), F32),
            pltpu.VMEM((qpb, N_KV, HEAD_DIM, GW), F32),
            pltpu.VMEM((qpb, N_KV, nsel, GW), F32),
            pltpu.VMEM((qpb, N_KV, 1, GW), F32),
            pltpu.VMEM((qpb, N_KV, VROWS, GW), F32),
            pltpu.VMEM((N_KV, 1, GW), F32),
            pltpu.VMEM((N_KV, VROWS, GW), F32),
            pltpu.VMEM((qpb, N_KV, KCH, GW), F32),
            pltpu.VMEM((N_KV, KCH, GW), F32),
            pltpu.VMEM((qpb, N_KV, 8, GW), F32),
            pltpu.VMEM((N_KV, ncmp, GW), F32),
            pltpu.VMEM((nsel, GW), F32),
        ],
        compiler_params=_params(("parallel", "arbitrary")),
        name="nsa",
    )(zq, qgain, kc, vct, kn, kn, vt, vt, ng)


def _merge_ffn_kernel(x_ref, ya_ref, o_ref_in, mg_ref, wn_ref, wo_ref, g2_ref, wg_ref, wu_ref, wd_ref,
                      out_ref, h_ref, *, tf):
    yb = _dot(o_ref_in[0], wn_ref[0:GW, :])
    for g in range(1, N_KV):
        yb = yb + _dot(o_ref_in[g], wn_ref[g * GW:(g + 1) * GW, :])
    tm = x_ref.shape[0]
    ya = jnp.concatenate([ya_ref[ck].reshape(tm, LANES) for ck in range(D_MODEL // LANES)], axis=1)
    mixed = (mg_ref[:, 0:D_MODEL].astype(F32) * ya
             + mg_ref[:, D_MODEL:2 * D_MODEL].astype(F32) * yb)
    x1 = x_ref[...] + _dot(mixed.astype(BF16), wo_ref[...])
    y = x1 * lax.rsqrt(jnp.mean(x1 * x1, axis=-1, keepdims=True) + EPS) * g2_ref[...]
    h_ref[...] = y.astype(BF16)
    out_ref[...] = x1
    for j in range(D_FF // tf):
        h = h_ref[...]
        sl = slice(j * tf, (j + 1) * tf)
        act = jax.nn.silu(_dot(h, wg_ref[:, sl])) * _dot(h, wu_ref[:, sl])
        out_ref[...] += _dot(act.astype(BF16), wd_ref[sl, :])


def _merge_ffn(x2d, ya, o, mg, wn, wo, gain2, wg, wu, wd, *, B, S, tm=512, tf=256):
    assert S % tm == 0 and D_FF % tf == 0
    ni = S // tm
    tok = lambda b, i: (b * ni + i, 0)
    once = lambda shape: pl.BlockSpec(shape, lambda b, i: (0,) * len(shape), pipeline_mode=pl.Buffered(1))
    return pl.pallas_call(
        functools.partial(_merge_ffn_kernel, tf=tf),
        grid=(B, ni),
        in_specs=[
            pl.BlockSpec((tm, D_MODEL), tok),
            pl.BlockSpec((D_MODEL // LANES, tm // 8, None, 8, LANES), lambda b, i: (0, i, b, 0, 0)),
            pl.BlockSpec((N_KV, tm, GW), lambda b, i: (0, b * ni + i, 0)),
            pl.BlockSpec((tm, 2 * D_MODEL), tok),
            once((N_HEADS * HEAD_DIM, D_MODEL)),
            once((D_MODEL, D_MODEL)),
            _const_spec((1, D_MODEL)),
            once((D_MODEL, D_FF)),
            once((D_MODEL, D_FF)),
            once((D_FF, D_MODEL)),
        ],
        out_specs=pl.BlockSpec((tm, D_MODEL), tok),
        out_shape=jax.ShapeDtypeStruct((B * S, D_MODEL), F32),
        scratch_shapes=[pltpu.VMEM((tm, D_MODEL), BF16)],
        compiler_params=_params(("parallel", "parallel")),
        name="merge_ffn",
    )(x2d, ya, o, mg, wn, wo, gain2, wg, wu, wd)


def _layer(x, p):
    B, S, _ = x.shape
    T = B * S
    dh = HEAD_DIM
    row = lambda v: v.reshape(1, -1)
    w_in = p["w_in"]
    n_ng = 3 * N_HEADS
    o_ng = 2 * D_RNN + N_HEADS * dh + 6 * KVW
    w_packed = jnp.concatenate(
        [w_in[:, :o_ng], jnp.pad(w_in[:, o_ng:o_ng + n_ng], ((0, 0), (0, NG_PAD - n_ng))), w_in[:, o_ng + n_ng:]],
        axis=1).astype(BF16)
    rx, gy, zq, zkv, ng, mg = _inproj(x, row(p["norm1"]), w_packed)

    slabs = lambda a: a.reshape(a.shape[0], S * B, LANES)
    ya = _rglru(slabs(rx), slabs(gy), p["conv_w"], row(p["conv_b"]),
                p["rg_wa"].astype(BF16), row(p["rg_ba"]), p["rg_wi"].astype(BF16), row(p["rg_bi"]),
                row(p["rg_lambda"]), p["w_rg_out"].astype(BF16), nb=B)
    ya = ya.reshape(D_MODEL // LANES, S // 8, B, 8, LANES)

    kgains = jnp.tile(p["k_norm"][1:3], (1, N_KV)).reshape(2, 1, KVW)
    kn, vt = _kvprep(zkv, kgains, B=B, S=S)
    pos = jnp.stack([p["cmp_pos_k"], p["cmp_pos_v"]]).reshape(2, CMP_STRIDE, 2 * dh)
    w1 = jnp.stack([p["cmp_k_w1"], p["cmp_v_w1"]]).astype(BF16)
    w2 = jnp.stack([p["cmp_k_w2"], p["cmp_v_w2"]]).astype(BF16)
    kc, vct = _compress(zkv, pos, w1, w2, row(p["k_norm"][0]), B=B, S=S)
    qgain = jnp.broadcast_to((p["q_norm"] * (dh ** -0.5 * LOG2E))[:, None], (dh, GW))
    o = _nsa(zq, qgain, kc, vct, kn, vt, ng, B=B, S=S)

    x2 = _merge_ffn(x.reshape(T, D_MODEL), ya, o, mg, p["w_nsa_out"].astype(BF16), p["w_o"].astype(BF16),
                    row(p["norm2"]), p["w_gate"].astype(BF16), p["w_up"].astype(BF16), p["w_down"].astype(BF16),
                    B=B, S=S)
    return x2.reshape(B, S, D_MODEL)


def kernel(x, norm1, w_in, conv_w, conv_b, rg_wa, rg_ba, rg_wi, rg_bi, rg_lambda, q_norm, k_norm, cmp_pos_k,
           cmp_pos_v, cmp_k_w1, cmp_k_w2, cmp_v_w1, cmp_v_w2, w_rg_out, w_nsa_out, w_o, norm2, w_gate, w_up, w_down):
    params = dict(norm1=norm1, w_in=w_in, conv_w=conv_w, conv_b=conv_b, rg_wa=rg_wa, rg_ba=rg_ba, rg_wi=rg_wi,
                  rg_bi=rg_bi, rg_lambda=rg_lambda, q_norm=q_norm, k_norm=k_norm, cmp_pos_k=cmp_pos_k,
                  cmp_pos_v=cmp_pos_v, cmp_k_w1=cmp_k_w1, cmp_k_w2=cmp_k_w2, cmp_v_w1=cmp_v_w1, cmp_v_w2=cmp_v_w2,
                  w_rg_out=w_rg_out, w_nsa_out=w_nsa_out, w_o=w_o, norm2=norm2, w_gate=w_gate, w_up=w_up,
                  w_down=w_down)
    for l in range(norm1.shape[0]):
        x = _layer(x, {k: v[l] for k, v in params.items()})
    return x
```

```python
import functools

import jax
import jax.numpy as jnp
from jax import lax
from jax.experimental import pallas as pl
from jax.experimental.pallas import tpu as pltpu

D_MODEL = 1024
D_RNN = 1024
RG_BLOCKS = 4
RG_BW = D_RNN // RG_BLOCKS
CONV_W = 4
RG_C = 8.0
N_HEADS = 16
N_KV = 4
HEAD_DIM = 64
HPG = N_HEADS // N_KV
CMP_LEN = 32
CMP_STRIDE = 16
CMP_HID = 256
SEL_LEN = 64
SEL_TOPN = 16
WINDOW = 512
Q_BLK = 64
D_FF = 2816
EPS = 1e-6
FORCE_SCORE = 1e6
NEG = -1e30
LOG2E = 1.4426950408889634

LANES = 128
KCH = 256
GW = HPG * HEAD_DIM
KVW = N_KV * HEAD_DIM
VROWS = 80
NG_PAD = LANES

F32 = jnp.float32
BF16 = jnp.bfloat16

VMEM_LIMIT = 56 * 1024 * 1024


def _params(sem):
    return pltpu.CompilerParams(dimension_semantics=sem, vmem_limit_bytes=VMEM_LIMIT)


def _dot(a, b):
    return jnp.dot(a, b, preferred_element_type=F32)


def _dot_nt(a, b):
    return lax.dot_general(a, b, (((1,), (1,)), ((), ())), preferred_element_type=F32)


def _split(x):
    hi = x.astype(BF16)
    lo = (x - hi.astype(F32)).astype(BF16)
    return hi, lo


def _iota(shape, dim):
    return lax.broadcasted_iota(jnp.int32, shape, dim)


def _const_spec(shape):
    return pl.BlockSpec(shape, lambda *_: (0,) * len(shape))


_O_RY = D_RNN
_O_Q = 2 * D_RNN
_O_KV = _O_Q + N_HEADS * HEAD_DIM
_O_NG = _O_KV + 6 * KVW
_O_MG = _O_NG + NG_PAD
_W_COLS = _O_MG + 2 * D_MODEL


def _inproj_kernel(x_ref, g_ref, w_ref, rx_ref, gy_ref, zq_ref, zkv_ref, ng_ref, mg_ref, h_ref):
    x = x_ref[...]
    y = x * lax.rsqrt(jnp.mean(x * x, axis=-1, keepdims=True) + EPS) * g_ref[...]
    h_ref[...] = y.astype(BF16)

    def proj(c0, width):
        return _dot(h_ref[...], w_ref[:, c0:c0 + width])

    cw = 256
    tm = x_ref.shape[0]
    for c in range(D_RNN // cw):
        rx = proj(c * cw, cw)
        gy = jax.nn.gelu(proj(_O_RY + c * cw, cw))
        for hf in range(cw // LANES):
            rx_ref[c * (cw // LANES) + hf] = rx[:, hf * LANES:(hf + 1) * LANES].reshape(tm // 8, 8, LANES)
            gy_ref[c * (cw // LANES) + hf] = gy[:, hf * LANES:(hf + 1) * LANES].reshape(tm // 8, 8, LANES)
    for g in range(N_KV):
        zq_ref[g] = proj(_O_Q + g * GW, GW)
    for s in range(6):
        z = proj(_O_KV + s * KVW, KVW)
        for pr in range(KVW // LANES):
            zkv_ref[s, pr] = z[:, pr * LANES:(pr + 1) * LANES]
    ng_ref[...] = jax.nn.sigmoid(proj(_O_NG, NG_PAD))
    for c in range(2 * D_MODEL // cw):
        mg_ref[:, c * cw:(c + 1) * cw] = jax.nn.sigmoid(proj(_O_MG + c * cw, cw)).astype(mg_ref.dtype)


def _inproj(x, gain, w, *, tm=256):
    B, S, D = x.shape
    T = B * S
    ni = S // tm
    nck = D_RNN // LANES
    tok = lambda b, i: (b * ni + i, 0)
    slab_spec = pl.BlockSpec((nck, tm // 8, None, 8, LANES), lambda b, i: (0, i, b, 0, 0))
    slab_shape = jax.ShapeDtypeStruct((nck, S // 8, B, 8, LANES), F32)
    return pl.pallas_call(
        _inproj_kernel,
        grid=(B, ni),
        in_specs=[
            pl.BlockSpec((None, tm, D), lambda b, i: (b, i, 0)),
            _const_spec((1, D)),
            pl.BlockSpec((D, _W_COLS), lambda b, i: (0, 0), pipeline_mode=pl.Buffered(1)),
        ],
        out_specs=[
            slab_spec,
            slab_spec,
            pl.BlockSpec((N_KV, tm, GW), lambda b, i: (0, b * ni + i, 0)),
            pl.BlockSpec((6, KVW // LANES, tm, LANES), lambda b, i: (0, 0, b * ni + i, 0)),
            pl.BlockSpec((tm, NG_PAD), tok),
            pl.BlockSpec((tm, 2 * D_MODEL), tok),
        ],
        out_shape=[
            slab_shape,
            slab_shape,
            jax.ShapeDtypeStruct((N_KV, T, GW), F32),
            jax.ShapeDtypeStruct((6, KVW // LANES, T, LANES), F32),
            jax.ShapeDtypeStruct((T, NG_PAD), F32),
            jax.ShapeDtypeStruct((T, 2 * D_MODEL), BF16),
        ],
        scratch_shapes=[pltpu.VMEM((tm, D), BF16)],
        compiler_params=_params(("parallel", "parallel")),
        name="inproj",
    )(x, gain, w)


def _rglru_kernel(x_ref, gy_ref, cw_ref, cb_ref, wa_ref, ba_ref, wi_ref, bi_ref, lam_ref, wo_ref,
                  o_ref, xb_ref, gyb_ref, a_ref, u_ref, hb_ref, hs_ref, yb_ref, *, nb, tt):
    R = nb * tt
    halo = (CONV_W - 1) * nb
    step = pl.program_id(0)
    nck = D_RNN // LANES

    @pl.when(step == 0)
    def _():
        xb_ref[0:halo, :] = jnp.zeros((halo, D_RNN), F32)
        hs_ref[...] = jnp.zeros((nb, D_RNN), F32)

    def slab_rows(t8, t_lo):
        return pl.ds(t8 * (8 * nb) + t_lo, nb, stride=8)

    def stage_in(t8, carry):
        for t_lo in range(8):
            r0 = pl.multiple_of((t8 * 8 + t_lo) * nb, nb)
            for ck in range(nck):
                cols = slice(ck * LANES, (ck + 1) * LANES)
                xb_ref[pl.ds(halo + r0, nb), cols] = x_ref[ck, slab_rows(t8, t_lo), :]
                gyb_ref[pl.ds(r0, nb), cols] = gy_ref[ck, slab_rows(t8, t_lo), :]
        return carry

    lax.fori_loop(0, tt // 8, stage_in, 0)
    xr = cb_ref[...] + cw_ref[0:1, :] * xb_ref[0:R, :]
    for k in range(1, CONV_W):
        xr = xr + cw_ref[k:k + 1, :] * xb_ref[k * nb:k * nb + R, :]
    tail = xb_ref[R:R + halo, :]
    xb_ref[0:halo, :] = tail

    xrb = xr.astype(BF16)
    decay_log2 = (-RG_C * LOG2E) * jax.nn.softplus(-lam_ref[...])
    row = _iota((R, RG_BW), 0)
    first = (row < nb) & (step == 0)
    for n in range(RG_BLOCKS):
        sl = slice(n * RG_BW, (n + 1) * RG_BW)
        xn = xrb[:, sl]
        r = jax.nn.sigmoid(_dot(xn, wa_ref[n]) + ba_ref[:, sl])
        ig = jax.nn.sigmoid(_dot(xn, wi_ref[n]) + bi_ref[:, sl])
        a = jnp.exp2(r * decay_log2[:, sl])
        mult = jnp.sqrt(1.0 - a * a)
        mult = jnp.where(first, 1.0, mult)
        a_ref[:, sl] = a
        u_ref[:, sl] = mult * ig * xr[:, sl]

    def body(t, h):
        r0 = pl.multiple_of(t * nb, nb)
        h = a_ref[pl.ds(r0, nb), :] * h + u_ref[pl.ds(r0, nb), :]
        hb_ref[pl.ds(r0, nb), :] = h
        return h

    h = lax.fori_loop(0, tt, body, hs_ref[...], unroll=8)
    hs_ref[...] = h
    yb_ref[...] = _dot((hb_ref[...] * gyb_ref[...]).astype(BF16), wo_ref[...])

    def stage_out(t8, carry):
        for t_lo in range(8):
            r0 = pl.multiple_of((t8 * 8 + t_lo) * nb, nb)
            for ck in range(nck):
                o_ref[ck, slab_rows(t8, t_lo), :] = yb_ref[pl.ds(r0, nb), ck * LANES:(ck + 1) * LANES]
        return carry

    lax.fori_loop(0, tt // 8, stage_out, 0)


def _rglru(rx, gy, conv_w, conv_b, wa, ba, wi, bi, lam, wo, *, nb, tt=64):
    nck, rows, _ = rx.shape
    S = rows // nb
    assert S % tt == 0 and tt % 8 == 0
    R = nb * tt
    halo = (CONV_W - 1) * nb
    return pl.pallas_call(
        functools.partial(_rglru_kernel, nb=nb, tt=tt),
        grid=(S // tt,),
        in_specs=[
            pl.BlockSpec((nck, R, LANES), lambda s: (0, s, 0)),
            pl.BlockSpec((nck, R, LANES), lambda s: (0, s, 0)),
            _const_spec((CONV_W, D_RNN)),
            _const_spec((1, D_RNN)),
            _const_spec((RG_BLOCKS, RG_BW, RG_BW)),
            _const_spec((1, D_RNN)),
            _const_spec((RG_BLOCKS, RG_BW, RG_BW)),
            _const_spec((1, D_RNN)),
            _const_spec((1, D_RNN)),
            _const_spec((D_RNN, D_MODEL)),
        ],
        out_specs=pl.BlockSpec((D_MODEL // LANES, R, LANES), lambda s: (0, s, 0)),
        out_shape=jax.ShapeDtypeStruct((D_MODEL // LANES, rows, LANES), F32),
        scratch_shapes=[
            pltpu.VMEM((R + halo, D_RNN), F32),
            pltpu.VMEM((R, D_RNN), F32),
            pltpu.VMEM((R, D_RNN), F32),
            pltpu.VMEM((R, D_RNN), F32),
            pltpu.VMEM((R, D_RNN), F32),
            pltpu.VMEM((nb, D_RNN), F32),
            pltpu.VMEM((R, D_MODEL), F32),
        ],
        compiler_params=_params(("arbitrary",)),
        name="rglru",
    )(rx, gy, conv_w, conv_b, wa, ba, wi, bi, lam, wo)


def _kvprep_kernel(zk_ref, zv_ref, g_ref, kn_ref, vt_ref):
    rows = zk_ref.shape[1]
    x = jnp.concatenate([zk_ref[0], zk_ref[1]], axis=1)
    seg = jnp.where(_iota((KVW, KVW), 0) // HEAD_DIM == _iota((KVW, KVW), 1) // HEAD_DIM, 1.0, 0.0).astype(BF16)
    hi, lo = _split(x * x)
    ssq = _dot(jnp.concatenate([hi, lo], axis=0), seg)
    kn = (x * lax.rsqrt((ssq[0:rows] + ssq[rows:2 * rows]) * (1.0 / HEAD_DIM) + EPS) * g_ref[...]).astype(BF16)
    for pr in range(KVW // LANES):
        kn_ref[pr] = kn[:, pr * LANES:(pr + 1) * LANES]
    eye = jnp.where(_iota((KVW, KVW), 0) == _iota((KVW, KVW), 1), 1.0, 0.0).astype(BF16)
    pad = jnp.where(_iota((VROWS - HEAD_DIM, KCH), 0) == 0, 1.0, 0.0).astype(BF16)
    for c in range(rows // KCH):
        sl = slice(c * KCH, (c + 1) * KCH)
        v = jnp.concatenate([zv_ref[0, sl, :], zv_ref[1, sl, :]], axis=1).astype(BF16)
        vt = _dot_nt(eye, v).astype(BF16)
        for g in range(N_KV):
            vt_ref[c, g, 0:HEAD_DIM, :] = vt[g * HEAD_DIM:(g + 1) * HEAD_DIM]
            vt_ref[c, g, HEAD_DIM:VROWS, :] = pad


def _kvprep(zkv, gains, *, B, S, cpb=4):
    nch = S // KCH
    assert nch % cpb == 0
    nblk = nch // cpb
    rows = cpb * KCH
    return pl.pallas_call(
        _kvprep_kernel,
        grid=(2, B, nblk),
        in_specs=[
            pl.BlockSpec((None, KVW // LANES, rows, LANES), lambda w, b, i: (2 + 2 * w, 0, b * nblk + i, 0)),
            pl.BlockSpec((None, KVW // LANES, rows, LANES), lambda w, b, i: (3 + 2 * w, 0, b * nblk + i, 0)),
            pl.BlockSpec((None, 1, KVW), lambda w, b, i: (w, 0, 0)),
        ],
        out_specs=[
            pl.BlockSpec((None, None, KVW // LANES, rows, LANES), lambda w, b, i: (w, b, 0, i, 0)),
            pl.BlockSpec((None, None, cpb, N_KV, VROWS, KCH), lambda w, b, i: (w, b, i, 0, 0, 0)),
        ],
        out_shape=[
            jax.ShapeDtypeStruct((2, B, KVW // LANES, S, LANES), BF16),
            jax.ShapeDtypeStruct((2, B, nch, N_KV, VROWS, KCH), BF16),
        ],
        compiler_params=_params(("parallel", "parallel", "parallel")),
        name="kvprep",
    )(zkv, zkv, gains)


def _compress_kernel(zk_ref, zv_ref, pos_ref, w1_ref, w2_ref, g_ref, kc_ref, vct_ref):
    nch = zk_ref.shape[0] // CMP_STRIDE
    lane = _iota((nch, LANES), 1)
    npair = CMP_STRIDE // 2
    eye = jnp.where(_iota((HEAD_DIM, HEAD_DIM), 0) == _iota((HEAD_DIM, HEAD_DIM), 1), 1.0, 0.0).astype(BF16)
    for kind, z_ref in enumerate((zk_ref, zv_ref)):
        for ge in range(2):
            combs = []
            for p in range(npair):
                a = z_ref[pl.ds(2 * p, nch, stride=CMP_STRIDE), :]
                b = z_ref[pl.ds(2 * p + 1, nch, stride=CMP_STRIDE), :]
                if ge == 0:
                    combs.append(jnp.where(lane < HEAD_DIM, a, pltpu.roll(b, HEAD_DIM, axis=1)))
                else:
                    combs.append(jnp.where(lane < HEAD_DIM, pltpu.roll(a, HEAD_DIM, axis=1), b))
            halves = []
            half_w = npair * LANES
            for half in range(2):
                lhs = jnp.concatenate(
                    [(combs[p] + pos_ref[kind, half * npair + p:half * npair + p + 1, :]).astype(BF16)
                     for p in range(npair)], axis=1)
                halves.append(_dot(lhs, w1_ref[kind, half * half_w:(half + 1) * half_w, :]))
            hid = halves[0] + pltpu.roll(halves[1], nch - 1, axis=0)
            hid = jax.nn.gelu(hid)
            out = _dot(hid.astype(BF16), w2_ref[kind])
            if kind == 0:
                out = out * lax.rsqrt(jnp.mean(out * out, axis=-1, keepdims=True) + EPS) * g_ref[...]
            res = jnp.where(_iota(out.shape, 0) < nch - 1, out, 0.0).astype(BF16)
            if kind == 0:
                kc_ref[ge] = res
            else:
                vct_ref[ge] = _dot_nt(eye, res).astype(BF16)


def _compress(zkv, pos, w1, w2, gain, *, B, S):
    nch = S // CMP_STRIDE
    return pl.pallas_call(
        _compress_kernel,
        grid=(B, KVW // LANES),
        in_specs=[
            pl.BlockSpec((None, None, S, LANES), lambda b, lp: (0, lp, b, 0)),
            pl.BlockSpec((None, None, S, LANES), lambda b, lp: (1, lp, b, 0)),
            _const_spec((2, CMP_STRIDE, LANES)),
            _const_spec((2, CMP_LEN * HEAD_DIM, CMP_HID)),
            _const_spec((2, CMP_HID, HEAD_DIM)),
            _const_spec((1, HEAD_DIM)),
        ],
        out_specs=[
            pl.BlockSpec((None, 2, nch, HEAD_DIM), lambda b, lp: (b, lp, 0, 0)),
            pl.BlockSpec((None, 2, HEAD_DIM, nch), lambda b, lp: (b, lp, 0, 0)),
        ],
        out_shape=[
            jax.ShapeDtypeStruct((B, N_KV, nch, HEAD_DIM), BF16),
            jax.ShapeDtypeStruct((B, N_KV, HEAD_DIM, nch), BF16),
        ],
        compiler_params=_params(("parallel", "parallel")),
        name="compress",
    )(zkv, zkv, pos, w1, w2, gain)


def _spread_heads(x):
    y = x + pltpu.roll(x, 2 * Q_BLK, axis=1)
    return y + pltpu.roll(y, Q_BLK, axis=1)


def _swap_heads(x):
    n = x.shape[0]
    lane = _iota((n, LANES), 1)
    halves = []
    for pr in range(x.shape[1] // LANES):
        slab = jnp.concatenate([x[:, pr * LANES:(pr + 1) * LANES], jnp.zeros((LANES -n, LANES), x.dtype)], axis=0)
        tr = slab.T
        halves.append(jnp.where(lane < n, tr[0:n], pltpu.roll(tr[n:2 * n], n, axis=1)))
    return jnp.concatenate(halves, axis=1)


def _nsa_kernel(*refs, nsel, qpb):
    per_chunk = KCH // SEL_LEN
    for k in range(qpb):
        _nsa_block(k, "head", None, *refs, nsel=nsel, qpb=qpb)

    def step(i, carry):
        for k in range(qpb):
            _nsa_block(k, "step", i, *refs, nsel=nsel, qpb=qpb)
        return carry

    common = pl.program_id(1) * (qpb // per_chunk)
    lax.fori_loop(0, common, step, 0)
    for k in range(qpb):
        for extra in range(k // per_chunk):
            _nsa_block(k, "step", common + extra, *refs, nsel=nsel, qpb=qpb)
    for k in range(qpb):
        _nsa_block(k, "tail", None, *refs, nsel=nsel, qpb=qpb)


def _nsa_block(k, phase, step_i, zq_ref, qg_ref, kc_ref, vct_ref, ks_ref, kw_ref, vst_ref, vwt_ref, ng_ref, o_ref,
               qtp_ref, oc_ref, ow_ref, bias_ref, ms_ref, accs_ref, mw_ref, accw_ref, sbuf_ref, wbuf_ref,
               gates_ref, wbias_ref, *, nsel, qpb):
    qtp_ref, oc_ref, ow_ref, bias_ref, ms_ref, accs_ref, sbuf_ref, gates_ref = (
        r.at[k] for r in (qtp_ref, oc_ref, ow_ref, bias_ref, ms_ref, accs_ref, sbuf_ref, gates_ref))
    qi = pl.program_id(1) * qpb + k
    q_rows = pl.ds(k * Q_BLK, Q_BLK)
    s0 = qi * Q_BLK
    cur = qi
    ncmp = kc_ref.shape[1]
    per_chunk = KCH // SEL_LEN
    groups = range(N_KV)
    n_chunks = qi // per_chunk + 1
    cur_slab = k % per_chunk
    cut_rows = slice(cur_slab * SEL_LEN, (cur_slab + 1) * SEL_LEN)
    last_slabs = per_chunk // 2 if cur_slab < per_chunk // 2 else per_chunk

    def scores(k_ref, i):
        k0 = pl.multiple_of(i * KCH, KCH)
        return [_dot(k_ref[g // 2, pl.ds(k0, KCH), :], qtp_ref[g]) for g in groups]

    def park(buf_ref, vals):
        for g in groups:
            buf_ref[g] = vals[g]

    def cut(buf_ref, tri):
        for g in groups:
            buf_ref[g, cut_rows, :] = buf_ref[g, cut_rows, :] + tri

    def update(g, blocks, shifts, vt, m_ref, acc_ref):
        m = m_ref[g]
        part = None
        for blk, sh in zip(blocks, shifts):
            top8 = jnp.max(blk.reshape(SEL_LEN // 8, 8, GW), axis=0) + sh
            part = top8 if part is None else jnp.maximum(part, top8)
        mnew = jnp.maximum(m, jnp.max(part, axis=0, keepdims=True))
        e = jnp.concatenate([jnp.exp2(blk + (sh - mnew)) for blk, sh in zip(blocks, shifts)], axis=0).astype(BF16)
        acc_ref[g] = jnp.exp2(m - mnew) * acc_ref[g] + _dot(vt, e)
        m_ref[g] = mnew

    def slabs(buf_ref, g, ns):
        return [buf_ref[g, nb * SEL_LEN:(nb + 1) * SEL_LEN, :] for nb in range(ns)]

    def sel_softmax(i, ns=per_chunk):
        for g in groups:
            shifts = [bias_ref[g, pl.ds(i * per_chunk + nb, 1), :] for nb in range(ns)]
            update(g, slabs(sbuf_ref, g, ns), shifts, vst_ref[i, g, :, 0:ns * SEL_LEN], ms_ref, accs_ref)

    key_row = _iota((SEL_LEN, GW), 0)
    in_blk_t = _iota((SEL_LEN, GW), 1) % Q_BLK
    causal_tri = jnp.where(key_row <= in_blk_t, 0.0, NEG)

    if phase == "step":
        nxt = scores(ks_ref, step_i + 1)
        sel_softmax(step_i)
        park(sbuf_ref, nxt)
        return

    if phase == "tail":
        cut(sbuf_ref, causal_tri)
        sel_softmax(n_chunks - 1, last_slabs)
        outs = []
        for g in groups:
            o_s = accs_ref[g, 0:HEAD_DIM, :] * (1.0 / jnp.maximum(accs_ref[g, HEAD_DIM:HEAD_DIM + 1, :], 1e-30))
            gate = [gates_ref[g, c:c + 1, :] for c in range(3)]
            o_t = gate[0] * oc_ref[g] + gate[1] * o_s + gate[2] * ow_ref[g]
            outs.append(_swap_heads(o_t))
        for g in groups:
            o_ref[g, q_rows, :] = outs[g].astype(o_ref.dtype)
        return

    lane_t = s0 + (_iota((1, GW), 1) % Q_BLK)
    lane_grp = _iota((1, GW), 1) // Q_BLK
    ov_n = _iota((nsel, ncmp), 0) * SEL_LEN
    ov_c = _iota((nsel, ncmp), 1) * CMP_STRIDE
    overlap = jnp.where((ov_c < ov_n + SEL_LEN) & (ov_c + CMP_LEN > ov_n), 1.0, 0.0).astype(BF16)


    ok_c = _iota((ncmp, GW), 0) * CMP_STRIDE + (CMP_LEN - 1) <= lane_t
    ngt = jnp.concatenate([ng_ref[q_rows, :], jnp.zeros((LANES -Q_BLK, NG_PAD), F32)], axis=0).T
    ngt_hi = pltpu.roll(ngt, Q_BLK, axis=1)
    low_half = _iota((1, LANES), 1) < Q_BLK
    for g in groups:
        for c in range(3):
            r = c * N_HEADS + g * HPG
            pairs = [jnp.where(low_half, ngt[r + j:r + j + 1], ngt_hi[r + j + 1:r + j + 2]) for j in (0, 2)]
            gates_ref[g, c:c + 1, :] = jnp.concatenate(pairs, axis=1)

    qts = []
    for g in groups:
        xt = _swap_heads(zq_ref[g, q_rows, :])
        ssq = jnp.sum(xt * xt, axis=0, keepdims=True)
        qts.append((xt * lax.rsqrt(ssq * (1.0 / HEAD_DIM) + EPS) * qg_ref[...]).astype(BF16))
    scs = []
    for g in groups:
        off = (g % 2) * HEAD_DIM
        qtp_ref[g, off:off + HEAD_DIM, :] = qts[g]
        qtp_ref[g, HEAD_DIM - off:2 * HEAD_DIM - off, :] = jnp.zeros((HEAD_DIM, GW), BF16)
        scs.append(_dot(kc_ref[g], qts[g]))

    n_win = WINDOW // KCH + 1
    win_ids = [jnp.maximum(n_chunks - (n_win - j), 0) for j in range(n_win)]
    win_dead = [jnp.where(n_chunks >= n_win - j, 0.0, NEG) for j in range(n_win)]
    park(sbuf_ref, scores(ks_ref, 0))
    park(wbuf_ref, scores(kw_ref, win_ids[0]))

    expired_tri = jnp.where(key_row > in_blk_t, 0.0, NEG)
    win_blocks = WINDOW // SEL_LEN
    n_all = _iota((nsel, GW), 0)
    wbias_ref[...] = jnp.where((n_all >= cur - win_blocks) & (n_all <= cur), 0.0, NEG)
    cut(wbuf_ref, jnp.where(cur >= win_blocks, expired_tri, 0.0))

    ms_ref[...] = jnp.full(ms_ref.shape, 0.5 * NEG, F32)
    mw_ref[...] = jnp.full(mw_ref.shape, 0.5 * NEG, F32)
    accs_ref[...] = jnp.zeros(accs_ref.shape, F32)
    accw_ref[...] = jnp.zeros(accw_ref.shape, F32)

    def win_softmax(i, dead, lo=0, hi=per_chunk):
        shifts = [wbias_ref[pl.ds(i * per_chunk + nb, 1), :] + dead for nb in range(lo, hi)]
        for g in groups:
            blocks = [wbuf_ref[g, nb * SEL_LEN:(nb + 1) * SEL_LEN, :] for nb in range(lo, hi)]
            update(g, blocks, shifts, vwt_ref[i, g, :, lo * SEL_LEN:hi * SEL_LEN], mw_ref, accw_ref)

    first_lo = per_chunk // 2 if cur_slab >= per_chunk // 2 else 0
    for j in range(n_win - 1):
        nxt = scores(kw_ref, win_ids[j + 1])
        win_softmax(win_ids[j], win_dead[j], lo=first_lo if j == 0 else 0)
        park(wbuf_ref, nxt)

    imp_all = jnp.zeros((nsel, GW), F32)
    some_c = lane_t >= CMP_LEN - 1
    ones_rows = jnp.ones((16, ncmp), BF16)
    for g in groups:
        scm = jnp.where(ok_c, scs[g], NEG)
        ec = jnp.exp2(scm - jnp.max(scm, axis=0, keepdims=True))
        both = _dot(jnp.concatenate([vct_ref[g], overlap, ones_rows], axis=0), ec.astype(BF16))
        denom = both[HEAD_DIM + nsel:HEAD_DIM + nsel + 1]
        both = both[0:HEAD_DIM + nsel] * jnp.where(some_c, 1.0 / jnp.maximum(denom, 1e-30), 0.0)
        oc_ref[g] = both[0:HEAD_DIM]
        imp = _spread_heads(both[HEAD_DIM:HEAD_DIM + nsel])
        imp_all = jnp.where(lane_grp == g, imp, imp_all)

    n_io = _iota((nsel, GW), 0)
    forced = (n_io == 0) | (n_io == cur) | (n_io == cur - 1)
    valid = n_io <= cur
    score = jnp.where(forced, FORCE_SCORE, imp_all)
    score = jnp.where(valid, score, -1.0)
    rows8 = [score[r:r + 8] for r in range(0, nsel, 8)]
    ranks = [jnp.zeros((8, GW), F32) for _ in rows8]
    sub = _iota((8, GW), 0)
    for m in range(nsel):
        rowv = score[m:m + 1, :]
        for k, blk in enumerate(rows8):
            ge = jnp.where(rowv >= blk, 1.0, 0.0)
            gt = jnp.where(rowv > blk, 1.0, 0.0)
            if 8 * k > m:
                inc = ge
            elif 8 * k + 7 < m:
                inc = gt
            else:
                inc = jnp.where(sub + 8 * k > m, ge, gt)
            ranks[k] = ranks[k] + inc
    rank = jnp.concatenate(ranks, axis=0)
    sel_all = jnp.where((rank < float(min(SEL_TOPN, nsel))) & valid, 1.0, 0.0)
    for g in groups:
        mine = jnp.where(lane_grp == g, sel_all, 0.0)
        bias_ref[g] = (_spread_heads(mine) - 1.0) * (-NEG)

    cut(wbuf_ref, causal_tri)
    win_softmax(win_ids[n_win - 1], win_dead[n_win - 1], hi=last_slabs)
    for g in groups:
        ow_ref[g] = accw_ref[g, 0:HEAD_DIM, :] * (1.0 / jnp.maximum(accw_ref[g, HEAD_DIM:HEAD_DIM + 1, :], 1e-30))


def _nsa(zq, qgain, kc, vct, kn, vt, ng, *, B, S):
    T = B * S
    qpb = 2 * (KCH // SEL_LEN)
    assert S % (Q_BLK * qpb) == 0
    nq = S // (Q_BLK * qpb)
    nch = S // KCH
    ncmp = S // CMP_STRIDE
    nsel = S // SEL_LEN
    tok = lambda b, i: (0, b * nq + i, 0)
    return pl.pallas_call(
        functools.partial(_nsa_kernel, nsel=nsel, qpb=qpb),
        grid=(B, nq),
        in_specs=[
            pl.BlockSpec((N_KV, Q_BLK * qpb, GW), tok),
            _const_spec((HEAD_DIM, GW)),
            pl.BlockSpec((None, N_KV, ncmp, HEAD_DIM), lambda b, i: (b, 0, 0, 0)),
            pl.BlockSpec((None, N_KV, HEAD_DIM, ncmp), lambda b, i: (b, 0, 0, 0)),
            pl.BlockSpec((None, None, 2, S, LANES), lambda b, i: (0, b, 0, 0, 0)),
            pl.BlockSpec((None, None, 2, S, LANES), lambda b, i: (1, b, 0, 0, 0)),
            pl.BlockSpec((None, None, nch, N_KV, VROWS, KCH), lambda b, i: (0, b, 0, 0, 0, 0)),
            pl.BlockSpec((None, None, nch, N_KV, VROWS, KCH), lambda b, i: (1, b, 0, 0, 0, 0)),
            pl.BlockSpec((Q_BLK * qpb, NG_PAD), lambda b, i: (b * nq + i, 0)),
        ],
        out_specs=pl.BlockSpec((N_KV, Q_BLK * qpb, GW), tok),
        out_shape=jax.ShapeDtypeStruct((N_KV, T, GW), BF16),
        scratch_shapes=[
            pltpu.VMEM((qpb, N_KV, 2 * HEAD_DIM, GW), BF16),
            pltpu.VMEM((qpb, N_KV, HEAD_DIM, GW), F32),
            pltpu.VMEM((qpb, N_KV, HEAD_DIM, GW), F32),
            pltpu.VMEM((qpb, N_KV, nsel, GW), F32),
            pltpu.VMEM((qpb, N_KV, 1, GW), F32),
            pltpu.VMEM((qpb, N_KV, VROWS, GW), F32),
            pltpu.VMEM((N_KV, 1, GW), F32),
            pltpu.VMEM((N_KV, VROWS, GW), F32),
            pltpu.VMEM((qpb, N_KV, KCH, GW), F32),
            pltpu.VMEM((N_KV, KCH, GW), F32),
            pltpu.VMEM((qpb, N_KV, 8, GW), F32),
            pltpu.VMEM((nsel, GW), F32),
        ],
        compiler_params=_params(("parallel", "arbitrary")),
        name="nsa",
    )(zq, qgain, kc, vct, kn, kn, vt, vt, ng)


def _merge_ffn_kernel(x_ref, ya_ref, o_ref_in, mg_ref, wn_ref, wo_ref, g2_ref, wg_ref, wu_ref, wd_ref,
                      out_ref, h_ref, *, tf):
    yb = _dot(o_ref_in[0], wn_ref[0:GW, :])
    for g in range(1, N_KV):
        yb = yb + _dot(o_ref_in[g], wn_ref[g * GW:(g + 1) * GW, :])
    tm = x_ref.shape[0]
    ya = jnp.concatenate([ya_ref[ck].reshape(tm, LANES) for ck in range(D_MODEL // LANES)], axis=1)
    mixed = (mg_ref[:, 0:D_MODEL].astype(F32) * ya
             + mg_ref[:, D_MODEL:2 * D_MODEL].astype(F32) * yb)
    x1 = x_ref[...] + _dot(mixed.astype(BF16), wo_ref[...])
    y = x1 * lax.rsqrt(jnp.mean(x1 * x1, axis=-1, keepdims=True) + EPS) * g2_ref[...]
    h_ref[...] = y.astype(BF16)
    out_ref[...] = x1
    for j in range(D_FF // tf):
        h = h_ref[...]
        sl = slice(j * tf, (j + 1) * tf)
        act = jax.nn.silu(_dot(h, wg_ref[:, sl])) * _dot(h, wu_ref[:, sl])
        out_ref[...] += _dot(act.astype(BF16), wd_ref[sl, :])


def _merge_ffn(x2d, ya, o, mg, wn, wo, gain2, wg, wu, wd, *, B, S, tm=512, tf=256):
    assert S % tm == 0 and D_FF % tf == 0
    ni = S // tm
    tok = lambda b, i: (b * ni + i, 0)
    once = lambda shape: pl.BlockSpec(shape, lambda b, i: (0,) * len(shape), pipeline_mode=pl.Buffered(1))
    return pl.pallas_call(
        functools.partial(_merge_ffn_kernel, tf=tf),
        grid=(B, ni),
        in_specs=[
            pl.BlockSpec((tm, D_MODEL), tok),
            pl.BlockSpec((D_MODEL // LANES, tm // 8, None, 8, LANES), lambda b, i: (0, i, b, 0, 0)),
            pl.BlockSpec((N_KV, tm, GW), lambda b, i: (0, b * ni + i, 0)),
            pl.BlockSpec((tm, 2 * D_MODEL), tok),
            once((N_HEADS * HEAD_DIM, D_MODEL)),
            once((D_MODEL, D_MODEL)),
            _const_spec((1, D_MODEL)),
            once((D_MODEL, D_FF)),
            once((D_MODEL, D_FF)),
            once((D_FF, D_MODEL)),
        ],
        out_specs=pl.BlockSpec((tm, D_MODEL), tok),
        out_shape=jax.ShapeDtypeStruct((B * S, D_MODEL), F32),
        scratch_shapes=[pltpu.VMEM((tm, D_MODEL), BF16)],
        compiler_params=_params(("parallel", "parallel")),
        name="merge_ffn",
    )(x2d, ya, o, mg, wn, wo, gain2, wg, wu, wd)


def _layer(x, p):
    B, S, _ = x.shape
    T = B * S
    dh = HEAD_DIM
    row = lambda v: v.reshape(1, -1)
    w_in = p["w_in"]
    n_ng = 3 * N_HEADS
    o_ng = 2 * D_RNN + N_HEADS * dh + 6 * KVW
    w_packed = jnp.concatenate(
        [w_in[:, :o_ng], jnp.pad(w_in[:, o_ng:o_ng + n_ng], ((0, 0), (0, NG_PAD - n_ng))), w_in[:, o_ng + n_ng:]],
        axis=1).astype(BF16)
    rx, gy, zq, zkv, ng, mg = _inproj(x, row(p["norm1"]), w_packed)

    slabs = lambda a: a.reshape(a.shape[0], S * B, LANES)
    ya = _rglru(slabs(rx), slabs(gy), p["conv_w"], row(p["conv_b"]),
                p["rg_wa"].astype(BF16), row(p["rg_ba"]), p["rg_wi"].astype(BF16), row(p["rg_bi"]),
                row(p["rg_lambda"]), p["w_rg_out"].astype(BF16), nb=B)
    ya = ya.reshape(D_MODEL // LANES, S // 8, B, 8, LANES)

    kgains = jnp.tile(p["k_norm"][1:3], (1, N_KV)).reshape(2, 1, KVW)
    kn, vt = _kvprep(zkv, kgains, B=B, S=S)
    pos = jnp.stack([p["cmp_pos_k"], p["cmp_pos_v"]]).reshape(2, CMP_STRIDE, 2 * dh)
    w1 = jnp.stack([p["cmp_k_w1"], p["cmp_v_w1"]]).astype(BF16)
    w2 = jnp.stack([p["cmp_k_w2"], p["cmp_v_w2"]]).astype(BF16)
    kc, vct = _compress(zkv, pos, w1, w2, row(p["k_norm"][0]), B=B, S=S)
    qgain = jnp.broadcast_to((p["q_norm"] * (dh ** -0.5 * LOG2E))[:, None], (dh, GW))
    o = _nsa(zq, qgain, kc, vct, kn, vt, ng, B=B, S=S)

    x2 = _merge_ffn(x.reshape(T, D_MODEL), ya, o, mg, p["w_nsa_out"].astype(BF16), p["w_o"].astype(BF16),
                    row(p["norm2"]), p["w_gate"].astype(BF16), p["w_up"].astype(BF16), p["w_down"].astype(BF16),
                    B=B, S=S)
    return x2.reshape(B, S, D_MODEL)


def kernel(x, norm1, w_in, conv_w, conv_b, rg_wa, rg_ba, rg_wi, rg_bi, rg_lambda, q_norm, k_norm, cmp_pos_k,
           cmp_pos_v, cmp_k_w1, cmp_k_w2, cmp_v_w1, cmp_v_w2, w_rg_out, w_nsa_out, w_o, norm2, w_gate, w_up, w_down):
    params = dict(norm1=norm1, w_in=w_in, conv_w=conv_w, conv_b=conv_b, rg_wa=rg_wa, rg_ba=rg_ba, rg_wi=rg_wi,
                  rg_bi=rg_bi, rg_lambda=rg_lambda, q_norm=q_norm, k_norm=k_norm, cmp_pos_k=cmp_pos_k,
                  cmp_pos_v=cmp_pos_v, cmp_k_w1=cmp_k_w1, cmp_k_w2=cmp_k_w2, cmp_v_w1=cmp_v_w1, cmp_v_w2=cmp_v_w2,
                  w_rg_out=w_rg_out, w_nsa_out=w_nsa_out, w_o=w_o, norm2=norm2, w_gate=w_gate, w_up=w_up,
                  w_down=w_down)
    for l in range(norm1.shape[0]):
        x = _layer(x, {k: v[l] for k, v in params.items()})
    return x
```

```python
import functools

import jax
import jax.numpy as jnp
from jax import lax
from jax.experimental import pallas as pl
from jax.experimental.pallas import tpu as pltpu

D_MODEL = 1024
D_RNN = 1024
RG_BLOCKS = 4
RG_BW = D_RNN // RG_BLOCKS
CONV_W = 4
RG_C = 8.0
N_HEADS = 16
N_KV = 4
HEAD_DIM = 64
HPG = N_HEADS // N_KV
CMP_LEN = 32
CMP_STRIDE = 16
CMP_HID = 256
SEL_LEN = 64
SEL_TOPN = 16
WINDOW = 512
Q_BLK = 64
D_FF = 2816
EPS = 1e-6
FORCE_SCORE = 1e6
NEG = -1e30
LOG2E = 1.4426950408889634

LANES = 128
KCH = 256
GW = HPG * HEAD_DIM
KVW = N_KV * HEAD_DIM
VROWS = 80
NG_PAD = LANES

F32 = jnp.float32
BF16 = jnp.bfloat16

VMEM_LIMIT = 56 * 1024 * 1024


def _params(sem):
    return pltpu.CompilerParams(dimension_semantics=sem, vmem_limit_bytes=VMEM_LIMIT)


def _dot(a, b):
    return jnp.dot(a, b, preferred_element_type=F32)


def _dot_nt(a, b):
    return lax.dot_general(a, b, (((1,), (1,)), ((), ())), preferred_element_type=F32)


def _split(x):
    hi = x.astype(BF16)
    lo = (x - hi.astype(F32)).astype(BF16)
    return hi, lo


def _iota(shape, dim):
    return lax.broadcasted_iota(jnp.int32, shape, dim)


def _const_spec(shape):
    return pl.BlockSpec(shape, lambda *_: (0,) * len(shape))


_O_RY = D_RNN
_O_Q = 2 * D_RNN
_O_KV = _O_Q + N_HEADS * HEAD_DIM
_O_NG = _O_KV + 6 * KVW
_O_MG = _O_NG + NG_PAD
_W_COLS = _O_MG + 2 * D_MODEL


def _inproj_kernel(x_ref, g_ref, w_ref, rx_ref, gy_ref, zq_ref, zkv_ref, ng_ref, mg_ref, h_ref):
    x = x_ref[...]
    y = x * lax.rsqrt(jnp.mean(x * x, axis=-1, keepdims=True) + EPS) * g_ref[...]
    h_ref[...] = y.astype(BF16)

    def proj(c0, width):
        return _dot(h_ref[...], w_ref[:, c0:c0 + width])

    cw = 256
    tm = x_ref.shape[0]
    for c in range(D_RNN // cw):
        rx = proj(c * cw, cw)
        gy = jax.nn.gelu(proj(_O_RY + c * cw, cw))
        for hf in range(cw // LANES):
            rx_ref[c * (cw // LANES) + hf] = rx[:, hf * LANES:(hf + 1) * LANES].reshape(tm // 8, 8, LANES)
            gy_ref[c * (cw // LANES) + hf] = gy[:, hf * LANES:(hf + 1) * LANES].reshape(tm // 8, 8, LANES)
    for g in range(N_KV):
        zq_ref[g] = proj(_O_Q + g * GW, GW)
    for s in range(6):
        z = proj(_O_KV + s * KVW, KVW)
        for pr in range(KVW // LANES):
            zkv_ref[s, pr] = z[:, pr * LANES:(pr + 1) * LANES]
    ng_ref[...] = jax.nn.sigmoid(proj(_O_NG, NG_PAD))
    for c in range(2 * D_MODEL // cw):
        mg_ref[:, c * cw:(c + 1) * cw] = jax.nn.sigmoid(proj(_O_MG + c * cw, cw)).astype(mg_ref.dtype)


def _inproj(x, gain, w, *, tm=256):
    B, S, D = x.shape
    T = B * S
    ni = S // tm
    nck = D_RNN // LANES
    tok = lambda b, i: (b * ni + i, 0)
    slab_spec = pl.BlockSpec((nck, tm // 8, None, 8, LANES), lambda b, i: (0, i, b, 0, 0))
    slab_shape = jax.ShapeDtypeStruct((nck, S // 8, B, 8, LANES), F32)
    return pl.pallas_call(
        _inproj_kernel,
        grid=(B, ni),
        in_specs=[
            pl.BlockSpec((None, tm, D), lambda b, i: (b, i, 0)),
            _const_spec((1, D)),
            pl.BlockSpec((D, _W_COLS), lambda b, i: (0, 0), pipeline_mode=pl.Buffered(1)),
        ],
        out_specs=[
            slab_spec,
            slab_spec,
            pl.BlockSpec((N_KV, tm, GW), lambda b, i: (0, b * ni + i, 0)),
            pl.BlockSpec((6, KVW // LANES, tm, LANES), lambda b, i: (0, 0, b * ni + i, 0)),
            pl.BlockSpec((tm, NG_PAD), tok),
            pl.BlockSpec((tm, 2 * D_MODEL), tok),
        ],
        out_shape=[
            slab_shape,
            slab_shape,
            jax.ShapeDtypeStruct((N_KV, T, GW), F32),
            jax.ShapeDtypeStruct((6, KVW // LANES, T, LANES), F32),
            jax.ShapeDtypeStruct((T, NG_PAD), F32),
            jax.ShapeDtypeStruct((T, 2 * D_MODEL), BF16),
        ],
        scratch_shapes=[pltpu.VMEM((tm, D), BF16)],
        compiler_params=_params(("parallel", "parallel")),
        name="inproj",
    )(x, gain, w)


def _rglru_kernel(x_ref, gy_ref, cw_ref, cb_ref, wa_ref, ba_ref, wi_ref, bi_ref, lam_ref, wo_ref,
                  o_ref, xb_ref, gyb_ref, a_ref, u_ref, hb_ref, hs_ref, yb_ref, *, nb, tt):
    R = nb * tt
    halo = (CONV_W - 1) * nb
    step = pl.program_id(0)
    nck = D_RNN // LANES

    @pl.when(step == 0)
    def _():
        xb_ref[0:halo, :] = jnp.zeros((halo, D_RNN), F32)
        hs_ref[...] = jnp.zeros((nb, D_RNN), F32)

    def slab_rows(t8, t_lo):
        return pl.ds(t8 * (8 * nb) + t_lo, nb, stride=8)

    def stage_in(t8, carry):
        for t_lo in range(8):
            r0 = pl.multiple_of((t8 * 8 + t_lo) * nb, nb)
            for ck in range(nck):
                cols = slice(ck * LANES, (ck + 1) * LANES)
                xb_ref[pl.ds(halo + r0, nb), cols] = x_ref[ck, slab_rows(t8, t_lo), :]
                gyb_ref[pl.ds(r0, nb), cols] = gy_ref[ck, slab_rows(t8, t_lo), :]
        return carry

    lax.fori_loop(0, tt // 8, stage_in, 0)
    xr = cb_ref[...] + cw_ref[0:1, :] * xb_ref[0:R, :]
    for k in range(1, CONV_W):
        xr = xr + cw_ref[k:k + 1, :] * xb_ref[k * nb:k * nb + R, :]
    tail = xb_ref[R:R + halo, :]
    xb_ref[0:halo, :] = tail

    xrb = xr.astype(BF16)
    decay_log2 = (-RG_C * LOG2E) * jax.nn.softplus(-lam_ref[...])
    row = _iota((R, RG_BW), 0)
    first = (row < nb) & (step == 0)
    for n in range(RG_BLOCKS):
        sl = slice(n * RG_BW, (n + 1) * RG_BW)
        xn = xrb[:, sl]
        r = jax.nn.sigmoid(_dot(xn, wa_ref[n]) + ba_ref[:, sl])
        ig = jax.nn.sigmoid(_dot(xn, wi_ref[n]) + bi_ref[:, sl])
        a = jnp.exp2(r * decay_log2[:, sl])
        mult = jnp.sqrt(1.0 - a * a)
        mult = jnp.where(first, 1.0, mult)
        a_ref[:, sl] = a
        u_ref[:, sl] = mult * ig * xr[:, sl]

    def body(t, h):
        r0 = pl.multiple_of(t * nb, nb)
        h = a_ref[pl.ds(r0, nb), :] * h + u_ref[pl.ds(r0, nb), :]
        hb_ref[pl.ds(r0, nb), :] = h
        return h

    h = lax.fori_loop(0, tt, body, hs_ref[...], unroll=8)
    hs_ref[...] = h
    yb_ref[...] = _dot((hb_ref[...] * gyb_ref[...]).astype(BF16), wo_ref[...])

    def stage_out(t8, carry):
        for t_lo in range(8):
            r0 = pl.multiple_of((t8 * 8 + t_lo) * nb, nb)
            for ck in range(nck):
                o_ref[ck, slab_rows(t8, t_lo), :] = yb_ref[pl.ds(r0, nb), ck * LANES:(ck + 1) * LANES]
        return carry

    lax.fori_loop(0, tt // 8, stage_out, 0)


def _rglru(rx, gy, conv_w, conv_b, wa, ba, wi, bi, lam, wo, *, nb, tt=64):
    nck, rows, _ = rx.shape
    S = rows // nb
    assert S % tt == 0 and tt % 8 == 0
    R = nb * tt
    halo = (CONV_W - 1) * nb
    return pl.pallas_call(
        functools.partial(_rglru_kernel, nb=nb, tt=tt),
        grid=(S // tt,),
        in_specs=[
            pl.BlockSpec((nck, R, LANES), lambda s: (0, s, 0)),
            pl.BlockSpec((nck, R, LANES), lambda s: (0, s, 0)),
            _const_spec((CONV_W, D_RNN)),
            _const_spec((1, D_RNN)),
            _const_spec((RG_BLOCKS, RG_BW, RG_BW)),
            _const_spec((1, D_RNN)),
            _const_spec((RG_BLOCKS, RG_BW, RG_BW)),
            _const_spec((1, D_RNN)),
            _const_spec((1, D_RNN)),
            _const_spec((D_RNN, D_MODEL)),
        ],
        out_specs=pl.BlockSpec((D_MODEL // LANES, R, LANES), lambda s: (0, s, 0)),
        out_shape=jax.ShapeDtypeStruct((D_MODEL // LANES, rows, LANES), F32),
        scratch_shapes=[
            pltpu.VMEM((R + halo, D_RNN), F32),
            pltpu.VMEM((R, D_RNN), F32),
            pltpu.VMEM((R, D_RNN), F32),
            pltpu.VMEM((R, D_RNN), F32),
            pltpu.VMEM((R, D_RNN), F32),
            pltpu.VMEM((nb, D_RNN), F32),
            pltpu.VMEM((R, D_MODEL), F32),
        ],
        compiler_params=_params(("arbitrary",)),
        name="rglru",
    )(rx, gy, conv_w, conv_b, wa, ba, wi, bi, lam, wo)


def _kvprep_kernel(zk_ref, zv_ref, g_ref, kn_ref, vt_ref):
    rows = zk_ref.shape[1]
    x = jnp.concatenate([zk_ref[0], zk_ref[1]], axis=1)
    seg = jnp.where(_iota((KVW, KVW), 0) // HEAD_DIM == _iota((KVW, KVW), 1) // HEAD_DIM, 1.0, 0.0).astype(BF16)
    hi, lo = _split(x * x)
    ssq = _dot(jnp.concatenate([hi, lo], axis=0), seg)
    kn = (x * lax.rsqrt((ssq[0:rows] + ssq[rows:2 * rows]) * (1.0 / HEAD_DIM) + EPS) * g_ref[...]).astype(BF16)
    for pr in range(KVW // LANES):
        kn_ref[pr] = kn[:, pr * LANES:(pr + 1) * LANES]
    eye = jnp.where(_iota((KVW, KVW), 0) == _iota((KVW, KVW), 1), 1.0, 0.0).astype(BF16)
    pad = jnp.where(_iota((VROWS - HEAD_DIM, KCH), 0) == 0, 1.0, 0.0).astype(BF16)
    for c in range(rows // KCH):
        sl = slice(c * KCH, (c + 1) * KCH)
        v = jnp.concatenate([zv_ref[0, sl, :], zv_ref[1, sl, :]], axis=1).astype(BF16)
        vt = _dot_nt(eye, v).astype(BF16)
        for g in range(N_KV):
            vt_ref[c, g, 0:HEAD_DIM, :] = vt[g * HEAD_DIM:(g + 1) * HEAD_DIM]
            vt_ref[c, g, HEAD_DIM:VROWS, :] = pad


def _kvprep(zkv, gains, *, B, S, cpb=4):
    nch = S // KCH
    assert nch % cpb == 0
    nblk = nch // cpb
    rows = cpb * KCH
    return pl.pallas_call(
        _kvprep_kernel,
        grid=(2, B, nblk),
        in_specs=[
            pl.BlockSpec((None, KVW // LANES, rows, LANES), lambda w, b, i: (2 + 2 * w, 0, b * nblk + i, 0)),
            pl.BlockSpec((None, KVW // LANES, rows, LANES), lambda w, b, i: (3 + 2 * w, 0, b * nblk + i, 0)),
            pl.BlockSpec((None, 1, KVW), lambda w, b, i: (w, 0, 0)),
        ],
        out_specs=[
            pl.BlockSpec((None, None, KVW // LANES, rows, LANES), lambda w, b, i: (w, b, 0, i, 0)),
            pl.BlockSpec((None, None, cpb, N_KV, VROWS, KCH), lambda w, b, i: (w, b, i, 0, 0, 0)),
        ],
        out_shape=[
            jax.ShapeDtypeStruct((2, B, KVW // LANES, S, LANES), BF16),
            jax.ShapeDtypeStruct((2, B, nch, N_KV, VROWS, KCH), BF16),
        ],
        compiler_params=_params(("parallel", "parallel", "parallel")),
        name="kvprep",
    )(zkv, zkv, gains)


def _compress_kernel(zk_ref, zv_ref, pos_ref, w1_ref, w2_ref, g_ref, kc_ref, vct_ref):
    nch = zk_ref.shape[0] // CMP_STRIDE
    lane = _iota((nch, LANES), 1)
    npair = CMP_STRIDE // 2
    eye = jnp.where(_iota((HEAD_DIM, HEAD_DIM), 0) == _iota((HEAD_DIM, HEAD_DIM), 1), 1.0, 0.0).astype(BF16)
    for kind, z_ref in enumerate((zk_ref, zv_ref)):
        for ge in range(2):
            combs = []
            for p in range(npair):
                a = z_ref[pl.ds(2 * p, nch, stride=CMP_STRIDE), :]
                b = z_ref[pl.ds(2 * p + 1, nch, stride=CMP_STRIDE), :]
                if ge == 0:
                    combs.append(jnp.where(lane < HEAD_DIM, a, pltpu.roll(b, HEAD_DIM, axis=1)))
                else:
                    combs.append(jnp.where(lane < HEAD_DIM, pltpu.roll(a, HEAD_DIM, axis=1), b))
            halves = []
            half_w = npair * LANES
            for half in range(2):
                lhs = jnp.concatenate(
                    [(combs[p] + pos_ref[kind, half * npair + p:half * npair + p + 1, :]).astype(BF16)
                     for p in range(npair)], axis=1)
                halves.append(_dot(lhs, w1_ref[kind, half * half_w:(half + 1) * half_w, :]))
            hid = halves[0] + pltpu.roll(halves[1], nch - 1, axis=0)
            hid = jax.nn.gelu(hid)
            out = _dot(hid.astype(BF16), w2_ref[kind])
            if kind == 0:
                out = out * lax.rsqrt(jnp.mean(out * out, axis=-1, keepdims=True) + EPS) * g_ref[...]
            res = jnp.where(_iota(out.shape, 0) < nch - 1, out, 0.0).astype(BF16)
            if kind == 0:
                kc_ref[ge] = res
            else:
                vct_ref[ge] = _dot_nt(eye, res).astype(BF16)


def _compress(zkv, pos, w1, w2, gain, *, B, S):
    nch = S // CMP_STRIDE
    return pl.pallas_call(
        _compress_kernel,
        grid=(B, KVW // LANES),
        in_specs=[
            pl.BlockSpec((None, None, S, LANES), lambda b, lp: (0, lp, b, 0)),
            pl.BlockSpec((None, None, S, LANES), lambda b, lp: (1, lp, b, 0)),
            _const_spec((2, CMP_STRIDE, LANES)),
            _const_spec((2, CMP_LEN * HEAD_DIM, CMP_HID)),
            _const_spec((2, CMP_HID, HEAD_DIM)),
            _const_spec((1, HEAD_DIM)),
        ],
        out_specs=[
            pl.BlockSpec((None, 2, nch, HEAD_DIM), lambda b, lp: (b, lp, 0, 0)),
            pl.BlockSpec((None, 2, HEAD_DIM, nch), lambda b, lp: (b, lp, 0, 0)),
        ],
        out_shape=[
            jax.ShapeDtypeStruct((B, N_KV, nch, HEAD_DIM), BF16),
            jax.ShapeDtypeStruct((B, N_KV, HEAD_DIM, nch), BF16),
        ],
        compiler_params=_params(("parallel", "parallel")),
        name="compress",
    )(zkv, zkv, pos, w1, w2, gain)


def _spread_heads(x):
    y = x + pltpu.roll(x, 2 * Q_BLK, axis=1)
    return y + pltpu.roll(y, Q_BLK, axis=1)


def _swap_heads(x):
    n = x.shape[0]
    lane = _iota((n, LANES), 1)
    halves = []
    for pr in range(x.shape[1] // LANES):
        slab = jnp.concatenate([x[:, pr * LANES:(pr + 1) * LANES], jnp.zeros((LANES -n, LANES), x.dtype)], axis=0)
        tr = slab.T
        halves.append(jnp.where(lane < n, tr[0:n], pltpu.roll(tr[n:2 * n], n, axis=1)))
    return jnp.concatenate(halves, axis=1)


def _nsa_kernel(*refs, nsel, qpb):
    per_chunk = KCH // SEL_LEN
    for k in range(qpb):
        _nsa_block(k, "head", None, *refs, nsel=nsel, qpb=qpb)

    def step(i, carry):
        for k in range(qpb):
            _nsa_block(k, "step", i, *refs, nsel=nsel, qpb=qpb)
        return carry

    common = pl.program_id(1) * (qpb // per_chunk)
    lax.fori_loop(0, common, step, 0)
    for k in range(qpb):
        for extra in range(k // per_chunk):
            _nsa_block(k, "step", common + extra, *refs, nsel=nsel, qpb=qpb)
    for k in range(qpb):
        _nsa_block(k, "tail", None, *refs, nsel=nsel, qpb=qpb)


def _nsa_block(k, phase, step_i, zq_ref, qg_ref, kc_ref, vct_ref, ks_ref, kw_ref, vst_ref, vwt_ref, ng_ref, o_ref,
               qtp_ref, oc_ref, ow_ref, bias_ref, ms_ref, accs_ref, mw_ref, accw_ref, sbuf_ref, wbuf_ref,
               gates_ref, scs_ref, wbias_ref, *, nsel, qpb):
    qtp_ref, oc_ref, ow_ref, bias_ref, ms_ref, accs_ref, sbuf_ref, gates_ref = (
        r.at[k] for r in (qtp_ref, oc_ref, ow_ref, bias_ref, ms_ref, accs_ref, sbuf_ref, gates_ref))
    qi = pl.program_id(1) * qpb + k
    q_rows = pl.ds(k * Q_BLK, Q_BLK)
    s0 = qi * Q_BLK
    cur = qi
    ncmp = kc_ref.shape[1]
    per_chunk = KCH // SEL_LEN
    groups = range(N_KV)
    n_chunks = qi // per_chunk + 1
    cur_slab = k % per_chunk
    cut_rows = slice(cur_slab * SEL_LEN, (cur_slab + 1) * SEL_LEN)
    last_slabs = per_chunk // 2 if cur_slab < per_chunk // 2 else per_chunk

    def scores(k_ref, i, gs=groups):
        k0 = pl.multiple_of(i * KCH, KCH)
        return [_dot(k_ref[g // 2, pl.ds(k0, KCH), :], qtp_ref[g]) for g in gs]

    def park(buf_ref, vals, gs=groups):
        for g, val in zip(gs, vals):
            buf_ref[g] = val

    def cut(buf_ref, tri):
        for g in groups:
            buf_ref[g, cut_rows, :] = buf_ref[g, cut_rows, :] + tri

    def update(g, blocks, shifts, vt, m_ref, acc_ref):
        m = m_ref[g]
        part = None
        for blk, sh in zip(blocks, shifts):
            top8 = jnp.max(blk.reshape(SEL_LEN // 8, 8, GW), axis=0) + sh
            part = top8 if part is None else jnp.maximum(part, top8)
        mnew = jnp.maximum(m, jnp.max(part, axis=0, keepdims=True))
        e = jnp.concatenate([jnp.exp2(blk + (sh - mnew)) for blk, sh in zip(blocks, shifts)], axis=0).astype(BF16)
        acc_ref[g] = jnp.exp2(m - mnew) * acc_ref[g] + _dot(vt, e)
        m_ref[g] = mnew

    def slabs(buf_ref, g, ns):
        return [buf_ref[g, nb * SEL_LEN:(nb + 1) * SEL_LEN, :] for nb in range(ns)]

    def sel_softmax(i, ns=per_chunk, gs=groups):
        for g in gs:
            shifts = [bias_ref[g, pl.ds(i * per_chunk + nb, 1), :] for nb in range(ns)]
            update(g, slabs(sbuf_ref, g, ns), shifts, vst_ref[i, g, :, 0:ns * SEL_LEN], ms_ref, accs_ref)

    key_row = _iota((SEL_LEN, GW), 0)
    in_blk_t = _iota((SEL_LEN, GW), 1) % Q_BLK
    causal_tri = jnp.where(key_row <= in_blk_t, 0.0, NEG)

    if phase == "step":
        for gs in (range(0, N_KV // 2), range(N_KV // 2, N_KV)):
            nxt = scores(ks_ref, step_i + 1, gs)
            sel_softmax(step_i, gs=gs)
            park(sbuf_ref, nxt, gs)
        return

    if phase == "tail":
        cut(sbuf_ref, causal_tri)
        sel_softmax(n_chunks - 1, last_slabs)
        outs = []
        for g in groups:
            o_s = accs_ref[g, 0:HEAD_DIM, :] * (1.0 / jnp.maximum(accs_ref[g, HEAD_DIM:HEAD_DIM + 1, :], 1e-30))
            gate = [gates_ref[g, c:c + 1, :] for c in range(3)]
            o_t = gate[0] * oc_ref[g] + gate[1] * o_s + gate[2] * ow_ref[g]
            outs.append(_swap_heads(o_t))
        for g in groups:
            o_ref[g, q_rows, :] = outs[g].astype(o_ref.dtype)
        return

    lane_t = s0 + (_iota((1, GW), 1) % Q_BLK)
    lane_grp = _iota((1, GW), 1) // Q_BLK
    ov_n = _iota((nsel, ncmp), 0) * SEL_LEN
    ov_c = _iota((nsel, ncmp), 1) * CMP_STRIDE
    overlap = jnp.where((ov_c < ov_n + SEL_LEN) & (ov_c + CMP_LEN > ov_n), 1.0, 0.0).astype(BF16)


    ok_c = _iota((ncmp, GW), 0) * CMP_STRIDE + (CMP_LEN - 1) <= lane_t
    ngt = jnp.concatenate([ng_ref[q_rows, :], jnp.zeros((LANES -Q_BLK, NG_PAD), F32)], axis=0).T
    ngt_hi = pltpu.roll(ngt, Q_BLK, axis=1)
    low_half = _iota((1, LANES), 1) < Q_BLK
    for g in groups:
        for c in range(3):
            r = c * N_HEADS + g * HPG
            pairs = [jnp.where(low_half, ngt[r + j:r + j + 1], ngt_hi[r + j + 1:r + j + 2]) for j in (0, 2)]
            gates_ref[g, c:c + 1, :] = jnp.concatenate(pairs, axis=1)

    qts = []
    for g in groups:
        xt = _swap_heads(zq_ref[g, q_rows, :])
        ssq = jnp.sum(xt * xt, axis=0, keepdims=True)
        qts.append((xt * lax.rsqrt(ssq * (1.0 / HEAD_DIM) + EPS) * qg_ref[...]).astype(BF16))
    scs = []
    for g in groups:
        off = (g % 2) * HEAD_DIM
        qtp_ref[g, off:off + HEAD_DIM, :] = qts[g]
        qtp_ref[g, HEAD_DIM - off:2 * HEAD_DIM - off, :] = jnp.zeros((HEAD_DIM, GW), BF16)
        scs.append(_dot(kc_ref[g], qts[g]))

    n_win = WINDOW // KCH + 1
    win_ids = [jnp.maximum(n_chunks - (n_win - j), 0) for j in range(n_win)]
    win_dead = [jnp.where(n_chunks >= n_win - j, 0.0, NEG) for j in range(n_win)]
    park(scs_ref, scs)
    park(sbuf_ref, scores(ks_ref, 0))
    park(wbuf_ref, scores(kw_ref, win_ids[0]))

    expired_tri = jnp.where(key_row > in_blk_t, 0.0, NEG)
    win_blocks = WINDOW // SEL_LEN
    n_all = _iota((nsel, GW), 0)
    wbias_ref[...] = jnp.where((n_all >= cur - win_blocks) & (n_all <= cur), 0.0, NEG)
    cut(wbuf_ref, jnp.where(cur >= win_blocks, expired_tri, 0.0))

    ms_ref[...] = jnp.full(ms_ref.shape, 0.5 * NEG, F32)
    mw_ref[...] = jnp.full(mw_ref.shape, 0.5 * NEG, F32)
    accs_ref[...] = jnp.zeros(accs_ref.shape, F32)
    accw_ref[...] = jnp.zeros(accw_ref.shape, F32)

    def win_softmax(i, dead, lo=0, hi=per_chunk):
        shifts = [wbias_ref[pl.ds(i * per_chunk + nb, 1), :] + dead for nb in range(lo, hi)]
        for g in groups:
            blocks = [wbuf_ref[g, nb * SEL_LEN:(nb + 1) * SEL_LEN, :] for nb in range(lo, hi)]
            update(g, blocks, shifts, vwt_ref[i, g, :, lo * SEL_LEN:hi * SEL_LEN], mw_ref, accw_ref)

    first_lo = per_chunk // 2 if cur_slab >= per_chunk // 2 else 0
    for j in range(n_win - 1):
        nxt = scores(kw_ref, win_ids[j + 1])
        win_softmax(win_ids[j], win_dead[j], lo=first_lo if j == 0 else 0)
        park(wbuf_ref, nxt)

    imp_all = jnp.zeros((nsel, GW), F32)
    some_c = lane_t >= CMP_LEN - 1
    ones_rows = jnp.ones((16, ncmp), BF16)
    for g in groups:
        scm = jnp.where(ok_c, scs_ref[g], NEG)
        ec = jnp.exp2(scm - jnp.max(scm, axis=0, keepdims=True))
        both = _dot(jnp.concatenate([vct_ref[g], overlap, ones_rows], axis=0), ec.astype(BF16))
        denom = both[HEAD_DIM + nsel:HEAD_DIM + nsel + 1]
        both = both[0:HEAD_DIM + nsel] * jnp.where(some_c, 1.0 / jnp.maximum(denom, 1e-30), 0.0)
        oc_ref[g] = both[0:HEAD_DIM]
        imp = _spread_heads(both[HEAD_DIM:HEAD_DIM + nsel])
        imp_all = jnp.where(lane_grp == g, imp, imp_all)

    n_io = _iota((nsel, GW), 0)
    forced = (n_io == 0) | (n_io == cur) | (n_io == cur - 1)
    valid = n_io <= cur
    score = jnp.where(forced, FORCE_SCORE, imp_all)
    score = jnp.where(valid, score, -1.0)
    rows8 = [score[r:r + 8] for r in range(0, nsel, 8)]
    ranks = [jnp.zeros((8, GW), F32) for _ in rows8]
    sub = _iota((8, GW), 0)
    for m in range(nsel):
        rowv = score[m:m + 1, :]
        for k, blk in enumerate(rows8):
            ge = jnp.where(rowv >= blk, 1.0, 0.0)
            gt = jnp.where(rowv > blk, 1.0, 0.0)
            if 8 * k > m:
                inc = ge
            elif 8 * k + 7 < m:
                inc = gt
            else:
                inc = jnp.where(sub + 8 * k > m, ge, gt)
            ranks[k] = ranks[k] + inc
    rank = jnp.concatenate(ranks, axis=0)
    sel_all = jnp.where((rank < float(min(SEL_TOPN, nsel))) & valid, 1.0, 0.0)
    for g in groups:
        mine = jnp.where(lane_grp == g, sel_all, 0.0)
        bias_ref[g] = (_spread_heads(mine) - 1.0) * (-NEG)

    cut(wbuf_ref, causal_tri)
    win_softmax(win_ids[n_win - 1], win_dead[n_win - 1], hi=last_slabs)
    for g in groups:
        ow_ref[g] = accw_ref[g, 0:HEAD_DIM, :] * (1.0 / jnp.maximum(accw_ref[g, HEAD_DIM:HEAD_DIM + 1, :], 1e-30))


def _nsa(zq, qgain, kc, vct, kn, vt, ng, *, B, S):
    T = B * S
    qpb = 2 * (KCH // SEL_LEN)
    assert S % (Q_BLK * qpb) == 0
    nq = S // (Q_BLK * qpb)
    nch = S // KCH
    ncmp = S // CMP_STRIDE
    nsel = S // SEL_LEN
    tok = lambda b, i: (0, b * nq + i, 0)
    return pl.pallas_call(
        functools.partial(_nsa_kernel, nsel=nsel, qpb=qpb),
        grid=(B, nq),
        in_specs=[
            pl.BlockSpec((N_KV, Q_BLK * qpb, GW), tok),
            _const_spec((HEAD_DIM, GW)),
            pl.BlockSpec((None, N_KV, ncmp, HEAD_DIM), lambda b, i: (b, 0, 0, 0)),
            pl.BlockSpec((None, N_KV, HEAD_DIM, ncmp), lambda b, i: (b, 0, 0, 0)),
            pl.BlockSpec((None, None, 2, S, LANES), lambda b, i: (0, b, 0, 0, 0)),
            pl.BlockSpec((None, None, 2, S, LANES), lambda b, i: (1, b, 0, 0, 0)),
            pl.BlockSpec((None, None, nch, N_KV, VROWS, KCH), lambda b, i: (0, b, 0, 0, 0, 0)),
            pl.BlockSpec((None, None, nch, N_KV, VROWS, KCH), lambda b, i: (1, b, 0, 0, 0, 0)),
            pl.BlockSpec((Q_BLK * qpb, NG_PAD), lambda b, i: (b * nq + i, 0)),
        ],
        out_specs=pl.BlockSpec((N_KV, Q_BLK * qpb, GW), tok),
        out_shape=jax.ShapeDtypeStruct((N_KV, T, GW), BF16),
        scratch_shapes=[
            pltpu.VMEM((qpb, N_KV, 2 * HEAD_DIM, GW), BF16),
            pltpu.VMEM((qpb, N_KV, HEAD_DIM, GW), F32),
            pltpu.VMEM((qpb, N_KV, HEAD_DIM, GW), F32),
            pltpu.VMEM((qpb, N_KV, nsel, GW), F32),
            pltpu.VMEM((qpb, N_KV, 1, GW), F32),
            pltpu.VMEM((qpb, N_KV, VROWS, GW), F32),
            pltpu.VMEM((N_KV, 1, GW), F32),
            pltpu.VMEM((N_KV, VROWS, GW), F32),
            pltpu.VMEM((qpb, N_KV, KCH, GW), F32),
            pltpu.VMEM((N_KV, KCH, GW), F32),
            pltpu.VMEM((qpb, N_KV, 8, GW), F32),
            pltpu.VMEM((N_KV, ncmp, GW), F32),
            pltpu.VMEM((nsel, GW), F32),
        ],
        compiler_params=_params(("parallel", "arbitrary")),
        name="nsa",
    )(zq, qgain, kc, vct, kn, kn, vt, vt, ng)


def _merge_ffn_kernel(x_ref, ya_ref, o_ref_in, mg_ref, wn_ref, wo_ref, g2_ref, wg_ref, wu_ref, wd_ref,
                      out_ref, h_ref, *, tf):
    yb = _dot(o_ref_in[0], wn_ref[0:GW, :])
    for g in range(1, N_KV):
        yb = yb + _dot(o_ref_in[g], wn_ref[g * GW:(g + 1) * GW, :])
    tm = x_ref.shape[0]
    ya = jnp.concatenate([ya_ref[ck].reshape(tm, LANES) for ck in range(D_MODEL // LANES)], axis=1)
    mixed = (mg_ref[:, 0:D_MODEL].astype(F32) * ya
             + mg_ref[:, D_MODEL:2 * D_MODEL].astype(F32) * yb)
    x1 = x_ref[...] + _dot(mixed.astype(BF16), wo_ref[...])
    y = x1 * lax.rsqrt(jnp.mean(x1 * x1, axis=-1, keepdims=True) + EPS) * g2_ref[...]
    h_ref[...] = y.astype(BF16)
    out_ref[...] = x1
    for j in range(D_FF // tf):
        h = h_ref[...]
        sl = slice(j * tf, (j + 1) * tf)
        act = jax.nn.silu(_dot(h, wg_ref[:, sl])) * _dot(h, wu_ref[:, sl])
        out_ref[...] += _dot(act.astype(BF16), wd_ref[sl, :])


def _merge_ffn(x2d, ya, o, mg, wn, wo, gain2, wg, wu, wd, *, B, S, tm=512, tf=256):
    assert S % tm == 0 and D_FF % tf == 0
    ni = S // tm
    tok = lambda b, i: (b * ni + i, 0)
    once = lambda shape: pl.BlockSpec(shape, lambda b, i: (0,) * len(shape), pipeline_mode=pl.Buffered(1))
    return pl.pallas_call(
        functools.partial(_merge_ffn_kernel, tf=tf),
        grid=(B, ni),
        in_specs=[
            pl.BlockSpec((tm, D_MODEL), tok),
            pl.BlockSpec((D_MODEL // LANES, tm // 8, None, 8, LANES), lambda b, i: (0, i, b, 0, 0)),
            pl.BlockSpec((N_KV, tm, GW), lambda b, i: (0, b * ni + i, 0)),
            pl.BlockSpec((tm, 2 * D_MODEL), tok),
            once((N_HEADS * HEAD_DIM, D_MODEL)),
            once((D_MODEL, D_MODEL)),
            _const_spec((1, D_MODEL)),
            once((D_MODEL, D_FF)),
            once((D_MODEL, D_FF)),
            once((D_FF, D_MODEL)),
        ],
        out_specs=pl.BlockSpec((tm, D_MODEL), tok),
        out_shape=jax.ShapeDtypeStruct((B * S, D_MODEL), F32),
        scratch_shapes=[pltpu.VMEM((tm, D_MODEL), BF16)],
        compiler_params=_params(("parallel", "parallel")),
        name="merge_ffn",
    )(x2d, ya, o, mg, wn, wo, gain2, wg, wu, wd)


def _layer(x, p):
    B, S, _ = x.shape
    T = B * S
    dh = HEAD_DIM
    row = lambda v: v.reshape(1, -1)
    w_in = p["w_in"]
    n_ng = 3 * N_HEADS
    o_ng = 2 * D_RNN + N_HEADS * dh + 6 * KVW
    w_packed = jnp.concatenate(
        [w_in[:, :o_ng], jnp.pad(w_in[:, o_ng:o_ng + n_ng], ((0, 0), (0, NG_PAD - n_ng))), w_in[:, o_ng + n_ng:]],
        axis=1).astype(BF16)
    rx, gy, zq, zkv, ng, mg = _inproj(x, row(p["norm1"]), w_packed)

    slabs = lambda a: a.reshape(a.shape[0], S * B, LANES)
    ya = _rglru(slabs(rx), slabs(gy), p["conv_w"], row(p["conv_b"]),
                p["rg_wa"].astype(BF16), row(p["rg_ba"]), p["rg_wi"].astype(BF16), row(p["rg_bi"]),
                row(p["rg_lambda"]), p["w_rg_out"].astype(BF16), nb=B)
    ya = ya.reshape(D_MODEL // LANES, S // 8, B, 8, LANES)

    kgains = jnp.tile(p["k_norm"][1:3], (1, N_KV)).reshape(2, 1, KVW)
    kn, vt = _kvprep(zkv, kgains, B=B, S=S)
    pos = jnp.stack([p["cmp_pos_k"], p["cmp_pos_v"]]).reshape(2, CMP_STRIDE, 2 * dh)
    w1 = jnp.stack([p["cmp_k_w1"], p["cmp_v_w1"]]).astype(BF16)
    w2 = jnp.stack([p["cmp_k_w2"], p["cmp_v_w2"]]).astype(BF16)
    kc, vct = _compress(zkv, pos, w1, w2, row(p["k_norm"][0]), B=B, S=S)
    qgain = jnp.broadcast_to((p["q_norm"] * (dh ** -0.5 * LOG2E))[:, None], (dh, GW))
    o = _nsa(zq, qgain, kc, vct, kn, vt, ng, B=B, S=S)

    x2 = _merge_ffn(x.reshape(T, D_MODEL), ya, o, mg, p["w_nsa_out"].astype(BF16), p["w_o"].astype(BF16),
                    row(p["norm2"]), p["w_gate"].astype(BF16), p["w_up"].astype(BF16), p["w_down"].astype(BF16),
                    B=B, S=S)
    return x2.reshape(B, S, D_MODEL)


def kernel(x, norm1, w_in, conv_w, conv_b, rg_wa, rg_ba, rg_wi, rg_bi, rg_lambda, q_norm, k_norm, cmp_pos_k,
           cmp_pos_v, cmp_k_w1, cmp_k_w2, cmp_v_w1, cmp_v_w2, w_rg_out, w_nsa_out, w_o, norm2, w_gate, w_up, w_down):
    params = dict(norm1=norm1, w_in=w_in, conv_w=conv_w, conv_b=conv_b, rg_wa=rg_wa, rg_ba=rg_ba, rg_wi=rg_wi,
                  rg_bi=rg_bi, rg_lambda=rg_lambda, q_norm=q_norm, k_norm=k_norm, cmp_pos_k=cmp_pos_k,
                  cmp_pos_v=cmp_pos_v, cmp_k_w1=cmp_k_w1, cmp_k_w2=cmp_k_w2, cmp_v_w1=cmp_v_w1, cmp_v_w2=cmp_v_w2,
                  w_rg_out=w_rg_out, w_nsa_out=w_nsa_out, w_o=w_o, norm2=norm2, w_gate=w_gate, w_up=w_up,
                  w_down=w_down)
    for l in range(norm1.shape[0]):
        x = _layer(x, {k: v[l] for k, v in params.items()})
    return x
```

```python
import functools

import jax
import jax.numpy as jnp
from jax import lax
from jax.experimental import pallas as pl
from jax.experimental.pallas import tpu as pltpu

D_MODEL = 1024
D_RNN = 1024
RG_BLOCKS = 4
RG_BW = D_RNN // RG_BLOCKS
CONV_W = 4
RG_C = 8.0
N_HEADS = 16
N_KV = 4
HEAD_DIM = 64
HPG = N_HEADS // N_KV
CMP_LEN = 32
CMP_STRIDE = 16
CMP_HID = 256
SEL_LEN = 64
SEL_TOPN = 16
WINDOW = 512
Q_BLK = 64
D_FF = 2816
EPS = 1e-6
FORCE_SCORE = 1e6
NEG = -1e30
LOG2E = 1.4426950408889634

LANES = 128
KCH = 256
GW = HPG * HEAD_DIM
KVW = N_KV * HEAD_DIM
VROWS = 80
NG_PAD = LANES

F32 = jnp.float32
BF16 = jnp.bfloat16

VMEM_LIMIT = 56 * 1024 * 1024


def _params(sem):
    return pltpu.CompilerParams(dimension_semantics=sem, vmem_limit_bytes=VMEM_LIMIT)


def _dot(a, b):
    return jnp.dot(a, b, preferred_element_type=F32)


def _dot_nt(a, b):
    return lax.dot_general(a, b, (((1,), (1,)), ((), ())), preferred_element_type=F32)


def _split(x):
    hi = x.astype(BF16)
    lo = (x - hi.astype(F32)).astype(BF16)
    return hi, lo


def _iota(shape, dim):
    return lax.broadcasted_iota(jnp.int32, shape, dim)


def _const_spec(shape):
    return pl.BlockSpec(shape, lambda *_: (0,) * len(shape))


_O_RY = D_RNN
_O_Q = 2 * D_RNN
_O_KV = _O_Q + N_HEADS * HEAD_DIM
_O_NG = _O_KV + 6 * KVW
_O_MG = _O_NG + NG_PAD
_W_COLS = _O_MG + 2 * D_MODEL


def _inproj_kernel(x_ref, g_ref, w_ref, rx_ref, gy_ref, zq_ref, zkv_ref, ng_ref, mg_ref, h_ref):
    x = x_ref[...]
    y = x * lax.rsqrt(jnp.mean(x * x, axis=-1, keepdims=True) + EPS) * g_ref[...]
    h_ref[...] = y.astype(BF16)

    def proj(c0, width):
        return _dot(h_ref[...], w_ref[:, c0:c0 + width])

    cw = 256
    tm = x_ref.shape[0]
    for c in range(D_RNN // cw):
        rx = proj(c * cw, cw)
        gy = jax.nn.gelu(proj(_O_RY + c * cw, cw))
        for hf in range(cw // LANES):
            rx_ref[c * (cw // LANES) + hf] = rx[:, hf * LANES:(hf + 1) * LANES].reshape(tm // 8, 8, LANES)
            gy_ref[c * (cw // LANES) + hf] = gy[:, hf * LANES:(hf + 1) * LANES].reshape(tm // 8, 8, LANES)
    for g in range(N_KV):
        zq_ref[g] = proj(_O_Q + g * GW, GW)
    for s in range(6):
        z = proj(_O_KV + s * KVW, KVW)
        for pr in range(KVW // LANES):
            zkv_ref[s, pr] = z[:, pr * LANES:(pr + 1) * LANES]
    ng_ref[...] = jax.nn.sigmoid(proj(_O_NG, NG_PAD))
    for c in range(2 * D_MODEL // cw):
        mg_ref[:, c * cw:(c + 1) * cw] = jax.nn.sigmoid(proj(_O_MG + c * cw, cw)).astype(mg_ref.dtype)


def _inproj(x, gain, w, *, tm=256):
    B, S, D = x.shape
    T = B * S
    ni = S // tm
    nck = D_RNN // LANES
    tok = lambda b, i: (b * ni + i, 0)
    slab_spec = pl.BlockSpec((nck, tm // 8, None, 8, LANES), lambda b, i: (0, i, b, 0, 0))
    slab_shape = jax.ShapeDtypeStruct((nck, S // 8, B, 8, LANES), F32)
    return pl.pallas_call(
        _inproj_kernel,
        grid=(B, ni),
        in_specs=[
            pl.BlockSpec((None, tm, D), lambda b, i: (b, i, 0)),
            _const_spec((1, D)),
            pl.BlockSpec((D, _W_COLS), lambda b, i: (0, 0), pipeline_mode=pl.Buffered(1)),
        ],
        out_specs=[
            slab_spec,
            slab_spec,
            pl.BlockSpec((N_KV, tm, GW), lambda b, i: (0, b * ni + i, 0)),
            pl.BlockSpec((6, KVW // LANES, tm, LANES), lambda b, i: (0, 0, b * ni + i, 0)),
            pl.BlockSpec((tm, NG_PAD), tok),
            pl.BlockSpec((tm, 2 * D_MODEL), tok),
        ],
        out_shape=[
            slab_shape,
            slab_shape,
            jax.ShapeDtypeStruct((N_KV, T, GW), F32),
            jax.ShapeDtypeStruct((6, KVW // LANES, T, LANES), F32),
            jax.ShapeDtypeStruct((T, NG_PAD), F32),
            jax.ShapeDtypeStruct((T, 2 * D_MODEL), BF16),
        ],
        scratch_shapes=[pltpu.VMEM((tm, D), BF16)],
        compiler_params=_params(("parallel", "parallel")),
        name="inproj",
    )(x, gain, w)


def _rglru_kernel(x_ref, gy_ref, cw_ref, cb_ref, wa_ref, ba_ref, wi_ref, bi_ref, lam_ref, wo_ref,
                  o_ref, xb_ref, gyb_ref, a_ref, u_ref, hb_ref, hs_ref, yb_ref, *, nb, tt):
    R = nb * tt
    halo = (CONV_W - 1) * nb
    step = pl.program_id(0)
    nck = D_RNN // LANES

    @pl.when(step == 0)
    def _():
        xb_ref[0:halo, :] = jnp.zeros((halo, D_RNN), F32)
        hs_ref[...] = jnp.zeros((nb, D_RNN), F32)

    def slab_rows(t8, t_lo):
        return pl.ds(t8 * (8 * nb) + t_lo, nb, stride=8)

    def stage_in(t8, carry):
        for t_lo in range(8):
            r0 = pl.multiple_of((t8 * 8 + t_lo) * nb, nb)
            for ck in range(nck):
                cols = slice(ck * LANES, (ck + 1) * LANES)
                xb_ref[pl.ds(halo + r0, nb), cols] = x_ref[ck, slab_rows(t8, t_lo), :]
                gyb_ref[pl.ds(r0, nb), cols] = gy_ref[ck, slab_rows(t8, t_lo), :]
        return carry

    lax.fori_loop(0, tt // 8, stage_in, 0)
    xr = cb_ref[...] + cw_ref[0:1, :] * xb_ref[0:R, :]
    for k in range(1, CONV_W):
        xr = xr + cw_ref[k:k + 1, :] * xb_ref[k * nb:k * nb + R, :]
    tail = xb_ref[R:R + halo, :]
    xb_ref[0:halo, :] = tail

    xrb = xr.astype(BF16)
    decay_log2 = (-RG_C * LOG2E) * jax.nn.softplus(-lam_ref[...])
    row = _iota((R, RG_BW), 0)
    first = (row < nb) & (step == 0)
    for n in range(RG_BLOCKS):
        sl = slice(n * RG_BW, (n + 1) * RG_BW)
        xn = xrb[:, sl]
        r = jax.nn.sigmoid(_dot(xn, wa_ref[n]) + ba_ref[:, sl])
        ig = jax.nn.sigmoid(_dot(xn, wi_ref[n]) + bi_ref[:, sl])
        a = jnp.exp2(r * decay_log2[:, sl])
        mult = jnp.sqrt(1.0 - a * a)
        mult = jnp.where(first, 1.0, mult)
        a_ref[:, sl] = a
        u_ref[:, sl] = mult * ig * xr[:, sl]

    def body(t, h):
        r0 = pl.multiple_of(t * nb, nb)
        h = a_ref[pl.ds(r0, nb), :] * h + u_ref[pl.ds(r0, nb), :]
        hb_ref[pl.ds(r0, nb), :] = h
        return h

    h = lax.fori_loop(0, tt, body, hs_ref[...], unroll=8)
    hs_ref[...] = h
    yb_ref[...] = _dot((hb_ref[...] * gyb_ref[...]).astype(BF16), wo_ref[...])

    def stage_out(t8, carry):
        for t_lo in range(8):
            r0 = pl.multiple_of((t8 * 8 + t_lo) * nb, nb)
            for ck in range(nck):
                o_ref[ck, slab_rows(t8, t_lo), :] = yb_ref[pl.ds(r0, nb), ck * LANES:(ck + 1) * LANES]
        return carry

    lax.fori_loop(0, tt // 8, stage_out, 0)


def _rglru(rx, gy, conv_w, conv_b, wa, ba, wi, bi, lam, wo, *, nb, tt=64):
    nck, rows, _ = rx.shape
    S = rows // nb
    assert S % tt == 0 and tt % 8 == 0
    R = nb * tt
    halo = (CONV_W - 1) * nb
    return pl.pallas_call(
        functools.partial(_rglru_kernel, nb=nb, tt=tt),
        grid=(S // tt,),
        in_specs=[
            pl.BlockSpec((nck, R, LANES), lambda s: (0, s, 0)),
            pl.BlockSpec((nck, R, LANES), lambda s: (0, s, 0)),
            _const_spec((CONV_W, D_RNN)),
            _const_spec((1, D_RNN)),
            _const_spec((RG_BLOCKS, RG_BW, RG_BW)),
            _const_spec((1, D_RNN)),
            _const_spec((RG_BLOCKS, RG_BW, RG_BW)),
            _const_spec((1, D_RNN)),
            _const_spec((1, D_RNN)),
            _const_spec((D_RNN, D_MODEL)),
        ],
        out_specs=pl.BlockSpec((D_MODEL // LANES, R, LANES), lambda s: (0, s, 0)),
        out_shape=jax.ShapeDtypeStruct((D_MODEL // LANES, rows, LANES), F32),
        scratch_shapes=[
            pltpu.VMEM((R + halo, D_RNN), F32),
            pltpu.VMEM((R, D_RNN), F32),
            pltpu.VMEM((R, D_RNN), F32),
            pltpu.VMEM((R, D_RNN), F32),
            pltpu.VMEM((R, D_RNN), F32),
            pltpu.VMEM((nb, D_RNN), F32),
            pltpu.VMEM((R, D_MODEL), F32),
        ],
        compiler_params=_params(("arbitrary",)),
        name="rglru",
    )(rx, gy, conv_w, conv_b, wa, ba, wi, bi, lam, wo)


def _kvprep_kernel(zk_ref, zv_ref, g_ref, kn_ref, vt_ref):
    rows = zk_ref.shape[1]
    x = jnp.concatenate([zk_ref[0], zk_ref[1]], axis=1)
    seg = jnp.where(_iota((KVW, KVW), 0) // HEAD_DIM == _iota((KVW, KVW), 1) // HEAD_DIM, 1.0, 0.0).astype(BF16)
    hi, lo = _split(x * x)
    ssq = _dot(jnp.concatenate([hi, lo], axis=0), seg)
    kn = (x * lax.rsqrt((ssq[0:rows] + ssq[rows:2 * rows]) * (1.0 / HEAD_DIM) + EPS) * g_ref[...]).astype(BF16)
    for pr in range(KVW // LANES):
        kn_ref[pr] = kn[:, pr * LANES:(pr + 1) * LANES]
    eye = jnp.where(_iota((KVW, KVW), 0) == _iota((KVW, KVW), 1), 1.0, 0.0).astype(BF16)
    pad = jnp.where(_iota((VROWS - HEAD_DIM, KCH), 0) == 0, 1.0, 0.0).astype(BF16)
    for c in range(rows // KCH):
        sl = slice(c * KCH, (c + 1) * KCH)
        v = jnp.concatenate([zv_ref[0, sl, :], zv_ref[1, sl, :]], axis=1).astype(BF16)
        vt = _dot_nt(eye, v).astype(BF16)
        for g in range(N_KV):
            vt_ref[c, g, 0:HEAD_DIM, :] = vt[g * HEAD_DIM:(g + 1) * HEAD_DIM]
            vt_ref[c, g, HEAD_DIM:VROWS, :] = pad


def _kvprep(zkv, gains, *, B, S, cpb=4):
    nch = S // KCH
    assert nch % cpb == 0
    nblk = nch // cpb
    rows = cpb * KCH
    return pl.pallas_call(
        _kvprep_kernel,
        grid=(2, B, nblk),
        in_specs=[
            pl.BlockSpec((None, KVW // LANES, rows, LANES), lambda w, b, i: (2 + 2 * w, 0, b * nblk + i, 0)),
            pl.BlockSpec((None, KVW // LANES, rows, LANES), lambda w, b, i: (3 + 2 * w, 0, b * nblk + i, 0)),
            pl.BlockSpec((None, 1, KVW), lambda w, b, i: (w, 0, 0)),
        ],
        out_specs=[
            pl.BlockSpec((None, None, KVW // LANES, rows, LANES), lambda w, b, i: (w, b, 0, i, 0)),
            pl.BlockSpec((None, None, cpb, N_KV, VROWS, KCH), lambda w, b, i: (w, b, i, 0, 0, 0)),
        ],
        out_shape=[
            jax.ShapeDtypeStruct((2, B, KVW // LANES, S, LANES), BF16),
            jax.ShapeDtypeStruct((2, B, nch, N_KV, VROWS, KCH), BF16),
        ],
        compiler_params=_params(("parallel", "parallel", "parallel")),
        name="kvprep",
    )(zkv, zkv, gains)


def _compress_kernel(zk_ref, zv_ref, pos_ref, w1_ref, w2_ref, g_ref, kc_ref, vct_ref):
    nch = zk_ref.shape[0] // CMP_STRIDE
    lane = _iota((nch, LANES), 1)
    npair = CMP_STRIDE // 2
    eye = jnp.where(_iota((HEAD_DIM, HEAD_DIM), 0) == _iota((HEAD_DIM, HEAD_DIM), 1), 1.0, 0.0).astype(BF16)
    for kind, z_ref in enumerate((zk_ref, zv_ref)):
        for ge in range(2):
            combs = []
            for p in range(npair):
                a = z_ref[pl.ds(2 * p, nch, stride=CMP_STRIDE), :]
                b = z_ref[pl.ds(2 * p + 1, nch, stride=CMP_STRIDE), :]
                if ge == 0:
                    combs.append(jnp.where(lane < HEAD_DIM, a, pltpu.roll(b, HEAD_DIM, axis=1)))
                else:
                    combs.append(jnp.where(lane < HEAD_DIM, pltpu.roll(a, HEAD_DIM, axis=1), b))
            halves = []
            half_w = npair * LANES
            for half in range(2):
                lhs = jnp.concatenate(
                    [(combs[p] + pos_ref[kind, half * npair + p:half * npair + p + 1, :]).astype(BF16)
                     for p in range(npair)], axis=1)
                halves.append(_dot(lhs, w1_ref[kind, half * half_w:(half + 1) * half_w, :]))
            hid = halves[0] + pltpu.roll(halves[1], nch - 1, axis=0)
            hid = jax.nn.gelu(hid)
            out = _dot(hid.astype(BF16), w2_ref[kind])
            if kind == 0:
                out = out * lax.rsqrt(jnp.mean(out * out, axis=-1, keepdims=True) + EPS) * g_ref[...]
            res = jnp.where(_iota(out.shape, 0) < nch - 1, out, 0.0).astype(BF16)
            if kind == 0:
                kc_ref[ge] = res
            else:
                vct_ref[ge] = _dot_nt(eye, res).astype(BF16)


def _compress(zkv, pos, w1, w2, gain, *, B, S):
    nch = S // CMP_STRIDE
    return pl.pallas_call(
        _compress_kernel,
        grid=(B, KVW // LANES),
        in_specs=[
            pl.BlockSpec((None, None, S, LANES), lambda b, lp: (0, lp, b, 0)),
            pl.BlockSpec((None, None, S, LANES), lambda b, lp: (1, lp, b, 0)),
            _const_spec((2, CMP_STRIDE, LANES)),
            _const_spec((2, CMP_LEN * HEAD_DIM, CMP_HID)),
            _const_spec((2, CMP_HID, HEAD_DIM)),
            _const_spec((1, HEAD_DIM)),
        ],
        out_specs=[
            pl.BlockSpec((None, 2, nch, HEAD_DIM), lambda b, lp: (b, lp, 0, 0)),
            pl.BlockSpec((None, 2, HEAD_DIM, nch), lambda b, lp: (b, lp, 0, 0)),
        ],
        out_shape=[
            jax.ShapeDtypeStruct((B, N_KV, nch, HEAD_DIM), BF16),
            jax.ShapeDtypeStruct((B, N_KV, HEAD_DIM, nch), BF16),
        ],
        compiler_params=_params(("parallel", "parallel")),
        name="compress",
    )(zkv, zkv, pos, w1, w2, gain)


def _spread_heads(x):
    y = x + pltpu.roll(x, 2 * Q_BLK, axis=1)
    return y + pltpu.roll(y, Q_BLK, axis=1)


def _swap_heads(x):
    n = x.shape[0]
    lane = _iota((n, LANES), 1)
    halves = []
    for pr in range(x.shape[1] // LANES):
        slab = jnp.concatenate([x[:, pr * LANES:(pr + 1) * LANES], jnp.zeros((LANES -n, LANES), x.dtype)], axis=0)
        tr = slab.T
        halves.append(jnp.where(lane < n, tr[0:n], pltpu.roll(tr[n:2 * n], n, axis=1)))
    return jnp.concatenate(halves, axis=1)


def _nsa_kernel(*refs, nsel, qpb):
    per_chunk = KCH // SEL_LEN
    for k in range(qpb):
        _nsa_block(k, "head", None, *refs, nsel=nsel, qpb=qpb)

    def step(i, carry):
        for k in range(qpb):
            _nsa_block(k, "step", i, *refs, nsel=nsel, qpb=qpb)
        return carry

    common = pl.program_id(1) * (qpb // per_chunk)
    lax.fori_loop(0, common, step, 0)
    for k in range(qpb):
        for extra in range(k // per_chunk):
            _nsa_block(k, "step", common + extra, *refs, nsel=nsel, qpb=qpb)
    for k in range(qpb):
        _nsa_block(k, "tail", None, *refs, nsel=nsel, qpb=qpb)


def _nsa_block(k, phase, step_i, zq_ref, qg_ref, kc_ref, vct_ref, ks_ref, kw_ref, vst_ref, vwt_ref, ng_ref, o_ref,
               qtp_ref, oc_ref, ow_ref, bias_ref, ms_ref, accs_ref, mw_ref, accw_ref, sbuf_ref, wbuf_ref,
               gates_ref, scs_ref, wbias_ref, *, nsel, qpb):
    qtp_ref, oc_ref, ow_ref, bias_ref, ms_ref, accs_ref, sbuf_ref, gates_ref = (
        r.at[k] for r in (qtp_ref, oc_ref, ow_ref, bias_ref, ms_ref, accs_ref, sbuf_ref, gates_ref))
    qi = pl.program_id(1) * qpb + k
    q_rows = pl.ds(k * Q_BLK, Q_BLK)
    s0 = qi * Q_BLK
    cur = qi
    ncmp = kc_ref.shape[1]
    per_chunk = KCH // SEL_LEN
    groups = range(N_KV)
    n_chunks = qi // per_chunk + 1
    cur_slab = k % per_chunk
    cut_rows = slice(cur_slab * SEL_LEN, (cur_slab + 1) * SEL_LEN)
    last_slabs = per_chunk // 2 if cur_slab < per_chunk // 2 else per_chunk

    def scores(k_ref, i, gs=groups):
        k0 = pl.multiple_of(i * KCH, KCH)
        return [_dot(k_ref[g // 2, pl.ds(k0, KCH), :], qtp_ref[g]) for g in gs]

    def park(buf_ref, vals, gs=groups):
        for g, val in zip(gs, vals):
            buf_ref[g] = val

    def cut(buf_ref, tri):
        for g in groups:
            buf_ref[g, cut_rows, :] = buf_ref[g, cut_rows, :] + tri

    def update(g, blocks, shifts, vt, m_ref, acc_ref):
        m = m_ref[g]
        part = None
        for blk, sh in zip(blocks, shifts):
            top8 = jnp.max(blk.reshape(SEL_LEN // 8, 8, GW), axis=0) + sh
            part = top8 if part is None else jnp.maximum(part, top8)
        mnew = jnp.maximum(m, jnp.max(part, axis=0, keepdims=True))
        e = jnp.concatenate([jnp.exp2(blk + (sh - mnew)) for blk, sh in zip(blocks, shifts)], axis=0).astype(BF16)
        acc_ref[g] = jnp.exp2(m - mnew) * acc_ref[g] + _dot(vt, e)
        m_ref[g] = mnew

    def slabs(buf_ref, g, ns):
        return [buf_ref[g, nb * SEL_LEN:(nb + 1) * SEL_LEN, :] for nb in range(ns)]

    def sel_softmax(i, ns=per_chunk, gs=groups):
        for g in gs:
            shifts = [bias_ref[g, pl.ds(i * per_chunk + nb, 1), :] for nb in range(ns)]
            update(g, slabs(sbuf_ref, g, ns), shifts, vst_ref[i, g, :, 0:ns * SEL_LEN], ms_ref, accs_ref)

    key_row = _iota((SEL_LEN, GW), 0)
    in_blk_t = _iota((SEL_LEN, GW), 1) % Q_BLK
    causal_tri = jnp.where(key_row <= in_blk_t, 0.0, NEG)

    if phase == "step":
        for gs in (range(0, N_KV // 2), range(N_KV // 2, N_KV)):
            nxt = scores(ks_ref, step_i + 1, gs)
            sel_softmax(step_i, gs=gs)
            park(sbuf_ref, nxt, gs)
        return

    if phase == "tail":
        cut(sbuf_ref, causal_tri)
        sel_softmax(n_chunks - 1, last_slabs)
        outs = []
        for g in groups:
            o_s = accs_ref[g, 0:HEAD_DIM, :] * (1.0 / jnp.maximum(accs_ref[g, HEAD_DIM:HEAD_DIM + 1, :], 1e-30))
            gate = [gates_ref[g, c:c + 1, :] for c in range(3)]
            o_t = gate[0] * oc_ref[g] + gate[1] * o_s + gate[2] * ow_ref[g]
            outs.append(_swap_heads(o_t))
        for g in groups:
            o_ref[g, q_rows, :] = outs[g].astype(o_ref.dtype)
        return

    lane_t = s0 + (_iota((1, GW), 1) % Q_BLK)
    lane_grp = _iota((1, GW), 1) // Q_BLK
    ov_n = _iota((nsel, ncmp), 0) * SEL_LEN
    ov_c = _iota((nsel, ncmp), 1) * CMP_STRIDE
    overlap = jnp.where((ov_c < ov_n + SEL_LEN) & (ov_c + CMP_LEN > ov_n), 1.0, 0.0).astype(BF16)


    ok_c = _iota((ncmp, GW), 0) * CMP_STRIDE + (CMP_LEN - 1) <= lane_t
    ngt = jnp.concatenate([ng_ref[q_rows, :], jnp.zeros((LANES -Q_BLK, NG_PAD), F32)], axis=0).T
    ngt_hi = pltpu.roll(ngt, Q_BLK, axis=1)
    low_half = _iota((1, LANES), 1) < Q_BLK
    for g in groups:
        for c in range(3):
            r = c * N_HEADS + g * HPG
            pairs = [jnp.where(low_half, ngt[r + j:r + j + 1], ngt_hi[r + j + 1:r + j + 2]) for j in (0, 2)]
            gates_ref[g, c:c + 1, :] = jnp.concatenate(pairs, axis=1)

    qts = []
    for g in groups:
        xt = _swap_heads(zq_ref[g, q_rows, :])
        ssq = jnp.sum(xt * xt, axis=0, keepdims=True)
        qts.append((xt * lax.rsqrt(ssq * (1.0 / HEAD_DIM) + EPS) * qg_ref[...]).astype(BF16))
    scs = []
    for g in groups:
        off = (g % 2) * HEAD_DIM
        qtp_ref[g, off:off + HEAD_DIM, :] = qts[g]
        qtp_ref[g, HEAD_DIM - off:2 * HEAD_DIM - off, :] = jnp.zeros((HEAD_DIM, GW), BF16)
        scs.append(_dot(kc_ref[g], qts[g]))

    n_win = WINDOW // KCH + 1
    win_ids = [jnp.maximum(n_chunks - (n_win - j), 0) for j in range(n_win)]
    win_dead = [jnp.where(n_chunks >= n_win - j, 0.0, NEG) for j in range(n_win)]
    park(scs_ref, scs)
    park(sbuf_ref, scores(ks_ref, 0))
    park(wbuf_ref, scores(kw_ref, win_ids[0]))

    expired_tri = jnp.where(key_row > in_blk_t, 0.0, NEG)
    win_blocks = WINDOW // SEL_LEN
    n_all = _iota((nsel, GW), 0)
    wbias_ref[...] = jnp.where((n_all >= cur - win_blocks) & (n_all <= cur), 0.0, NEG)
    cut(wbuf_ref, jnp.where(cur >= win_blocks, expired_tri, 0.0))

    ms_ref[...] = jnp.full(ms_ref.shape, 0.5 * NEG, F32)
    mw_ref[...] = jnp.full(mw_ref.shape, 0.5 * NEG, F32)
    accs_ref[...] = jnp.zeros(accs_ref.shape, F32)
    accw_ref[...] = jnp.zeros(accw_ref.shape, F32)

    def win_softmax(i, dead, lo=0, hi=per_chunk, gs=groups):
        shifts = [wbias_ref[pl.ds(i * per_chunk + nb, 1), :] + dead for nb in range(lo, hi)]
        for g in gs:
            blocks = [wbuf_ref[g, nb * SEL_LEN:(nb + 1) * SEL_LEN, :] for nb in range(lo, hi)]
            update(g, blocks, shifts, vwt_ref[i, g, :, lo * SEL_LEN:hi * SEL_LEN], mw_ref, accw_ref)

    first_lo = per_chunk // 2 if cur_slab >= per_chunk // 2 else 0
    for j in range(n_win - 1):
        for gs in (range(0, N_KV // 2), range(N_KV // 2, N_KV)):
            nxt = scores(kw_ref, win_ids[j + 1], gs)
            win_softmax(win_ids[j], win_dead[j], lo=first_lo if j == 0 else 0, gs=gs)
            park(wbuf_ref, nxt, gs)

    imp_all = jnp.zeros((nsel, GW), F32)
    some_c = lane_t >= CMP_LEN - 1
    ones_rows = jnp.ones((16, ncmp), BF16)
    for g in groups:
        scm = jnp.where(ok_c, scs_ref[g], NEG)
        ec = jnp.exp2(scm - jnp.max(scm, axis=0, keepdims=True))
        both = _dot(jnp.concatenate([vct_ref[g], overlap, ones_rows], axis=0), ec.astype(BF16))
        denom = both[HEAD_DIM + nsel:HEAD_DIM + nsel + 1]
        both = both[0:HEAD_DIM + nsel] * jnp.where(some_c, 1.0 / jnp.maximum(denom, 1e-30), 0.0)
        oc_ref[g] = both[0:HEAD_DIM]
        imp = _spread_heads(both[HEAD_DIM:HEAD_DIM + nsel])
        imp_all = jnp.where(lane_grp == g, imp, imp_all)

    n_io = _iota((nsel, GW), 0)
    forced = (n_io == 0) | (n_io == cur) | (n_io == cur - 1)
    valid = n_io <= cur
    score = jnp.where(forced, FORCE_SCORE, imp_all)
    score = jnp.where(valid, score, -1.0)
    rows8 = [score[r:r + 8] for r in range(0, nsel, 8)]
    ranks = [jnp.zeros((8, GW), F32) for _ in rows8]
    sub = _iota((8, GW), 0)
    for m in range(nsel):
        rowv = score[m:m + 1, :]
        for k, blk in enumerate(rows8):
            ge = jnp.where(rowv >= blk, 1.0, 0.0)
            gt = jnp.where(rowv > blk, 1.0, 0.0)
            if 8 * k > m:
                inc = ge
            elif 8 * k + 7 < m:
                inc = gt
            else:
                inc = jnp.where(sub + 8 * k > m, ge, gt)
            ranks[k] = ranks[k] + inc
    rank = jnp.concatenate(ranks, axis=0)
    sel_all = jnp.where((rank < float(min(SEL_TOPN, nsel))) & valid, 1.0, 0.0)
    for g in groups:
        mine = jnp.where(lane_grp == g, sel_all, 0.0)
        bias_ref[g] = (_spread_heads(mine) - 1.0) * (-NEG)

    cut(wbuf_ref, causal_tri)
    win_softmax(win_ids[n_win - 1], win_dead[n_win - 1], hi=last_slabs)
    for g in groups:
        ow_ref[g] = accw_ref[g, 0:HEAD_DIM, :] * (1.0 / jnp.maximum(accw_ref[g, HEAD_DIM:HEAD_DIM + 1, :], 1e-30))


def _nsa(zq, qgain, kc, vct, kn, vt, ng, *, B, S):
    T = B * S
    qpb = 2 * (KCH // SEL_LEN)
    assert S % (Q_BLK * qpb) == 0
    nq = S // (Q_BLK * qpb)
    nch = S // KCH
    ncmp = S // CMP_STRIDE
    nsel = S // SEL_LEN
    tok = lambda b, i: (0, b * nq + i, 0)
    return pl.pallas_call(
        functools.partial(_nsa_kernel, nsel=nsel, qpb=qpb),
        grid=(B, nq),
        in_specs=[
            pl.BlockSpec((N_KV, Q_BLK * qpb, GW), tok),
            _const_spec((HEAD_DIM, GW)),
            pl.BlockSpec((None, N_KV, ncmp, HEAD_DIM), lambda b, i: (b, 0, 0, 0)),
            pl.BlockSpec((None, N_KV, HEAD_DIM, ncmp), lambda b, i: (b, 0, 0, 0)),
            pl.BlockSpec((None, None, 2, S, LANES), lambda b, i: (0, b, 0, 0, 0)),
            pl.BlockSpec((None, None, 2, S, LANES), lambda b, i: (1, b, 0, 0, 0)),
            pl.BlockSpec((None, None, nch, N_KV, VROWS, KCH), lambda b, i: (0, b, 0, 0, 0, 0)),
            pl.BlockSpec((None, None, nch, N_KV, VROWS, KCH), lambda b, i: (1, b, 0, 0, 0, 0)),
            pl.BlockSpec((Q_BLK * qpb, NG_PAD), lambda b, i: (b * nq + i, 0)),
        ],
        out_specs=pl.BlockSpec((N_KV, Q_BLK * qpb, GW), tok),
        out_shape=jax.ShapeDtypeStruct((N_KV, T, GW), BF16),
        scratch_shapes=[
            pltpu.VMEM((qpb, N_KV, 2 * HEAD_DIM, GW), BF16),
            pltpu.VMEM((qpb, N_KV, HEAD_DIM, GW), F32),
            pltpu.VMEM((qpb, N_KV, HEAD_DIM, GW), F32),
            pltpu.VMEM((qpb, N_KV, nsel, GW), F32),
            pltpu.VMEM((qpb, N_KV, 1, GW), F32),
            pltpu.VMEM((qpb, N_KV, VROWS, GW), F32),
            pltpu.VMEM((N_KV, 1, GW), F32),
            pltpu.VMEM((N_KV, VROWS, GW), F32),
            pltpu.VMEM((qpb, N_KV, KCH, GW), F32),
            pltpu.VMEM((N_KV, KCH, GW), F32),
            pltpu.VMEM((qpb, N_KV, 8, GW), F32),
            pltpu.VMEM((N_KV, ncmp, GW), F32),
            pltpu.VMEM((nsel, GW), F32),
        ],
        compiler_params=_params(("parallel", "arbitrary")),
        name="nsa",
    )(zq, qgain, kc, vct, kn, kn, vt, vt, ng)


def _merge_ffn_kernel(x_ref, ya_ref, o_ref_in, mg_ref, wn_ref, wo_ref, g2_ref, wg_ref, wu_ref, wd_ref,
                      out_ref, h_ref, *, tf):
    yb = _dot(o_ref_in[0], wn_ref[0:GW, :])
    for g in range(1, N_KV):
        yb = yb + _dot(o_ref_in[g], wn_ref[g * GW:(g + 1) * GW, :])
    tm = x_ref.shape[0]
    ya = jnp.concatenate([ya_ref[ck].reshape(tm, LANES) for ck in range(D_MODEL // LANES)], axis=1)
    mixed = (mg_ref[:, 0:D_MODEL].astype(F32) * ya
             + mg_ref[:, D_MODEL:2 * D_MODEL].astype(F32) * yb)
    x1 = x_ref[...] + _dot(mixed.astype(BF16), wo_ref[...])
    y = x1 * lax.rsqrt(jnp.mean(x1 * x1, axis=-1, keepdims=True) + EPS) * g2_ref[...]
    h_ref[...] = y.astype(BF16)
    out_ref[...] = x1
    for j in range(D_FF // tf):
        h = h_ref[...]
        sl = slice(j * tf, (j + 1) * tf)
        act = jax.nn.silu(_dot(h, wg_ref[:, sl])) * _dot(h, wu_ref[:, sl])
        out_ref[...] += _dot(act.astype(BF16), wd_ref[sl, :])


def _merge_ffn(x2d, ya, o, mg, wn, wo, gain2, wg, wu, wd, *, B, S, tm=512, tf=256):
    assert S % tm == 0 and D_FF % tf == 0
    ni = S // tm
    tok = lambda b, i: (b * ni + i, 0)
    once = lambda shape: pl.BlockSpec(shape, lambda b, i: (0,) * len(shape), pipeline_mode=pl.Buffered(1))
    return pl.pallas_call(
        functools.partial(_merge_ffn_kernel, tf=tf),
        grid=(B, ni),
        in_specs=[
            pl.BlockSpec((tm, D_MODEL), tok),
            pl.BlockSpec((D_MODEL // LANES, tm // 8, None, 8, LANES), lambda b, i: (0, i, b, 0, 0)),
            pl.BlockSpec((N_KV, tm, GW), lambda b, i: (0, b * ni + i, 0)),
            pl.BlockSpec((tm, 2 * D_MODEL), tok),
            once((N_HEADS * HEAD_DIM, D_MODEL)),
            once((D_MODEL, D_MODEL)),
            _const_spec((1, D_MODEL)),
            once((D_MODEL, D_FF)),
            once((D_MODEL, D_FF)),
            once((D_FF, D_MODEL)),
        ],
        out_specs=pl.BlockSpec((tm, D_MODEL), tok),
        out_shape=jax.ShapeDtypeStruct((B * S, D_MODEL), F32),
        scratch_shapes=[pltpu.VMEM((tm, D_MODEL), BF16)],
        compiler_params=_params(("parallel", "parallel")),
        name="merge_ffn",
    )(x2d, ya, o, mg, wn, wo, gain2, wg, wu, wd)


def _layer(x, p):
    B, S, _ = x.shape
    T = B * S
    dh = HEAD_DIM
    row = lambda v: v.reshape(1, -1)
    w_in = p["w_in"]
    n_ng = 3 * N_HEADS
    o_ng = 2 * D_RNN + N_HEADS * dh + 6 * KVW
    w_packed = jnp.concatenate(
        [w_in[:, :o_ng], jnp.pad(w_in[:, o_ng:o_ng + n_ng], ((0, 0), (0, NG_PAD - n_ng))), w_in[:, o_ng + n_ng:]],
        axis=1).astype(BF16)
    rx, gy, zq, zkv, ng, mg = _inproj(x, row(p["norm1"]), w_packed)

    slabs = lambda a: a.reshape(a.shape[0], S * B, LANES)
    ya = _rglru(slabs(rx), slabs(gy), p["conv_w"], row(p["conv_b"]),
                p["rg_wa"].astype(BF16), row(p["rg_ba"]), p["rg_wi"].astype(BF16), row(p["rg_bi"]),
                row(p["rg_lambda"]), p["w_rg_out"].astype(BF16), nb=B)
    ya = ya.reshape(D_MODEL // LANES, S // 8, B, 8, LANES)

    kgains = jnp.tile(p["k_norm"][1:3], (1, N_KV)).reshape(2, 1, KVW)
    kn, vt = _kvprep(zkv, kgains, B=B, S=S)
    pos = jnp.stack([p["cmp_pos_k"], p["cmp_pos_v"]]).reshape(2, CMP_STRIDE, 2 * dh)
    w1 = jnp.stack([p["cmp_k_w1"], p["cmp_v_w1"]]).astype(BF16)
    w2 = jnp.stack([p["cmp_k_w2"], p["cmp_v_w2"]]).astype(BF16)
    kc, vct = _compress(zkv, pos, w1, w2, row(p["k_norm"][0]), B=B, S=S)
    qgain = jnp.broadcast_to((p["q_norm"] * (dh ** -0.5 * LOG2E))[:, None], (dh, GW))
    o = _nsa(zq, qgain, kc, vct, kn, vt, ng, B=B, S=S)

    x2 = _merge_ffn(x.reshape(T, D_MODEL), ya, o, mg, p["w_nsa_out"].astype(BF16), p["w_o"].astype(BF16),
                    row(p["norm2"]), p["w_gate"].astype(BF16), p["w_up"].astype(BF16), p["w_down"].astype(BF16),
                    B=B, S=S)
    return x2.reshape(B, S, D_MODEL)


def kernel(x, norm1, w_in, conv_w, conv_b, rg_wa, rg_ba, rg_wi, rg_bi, rg_lambda, q_norm, k_norm, cmp_pos_k,
           cmp_pos_v, cmp_k_w1, cmp_k_w2, cmp_v_w1, cmp_v_w2, w_rg_out, w_nsa_out, w_o, norm2, w_gate, w_up, w_down):
    params = dict(norm1=norm1, w_in=w_in, conv_w=conv_w, conv_b=conv_b, rg_wa=rg_wa, rg_ba=rg_ba, rg_wi=rg_wi,
                  rg_bi=rg_bi, rg_lambda=rg_lambda, q_norm=q_norm, k_norm=k_norm, cmp_pos_k=cmp_pos_k,
                  cmp_pos_v=cmp_pos_v, cmp_k_w1=cmp_k_w1, cmp_k_w2=cmp_k_w2, cmp_v_w1=cmp_v_w1, cmp_v_w2=cmp_v_w2,
                  w_rg_out=w_rg_out, w_nsa_out=w_nsa_out, w_o=w_o, norm2=norm2, w_gate=w_gate, w_up=w_up,
                  w_down=w_down)
    for l in range(norm1.shape[0]):
        x = _layer(x, {k: v[l] for k, v in params.items()})
    return x
```
